```python
import jax, jax.numpy as jnp
from jax import lax
import numpy as np

D_MODEL = 1024
BATCH = 8
SEQ = 2048
DEPTH = 1

HEAD_DIM = 64
FOX_HEADS = 8
DIL_GROUPS = ((128, 1), (512, 4), (2048, 16))
DIL_HEADS_PER_GROUP = 4
N_DIL_GROUPS = len(DIL_GROUPS)
DIL_HEADS = N_DIL_GROUPS * DIL_HEADS_PER_GROUP
QK_HEADS = FOX_HEADS + DIL_HEADS
FOX_WIDTH = FOX_HEADS * HEAD_DIM
DIL_WIDTH = DIL_HEADS * HEAD_DIM
DIL_OUT_WIDTH = DIL_HEADS_PER_GROUP * HEAD_DIM
Q_BLOCK = 128
NUM_BUCKETS = 32
REL_MAX_DISTANCE = 2048
N_GROUPS = 4
EXPERTS_PER_GROUP = 8
N_EXPERTS = N_GROUPS * EXPERTS_PER_GROUP
TOP_K_IN_GROUP = 2
EXPERT_HIDDEN = D_MODEL // 2
MOE_BLOCK = 128
EPS = 1e-6

OFF_FOX_Q = 0
OFF_FOX_K = OFF_FOX_Q + FOX_WIDTH
OFF_FOX_V = OFF_FOX_K + FOX_WIDTH
OFF_FOX_F = OFF_FOX_V + FOX_WIDTH
OFF_DIL_Q = OFF_FOX_F + FOX_HEADS
OFF_DIL_K = OFF_DIL_Q + DIL_WIDTH
OFF_DIL_V = OFF_DIL_K + DIL_WIDTH
OFF_GATE_A = OFF_DIL_V + DIL_WIDTH
OFF_GATE_B = OFF_GATE_A + D_MODEL
N_IN = OFF_GATE_B + D_MODEL

kernel_name = 'hybrid_fox_dilated_hmoe_adaln'


def _rmsnorm(x, g):
    x32 = x.astype(jnp.float32)
    y = x32 * lax.rsqrt(jnp.mean(x32 * x32, axis=-1, keepdims=True) + EPS)
    return y.astype(x.dtype) * g


def _head_rms(t, g):
    t32 = t.astype(jnp.float32)
    y = t32 * lax.rsqrt(jnp.mean(t32 * t32, axis=-1, keepdims=True) + EPS)
    return y.astype(t.dtype) * g


def _t5_bucket(dist):
    max_exact = NUM_BUCKETS // 2
    d = np.maximum(dist, 1).astype(np.float32)
    large = max_exact + (np.log(d / max_exact) / np.log(REL_MAX_DISTANCE / max_exact)
                         * (NUM_BUCKETS - max_exact)).astype(np.int32)
    large = np.minimum(large, NUM_BUCKETS - 1)
    return np.where(dist < max_exact, dist, large).astype(np.int32)


def _forgetting_attention(q, k, v, log_f):
    B, S, H, E = q.shape
    scale = HEAD_DIM ** -0.5
    cum = jnp.cumsum(log_f, axis=1).transpose(0, 2, 1)
    outs = []
    for blk in range(S // Q_BLOCK):
        s0 = blk * Q_BLOCK
        s1 = s0 + Q_BLOCK
        logits = jnp.einsum('bqhe,bkhe->bhqk', q[:, s0:s1], k[:, :s1]).astype(jnp.float32) * scale
        logits = logits + cum[:, :, s0:s1, None] - cum[:, :, None, :s1]
        causal = np.arange(s0, s1)[:, None] >= np.arange(s1)[None, :]
        p = jax.nn.softmax(jnp.where(causal, logits, -jnp.inf), axis=-1)
        outs.append(jnp.einsum('bhqk,bkhe->bqhe', p.astype(v.dtype), v[:, :s1]))
    return jnp.concatenate(outs, axis=1)


def _dilated_group(q, k, v, table, window, dilation):
    B, S, H, E = q.shape
    L = window // dilation
    unit = L * dilation
    S_pad = -(-S // unit) * unit
    N = S_pad // dilation
    nb = N // L
    scale = HEAD_DIM ** -0.5

    def to_blocks(t):
        t = jnp.pad(t, ((0, 0), (0, S_pad - S), (0, 0), (0, 0)))
        t = t.reshape(B, N, dilation, H, E).transpose(0, 2, 1, 3, 4)
        return t.reshape(B, dilation, nb, L, H, E)

    def with_prev(t):
        prev = jnp.pad(t, ((0, 0), (0, 0), (1, 0), (0, 0), (0, 0), (0, 0)))[:, :, :-1]
        return jnp.concatenate([prev, t], axis=3)

    qb = to_blocks(q)
    kk = with_prev(to_blocks(k))
    vv = with_prev(to_blocks(v))
    i = np.arange(L)[:, None]
    j = np.arange(2 * L)[None, :]
    m = L + i - j
    band = (m >= 0) & (m <= L)
    first = band & (j >= L)
    valid = np.where(np.arange(nb)[:, None, None] == 0, first[None], band[None])
    bucket = _t5_bucket(np.clip(m, 0, None) * dilation)
    bias = table[bucket].transpose(2, 0, 1)
    logits = jnp.einsum('bdnqhe,bdnkhe->bdnhqk', qb, kk).astype(jnp.float32) * scale + bias
    logits = jnp.where(valid[None, None, :, None], logits, -jnp.inf)
    lse = jax.nn.logsumexp(logits, axis=-1)
    p = jnp.exp(logits - lse[..., None])
    o = jnp.einsum('bdnhqk,bdnkhe->bdnqhe', p.astype(v.dtype), vv)
    o = o.reshape(B, dilation, N, H, E).transpose(0, 2, 1, 3, 4).reshape(B, S_pad, H, E)[:, :S]
    lse = lse.transpose(0, 1, 2, 4, 3).reshape(B, dilation, N, H).transpose(0, 2, 1, 3).reshape(B, S_pad, H)[:, :S]
    return o, lse


def _mixer(h, rel_bias_table, w_in, b_forget, q_gain, k_gain, w_branch_a, w_branch_b, w_out):
    B, S, D = h.shape
    proj = h @ w_in
    qa = _head_rms(proj[..., OFF_FOX_Q:OFF_FOX_K].reshape(B, S, FOX_HEADS, HEAD_DIM), q_gain[:FOX_HEADS])
    ka = _head_rms(proj[..., OFF_FOX_K:OFF_FOX_V].reshape(B, S, FOX_HEADS, HEAD_DIM), k_gain[:FOX_HEADS])
    va = proj[..., OFF_FOX_V:OFF_FOX_F].reshape(B, S, FOX_HEADS, HEAD_DIM)
    log_f = jax.nn.log_sigmoid((proj[..., OFF_FOX_F:OFF_DIL_Q] + b_forget).astype(jnp.float32))
    ya = _forgetting_attention(qa, ka, va, log_f).reshape(B, S, FOX_WIDTH)
    qd = _head_rms(proj[..., OFF_DIL_Q:OFF_DIL_K].reshape(B, S, DIL_HEADS, HEAD_DIM), q_gain[FOX_HEADS:])
    kd = _head_rms(proj[..., OFF_DIL_K:OFF_DIL_V].reshape(B, S, DIL_HEADS, HEAD_DIM), k_gain[FOX_HEADS:])
    vd = proj[..., OFF_DIL_V:OFF_GATE_A].reshape(B, S, DIL_HEADS, HEAD_DIM)
    outs = []
    lses = []
    for g, (window, dilation) in enumerate(DIL_GROUPS):
        hs = slice(g * DIL_HEADS_PER_GROUP, (g + 1) * DIL_HEADS_PER_GROUP)
        o_g, lse_g = _dilated_group(qd[:, :, hs], kd[:, :, hs], vd[:, :, hs], rel_bias_table[:, hs], window, dilation)
        outs.append(o_g)
        lses.append(lse_g)
    wts = jax.nn.softmax(jnp.stack(lses, axis=0), axis=0)
    yd = jnp.sum(wts[..., None].astype(vd.dtype) * jnp.stack(outs, axis=0), axis=0).reshape(B, S, DIL_OUT_WIDTH)
    gate_a = jax.nn.sigmoid(proj[..., OFF_GATE_A:OFF_GATE_B])
    gate_b = jax.nn.sigmoid(proj[..., OFF_GATE_B:N_IN])
    merged = gate_a * (ya @ w_branch_a) + gate_b * (yd @ w_branch_b)
    return merged @ w_out


def _hier_moe(h, w_rg, b_rg, w_re, b_re, w1, w3, w2):
    B, S, D = h.shape
    T = B * S
    hf = h.reshape(T, D)
    g_logits = (hf @ w_rg + b_rg).astype(jnp.float32)
    g_prob = jax.nn.softmax(g_logits, axis=-1)
    g_idx = jnp.argmax(g_logits, axis=-1).astype(jnp.int32)
    g_w = jnp.take_along_axis(g_prob, g_idx[:, None], axis=1)
    e_logits = (hf @ w_re + b_re).astype(jnp.float32).reshape(T, N_GROUPS, EXPERTS_PER_GROUP)
    e_in = jnp.take_along_axis(e_logits, g_idx[:, None, None], axis=1)[:, 0]
    top_v, top_i = lax.top_k(e_in, TOP_K_IN_GROUP)
    weights = g_w * jax.nn.softmax(top_v, axis=-1)
    expert_ids = (g_idx[:, None] * EXPERTS_PER_GROUP + top_i).astype(jnp.int32)
    A = T * TOP_K_IN_GROUP
    flat_e = expert_ids.reshape(A)
    flat_tok = jnp.repeat(jnp.arange(T, dtype=jnp.int32), TOP_K_IN_GROUP)
    flat_w = weights.reshape(A)
    order = jnp.argsort(flat_e)
    sorted_e = flat_e[order]
    counts = jnp.zeros((N_EXPERTS,), jnp.int32).at[flat_e].add(1)
    starts = jnp.cumsum(counts) - counts
    padded = (counts + MOE_BLOCK - 1) // MOE_BLOCK * MOE_BLOCK
    pends = jnp.cumsum(padded)
    pstarts = pends - padded
    rank = jnp.arange(A, dtype=jnp.int32) - starts[sorted_e]
    dest = pstarts[sorted_e] + rank
    R = A + N_EXPERTS * MOE_BLOCK
    n_blk = R // MOE_BLOCK
    row_tok = jnp.full((R,), T, jnp.int32).at[dest].set(flat_tok[order])
    row_w = jnp.zeros((R,), jnp.float32).at[dest].set(flat_w[order])
    block_e = jnp.minimum(jnp.searchsorted(pends, jnp.arange(n_blk, dtype=jnp.int32) * MOE_BLOCK, side='right'),
                          N_EXPERTS - 1).astype(jnp.int32)
    xpad = jnp.concatenate([hf, jnp.zeros((1, D), hf.dtype)], axis=0)
    xs = xpad[row_tok].reshape(n_blk, MOE_BLOCK, D)

    def expert_block(args):
        xb, e = args
        return (jax.nn.silu(xb @ w1[e]) * (xb @ w3[e])) @ w2[e]

    y = lax.map(expert_block, (xs, block_e)).reshape(R, D)
    y = y * row_w[:, None].astype(y.dtype)
    out = jnp.zeros((T + 1, D), y.dtype).at[row_tok].add(y)[:T]
    return out.reshape(B, S, D)


def setup_inputs(seed: int = 0) -> dict:
    key = jax.random.key(seed)
    ks = jax.random.split(key, 21)

    def nrm(k, shape, scale):
        return jax.random.normal(k, shape, jnp.float32) * scale

    return {
        'x': nrm(ks[0], (BATCH, SEQ, D_MODEL), 1.0),
        'c': nrm(ks[1], (BATCH, D_MODEL), 1.0),
        'rel_bias_table': nrm(ks[2], (NUM_BUCKETS, DIL_HEADS), 0.5),
        'w_ada': nrm(ks[3], (DEPTH, D_MODEL, 6 * D_MODEL), D_MODEL ** -0.5),
        'b_ada': nrm(ks[4], (DEPTH, 6 * D_MODEL), 0.02),
        'norm1_g': 1.0 + nrm(ks[5], (DEPTH, D_MODEL), 0.02),
        'w_in': nrm(ks[6], (DEPTH, D_MODEL, N_IN), D_MODEL ** -0.5),
        'b_forget': 3.0 + nrm(ks[7], (DEPTH, FOX_HEADS), 0.1),
        'q_gain': 1.0 + nrm(ks[8], (DEPTH, QK_HEADS, HEAD_DIM), 0.02),
        'k_gain': 1.0 + nrm(ks[9], (DEPTH, QK_HEADS, HEAD_DIM), 0.02),
        'w_branch_a': nrm(ks[10], (DEPTH, FOX_WIDTH, D_MODEL), FOX_WIDTH ** -0.5),
        'w_branch_b': nrm(ks[11], (DEPTH, DIL_OUT_WIDTH, D_MODEL), DIL_OUT_WIDTH ** -0.5),
        'w_out': nrm(ks[12], (DEPTH, D_MODEL, D_MODEL), D_MODEL ** -0.5),
        'norm2_g': 1.0 + nrm(ks[13], (DEPTH, D_MODEL), 0.02),
        'w_router_group': nrm(ks[14], (DEPTH, D_MODEL, N_GROUPS), D_MODEL ** -0.5),
        'b_router_group': nrm(ks[15], (DEPTH, N_GROUPS), 0.01),
        'w_router_expert': nrm(ks[16], (DEPTH, D_MODEL, N_EXPERTS), D_MODEL ** -0.5),
        'b_router_expert': nrm(ks[17], (DEPTH, N_EXPERTS), 0.01),
        'w1': nrm(ks[18], (DEPTH, N_EXPERTS, D_MODEL, EXPERT_HIDDEN), D_MODEL ** -0.5),
        'w3': nrm(ks[19], (DEPTH, N_EXPERTS, D_MODEL, EXPERT_HIDDEN), D_MODEL ** -0.5),
        'w2': nrm(ks[20], (DEPTH, N_EXPERTS, EXPERT_HIDDEN, D_MODEL), EXPERT_HIDDEN ** -0.5),
    }


def reference(x, c, rel_bias_table, w_ada, b_ada, norm1_g, w_in, b_forget, q_gain, k_gain,
              w_branch_a, w_branch_b, w_out, norm2_g, w_router_group, b_router_group,
              w_router_expert, b_router_expert, w1, w3, w2):
    for l in range(DEPTH):
        mod = jax.nn.silu(c) @ w_ada[l] + b_ada[l]
        sh1, sc1, g1, sh2, sc2, g2 = jnp.split(mod, 6, axis=-1)
        h = _rmsnorm(x, norm1_g[l]) * (1 + sc1[:, None]) + sh1[:, None]
        x = x + g1[:, None] * _mixer(h, rel_bias_table, w_in[l], b_forget[l], q_gain[l], k_gain[l],
                                     w_branch_a[l], w_branch_b[l], w_out[l])
        h = _rmsnorm(x, norm2_g[l]) * (1 + sc2[:, None]) + sh2[:, None]
        x = x + g2[:, None] * _hier_moe(h, w_router_group[l], b_router_group[l], w_router_expert[l],
                                        b_router_expert[l], w1[l], w3[l], w2[l])
    return x
```

```python
import functools
import math

import numpy as np
import jax
import jax.numpy as jnp
from jax import lax
from jax.experimental import pallas as pl
from jax.experimental.pallas import tpu as pltpu

F32 = jnp.float32
BF16 = jnp.bfloat16

D_MODEL = 1024
HEAD_DIM = 64
FOX_HEADS = 8
DIL_GROUPS = ((128, 1), (512, 4), (2048, 16))
DIL_HEADS_PER_GROUP = 4
N_DIL_GROUPS = len(DIL_GROUPS)
DIL_HEADS = N_DIL_GROUPS * DIL_HEADS_PER_GROUP
FOX_WIDTH = FOX_HEADS * HEAD_DIM
DIL_WIDTH = DIL_HEADS * HEAD_DIM
DIL_OUT_WIDTH = DIL_HEADS_PER_GROUP * HEAD_DIM
NUM_BUCKETS = 32
REL_MAX_DISTANCE = 2048
N_GROUPS = 4
EXPERTS_PER_GROUP = 8
N_EXPERTS = N_GROUPS * EXPERTS_PER_GROUP
EXPERT_HIDDEN = D_MODEL // 2
EPS = 1e-6
LOG2E = math.log2(math.e)

OFF_FOX_Q = 0
OFF_FOX_K = OFF_FOX_Q + FOX_WIDTH
OFF_FOX_V = OFF_FOX_K + FOX_WIDTH
OFF_FOX_F = OFF_FOX_V + FOX_WIDTH
OFF_DIL_Q = OFF_FOX_F + FOX_HEADS
OFF_DIL_K = OFF_DIL_Q + DIL_WIDTH
OFF_DIL_V = OFF_DIL_K + DIL_WIDTH
OFF_GATE_A = OFF_DIL_V + DIL_WIDTH
OFF_GATE_B = OFF_GATE_A + D_MODEL
N_IN = OFF_GATE_B + D_MODEL

LANES = 128
UNIT = 256
DIL_L = 128

U_GATE_A, U_GATE_B, U_FOX_Q, U_FOX_K, U_FOX_V, U_DIL, U_FORGET = 0, 4, 8, 10, 12, 14, 23
N_UNITS = 24
P_WIDTH = U_FORGET * UNIT
_KIND = (["gate"] * 8 + ["normq"] * 2 + ["normk"] * 2 + ["plain"] * 2
         + ["normq", "normk", "plain"] * 3 + ["forget"])

TM_PROJ = 512
TQ_FOX = 256
MOE_ROWS = 256
VMEM_LIMIT = 56 * 1024 * 1024


def _dot(a, b):
    return jnp.dot(a, b, preferred_element_type=F32)


def _dot_nt(a, b):
    return lax.dot_general(a, b, (((1,), (1,)), ((), ())), preferred_element_type=F32)


def _split3(x):
    hi = x.astype(BF16)
    r1 = x - hi.astype(F32)
    mid = r1.astype(BF16)
    lo = (r1 - mid.astype(F32)).astype(BF16)
    return hi, mid, lo


def _ada_kernel(c_ref, w_ref, b_ref, o_ref):
    c = c_ref[...]
    s = c * jax.nn.sigmoid(c)
    s_hi = s.astype(BF16)
    s_lo = (s - s_hi.astype(F32)).astype(BF16)
    w = w_ref[...]
    w_hi = w.astype(BF16)
    w_lo = (w - w_hi.astype(F32)).astype(BF16)
    acc = _dot(s_hi, w_hi) + _dot(s_hi, w_lo) + _dot(s_lo, w_hi)
    o_ref[...] = acc + b_ref[...]


def _ada(c, w_ada, b_ada):
    B = c.shape[0]
    n_out = w_ada.shape[1]
    tn = 512
    return pl.pallas_call(
        _ada_kernel,
        grid=(n_out // tn,),
        in_specs=[pl.BlockSpec((B, D_MODEL), lambda j: (0, 0)),
                  pl.BlockSpec((D_MODEL, tn), lambda j: (0, j)),
                  pl.BlockSpec((1, tn), lambda j: (0, j))],
        out_specs=pl.BlockSpec((B, tn), lambda j: (0, j)),
        out_shape=jax.ShapeDtypeStruct((B, n_out), F32),
        name="ada_mod",
    )(c, w_ada, b_ada.reshape(1, n_out))


def _inproj_kernel(x_ref, g_ref, sc_ref, sh_ref, w_ref, gain_ref, bd_ref, p_ref, f_ref):
    x = x_ref[...]
    ms = jnp.mean(x * x, axis=-1, keepdims=True)
    h = x * lax.rsqrt(ms + EPS) * g_ref[...]
    h = h * (1.0 + sc_ref[0]) + sh_ref[0]
    hb = h.astype(BF16)
    for u in range(N_UNITS):
        cols = slice(u * UNIT, (u + 1) * UNIT)
        acc = _dot(hb, w_ref[:, cols])
        kind = _KIND[u]
        if kind == "forget":
            f_ref[...] = acc[:, :LANES]
            continue
        if kind == "gate":
            out = jax.nn.sigmoid(acc)
        elif kind in ("normq", "normk"):
            ss = _dot((acc * acc).astype(BF16), bd_ref[...])
            out = acc * lax.rsqrt(ss * (1.0 / HEAD_DIM) + EPS) * gain_ref[:, cols]
        else:
            out = acc
        p_ref[:, cols] = out.astype(BF16)


def _inproj(x2d, norm_g, sc, sh, w_re, gain_row, S):
    T = x2d.shape[0]
    tm = TM_PROJ
    per_b = S // tm
    bd = np.kron(np.eye(UNIT // HEAD_DIM), np.ones((HEAD_DIM, HEAD_DIM))).astype(np.float32)
    return pl.pallas_call(
        _inproj_kernel,
        grid=(T // tm,),
        in_specs=[pl.BlockSpec((tm, D_MODEL), lambda i: (i, 0)),
                  pl.BlockSpec((1, D_MODEL), lambda i: (0, 0)),
                  pl.BlockSpec((1, 1, D_MODEL), lambda i: (i // per_b, 0, 0)),
                  pl.BlockSpec((1, 1, D_MODEL), lambda i: (i // per_b, 0, 0)),
                  pl.BlockSpec((D_MODEL, N_UNITS * UNIT), lambda i: (0, 0)),
                  pl.BlockSpec((1, N_UNITS * UNIT), lambda i: (0, 0)),
                  pl.BlockSpec((UNIT, UNIT), lambda i: (0, 0))],
        out_specs=[pl.BlockSpec((tm, P_WIDTH), lambda i: (i, 0)),
                   pl.BlockSpec((tm, LANES), lambda i: (i, 0))],
        out_shape=[jax.ShapeDtypeStruct((T, P_WIDTH), BF16),
                   jax.ShapeDtypeStruct((T, LANES), F32)],
        compiler_params=pltpu.CompilerParams(vmem_limit_bytes=VMEM_LIMIT),
        name="in_proj",
    )(x2d, norm_g, sc, sh, w_re, gain_row, jnp.asarray(bd, BF16))


def _fcum_kernel(f_ref, b_ref, tri_ref, o_ref):
    S = f_ref.shape[1]
    xf = f_ref[0] + b_ref[...]
    ls = (jnp.minimum(xf, 0.0) - jnp.log(1.0 + jnp.exp(-jnp.abs(xf)))) * LOG2E
    lst = ls.T
    carry = jnp.zeros((LANES, UNIT), F32)
    for blk in range(S // UNIT):
        seg = lst[:, blk * UNIT:(blk + 1) * UNIT]
        hi, mid, lo = _split3(seg)
        tri = tri_ref[...]
        res = _dot(hi, tri) + _dot(mid, tri) + _dot(lo, tri)
        o_ref[0, :, blk * UNIT:(blk + 1) * UNIT] = (res[:, :UNIT] + carry)[:FOX_HEADS]
        carry = carry + res[:, UNIT:]


def _fcum(fgt, b_forget):
    B, S, _ = fgt.shape
    brow = jnp.zeros((1, LANES), F32).at[0, :FOX_HEADS].set(b_forget)
    tri = np.concatenate([np.triu(np.ones((UNIT, UNIT))), np.ones((UNIT, UNIT))], axis=1)
    return pl.pallas_call(
        _fcum_kernel,
        grid=(B,),
        in_specs=[pl.BlockSpec((1, S, LANES), lambda b: (b, 0, 0)),
                  pl.BlockSpec((1, LANES), lambda b: (0, 0)),
                  pl.BlockSpec((UNIT, 2 * UNIT), lambda b: (0, 0))],
        out_specs=pl.BlockSpec((1, FOX_HEADS, S), lambda b: (b, 0, 0)),
        out_shape=jax.ShapeDtypeStruct((B, FOX_HEADS, S), F32),
        name="forget_cumsum",
    )(fgt, brow, jnp.asarray(tri, BF16))


def _fox_kernel(q_ref, k_ref, v_ref, ck_ref, o_ref):
    S = q_ref.shape[1]
    pair = pl.program_id(1)
    tq = TQ_FOX
    lane = lax.broadcasted_iota(jnp.int32, (1, LANES), 1)
    row = lax.broadcasted_iota(jnp.int32, (tq, tq), 0)
    col = lax.broadcasted_iota(jnp.int32, (tq, tq), 1)
    causal = col <= row
    cks = [ck_ref[0, pl.ds(2 * pair + hh, 1), :] for hh in range(2)]
    for t in range(S // tq):
        r0, r1 = t * tq, (t + 1) * tq
        qt = q_ref[0, r0:r1, :]
        outs = []
        for hh in range(2):
            hsel = (lane >= HEAD_DIM) == bool(hh)
            qm = jnp.where(hsel, qt, jnp.zeros_like(qt))
            ck = cks[hh]
            s_d = _dot_nt(qm, k_ref[0, r0:r1, :]) - ck[:, r0:r1]
            s_d = jnp.where(causal, s_d, -jnp.inf)
            m = jnp.max(s_d, axis=-1, keepdims=True)
            if t > 0:
                s_o = _dot_nt(qm, k_ref[0, :r0, :]) - ck[:, :r0]
                m = jnp.maximum(m, jnp.max(s_o, axis=-1, keepdims=True))
            p_d = jnp.exp2(s_d - m)
            l = jnp.sum(p_d, axis=-1, keepdims=True)
            acc = _dot(p_d.astype(BF16), v_ref[0, r0:r1, :])
            if t > 0:
                p_o = jnp.exp2(s_o - m)
                l = l + jnp.sum(p_o, axis=-1, keepdims=True)
                acc = acc + _dot(p_o.astype(BF16), v_ref[0, :r0, :])
            outs.append(acc / l)
        o_ref[0, r0:r1, :] = jnp.where(lane < HEAD_DIM, outs[0], outs[1]).astype(BF16)


def _fox(p3, ck):
    B, S, _ = p3.shape
    nq, nk, nv = (U_FOX_Q * UNIT // LANES, U_FOX_K * UNIT // LANES, U_FOX_V * UNIT // LANES)
    return pl.pallas_call(
        _fox_kernel,
        grid=(B, FOX_HEADS // 2),
        in_specs=[pl.BlockSpec((1, S, LANES), lambda b, p: (b, 0, nq + p)),
                  pl.BlockSpec((1, S, LANES), lambda b, p: (b, 0, nk + p)),
                  pl.BlockSpec((1, S, LANES), lambda b, p: (b, 0, nv + p)),
                  pl.BlockSpec((1, FOX_HEADS, S), lambda b, p: (b, 0, 0))],
        out_specs=pl.BlockSpec((1, S, LANES), lambda b, p: (b, 0, p)),
        out_shape=jax.ShapeDtypeStruct((B, S, FOX_WIDTH), BF16),
        compiler_params=pltpu.CompilerParams(vmem_limit_bytes=VMEM_LIMIT),
        name="fox_attn",
    )(p3, p3, p3, ck)


def _t5_bucket(dist):
    max_exact = NUM_BUCKETS // 2
    d = np.maximum(dist, 1).astype(np.float32)
    large = max_exact + (np.log(d / max_exact) / np.log(REL_MAX_DISTANCE / max_exact)
                         * (NUM_BUCKETS - max_exact)).astype(np.int32)
    large = np.minimum(large, NUM_BUCKETS - 1)
    return np.where(dist < max_exact, dist, large).astype(np.int32)


def _relbias_kernel(tab_ref, bucket_ref, valid_ref, o_ref):
    g = pl.program_id(0)
    bk = bucket_ref[0]
    vd = valid_ref[0]
    for hs in range(DIL_HEADS_PER_GROUP):
        acc = jnp.zeros(bk.shape, F32)
        for b in range(NUM_BUCKETS):
            acc = jnp.where(bk == b, tab_ref[b, g * DIL_HEADS_PER_GROUP + hs], acc)
        o_ref[hs] = jnp.where(vd != 0, acc * LOG2E, -jnp.inf)


def _relbias(table):
    L = DIL_L
    i = np.arange(L)[:, None]
    j = np.arange(2 * L)[None, :]
    m = L + i - j
    valid = ((m >= 0) & (m <= L)).astype(np.int32)
    buckets = np.stack([_t5_bucket(np.clip(m, 0, None) * d) for _, d in DIL_GROUPS])
    valids = np.stack([valid] * N_DIL_GROUPS)
    return pl.pallas_call(
        _relbias_kernel,
        grid=(N_DIL_GROUPS,),
        in_specs=[pl.BlockSpec(memory_space=pltpu.SMEM),
                  pl.BlockSpec((1, L, 2 * L), lambda g: (g, 0, 0)),
                  pl.BlockSpec((1, L, 2 * L), lambda g: (g, 0, 0))],
        out_specs=pl.BlockSpec((DIL_HEADS_PER_GROUP, L, 2 * L), lambda g: (g, 0, 0)),
        out_shape=jax.ShapeDtypeStruct((DIL_HEADS, L, 2 * L), F32),
        name="rel_bias",
    )(table, jnp.asarray(buckets), jnp.asarray(valids))


def _dil_block(q_ref, k_ref, v_ref, bias_ref, o_ref, lse_ref, q0, first):
    L = DIL_L
    lane = lax.broadcasted_iota(jnp.int32, (1, LANES), 1)
    k0 = q0 if first else q0 - L
    nk = L if first else 2 * L
    for pr in range(2):
        lanes = slice(pr * LANES, (pr + 1) * LANES)
        qt = q_ref[0, pl.ds(q0, L), lanes]
        kt = k_ref[0, pl.ds(k0, nk), lanes]
        vt = v_ref[0, pl.ds(k0, nk), lanes]
        outs, lses = [], []
        for hh in range(2):
            hsel = (lane >= HEAD_DIM) == bool(hh)
            qm = jnp.where(hsel, qt, jnp.zeros_like(qt))
            bias = bias_ref[2 * pr + hh]
            if first:
                bias = bias[:, L:]
            s = _dot_nt(qm, kt) + bias
            m = jnp.max(s, axis=-1, keepdims=True)
            p = jnp.exp2(s - m)
            l = jnp.sum(p, axis=-1, keepdims=True)
            acc = _dot(p.astype(BF16), vt)
            outs.append(acc / l)
            lses.append(m + jnp.log2(l))
        o_ref[0, pl.ds(q0, L), lanes] = jnp.where(lane < HEAD_DIM, outs[0], outs[1]).astype(BF16)
        lse_ref[0, pl.ds(q0, L), lanes] = jnp.where(lane < HEAD_DIM, lses[0], lses[1])


def _dil_kernel(q_ref, k_ref, v_ref, bias_ref, o_ref, lse_ref, *, nb):
    _dil_block(q_ref, k_ref, v_ref, bias_ref, o_ref, lse_ref, 0, True)
    if nb > 1:
        def body(n, carry):
            q0 = pl.multiple_of(n * DIL_L, DIL_L)
            _dil_block(q_ref, k_ref, v_ref, bias_ref, o_ref, lse_ref, q0, False)
            return carry
        lax.fori_loop(1, nb, body, 0)


def _dil(p3, bias, g):
    B, S, _ = p3.shape
    window, d = DIL_GROUPS[g]
    L = window // d
    assert L == DIL_L and S % window == 0
    N = S // d
    nb = N // L
    pv = p3.reshape(B, N, d * P_WIDTH)
    upr = P_WIDTH // UNIT
    u0 = U_DIL + 3 * g
    out_w = DIL_OUT_WIDTH
    o, lse = pl.pallas_call(
        functools.partial(_dil_kernel, nb=nb),
        grid=(B, d),
        in_specs=[pl.BlockSpec((1, N, UNIT), lambda b, r: (b, 0, r * upr + u0)),
                  pl.BlockSpec((1, N, UNIT), lambda b, r: (b, 0, r * upr + u0 + 1)),
                  pl.BlockSpec((1, N, UNIT), lambda b, r: (b, 0, r * upr + u0 + 2)),
                  pl.BlockSpec((DIL_HEADS_PER_GROUP, L, 2 * L), lambda b, r: (g, 0, 0))],
        out_specs=[pl.BlockSpec((1, N, out_w), lambda b, r: (b, 0, r)),
                   pl.BlockSpec((1, N, out_w), lambda b, r: (b, 0, r))],
        out_shape=[jax.ShapeDtypeStruct((B, N, d * out_w), BF16),
                   jax.ShapeDtypeStruct((B, N, d * out_w), F32)],
        name=f"dil_attn_{g}",
    )(pv, pv, pv, bias)
    return o.reshape(B * S, out_w), lse.reshape(B * S, out_w)


def _outproj_kernel(x_ref, ya_ref, o0_ref, o1_ref, o2_ref, l0_ref, l1_ref, l2_ref, ga_ref, gb_ref,
                    g1_ref, sc_ref, sh_ref, ng_ref, wa_ref, wb_ref, wo_ref, wr_ref, br_ref,
                    x1_ref, h2_ref, lg_ref):
    l0, l1, l2 = l0_ref[...], l1_ref[...], l2_ref[...]
    m = jnp.maximum(jnp.maximum(l0, l1), l2)
    e0, e1, e2 = jnp.exp2(l0 - m), jnp.exp2(l1 - m), jnp.exp2(l2 - m)
    num = (e0 * o0_ref[...].astype(F32) + e1 * o1_ref[...].astype(F32)
           + e2 * o2_ref[...].astype(F32))
    yd = (num / (e0 + e1 + e2)).astype(BF16)
    a = _dot(ya_ref[...], wa_ref[...])
    bm = _dot(yd, wb_ref[...])
    merged = ga_ref[...].astype(F32) * a + gb_ref[...].astype(F32) * bm
    out = _dot(merged.astype(BF16), wo_ref[...])
    x1 = x_ref[...] + g1_ref[0] * out
    x1_ref[...] = x1
    ms = jnp.mean(x1 * x1, axis=-1, keepdims=True)
    h = x1 * lax.rsqrt(ms + EPS) * ng_ref[...]
    h = h * (1.0 + sc_ref[0]) + sh_ref[0]
    h2_ref[...] = h
    lg_ref[...] = _dot(h.astype(BF16), wr_ref[...]) + br_ref[...]


def _outproj(x2d, ya2d, os_, lses, p2d, g1, sc2, sh2, norm_g, wa, wb, wo, wr, br, S):
    T = x2d.shape[0]
    tm = TM_PROJ
    per_b = S // tm
    row = lambda w: pl.BlockSpec((tm, w), lambda i: (i, 0))
    full = lambda a: pl.BlockSpec(a.shape, lambda i: (0,) * a.ndim)
    mod = pl.BlockSpec((1, 1, D_MODEL), lambda i: (i // per_b, 0, 0))
    return pl.pallas_call(
        _outproj_kernel,
        grid=(T // tm,),
        in_specs=[row(D_MODEL), row(FOX_WIDTH)] + [row(DIL_OUT_WIDTH)] * 6
                 + [pl.BlockSpec((tm, D_MODEL), lambda i: (i, U_GATE_A * UNIT // D_MODEL)),
                    pl.BlockSpec((tm, D_MODEL), lambda i: (i, U_GATE_B * UNIT // D_MODEL)),
                    mod, mod, mod, full(norm_g), full(wa), full(wb), full(wo), full(wr), full(br)],
        out_specs=[row(D_MODEL), row(D_MODEL), row(LANES)],
        out_shape=[jax.ShapeDtypeStruct((T, D_MODEL), F32),
                   jax.ShapeDtypeStruct((T, D_MODEL), F32),
                   jax.ShapeDtypeStruct((T, LANES), F32)],
        compiler_params=pltpu.CompilerParams(vmem_limit_bytes=VMEM_LIMIT),
        name="out_proj",
    )(x2d, ya2d, *os_, *lses, p2d, p2d, g1, sc2, sh2, norm_g, wa, wb, wo, wr, br)


def _route_kernel(lg_ref, o_ref):
    lg = lg_ref[...]
    lane = lax.broadcasted_iota(jnp.int32, lg.shape, 1)
    neg = -jnp.inf
    big = jnp.int32(LANES)
    gl = jnp.where(lane < N_GROUPS, lg, neg)
    gmax = jnp.max(gl, axis=-1, keepdims=True)
    gidx = jnp.min(jnp.where(gl == gmax, lane, big), axis=-1, keepdims=True)
    gsum = jnp.sum(jnp.where(lane < N_GROUPS, jnp.exp(lg - gmax), 0.0), axis=-1, keepdims=True)
    lo = N_GROUPS + EXPERTS_PER_GROUP * gidx
    el = jnp.where((lane >= lo) & (lane < lo + EXPERTS_PER_GROUP), lg, neg)
    v1 = jnp.max(el, axis=-1, keepdims=True)
    i1 = jnp.min(jnp.where(el == v1, lane, big), axis=-1, keepdims=True)
    el2 = jnp.where(lane == i1, neg, el)
    v2 = jnp.max(el2, axis=-1, keepdims=True)
    i2 = jnp.min(jnp.where(el2 == v2, lane, big), axis=-1, keepdims=True)
    t = jnp.exp(v2 - v1)
    w1 = 1.0 / ((1.0 + t) * gsum)
    w2 = t / ((1.0 + t) * gsum)
    out = jnp.where(lane == 0, (i1 - N_GROUPS).astype(F32), 0.0)
    out = jnp.where(lane == 1, (i2 - N_GROUPS).astype(F32), out)
    out = jnp.where(lane == 2, w1, out)
    out = jnp.where(lane == 3, w2, out)
    o_ref[...] = out


def _route(logits):
    T = logits.shape[0]
    tm = 1024
    return pl.pallas_call(
        _route_kernel,
        grid=(T // tm,),
        in_specs=[pl.BlockSpec((tm, LANES), lambda i: (i, 0))],
        out_specs=pl.BlockSpec((tm, LANES), lambda i: (i, 0)),
        out_shape=jax.ShapeDtypeStruct((T, LANES), F32),
        name="route_topk",
    )(logits)


def _moe_kernel(be_ref, nv_ref, src0_ref, srcn_ref, dst_ref, rw_ref, w1_ref, w3_ref, w2_ref, h2_hbm,
                yk_hbm, xbuf, ybuf, wb1, wb3, wb2, gsem, ssem):
    i = pl.program_id(0)
    last = pl.num_programs(0) - 1
    slot = i % 2
    rows = MOE_ROWS
    nv = nv_ref[i]

    def gather(src_ref, s):
        def body(r, c):
            t = src_ref[0, 0, r]
            pltpu.make_async_copy(h2_hbm.at[pl.ds(t, 1)], xbuf.at[s, pl.ds(r, 1)], gsem.at[s]).start()
            return c
        lax.fori_loop(0, rows, body, 0, unroll=8)

    def wait_gather(s):
        pltpu.make_async_copy(h2_hbm.at[pl.ds(0, rows)], xbuf.at[s], gsem.at[s]).wait()

    def scatter_row(s, r, d):
        return pltpu.make_async_copy(ybuf.at[s, pl.ds(r, 1)], yk_hbm.at[pl.ds(d, 1)], ssem.at[s])

    def wait_scatter(s, count):
        def body(r, c):
            scatter_row(s, 0, 0).wait()
            return c
        lax.fori_loop(0, count, body, 0)

    @pl.when((i == 0) & (nv > 0))
    def _():
        gather(src0_ref, 0)

    @pl.when((i < last) & (nv_ref[jnp.minimum(i + 1, last)] > 0))
    def _():
        gather(srcn_ref, 1 - slot)

    @pl.when(i >= 2)
    def _():
        wait_scatter(slot, nv_ref[jnp.maximum(i - 2, 0)])

    @pl.when(nv > 0)
    def _():
        e = be_ref[i]
        e_prev = be_ref[jnp.maximum(i - 1, 0)]

        @pl.when((i == 0) | (e != e_prev))
        def _():
            wb1[...] = w1_ref[0].astype(BF16)
            wb3[...] = w3_ref[0].astype(BF16)
            wb2[...] = w2_ref[0].astype(BF16)

        wait_gather(slot)
        x = xbuf[slot].astype(BF16)
        a = _dot(x, wb1[...])
        b = _dot(x, wb3[...])
        hmid = (a * jax.nn.sigmoid(a) * b).astype(BF16)
        ybuf[slot] = _dot(hmid, wb2[...]) * rw_ref[...]

        def body(r, c):
            scatter_row(slot, r, dst_ref[0, 0, r]).start()
            return c
        lax.fori_loop(0, nv, body, 0)

    @pl.when(i == last)
    def _():
        @pl.when(last >= 1)
        def _():
            wait_scatter(1 - slot, nv_ref[jnp.maximum(last - 1, 0)])
        wait_scatter(slot, nv)


def _moe(h2, block_e, n_valid, src, dst, row_w, w1, w3, w2, n_rows_out):
    n_blk = block_e.shape[0]
    rows = MOE_ROWS
    last = n_blk - 1
    idx_spec = lambda f: pl.BlockSpec((1, 1, rows), f, memory_space=pltpu.SMEM)
    grid_spec = pltpu.PrefetchScalarGridSpec(
        num_scalar_prefetch=2,
        grid=(n_blk,),
        in_specs=[idx_spec(lambda i, be, nu: (0, 0, 0)),
                  idx_spec(lambda i, be, nu: (jnp.minimum(i + 1, last), 0, 0)),
                  idx_spec(lambda i, be, nu: (i, 0, 0)),
                  pl.BlockSpec((rows, 1), lambda i, be, nu: (i, 0)),
                  pl.BlockSpec((1, D_MODEL, EXPERT_HIDDEN), lambda i, be, nu: (be[i], 0, 0)),
                  pl.BlockSpec((1, D_MODEL, EXPERT_HIDDEN), lambda i, be, nu: (be[i], 0, 0)),
                  pl.BlockSpec((1, EXPERT_HIDDEN, D_MODEL), lambda i, be, nu: (be[i], 0, 0)),
                  pl.BlockSpec(memory_space=pl.ANY)],
        out_specs=pl.BlockSpec(memory_space=pl.ANY),
        scratch_shapes=[pltpu.VMEM((2, rows, D_MODEL), F32),
                        pltpu.VMEM((2, rows, D_MODEL), F32),
                        pltpu.VMEM((D_MODEL, EXPERT_HIDDEN), BF16),
                        pltpu.VMEM((D_MODEL, EXPERT_HIDDEN), BF16),
                        pltpu.VMEM((EXPERT_HIDDEN, D_MODEL), BF16),
                        pltpu.SemaphoreType.DMA((2,)),
                        pltpu.SemaphoreType.DMA((2,))])
    return pl.pallas_call(
        _moe_kernel,
        grid_spec=grid_spec,
        out_shape=jax.ShapeDtypeStruct((n_rows_out, D_MODEL), F32),
        compiler_params=pltpu.CompilerParams(dimension_semantics=("arbitrary",),
                                             vmem_limit_bytes=VMEM_LIMIT),
        name="moe_ffn",
    )(block_e, n_valid, src, src, dst, row_w, w1, w3, w2, h2)


def _final_kernel(x1_ref, yk_ref, g2_ref, o_ref):
    yk = yk_ref[...]
    o_ref[...] = x1_ref[...] + g2_ref[0] * (yk[:, :D_MODEL] + yk[:, D_MODEL:])


def _final(x1, yk2, g2, S):
    T = x1.shape[0]
    tm = TM_PROJ
    per_b = S // tm
    return pl.pallas_call(
        _final_kernel,
        grid=(T // tm,),
        in_specs=[pl.BlockSpec((tm, D_MODEL), lambda i: (i, 0)),
                  pl.BlockSpec((tm, 2 * D_MODEL), lambda i: (i, 0)),
                  pl.BlockSpec((1, 1, D_MODEL), lambda i: (i // per_b, 0, 0))],
        out_specs=pl.BlockSpec((tm, D_MODEL), lambda i: (i, 0)),
        out_shape=jax.ShapeDtypeStruct((T, D_MODEL), F32),
        name="final_residual",
    )(x1, yk2, g2)


def _prep_w_in(w_in):
    dq = w_in[:, OFF_DIL_Q:OFF_DIL_K]
    dk = w_in[:, OFF_DIL_K:OFF_DIL_V]
    dv = w_in[:, OFF_DIL_V:OFF_GATE_A]
    dil = []
    for g in range(N_DIL_GROUPS):
        cs = slice(g * DIL_OUT_WIDTH, (g + 1) * DIL_OUT_WIDTH)
        dil += [dq[:, cs], dk[:, cs], dv[:, cs]]
    pad = jnp.zeros((D_MODEL, UNIT - FOX_HEADS), w_in.dtype)
    cols = [w_in[:, OFF_GATE_A:OFF_GATE_B], w_in[:, OFF_GATE_B:N_IN],
            w_in[:, OFF_FOX_Q:OFF_FOX_K], w_in[:, OFF_FOX_K:OFF_FOX_V], w_in[:, OFF_FOX_V:OFF_FOX_F],
            *dil, w_in[:, OFF_FOX_F:OFF_DIL_Q], pad]
    return jnp.concatenate(cols, axis=1).astype(BF16)


def _prep_gain(q_gain, k_gain):
    qs = HEAD_DIM ** -0.5 * LOG2E
    ones = jnp.ones((UNIT,), F32)
    fq = q_gain[:FOX_HEADS].reshape(-1) * qs
    fk = k_gain[:FOX_HEADS].reshape(-1)
    dq = q_gain[FOX_HEADS:].reshape(-1) * qs
    dk = k_gain[FOX_HEADS:].reshape(-1)
    dil = []
    for g in range(N_DIL_GROUPS):
        cs = slice(g * DIL_OUT_WIDTH, (g + 1) * DIL_OUT_WIDTH)
        dil += [dq[cs], dk[cs], ones]
    parts = [ones] * 8 + [fq, fk, ones, ones] + dil + [ones]
    return jnp.concatenate(parts).reshape(1, N_UNITS * UNIT)


def _dispatch_plan(routing, T):
    rows = MOE_ROWS
    A = 2 * T
    flat_e = routing[:, 0:2].astype(jnp.int32).reshape(A)
    flat_w = routing[:, 2:4].reshape(A)
    counts = jnp.zeros((N_EXPERTS,), jnp.int32).at[flat_e].add(1)
    starts = jnp.cumsum(counts) - counts
    padded = (counts + rows - 1) // rows * rows
    pends = jnp.cumsum(padded)
    pstarts = pends - padded
    order = jnp.argsort(flat_e).astype(jnp.int32)
    sorted_e = flat_e[order]
    dest = pstarts[sorted_e] + jnp.arange(A, dtype=jnp.int32) - starts[sorted_e]
    R = A + N_EXPERTS * rows
    n_blk = R // rows
    src = jnp.zeros((R,), jnp.int32).at[dest].set(order // 2)
    dst = jnp.zeros((R,), jnp.int32).at[dest].set(order)
    row_w = jnp.zeros((R,), F32).at[dest].set(flat_w[order])
    blk0 = jnp.arange(n_blk, dtype=jnp.int32) * rows
    block_e = jnp.minimum(jnp.searchsorted(pends, blk0, side="right"),
                          N_EXPERTS - 1).astype(jnp.int32)
    n_valid = jnp.clip(pstarts[block_e] + counts[block_e] - blk0, 0, rows)
    n_valid = jnp.where(blk0 < pends[-1], n_valid, 0).astype(jnp.int32)
    return (block_e, n_valid, src.reshape(n_blk, 1, rows), dst.reshape(n_blk, 1, rows),
            row_w.reshape(R, 1))


def _layer(x, mod, rel_bias_table, norm1_g, w_in, b_forget, q_gain, k_gain, w_branch_a, w_branch_b,
           w_out, norm2_g, w_rg, b_rg, w_re, b_re, w1, w3, w2):
    B, S, D = x.shape
    T = B * S
    sh1, sc1, g1, sh2, sc2, g2 = [m.reshape(B, 1, D) for m in jnp.split(mod, 6, axis=-1)]
    x2d = x.reshape(T, D)

    p2d, fgt = _inproj(x2d, norm1_g.reshape(1, D), sc1, sh1, _prep_w_in(w_in),
                       _prep_gain(q_gain, k_gain), S)
    p3 = p2d.reshape(B, S, P_WIDTH)
    ck = _fcum(fgt.reshape(B, S, LANES), b_forget)
    ya = _fox(p3, ck)
    bias = _relbias(rel_bias_table)
    dil = [_dil(p3, bias, g) for g in range(N_DIL_GROUPS)]

    n_router = N_GROUPS + N_EXPERTS
    wr = jnp.concatenate([w_rg, w_re, jnp.zeros((D, LANES - n_router), F32)], axis=1).astype(BF16)
    br = jnp.concatenate([b_rg, b_re, jnp.zeros((LANES - n_router,), F32)]).reshape(1, LANES)
    x1, h2, logits = _outproj(x2d, ya.reshape(T, FOX_WIDTH), [o for o, _ in dil],
                              [l for _, l in dil], p2d, g1, sc2, sh2, norm2_g.reshape(1, D),
                              w_branch_a.astype(BF16), w_branch_b.astype(BF16), w_out.astype(BF16),
                              wr, br, S)
    routing = _route(logits)
    block_e, n_valid, src, dst, row_w = _dispatch_plan(routing, T)
    yk = _moe(h2, block_e, n_valid, src, dst, row_w, w1, w3, w2, 2 * T)
    out = _final(x1, yk.reshape(T, 2 * D), g2, S)
    return out.reshape(B, S, D)


def kernel(x, c, rel_bias_table, w_ada, b_ada, norm1_g, w_in, b_forget, q_gain, k_gain, w_branch_a, w_branch_b, w_out, norm2_g, w_router_group, b_router_group, w_router_expert, b_router_expert, w1, w3, w2):
    depth = w_ada.shape[0]
    for l in range(depth):
        mod = _ada(c, w_ada[l], b_ada[l])
        x = _layer(x, mod, rel_bias_table, norm1_g[l], w_in[l], b_forget[l], q_gain[l], k_gain[l],
                   w_branch_a[l], w_branch_b[l], w_out[l], norm2_g[l], w_router_group[l],
                   b_router_group[l], w_router_expert[l], b_router_expert[l], w1[l], w3[l], w2[l])
    return x
```

```python
import functools
import math

import numpy as np
import jax
import jax.numpy as jnp
from jax import lax
from jax.experimental import pallas as pl
from jax.experimental.pallas import tpu as pltpu

F32 = jnp.float32
BF16 = jnp.bfloat16

D_MODEL = 1024
HEAD_DIM = 64
FOX_HEADS = 8
DIL_GROUPS = ((128, 1), (512, 4), (2048, 16))
DIL_HEADS_PER_GROUP = 4
N_DIL_GROUPS = len(DIL_GROUPS)
DIL_HEADS = N_DIL_GROUPS * DIL_HEADS_PER_GROUP
FOX_WIDTH = FOX_HEADS * HEAD_DIM
DIL_WIDTH = DIL_HEADS * HEAD_DIM
DIL_OUT_WIDTH = DIL_HEADS_PER_GROUP * HEAD_DIM
NUM_BUCKETS = 32
REL_MAX_DISTANCE = 2048
N_GROUPS = 4
EXPERTS_PER_GROUP = 8
N_EXPERTS = N_GROUPS * EXPERTS_PER_GROUP
EXPERT_HIDDEN = D_MODEL // 2
EPS = 1e-6
LOG2E = math.log2(math.e)

OFF_FOX_Q = 0
OFF_FOX_K = OFF_FOX_Q + FOX_WIDTH
OFF_FOX_V = OFF_FOX_K + FOX_WIDTH
OFF_FOX_F = OFF_FOX_V + FOX_WIDTH
OFF_DIL_Q = OFF_FOX_F + FOX_HEADS
OFF_DIL_K = OFF_DIL_Q + DIL_WIDTH
OFF_DIL_V = OFF_DIL_K + DIL_WIDTH
OFF_GATE_A = OFF_DIL_V + DIL_WIDTH
OFF_GATE_B = OFF_GATE_A + D_MODEL
N_IN = OFF_GATE_B + D_MODEL

LANES = 128
UNIT = 256
DIL_L = 128

U_GATE_A, U_GATE_B, U_FOX_Q, U_FOX_K, U_FOX_V, U_DIL, U_FORGET = 0, 4, 8, 10, 12, 14, 23
N_UNITS = 24
P_WIDTH = U_DIL * UNIT
N_SLABS = 3 * N_DIL_GROUPS
_KIND = (["gate"] * 8 + ["normq"] * 2 + ["normk"] * 2 + ["plain"] * 2
         + ["normq", "normk", "plain"] * 3 + ["forget"])

TM_PROJ = 512
TQ_FOX = 256
MOE_ROWS = 256
VMEM_LIMIT = 56 * 1024 * 1024


def _dot(a, b):
    return jnp.dot(a, b, preferred_element_type=F32)


def _dot_nt(a, b):
    return lax.dot_general(a, b, (((1,), (1,)), ((), ())), preferred_element_type=F32)


def _split3(x):
    hi = x.astype(BF16)
    r1 = x - hi.astype(F32)
    mid = r1.astype(BF16)
    lo = (r1 - mid.astype(F32)).astype(BF16)
    return hi, mid, lo


def _ada_kernel(c_ref, w_ref, b_ref, o_ref):
    c = c_ref[...]
    s = c * jax.nn.sigmoid(c)
    s_hi = s.astype(BF16)
    s_lo = (s - s_hi.astype(F32)).astype(BF16)
    w = w_ref[...]
    w_hi = w.astype(BF16)
    w_lo = (w - w_hi.astype(F32)).astype(BF16)
    acc = _dot(s_hi, w_hi) + _dot(s_hi, w_lo) + _dot(s_lo, w_hi)
    o_ref[...] = acc + b_ref[...]


def _ada(c, w_ada, b_ada):
    B = c.shape[0]
    n_out = w_ada.shape[1]
    tn = 512
    return pl.pallas_call(
        _ada_kernel,
        grid=(n_out // tn,),
        in_specs=[pl.BlockSpec((B, D_MODEL), lambda j: (0, 0)),
                  pl.BlockSpec((D_MODEL, tn), lambda j: (0, j)),
                  pl.BlockSpec((1, tn), lambda j: (0, j))],
        out_specs=pl.BlockSpec((B, tn), lambda j: (0, j)),
        out_shape=jax.ShapeDtypeStruct((B, n_out), F32),
        name="ada_mod",
    )(c, w_ada, b_ada.reshape(1, n_out))


def _pack_bf16_pair(lo, hi):
    lo_bits = pltpu.bitcast(lo.astype(BF16).astype(F32), jnp.uint32) >> 16
    hi_bits = pltpu.bitcast(hi.astype(BF16).astype(F32), jnp.uint32) & jnp.uint32(0xFFFF0000)
    return lo_bits | hi_bits


def _unpack_bf16_pair(u):
    lo = pltpu.bitcast(u << 16, F32).astype(BF16)
    hi = pltpu.bitcast(u & jnp.uint32(0xFFFF0000), F32).astype(BF16)
    return lo, hi


def _inproj_kernel(x_ref, g_ref, sc_ref, sh_ref, w_ref, gain_ref, bd_ref, p_ref, f_ref, s_ref):
    x = x_ref[...]
    ms = jnp.mean(x * x, axis=-1, keepdims=True)
    h = x * lax.rsqrt(ms + EPS) * g_ref[...]
    h = h * (1.0 + sc_ref[0]) + sh_ref[0]
    hb = h.astype(BF16)

    def unit(u):
        cols = slice(u * UNIT, (u + 1) * UNIT)
        acc = _dot(hb, w_ref[:, cols])
        kind = _KIND[u]
        if kind == "gate":
            return jax.nn.sigmoid(acc)
        if kind in ("normq", "normk"):
            ss = _dot((acc * acc).astype(BF16), bd_ref[...])
            return acc * lax.rsqrt(ss * (1.0 / HEAD_DIM) + EPS) * gain_ref[:, cols]
        return acc

    for u in range(U_DIL):
        p_ref[:, u * UNIT:(u + 1) * UNIT] = unit(u).astype(BF16)
    for g in range(N_DIL_GROUPS):
        q, k, v = (unit(U_DIL + 3 * g + j) for j in range(3))
        s_ref[0, 3 * g] = _pack_bf16_pair(q[:, :LANES], k[:, :LANES])
        s_ref[0, 3 * g + 1] = _pack_bf16_pair(q[:, LANES:], k[:, LANES:])
        s_ref[0, 3 * g + 2] = _pack_bf16_pair(v[:, :LANES], v[:, LANES:])
    f_ref[...] = unit(U_FORGET)[:, :LANES]


def _inproj(x2d, norm_g, sc, sh, w_re, gain_row, S):
    T = x2d.shape[0]
    tm = TM_PROJ
    per_b = S // tm
    bd = np.kron(np.eye(UNIT // HEAD_DIM), np.ones((HEAD_DIM, HEAD_DIM))).astype(np.float32)
    return pl.pallas_call(
        _inproj_kernel,
        grid=(T // tm,),
        in_specs=[pl.BlockSpec((tm, D_MODEL), lambda i: (i, 0)),
                  pl.BlockSpec((1, D_MODEL), lambda i: (0, 0)),
                  pl.BlockSpec((1, 1, D_MODEL), lambda i: (i // per_b, 0, 0)),
                  pl.BlockSpec((1, 1, D_MODEL), lambda i: (i // per_b, 0, 0)),
                  pl.BlockSpec((D_MODEL, N_UNITS * UNIT), lambda i: (0, 0)),
                  pl.BlockSpec((1, N_UNITS * UNIT), lambda i: (0, 0)),
                  pl.BlockSpec((UNIT, UNIT), lambda i: (0, 0))],
        out_specs=[pl.BlockSpec((tm, P_WIDTH), lambda i: (i, 0)),
                   pl.BlockSpec((tm, LANES), lambda i: (i, 0)),
                   pl.BlockSpec((1, N_SLABS, tm, LANES), lambda i: (i // per_b, 0, i % per_b, 0))],
        out_shape=[jax.ShapeDtypeStruct((T, P_WIDTH), BF16),
                   jax.ShapeDtypeStruct((T, LANES), F32),
                   jax.ShapeDtypeStruct((T // S, N_SLABS, S, LANES), jnp.uint32)],
        compiler_params=pltpu.CompilerParams(vmem_limit_bytes=VMEM_LIMIT),
        name="in_proj",
    )(x2d, norm_g, sc, sh, w_re, gain_row, jnp.asarray(bd, BF16))


def _fcum_kernel(f_ref, b_ref, tri_ref, o_ref):
    S = f_ref.shape[1]
    xf = f_ref[0] + b_ref[...]
    ls = (jnp.minimum(xf, 0.0) - jnp.log(1.0 + jnp.exp(-jnp.abs(xf)))) * LOG2E
    lst = ls.T
    carry = jnp.zeros((LANES, UNIT), F32)
    for blk in range(S // UNIT):
        seg = lst[:, blk * UNIT:(blk + 1) * UNIT]
        hi, mid, lo = _split3(seg)
        tri = tri_ref[...]
        res = _dot(hi, tri) + _dot(mid, tri) + _dot(lo, tri)
        o_ref[0, :, blk * UNIT:(blk + 1) * UNIT] = (res[:, :UNIT] + carry)[:FOX_HEADS]
        carry = carry + res[:, UNIT:]


def _fcum(fgt, b_forget):
    B, S, _ = fgt.shape
    brow = jnp.zeros((1, LANES), F32).at[0, :FOX_HEADS].set(b_forget)
    tri = np.concatenate([np.triu(np.ones((UNIT, UNIT))), np.ones((UNIT, UNIT))], axis=1)
    return pl.pallas_call(
        _fcum_kernel,
        grid=(B,),
        in_specs=[pl.BlockSpec((1, S, LANES), lambda b: (b, 0, 0)),
                  pl.BlockSpec((1, LANES), lambda b: (0, 0)),
                  pl.BlockSpec((UNIT, 2 * UNIT), lambda b: (0, 0))],
        out_specs=pl.BlockSpec((1, FOX_HEADS, S), lambda b: (b, 0, 0)),
        out_shape=jax.ShapeDtypeStruct((B, FOX_HEADS, S), F32),
        name="forget_cumsum",
    )(fgt, brow, jnp.asarray(tri, BF16))


def _fox_kernel(q_ref, k_ref, v_ref, ck_ref, o_ref):
    S = q_ref.shape[1]
    pair = pl.program_id(1)
    tq = TQ_FOX
    lane = lax.broadcasted_iota(jnp.int32, (1, LANES), 1)
    row = lax.broadcasted_iota(jnp.int32, (tq, tq), 0)
    col = lax.broadcasted_iota(jnp.int32, (tq, tq), 1)
    causal = col <= row
    cks = [ck_ref[0, pl.ds(2 * pair + hh, 1), :] for hh in range(2)]
    for t in range(S // tq):
        r0, r1 = t * tq, (t + 1) * tq
        qt = q_ref[0, r0:r1, :]
        outs = []
        for hh in range(2):
            hsel = (lane >= HEAD_DIM) == bool(hh)
            qm = jnp.where(hsel, qt, jnp.zeros_like(qt))
            ck = cks[hh]
            s_d = _dot_nt(qm, k_ref[0, r0:r1, :]) - ck[:, r0:r1]
            s_d = jnp.where(causal, s_d, -jnp.inf)
            m = jnp.max(s_d, axis=-1, keepdims=True)
            if t > 0:
                s_o = _dot_nt(qm, k_ref[0, :r0, :]) - ck[:, :r0]
                m = jnp.maximum(m, jnp.max(s_o, axis=-1, keepdims=True))
            p_d = jnp.exp2(s_d - m)
            l = jnp.sum(p_d, axis=-1, keepdims=True)
            acc = _dot(p_d.astype(BF16), v_ref[0, r0:r1, :])
            if t > 0:
                p_o = jnp.exp2(s_o - m)
                l = l + jnp.sum(p_o, axis=-1, keepdims=True)
                acc = acc + _dot(p_o.astype(BF16), v_ref[0, :r0, :])
            outs.append(acc / l)
        o_ref[0, r0:r1, :] = jnp.where(lane < HEAD_DIM, outs[0], outs[1]).astype(BF16)


def _fox(p3, ck):
    B, S, _ = p3.shape
    nq, nk, nv = (U_FOX_Q * UNIT // LANES, U_FOX_K * UNIT // LANES, U_FOX_V * UNIT // LANES)
    return pl.pallas_call(
        _fox_kernel,
        grid=(B, FOX_HEADS // 2),
        in_specs=[pl.BlockSpec((1, S, LANES), lambda b, p: (b, 0, nq + p)),
                  pl.BlockSpec((1, S, LANES), lambda b, p: (b, 0, nk + p)),
                  pl.BlockSpec((1, S, LANES), lambda b, p: (b, 0, nv + p)),
                  pl.BlockSpec((1, FOX_HEADS, S), lambda b, p: (b, 0, 0))],
        out_specs=pl.BlockSpec((1, S, LANES), lambda b, p: (b, 0, p)),
        out_shape=jax.ShapeDtypeStruct((B, S, FOX_WIDTH), BF16),
        compiler_params=pltpu.CompilerParams(vmem_limit_bytes=VMEM_LIMIT),
        name="fox_attn",
    )(p3, p3, p3, ck)


def _t5_bucket(dist):
    max_exact = NUM_BUCKETS // 2
    d = np.maximum(dist, 1).astype(np.float32)
    large = max_exact + (np.log(d / max_exact) / np.log(REL_MAX_DISTANCE / max_exact)
                         * (NUM_BUCKETS - max_exact)).astype(np.int32)
    large = np.minimum(large, NUM_BUCKETS - 1)
    return np.where(dist < max_exact, dist, large).astype(np.int32)


def _relbias_kernel(tab_ref, bucket_ref, valid_ref, o_ref):
    g = pl.program_id(0)
    bk = bucket_ref[0]
    vd = valid_ref[0]
    for hs in range(DIL_HEADS_PER_GROUP):
        acc = jnp.zeros(bk.shape, F32)
        for b in range(NUM_BUCKETS):
            acc = jnp.where(bk == b, tab_ref[b, g * DIL_HEADS_PER_GROUP + hs], acc)
        bias = jnp.where(vd != 0, acc * LOG2E, -jnp.inf)
        o_ref[0, hs] = bias
        col = lax.broadcasted_iota(jnp.int32, bias.shape, 1)
        o_ref[1, hs] = jnp.where(col >= DIL_L, bias, -jnp.inf)


def _relbias(table):
    L = DIL_L
    i = np.arange(L)[:, None]
    j = np.arange(2 * L)[None, :]
    m = L + i - j
    valid = ((m >= 0) & (m <= L)).astype(np.int32)
    buckets = np.stack([_t5_bucket(np.clip(m, 0, None) * d) for _, d in DIL_GROUPS])
    valids = np.stack([valid] * N_DIL_GROUPS)
    return pl.pallas_call(
        _relbias_kernel,
        grid=(N_DIL_GROUPS,),
        in_specs=[pl.BlockSpec(memory_space=pltpu.SMEM),
                  pl.BlockSpec((1, L, 2 * L), lambda g: (g, 0, 0)),
                  pl.BlockSpec((1, L, 2 * L), lambda g: (g, 0, 0))],
        out_specs=pl.BlockSpec((2, DIL_HEADS_PER_GROUP, L, 2 * L), lambda g: (0, g, 0, 0)),
        out_shape=jax.ShapeDtypeStruct((2, DIL_HEADS, L, 2 * L), F32),
        name="rel_bias",
    )(table, jnp.asarray(buckets), jnp.asarray(valids))


def _dil_rows(start, d):
    return pl.ds(start, DIL_L) if d == 1 else pl.ds(start, DIL_L, stride=d)


def _dil_block(qkv_ref, bias_ref, m_scr, l_scr, acc_scr, g, d, r, n):
    L = DIL_L
    lane = lax.broadcasted_iota(jnp.int32, (1, LANES), 1)
    first = 1 - jnp.minimum(n, 1)
    cur = _dil_rows(r + d * (n * L), d)
    prev = _dil_rows(r + d * (jnp.maximum(n - 1, 0) * L), d)
    v_cur = _unpack_bf16_pair(qkv_ref[0, 3 * g + 2, cur, :])
    v_prev = _unpack_bf16_pair(qkv_ref[0, 3 * g + 2, prev, :])
    for pr in range(2):
        qt, k_cur = _unpack_bf16_pair(qkv_ref[0, 3 * g + pr, cur, :])
        _, k_prev = _unpack_bf16_pair(qkv_ref[0, 3 * g + pr, prev, :])
        kt = jnp.concatenate([k_prev, k_cur], axis=0)
        vt = jnp.concatenate([v_prev[pr], v_cur[pr]], axis=0)
        ms, ls, accs = [], [], []
        for hh in range(2):
            hsel = (lane >= HEAD_DIM) == bool(hh)
            qm = jnp.where(hsel, qt, jnp.zeros_like(qt))
            s = _dot_nt(qm, kt) + bias_ref[first, DIL_HEADS_PER_GROUP * g + 2 * pr + hh]
            m = jnp.max(s, axis=-1, keepdims=True)
            p = jnp.exp2(s - m)
            ms.append(m)
            ls.append(jnp.sum(p, axis=-1, keepdims=True))
            accs.append(_dot(p.astype(BF16), vt))
        low = lane < HEAD_DIM
        m_b = jnp.where(low, ms[0], ms[1])
        l_b = jnp.where(low, ls[0], ls[1])
        acc_b = jnp.where(low, accs[0], accs[1])
        if g == 0:
            m_scr[pr, cur, :] = m_b
            l_scr[pr, cur, :] = l_b
            acc_scr[pr, cur, :] = acc_b
        else:
            m_o = m_scr[pr, cur, :]
            m_n = jnp.maximum(m_o, m_b)
            a_o = jnp.exp2(m_o - m_n)
            a_b = jnp.exp2(m_b - m_n)
            m_scr[pr, cur, :] = m_n
            l_scr[pr, cur, :] = l_scr[pr, cur, :] * a_o + l_b * a_b
            acc_scr[pr, cur, :] = acc_scr[pr, cur, :] * a_o + acc_b * a_b


def _dil_kernel(qkv_ref, bias_ref, o_ref, m_scr, l_scr, acc_scr):
    S = o_ref.shape[1]
    for g, (window, d) in enumerate(DIL_GROUPS):
        nb = S // window

        def body(it, carry, g=g, d=d, nb=nb):
            _dil_block(qkv_ref, bias_ref, m_scr, l_scr, acc_scr, g, d, it // nb, it % nb)
            return carry
        lax.fori_loop(0, d * nb, body, 0)
    for pr in range(2):
        o_ref[0, :, pr * LANES:(pr + 1) * LANES] = (acc_scr[pr] / l_scr[pr]).astype(BF16)


def _dil(slabs, bias):
    B, _, S, _ = slabs.shape
    for window, d in DIL_GROUPS:
        assert window // d == DIL_L and S % window == 0
    stat = pltpu.VMEM((2, S, LANES), F32)
    return pl.pallas_call(
        _dil_kernel,
        grid=(B,),
        in_specs=[pl.BlockSpec((1, N_SLABS, S, LANES), lambda b: (b, 0, 0, 0)),
                  pl.BlockSpec(bias.shape, lambda b: (0, 0, 0, 0))],
        out_specs=pl.BlockSpec((1, S, DIL_OUT_WIDTH), lambda b: (b, 0, 0)),
        out_shape=jax.ShapeDtypeStruct((B, S, DIL_OUT_WIDTH), BF16),
        scratch_shapes=[stat, stat, stat],
        compiler_params=pltpu.CompilerParams(vmem_limit_bytes=VMEM_LIMIT),
        name="dil_attn",
    )(slabs, bias)


def _outproj_kernel(x_ref, ya_ref, yd_ref, ga_ref, gb_ref,
                    g1_ref, sc_ref, sh_ref, ng_ref, wa_ref, wb_ref, wo_ref, wr_ref, br_ref,
                    x1_ref, h2_ref, lg_ref):
    a = _dot(ya_ref[...], wa_ref[...])
    bm = _dot(yd_ref[...], wb_ref[...])
    merged = ga_ref[...].astype(F32) * a + gb_ref[...].astype(F32) * bm
    out = _dot(merged.astype(BF16), wo_ref[...])
    x1 = x_ref[...] + g1_ref[0] * out
    x1_ref[...] = x1
    ms = jnp.mean(x1 * x1, axis=-1, keepdims=True)
    h = x1 * lax.rsqrt(ms + EPS) * ng_ref[...]
    h = h * (1.0 + sc_ref[0]) + sh_ref[0]
    h2_ref[...] = h
    lg_ref[...] = _dot(h.astype(BF16), wr_ref[...]) + br_ref[...]


def _outproj(x2d, ya2d, yd2d, p2d, g1, sc2, sh2, norm_g, wa, wb, wo, wr, br, S):
    T = x2d.shape[0]
    tm = TM_PROJ
    per_b = S // tm
    row = lambda w: pl.BlockSpec((tm, w), lambda i: (i, 0))
    full = lambda a: pl.BlockSpec(a.shape, lambda i: (0,) * a.ndim)
    mod = pl.BlockSpec((1, 1, D_MODEL), lambda i: (i // per_b, 0, 0))
    return pl.pallas_call(
        _outproj_kernel,
        grid=(T // tm,),
        in_specs=[row(D_MODEL), row(FOX_WIDTH), row(DIL_OUT_WIDTH)]
                 + [pl.BlockSpec((tm, D_MODEL), lambda i: (i, U_GATE_A * UNIT // D_MODEL)),
                    pl.BlockSpec((tm, D_MODEL), lambda i: (i, U_GATE_B * UNIT // D_MODEL)),
                    mod, mod, mod, full(norm_g), full(wa), full(wb), full(wo), full(wr), full(br)],
        out_specs=[row(D_MODEL), row(D_MODEL), row(LANES)],
        out_shape=[jax.ShapeDtypeStruct((T, D_MODEL), F32),
                   jax.ShapeDtypeStruct((T, D_MODEL), F32),
                   jax.ShapeDtypeStruct((T, LANES), F32)],
        compiler_params=pltpu.CompilerParams(vmem_limit_bytes=VMEM_LIMIT),
        name="out_proj",
    )(x2d, ya2d, yd2d, p2d, p2d, g1, sc2, sh2, norm_g, wa, wb, wo, wr, br)


def _route_kernel(lg_ref, o_ref):
    lg = lg_ref[...]
    lane = lax.broadcasted_iota(jnp.int32, lg.shape, 1)
    neg = -jnp.inf
    big = jnp.int32(LANES)
    gl = jnp.where(lane < N_GROUPS, lg, neg)
    gmax = jnp.max(gl, axis=-1, keepdims=True)
    gidx = jnp.min(jnp.where(gl == gmax, lane, big), axis=-1, keepdims=True)
    gsum = jnp.sum(jnp.where(lane < N_GROUPS, jnp.exp(lg - gmax), 0.0), axis=-1, keepdims=True)
    lo = N_GROUPS + EXPERTS_PER_GROUP * gidx
    el = jnp.where((lane >= lo) & (lane < lo + EXPERTS_PER_GROUP), lg, neg)
    v1 = jnp.max(el, axis=-1, keepdims=True)
    i1 = jnp.min(jnp.where(el == v1, lane, big), axis=-1, keepdims=True)
    el2 = jnp.where(lane == i1, neg, el)
    v2 = jnp.max(el2, axis=-1, keepdims=True)
    i2 = jnp.min(jnp.where(el2 == v2, lane, big), axis=-1, keepdims=True)
    t = jnp.exp(v2 - v1)
    w1 = 1.0 / ((1.0 + t) * gsum)
    w2 = t / ((1.0 + t) * gsum)
    out = jnp.where(lane == 0, (i1 - N_GROUPS).astype(F32), 0.0)
    out = jnp.where(lane == 1, (i2 - N_GROUPS).astype(F32), out)
    out = jnp.where(lane == 2, w1, out)
    out = jnp.where(lane == 3, w2, out)
    o_ref[...] = out


def _route(logits):
    T = logits.shape[0]
    tm = 1024
    return pl.pallas_call(
        _route_kernel,
        grid=(T // tm,),
        in_specs=[pl.BlockSpec((tm, LANES), lambda i: (i, 0))],
        out_specs=pl.BlockSpec((tm, LANES), lambda i: (i, 0)),
        out_shape=jax.ShapeDtypeStruct((T, LANES), F32),
        name="route_topk",
    )(logits)


def _moe_kernel(be_ref, nv_ref, src0_ref, srcn_ref, dst_ref, rw_ref, w1_ref, w3_ref, w2_ref, h2_hbm,
                yk_hbm, xbuf, ybuf, wb1, wb3, wb2, gsem, ssem):
    i = pl.program_id(0)
    last = pl.num_programs(0) - 1
    slot = i % 2
    rows = MOE_ROWS
    nv = nv_ref[i]

    def gather(src_ref, s):
        def body(r, c):
            t = src_ref[0, 0, r]
            pltpu.make_async_copy(h2_hbm.at[pl.ds(t, 1)], xbuf.at[s, pl.ds(r, 1)], gsem.at[s]).start()
            return c
        lax.fori_loop(0, rows, body, 0, unroll=8)

    def wait_gather(s):
        pltpu.make_async_copy(h2_hbm.at[pl.ds(0, rows)], xbuf.at[s], gsem.at[s]).wait()

    def scatter_row(s, r, d):
        return pltpu.make_async_copy(ybuf.at[s, pl.ds(r, 1)], yk_hbm.at[pl.ds(d, 1)], ssem.at[s])

    def wait_scatter(s, count):
        def body(r, c):
            scatter_row(s, 0, 0).wait()
            return c
        lax.fori_loop(0, count, body, 0)

    @pl.when((i == 0) & (nv > 0))
    def _():
        gather(src0_ref, 0)

    @pl.when((i < last) & (nv_ref[jnp.minimum(i + 1, last)] > 0))
    def _():
        gather(srcn_ref, 1 - slot)

    @pl.when(i >= 2)
    def _():
        wait_scatter(slot, nv_ref[jnp.maximum(i - 2, 0)])

    @pl.when(nv > 0)
    def _():
        e = be_ref[i]
        e_prev = be_ref[jnp.maximum(i - 1, 0)]

        @pl.when((i == 0) | (e != e_prev))
        def _():
            wb1[...] = w1_ref[0].astype(BF16)
            wb3[...] = w3_ref[0].astype(BF16)
            wb2[...] = w2_ref[0].astype(BF16)

        wait_gather(slot)
        x = xbuf[slot].astype(BF16)
        a = _dot(x, wb1[...])
        b = _dot(x, wb3[...])
        hmid = (a * jax.nn.sigmoid(a) * b).astype(BF16)
        ybuf[slot] = _dot(hmid, wb2[...]) * rw_ref[...]

        def body(r, c):
            scatter_row(slot, r, dst_ref[0, 0, r]).start()
            return c
        lax.fori_loop(0, nv, body, 0)

    @pl.when(i == last)
    def _():
        @pl.when(last >= 1)
        def _():
            wait_scatter(1 - slot, nv_ref[jnp.maximum(last - 1, 0)])
        wait_scatter(slot, nv)


def _moe(h2, block_e, n_valid, src, dst, row_w, w1, w3, w2, n_rows_out):
    n_blk = block_e.shape[0]
    rows = MOE_ROWS
    last = n_blk - 1
    idx_spec = lambda f: pl.BlockSpec((1, 1, rows), f, memory_space=pltpu.SMEM)
    grid_spec = pltpu.PrefetchScalarGridSpec(
        num_scalar_prefetch=2,
        grid=(n_blk,),
        in_specs=[idx_spec(lambda i, be, nu: (0, 0, 0)),
                  idx_spec(lambda i, be, nu: (jnp.minimum(i + 1, last), 0, 0)),
                  idx_spec(lambda i, be, nu: (i, 0, 0)),
                  pl.BlockSpec((rows, 1), lambda i, be, nu: (i, 0)),
                  pl.BlockSpec((1, D_MODEL, EXPERT_HIDDEN), lambda i, be, nu: (be[i], 0, 0)),
                  pl.BlockSpec((1, D_MODEL, EXPERT_HIDDEN), lambda i, be, nu: (be[i], 0, 0)),
                  pl.BlockSpec((1, EXPERT_HIDDEN, D_MODEL), lambda i, be, nu: (be[i], 0, 0)),
                  pl.BlockSpec(memory_space=pl.ANY)],
        out_specs=pl.BlockSpec(memory_space=pl.ANY),
        scratch_shapes=[pltpu.VMEM((2, rows, D_MODEL), F32),
                        pltpu.VMEM((2, rows, D_MODEL), F32),
                        pltpu.VMEM((D_MODEL, EXPERT_HIDDEN), BF16),
                        pltpu.VMEM((D_MODEL, EXPERT_HIDDEN), BF16),
                        pltpu.VMEM((EXPERT_HIDDEN, D_MODEL), BF16),
                        pltpu.SemaphoreType.DMA((2,)),
                        pltpu.SemaphoreType.DMA((2,))])
    return pl.pallas_call(
        _moe_kernel,
        grid_spec=grid_spec,
        out_shape=jax.ShapeDtypeStruct((n_rows_out, D_MODEL), F32),
        compiler_params=pltpu.CompilerParams(dimension_semantics=("arbitrary",),
                                             vmem_limit_bytes=VMEM_LIMIT),
        name="moe_ffn",
    )(block_e, n_valid, src, src, dst, row_w, w1, w3, w2, h2)


def _final_kernel(x1_ref, yk_ref, g2_ref, o_ref):
    yk = yk_ref[...]
    o_ref[...] = x1_ref[...] + g2_ref[0] * (yk[:, :D_MODEL] + yk[:, D_MODEL:])


def _final(x1, yk2, g2, S):
    T = x1.shape[0]
    tm = TM_PROJ
    per_b = S // tm
    return pl.pallas_call(
        _final_kernel,
        grid=(T // tm,),
        in_specs=[pl.BlockSpec((tm, D_MODEL), lambda i: (i, 0)),
                  pl.BlockSpec((tm, 2 * D_MODEL), lambda i: (i, 0)),
                  pl.BlockSpec((1, 1, D_MODEL), lambda i: (i // per_b, 0, 0))],
        out_specs=pl.BlockSpec((tm, D_MODEL), lambda i: (i, 0)),
        out_shape=jax.ShapeDtypeStruct((T, D_MODEL), F32),
        name="final_residual",
    )(x1, yk2, g2)


def _prep_w_in(w_in):
    dq = w_in[:, OFF_DIL_Q:OFF_DIL_K]
    dk = w_in[:, OFF_DIL_K:OFF_DIL_V]
    dv = w_in[:, OFF_DIL_V:OFF_GATE_A]
    dil = []
    for g in range(N_DIL_GROUPS):
        cs = slice(g * DIL_OUT_WIDTH, (g + 1) * DIL_OUT_WIDTH)
        dil += [dq[:, cs], dk[:, cs], dv[:, cs]]
    pad = jnp.zeros((D_MODEL, UNIT - FOX_HEADS), w_in.dtype)
    cols = [w_in[:, OFF_GATE_A:OFF_GATE_B], w_in[:, OFF_GATE_B:N_IN],
            w_in[:, OFF_FOX_Q:OFF_FOX_K], w_in[:, OFF_FOX_K:OFF_FOX_V], w_in[:, OFF_FOX_V:OFF_FOX_F],
            *dil, w_in[:, OFF_FOX_F:OFF_DIL_Q], pad]
    return jnp.concatenate(cols, axis=1).astype(BF16)


def _prep_gain(q_gain, k_gain):
    qs = HEAD_DIM ** -0.5 * LOG2E
    ones = jnp.ones((UNIT,), F32)
    fq = q_gain[:FOX_HEADS].reshape(-1) * qs
    fk = k_gain[:FOX_HEADS].reshape(-1)
    dq = q_gain[FOX_HEADS:].reshape(-1) * qs
    dk = k_gain[FOX_HEADS:].reshape(-1)
    dil = []
    for g in range(N_DIL_GROUPS):
        cs = slice(g * DIL_OUT_WIDTH, (g + 1) * DIL_OUT_WIDTH)
        dil += [dq[cs], dk[cs], ones]
    parts = [ones] * 8 + [fq, fk, ones, ones] + dil + [ones]
    return jnp.concatenate(parts).reshape(1, N_UNITS * UNIT)


def _dispatch_plan(routing, T):
    rows = MOE_ROWS
    A = 2 * T
    flat_e = routing[:, 0:2].astype(jnp.int32).reshape(A)
    flat_w = routing[:, 2:4].reshape(A)
    counts = jnp.zeros((N_EXPERTS,), jnp.int32).at[flat_e].add(1)
    starts = jnp.cumsum(counts) - counts
    padded = (counts + rows - 1) // rows * rows
    pends = jnp.cumsum(padded)
    pstarts = pends - padded
    order = jnp.argsort(flat_e).astype(jnp.int32)
    sorted_e = flat_e[order]
    dest = pstarts[sorted_e] + jnp.arange(A, dtype=jnp.int32) - starts[sorted_e]
    R = A + N_EXPERTS * rows
    n_blk = R // rows
    src = jnp.zeros((R,), jnp.int32).at[dest].set(order // 2)
    dst = jnp.zeros((R,), jnp.int32).at[dest].set(order)
    row_w = jnp.zeros((R,), F32).at[dest].set(flat_w[order])
    blk0 = jnp.arange(n_blk, dtype=jnp.int32) * rows
    block_e = jnp.minimum(jnp.searchsorted(pends, blk0, side="right"),
                          N_EXPERTS - 1).astype(jnp.int32)
    n_valid = jnp.clip(pstarts[block_e] + counts[block_e] - blk0, 0, rows)
    n_valid = jnp.where(blk0 < pends[-1], n_valid, 0).astype(jnp.int32)
    return (block_e, n_valid, src.reshape(n_blk, 1, rows), dst.reshape(n_blk, 1, rows),
            row_w.reshape(R, 1))


def _layer(x, mod, rel_bias_table, norm1_g, w_in, b_forget, q_gain, k_gain, w_branch_a, w_branch_b,
           w_out, norm2_g, w_rg, b_rg, w_re, b_re, w1, w3, w2):
    B, S, D = x.shape
    T = B * S
    sh1, sc1, g1, sh2, sc2, g2 = [m.reshape(B, 1, D) for m in jnp.split(mod, 6, axis=-1)]
    x2d = x.reshape(T, D)

    p2d, fgt, slabs = _inproj(x2d, norm1_g.reshape(1, D), sc1, sh1, _prep_w_in(w_in),
                              _prep_gain(q_gain, k_gain), S)
    p3 = p2d.reshape(B, S, P_WIDTH)
    ck = _fcum(fgt.reshape(B, S, LANES), b_forget)
    ya = _fox(p3, ck)
    yd = _dil(slabs, _relbias(rel_bias_table))

    n_router = N_GROUPS + N_EXPERTS
    wr = jnp.concatenate([w_rg, w_re, jnp.zeros((D, LANES - n_router), F32)], axis=1).astype(BF16)
    br = jnp.concatenate([b_rg, b_re, jnp.zeros((LANES - n_router,), F32)]).reshape(1, LANES)
    x1, h2, logits = _outproj(x2d, ya.reshape(T, FOX_WIDTH), yd.reshape(T, DIL_OUT_WIDTH), p2d,
                              g1, sc2, sh2, norm2_g.reshape(1, D),
                              w_branch_a.astype(BF16), w_branch_b.astype(BF16), w_out.astype(BF16),
                              wr, br, S)
    routing = _route(logits)
    block_e, n_valid, src, dst, row_w = _dispatch_plan(routing, T)
    yk = _moe(h2, block_e, n_valid, src, dst, row_w, w1, w3, w2, 2 * T)
    out = _final(x1, yk.reshape(T, 2 * D), g2, S)
    return out.reshape(B, S, D)


def kernel(x, c, rel_bias_table, w_ada, b_ada, norm1_g, w_in, b_forget, q_gain, k_gain, w_branch_a, w_branch_b, w_out, norm2_g, w_router_group, b_router_group, w_router_expert, b_router_expert, w1, w3, w2):
    depth = w_ada.shape[0]
    for l in range(depth):
        mod = _ada(c, w_ada[l], b_ada[l])
        x = _layer(x, mod, rel_bias_table, norm1_g[l], w_in[l], b_forget[l], q_gain[l], k_gain[l],
                   w_branch_a[l], w_branch_b[l], w_out[l], norm2_g[l], w_router_group[l],
                   b_router_group[l], w_router_expert[l], b_router_expert[l], w1[l], w3[l], w2[l])
    return x
```

```python
import functools
import math

import numpy as np
import jax
import jax.numpy as jnp
from jax import lax
from jax.experimental import pallas as pl
from jax.experimental.pallas import tpu as pltpu

F32 = jnp.float32
BF16 = jnp.bfloat16

D_MODEL = 1024
HEAD_DIM = 64
FOX_HEADS = 8
DIL_GROUPS = ((128, 1), (512, 4), (2048, 16))
DIL_HEADS_PER_GROUP = 4
N_DIL_GROUPS = len(DIL_GROUPS)
DIL_HEADS = N_DIL_GROUPS * DIL_HEADS_PER_GROUP
FOX_WIDTH = FOX_HEADS * HEAD_DIM
DIL_WIDTH = DIL_HEADS * HEAD_DIM
DIL_OUT_WIDTH = DIL_HEADS_PER_GROUP * HEAD_DIM
NUM_BUCKETS = 32
REL_MAX_DISTANCE = 2048
N_GROUPS = 4
EXPERTS_PER_GROUP = 8
N_EXPERTS = N_GROUPS * EXPERTS_PER_GROUP
EXPERT_HIDDEN = D_MODEL // 2
EPS = 1e-6
LOG2E = math.log2(math.e)

OFF_FOX_Q = 0
OFF_FOX_K = OFF_FOX_Q + FOX_WIDTH
OFF_FOX_V = OFF_FOX_K + FOX_WIDTH
OFF_FOX_F = OFF_FOX_V + FOX_WIDTH
OFF_DIL_Q = OFF_FOX_F + FOX_HEADS
OFF_DIL_K = OFF_DIL_Q + DIL_WIDTH
OFF_DIL_V = OFF_DIL_K + DIL_WIDTH
OFF_GATE_A = OFF_DIL_V + DIL_WIDTH
OFF_GATE_B = OFF_GATE_A + D_MODEL
N_IN = OFF_GATE_B + D_MODEL

LANES = 128
UNIT = 256
DIL_L = 128

U_GATE_A, U_GATE_B, U_FOX_Q, U_FOX_K, U_FOX_V, U_DIL, U_FORGET = 0, 4, 8, 10, 12, 14, 23
N_UNITS = 24
P_WIDTH = U_DIL * UNIT
N_SLABS = 3 * N_DIL_GROUPS
_KIND = (["gate"] * 8 + ["normq"] * 2 + ["normk"] * 2 + ["plain"] * 2
         + ["normq", "normk", "plain"] * 3 + ["forget"])

TM_PROJ = 512
TQ_FOX = 256
MOE_ROWS = 256
MOE_TILE = 512
XS_WIDTH = D_MODEL // 2 + LANES
SUBLANES = 8
TILE_ROWS = 2 * MOE_TILE + N_EXPERTS * SUBLANES
VMEM_LIMIT = 56 * 1024 * 1024


def _dot(a, b):
    return jnp.dot(a, b, preferred_element_type=F32)


def _dot_nt(a, b):
    return lax.dot_general(a, b, (((1,), (1,)), ((), ())), preferred_element_type=F32)


def _split3(x):
    hi = x.astype(BF16)
    r1 = x - hi.astype(F32)
    mid = r1.astype(BF16)
    lo = (r1 - mid.astype(F32)).astype(BF16)
    return hi, mid, lo


def _ada_kernel(c_ref, w_ref, b_ref, o_ref):
    c = c_ref[...]
    s = c * jax.nn.sigmoid(c)
    s_hi = s.astype(BF16)
    s_lo = (s - s_hi.astype(F32)).astype(BF16)
    w = w_ref[...]
    w_hi = w.astype(BF16)
    w_lo = (w - w_hi.astype(F32)).astype(BF16)
    acc = _dot(s_hi, w_hi) + _dot(s_hi, w_lo) + _dot(s_lo, w_hi)
    o_ref[...] = acc + b_ref[...]


def _ada(c, w_ada, b_ada):
    B = c.shape[0]
    n_out = w_ada.shape[1]
    tn = 512
    return pl.pallas_call(
        _ada_kernel,
        grid=(n_out // tn,),
        in_specs=[pl.BlockSpec((B, D_MODEL), lambda j: (0, 0)),
                  pl.BlockSpec((D_MODEL, tn), lambda j: (0, j)),
                  pl.BlockSpec((1, tn), lambda j: (0, j))],
        out_specs=pl.BlockSpec((B, tn), lambda j: (0, j)),
        out_shape=jax.ShapeDtypeStruct((B, n_out), F32),
        name="ada_mod",
    )(c, w_ada, b_ada.reshape(1, n_out))


def _pack_bf16_pair(lo, hi):
    lo_bits = pltpu.bitcast(lo.astype(BF16).astype(F32), jnp.uint32) >> 16
    hi_bits = pltpu.bitcast(hi.astype(BF16).astype(F32), jnp.uint32) & jnp.uint32(0xFFFF0000)
    return lo_bits | hi_bits


def _unpack_bf16_pair(u):
    lo = pltpu.bitcast(u << 16, F32).astype(BF16)
    hi = pltpu.bitcast(u & jnp.uint32(0xFFFF0000), F32).astype(BF16)
    return lo, hi


def _inproj_kernel(x_ref, g_ref, sc_ref, sh_ref, w_ref, gain_ref, bd_ref, p_ref, f_ref, s_ref):
    x = x_ref[...]
    ms = jnp.mean(x * x, axis=-1, keepdims=True)
    h = x * lax.rsqrt(ms + EPS) * g_ref[...]
    h = h * (1.0 + sc_ref[0]) + sh_ref[0]
    hb = h.astype(BF16)

    def unit(u):
        cols = slice(u * UNIT, (u + 1) * UNIT)
        acc = _dot(hb, w_ref[:, cols])
        kind = _KIND[u]
        if kind == "gate":
            return jax.nn.sigmoid(acc)
        if kind in ("normq", "normk"):
            ss = _dot((acc * acc).astype(BF16), bd_ref[...])
            return acc * lax.rsqrt(ss * (1.0 / HEAD_DIM) + EPS) * gain_ref[:, cols]
        return acc

    for u in range(U_DIL):
        p_ref[:, u * UNIT:(u + 1) * UNIT] = unit(u).astype(BF16)
    for g in range(N_DIL_GROUPS):
        q, k, v = (unit(U_DIL + 3 * g + j) for j in range(3))
        s_ref[0, 3 * g] = _pack_bf16_pair(q[:, :LANES], k[:, :LANES])
        s_ref[0, 3 * g + 1] = _pack_bf16_pair(q[:, LANES:], k[:, LANES:])
        s_ref[0, 3 * g + 2] = _pack_bf16_pair(v[:, :LANES], v[:, LANES:])
    f_ref[...] = unit(U_FORGET)[:, :LANES]


def _inproj(x2d, norm_g, sc, sh, w_re, gain_row, S):
    T = x2d.shape[0]
    tm = TM_PROJ
    per_b = S // tm
    bd = np.kron(np.eye(UNIT // HEAD_DIM), np.ones((HEAD_DIM, HEAD_DIM))).astype(np.float32)
    return pl.pallas_call(
        _inproj_kernel,
        grid=(T // tm,),
        in_specs=[pl.BlockSpec((tm, D_MODEL), lambda i: (i, 0)),
                  pl.BlockSpec((1, D_MODEL), lambda i: (0, 0)),
                  pl.BlockSpec((1, 1, D_MODEL), lambda i: (i // per_b, 0, 0)),
                  pl.BlockSpec((1, 1, D_MODEL), lambda i: (i // per_b, 0, 0)),
                  pl.BlockSpec((D_MODEL, N_UNITS * UNIT), lambda i: (0, 0)),
                  pl.BlockSpec((1, N_UNITS * UNIT), lambda i: (0, 0)),
                  pl.BlockSpec((UNIT, UNIT), lambda i: (0, 0))],
        out_specs=[pl.BlockSpec((tm, P_WIDTH), lambda i: (i, 0)),
                   pl.BlockSpec((tm, LANES), lambda i: (i, 0)),
                   pl.BlockSpec((1, N_SLABS, tm, LANES), lambda i: (i // per_b, 0, i % per_b, 0))],
        out_shape=[jax.ShapeDtypeStruct((T, P_WIDTH), BF16),
                   jax.ShapeDtypeStruct((T, LANES), F32),
                   jax.ShapeDtypeStruct((T // S, N_SLABS, S, LANES), jnp.uint32)],
        compiler_params=pltpu.CompilerParams(vmem_limit_bytes=VMEM_LIMIT),
        name="in_proj",
    )(x2d, norm_g, sc, sh, w_re, gain_row, jnp.asarray(bd, BF16))


def _fcum_kernel(f_ref, b_ref, tri_ref, o_ref):
    S = f_ref.shape[1]
    xf = f_ref[0] + b_ref[...]
    ls = (jnp.minimum(xf, 0.0) - jnp.log(1.0 + jnp.exp(-jnp.abs(xf)))) * LOG2E
    lst = ls.T
    carry = jnp.zeros((LANES, UNIT), F32)
    for blk in range(S // UNIT):
        seg = lst[:, blk * UNIT:(blk + 1) * UNIT]
        hi, mid, lo = _split3(seg)
        tri = tri_ref[...]
        res = _dot(hi, tri) + _dot(mid, tri) + _dot(lo, tri)
        o_ref[0, :, blk * UNIT:(blk + 1) * UNIT] = (res[:, :UNIT] + carry)[:FOX_HEADS]
        carry = carry + res[:, UNIT:]


def _fcum(fgt, b_forget):
    B, S, _ = fgt.shape
    brow = jnp.zeros((1, LANES), F32).at[0, :FOX_HEADS].set(b_forget)
    tri = np.concatenate([np.triu(np.ones((UNIT, UNIT))), np.ones((UNIT, UNIT))], axis=1)
    return pl.pallas_call(
        _fcum_kernel,
        grid=(B,),
        in_specs=[pl.BlockSpec((1, S, LANES), lambda b: (b, 0, 0)),
                  pl.BlockSpec((1, LANES), lambda b: (0, 0)),
                  pl.BlockSpec((UNIT, 2 * UNIT), lambda b: (0, 0))],
        out_specs=pl.BlockSpec((1, FOX_HEADS, S), lambda b: (b, 0, 0)),
        out_shape=jax.ShapeDtypeStruct((B, FOX_HEADS, S), F32),
        name="forget_cumsum",
    )(fgt, brow, jnp.asarray(tri, BF16))


def _fox_kernel(q_ref, k_ref, v_ref, ck_ref, o_ref):
    S = q_ref.shape[1]
    pair = pl.program_id(1)
    tq = TQ_FOX
    lane = lax.broadcasted_iota(jnp.int32, (1, LANES), 1)
    row = lax.broadcasted_iota(jnp.int32, (tq, tq), 0)
    col = lax.broadcasted_iota(jnp.int32, (tq, tq), 1)
    causal = col <= row
    cks = [ck_ref[0, pl.ds(2 * pair + hh, 1), :] for hh in range(2)]
    for t in range(S // tq):
        r0, r1 = t * tq, (t + 1) * tq
        qt = q_ref[0, r0:r1, :]
        outs = []
        for hh in range(2):
            hsel = (lane >= HEAD_DIM) == bool(hh)
            qm = jnp.where(hsel, qt, jnp.zeros_like(qt))
            ck = cks[hh]
            s_d = _dot_nt(qm, k_ref[0, r0:r1, :]) - ck[:, r0:r1]
            s_d = jnp.where(causal, s_d, -jnp.inf)
            m = jnp.max(s_d, axis=-1, keepdims=True)
            if t > 0:
                s_o = _dot_nt(qm, k_ref[0, :r0, :]) - ck[:, :r0]
                m = jnp.maximum(m, jnp.max(s_o, axis=-1, keepdims=True))
            p_d = jnp.exp2(s_d - m)
            l = jnp.sum(p_d, axis=-1, keepdims=True)
            acc = _dot(p_d.astype(BF16), v_ref[0, r0:r1, :])
            if t > 0:
                p_o = jnp.exp2(s_o - m)
                l = l + jnp.sum(p_o, axis=-1, keepdims=True)
                acc = acc + _dot(p_o.astype(BF16), v_ref[0, :r0, :])
            outs.append(acc / l)
        o_ref[0, r0:r1, :] = jnp.where(lane < HEAD_DIM, outs[0], outs[1]).astype(BF16)


def _fox(p3, ck):
    B, S, _ = p3.shape
    nq, nk, nv = (U_FOX_Q * UNIT // LANES, U_FOX_K * UNIT // LANES, U_FOX_V * UNIT // LANES)
    return pl.pallas_call(
        _fox_kernel,
        grid=(B, FOX_HEADS // 2),
        in_specs=[pl.BlockSpec((1, S, LANES), lambda b, p: (b, 0, nq + p)),
                  pl.BlockSpec((1, S, LANES), lambda b, p: (b, 0, nk + p)),
                  pl.BlockSpec((1, S, LANES), lambda b, p: (b, 0, nv + p)),
                  pl.BlockSpec((1, FOX_HEADS, S), lambda b, p: (b, 0, 0))],
        out_specs=pl.BlockSpec((1, S, LANES), lambda b, p: (b, 0, p)),
        out_shape=jax.ShapeDtypeStruct((B, S, FOX_WIDTH), BF16),
        compiler_params=pltpu.CompilerParams(vmem_limit_bytes=VMEM_LIMIT),
        name="fox_attn",
    )(p3, p3, p3, ck)


def _t5_bucket(dist):
    max_exact = NUM_BUCKETS // 2
    d = np.maximum(dist, 1).astype(np.float32)
    large = max_exact + (np.log(d / max_exact) / np.log(REL_MAX_DISTANCE / max_exact)
                         * (NUM_BUCKETS - max_exact)).astype(np.int32)
    large = np.minimum(large, NUM_BUCKETS - 1)
    return np.where(dist < max_exact, dist, large).astype(np.int32)


def _relbias_kernel(tab_ref, bucket_ref, valid_ref, o_ref):
    g = pl.program_id(0)
    bk = bucket_ref[0]
    vd = valid_ref[0]
    for hs in range(DIL_HEADS_PER_GROUP):
        acc = jnp.zeros(bk.shape, F32)
        for b in range(NUM_BUCKETS):
            acc = jnp.where(bk == b, tab_ref[b, g * DIL_HEADS_PER_GROUP + hs], acc)
        bias = jnp.where(vd != 0, acc * LOG2E, -jnp.inf)
        o_ref[0, hs] = bias
        col = lax.broadcasted_iota(jnp.int32, bias.shape, 1)
        o_ref[1, hs] = jnp.where(col >= DIL_L, bias, -jnp.inf)


def _relbias(table):
    L = DIL_L
    i = np.arange(L)[:, None]
    j = np.arange(2 * L)[None, :]
    m = L + i - j
    valid = ((m >= 0) & (m <= L)).astype(np.int32)
    buckets = np.stack([_t5_bucket(np.clip(m, 0, None) * d) for _, d in DIL_GROUPS])
    valids = np.stack([valid] * N_DIL_GROUPS)
    return pl.pallas_call(
        _relbias_kernel,
        grid=(N_DIL_GROUPS,),
        in_specs=[pl.BlockSpec(memory_space=pltpu.SMEM),
                  pl.BlockSpec((1, L, 2 * L), lambda g: (g, 0, 0)),
                  pl.BlockSpec((1, L, 2 * L), lambda g: (g, 0, 0))],
        out_specs=pl.BlockSpec((2, DIL_HEADS_PER_GROUP, L, 2 * L), lambda g: (0, g, 0, 0)),
        out_shape=jax.ShapeDtypeStruct((2, DIL_HEADS, L, 2 * L), F32),
        name="rel_bias",
    )(table, jnp.asarray(buckets), jnp.asarray(valids))


def _dil_rows(start, d):
    return pl.ds(start, DIL_L) if d == 1 else pl.ds(start, DIL_L, stride=d)


def _dil_block(qkv_ref, bias_ref, m_scr, l_scr, acc_scr, g, d, r, n):
    L = DIL_L
    lane = lax.broadcasted_iota(jnp.int32, (1, LANES), 1)
    first = 1 - jnp.minimum(n, 1)
    cur = _dil_rows(r + d * (n * L), d)
    prev = _dil_rows(r + d * (jnp.maximum(n - 1, 0) * L), d)
    v_cur = _unpack_bf16_pair(qkv_ref[0, 3 * g + 2, cur, :])
    v_prev = _unpack_bf16_pair(qkv_ref[0, 3 * g + 2, prev, :])
    for pr in range(2):
        qt, k_cur = _unpack_bf16_pair(qkv_ref[0, 3 * g + pr, cur, :])
        _, k_prev = _unpack_bf16_pair(qkv_ref[0, 3 * g + pr, prev, :])
        kt = jnp.concatenate([k_prev, k_cur], axis=0)
        vt = jnp.concatenate([v_prev[pr], v_cur[pr]], axis=0)
        ms, ls, accs = [], [], []
        for hh in range(2):
            hsel = (lane >= HEAD_DIM) == bool(hh)
            qm = jnp.where(hsel, qt, jnp.zeros_like(qt))
            s = _dot_nt(qm, kt) + bias_ref[first, DIL_HEADS_PER_GROUP * g + 2 * pr + hh]
            m = jnp.max(s, axis=-1, keepdims=True)
            p = jnp.exp2(s - m)
            ms.append(m)
            ls.append(jnp.sum(p, axis=-1, keepdims=True))
            accs.append(_dot(p.astype(BF16), vt))
        low = lane < HEAD_DIM
        m_b = jnp.where(low, ms[0], ms[1])
        l_b = jnp.where(low, ls[0], ls[1])
        acc_b = jnp.where(low, accs[0], accs[1])
        if g == 0:
            m_scr[pr, cur, :] = m_b
            l_scr[pr, cur, :] = l_b
            acc_scr[pr, cur, :] = acc_b
        else:
            m_o = m_scr[pr, cur, :]
            m_n = jnp.maximum(m_o, m_b)
            a_o = jnp.exp2(m_o - m_n)
            a_b = jnp.exp2(m_b - m_n)
            m_scr[pr, cur, :] = m_n
            l_scr[pr, cur, :] = l_scr[pr, cur, :] * a_o + l_b * a_b
            acc_scr[pr, cur, :] = acc_scr[pr, cur, :] * a_o + acc_b * a_b


def _dil_kernel(qkv_ref, bias_ref, o_ref, m_scr, l_scr, acc_scr):
    S = o_ref.shape[1]
    for g, (window, d) in enumerate(DIL_GROUPS):
        nb = S // window

        def body(it, carry, g=g, d=d, nb=nb):
            _dil_block(qkv_ref, bias_ref, m_scr, l_scr, acc_scr, g, d, it // nb, it % nb)
            return carry
        lax.fori_loop(0, d * nb, body, 0)
    for pr in range(2):
        o_ref[0, :, pr * LANES:(pr + 1) * LANES] = (acc_scr[pr] / l_scr[pr]).astype(BF16)


def _dil(slabs, bias):
    B, _, S, _ = slabs.shape
    for window, d in DIL_GROUPS:
        assert window // d == DIL_L and S % window == 0
    stat = pltpu.VMEM((2, S, LANES), F32)
    return pl.pallas_call(
        _dil_kernel,
        grid=(B,),
        in_specs=[pl.BlockSpec((1, N_SLABS, S, LANES), lambda b: (b, 0, 0, 0)),
                  pl.BlockSpec(bias.shape, lambda b: (0, 0, 0, 0))],
        out_specs=pl.BlockSpec((1, S, DIL_OUT_WIDTH), lambda b: (b, 0, 0)),
        out_shape=jax.ShapeDtypeStruct((B, S, DIL_OUT_WIDTH), BF16),
        scratch_shapes=[stat, stat, stat],
        compiler_params=pltpu.CompilerParams(vmem_limit_bytes=VMEM_LIMIT),
        name="dil_attn",
    )(slabs, bias)


def _outproj_kernel(x_ref, ya_ref, yd_ref, ga_ref, gb_ref,
                    g1_ref, sc_ref, sh_ref, ng_ref, wa_ref, wb_ref, wo_ref, wr_ref, br_ref,
                    x1_ref, h2_ref, lg_ref):
    a = _dot(ya_ref[...], wa_ref[...])
    bm = _dot(yd_ref[...], wb_ref[...])
    merged = ga_ref[...].astype(F32) * a + gb_ref[...].astype(F32) * bm
    out = _dot(merged.astype(BF16), wo_ref[...])
    x1 = x_ref[...] + g1_ref[0] * out
    x1_ref[...] = x1
    ms = jnp.mean(x1 * x1, axis=-1, keepdims=True)
    h = x1 * lax.rsqrt(ms + EPS) * ng_ref[...]
    h = h * (1.0 + sc_ref[0]) + sh_ref[0]
    hb = h.astype(BF16)
    h2_ref[...] = hb
    lg_ref[...] = _dot(hb, wr_ref[...]) + br_ref[...]


def _outproj(x2d, ya2d, yd2d, p2d, g1, sc2, sh2, norm_g, wa, wb, wo, wr, br, S):
    T = x2d.shape[0]
    tm = TM_PROJ
    per_b = S // tm
    row = lambda w: pl.BlockSpec((tm, w), lambda i: (i, 0))
    full = lambda a: pl.BlockSpec(a.shape, lambda i: (0,) * a.ndim)
    mod = pl.BlockSpec((1, 1, D_MODEL), lambda i: (i // per_b, 0, 0))
    return pl.pallas_call(
        _outproj_kernel,
        grid=(T // tm,),
        in_specs=[row(D_MODEL), row(FOX_WIDTH), row(DIL_OUT_WIDTH)]
                 + [pl.BlockSpec((tm, D_MODEL), lambda i: (i, U_GATE_A * UNIT // D_MODEL)),
                    pl.BlockSpec((tm, D_MODEL), lambda i: (i, U_GATE_B * UNIT // D_MODEL)),
                    mod, mod, mod, full(norm_g), full(wa), full(wb), full(wo), full(wr), full(br)],
        out_specs=[row(D_MODEL), row(D_MODEL), row(LANES)],
        out_shape=[jax.ShapeDtypeStruct((T, D_MODEL), F32),
                   jax.ShapeDtypeStruct((T, D_MODEL), BF16),
                   jax.ShapeDtypeStruct((T, LANES), F32)],
        compiler_params=pltpu.CompilerParams(vmem_limit_bytes=VMEM_LIMIT),
        name="out_proj",
    )(x2d, ya2d, yd2d, p2d, p2d, g1, sc2, sh2, norm_g, wa, wb, wo, wr, br)


def _dispatch_kernel(lg_ref, h_ref, tri_ref, xs_ref, cnt_ref, pos_ref):
    tt = lg_ref.shape[0]
    lt = lg_ref[...].T
    row = lambda i: lt[i:i + 1, :]
    neg = -jnp.inf
    g = [row(i) for i in range(N_GROUPS)]
    gmax = functools.reduce(jnp.maximum, g)
    gidx = jnp.full(gmax.shape, N_GROUPS - 1, jnp.int32)
    for i in reversed(range(N_GROUPS - 1)):
        gidx = jnp.where(g[i] == gmax, i, gidx)
    gsum = sum(jnp.exp(gi - gmax) for gi in g)
    el = []
    for j in range(EXPERTS_PER_GROUP):
        v = row(N_GROUPS + EXPERTS_PER_GROUP * (N_GROUPS - 1) + j)
        for gg in reversed(range(N_GROUPS - 1)):
            v = jnp.where(gidx == gg, row(N_GROUPS + EXPERTS_PER_GROUP * gg + j), v)
        el.append(v)

    def top(vals):
        best = functools.reduce(jnp.maximum, vals)
        idx = jnp.full(best.shape, EXPERTS_PER_GROUP - 1, jnp.int32)
        for j in reversed(range(EXPERTS_PER_GROUP - 1)):
            idx = jnp.where(vals[j] == best, j, idx)
        return best, idx

    v1, i1 = top(el)
    v2, i2 = top([jnp.where(i1 == j, neg, el[j]) for j in range(EXPERTS_PER_GROUP)])
    t = jnp.exp(v2 - v1)
    den = (1.0 + t) * gsum
    wts = [1.0 / den, t / den]
    eid = [gidx * EXPERTS_PER_GROUP + i1, gidx * EXPERTS_PER_GROUP + i2]

    esub = lax.broadcasted_iota(jnp.int32, (N_EXPERTS, tt), 0)
    ohf = jnp.concatenate([jnp.where(esub == eid[k], 1.0, 0.0) for k in range(2)], axis=1)
    res = _dot(ohf.astype(BF16), tri_ref[...])
    prefix, cnt = res[:, :2 * tt], res[:, 2 * tt:]
    cnt = (((cnt.astype(jnp.int32) + (SUBLANES - 1)) // SUBLANES) * SUBLANES).astype(F32)
    esub_c = lax.broadcasted_iota(jnp.int32, cnt.shape, 0)
    start = jnp.zeros_like(cnt)
    for e in range(N_EXPERTS - 1):
        start = start + jnp.where(esub_c > e, cnt[e:e + 1, :], 0.0)
    start_w = jnp.concatenate([start] * (2 * tt // LANES), axis=1)
    pos = jnp.sum(ohf * (start_w + prefix), axis=0, keepdims=True)
    pos_k = [pos[:, :tt], pos[:, tt:]]

    n_rows = xs_ref.shape[0]
    psub = lax.broadcasted_iota(jnp.int32, (n_rows, tt), 0).astype(F32)
    pm = [jnp.where(psub == pos_k[k], 1.0, 0.0).astype(BF16) for k in range(2)]
    xs = _dot(pm[0] + pm[1], h_ref[...])
    wsub = lax.broadcasted_iota(jnp.int32, (LANES, tt), 0)
    ws = jnp.zeros((n_rows, LANES), F32)
    for k in range(2):
        parts = _split3(wts[k])
        wrows = jnp.zeros((LANES, tt), F32)
        for j in range(3):
            wrows = jnp.where(wsub == j, parts[j].astype(F32), wrows)
        ws = ws + _dot_nt(pm[k], wrows.astype(BF16))
    half = D_MODEL // 2
    xs_ref[:, :half] = _pack_bf16_pair(xs[:, :half], xs[:, half:])
    xs_ref[:, half:] = pltpu.bitcast(ws, jnp.uint32)
    cnt_ref[0] = cnt.astype(jnp.int32)
    posr = jnp.where(wsub == 0, pos_k[0], jnp.where(wsub == 1, pos_k[1], 0.0))
    pos_ref[...] = posr.T


def _dispatch(logits, h2):
    T = logits.shape[0]
    tt = MOE_TILE
    n_tiles = T // tt
    tri = np.concatenate([np.triu(np.ones((2 * tt, 2 * tt)), 1), np.ones((2 * tt, LANES))], axis=1)
    return pl.pallas_call(
        _dispatch_kernel,
        grid=(n_tiles,),
        in_specs=[pl.BlockSpec((tt, LANES), lambda i: (i, 0)),
                  pl.BlockSpec((tt, D_MODEL), lambda i: (i, 0)),
                  pl.BlockSpec(tri.shape, lambda i: (0, 0))],
        out_specs=[pl.BlockSpec((TILE_ROWS, XS_WIDTH), lambda i: (i, 0)),
                   pl.BlockSpec((1, N_EXPERTS, LANES), lambda i: (i, 0, 0)),
                   pl.BlockSpec((tt, LANES), lambda i: (i, 0))],
        out_shape=[jax.ShapeDtypeStruct((n_tiles * TILE_ROWS, XS_WIDTH), jnp.uint32),
                   jax.ShapeDtypeStruct((n_tiles, N_EXPERTS, LANES), jnp.int32),
                   jax.ShapeDtypeStruct((T, LANES), F32)],
        compiler_params=pltpu.CompilerParams(vmem_limit_bytes=VMEM_LIMIT),
        name="moe_dispatch",
    )(logits, h2, jnp.asarray(tri, BF16))


def _plan_kernel(cnt_ref, be_ref, k0_ref, nv_ref, t0_ref, t1_ref, off_ref, cs_ref):
    n_tiles = cnt_ref.shape[0]
    n_blk = be_ref.shape[0]
    rows = MOE_ROWS

    def per_expert(e, b):
        def per_tile(t, tot):
            off_ref[t * N_EXPERTS + e] = tot
            return tot + cnt_ref[t, e]
        tot = lax.fori_loop(0, n_tiles, per_tile, 0)

        def per_block(j, c):
            be_ref[b + j] = e
            k0_ref[b + j] = j * rows
            nv_ref[b + j] = jnp.minimum(rows, tot - j * rows)
            return c
        nb = (tot + rows - 1) // rows
        lax.fori_loop(0, nb, per_block, 0)
        return b + nb
    n_used = lax.fori_loop(0, N_EXPERTS, per_expert, 0)

    def unused(b, c):
        be_ref[b] = be_ref[n_used - 1]
        k0_ref[b] = 0
        nv_ref[b] = 0
        t0_ref[b] = 0
        t1_ref[b] = 0
        return c
    lax.fori_loop(n_used, n_blk, unused, 0)

    def tile_starts(t, c):
        def per_e(e, acc):
            cs_ref[t * N_EXPERTS + e] = acc
            return acc + cnt_ref[t, e]
        lax.fori_loop(0, N_EXPERTS, per_e, 0)
        return c
    lax.fori_loop(0, n_tiles, tile_starts, 0)

    def tile_range(b, c):
        e = be_ref[b]
        k0 = k0_ref[b]
        k1 = k0 + nv_ref[b]

        def scan(t, st):
            lo, hi = st
            o = off_ref[t * N_EXPERTS + e]
            inter = (o < k1) & (o + cnt_ref[t, e] > k0)
            return jnp.where(inter, jnp.minimum(lo, t), lo), jnp.where(inter, t + 1, hi)
        lo, hi = lax.fori_loop(0, n_tiles, scan, (n_tiles, 0))
        t0_ref[b] = jnp.minimum(lo, hi)
        t1_ref[b] = hi
        return c
    lax.fori_loop(0, n_used, tile_range, 0)


def _plan(cnt, n_blk):
    n_tiles = cnt.shape[0]
    smem = pl.BlockSpec(memory_space=pltpu.SMEM)
    blk = jax.ShapeDtypeStruct((n_blk,), jnp.int32)
    run = jax.ShapeDtypeStruct((n_tiles * N_EXPERTS,), jnp.int32)
    return pl.pallas_call(
        _plan_kernel,
        in_specs=[smem],
        out_specs=[smem] * 7,
        out_shape=[blk] * 5 + [run] * 2,
        name="moe_plan",
    )(cnt)


def _pow2_pieces(n, fn):
    for b in reversed(range(SUBLANES.bit_length() - 1, MOE_ROWS.bit_length())):
        size = 1 << b

        @pl.when((n & size) != 0)
        def _():
            fn((n >> (b + 1)) << (b + 1), size)


def _moe_kernel(be_ref, k0_ref, nv_ref, t0_ref, t1_ref, off_ref, cs_ref, cnt_ref,
                w1_ref, w3_ref, w2_ref, xs_hbm, ys_hbm, xbuf, ybuf, wb1, wb3, wb2, gsem, ssem):
    i = pl.program_id(0)
    last = pl.num_programs(0) - 1
    slot = i % 2
    nv = nv_ref[i]
    half = D_MODEL // 2

    def for_pieces(blk, fn):
        e = be_ref[blk]
        k0 = k0_ref[blk]
        k1 = k0 + nv_ref[blk]

        def per_tile(t, c):
            o = off_ref[t * N_EXPERTS + e]
            lo = jnp.maximum(o, k0)
            n = jnp.maximum(jnp.minimum(o + cnt_ref[t * N_EXPERTS + e], k1) - lo, 0)
            src = t * TILE_ROWS + cs_ref[t * N_EXPERTS + e] + (lo - o)
            _pow2_pieces(n, lambda a, size: fn(pl.multiple_of(lo - k0 + a, SUBLANES),
                                               pl.multiple_of(src + a, SUBLANES), size))
            return c
        lax.fori_loop(t0_ref[blk], t1_ref[blk], per_tile, 0)

    def gather(blk, s):
        for_pieces(blk, lambda r, g, size: pltpu.make_async_copy(
            xs_hbm.at[pl.ds(g, size)], xbuf.at[s, pl.ds(r, size)], gsem.at[s]).start())

    def scatter(blk, s):
        for_pieces(blk, lambda r, g, size: pltpu.make_async_copy(
            ybuf.at[s, pl.ds(r, size)], ys_hbm.at[pl.ds(g, size)], ssem.at[s]).start())

    def wait_gather(s, count):
        _pow2_pieces(count, lambda a, size: pltpu.make_async_copy(
            xs_hbm.at[pl.ds(0, size)], xbuf.at[s, pl.ds(0, size)], gsem.at[s]).wait())

    def wait_scatter(s, count):
        _pow2_pieces(count, lambda a, size: pltpu.make_async_copy(
            ybuf.at[s, pl.ds(0, size)], ys_hbm.at[pl.ds(0, size)], ssem.at[s]).wait())

    @pl.when(i == 0)
    def _():
        xbuf[...] = jnp.zeros(xbuf.shape, xbuf.dtype)

        @pl.when(nv > 0)
        def _():
            gather(0, 0)

        ybuf[1] = jnp.zeros(ybuf.shape[1:], ybuf.dtype)
        n_tiles = cnt_ref.shape[0] // N_EXPERTS

        def tail(t):
            used = cs_ref[t * N_EXPERTS + N_EXPERTS - 1] + cnt_ref[t * N_EXPERTS + N_EXPERTS - 1]
            return t * TILE_ROWS + used, TILE_ROWS - used

        def fill(t, c):
            row0, n = tail(t)
            _pow2_pieces(n, lambda a, size: pltpu.make_async_copy(
                ybuf.at[1, pl.ds(0, size)], ys_hbm.at[pl.ds(pl.multiple_of(row0 + a, SUBLANES), size)],
                ssem.at[1]).start())
            return c
        lax.fori_loop(0, n_tiles, fill, 0)

        def drain(t, c):
            wait_scatter(1, tail(t)[1])
            return c
        lax.fori_loop(0, n_tiles, drain, 0)

    nxt = jnp.minimum(i + 1, last)

    @pl.when((i < last) & (nv_ref[nxt] > 0))
    def _():
        gather(nxt, 1 - slot)

    @pl.when(i >= 2)
    def _():
        wait_scatter(slot, nv_ref[jnp.maximum(i - 2, 0)])

    @pl.when(nv > 0)
    def _():
        e = be_ref[i]
        e_prev = be_ref[jnp.maximum(i - 1, 0)]

        @pl.when((i == 0) | (e != e_prev))
        def _():
            wb1[...] = w1_ref[0].astype(BF16)
            wb3[...] = w3_ref[0].astype(BF16)
            wb2[...] = w2_ref[0].astype(BF16)

        wait_gather(slot, nv)
        u = xbuf[slot]
        xa, xb = _unpack_bf16_pair(u[:, :half])
        wv = pltpu.bitcast(u[:, half:], F32)
        roww = wv[:, 0:1] + wv[:, 1:2] + wv[:, 2:3]
        a = _dot(xa, wb1[:half, :]) + _dot(xb, wb1[half:, :])
        b = _dot(xa, wb3[:half, :]) + _dot(xb, wb3[half:, :])
        hmid = (a * jax.nn.sigmoid(a) * b).astype(BF16)
        y = _dot(hmid, wb2[...]) * roww
        ybuf[slot] = _pack_bf16_pair(y[:, :half], y[:, half:])
        scatter(i, slot)

    @pl.when(i == last)
    def _():
        @pl.when(last >= 1)
        def _():
            wait_scatter(1 - slot, nv_ref[jnp.maximum(last - 1, 0)])
        wait_scatter(slot, nv)


def _moe(xs, plan, cnt_flat, w1, w3, w2):
    n_blk = plan[0].shape[0]
    rows = MOE_ROWS
    half = D_MODEL // 2
    wspec = lambda shape: pl.BlockSpec((1,) + shape, lambda i, be, *_: (be[i], 0, 0))
    grid_spec = pltpu.PrefetchScalarGridSpec(
        num_scalar_prefetch=8,
        grid=(n_blk,),
        in_specs=[wspec((D_MODEL, EXPERT_HIDDEN)), wspec((D_MODEL, EXPERT_HIDDEN)),
                  wspec((EXPERT_HIDDEN, D_MODEL)), pl.BlockSpec(memory_space=pl.ANY)],
        out_specs=pl.BlockSpec(memory_space=pl.ANY),
        scratch_shapes=[pltpu.VMEM((2, rows, XS_WIDTH), jnp.uint32),
                        pltpu.VMEM((2, rows, half), jnp.uint32),
                        pltpu.VMEM((D_MODEL, EXPERT_HIDDEN), BF16),
                        pltpu.VMEM((D_MODEL, EXPERT_HIDDEN), BF16),
                        pltpu.VMEM((EXPERT_HIDDEN, D_MODEL), BF16),
                        pltpu.SemaphoreType.DMA((2,)),
                        pltpu.SemaphoreType.DMA((2,))])
    return pl.pallas_call(
        _moe_kernel,
        grid_spec=grid_spec,
        out_shape=jax.ShapeDtypeStruct((xs.shape[0], half), jnp.uint32),
        compiler_params=pltpu.CompilerParams(dimension_semantics=("arbitrary",),
                                             vmem_limit_bytes=VMEM_LIMIT),
        name="moe_ffn",
    )(*plan, cnt_flat, w1, w3, w2, xs)


def _combine_kernel(x1_ref, ys_ref, pos_ref, g2_ref, o_ref):
    tt = x1_ref.shape[0]
    half = D_MODEL // 2
    pos = pos_ref[...]
    pcol = lax.broadcasted_iota(jnp.int32, (tt, ys_ref.shape[0]), 1).astype(F32)
    sel = (jnp.where(pcol == pos[:, 0:1], 1.0, 0.0) + jnp.where(pcol == pos[:, 1:2], 1.0, 0.0))
    sel = sel.astype(BF16)
    lo, hi = _unpack_bf16_pair(ys_ref[...])
    g2 = g2_ref[0]
    x1 = x1_ref[...]
    o_ref[:, :half] = x1[:, :half] + g2[:, :half] * _dot(sel, lo)
    o_ref[:, half:] = x1[:, half:] + g2[:, half:] * _dot(sel, hi)


def _combine(x1, ys, pos, g2, S):
    T = x1.shape[0]
    tt = MOE_TILE
    per_b = S // tt
    return pl.pallas_call(
        _combine_kernel,
        grid=(T // tt,),
        in_specs=[pl.BlockSpec((tt, D_MODEL), lambda i: (i, 0)),
                  pl.BlockSpec((TILE_ROWS, D_MODEL // 2), lambda i: (i, 0)),
                  pl.BlockSpec((tt, LANES), lambda i: (i, 0)),
                  pl.BlockSpec((1, 1, D_MODEL), lambda i: (i // per_b, 0, 0))],
        out_specs=pl.BlockSpec((tt, D_MODEL), lambda i: (i, 0)),
        out_shape=jax.ShapeDtypeStruct((T, D_MODEL), F32),
        compiler_params=pltpu.CompilerParams(vmem_limit_bytes=VMEM_LIMIT),
        name="moe_combine",
    )(x1, ys, pos, g2)


def _prep_w_in(w_in):
    dq = w_in[:, OFF_DIL_Q:OFF_DIL_K]
    dk = w_in[:, OFF_DIL_K:OFF_DIL_V]
    dv = w_in[:, OFF_DIL_V:OFF_GATE_A]
    dil = []
    for g in range(N_DIL_GROUPS):
        cs = slice(g * DIL_OUT_WIDTH, (g + 1) * DIL_OUT_WIDTH)
        dil += [dq[:, cs], dk[:, cs], dv[:, cs]]
    pad = jnp.zeros((D_MODEL, UNIT - FOX_HEADS), w_in.dtype)
    cols = [w_in[:, OFF_GATE_A:OFF_GATE_B], w_in[:, OFF_GATE_B:N_IN],
            w_in[:, OFF_FOX_Q:OFF_FOX_K], w_in[:, OFF_FOX_K:OFF_FOX_V], w_in[:, OFF_FOX_V:OFF_FOX_F],
            *dil, w_in[:, OFF_FOX_F:OFF_DIL_Q], pad]
    return jnp.concatenate(cols, axis=1).astype(BF16)


def _prep_gain(q_gain, k_gain):
    qs = HEAD_DIM ** -0.5 * LOG2E
    ones = jnp.ones((UNIT,), F32)
    fq = q_gain[:FOX_HEADS].reshape(-1) * qs
    fk = k_gain[:FOX_HEADS].reshape(-1)
    dq = q_gain[FOX_HEADS:].reshape(-1) * qs
    dk = k_gain[FOX_HEADS:].reshape(-1)
    dil = []
    for g in range(N_DIL_GROUPS):
        cs = slice(g * DIL_OUT_WIDTH, (g + 1) * DIL_OUT_WIDTH)
        dil += [dq[cs], dk[cs], ones]
    parts = [ones] * 8 + [fq, fk, ones, ones] + dil + [ones]
    return jnp.concatenate(parts).reshape(1, N_UNITS * UNIT)


def _layer(x, mod, rel_bias_table, norm1_g, w_in, b_forget, q_gain, k_gain, w_branch_a, w_branch_b,
           w_out, norm2_g, w_rg, b_rg, w_re, b_re, w1, w3, w2):
    B, S, D = x.shape
    T = B * S
    sh1, sc1, g1, sh2, sc2, g2 = [m.reshape(B, 1, D) for m in jnp.split(mod, 6, axis=-1)]
    x2d = x.reshape(T, D)

    p2d, fgt, slabs = _inproj(x2d, norm1_g.reshape(1, D), sc1, sh1, _prep_w_in(w_in),
                              _prep_gain(q_gain, k_gain), S)
    p3 = p2d.reshape(B, S, P_WIDTH)
    ck = _fcum(fgt.reshape(B, S, LANES), b_forget)
    ya = _fox(p3, ck)
    yd = _dil(slabs, _relbias(rel_bias_table))

    n_router = N_GROUPS + N_EXPERTS
    wr = jnp.concatenate([w_rg, w_re, jnp.zeros((D, LANES - n_router), F32)], axis=1).astype(BF16)
    br = jnp.concatenate([b_rg, b_re, jnp.zeros((LANES - n_router,), F32)]).reshape(1, LANES)
    x1, h2, logits = _outproj(x2d, ya.reshape(T, FOX_WIDTH), yd.reshape(T, DIL_OUT_WIDTH), p2d,
                              g1, sc2, sh2, norm2_g.reshape(1, D),
                              w_branch_a.astype(BF16), w_branch_b.astype(BF16), w_out.astype(BF16),
                              wr, br, S)
    xs, cnt, pos = _dispatch(logits, h2)
    cnt2 = cnt[:, :, 0]
    n_blk = cnt.shape[0] * TILE_ROWS // MOE_ROWS + N_EXPERTS
    plan = _plan(cnt2, n_blk)
    ys = _moe(xs, plan, cnt2.reshape(-1), w1, w3, w2)
    out = _combine(x1, ys, pos, g2, S)
    return out.reshape(B, S, D)


def kernel(x, c, rel_bias_table, w_ada, b_ada, norm1_g, w_in, b_forget, q_gain, k_gain, w_branch_a, w_branch_b, w_out, norm2_g, w_router_group, b_router_group, w_router_expert, b_router_expert, w1, w3, w2):
    depth = w_ada.shape[0]
    for l in range(depth):
        mod = _ada(c, w_ada[l], b_ada[l])
        x = _layer(x, mod, rel_bias_table, norm1_g[l], w_in[l], b_forget[l], q_gain[l], k_gain[l],
                   w_branch_a[l], w_branch_b[l], w_out[l], norm2_g[l], w_router_group[l],
                   b_router_group[l], w_router_expert[l], b_router_expert[l], w1[l], w3[l], w2[l])
    return x
```

```python
import functools
import math

import numpy as np
import jax
import jax.numpy as jnp
from jax import lax
from jax.experimental import pallas as pl
from jax.experimental.pallas import tpu as pltpu

F32 = jnp.float32
BF16 = jnp.bfloat16

D_MODEL = 1024
HEAD_DIM = 64
FOX_HEADS = 8
DIL_GROUPS = ((128, 1), (512, 4), (2048, 16))
DIL_HEADS_PER_GROUP = 4
N_DIL_GROUPS = len(DIL_GROUPS)
DIL_HEADS = N_DIL_GROUPS * DIL_HEADS_PER_GROUP
FOX_WIDTH = FOX_HEADS * HEAD_DIM
DIL_WIDTH = DIL_HEADS * HEAD_DIM
DIL_OUT_WIDTH = DIL_HEADS_PER_GROUP * HEAD_DIM
NUM_BUCKETS = 32
REL_MAX_DISTANCE = 2048
N_GROUPS = 4
EXPERTS_PER_GROUP = 8
N_EXPERTS = N_GROUPS * EXPERTS_PER_GROUP
EXPERT_HIDDEN = D_MODEL // 2
EPS = 1e-6
LOG2E = math.log2(math.e)

OFF_FOX_Q = 0
OFF_FOX_K = OFF_FOX_Q + FOX_WIDTH
OFF_FOX_V = OFF_FOX_K + FOX_WIDTH
OFF_FOX_F = OFF_FOX_V + FOX_WIDTH
OFF_DIL_Q = OFF_FOX_F + FOX_HEADS
OFF_DIL_K = OFF_DIL_Q + DIL_WIDTH
OFF_DIL_V = OFF_DIL_K + DIL_WIDTH
OFF_GATE_A = OFF_DIL_V + DIL_WIDTH
OFF_GATE_B = OFF_GATE_A + D_MODEL
N_IN = OFF_GATE_B + D_MODEL

LANES = 128
UNIT = 256
DIL_L = 128

U_GATE_A, U_GATE_B, U_FOX_Q, U_FOX_K, U_FOX_V, U_DIL, U_FORGET = 0, 4, 8, 10, 12, 14, 23
N_UNITS = 24
P_WIDTH = U_DIL * UNIT
N_SLABS = 3 * N_DIL_GROUPS
_KIND = (["gate"] * 8 + ["normq"] * 2 + ["normk"] * 2 + ["plain"] * 2
         + ["normq", "normk", "plain"] * 3 + ["forget"])

TM_PROJ = 512
TQ_FOX = 256
MOE_ROWS = 256
MOE_TILE = 512
XS_WIDTH = D_MODEL // 2 + LANES
SUBLANES = 8
TILE_ROWS = 2 * MOE_TILE + N_EXPERTS * SUBLANES
VMEM_LIMIT = 56 * 1024 * 1024


def _dot(a, b):
    return jnp.dot(a, b, preferred_element_type=F32)


def _dot_nt(a, b):
    return lax.dot_general(a, b, (((1,), (1,)), ((), ())), preferred_element_type=F32)


def _split3(x):
    hi = x.astype(BF16)
    r1 = x - hi.astype(F32)
    mid = r1.astype(BF16)
    lo = (r1 - mid.astype(F32)).astype(BF16)
    return hi, mid, lo


def _ada_kernel(c_ref, w_ref, b_ref, o_ref):
    c = c_ref[...]
    s = c * jax.nn.sigmoid(c)
    s_hi = s.astype(BF16)
    s_lo = (s - s_hi.astype(F32)).astype(BF16)
    w = w_ref[...]
    w_hi = w.astype(BF16)
    w_lo = (w - w_hi.astype(F32)).astype(BF16)
    acc = _dot(s_hi, w_hi) + _dot(s_hi, w_lo) + _dot(s_lo, w_hi)
    o_ref[...] = acc + b_ref[...]


def _ada(c, w_ada, b_ada):
    B = c.shape[0]
    n_out = w_ada.shape[1]
    tn = 512
    return pl.pallas_call(
        _ada_kernel,
        grid=(n_out // tn,),
        in_specs=[pl.BlockSpec((B, D_MODEL), lambda j: (0, 0)),
                  pl.BlockSpec((D_MODEL, tn), lambda j: (0, j)),
                  pl.BlockSpec((1, tn), lambda j: (0, j))],
        out_specs=pl.BlockSpec((B, tn), lambda j: (0, j)),
        out_shape=jax.ShapeDtypeStruct((B, n_out), F32),
        name="ada_mod",
    )(c, w_ada, b_ada.reshape(1, n_out))


def _pack_bf16_pair(lo, hi):
    lo_bits = pltpu.bitcast(lo.astype(BF16).astype(F32), jnp.uint32) >> 16
    hi_bits = pltpu.bitcast(hi.astype(BF16).astype(F32), jnp.uint32) & jnp.uint32(0xFFFF0000)
    return lo_bits | hi_bits


def _unpack_bf16_pair(u):
    lo = pltpu.bitcast(u << 16, F32).astype(BF16)
    hi = pltpu.bitcast(u & jnp.uint32(0xFFFF0000), F32).astype(BF16)
    return lo, hi


def _inproj_kernel(x_ref, g_ref, sc_ref, sh_ref, w_ref, gain_ref, bd_ref, p_ref, f_ref, s_ref):
    x = x_ref[...]
    ms = jnp.mean(x * x, axis=-1, keepdims=True)
    h = x * lax.rsqrt(ms + EPS) * g_ref[...]
    h = h * (1.0 + sc_ref[0]) + sh_ref[0]
    hb = h.astype(BF16)

    def unit(u):
        cols = slice(u * UNIT, (u + 1) * UNIT)
        acc = _dot(hb, w_ref[:, cols])
        kind = _KIND[u]
        if kind == "gate":
            return jax.nn.sigmoid(acc)
        if kind in ("normq", "normk"):
            ss = _dot((acc * acc).astype(BF16), bd_ref[...])
            return acc * lax.rsqrt(ss * (1.0 / HEAD_DIM) + EPS) * gain_ref[:, cols]
        return acc

    for u in range(U_DIL):
        p_ref[:, u * UNIT:(u + 1) * UNIT] = unit(u).astype(BF16)
    for g in range(N_DIL_GROUPS):
        q, k, v = (unit(U_DIL + 3 * g + j) for j in range(3))
        s_ref[0, 3 * g] = _pack_bf16_pair(q[:, :LANES], k[:, :LANES])
        s_ref[0, 3 * g + 1] = _pack_bf16_pair(q[:, LANES:], k[:, LANES:])
        s_ref[0, 3 * g + 2] = _pack_bf16_pair(v[:, :LANES], v[:, LANES:])
    f_ref[...] = unit(U_FORGET)[:, :LANES]


def _inproj(x2d, norm_g, sc, sh, w_re, gain_row, S):
    T = x2d.shape[0]
    tm = TM_PROJ
    per_b = S // tm
    bd = np.kron(np.eye(UNIT // HEAD_DIM), np.ones((HEAD_DIM, HEAD_DIM))).astype(np.float32)
    return pl.pallas_call(
        _inproj_kernel,
        grid=(T // tm,),
        in_specs=[pl.BlockSpec((tm, D_MODEL), lambda i: (i, 0)),
                  pl.BlockSpec((1, D_MODEL), lambda i: (0, 0)),
                  pl.BlockSpec((1, 1, D_MODEL), lambda i: (i // per_b, 0, 0)),
                  pl.BlockSpec((1, 1, D_MODEL), lambda i: (i // per_b, 0, 0)),
                  pl.BlockSpec((D_MODEL, N_UNITS * UNIT), lambda i: (0, 0)),
                  pl.BlockSpec((1, N_UNITS * UNIT), lambda i: (0, 0)),
                  pl.BlockSpec((UNIT, UNIT), lambda i: (0, 0))],
        out_specs=[pl.BlockSpec((tm, P_WIDTH), lambda i: (i, 0)),
                   pl.BlockSpec((tm, LANES), lambda i: (i, 0)),
                   pl.BlockSpec((1, N_SLABS, tm, LANES), lambda i: (i // per_b, 0, i % per_b, 0))],
        out_shape=[jax.ShapeDtypeStruct((T, P_WIDTH), BF16),
                   jax.ShapeDtypeStruct((T, LANES), F32),
                   jax.ShapeDtypeStruct((T // S, N_SLABS, S, LANES), jnp.uint32)],
        compiler_params=pltpu.CompilerParams(vmem_limit_bytes=VMEM_LIMIT),
        name="in_proj",
    )(x2d, norm_g, sc, sh, w_re, gain_row, jnp.asarray(bd, BF16))


def _fcum_kernel(f_ref, b_ref, tri_ref, o_ref):
    S = f_ref.shape[1]
    xf = f_ref[0] + b_ref[...]
    ls = (jnp.minimum(xf, 0.0) - jnp.log(1.0 + jnp.exp(-jnp.abs(xf)))) * LOG2E
    lst = ls.T
    carry = jnp.zeros((LANES, UNIT), F32)
    for blk in range(S // UNIT):
        seg = lst[:, blk * UNIT:(blk + 1) * UNIT]
        hi, mid, lo = _split3(seg)
        tri = tri_ref[...]
        res = _dot(hi, tri) + _dot(mid, tri) + _dot(lo, tri)
        o_ref[0, :, blk * UNIT:(blk + 1) * UNIT] = (res[:, :UNIT] + carry)[:FOX_HEADS]
        carry = carry + res[:, UNIT:]


def _fcum(fgt, b_forget):
    B, S, _ = fgt.shape
    brow = jnp.zeros((1, LANES), F32).at[0, :FOX_HEADS].set(b_forget)
    tri = np.concatenate([np.triu(np.ones((UNIT, UNIT))), np.ones((UNIT, UNIT))], axis=1)
    return pl.pallas_call(
        _fcum_kernel,
        grid=(B,),
        in_specs=[pl.BlockSpec((1, S, LANES), lambda b: (b, 0, 0)),
                  pl.BlockSpec((1, LANES), lambda b: (0, 0)),
                  pl.BlockSpec((UNIT, 2 * UNIT), lambda b: (0, 0))],
        out_specs=pl.BlockSpec((1, FOX_HEADS, S), lambda b: (b, 0, 0)),
        out_shape=jax.ShapeDtypeStruct((B, FOX_HEADS, S), F32),
        name="forget_cumsum",
    )(fgt, brow, jnp.asarray(tri, BF16))


def _fox_kernel(q_ref, k_ref, v_ref, ck_ref, o_ref):
    S = q_ref.shape[1]
    pair = pl.program_id(1)
    tq = TQ_FOX
    lane = lax.broadcasted_iota(jnp.int32, (1, LANES), 1)
    row = lax.broadcasted_iota(jnp.int32, (tq, tq), 0)
    col = lax.broadcasted_iota(jnp.int32, (tq, tq), 1)
    causal = col <= row
    cks = [ck_ref[0, pl.ds(2 * pair + hh, 1), :] for hh in range(2)]
    for t in range(S // tq):
        r0, r1 = t * tq, (t + 1) * tq
        qt = q_ref[0, r0:r1, :]
        outs = []
        for hh in range(2):
            hsel = (lane >= HEAD_DIM) == bool(hh)
            qm = jnp.where(hsel, qt, jnp.zeros_like(qt))
            ck = cks[hh]
            s_d = _dot_nt(qm, k_ref[0, r0:r1, :]) - ck[:, r0:r1]
            s_d = jnp.where(causal, s_d, -jnp.inf)
            m = jnp.max(s_d, axis=-1, keepdims=True)
            if t > 0:
                s_o = _dot_nt(qm, k_ref[0, :r0, :]) - ck[:, :r0]
                m = jnp.maximum(m, jnp.max(s_o, axis=-1, keepdims=True))
            p_d = jnp.exp2(s_d - m)
            l = jnp.sum(p_d, axis=-1, keepdims=True)
            acc = _dot(p_d.astype(BF16), v_ref[0, r0:r1, :])
            if t > 0:
                p_o = jnp.exp2(s_o - m)
                l = l + jnp.sum(p_o, axis=-1, keepdims=True)
                acc = acc + _dot(p_o.astype(BF16), v_ref[0, :r0, :])
            outs.append(acc / l)
        o_ref[0, r0:r1, :] = jnp.where(lane < HEAD_DIM, outs[0], outs[1]).astype(BF16)


def _fox(p3, ck):
    B, S, _ = p3.shape
    nq, nk, nv = (U_FOX_Q * UNIT // LANES, U_FOX_K * UNIT // LANES, U_FOX_V * UNIT // LANES)
    return pl.pallas_call(
        _fox_kernel,
        grid=(B, FOX_HEADS // 2),
        in_specs=[pl.BlockSpec((1, S, LANES), lambda b, p: (b, 0, nq + p)),
                  pl.BlockSpec((1, S, LANES), lambda b, p: (b, 0, nk + p)),
                  pl.BlockSpec((1, S, LANES), lambda b, p: (b, 0, nv + p)),
                  pl.BlockSpec((1, FOX_HEADS, S), lambda b, p: (b, 0, 0))],
        out_specs=pl.BlockSpec((1, S, LANES), lambda b, p: (b, 0, p)),
        out_shape=jax.ShapeDtypeStruct((B, S, FOX_WIDTH), BF16),
        compiler_params=pltpu.CompilerParams(vmem_limit_bytes=VMEM_LIMIT),
        name="fox_attn",
    )(p3, p3, p3, ck)


def _t5_bucket(dist):
    max_exact = NUM_BUCKETS // 2
    d = np.maximum(dist, 1).astype(np.float32)
    large = max_exact + (np.log(d / max_exact) / np.log(REL_MAX_DISTANCE / max_exact)
                         * (NUM_BUCKETS - max_exact)).astype(np.int32)
    large = np.minimum(large, NUM_BUCKETS - 1)
    return np.where(dist < max_exact, dist, large).astype(np.int32)


def _relbias_kernel(tab_ref, bucket_ref, valid_ref, o_ref):
    g = pl.program_id(0)
    bk = bucket_ref[0]
    vd = valid_ref[0]
    for hs in range(DIL_HEADS_PER_GROUP):
        acc = jnp.zeros(bk.shape, F32)
        for b in range(NUM_BUCKETS):
            acc = jnp.where(bk == b, tab_ref[b, g * DIL_HEADS_PER_GROUP + hs], acc)
        bias = jnp.where(vd != 0, acc * LOG2E, -jnp.inf)
        o_ref[0, hs] = bias
        col = lax.broadcasted_iota(jnp.int32, bias.shape, 1)
        o_ref[1, hs] = jnp.where(col >= DIL_L, bias, -jnp.inf)


def _relbias(table):
    L = DIL_L
    i = np.arange(L)[:, None]
    j = np.arange(2 * L)[None, :]
    m = L + i - j
    valid = ((m >= 0) & (m <= L)).astype(np.int32)
    buckets = np.stack([_t5_bucket(np.clip(m, 0, None) * d) for _, d in DIL_GROUPS])
    valids = np.stack([valid] * N_DIL_GROUPS)
    return pl.pallas_call(
        _relbias_kernel,
        grid=(N_DIL_GROUPS,),
        in_specs=[pl.BlockSpec(memory_space=pltpu.SMEM),
                  pl.BlockSpec((1, L, 2 * L), lambda g: (g, 0, 0)),
                  pl.BlockSpec((1, L, 2 * L), lambda g: (g, 0, 0))],
        out_specs=pl.BlockSpec((2, DIL_HEADS_PER_GROUP, L, 2 * L), lambda g: (0, g, 0, 0)),
        out_shape=jax.ShapeDtypeStruct((2, DIL_HEADS, L, 2 * L), F32),
        name="rel_bias",
    )(table, jnp.asarray(buckets), jnp.asarray(valids))


def _dil_rows(start, d):
    return pl.ds(start, DIL_L) if d == 1 else pl.ds(start, DIL_L, stride=d)


def _dil_block(qkv_ref, bias_ref, m_scr, l_scr, acc_scr, g, d, r, n):
    L = DIL_L
    lane = lax.broadcasted_iota(jnp.int32, (1, LANES), 1)
    first = 1 - jnp.minimum(n, 1)
    cur = _dil_rows(r + d * (n * L), d)
    prev = _dil_rows(r + d * (jnp.maximum(n - 1, 0) * L), d)
    v_cur = _unpack_bf16_pair(qkv_ref[0, 3 * g + 2, cur, :])
    v_prev = _unpack_bf16_pair(qkv_ref[0, 3 * g + 2, prev, :])
    for pr in range(2):
        qt, k_cur = _unpack_bf16_pair(qkv_ref[0, 3 * g + pr, cur, :])
        _, k_prev = _unpack_bf16_pair(qkv_ref[0, 3 * g + pr, prev, :])
        kt = jnp.concatenate([k_prev, k_cur], axis=0)
        vt = jnp.concatenate([v_prev[pr], v_cur[pr]], axis=0)
        ms, ls, accs = [], [], []
        for hh in range(2):
            hsel = (lane >= HEAD_DIM) == bool(hh)
            qm = jnp.where(hsel, qt, jnp.zeros_like(qt))
            s = _dot_nt(qm, kt) + bias_ref[first, DIL_HEADS_PER_GROUP * g + 2 * pr + hh]
            m = jnp.max(s, axis=-1, keepdims=True)
            p = jnp.exp2(s - m)
            ms.append(m)
            ls.append(jnp.sum(p, axis=-1, keepdims=True))
            accs.append(_dot(p.astype(BF16), vt))
        low = lane < HEAD_DIM
        m_b = jnp.where(low, ms[0], ms[1])
        l_b = jnp.where(low, ls[0], ls[1])
        acc_b = jnp.where(low, accs[0], accs[1])
        if g == 0:
            m_scr[pr, cur, :] = m_b
            l_scr[pr, cur, :] = l_b
            acc_scr[pr, cur, :] = acc_b
        else:
            m_o = m_scr[pr, cur, :]
            m_n = jnp.maximum(m_o, m_b)
            a_o = jnp.exp2(m_o - m_n)
            a_b = jnp.exp2(m_b - m_n)
            m_scr[pr, cur, :] = m_n
            l_scr[pr, cur, :] = l_scr[pr, cur, :] * a_o + l_b * a_b
            acc_scr[pr, cur, :] = acc_scr[pr, cur, :] * a_o + acc_b * a_b


def _dil_kernel(qkv_ref, bias_ref, o_ref, m_scr, l_scr, acc_scr):
    S = o_ref.shape[1]
    for g, (window, d) in enumerate(DIL_GROUPS):
        nb = S // window

        def body(it, carry, g=g, d=d, nb=nb):
            _dil_block(qkv_ref, bias_ref, m_scr, l_scr, acc_scr, g, d, it // nb, it % nb)
            return carry
        lax.fori_loop(0, d * nb, body, 0, unroll=4)
    for pr in range(2):
        o_ref[0, :, pr * LANES:(pr + 1) * LANES] = (acc_scr[pr] / l_scr[pr]).astype(BF16)


def _dil(slabs, bias):
    B, _, S, _ = slabs.shape
    for window, d in DIL_GROUPS:
        assert window // d == DIL_L and S % window == 0
    stat = pltpu.VMEM((2, S, LANES), F32)
    return pl.pallas_call(
        _dil_kernel,
        grid=(B,),
        in_specs=[pl.BlockSpec((1, N_SLABS, S, LANES), lambda b: (b, 0, 0, 0)),
                  pl.BlockSpec(bias.shape, lambda b: (0, 0, 0, 0))],
        out_specs=pl.BlockSpec((1, S, DIL_OUT_WIDTH), lambda b: (b, 0, 0)),
        out_shape=jax.ShapeDtypeStruct((B, S, DIL_OUT_WIDTH), BF16),
        scratch_shapes=[stat, stat, stat],
        compiler_params=pltpu.CompilerParams(vmem_limit_bytes=VMEM_LIMIT),
        name="dil_attn",
    )(slabs, bias)


def _outproj_kernel(x_ref, ya_ref, yd_ref, ga_ref, gb_ref,
                    g1_ref, sc_ref, sh_ref, ng_ref, wa_ref, wb_ref, wo_ref, wr_ref, br_ref,
                    x1_ref, h2_ref, lg_ref):
    a = _dot(ya_ref[...], wa_ref[...])
    bm = _dot(yd_ref[...], wb_ref[...])
    merged = ga_ref[...].astype(F32) * a + gb_ref[...].astype(F32) * bm
    out = _dot(merged.astype(BF16), wo_ref[...])
    x1 = x_ref[...] + g1_ref[0] * out
    x1_ref[...] = x1
    ms = jnp.mean(x1 * x1, axis=-1, keepdims=True)
    h = x1 * lax.rsqrt(ms + EPS) * ng_ref[...]
    h = h * (1.0 + sc_ref[0]) + sh_ref[0]
    hb = h.astype(BF16)
    h2_ref[...] = hb
    lg_ref[...] = _dot(hb, wr_ref[...]) + br_ref[...]


def _outproj(x2d, ya2d, yd2d, p2d, g1, sc2, sh2, norm_g, wa, wb, wo, wr, br, S):
    T = x2d.shape[0]
    tm = TM_PROJ
    per_b = S // tm
    row = lambda w: pl.BlockSpec((tm, w), lambda i: (i, 0))
    full = lambda a: pl.BlockSpec(a.shape, lambda i: (0,) * a.ndim)
    mod = pl.BlockSpec((1, 1, D_MODEL), lambda i: (i // per_b, 0, 0))
    return pl.pallas_call(
        _outproj_kernel,
        grid=(T // tm,),
        in_specs=[row(D_MODEL), row(FOX_WIDTH), row(DIL_OUT_WIDTH)]
                 + [pl.BlockSpec((tm, D_MODEL), lambda i: (i, U_GATE_A * UNIT // D_MODEL)),
                    pl.BlockSpec((tm, D_MODEL), lambda i: (i, U_GATE_B * UNIT // D_MODEL)),
                    mod, mod, mod, full(norm_g), full(wa), full(wb), full(wo), full(wr), full(br)],
        out_specs=[row(D_MODEL), row(D_MODEL), row(LANES)],
        out_shape=[jax.ShapeDtypeStruct((T, D_MODEL), F32),
                   jax.ShapeDtypeStruct((T, D_MODEL), BF16),
                   jax.ShapeDtypeStruct((T, LANES), F32)],
        compiler_params=pltpu.CompilerParams(vmem_limit_bytes=VMEM_LIMIT),
        name="out_proj",
    )(x2d, ya2d, yd2d, p2d, p2d, g1, sc2, sh2, norm_g, wa, wb, wo, wr, br)


def _dispatch_kernel(lg_ref, h_ref, tri_ref, xs_ref, cnt_ref, pos_ref):
    tt = lg_ref.shape[0]
    lt = lg_ref[...].T
    row = lambda i: lt[i:i + 1, :]
    neg = -jnp.inf
    g = [row(i) for i in range(N_GROUPS)]
    gmax = functools.reduce(jnp.maximum, g)
    gidx = jnp.full(gmax.shape, N_GROUPS - 1, jnp.int32)
    for i in reversed(range(N_GROUPS - 1)):
        gidx = jnp.where(g[i] == gmax, i, gidx)
    gsum = sum(jnp.exp(gi - gmax) for gi in g)
    el = []
    for j in range(EXPERTS_PER_GROUP):
        v = row(N_GROUPS + EXPERTS_PER_GROUP * (N_GROUPS - 1) + j)
        for gg in reversed(range(N_GROUPS - 1)):
            v = jnp.where(gidx == gg, row(N_GROUPS + EXPERTS_PER_GROUP * gg + j), v)
        el.append(v)

    def top(vals):
        best = functools.reduce(jnp.maximum, vals)
        idx = jnp.full(best.shape, EXPERTS_PER_GROUP - 1, jnp.int32)
        for j in reversed(range(EXPERTS_PER_GROUP - 1)):
            idx = jnp.where(vals[j] == best, j, idx)
        return best, idx

    v1, i1 = top(el)
    v2, i2 = top([jnp.where(i1 == j, neg, el[j]) for j in range(EXPERTS_PER_GROUP)])
    t = jnp.exp(v2 - v1)
    den = (1.0 + t) * gsum
    wts = [1.0 / den, t / den]
    eid = [gidx * EXPERTS_PER_GROUP + i1, gidx * EXPERTS_PER_GROUP + i2]

    esub = lax.broadcasted_iota(jnp.int32, (N_EXPERTS, tt), 0)
    ohf = jnp.concatenate([jnp.where(esub == eid[k], 1.0, 0.0) for k in range(2)], axis=1)
    res = _dot(ohf.astype(BF16), tri_ref[...])
    prefix, cnt = res[:, :2 * tt], res[:, 2 * tt:]
    cnt = (((cnt.astype(jnp.int32) + (SUBLANES - 1)) // SUBLANES) * SUBLANES).astype(F32)
    esub_c = lax.broadcasted_iota(jnp.int32, cnt.shape, 0)
    start = jnp.zeros_like(cnt)
    for e in range(N_EXPERTS - 1):
        start = start + jnp.where(esub_c > e, cnt[e:e + 1, :], 0.0)
    start_w = jnp.concatenate([start] * (2 * tt // LANES), axis=1)
    pos = jnp.sum(ohf * (start_w + prefix), axis=0, keepdims=True)
    pos_k = [pos[:, :tt], pos[:, tt:]]

    n_rows = xs_ref.shape[0]
    psub = lax.broadcasted_iota(jnp.int32, (n_rows, tt), 0).astype(F32)
    pm = [jnp.where(psub == pos_k[k], 1.0, 0.0).astype(BF16) for k in range(2)]
    xs = _dot(pm[0] + pm[1], h_ref[...])
    wsub = lax.broadcasted_iota(jnp.int32, (LANES, tt), 0)
    ws = jnp.zeros((n_rows, LANES), F32)
    for k in range(2):
        parts = _split3(wts[k])
        wrows = jnp.zeros((LANES, tt), F32)
        for j in range(3):
            wrows = jnp.where(wsub == j, parts[j].astype(F32), wrows)
        ws = ws + _dot_nt(pm[k], wrows.astype(BF16))
    half = D_MODEL // 2
    xs_ref[:, :half] = _pack_bf16_pair(xs[:, :half], xs[:, half:])
    xs_ref[:, half:] = pltpu.bitcast(ws, jnp.uint32)
    cnt_ref[0] = cnt.astype(jnp.int32)
    posr = jnp.where(wsub == 0, pos_k[0], jnp.where(wsub == 1, pos_k[1], 0.0))
    pos_ref[...] = posr.T


def _dispatch(logits, h2):
    T = logits.shape[0]
    tt = MOE_TILE
    n_tiles = T // tt
    tri = np.concatenate([np.triu(np.ones((2 * tt, 2 * tt)), 1), np.ones((2 * tt, LANES))], axis=1)
    return pl.pallas_call(
        _dispatch_kernel,
        grid=(n_tiles,),
        in_specs=[pl.BlockSpec((tt, LANES), lambda i: (i, 0)),
                  pl.BlockSpec((tt, D_MODEL), lambda i: (i, 0)),
                  pl.BlockSpec(tri.shape, lambda i: (0, 0))],
        out_specs=[pl.BlockSpec((TILE_ROWS, XS_WIDTH), lambda i: (i, 0)),
                   pl.BlockSpec((1, N_EXPERTS, LANES), lambda i: (i, 0, 0)),
                   pl.BlockSpec((tt, LANES), lambda i: (i, 0))],
        out_shape=[jax.ShapeDtypeStruct((n_tiles * TILE_ROWS, XS_WIDTH), jnp.uint32),
                   jax.ShapeDtypeStruct((n_tiles, N_EXPERTS, LANES), jnp.int32),
                   jax.ShapeDtypeStruct((T, LANES), F32)],
        compiler_params=pltpu.CompilerParams(vmem_limit_bytes=VMEM_LIMIT),
        name="moe_dispatch",
    )(logits, h2, jnp.asarray(tri, BF16))


def _plan_kernel(cnt_ref, be_ref, k0_ref, nv_ref, t0_ref, t1_ref, off_ref, cs_ref):
    n_tiles = cnt_ref.shape[0]
    n_blk = be_ref.shape[0]
    rows = MOE_ROWS
    row_shift = rows.bit_length() - 1
    assert rows == 1 << row_shift

    def init(b, c):
        k0_ref[b] = 0
        nv_ref[b] = 0
        t0_ref[b] = n_tiles
        t1_ref[b] = 0
        return c
    lax.fori_loop(0, n_blk, init, 0)

    def per_expert(e, b):
        def per_tile(t, tot):
            c = cnt_ref[t, e]
            off_ref[t * N_EXPERTS + e] = tot

            @pl.when(c > 0)
            def _():
                def touch(j, cc):
                    t0_ref[b + j] = jnp.minimum(t0_ref[b + j], t)
                    t1_ref[b + j] = t + 1
                    return cc
                lax.fori_loop(lax.shift_right_logical(tot, row_shift),
                              lax.shift_right_logical(tot + c - 1, row_shift) + 1, touch, 0)
            return tot + c
        tot = lax.fori_loop(0, n_tiles, per_tile, 0)

        def per_block(j, c):
            be_ref[b + j] = e
            k0_ref[b + j] = j * rows
            nv_ref[b + j] = jnp.minimum(rows, tot - j * rows)
            return c
        nb = lax.shift_right_logical(tot + rows - 1, row_shift)
        lax.fori_loop(0, nb, per_block, 0)
        return b + nb
    n_used = lax.fori_loop(0, N_EXPERTS, per_expert, 0)

    def unused(b, c):
        be_ref[b] = be_ref[n_used - 1]
        return c
    lax.fori_loop(n_used, n_blk, unused, 0)

    def tile_starts(t, c):
        def per_e(e, acc):
            cs_ref[t * N_EXPERTS + e] = acc
            return acc + cnt_ref[t, e]
        lax.fori_loop(0, N_EXPERTS, per_e, 0, unroll=8)
        return c
    lax.fori_loop(0, n_tiles, tile_starts, 0)


def _plan(cnt, n_blk):
    n_tiles = cnt.shape[0]
    smem = pl.BlockSpec(memory_space=pltpu.SMEM)
    blk = jax.ShapeDtypeStruct((n_blk,), jnp.int32)
    run = jax.ShapeDtypeStruct((n_tiles * N_EXPERTS,), jnp.int32)
    return pl.pallas_call(
        _plan_kernel,
        in_specs=[smem],
        out_specs=[smem] * 7,
        out_shape=[blk] * 5 + [run] * 2,
        name="moe_plan",
    )(cnt)


def _pow2_pieces(n, fn):
    for b in reversed(range(SUBLANES.bit_length() - 1, MOE_ROWS.bit_length())):
        size = 1 << b

        @pl.when((n & size) != 0)
        def _():
            fn((n >> (b + 1)) << (b + 1), size)


def _moe_kernel(be_ref, k0_ref, nv_ref, t0_ref, t1_ref, off_ref, cs_ref, cnt_ref,
                w1_ref, w3_ref, w2_ref, xs_hbm, ys_hbm, xbuf, ybuf, wb1, wb3, wb2, gsem, ssem):
    i = pl.program_id(0)
    last = pl.num_programs(0) - 1
    slot = i % 2
    nv = nv_ref[i]
    half = D_MODEL // 2

    def for_pieces(blk, fn):
        e = be_ref[blk]
        k0 = k0_ref[blk]
        k1 = k0 + nv_ref[blk]

        def per_tile(t, c):
            o = off_ref[t * N_EXPERTS + e]
            lo = jnp.maximum(o, k0)
            n = jnp.maximum(jnp.minimum(o + cnt_ref[t * N_EXPERTS + e], k1) - lo, 0)
            src = t * TILE_ROWS + cs_ref[t * N_EXPERTS + e] + (lo - o)
            _pow2_pieces(n, lambda a, size: fn(pl.multiple_of(lo - k0 + a, SUBLANES),
                                               pl.multiple_of(src + a, SUBLANES), size))
            return c
        lax.fori_loop(t0_ref[blk], t1_ref[blk], per_tile, 0)

    def gather(blk, s):
        for_pieces(blk, lambda r, g, size: pltpu.make_async_copy(
            xs_hbm.at[pl.ds(g, size)], xbuf.at[s, pl.ds(r, size)], gsem.at[s]).start())

    def scatter(blk, s):
        for_pieces(blk, lambda r, g, size: pltpu.make_async_copy(
            ybuf.at[s, pl.ds(r, size)], ys_hbm.at[pl.ds(g, size)], ssem.at[s]).start())

    def wait_gather(s, count):
        _pow2_pieces(count, lambda a, size: pltpu.make_async_copy(
            xs_hbm.at[pl.ds(0, size)], xbuf.at[s, pl.ds(0, size)], gsem.at[s]).wait())

    def wait_scatter(s, count):
        _pow2_pieces(count, lambda a, size: pltpu.make_async_copy(
            ybuf.at[s, pl.ds(0, size)], ys_hbm.at[pl.ds(0, size)], ssem.at[s]).wait())

    @pl.when(i == 0)
    def _():
        xbuf[...] = jnp.zeros(xbuf.shape, xbuf.dtype)

        @pl.when(nv > 0)
        def _():
            gather(0, 0)

        ybuf[1] = jnp.zeros(ybuf.shape[1:], ybuf.dtype)
        n_tiles = cnt_ref.shape[0] // N_EXPERTS

        def tail(t):
            used = cs_ref[t * N_EXPERTS + N_EXPERTS - 1] + cnt_ref[t * N_EXPERTS + N_EXPERTS - 1]
            return t * TILE_ROWS + used, TILE_ROWS - used

        def fill(t, c):
            row0, n = tail(t)
            _pow2_pieces(n, lambda a, size: pltpu.make_async_copy(
                ybuf.at[1, pl.ds(0, size)], ys_hbm.at[pl.ds(pl.multiple_of(row0 + a, SUBLANES), size)],
                ssem.at[1]).start())
            return c
        lax.fori_loop(0, n_tiles, fill, 0)

        def drain(t, c):
            wait_scatter(1, tail(t)[1])
            return c
        lax.fori_loop(0, n_tiles, drain, 0)

    nxt = jnp.minimum(i + 1, last)

    @pl.when((i < last) & (nv_ref[nxt] > 0))
    def _():
        gather(nxt, 1 - slot)

    @pl.when(i >= 2)
    def _():
        wait_scatter(slot, nv_ref[jnp.maximum(i - 2, 0)])

    @pl.when(nv > 0)
    def _():
        e = be_ref[i]
        e_prev = be_ref[jnp.maximum(i - 1, 0)]

        @pl.when((i == 0) | (e != e_prev))
        def _():
            wb1[...] = w1_ref[0].astype(BF16)
            wb3[...] = w3_ref[0].astype(BF16)
            wb2[...] = w2_ref[0].astype(BF16)

        wait_gather(slot, nv)
        u = xbuf[slot]
        xa, xb = _unpack_bf16_pair(u[:, :half])
        wv = pltpu.bitcast(u[:, half:], F32)
        roww = wv[:, 0:1] + wv[:, 1:2] + wv[:, 2:3]
        a = _dot(xa, wb1[:half, :]) + _dot(xb, wb1[half:, :])
        b = _dot(xa, wb3[:half, :]) + _dot(xb, wb3[half:, :])
        hmid = (a * jax.nn.sigmoid(a) * b).astype(BF16)
        y = _dot(hmid, wb2[...]) * roww
        ybuf[slot] = _pack_bf16_pair(y[:, :half], y[:, half:])
        scatter(i, slot)

    @pl.when(i == last)
    def _():
        @pl.when(last >= 1)
        def _():
            wait_scatter(1 - slot, nv_ref[jnp.maximum(last - 1, 0)])
        wait_scatter(slot, nv)


def _moe(xs, plan, cnt_flat, w1, w3, w2):
    n_blk = plan[0].shape[0]
    rows = MOE_ROWS
    half = D_MODEL // 2
    wspec = lambda shape: pl.BlockSpec((1,) + shape, lambda i, be, *_: (be[i], 0, 0))
    grid_spec = pltpu.PrefetchScalarGridSpec(
        num_scalar_prefetch=8,
        grid=(n_blk,),
        in_specs=[wspec((D_MODEL, EXPERT_HIDDEN)), wspec((D_MODEL, EXPERT_HIDDEN)),
                  wspec((EXPERT_HIDDEN, D_MODEL)), pl.BlockSpec(memory_space=pl.ANY)],
        out_specs=pl.BlockSpec(memory_space=pl.ANY),
        scratch_shapes=[pltpu.VMEM((2, rows, XS_WIDTH), jnp.uint32),
                        pltpu.VMEM((2, rows, half), jnp.uint32),
                        pltpu.VMEM((D_MODEL, EXPERT_HIDDEN), BF16),
                        pltpu.VMEM((D_MODEL, EXPERT_HIDDEN), BF16),
                        pltpu.VMEM((EXPERT_HIDDEN, D_MODEL), BF16),
                        pltpu.SemaphoreType.DMA((2,)),
                        pltpu.SemaphoreType.DMA((2,))])
    return pl.pallas_call(
        _moe_kernel,
        grid_spec=grid_spec,
        out_shape=jax.ShapeDtypeStruct((xs.shape[0], half), jnp.uint32),
        compiler_params=pltpu.CompilerParams(dimension_semantics=("arbitrary",),
                                             vmem_limit_bytes=VMEM_LIMIT),
        name="moe_ffn",
    )(*plan, cnt_flat, w1, w3, w2, xs)


def _combine_kernel(x1_ref, ys_ref, pos_ref, g2_ref, o_ref):
    tt = x1_ref.shape[0]
    half = D_MODEL // 2
    pos = pos_ref[...]
    pcol = lax.broadcasted_iota(jnp.int32, (tt, ys_ref.shape[0]), 1).astype(F32)
    sel = (jnp.where(pcol == pos[:, 0:1], 1.0, 0.0) + jnp.where(pcol == pos[:, 1:2], 1.0, 0.0))
    sel = sel.astype(BF16)
    lo, hi = _unpack_bf16_pair(ys_ref[...])
    g2 = g2_ref[0]
    x1 = x1_ref[...]
    o_ref[:, :half] = x1[:, :half] + g2[:, :half] * _dot(sel, lo)
    o_ref[:, half:] = x1[:, half:] + g2[:, half:] * _dot(sel, hi)


def _combine(x1, ys, pos, g2, S):
    T = x1.shape[0]
    tt = MOE_TILE
    per_b = S // tt
    return pl.pallas_call(
        _combine_kernel,
        grid=(T // tt,),
        in_specs=[pl.BlockSpec((tt, D_MODEL), lambda i: (i, 0)),
                  pl.BlockSpec((TILE_ROWS, D_MODEL // 2), lambda i: (i, 0)),
                  pl.BlockSpec((tt, LANES), lambda i: (i, 0)),
                  pl.BlockSpec((1, 1, D_MODEL), lambda i: (i // per_b, 0, 0))],
        out_specs=pl.BlockSpec((tt, D_MODEL), lambda i: (i, 0)),
        out_shape=jax.ShapeDtypeStruct((T, D_MODEL), F32),
        compiler_params=pltpu.CompilerParams(vmem_limit_bytes=VMEM_LIMIT),
        name="moe_combine",
    )(x1, ys, pos, g2)


def _prep_w_in(w_in):
    dq = w_in[:, OFF_DIL_Q:OFF_DIL_K]
    dk = w_in[:, OFF_DIL_K:OFF_DIL_V]
    dv = w_in[:, OFF_DIL_V:OFF_GATE_A]
    dil = []
    for g in range(N_DIL_GROUPS):
        cs = slice(g * DIL_OUT_WIDTH, (g + 1) * DIL_OUT_WIDTH)
        dil += [dq[:, cs], dk[:, cs], dv[:, cs]]
    pad = jnp.zeros((D_MODEL, UNIT - FOX_HEADS), w_in.dtype)
    cols = [w_in[:, OFF_GATE_A:OFF_GATE_B], w_in[:, OFF_GATE_B:N_IN],
            w_in[:, OFF_FOX_Q:OFF_FOX_K], w_in[:, OFF_FOX_K:OFF_FOX_V], w_in[:, OFF_FOX_V:OFF_FOX_F],
            *dil, w_in[:, OFF_FOX_F:OFF_DIL_Q], pad]
    return jnp.concatenate(cols, axis=1).astype(BF16)


def _prep_gain(q_gain, k_gain):
    qs = HEAD_DIM ** -0.5 * LOG2E
    ones = jnp.ones((UNIT,), F32)
    fq = q_gain[:FOX_HEADS].reshape(-1) * qs
    fk = k_gain[:FOX_HEADS].reshape(-1)
    dq = q_gain[FOX_HEADS:].reshape(-1) * qs
    dk = k_gain[FOX_HEADS:].reshape(-1)
    dil = []
    for g in range(N_DIL_GROUPS):
        cs = slice(g * DIL_OUT_WIDTH, (g + 1) * DIL_OUT_WIDTH)
        dil += [dq[cs], dk[cs], ones]
    parts = [ones] * 8 + [fq, fk, ones, ones] + dil + [ones]
    return jnp.concatenate(parts).reshape(1, N_UNITS * UNIT)


def _layer(x, mod, rel_bias_table, norm1_g, w_in, b_forget, q_gain, k_gain, w_branch_a, w_branch_b,
           w_out, norm2_g, w_rg, b_rg, w_re, b_re, w1, w3, w2):
    B, S, D = x.shape
    T = B * S
    sh1, sc1, g1, sh2, sc2, g2 = [m.reshape(B, 1, D) for m in jnp.split(mod, 6, axis=-1)]
    x2d = x.reshape(T, D)

    p2d, fgt, slabs = _inproj(x2d, norm1_g.reshape(1, D), sc1, sh1, _prep_w_in(w_in),
                              _prep_gain(q_gain, k_gain), S)
    p3 = p2d.reshape(B, S, P_WIDTH)
    ck = _fcum(fgt.reshape(B, S, LANES), b_forget)
    ya = _fox(p3, ck)
    yd = _dil(slabs, _relbias(rel_bias_table))

    n_router = N_GROUPS + N_EXPERTS
    wr = jnp.concatenate([w_rg, w_re, jnp.zeros((D, LANES - n_router), F32)], axis=1).astype(BF16)
    br = jnp.concatenate([b_rg, b_re, jnp.zeros((LANES - n_router,), F32)]).reshape(1, LANES)
    x1, h2, logits = _outproj(x2d, ya.reshape(T, FOX_WIDTH), yd.reshape(T, DIL_OUT_WIDTH), p2d,
                              g1, sc2, sh2, norm2_g.reshape(1, D),
                              w_branch_a.astype(BF16), w_branch_b.astype(BF16), w_out.astype(BF16),
                              wr, br, S)
    xs, cnt, pos = _dispatch(logits, h2)
    cnt2 = cnt[:, :, 0]
    n_blk = cnt.shape[0] * TILE_ROWS // MOE_ROWS + N_EXPERTS
    plan = _plan(cnt2, n_blk)
    ys = _moe(xs, plan, cnt2.reshape(-1), w1, w3, w2)
    out = _combine(x1, ys, pos, g2, S)
    return out.reshape(B, S, D)


def kernel(x, c, rel_bias_table, w_ada, b_ada, norm1_g, w_in, b_forget, q_gain, k_gain, w_branch_a, w_branch_b, w_out, norm2_g, w_router_group, b_router_group, w_router_expert, b_router_expert, w1, w3, w2):
    depth = w_ada.shape[0]
    for l in range(depth):
        mod = _ada(c, w_ada[l], b_ada[l])
        x = _layer(x, mod, rel_bias_table, norm1_g[l], w_in[l], b_forget[l], q_gain[l], k_gain[l],
                   w_branch_a[l], w_branch_b[l], w_out[l], norm2_g[l], w_router_group[l],
                   b_router_group[l], w_router_expert[l], b_router_expert[l], w1[l], w3[l], w2[l])
    return x
```

```python
import functools
import math

import numpy as np
import jax
import jax.numpy as jnp
from jax import lax
from jax.experimental import pallas as pl
from jax.experimental.pallas import tpu as pltpu

F32 = jnp.float32
BF16 = jnp.bfloat16

D_MODEL = 1024
HEAD_DIM = 64
FOX_HEADS = 8
DIL_GROUPS = ((128, 1), (512, 4), (2048, 16))
DIL_HEADS_PER_GROUP = 4
N_DIL_GROUPS = len(DIL_GROUPS)
DIL_HEADS = N_DIL_GROUPS * DIL_HEADS_PER_GROUP
FOX_WIDTH = FOX_HEADS * HEAD_DIM
DIL_WIDTH = DIL_HEADS * HEAD_DIM
DIL_OUT_WIDTH = DIL_HEADS_PER_GROUP * HEAD_DIM
NUM_BUCKETS = 32
REL_MAX_DISTANCE = 2048
N_GROUPS = 4
EXPERTS_PER_GROUP = 8
N_EXPERTS = N_GROUPS * EXPERTS_PER_GROUP
EXPERT_HIDDEN = D_MODEL // 2
EPS = 1e-6
LOG2E = math.log2(math.e)

OFF_FOX_Q = 0
OFF_FOX_K = OFF_FOX_Q + FOX_WIDTH
OFF_FOX_V = OFF_FOX_K + FOX_WIDTH
OFF_FOX_F = OFF_FOX_V + FOX_WIDTH
OFF_DIL_Q = OFF_FOX_F + FOX_HEADS
OFF_DIL_K = OFF_DIL_Q + DIL_WIDTH
OFF_DIL_V = OFF_DIL_K + DIL_WIDTH
OFF_GATE_A = OFF_DIL_V + DIL_WIDTH
OFF_GATE_B = OFF_GATE_A + D_MODEL
N_IN = OFF_GATE_B + D_MODEL

LANES = 128
UNIT = 256
DIL_L = 128

U_GATE_A, U_GATE_B, U_FOX_Q, U_FOX_K, U_FOX_V, U_DIL, U_FORGET = 0, 4, 8, 10, 12, 14, 23
N_UNITS = 24
P_WIDTH = U_DIL * UNIT
N_SLABS = 3 * N_DIL_GROUPS
_KIND = (["gate"] * 8 + ["normq"] * 2 + ["normk"] * 2 + ["plain"] * 2
         + ["normq", "normk", "plain"] * 3 + ["forget"])

TM_PROJ = 512
TQ_FOX = 256
MOE_ROWS = 256
MOE_TILE = 512
XS_WIDTH = D_MODEL // 2 + LANES
SUBLANES = 8
TILE_ROWS = 2 * MOE_TILE + N_EXPERTS * SUBLANES
MOE_GROUPS = MOE_ROWS // SUBLANES
VMEM_LIMIT = 56 * 1024 * 1024


def _dot(a, b):
    return jnp.dot(a, b, preferred_element_type=F32)


def _dot_nt(a, b):
    return lax.dot_general(a, b, (((1,), (1,)), ((), ())), preferred_element_type=F32)


def _split3(x):
    hi = x.astype(BF16)
    r1 = x - hi.astype(F32)
    mid = r1.astype(BF16)
    lo = (r1 - mid.astype(F32)).astype(BF16)
    return hi, mid, lo


def _ada_kernel(c_ref, w_ref, b_ref, o_ref):
    c = c_ref[...]
    s = c * jax.nn.sigmoid(c)
    s_hi = s.astype(BF16)
    s_lo = (s - s_hi.astype(F32)).astype(BF16)
    w = w_ref[...]
    w_hi = w.astype(BF16)
    w_lo = (w - w_hi.astype(F32)).astype(BF16)
    acc = _dot(s_hi, w_hi) + _dot(s_hi, w_lo) + _dot(s_lo, w_hi)
    o_ref[...] = acc + b_ref[...]


def _ada(c, w_ada, b_ada):
    B = c.shape[0]
    n_out = w_ada.shape[1]
    tn = 512
    return pl.pallas_call(
        _ada_kernel,
        grid=(n_out // tn,),
        in_specs=[pl.BlockSpec((B, D_MODEL), lambda j: (0, 0)),
                  pl.BlockSpec((D_MODEL, tn), lambda j: (0, j)),
                  pl.BlockSpec((1, tn), lambda j: (0, j))],
        out_specs=pl.BlockSpec((B, tn), lambda j: (0, j)),
        out_shape=jax.ShapeDtypeStruct((B, n_out), F32),
        name="ada_mod",
    )(c, w_ada, b_ada.reshape(1, n_out))


def _pack_bf16_pair(lo, hi):
    lo_bits = pltpu.bitcast(lo.astype(BF16).astype(F32), jnp.uint32) >> 16
    hi_bits = pltpu.bitcast(hi.astype(BF16).astype(F32), jnp.uint32) & jnp.uint32(0xFFFF0000)
    return lo_bits | hi_bits


def _unpack_bf16_pair(u):
    lo = pltpu.bitcast(u << 16, F32).astype(BF16)
    hi = pltpu.bitcast(u & jnp.uint32(0xFFFF0000), F32).astype(BF16)
    return lo, hi


def _inproj_kernel(x_ref, g_ref, sc_ref, sh_ref, w_ref, gain_ref, bd_ref, p_ref, f_ref, s_ref):
    x = x_ref[...]
    ms = jnp.mean(x * x, axis=-1, keepdims=True)
    h = x * lax.rsqrt(ms + EPS) * g_ref[...]
    h = h * (1.0 + sc_ref[0]) + sh_ref[0]
    hb = h.astype(BF16)

    def unit(u):
        cols = slice(u * UNIT, (u + 1) * UNIT)
        acc = _dot(hb, w_ref[:, cols])
        kind = _KIND[u]
        if kind == "gate":
            return jax.nn.sigmoid(acc)
        if kind in ("normq", "normk"):
            ss = _dot((acc * acc).astype(BF16), bd_ref[...])
            return acc * lax.rsqrt(ss * (1.0 / HEAD_DIM) + EPS) * gain_ref[:, cols]
        return acc

    for u in range(U_DIL):
        p_ref[:, u * UNIT:(u + 1) * UNIT] = unit(u).astype(BF16)
    for g in range(N_DIL_GROUPS):
        q, k, v = (unit(U_DIL + 3 * g + j) for j in range(3))
        s_ref[0, 3 * g] = _pack_bf16_pair(q[:, :LANES], k[:, :LANES])
        s_ref[0, 3 * g + 1] = _pack_bf16_pair(q[:, LANES:], k[:, LANES:])
        s_ref[0, 3 * g + 2] = _pack_bf16_pair(v[:, :LANES], v[:, LANES:])
    f_ref[...] = unit(U_FORGET)[:, :LANES]


def _inproj(x2d, norm_g, sc, sh, w_re, gain_row, S):
    T = x2d.shape[0]
    tm = TM_PROJ
    per_b = S // tm
    bd = np.kron(np.eye(UNIT // HEAD_DIM), np.ones((HEAD_DIM, HEAD_DIM))).astype(np.float32)
    return pl.pallas_call(
        _inproj_kernel,
        grid=(T // tm,),
        in_specs=[pl.BlockSpec((tm, D_MODEL), lambda i: (i, 0)),
                  pl.BlockSpec((1, D_MODEL), lambda i: (0, 0)),
                  pl.BlockSpec((1, 1, D_MODEL), lambda i: (i // per_b, 0, 0)),
                  pl.BlockSpec((1, 1, D_MODEL), lambda i: (i // per_b, 0, 0)),
                  pl.BlockSpec((D_MODEL, N_UNITS * UNIT), lambda i: (0, 0)),
                  pl.BlockSpec((1, N_UNITS * UNIT), lambda i: (0, 0)),
                  pl.BlockSpec((UNIT, UNIT), lambda i: (0, 0))],
        out_specs=[pl.BlockSpec((tm, P_WIDTH), lambda i: (i, 0)),
                   pl.BlockSpec((tm, LANES), lambda i: (i, 0)),
                   pl.BlockSpec((1, N_SLABS, tm, LANES), lambda i: (i // per_b, 0, i % per_b, 0))],
        out_shape=[jax.ShapeDtypeStruct((T, P_WIDTH), BF16),
                   jax.ShapeDtypeStruct((T, LANES), F32),
                   jax.ShapeDtypeStruct((T // S, N_SLABS, S, LANES), jnp.uint32)],
        compiler_params=pltpu.CompilerParams(vmem_limit_bytes=VMEM_LIMIT),
        name="in_proj",
    )(x2d, norm_g, sc, sh, w_re, gain_row, jnp.asarray(bd, BF16))


def _fcum_kernel(f_ref, b_ref, tri_ref, o_ref):
    S = f_ref.shape[1]
    xf = f_ref[0] + b_ref[...]
    ls = (jnp.minimum(xf, 0.0) - jnp.log(1.0 + jnp.exp(-jnp.abs(xf)))) * LOG2E
    lst = ls.T
    carry = jnp.zeros((LANES, UNIT), F32)
    for blk in range(S // UNIT):
        seg = lst[:, blk * UNIT:(blk + 1) * UNIT]
        hi, mid, lo = _split3(seg)
        tri = tri_ref[...]
        res = _dot(hi, tri) + _dot(mid, tri) + _dot(lo, tri)
        o_ref[0, :, blk * UNIT:(blk + 1) * UNIT] = (res[:, :UNIT] + carry)[:FOX_HEADS]
        carry = carry + res[:, UNIT:]


def _fcum(fgt, b_forget):
    B, S, _ = fgt.shape
    brow = jnp.zeros((1, LANES), F32).at[0, :FOX_HEADS].set(b_forget)
    tri = np.concatenate([np.triu(np.ones((UNIT, UNIT))), np.ones((UNIT, UNIT))], axis=1)
    return pl.pallas_call(
        _fcum_kernel,
        grid=(B,),
        in_specs=[pl.BlockSpec((1, S, LANES), lambda b: (b, 0, 0)),
                  pl.BlockSpec((1, LANES), lambda b: (0, 0)),
                  pl.BlockSpec((UNIT, 2 * UNIT), lambda b: (0, 0))],
        out_specs=pl.BlockSpec((1, FOX_HEADS, S), lambda b: (b, 0, 0)),
        out_shape=jax.ShapeDtypeStruct((B, FOX_HEADS, S), F32),
        name="forget_cumsum",
    )(fgt, brow, jnp.asarray(tri, BF16))


def _fox_kernel(q_ref, k_ref, v_ref, ck_ref, o_ref):
    S = q_ref.shape[1]
    pair = pl.program_id(1)
    tq = TQ_FOX
    lane = lax.broadcasted_iota(jnp.int32, (1, LANES), 1)
    row = lax.broadcasted_iota(jnp.int32, (tq, tq), 0)
    col = lax.broadcasted_iota(jnp.int32, (tq, tq), 1)
    causal = col <= row
    cks = [ck_ref[0, pl.ds(2 * pair + hh, 1), :] for hh in range(2)]
    for t in range(S // tq):
        r0, r1 = t * tq, (t + 1) * tq
        qt = q_ref[0, r0:r1, :]
        outs = []
        for hh in range(2):
            hsel = (lane >= HEAD_DIM) == bool(hh)
            qm = jnp.where(hsel, qt, jnp.zeros_like(qt))
            ck = cks[hh]
            s_d = _dot_nt(qm, k_ref[0, r0:r1, :]) - ck[:, r0:r1]
            s_d = jnp.where(causal, s_d, -jnp.inf)
            m = jnp.max(s_d, axis=-1, keepdims=True)
            if t > 0:
                s_o = _dot_nt(qm, k_ref[0, :r0, :]) - ck[:, :r0]
                m = jnp.maximum(m, jnp.max(s_o, axis=-1, keepdims=True))
            p_d = jnp.exp2(s_d - m)
            l = jnp.sum(p_d, axis=-1, keepdims=True)
            acc = _dot(p_d.astype(BF16), v_ref[0, r0:r1, :])
            if t > 0:
                p_o = jnp.exp2(s_o - m)
                l = l + jnp.sum(p_o, axis=-1, keepdims=True)
                acc = acc + _dot(p_o.astype(BF16), v_ref[0, :r0, :])
            outs.append(acc / l)
        o_ref[0, r0:r1, :] = jnp.where(lane < HEAD_DIM, outs[0], outs[1]).astype(BF16)


def _fox(p3, ck):
    B, S, _ = p3.shape
    nq, nk, nv = (U_FOX_Q * UNIT // LANES, U_FOX_K * UNIT // LANES, U_FOX_V * UNIT // LANES)
    return pl.pallas_call(
        _fox_kernel,
        grid=(B, FOX_HEADS // 2),
        in_specs=[pl.BlockSpec((1, S, LANES), lambda b, p: (b, 0, nq + p)),
                  pl.BlockSpec((1, S, LANES), lambda b, p: (b, 0, nk + p)),
                  pl.BlockSpec((1, S, LANES), lambda b, p: (b, 0, nv + p)),
                  pl.BlockSpec((1, FOX_HEADS, S), lambda b, p: (b, 0, 0))],
        out_specs=pl.BlockSpec((1, S, LANES), lambda b, p: (b, 0, p)),
        out_shape=jax.ShapeDtypeStruct((B, S, FOX_WIDTH), BF16),
        compiler_params=pltpu.CompilerParams(vmem_limit_bytes=VMEM_LIMIT),
        name="fox_attn",
    )(p3, p3, p3, ck)


def _t5_bucket(dist):
    max_exact = NUM_BUCKETS // 2
    d = np.maximum(dist, 1).astype(np.float32)
    large = max_exact + (np.log(d / max_exact) / np.log(REL_MAX_DISTANCE / max_exact)
                         * (NUM_BUCKETS - max_exact)).astype(np.int32)
    large = np.minimum(large, NUM_BUCKETS - 1)
    return np.where(dist < max_exact, dist, large).astype(np.int32)


def _relbias_kernel(tab_ref, bucket_ref, valid_ref, o_ref):
    g = pl.program_id(0)
    bk = bucket_ref[0]
    vd = valid_ref[0]
    for hs in range(DIL_HEADS_PER_GROUP):
        acc = jnp.zeros(bk.shape, F32)
        for b in range(NUM_BUCKETS):
            acc = jnp.where(bk == b, tab_ref[b, g * DIL_HEADS_PER_GROUP + hs], acc)
        bias = jnp.where(vd != 0, acc * LOG2E, -jnp.inf)
        o_ref[0, hs] = bias
        col = lax.broadcasted_iota(jnp.int32, bias.shape, 1)
        o_ref[1, hs] = jnp.where(col >= DIL_L, bias, -jnp.inf)


def _relbias(table):
    L = DIL_L
    i = np.arange(L)[:, None]
    j = np.arange(2 * L)[None, :]
    m = L + i - j
    valid = ((m >= 0) & (m <= L)).astype(np.int32)
    buckets = np.stack([_t5_bucket(np.clip(m, 0, None) * d) for _, d in DIL_GROUPS])
    valids = np.stack([valid] * N_DIL_GROUPS)
    return pl.pallas_call(
        _relbias_kernel,
        grid=(N_DIL_GROUPS,),
        in_specs=[pl.BlockSpec(memory_space=pltpu.SMEM),
                  pl.BlockSpec((1, L, 2 * L), lambda g: (g, 0, 0)),
                  pl.BlockSpec((1, L, 2 * L), lambda g: (g, 0, 0))],
        out_specs=pl.BlockSpec((2, DIL_HEADS_PER_GROUP, L, 2 * L), lambda g: (0, g, 0, 0)),
        out_shape=jax.ShapeDtypeStruct((2, DIL_HEADS, L, 2 * L), F32),
        name="rel_bias",
    )(table, jnp.asarray(buckets), jnp.asarray(valids))


def _dil_rows(start, d):
    return pl.ds(start, DIL_L) if d == 1 else pl.ds(start, DIL_L, stride=d)


def _dil_block(qkv_ref, bias_ref, m_scr, l_scr, acc_scr, g, d, r, n):
    L = DIL_L
    lane = lax.broadcasted_iota(jnp.int32, (1, LANES), 1)
    first = 1 - jnp.minimum(n, 1)
    cur = _dil_rows(r + d * (n * L), d)
    prev = _dil_rows(r + d * (jnp.maximum(n - 1, 0) * L), d)
    v_cur = _unpack_bf16_pair(qkv_ref[0, 3 * g + 2, cur, :])
    v_prev = _unpack_bf16_pair(qkv_ref[0, 3 * g + 2, prev, :])
    for pr in range(2):
        qt, k_cur = _unpack_bf16_pair(qkv_ref[0, 3 * g + pr, cur, :])
        _, k_prev = _unpack_bf16_pair(qkv_ref[0, 3 * g + pr, prev, :])
        kt = jnp.concatenate([k_prev, k_cur], axis=0)
        vt = jnp.concatenate([v_prev[pr], v_cur[pr]], axis=0)
        ms, ls, accs = [], [], []
        for hh in range(2):
            hsel = (lane >= HEAD_DIM) == bool(hh)
            qm = jnp.where(hsel, qt, jnp.zeros_like(qt))
            s = _dot_nt(qm, kt) + bias_ref[first, DIL_HEADS_PER_GROUP * g + 2 * pr + hh]
            m = jnp.max(s, axis=-1, keepdims=True)
            p = jnp.exp2(s - m)
            ms.append(m)
            ls.append(jnp.sum(p, axis=-1, keepdims=True))
            accs.append(_dot(p.astype(BF16), vt))
        low = lane < HEAD_DIM
        m_b = jnp.where(low, ms[0], ms[1])
        l_b = jnp.where(low, ls[0], ls[1])
        acc_b = jnp.where(low, accs[0], accs[1])
        if g == 0:
            m_scr[pr, cur, :] = m_b
            l_scr[pr, cur, :] = l_b
            acc_scr[pr, cur, :] = acc_b
        else:
            m_o = m_scr[pr, cur, :]
            m_n = jnp.maximum(m_o, m_b)
            a_o = jnp.exp2(m_o - m_n)
            a_b = jnp.exp2(m_b - m_n)
            m_scr[pr, cur, :] = m_n
            l_scr[pr, cur, :] = l_scr[pr, cur, :] * a_o + l_b * a_b
            acc_scr[pr, cur, :] = acc_scr[pr, cur, :] * a_o + acc_b * a_b


def _dil_kernel(qkv_ref, bias_ref, o_ref, m_scr, l_scr, acc_scr):
    S = o_ref.shape[1]
    for g, (window, d) in enumerate(DIL_GROUPS):
        nb = S // window

        def body(it, carry, g=g, d=d, nb=nb):
            _dil_block(qkv_ref, bias_ref, m_scr, l_scr, acc_scr, g, d, it // nb, it % nb)
            return carry
        lax.fori_loop(0, d * nb, body, 0, unroll=4)
    for pr in range(2):
        o_ref[0, :, pr * LANES:(pr + 1) * LANES] = (acc_scr[pr] / l_scr[pr]).astype(BF16)


def _dil(slabs, bias):
    B, _, S, _ = slabs.shape
    for window, d in DIL_GROUPS:
        assert window // d == DIL_L and S % window == 0
    stat = pltpu.VMEM((2, S, LANES), F32)
    return pl.pallas_call(
        _dil_kernel,
        grid=(B,),
        in_specs=[pl.BlockSpec((1, N_SLABS, S, LANES), lambda b: (b, 0, 0, 0)),
                  pl.BlockSpec(bias.shape, lambda b: (0, 0, 0, 0))],
        out_specs=pl.BlockSpec((1, S, DIL_OUT_WIDTH), lambda b: (b, 0, 0)),
        out_shape=jax.ShapeDtypeStruct((B, S, DIL_OUT_WIDTH), BF16),
        scratch_shapes=[stat, stat, stat],
        compiler_params=pltpu.CompilerParams(vmem_limit_bytes=VMEM_LIMIT),
        name="dil_attn",
    )(slabs, bias)


def _outproj_kernel(x_ref, ya_ref, yd_ref, ga_ref, gb_ref,
                    g1_ref, sc_ref, sh_ref, ng_ref, wa_ref, wb_ref, wo_ref, wr_ref, br_ref,
                    x1_ref, h2_ref, lg_ref):
    a = _dot(ya_ref[...], wa_ref[...])
    bm = _dot(yd_ref[...], wb_ref[...])
    merged = ga_ref[...].astype(F32) * a + gb_ref[...].astype(F32) * bm
    out = _dot(merged.astype(BF16), wo_ref[...])
    x1 = x_ref[...] + g1_ref[0] * out
    x1_ref[...] = x1
    ms = jnp.mean(x1 * x1, axis=-1, keepdims=True)
    h = x1 * lax.rsqrt(ms + EPS) * ng_ref[...]
    h = h * (1.0 + sc_ref[0]) + sh_ref[0]
    hb = h.astype(BF16)
    h2_ref[...] = hb
    lg_ref[...] = _dot(hb, wr_ref[...]) + br_ref[...]


def _outproj(x2d, ya2d, yd2d, p2d, g1, sc2, sh2, norm_g, wa, wb, wo, wr, br, S):
    T = x2d.shape[0]
    tm = TM_PROJ
    per_b = S // tm
    row = lambda w: pl.BlockSpec((tm, w), lambda i: (i, 0))
    full = lambda a: pl.BlockSpec(a.shape, lambda i: (0,) * a.ndim)
    mod = pl.BlockSpec((1, 1, D_MODEL), lambda i: (i // per_b, 0, 0))
    return pl.pallas_call(
        _outproj_kernel,
        grid=(T // tm,),
        in_specs=[row(D_MODEL), row(FOX_WIDTH), row(DIL_OUT_WIDTH)]
                 + [pl.BlockSpec((tm, D_MODEL), lambda i: (i, U_GATE_A * UNIT // D_MODEL)),
                    pl.BlockSpec((tm, D_MODEL), lambda i: (i, U_GATE_B * UNIT // D_MODEL)),
                    mod, mod, mod, full(norm_g), full(wa), full(wb), full(wo), full(wr), full(br)],
        out_specs=[row(D_MODEL), row(D_MODEL), row(LANES)],
        out_shape=[jax.ShapeDtypeStruct((T, D_MODEL), F32),
                   jax.ShapeDtypeStruct((T, D_MODEL), BF16),
                   jax.ShapeDtypeStruct((T, LANES), F32)],
        compiler_params=pltpu.CompilerParams(vmem_limit_bytes=VMEM_LIMIT),
        name="out_proj",
    )(x2d, ya2d, yd2d, p2d, p2d, g1, sc2, sh2, norm_g, wa, wb, wo, wr, br)


def _dispatch_kernel(lg_ref, h_ref, tri_ref, xs_ref, cnt_ref, pos_ref):
    tt = lg_ref.shape[0]
    lt = lg_ref[...].T
    row = lambda i: lt[i:i + 1, :]
    neg = -jnp.inf
    g = [row(i) for i in range(N_GROUPS)]
    gmax = functools.reduce(jnp.maximum, g)
    gidx = jnp.full(gmax.shape, N_GROUPS - 1, jnp.int32)
    for i in reversed(range(N_GROUPS - 1)):
        gidx = jnp.where(g[i] == gmax, i, gidx)
    gsum = sum(jnp.exp(gi - gmax) for gi in g)
    el = []
    for j in range(EXPERTS_PER_GROUP):
        v = row(N_GROUPS + EXPERTS_PER_GROUP * (N_GROUPS - 1) + j)
        for gg in reversed(range(N_GROUPS - 1)):
            v = jnp.where(gidx == gg, row(N_GROUPS + EXPERTS_PER_GROUP * gg + j), v)
        el.append(v)

    def top(vals):
        best = functools.reduce(jnp.maximum, vals)
        idx = jnp.full(best.shape, EXPERTS_PER_GROUP - 1, jnp.int32)
        for j in reversed(range(EXPERTS_PER_GROUP - 1)):
            idx = jnp.where(vals[j] == best, j, idx)
        return best, idx

    v1, i1 = top(el)
    v2, i2 = top([jnp.where(i1 == j, neg, el[j]) for j in range(EXPERTS_PER_GROUP)])
    t = jnp.exp(v2 - v1)
    den = (1.0 + t) * gsum
    wts = [1.0 / den, t / den]
    eid = [gidx * EXPERTS_PER_GROUP + i1, gidx * EXPERTS_PER_GROUP + i2]

    esub = lax.broadcasted_iota(jnp.int32, (N_EXPERTS, tt), 0)
    ohf = jnp.concatenate([jnp.where(esub == eid[k], 1.0, 0.0) for k in range(2)], axis=1)
    res = _dot(ohf.astype(BF16), tri_ref[...])
    prefix, cnt = res[:, :2 * tt], res[:, 2 * tt:]
    cnt = (((cnt.astype(jnp.int32) + (SUBLANES - 1)) // SUBLANES) * SUBLANES).astype(F32)
    esub_c = lax.broadcasted_iota(jnp.int32, cnt.shape, 0)
    start = jnp.zeros_like(cnt)
    for e in range(N_EXPERTS - 1):
        start = start + jnp.where(esub_c > e, cnt[e:e + 1, :], 0.0)
    start_w = jnp.concatenate([start] * (2 * tt // LANES), axis=1)
    pos = jnp.sum(ohf * (start_w + prefix), axis=0, keepdims=True)
    pos_k = [pos[:, :tt], pos[:, tt:]]

    n_rows = xs_ref.shape[0]
    psub = lax.broadcasted_iota(jnp.int32, (n_rows, tt), 0).astype(F32)
    pm = [jnp.where(psub == pos_k[k], 1.0, 0.0).astype(BF16) for k in range(2)]
    xs = _dot(pm[0] + pm[1], h_ref[...])
    wsub = lax.broadcasted_iota(jnp.int32, (LANES, tt), 0)
    ws = jnp.zeros((n_rows, LANES), F32)
    for k in range(2):
        parts = _split3(wts[k])
        wrows = jnp.zeros((LANES, tt), F32)
        for j in range(3):
            wrows = jnp.where(wsub == j, parts[j].astype(F32), wrows)
        ws = ws + _dot_nt(pm[k], wrows.astype(BF16))
    half = D_MODEL // 2
    xs_ref[:, :half] = _pack_bf16_pair(xs[:, :half], xs[:, half:])
    xs_ref[:, half:] = pltpu.bitcast(ws, jnp.uint32)
    cnt_ref[0] = cnt.astype(jnp.int32)
    posr = jnp.where(wsub == 0, pos_k[0], jnp.where(wsub == 1, pos_k[1], 0.0))
    pos_ref[...] = posr.T


def _dispatch(logits, h2):
    T = logits.shape[0]
    tt = MOE_TILE
    n_tiles = T // tt
    tri = np.concatenate([np.triu(np.ones((2 * tt, 2 * tt)), 1), np.ones((2 * tt, LANES))], axis=1)
    return pl.pallas_call(
        _dispatch_kernel,
        grid=(n_tiles,),
        in_specs=[pl.BlockSpec((tt, LANES), lambda i: (i, 0)),
                  pl.BlockSpec((tt, D_MODEL), lambda i: (i, 0)),
                  pl.BlockSpec(tri.shape, lambda i: (0, 0))],
        out_specs=[pl.BlockSpec((TILE_ROWS, XS_WIDTH), lambda i: (i, 0)),
                   pl.BlockSpec((1, N_EXPERTS, LANES), lambda i: (i, 0, 0)),
                   pl.BlockSpec((tt, LANES), lambda i: (i, 0))],
        out_shape=[jax.ShapeDtypeStruct((n_tiles * TILE_ROWS, XS_WIDTH), jnp.uint32),
                   jax.ShapeDtypeStruct((n_tiles, N_EXPERTS, LANES), jnp.int32),
                   jax.ShapeDtypeStruct((T, LANES), F32)],
        compiler_params=pltpu.CompilerParams(vmem_limit_bytes=VMEM_LIMIT),
        name="moe_dispatch",
    )(logits, h2, jnp.asarray(tri, BF16))


def _plan_kernel(cnt_ref, be_ref, nv_ref, grp_ref, used_ref, cs_ref):
    n_tiles = cnt_ref.shape[0]
    n_blk = be_ref.shape[0]
    rows = MOE_ROWS
    row_shift = rows.bit_length() - 1
    grp_shift = SUBLANES.bit_length() - 1
    assert rows == 1 << row_shift and SUBLANES == 1 << grp_shift

    def tile_starts(t, c):
        def per_e(e, acc):
            cs_ref[t * N_EXPERTS + e] = acc
            return acc + cnt_ref[t, e]
        used_ref[t] = lax.fori_loop(0, N_EXPERTS, per_e, 0, unroll=8)
        return c
    lax.fori_loop(0, n_tiles, tile_starts, 0)

    def clear(b, c):
        nv_ref[b] = 0
        return c
    lax.fori_loop(0, n_blk, clear, 0)

    def clear_groups(g, c):
        grp_ref[g] = 0
        return c
    lax.fori_loop(0, n_blk * MOE_GROUPS, clear_groups, 0, unroll=8)

    def per_expert(e, b):
        g0 = b * MOE_GROUPS

        def per_tile(t, tot):
            c = cnt_ref[t, e]
            src = t * TILE_ROWS + cs_ref[t * N_EXPERTS + e]
            first = g0 + lax.shift_right_logical(tot, grp_shift)

            def per_group(k, cc):
                grp_ref[first + k] = src + k * SUBLANES
                return cc
            lax.fori_loop(0, lax.shift_right_logical(c, grp_shift), per_group, 0)
            return tot + c
        tot = lax.fori_loop(0, n_tiles, per_tile, 0)

        def per_block(j, c):
            be_ref[b + j] = e
            nv_ref[b + j] = jnp.minimum(rows, tot - j * rows)
            return c
        nb = lax.shift_right_logical(tot + rows - 1, row_shift)
        lax.fori_loop(0, nb, per_block, 0)
        return b + nb
    n_used = lax.fori_loop(0, N_EXPERTS, per_expert, 0)

    def unused(b, c):
        be_ref[b] = be_ref[n_used - 1]
        return c
    lax.fori_loop(n_used, n_blk, unused, 0)


def _plan(cnt, n_blk):
    n_tiles = cnt.shape[0]
    smem = pl.BlockSpec(memory_space=pltpu.SMEM)
    i32 = lambda n: jax.ShapeDtypeStruct((n,), jnp.int32)
    return pl.pallas_call(
        _plan_kernel,
        in_specs=[smem],
        out_specs=[smem] * 4,
        out_shape=[i32(n_blk), i32(n_blk), i32(n_blk * MOE_GROUPS), i32(n_tiles)],
        scratch_shapes=[pltpu.SMEM((n_tiles * N_EXPERTS,), jnp.int32)],
        name="moe_plan",
    )(cnt)


def _pow2_pieces(n, fn):
    for b in reversed(range(SUBLANES.bit_length() - 1, MOE_ROWS.bit_length())):
        size = 1 << b

        @pl.when((n & size) != 0)
        def _():
            fn((n >> (b + 1)) << (b + 1), size)


def _moe_kernel(be_ref, nv_ref, grp_ref, used_ref,
                w1_ref, w3_ref, w2_ref, xs_hbm, ys_hbm, xbuf, ybuf, wb1, wb3, wb2, gsem, ssem):
    i = pl.program_id(0)
    last = pl.num_programs(0) - 1
    slot = i % 2
    nv = nv_ref[i]
    half = D_MODEL // 2
    grp_shift = SUBLANES.bit_length() - 1

    def group_row(blk, g):
        return pl.multiple_of(grp_ref[blk * MOE_GROUPS + g], SUBLANES)

    def gather(blk, s):
        top = jnp.maximum(lax.shift_right_logical(nv_ref[blk], grp_shift) - 1, 0)
        for g in range(MOE_GROUPS):
            src = group_row(blk, jnp.minimum(g, top))
            pltpu.make_async_copy(xs_hbm.at[pl.ds(src, SUBLANES)],
                                  xbuf.at[s, pl.ds(g * SUBLANES, SUBLANES)], gsem.at[s]).start()

    def wait_gather(s):
        pltpu.make_async_copy(xs_hbm.at[pl.ds(0, MOE_ROWS)], xbuf.at[s], gsem.at[s]).wait()

    def scatter(blk, s):
        def body(g, c):
            r = pl.multiple_of(g * SUBLANES, SUBLANES)
            pltpu.make_async_copy(ybuf.at[s, pl.ds(r, SUBLANES)],
                                  ys_hbm.at[pl.ds(group_row(blk, g), SUBLANES)], ssem.at[s]).start()
            return c
        lax.fori_loop(0, lax.shift_right_logical(nv_ref[blk], grp_shift), body, 0)

    def wait_scatter(s, count):
        _pow2_pieces(count, lambda a, size: pltpu.make_async_copy(
            ybuf.at[s, pl.ds(0, size)], ys_hbm.at[pl.ds(0, size)], ssem.at[s]).wait())

    @pl.when(i == 0)
    def _():
        @pl.when(nv > 0)
        def _():
            gather(0, 0)

        ybuf[1] = jnp.zeros(ybuf.shape[1:], ybuf.dtype)
        n_tiles = used_ref.shape[0]

        def fill(t, c):
            row0 = t * TILE_ROWS + used_ref[t]
            _pow2_pieces(TILE_ROWS - used_ref[t], lambda a, size: pltpu.make_async_copy(
                ybuf.at[1, pl.ds(0, size)], ys_hbm.at[pl.ds(pl.multiple_of(row0 + a, SUBLANES), size)],
                ssem.at[1]).start())
            return c
        lax.fori_loop(0, n_tiles, fill, 0)

        def drain(t, c):
            wait_scatter(1, TILE_ROWS - used_ref[t])
            return c
        lax.fori_loop(0, n_tiles, drain, 0)

    @pl.when(i >= 2)
    def _():
        wait_scatter(slot, nv_ref[jnp.maximum(i - 2, 0)])

    @pl.when((nv == 0) & (i > 0) & (nv_ref[jnp.maximum(i - 1, 0)] > 0))
    def _():
        wait_gather(slot)

    @pl.when(nv > 0)
    def _():
        e = be_ref[i]
        e_prev = be_ref[jnp.maximum(i - 1, 0)]

        @pl.when((i == 0) | (e != e_prev))
        def _():
            wb1[...] = w1_ref[0].astype(BF16)
            wb3[...] = w3_ref[0].astype(BF16)
            wb2[...] = w2_ref[0].astype(BF16)

        wait_gather(slot)
        gather(jnp.minimum(i + 1, last), 1 - slot)
        u = xbuf[slot]
        xa, xb = _unpack_bf16_pair(u[:, :half])
        wv = pltpu.bitcast(u[:, half:], F32)
        roww = wv[:, 0:1] + wv[:, 1:2] + wv[:, 2:3]
        a = _dot(xa, wb1[:half, :]) + _dot(xb, wb1[half:, :])
        b = _dot(xa, wb3[:half, :]) + _dot(xb, wb3[half:, :])
        hmid = (a * jax.nn.sigmoid(a) * b).astype(BF16)
        y = _dot(hmid, wb2[...]) * roww
        ybuf[slot] = _pack_bf16_pair(y[:, :half], y[:, half:])
        scatter(i, slot)

    @pl.when(i == last)
    def _():
        @pl.when(nv > 0)
        def _():
            wait_gather(1 - slot)

        @pl.when(last >= 1)
        def _():
            wait_scatter(1 - slot, nv_ref[jnp.maximum(last - 1, 0)])
        wait_scatter(slot, nv)


def _moe(xs, plan, w1, w3, w2):
    n_blk = plan[0].shape[0]
    rows = MOE_ROWS
    half = D_MODEL // 2
    wspec = lambda shape: pl.BlockSpec((1,) + shape, lambda i, be, *_: (be[i], 0, 0))
    grid_spec = pltpu.PrefetchScalarGridSpec(
        num_scalar_prefetch=4,
        grid=(n_blk,),
        in_specs=[wspec((D_MODEL, EXPERT_HIDDEN)), wspec((D_MODEL, EXPERT_HIDDEN)),
                  wspec((EXPERT_HIDDEN, D_MODEL)), pl.BlockSpec(memory_space=pl.ANY)],
        out_specs=pl.BlockSpec(memory_space=pl.ANY),
        scratch_shapes=[pltpu.VMEM((2, rows, XS_WIDTH), jnp.uint32),
                        pltpu.VMEM((2, rows, half), jnp.uint32),
                        pltpu.VMEM((D_MODEL, EXPERT_HIDDEN), BF16),
                        pltpu.VMEM((D_MODEL, EXPERT_HIDDEN), BF16),
                        pltpu.VMEM((EXPERT_HIDDEN, D_MODEL), BF16),
                        pltpu.SemaphoreType.DMA((2,)),
                        pltpu.SemaphoreType.DMA((2,))])
    return pl.pallas_call(
        _moe_kernel,
        grid_spec=grid_spec,
        out_shape=jax.ShapeDtypeStruct((xs.shape[0], half), jnp.uint32),
        compiler_params=pltpu.CompilerParams(dimension_semantics=("arbitrary",),
                                             vmem_limit_bytes=VMEM_LIMIT),
        name="moe_ffn",
    )(*plan, w1, w3, w2, xs)


def _combine_kernel(x1_ref, ys_ref, pos_ref, g2_ref, o_ref):
    tt = x1_ref.shape[0]
    half = D_MODEL // 2
    pos = pos_ref[...]
    pcol = lax.broadcasted_iota(jnp.int32, (tt, ys_ref.shape[0]), 1).astype(F32)
    sel = (jnp.where(pcol == pos[:, 0:1], 1.0, 0.0) + jnp.where(pcol == pos[:, 1:2], 1.0, 0.0))
    sel = sel.astype(BF16)
    lo, hi = _unpack_bf16_pair(ys_ref[...])
    g2 = g2_ref[0]
    x1 = x1_ref[...]
    o_ref[:, :half] = x1[:, :half] + g2[:, :half] * _dot(sel, lo)
    o_ref[:, half:] = x1[:, half:] + g2[:, half:] * _dot(sel, hi)


def _combine(x1, ys, pos, g2, S):
    T = x1.shape[0]
    tt = MOE_TILE
    per_b = S // tt
    return pl.pallas_call(
        _combine_kernel,
        grid=(T // tt,),
        in_specs=[pl.BlockSpec((tt, D_MODEL), lambda i: (i, 0)),
                  pl.BlockSpec((TILE_ROWS, D_MODEL // 2), lambda i: (i, 0)),
                  pl.BlockSpec((tt, LANES), lambda i: (i, 0)),
                  pl.BlockSpec((1, 1, D_MODEL), lambda i: (i // per_b, 0, 0))],
        out_specs=pl.BlockSpec((tt, D_MODEL), lambda i: (i, 0)),
        out_shape=jax.ShapeDtypeStruct((T, D_MODEL), F32),
        compiler_params=pltpu.CompilerParams(vmem_limit_bytes=VMEM_LIMIT),
        name="moe_combine",
    )(x1, ys, pos, g2)


def _prep_w_in(w_in):
    dq = w_in[:, OFF_DIL_Q:OFF_DIL_K]
    dk = w_in[:, OFF_DIL_K:OFF_DIL_V]
    dv = w_in[:, OFF_DIL_V:OFF_GATE_A]
    dil = []
    for g in range(N_DIL_GROUPS):
        cs = slice(g * DIL_OUT_WIDTH, (g + 1) * DIL_OUT_WIDTH)
        dil += [dq[:, cs], dk[:, cs], dv[:, cs]]
    pad = jnp.zeros((D_MODEL, UNIT - FOX_HEADS), w_in.dtype)
    cols = [w_in[:, OFF_GATE_A:OFF_GATE_B], w_in[:, OFF_GATE_B:N_IN],
            w_in[:, OFF_FOX_Q:OFF_FOX_K], w_in[:, OFF_FOX_K:OFF_FOX_V], w_in[:, OFF_FOX_V:OFF_FOX_F],
            *dil, w_in[:, OFF_FOX_F:OFF_DIL_Q], pad]
    return jnp.concatenate(cols, axis=1).astype(BF16)


def _prep_gain(q_gain, k_gain):
    qs = HEAD_DIM ** -0.5 * LOG2E
    ones = jnp.ones((UNIT,), F32)
    fq = q_gain[:FOX_HEADS].reshape(-1) * qs
    fk = k_gain[:FOX_HEADS].reshape(-1)
    dq = q_gain[FOX_HEADS:].reshape(-1) * qs
    dk = k_gain[FOX_HEADS:].reshape(-1)
    dil = []
    for g in range(N_DIL_GROUPS):
        cs = slice(g * DIL_OUT_WIDTH, (g + 1) * DIL_OUT_WIDTH)
        dil += [dq[cs], dk[cs], ones]
    parts = [ones] * 8 + [fq, fk, ones, ones] + dil + [ones]
    return jnp.concatenate(parts).reshape(1, N_UNITS * UNIT)


def _layer(x, mod, rel_bias_table, norm1_g, w_in, b_forget, q_gain, k_gain, w_branch_a, w_branch_b,
           w_out, norm2_g, w_rg, b_rg, w_re, b_re, w1, w3, w2):
    B, S, D = x.shape
    T = B * S
    sh1, sc1, g1, sh2, sc2, g2 = [m.reshape(B, 1, D) for m in jnp.split(mod, 6, axis=-1)]
    x2d = x.reshape(T, D)

    p2d, fgt, slabs = _inproj(x2d, norm1_g.reshape(1, D), sc1, sh1, _prep_w_in(w_in),
                              _prep_gain(q_gain, k_gain), S)
    p3 = p2d.reshape(B, S, P_WIDTH)
    ck = _fcum(fgt.reshape(B, S, LANES), b_forget)
    ya = _fox(p3, ck)
    yd = _dil(slabs, _relbias(rel_bias_table))

    n_router = N_GROUPS + N_EXPERTS
    wr = jnp.concatenate([w_rg, w_re, jnp.zeros((D, LANES - n_router), F32)], axis=1).astype(BF16)
    br = jnp.concatenate([b_rg, b_re, jnp.zeros((LANES - n_router,), F32)]).reshape(1, LANES)
    x1, h2, logits = _outproj(x2d, ya.reshape(T, FOX_WIDTH), yd.reshape(T, DIL_OUT_WIDTH), p2d,
                              g1, sc2, sh2, norm2_g.reshape(1, D),
                              w_branch_a.astype(BF16), w_branch_b.astype(BF16), w_out.astype(BF16),
                              wr, br, S)
    xs, cnt, pos = _dispatch(logits, h2)
    cnt2 = cnt[:, :, 0]
    n_blk = cnt.shape[0] * TILE_ROWS // MOE_ROWS + N_EXPERTS
    plan = _plan(cnt2, n_blk)
    ys = _moe(xs, plan, w1, w3, w2)
    out = _combine(x1, ys, pos, g2, S)
    return out.reshape(B, S, D)


def kernel(x, c, rel_bias_table, w_ada, b_ada, norm1_g, w_in, b_forget, q_gain, k_gain, w_branch_a, w_branch_b, w_out, norm2_g, w_router_group, b_router_group, w_router_expert, b_router_expert, w1, w3, w2):
    depth = w_ada.shape[0]
    for l in range(depth):
        mod = _ada(c, w_ada[l], b_ada[l])
        x = _layer(x, mod, rel_bias_table, norm1_g[l], w_in[l], b_forget[l], q_gain[l], k_gain[l],
                   w_branch_a[l], w_branch_b[l], w_out[l], norm2_g[l], w_router_group[l],
                   b_router_group[l], w_router_expert[l], b_router_expert[l], w1[l], w3[l], w2[l])
    return x
```

```python
import functools
import math

import numpy as np
import jax
import jax.numpy as jnp
from jax import lax
from jax.experimental import pallas as pl
from jax.experimental.pallas import tpu as pltpu

F32 = jnp.float32
BF16 = jnp.bfloat16

D_MODEL = 1024
HEAD_DIM = 64
FOX_HEADS = 8
DIL_GROUPS = ((128, 1), (512, 4), (2048, 16))
DIL_HEADS_PER_GROUP = 4
N_DIL_GROUPS = len(DIL_GROUPS)
DIL_HEADS = N_DIL_GROUPS * DIL_HEADS_PER_GROUP
FOX_WIDTH = FOX_HEADS * HEAD_DIM
DIL_WIDTH = DIL_HEADS * HEAD_DIM
DIL_OUT_WIDTH = DIL_HEADS_PER_GROUP * HEAD_DIM
NUM_BUCKETS = 32
REL_MAX_DISTANCE = 2048
N_GROUPS = 4
EXPERTS_PER_GROUP = 8
N_EXPERTS = N_GROUPS * EXPERTS_PER_GROUP
EXPERT_HIDDEN = D_MODEL // 2
EPS = 1e-6
LOG2E = math.log2(math.e)

OFF_FOX_Q = 0
OFF_FOX_K = OFF_FOX_Q + FOX_WIDTH
OFF_FOX_V = OFF_FOX_K + FOX_WIDTH
OFF_FOX_F = OFF_FOX_V + FOX_WIDTH
OFF_DIL_Q = OFF_FOX_F + FOX_HEADS
OFF_DIL_K = OFF_DIL_Q + DIL_WIDTH
OFF_DIL_V = OFF_DIL_K + DIL_WIDTH
OFF_GATE_A = OFF_DIL_V + DIL_WIDTH
OFF_GATE_B = OFF_GATE_A + D_MODEL
N_IN = OFF_GATE_B + D_MODEL

LANES = 128
UNIT = 256
DIL_L = 128

U_GATE_A, U_GATE_B, U_FOX_Q, U_FOX_K, U_FOX_V, U_DIL, U_FORGET = 0, 4, 8, 10, 12, 14, 23
N_UNITS = 24
P_WIDTH = U_DIL * UNIT
N_SLABS = 3 * N_DIL_GROUPS
_KIND = (["gate"] * 8 + ["normq"] * 2 + ["normk"] * 2 + ["plain"] * 2
         + ["normq", "normk", "plain"] * 3 + ["forget"])

TM_INPROJ = 1024
TM_PROJ = 512
TQ_FOX = 512
MOE_ROWS = 256
MOE_TILE = 512
XS_WIDTH = D_MODEL // 2 + LANES
SUBLANES = 8
TILE_ROWS = 2 * MOE_TILE + N_EXPERTS * SUBLANES
MOE_GROUPS = MOE_ROWS // SUBLANES
VMEM_LIMIT = 56 * 1024 * 1024


def _dot(a, b):
    return jnp.dot(a, b, preferred_element_type=F32)


def _dot_nt(a, b):
    return lax.dot_general(a, b, (((1,), (1,)), ((), ())), preferred_element_type=F32)


def _split3(x):
    hi = x.astype(BF16)
    r1 = x - hi.astype(F32)
    mid = r1.astype(BF16)
    lo = (r1 - mid.astype(F32)).astype(BF16)
    return hi, mid, lo


def _ada_kernel(c_ref, w_ref, b_ref, o_ref):
    c = c_ref[...]
    s = c * jax.nn.sigmoid(c)
    s_hi = s.astype(BF16)
    s_lo = (s - s_hi.astype(F32)).astype(BF16)
    w = w_ref[...]
    w_hi = w.astype(BF16)
    w_lo = (w - w_hi.astype(F32)).astype(BF16)
    acc = _dot(s_hi, w_hi) + _dot(s_hi, w_lo) + _dot(s_lo, w_hi)
    o_ref[...] = acc + b_ref[...]


def _ada(c, w_ada, b_ada):
    B = c.shape[0]
    n_out = w_ada.shape[1]
    tn = 512
    return pl.pallas_call(
        _ada_kernel,
        grid=(n_out // tn,),
        in_specs=[pl.BlockSpec((B, D_MODEL), lambda j: (0, 0)),
                  pl.BlockSpec((D_MODEL, tn), lambda j: (0, j)),
                  pl.BlockSpec((1, tn), lambda j: (0, j))],
        out_specs=pl.BlockSpec((B, tn), lambda j: (0, j)),
        out_shape=jax.ShapeDtypeStruct((B, n_out), F32),
        name="ada_mod",
    )(c, w_ada, b_ada.reshape(1, n_out))


def _pack_bf16_pair(lo, hi):
    lo_bits = pltpu.bitcast(lo.astype(BF16).astype(F32), jnp.uint32) >> 16
    hi_bits = pltpu.bitcast(hi.astype(BF16).astype(F32), jnp.uint32) & jnp.uint32(0xFFFF0000)
    return lo_bits | hi_bits


def _unpack_bf16_pair(u):
    lo = pltpu.bitcast(u << 16, F32).astype(BF16)
    hi = pltpu.bitcast(u & jnp.uint32(0xFFFF0000), F32).astype(BF16)
    return lo, hi


def _inproj_kernel(x_ref, g_ref, sc_ref, sh_ref, w_ref, gain_ref, bd_ref, p_ref, f_ref, s_ref):
    x = x_ref[...]
    ms = jnp.mean(x * x, axis=-1, keepdims=True)
    h = x * lax.rsqrt(ms + EPS) * g_ref[...]
    h = h * (1.0 + sc_ref[0]) + sh_ref[0]
    hb = h.astype(BF16)

    def unit(u):
        cols = slice(u * UNIT, (u + 1) * UNIT)
        acc = _dot(hb, w_ref[:, cols])
        kind = _KIND[u]
        if kind == "gate":
            return jax.nn.sigmoid(acc)
        if kind in ("normq", "normk"):
            ss = _dot((acc * acc).astype(BF16), bd_ref[...])
            return acc * lax.rsqrt(ss * (1.0 / HEAD_DIM) + EPS) * gain_ref[:, cols]
        return acc

    for u in range(U_DIL):
        p_ref[:, u * UNIT:(u + 1) * UNIT] = unit(u).astype(BF16)
    for g in range(N_DIL_GROUPS):
        q, k, v = (unit(U_DIL + 3 * g + j) for j in range(3))
        s_ref[0, 3 * g] = _pack_bf16_pair(q[:, :LANES], k[:, :LANES])
        s_ref[0, 3 * g + 1] = _pack_bf16_pair(q[:, LANES:], k[:, LANES:])
        s_ref[0, 3 * g + 2] = _pack_bf16_pair(v[:, :LANES], v[:, LANES:])
    f_ref[...] = unit(U_FORGET)[:, :LANES]


def _inproj(x2d, norm_g, sc, sh, w_re, gain_row, S):
    T = x2d.shape[0]
    tm = TM_INPROJ
    per_b = S // tm
    bd = np.kron(np.eye(UNIT // HEAD_DIM), np.ones((HEAD_DIM, HEAD_DIM))).astype(np.float32)
    once = dict(pipeline_mode=pl.Buffered(1))
    return pl.pallas_call(
        _inproj_kernel,
        grid=(T // tm,),
        in_specs=[pl.BlockSpec((tm, D_MODEL), lambda i: (i, 0)),
                  pl.BlockSpec((1, D_MODEL), lambda i: (0, 0)),
                  pl.BlockSpec((1, 1, D_MODEL), lambda i: (i // per_b, 0, 0)),
                  pl.BlockSpec((1, 1, D_MODEL), lambda i: (i // per_b, 0, 0)),
                  pl.BlockSpec((D_MODEL, N_UNITS * UNIT), lambda i: (0, 0), **once),
                  pl.BlockSpec((1, N_UNITS * UNIT), lambda i: (0, 0), **once),
                  pl.BlockSpec((UNIT, UNIT), lambda i: (0, 0), **once)],
        out_specs=[pl.BlockSpec((tm, P_WIDTH), lambda i: (i, 0)),
                   pl.BlockSpec((tm, LANES), lambda i: (i, 0)),
                   pl.BlockSpec((1, N_SLABS, tm, LANES), lambda i: (i // per_b, 0, i % per_b, 0))],
        out_shape=[jax.ShapeDtypeStruct((T, P_WIDTH), BF16),
                   jax.ShapeDtypeStruct((T, LANES), F32),
                   jax.ShapeDtypeStruct((T // S, N_SLABS, S, LANES), jnp.uint32)],
        compiler_params=pltpu.CompilerParams(vmem_limit_bytes=VMEM_LIMIT),
        name="in_proj",
    )(x2d, norm_g, sc, sh, w_re, gain_row, jnp.asarray(bd, BF16))


def _fcum_kernel(f_ref, b_ref, tri_ref, o_ref):
    S = f_ref.shape[1]
    xf = f_ref[0] + b_ref[...]
    ls = (jnp.minimum(xf, 0.0) - jnp.log(1.0 + jnp.exp(-jnp.abs(xf)))) * LOG2E
    lst = ls.T
    carry = jnp.zeros((LANES, UNIT), F32)
    for blk in range(S // UNIT):
        seg = lst[:, blk * UNIT:(blk + 1) * UNIT]
        hi, mid, lo = _split3(seg)
        tri = tri_ref[...]
        res = _dot(hi, tri) + _dot(mid, tri) + _dot(lo, tri)
        o_ref[0, :, blk * UNIT:(blk + 1) * UNIT] = (res[:, :UNIT] + carry)[:FOX_HEADS]
        carry = carry + res[:, UNIT:]


def _fcum(fgt, b_forget):
    B, S, _ = fgt.shape
    brow = jnp.zeros((1, LANES), F32).at[0, :FOX_HEADS].set(b_forget)
    tri = np.concatenate([np.triu(np.ones((UNIT, UNIT))), np.ones((UNIT, UNIT))], axis=1)
    return pl.pallas_call(
        _fcum_kernel,
        grid=(B,),
        in_specs=[pl.BlockSpec((1, S, LANES), lambda b: (b, 0, 0)),
                  pl.BlockSpec((1, LANES), lambda b: (0, 0)),
                  pl.BlockSpec((UNIT, 2 * UNIT), lambda b: (0, 0))],
        out_specs=pl.BlockSpec((1, FOX_HEADS, S), lambda b: (b, 0, 0)),
        out_shape=jax.ShapeDtypeStruct((B, FOX_HEADS, S), F32),
        name="forget_cumsum",
    )(fgt, brow, jnp.asarray(tri, BF16))


def _fox_kernel(q_ref, k_ref, v_ref, ck_ref, o_ref):
    S = q_ref.shape[1]
    pair = pl.program_id(1)
    tq = TQ_FOX
    lane = lax.broadcasted_iota(jnp.int32, (1, LANES), 1)
    row = lax.broadcasted_iota(jnp.int32, (tq, tq), 0)
    col = lax.broadcasted_iota(jnp.int32, (tq, tq), 1)
    causal = col <= row
    cks = [ck_ref[0, pl.ds(2 * pair + hh, 1), :] for hh in range(2)]
    for t in range(S // tq):
        r0, r1 = t * tq, (t + 1) * tq
        qt = q_ref[0, r0:r1, :]
        outs = []
        for hh in range(2):
            hsel = (lane >= HEAD_DIM) == bool(hh)
            qm = jnp.where(hsel, qt, jnp.zeros_like(qt))
            ck = cks[hh]
            s_d = _dot_nt(qm, k_ref[0, r0:r1, :]) - ck[:, r0:r1]
            s_d = jnp.where(causal, s_d, -jnp.inf)
            m = jnp.max(s_d, axis=-1, keepdims=True)
            if t > 0:
                s_o = _dot_nt(qm, k_ref[0, :r0, :]) - ck[:, :r0]
                m = jnp.maximum(m, jnp.max(s_o, axis=-1, keepdims=True))
            p_d = jnp.exp2(s_d - m)
            l = jnp.sum(p_d, axis=-1, keepdims=True)
            acc = _dot(p_d.astype(BF16), v_ref[0, r0:r1, :])
            if t > 0:
                p_o = jnp.exp2(s_o - m)
                l = l + jnp.sum(p_o, axis=-1, keepdims=True)
                acc = acc + _dot(p_o.astype(BF16), v_ref[0, :r0, :])
            outs.append(acc / l)
        o_ref[0, r0:r1, :] = jnp.where(lane < HEAD_DIM, outs[0], outs[1]).astype(BF16)


def _fox(p3, ck):
    B, S, _ = p3.shape
    nq, nk, nv = (U_FOX_Q * UNIT // LANES, U_FOX_K * UNIT // LANES, U_FOX_V * UNIT // LANES)
    return pl.pallas_call(
        _fox_kernel,
        grid=(B, FOX_HEADS // 2),
        in_specs=[pl.BlockSpec((1, S, LANES), lambda b, p: (b, 0, nq + p)),
                  pl.BlockSpec((1, S, LANES), lambda b, p: (b, 0, nk + p)),
                  pl.BlockSpec((1, S, LANES), lambda b, p: (b, 0, nv + p)),
                  pl.BlockSpec((1, FOX_HEADS, S), lambda b, p: (b, 0, 0))],
        out_specs=pl.BlockSpec((1, S, LANES), lambda b, p: (b, 0, p)),
        out_shape=jax.ShapeDtypeStruct((B, S, FOX_WIDTH), BF16),
        compiler_params=pltpu.CompilerParams(vmem_limit_bytes=VMEM_LIMIT),
        name="fox_attn",
    )(p3, p3, p3, ck)


def _t5_bucket(dist):
    max_exact = NUM_BUCKETS // 2
    d = np.maximum(dist, 1).astype(np.float32)
    large = max_exact + (np.log(d / max_exact) / np.log(REL_MAX_DISTANCE / max_exact)
                         * (NUM_BUCKETS - max_exact)).astype(np.int32)
    large = np.minimum(large, NUM_BUCKETS - 1)
    return np.where(dist < max_exact, dist, large).astype(np.int32)


def _relbias_kernel(tab_ref, bucket_ref, valid_ref, o_ref):
    g = pl.program_id(0)
    bk = bucket_ref[0]
    vd = valid_ref[0]
    for hs in range(DIL_HEADS_PER_GROUP):
        acc = jnp.zeros(bk.shape, F32)
        for b in range(NUM_BUCKETS):
            acc = jnp.where(bk == b, tab_ref[b, g * DIL_HEADS_PER_GROUP + hs], acc)
        bias = jnp.where(vd != 0, acc * LOG2E, -jnp.inf)
        o_ref[0, hs] = bias
        col = lax.broadcasted_iota(jnp.int32, bias.shape, 1)
        o_ref[1, hs] = jnp.where(col >= DIL_L, bias, -jnp.inf)


def _relbias(table):
    L = DIL_L
    i = np.arange(L)[:, None]
    j = np.arange(2 * L)[None, :]
    m = L + i - j
    valid = ((m >= 0) & (m <= L)).astype(np.int32)
    buckets = np.stack([_t5_bucket(np.clip(m, 0, None) * d) for _, d in DIL_GROUPS])
    valids = np.stack([valid] * N_DIL_GROUPS)
    return pl.pallas_call(
        _relbias_kernel,
        grid=(N_DIL_GROUPS,),
        in_specs=[pl.BlockSpec(memory_space=pltpu.SMEM),
                  pl.BlockSpec((1, L, 2 * L), lambda g: (g, 0, 0)),
                  pl.BlockSpec((1, L, 2 * L), lambda g: (g, 0, 0))],
        out_specs=pl.BlockSpec((2, DIL_HEADS_PER_GROUP, L, 2 * L), lambda g: (0, g, 0, 0)),
        out_shape=jax.ShapeDtypeStruct((2, DIL_HEADS, L, 2 * L), F32),
        name="rel_bias",
    )(table, jnp.asarray(buckets), jnp.asarray(valids))


def _dil_rows(start, d):
    return pl.ds(start, DIL_L) if d == 1 else pl.ds(start, DIL_L, stride=d)


def _dil_block(qkv_ref, bias_ref, m_scr, l_scr, acc_scr, g, d, r, n):
    L = DIL_L
    lane = lax.broadcasted_iota(jnp.int32, (1, LANES), 1)
    first = 1 - jnp.minimum(n, 1)
    cur = _dil_rows(r + d * (n * L), d)
    prev = _dil_rows(r + d * (jnp.maximum(n - 1, 0) * L), d)
    v_cur = _unpack_bf16_pair(qkv_ref[0, 3 * g + 2, cur, :])
    v_prev = _unpack_bf16_pair(qkv_ref[0, 3 * g + 2, prev, :])
    for pr in range(2):
        qt, k_cur = _unpack_bf16_pair(qkv_ref[0, 3 * g + pr, cur, :])
        _, k_prev = _unpack_bf16_pair(qkv_ref[0, 3 * g + pr, prev, :])
        kt = jnp.concatenate([k_prev, k_cur], axis=0)
        vt = jnp.concatenate([v_prev[pr], v_cur[pr]], axis=0)
        ms, ls, accs = [], [], []
        for hh in range(2):
            hsel = (lane >= HEAD_DIM) == bool(hh)
            qm = jnp.where(hsel, qt, jnp.zeros_like(qt))
            s = _dot_nt(qm, kt) + bias_ref[first, DIL_HEADS_PER_GROUP * g + 2 * pr + hh]
            m = jnp.max(s, axis=-1, keepdims=True)
            p = jnp.exp2(s - m)
            ms.append(m)
            ls.append(jnp.sum(p, axis=-1, keepdims=True))
            accs.append(_dot(p.astype(BF16), vt))
        low = lane < HEAD_DIM
        m_b = jnp.where(low, ms[0], ms[1])
        l_b = jnp.where(low, ls[0], ls[1])
        acc_b = jnp.where(low, accs[0], accs[1])
        if g == 0:
            m_scr[pr, cur, :] = m_b
            l_scr[pr, cur, :] = l_b
            acc_scr[pr, cur, :] = acc_b
        else:
            m_o = m_scr[pr, cur, :]
            m_n = jnp.maximum(m_o, m_b)
            a_o = jnp.exp2(m_o - m_n)
            a_b = jnp.exp2(m_b - m_n)
            m_scr[pr, cur, :] = m_n
            l_scr[pr, cur, :] = l_scr[pr, cur, :] * a_o + l_b * a_b
            acc_scr[pr, cur, :] = acc_scr[pr, cur, :] * a_o + acc_b * a_b


def _dil_kernel(qkv_ref, bias_ref, o_ref, m_scr, l_scr, acc_scr):
    S = o_ref.shape[1]
    for g, (window, d) in enumerate(DIL_GROUPS):
        nb = S // window

        def body(it, carry, g=g, d=d, nb=nb):
            _dil_block(qkv_ref, bias_ref, m_scr, l_scr, acc_scr, g, d, it // nb, it % nb)
            return carry
        lax.fori_loop(0, d * nb, body, 0, unroll=4)
    for pr in range(2):
        o_ref[0, :, pr * LANES:(pr + 1) * LANES] = (acc_scr[pr] / l_scr[pr]).astype(BF16)


def _dil(slabs, bias):
    B, _, S, _ = slabs.shape
    for window, d in DIL_GROUPS:
        assert window // d == DIL_L and S % window == 0
    stat = pltpu.VMEM((2, S, LANES), F32)
    return pl.pallas_call(
        _dil_kernel,
        grid=(B,),
        in_specs=[pl.BlockSpec((1, N_SLABS, S, LANES), lambda b: (b, 0, 0, 0)),
                  pl.BlockSpec(bias.shape, lambda b: (0, 0, 0, 0))],
        out_specs=pl.BlockSpec((1, S, DIL_OUT_WIDTH), lambda b: (b, 0, 0)),
        out_shape=jax.ShapeDtypeStruct((B, S, DIL_OUT_WIDTH), BF16),
        scratch_shapes=[stat, stat, stat],
        compiler_params=pltpu.CompilerParams(vmem_limit_bytes=VMEM_LIMIT),
        name="dil_attn",
    )(slabs, bias)


def _outproj_kernel(x_ref, ya_ref, yd_ref, ga_ref, gb_ref,
                    g1_ref, sc_ref, sh_ref, ng_ref, wa_ref, wb_ref, wo_ref, wr_ref, br_ref,
                    x1_ref, h2_ref, lg_ref):
    a = _dot(ya_ref[...], wa_ref[...])
    bm = _dot(yd_ref[...], wb_ref[...])
    merged = ga_ref[...].astype(F32) * a + gb_ref[...].astype(F32) * bm
    out = _dot(merged.astype(BF16), wo_ref[...])
    x1 = x_ref[...] + g1_ref[0] * out
    x1_ref[...] = x1
    ms = jnp.mean(x1 * x1, axis=-1, keepdims=True)
    h = x1 * lax.rsqrt(ms + EPS) * ng_ref[...]
    h = h * (1.0 + sc_ref[0]) + sh_ref[0]
    hb = h.astype(BF16)
    h2_ref[...] = hb
    lg_ref[...] = _dot(hb, wr_ref[...]) + br_ref[...]


def _outproj(x2d, ya2d, yd2d, p2d, g1, sc2, sh2, norm_g, wa, wb, wo, wr, br, S):
    T = x2d.shape[0]
    tm = TM_PROJ
    per_b = S // tm
    row = lambda w: pl.BlockSpec((tm, w), lambda i: (i, 0))
    full = lambda a: pl.BlockSpec(a.shape, lambda i: (0,) * a.ndim)
    mod = pl.BlockSpec((1, 1, D_MODEL), lambda i: (i // per_b, 0, 0))
    return pl.pallas_call(
        _outproj_kernel,
        grid=(T // tm,),
        in_specs=[row(D_MODEL), row(FOX_WIDTH), row(DIL_OUT_WIDTH)]
                 + [pl.BlockSpec((tm, D_MODEL), lambda i: (i, U_GATE_A * UNIT // D_MODEL)),
                    pl.BlockSpec((tm, D_MODEL), lambda i: (i, U_GATE_B * UNIT // D_MODEL)),
                    mod, mod, mod, full(norm_g), full(wa), full(wb), full(wo), full(wr), full(br)],
        out_specs=[row(D_MODEL), row(D_MODEL), row(LANES)],
        out_shape=[jax.ShapeDtypeStruct((T, D_MODEL), F32),
                   jax.ShapeDtypeStruct((T, D_MODEL), BF16),
                   jax.ShapeDtypeStruct((T, LANES), F32)],
        compiler_params=pltpu.CompilerParams(vmem_limit_bytes=VMEM_LIMIT),
        name="out_proj",
    )(x2d, ya2d, yd2d, p2d, p2d, g1, sc2, sh2, norm_g, wa, wb, wo, wr, br)


def _dispatch_kernel(lg_ref, h_ref, tri_ref, xs_ref, cnt_ref, pos_ref):
    tt = lg_ref.shape[0]
    lt = lg_ref[...].T
    row = lambda i: lt[i:i + 1, :]
    neg = -jnp.inf
    g = [row(i) for i in range(N_GROUPS)]
    gmax = functools.reduce(jnp.maximum, g)
    gidx = jnp.full(gmax.shape, N_GROUPS - 1, jnp.int32)
    for i in reversed(range(N_GROUPS - 1)):
        gidx = jnp.where(g[i] == gmax, i, gidx)
    gsum = sum(jnp.exp(gi - gmax) for gi in g)
    el = []
    for j in range(EXPERTS_PER_GROUP):
        v = row(N_GROUPS + EXPERTS_PER_GROUP * (N_GROUPS - 1) + j)
        for gg in reversed(range(N_GROUPS - 1)):
            v = jnp.where(gidx == gg, row(N_GROUPS + EXPERTS_PER_GROUP * gg + j), v)
        el.append(v)

    def top(vals):
        best = functools.reduce(jnp.maximum, vals)
        idx = jnp.full(best.shape, EXPERTS_PER_GROUP - 1, jnp.int32)
        for j in reversed(range(EXPERTS_PER_GROUP - 1)):
            idx = jnp.where(vals[j] == best, j, idx)
        return best, idx

    v1, i1 = top(el)
    v2, i2 = top([jnp.where(i1 == j, neg, el[j]) for j in range(EXPERTS_PER_GROUP)])
    t = jnp.exp(v2 - v1)
    den = (1.0 + t) * gsum
    wts = [1.0 / den, t / den]
    eid = [gidx * EXPERTS_PER_GROUP + i1, gidx * EXPERTS_PER_GROUP + i2]

    esub = lax.broadcasted_iota(jnp.int32, (N_EXPERTS, tt), 0)
    ohf = jnp.concatenate([jnp.where(esub == eid[k], 1.0, 0.0) for k in range(2)], axis=1)
    res = _dot(ohf.astype(BF16), tri_ref[...])
    prefix, cnt = res[:, :2 * tt], res[:, 2 * tt:]
    cnt = (((cnt.astype(jnp.int32) + (SUBLANES - 1)) // SUBLANES) * SUBLANES).astype(F32)
    esub_c = lax.broadcasted_iota(jnp.int32, cnt.shape, 0)
    start = jnp.zeros_like(cnt)
    for e in range(N_EXPERTS - 1):
        start = start + jnp.where(esub_c > e, cnt[e:e + 1, :], 0.0)
    start_w = jnp.concatenate([start] * (2 * tt // LANES), axis=1)
    pos = jnp.sum(ohf * (start_w + prefix), axis=0, keepdims=True)
    pos_k = [pos[:, :tt], pos[:, tt:]]

    n_rows = xs_ref.shape[0]
    psub = lax.broadcasted_iota(jnp.int32, (n_rows, tt), 0).astype(F32)
    pm = [jnp.where(psub == pos_k[k], 1.0, 0.0).astype(BF16) for k in range(2)]
    xs = _dot(pm[0] + pm[1], h_ref[...])
    wsub = lax.broadcasted_iota(jnp.int32, (LANES, tt), 0)
    ws = jnp.zeros((n_rows, LANES), F32)
    for k in range(2):
        parts = _split3(wts[k])
        wrows = jnp.zeros((LANES, tt), F32)
        for j in range(3):
            wrows = jnp.where(wsub == j, parts[j].astype(F32), wrows)
        ws = ws + _dot_nt(pm[k], wrows.astype(BF16))
    half = D_MODEL // 2
    xs_ref[:, :half] = _pack_bf16_pair(xs[:, :half], xs[:, half:])
    xs_ref[:, half:] = pltpu.bitcast(ws, jnp.uint32)
    cnt_ref[0] = cnt.astype(jnp.int32)
    posr = jnp.where(wsub == 0, pos_k[0], jnp.where(wsub == 1, pos_k[1], 0.0))
    pos_ref[...] = posr.T


def _dispatch(logits, h2):
    T = logits.shape[0]
    tt = MOE_TILE
    n_tiles = T // tt
    tri = np.concatenate([np.triu(np.ones((2 * tt, 2 * tt)), 1), np.ones((2 * tt, LANES))], axis=1)
    return pl.pallas_call(
        _dispatch_kernel,
        grid=(n_tiles,),
        in_specs=[pl.BlockSpec((tt, LANES), lambda i: (i, 0)),
                  pl.BlockSpec((tt, D_MODEL), lambda i: (i, 0)),
                  pl.BlockSpec(tri.shape, lambda i: (0, 0))],
        out_specs=[pl.BlockSpec((TILE_ROWS, XS_WIDTH), lambda i: (i, 0)),
                   pl.BlockSpec((1, N_EXPERTS, LANES), lambda i: (i, 0, 0)),
                   pl.BlockSpec((tt, LANES), lambda i: (i, 0))],
        out_shape=[jax.ShapeDtypeStruct((n_tiles * TILE_ROWS, XS_WIDTH), jnp.uint32),
                   jax.ShapeDtypeStruct((n_tiles, N_EXPERTS, LANES), jnp.int32),
                   jax.ShapeDtypeStruct((T, LANES), F32)],
        compiler_params=pltpu.CompilerParams(vmem_limit_bytes=VMEM_LIMIT),
        name="moe_dispatch",
    )(logits, h2, jnp.asarray(tri, BF16))


def _plan_kernel(cnt_ref, be_ref, nv_ref, grp_ref, used_ref, cs_ref):
    n_tiles = cnt_ref.shape[0]
    n_blk = be_ref.shape[0]
    rows = MOE_ROWS
    row_shift = rows.bit_length() - 1
    grp_shift = SUBLANES.bit_length() - 1
    assert rows == 1 << row_shift and SUBLANES == 1 << grp_shift

    def tile_starts(t, c):
        def per_e(e, acc):
            cs_ref[t * N_EXPERTS + e] = acc
            return acc + cnt_ref[t, e]
        used_ref[t] = lax.fori_loop(0, N_EXPERTS, per_e, 0, unroll=8)
        return c
    lax.fori_loop(0, n_tiles, tile_starts, 0)

    def clear(b, c):
        nv_ref[b] = 0
        return c
    lax.fori_loop(0, n_blk, clear, 0)

    def clear_groups(g, c):
        grp_ref[g] = 0
        return c
    lax.fori_loop(0, n_blk * MOE_GROUPS, clear_groups, 0, unroll=8)

    def per_expert(e, b):
        g0 = b * MOE_GROUPS

        def per_tile(t, tot):
            c = cnt_ref[t, e]
            src = t * TILE_ROWS + cs_ref[t * N_EXPERTS + e]
            first = g0 + lax.shift_right_logical(tot, grp_shift)

            def per_group(k, cc):
                grp_ref[first + k] = src + k * SUBLANES
                return cc
            lax.fori_loop(0, lax.shift_right_logical(c, grp_shift), per_group, 0)
            return tot + c
        tot = lax.fori_loop(0, n_tiles, per_tile, 0)

        def per_block(j, c):
            be_ref[b + j] = e
            nv_ref[b + j] = jnp.minimum(rows, tot - j * rows)
            return c
        nb = lax.shift_right_logical(tot + rows - 1, row_shift)
        lax.fori_loop(0, nb, per_block, 0)
        return b + nb
    n_used = lax.fori_loop(0, N_EXPERTS, per_expert, 0)

    def unused(b, c):
        be_ref[b] = be_ref[n_used - 1]
        return c
    lax.fori_loop(n_used, n_blk, unused, 0)


def _plan(cnt, n_blk):
    n_tiles = cnt.shape[0]
    smem = pl.BlockSpec(memory_space=pltpu.SMEM)
    i32 = lambda n: jax.ShapeDtypeStruct((n,), jnp.int32)
    return pl.pallas_call(
        _plan_kernel,
        in_specs=[smem],
        out_specs=[smem] * 4,
        out_shape=[i32(n_blk), i32(n_blk), i32(n_blk * MOE_GROUPS), i32(n_tiles)],
        scratch_shapes=[pltpu.SMEM((n_tiles * N_EXPERTS,), jnp.int32)],
        name="moe_plan",
    )(cnt)


def _pow2_pieces(n, fn):
    for b in reversed(range(SUBLANES.bit_length() - 1, MOE_ROWS.bit_length())):
        size = 1 << b

        @pl.when((n & size) != 0)
        def _():
            fn((n >> (b + 1)) << (b + 1), size)


def _moe_kernel(be_ref, nv_ref, grp_ref, used_ref,
                w1_ref, w3_ref, w2_ref, xs_hbm, ys_hbm, xbuf, ybuf, wb1, wb3, wb2, gsem, ssem):
    i = pl.program_id(0)
    last = pl.num_programs(0) - 1
    slot = i % 2
    nv = nv_ref[i]
    half = D_MODEL // 2
    grp_shift = SUBLANES.bit_length() - 1

    def group_row(blk, g):
        return pl.multiple_of(grp_ref[blk * MOE_GROUPS + g], SUBLANES)

    def gather(blk, s):
        top = jnp.maximum(lax.shift_right_logical(nv_ref[blk], grp_shift) - 1, 0)
        for g in range(MOE_GROUPS):
            src = group_row(blk, jnp.minimum(g, top))
            pltpu.make_async_copy(xs_hbm.at[pl.ds(src, SUBLANES)],
                                  xbuf.at[s, pl.ds(g * SUBLANES, SUBLANES)], gsem.at[s]).start()

    def wait_gather(s):
        pltpu.make_async_copy(xs_hbm.at[pl.ds(0, MOE_ROWS)], xbuf.at[s], gsem.at[s]).wait()

    def scatter(blk, s):
        def body(g, c):
            r = pl.multiple_of(g * SUBLANES, SUBLANES)
            pltpu.make_async_copy(ybuf.at[s, pl.ds(r, SUBLANES)],
                                  ys_hbm.at[pl.ds(group_row(blk, g), SUBLANES)], ssem.at[s]).start()
            return c
        lax.fori_loop(0, lax.shift_right_logical(nv_ref[blk], grp_shift), body, 0)

    def wait_scatter(s, count):
        _pow2_pieces(count, lambda a, size: pltpu.make_async_copy(
            ybuf.at[s, pl.ds(0, size)], ys_hbm.at[pl.ds(0, size)], ssem.at[s]).wait())

    @pl.when(i == 0)
    def _():
        @pl.when(nv > 0)
        def _():
            gather(0, 0)

        ybuf[1] = jnp.zeros(ybuf.shape[1:], ybuf.dtype)
        n_tiles = used_ref.shape[0]

        def fill(t, c):
            row0 = t * TILE_ROWS + used_ref[t]
            _pow2_pieces(TILE_ROWS - used_ref[t], lambda a, size: pltpu.make_async_copy(
                ybuf.at[1, pl.ds(0, size)], ys_hbm.at[pl.ds(pl.multiple_of(row0 + a, SUBLANES), size)],
                ssem.at[1]).start())
            return c
        lax.fori_loop(0, n_tiles, fill, 0)

        def drain(t, c):
            wait_scatter(1, TILE_ROWS - used_ref[t])
            return c
        lax.fori_loop(0, n_tiles, drain, 0)

    @pl.when(i >= 2)
    def _():
        wait_scatter(slot, nv_ref[jnp.maximum(i - 2, 0)])

    @pl.when((nv == 0) & (i > 0) & (nv_ref[jnp.maximum(i - 1, 0)] > 0))
    def _():
        wait_gather(slot)

    @pl.when(nv > 0)
    def _():
        e = be_ref[i]
        e_prev = be_ref[jnp.maximum(i - 1, 0)]

        @pl.when((i == 0) | (e != e_prev))
        def _():
            wb1[...] = w1_ref[0].astype(BF16)
            wb3[...] = w3_ref[0].astype(BF16)
            wb2[...] = w2_ref[0].astype(BF16)

        wait_gather(slot)
        gather(jnp.minimum(i + 1, last), 1 - slot)
        u = xbuf[slot]
        xa, xb = _unpack_bf16_pair(u[:, :half])
        wv = pltpu.bitcast(u[:, half:], F32)
        roww = wv[:, 0:1] + wv[:, 1:2] + wv[:, 2:3]
        a = _dot(xa, wb1[:half, :]) + _dot(xb, wb1[half:, :])
        b = _dot(xa, wb3[:half, :]) + _dot(xb, wb3[half:, :])
        hmid = (a * jax.nn.sigmoid(a) * b).astype(BF16)
        y = _dot(hmid, wb2[...]) * roww
        ybuf[slot] = _pack_bf16_pair(y[:, :half], y[:, half:])
        scatter(i, slot)

    @pl.when(i == last)
    def _():
        @pl.when(nv > 0)
        def _():
            wait_gather(1 - slot)

        @pl.when(last >= 1)
        def _():
            wait_scatter(1 - slot, nv_ref[jnp.maximum(last - 1, 0)])
        wait_scatter(slot, nv)


def _moe(xs, plan, w1, w3, w2):
    n_blk = plan[0].shape[0]
    rows = MOE_ROWS
    half = D_MODEL // 2
    wspec = lambda shape: pl.BlockSpec((1,) + shape, lambda i, be, *_: (be[i], 0, 0))
    grid_spec = pltpu.PrefetchScalarGridSpec(
        num_scalar_prefetch=4,
        grid=(n_blk,),
        in_specs=[wspec((D_MODEL, EXPERT_HIDDEN)), wspec((D_MODEL, EXPERT_HIDDEN)),
                  wspec((EXPERT_HIDDEN, D_MODEL)), pl.BlockSpec(memory_space=pl.ANY)],
        out_specs=pl.BlockSpec(memory_space=pl.ANY),
        scratch_shapes=[pltpu.VMEM((2, rows, XS_WIDTH), jnp.uint32),
                        pltpu.VMEM((2, rows, half), jnp.uint32),
                        pltpu.VMEM((D_MODEL, EXPERT_HIDDEN), BF16),
                        pltpu.VMEM((D_MODEL, EXPERT_HIDDEN), BF16),
                        pltpu.VMEM((EXPERT_HIDDEN, D_MODEL), BF16),
                        pltpu.SemaphoreType.DMA((2,)),
                        pltpu.SemaphoreType.DMA((2,))])
    return pl.pallas_call(
        _moe_kernel,
        grid_spec=grid_spec,
        out_shape=jax.ShapeDtypeStruct((xs.shape[0], half), jnp.uint32),
        compiler_params=pltpu.CompilerParams(dimension_semantics=("arbitrary",),
                                             vmem_limit_bytes=VMEM_LIMIT),
        name="moe_ffn",
    )(*plan, w1, w3, w2, xs)


def _combine_kernel(x1_ref, ys_ref, pos_ref, g2_ref, o_ref):
    tt = x1_ref.shape[0]
    half = D_MODEL // 2
    pos = pos_ref[...]
    pcol = lax.broadcasted_iota(jnp.int32, (tt, ys_ref.shape[0]), 1).astype(F32)
    sel = (jnp.where(pcol == pos[:, 0:1], 1.0, 0.0) + jnp.where(pcol == pos[:, 1:2], 1.0, 0.0))
    sel = sel.astype(BF16)
    lo, hi = _unpack_bf16_pair(ys_ref[...])
    g2 = g2_ref[0]
    x1 = x1_ref[...]
    o_ref[:, :half] = x1[:, :half] + g2[:, :half] * _dot(sel, lo)
    o_ref[:, half:] = x1[:, half:] + g2[:, half:] * _dot(sel, hi)


def _combine(x1, ys, pos, g2, S):
    T = x1.shape[0]
    tt = MOE_TILE
    per_b = S // tt
    return pl.pallas_call(
        _combine_kernel,
        grid=(T // tt,),
        in_specs=[pl.BlockSpec((tt, D_MODEL), lambda i: (i, 0)),
                  pl.BlockSpec((TILE_ROWS, D_MODEL // 2), lambda i: (i, 0)),
                  pl.BlockSpec((tt, LANES), lambda i: (i, 0)),
                  pl.BlockSpec((1, 1, D_MODEL), lambda i: (i // per_b, 0, 0))],
        out_specs=pl.BlockSpec((tt, D_MODEL), lambda i: (i, 0)),
        out_shape=jax.ShapeDtypeStruct((T, D_MODEL), F32),
        compiler_params=pltpu.CompilerParams(vmem_limit_bytes=VMEM_LIMIT),
        name="moe_combine",
    )(x1, ys, pos, g2)


def _prep_w_in(w_in):
    dq = w_in[:, OFF_DIL_Q:OFF_DIL_K]
    dk = w_in[:, OFF_DIL_K:OFF_DIL_V]
    dv = w_in[:, OFF_DIL_V:OFF_GATE_A]
    dil = []
    for g in range(N_DIL_GROUPS):
        cs = slice(g * DIL_OUT_WIDTH, (g + 1) * DIL_OUT_WIDTH)
        dil += [dq[:, cs], dk[:, cs], dv[:, cs]]
    pad = jnp.zeros((D_MODEL, UNIT - FOX_HEADS), w_in.dtype)
    cols = [w_in[:, OFF_GATE_A:OFF_GATE_B], w_in[:, OFF_GATE_B:N_IN],
            w_in[:, OFF_FOX_Q:OFF_FOX_K], w_in[:, OFF_FOX_K:OFF_FOX_V], w_in[:, OFF_FOX_V:OFF_FOX_F],
            *dil, w_in[:, OFF_FOX_F:OFF_DIL_Q], pad]
    return jnp.concatenate(cols, axis=1).astype(BF16)


def _prep_gain(q_gain, k_gain):
    qs = HEAD_DIM ** -0.5 * LOG2E
    ones = jnp.ones((UNIT,), F32)
    fq = q_gain[:FOX_HEADS].reshape(-1) * qs
    fk = k_gain[:FOX_HEADS].reshape(-1)
    dq = q_gain[FOX_HEADS:].reshape(-1) * qs
    dk = k_gain[FOX_HEADS:].reshape(-1)
    dil = []
    for g in range(N_DIL_GROUPS):
        cs = slice(g * DIL_OUT_WIDTH, (g + 1) * DIL_OUT_WIDTH)
        dil += [dq[cs], dk[cs], ones]
    parts = [ones] * 8 + [fq, fk, ones, ones] + dil + [ones]
    return jnp.concatenate(parts).reshape(1, N_UNITS * UNIT)


def _layer(x, mod, rel_bias_table, norm1_g, w_in, b_forget, q_gain, k_gain, w_branch_a, w_branch_b,
           w_out, norm2_g, w_rg, b_rg, w_re, b_re, w1, w3, w2):
    B, S, D = x.shape
    T = B * S
    sh1, sc1, g1, sh2, sc2, g2 = [m.reshape(B, 1, D) for m in jnp.split(mod, 6, axis=-1)]
    x2d = x.reshape(T, D)

    p2d, fgt, slabs = _inproj(x2d, norm1_g.reshape(1, D), sc1, sh1, _prep_w_in(w_in),
                              _prep_gain(q_gain, k_gain), S)
    p3 = p2d.reshape(B, S, P_WIDTH)
    ck = _fcum(fgt.reshape(B, S, LANES), b_forget)
    ya = _fox(p3, ck)
    yd = _dil(slabs, _relbias(rel_bias_table))

    n_router = N_GROUPS + N_EXPERTS
    wr = jnp.concatenate([w_rg, w_re, jnp.zeros((D, LANES - n_router), F32)], axis=1).astype(BF16)
    br = jnp.concatenate([b_rg, b_re, jnp.zeros((LANES - n_router,), F32)]).reshape(1, LANES)
    x1, h2, logits = _outproj(x2d, ya.reshape(T, FOX_WIDTH), yd.reshape(T, DIL_OUT_WIDTH), p2d,
                              g1, sc2, sh2, norm2_g.reshape(1, D),
                              w_branch_a.astype(BF16), w_branch_b.astype(BF16), w_out.astype(BF16),
                              wr, br, S)
    xs, cnt, pos = _dispatch(logits, h2)
    cnt2 = cnt[:, :, 0]
    n_blk = cnt.shape[0] * TILE_ROWS // MOE_ROWS + N_EXPERTS
    plan = _plan(cnt2, n_blk)
    ys = _moe(xs, plan, w1, w3, w2)
    out = _combine(x1, ys, pos, g2, S)
    return out.reshape(B, S, D)


def kernel(x, c, rel_bias_table, w_ada, b_ada, norm1_g, w_in, b_forget, q_gain, k_gain, w_branch_a, w_branch_b, w_out, norm2_g, w_router_group, b_router_group, w_router_expert, b_router_expert, w1, w3, w2):
    depth = w_ada.shape[0]
    for l in range(depth):
        mod = _ada(c, w_ada[l], b_ada[l])
        x = _layer(x, mod, rel_bias_table, norm1_g[l], w_in[l], b_forget[l], q_gain[l], k_gain[l],
                   w_branch_a[l], w_branch_b[l], w_out[l], norm2_g[l], w_router_group[l],
                   b_router_group[l], w_router_expert[l], b_router_expert[l], w1[l], w3[l], w2[l])
    return x
```

```python
import functools
import math

import numpy as np
import jax
import jax.numpy as jnp
from jax import lax
from jax.experimental import pallas as pl
from jax.experimental.pallas import tpu as pltpu

F32 = jnp.float32
BF16 = jnp.bfloat16

D_MODEL = 1024
HEAD_DIM = 64
FOX_HEADS = 8
DIL_GROUPS = ((128, 1), (512, 4), (2048, 16))
DIL_HEADS_PER_GROUP = 4
N_DIL_GROUPS = len(DIL_GROUPS)
DIL_HEADS = N_DIL_GROUPS * DIL_HEADS_PER_GROUP
FOX_WIDTH = FOX_HEADS * HEAD_DIM
DIL_WIDTH = DIL_HEADS * HEAD_DIM
DIL_OUT_WIDTH = DIL_HEADS_PER_GROUP * HEAD_DIM
NUM_BUCKETS = 32
REL_MAX_DISTANCE = 2048
N_GROUPS = 4
EXPERTS_PER_GROUP = 8
N_EXPERTS = N_GROUPS * EXPERTS_PER_GROUP
EXPERT_HIDDEN = D_MODEL // 2
EPS = 1e-6
LOG2E = math.log2(math.e)

OFF_FOX_Q = 0
OFF_FOX_K = OFF_FOX_Q + FOX_WIDTH
OFF_FOX_V = OFF_FOX_K + FOX_WIDTH
OFF_FOX_F = OFF_FOX_V + FOX_WIDTH
OFF_DIL_Q = OFF_FOX_F + FOX_HEADS
OFF_DIL_K = OFF_DIL_Q + DIL_WIDTH
OFF_DIL_V = OFF_DIL_K + DIL_WIDTH
OFF_GATE_A = OFF_DIL_V + DIL_WIDTH
OFF_GATE_B = OFF_GATE_A + D_MODEL
N_IN = OFF_GATE_B + D_MODEL

LANES = 128
UNIT = 256
DIL_L = 128

U_GATE_A, U_GATE_B, U_FOX_Q, U_FOX_K, U_FOX_V, U_DIL, U_FORGET = 0, 4, 8, 10, 12, 14, 23
N_UNITS = 24
P_WIDTH = U_DIL * UNIT
N_SLABS = 3 * N_DIL_GROUPS
_KIND = (["gate"] * 8 + ["normq"] * 2 + ["normk"] * 2 + ["plain"] * 2
         + ["normq", "normk", "plain"] * 3 + ["forget"])

TM_INPROJ = 1024
TM_PROJ = 512
TQ_FOX = 512
MOE_ROWS = 256
MOE_TILE = 512
XS_WIDTH = D_MODEL // 2 + LANES
SUBLANES = 8
TILE_ROWS = 2 * MOE_TILE + N_EXPERTS * SUBLANES
MOE_GROUPS = MOE_ROWS // SUBLANES
VMEM_LIMIT = 56 * 1024 * 1024


def _dot(a, b):
    return jnp.dot(a, b, preferred_element_type=F32)


def _dot_nt(a, b):
    return lax.dot_general(a, b, (((1,), (1,)), ((), ())), preferred_element_type=F32)


def _split3(x):
    hi = x.astype(BF16)
    r1 = x - hi.astype(F32)
    mid = r1.astype(BF16)
    lo = (r1 - mid.astype(F32)).astype(BF16)
    return hi, mid, lo


def _ada_kernel(c_ref, w_ref, b_ref, o_ref):
    c = c_ref[...]
    s = c * jax.nn.sigmoid(c)
    s_hi = s.astype(BF16)
    s_lo = (s - s_hi.astype(F32)).astype(BF16)
    w = w_ref[...]
    w_hi = w.astype(BF16)
    w_lo = (w - w_hi.astype(F32)).astype(BF16)
    acc = _dot(s_hi, w_hi) + _dot(s_hi, w_lo) + _dot(s_lo, w_hi)
    o_ref[...] = acc + b_ref[...]


def _ada(c, w_ada, b_ada):
    B = c.shape[0]
    n_out = w_ada.shape[1]
    tn = 512
    return pl.pallas_call(
        _ada_kernel,
        grid=(n_out // tn,),
        in_specs=[pl.BlockSpec((B, D_MODEL), lambda j: (0, 0)),
                  pl.BlockSpec((D_MODEL, tn), lambda j: (0, j)),
                  pl.BlockSpec((1, tn), lambda j: (0, j))],
        out_specs=pl.BlockSpec((B, tn), lambda j: (0, j)),
        out_shape=jax.ShapeDtypeStruct((B, n_out), F32),
        name="ada_mod",
    )(c, w_ada, b_ada.reshape(1, n_out))


def _pack_bf16_pair(lo, hi):
    lo_bits = pltpu.bitcast(lo.astype(BF16).astype(F32), jnp.uint32) >> 16
    hi_bits = pltpu.bitcast(hi.astype(BF16).astype(F32), jnp.uint32) & jnp.uint32(0xFFFF0000)
    return lo_bits | hi_bits


def _unpack_bf16_pair(u):
    lo = pltpu.bitcast(u << 16, F32).astype(BF16)
    hi = pltpu.bitcast(u & jnp.uint32(0xFFFF0000), F32).astype(BF16)
    return lo, hi


def _inproj_kernel(x_ref, g_ref, sc_ref, sh_ref, w_ref, gain_ref, bd_ref, p_ref, f_ref, s_ref):
    x = x_ref[...]
    ms = jnp.mean(x * x, axis=-1, keepdims=True)
    h = x * lax.rsqrt(ms + EPS) * g_ref[...]
    h = h * (1.0 + sc_ref[0]) + sh_ref[0]
    hb = h.astype(BF16)

    def unit(u):
        cols = slice(u * UNIT, (u + 1) * UNIT)
        acc = _dot(hb, w_ref[:, cols])
        kind = _KIND[u]
        if kind == "gate":
            return jax.nn.sigmoid(acc)
        if kind in ("normq", "normk"):
            ss = _dot((acc * acc).astype(BF16), bd_ref[...])
            return acc * lax.rsqrt(ss * (1.0 / HEAD_DIM) + EPS) * gain_ref[:, cols]
        return acc

    for u in range(U_DIL):
        p_ref[:, u * UNIT:(u + 1) * UNIT] = unit(u).astype(BF16)
    for g in range(N_DIL_GROUPS):
        q, k, v = (unit(U_DIL + 3 * g + j) for j in range(3))
        s_ref[0, 3 * g] = _pack_bf16_pair(q[:, :LANES], k[:, :LANES])
        s_ref[0, 3 * g + 1] = _pack_bf16_pair(q[:, LANES:], k[:, LANES:])
        s_ref[0, 3 * g + 2] = _pack_bf16_pair(v[:, :LANES], v[:, LANES:])
    f_ref[...] = unit(U_FORGET)[:, :LANES]


def _inproj(x2d, norm_g, sc, sh, w_re, gain_row, S):
    T = x2d.shape[0]
    tm = TM_INPROJ
    per_b = S // tm
    bd = np.kron(np.eye(UNIT // HEAD_DIM), np.ones((HEAD_DIM, HEAD_DIM))).astype(np.float32)
    once = dict(pipeline_mode=pl.Buffered(1))
    return pl.pallas_call(
        _inproj_kernel,
        grid=(T // tm,),
        in_specs=[pl.BlockSpec((tm, D_MODEL), lambda i: (i, 0)),
                  pl.BlockSpec((1, D_MODEL), lambda i: (0, 0)),
                  pl.BlockSpec((1, 1, D_MODEL), lambda i: (i // per_b, 0, 0)),
                  pl.BlockSpec((1, 1, D_MODEL), lambda i: (i // per_b, 0, 0)),
                  pl.BlockSpec((D_MODEL, N_UNITS * UNIT), lambda i: (0, 0), **once),
                  pl.BlockSpec((1, N_UNITS * UNIT), lambda i: (0, 0), **once),
                  pl.BlockSpec((UNIT, UNIT), lambda i: (0, 0), **once)],
        out_specs=[pl.BlockSpec((tm, P_WIDTH), lambda i: (i, 0)),
                   pl.BlockSpec((tm, LANES), lambda i: (i, 0)),
                   pl.BlockSpec((1, N_SLABS, tm, LANES), lambda i: (i // per_b, 0, i % per_b, 0))],
        out_shape=[jax.ShapeDtypeStruct((T, P_WIDTH), BF16),
                   jax.ShapeDtypeStruct((T, LANES), F32),
                   jax.ShapeDtypeStruct((T // S, N_SLABS, S, LANES), jnp.uint32)],
        compiler_params=pltpu.CompilerParams(vmem_limit_bytes=VMEM_LIMIT),
        name="in_proj",
    )(x2d, norm_g, sc, sh, w_re, gain_row, jnp.asarray(bd, BF16))


def _fcum_kernel(f_ref, b_ref, tri_ref, o_ref):
    S = f_ref.shape[1]
    xf = f_ref[0] + b_ref[...]
    ls = (jnp.minimum(xf, 0.0) - jnp.log(1.0 + jnp.exp(-jnp.abs(xf)))) * LOG2E
    lst = ls.T
    carry = jnp.zeros((LANES, UNIT), F32)
    for blk in range(S // UNIT):
        seg = lst[:, blk * UNIT:(blk + 1) * UNIT]
        hi, mid, lo = _split3(seg)
        tri = tri_ref[...]
        res = _dot(hi, tri) + _dot(mid, tri) + _dot(lo, tri)
        o_ref[0, :, blk * UNIT:(blk + 1) * UNIT] = (res[:, :UNIT] + carry)[:FOX_HEADS]
        carry = carry + res[:, UNIT:]


def _fcum(fgt, b_forget):
    B, S, _ = fgt.shape
    brow = jnp.zeros((1, LANES), F32).at[0, :FOX_HEADS].set(b_forget)
    tri = np.concatenate([np.triu(np.ones((UNIT, UNIT))), np.ones((UNIT, UNIT))], axis=1)
    return pl.pallas_call(
        _fcum_kernel,
        grid=(B,),
        in_specs=[pl.BlockSpec((1, S, LANES), lambda b: (b, 0, 0)),
                  pl.BlockSpec((1, LANES), lambda b: (0, 0)),
                  pl.BlockSpec((UNIT, 2 * UNIT), lambda b: (0, 0))],
        out_specs=pl.BlockSpec((1, FOX_HEADS, S), lambda b: (b, 0, 0)),
        out_shape=jax.ShapeDtypeStruct((B, FOX_HEADS, S), F32),
        name="forget_cumsum",
    )(fgt, brow, jnp.asarray(tri, BF16))


def _fox_kernel(q_ref, k_ref, v_ref, ck_ref, o_ref):
    S = q_ref.shape[1]
    pair = pl.program_id(1)
    tq = TQ_FOX
    lane = lax.broadcasted_iota(jnp.int32, (1, LANES), 1)
    row = lax.broadcasted_iota(jnp.int32, (tq, tq), 0)
    col = lax.broadcasted_iota(jnp.int32, (tq, tq), 1)
    causal = col <= row
    cks = [ck_ref[0, pl.ds(2 * pair + hh, 1), :] for hh in range(2)]
    for t in range(S // tq):
        r0, r1 = t * tq, (t + 1) * tq
        qt = q_ref[0, r0:r1, :]
        outs = []
        for hh in range(2):
            hsel = (lane >= HEAD_DIM) == bool(hh)
            qm = jnp.where(hsel, qt, jnp.zeros_like(qt))
            ck = cks[hh]
            s_d = _dot_nt(qm, k_ref[0, r0:r1, :]) - ck[:, r0:r1]
            s_d = jnp.where(causal, s_d, -jnp.inf)
            m = jnp.max(s_d, axis=-1, keepdims=True)
            if t > 0:
                s_o = _dot_nt(qm, k_ref[0, :r0, :]) - ck[:, :r0]
                m = jnp.maximum(m, jnp.max(s_o, axis=-1, keepdims=True))
            p_d = jnp.exp2(s_d - m)
            l = jnp.sum(p_d, axis=-1, keepdims=True)
            acc = _dot(p_d.astype(BF16), v_ref[0, r0:r1, :])
            if t > 0:
                p_o = jnp.exp2(s_o - m)
                l = l + jnp.sum(p_o, axis=-1, keepdims=True)
                acc = acc + _dot(p_o.astype(BF16), v_ref[0, :r0, :])
            outs.append(acc / l)
        o_ref[0, r0:r1, :] = jnp.where(lane < HEAD_DIM, outs[0], outs[1]).astype(BF16)


def _fox(p3, ck):
    B, S, _ = p3.shape
    nq, nk, nv = (U_FOX_Q * UNIT // LANES, U_FOX_K * UNIT // LANES, U_FOX_V * UNIT // LANES)
    return pl.pallas_call(
        _fox_kernel,
        grid=(B, FOX_HEADS // 2),
        in_specs=[pl.BlockSpec((1, S, LANES), lambda b, p: (b, 0, nq + p)),
                  pl.BlockSpec((1, S, LANES), lambda b, p: (b, 0, nk + p)),
                  pl.BlockSpec((1, S, LANES), lambda b, p: (b, 0, nv + p)),
                  pl.BlockSpec((1, FOX_HEADS, S), lambda b, p: (b, 0, 0))],
        out_specs=pl.BlockSpec((1, S, LANES), lambda b, p: (b, 0, p)),
        out_shape=jax.ShapeDtypeStruct((B, S, FOX_WIDTH), BF16),
        compiler_params=pltpu.CompilerParams(vmem_limit_bytes=VMEM_LIMIT),
        name="fox_attn",
    )(p3, p3, p3, ck)


def _t5_bucket(dist):
    max_exact = NUM_BUCKETS // 2
    d = np.maximum(dist, 1).astype(np.float32)
    large = max_exact + (np.log(d / max_exact) / np.log(REL_MAX_DISTANCE / max_exact)
                         * (NUM_BUCKETS - max_exact)).astype(np.int32)
    large = np.minimum(large, NUM_BUCKETS - 1)
    return np.where(dist < max_exact, dist, large).astype(np.int32)


def _relbias_kernel(tab_ref, bucket_ref, valid_ref, o_ref):
    g = pl.program_id(0)
    bk = bucket_ref[0]
    vd = valid_ref[0]
    for hs in range(DIL_HEADS_PER_GROUP):
        acc = jnp.zeros(bk.shape, F32)
        for b in range(NUM_BUCKETS):
            acc = jnp.where(bk == b, tab_ref[b, g * DIL_HEADS_PER_GROUP + hs], acc)
        bias = jnp.where(vd != 0, acc * LOG2E, -jnp.inf)
        o_ref[0, hs] = bias
        col = lax.broadcasted_iota(jnp.int32, bias.shape, 1)
        o_ref[1, hs] = jnp.where(col >= DIL_L, bias, -jnp.inf)


def _relbias(table):
    L = DIL_L
    i = np.arange(L)[:, None]
    j = np.arange(2 * L)[None, :]
    m = L + i - j
    valid = ((m >= 0) & (m <= L)).astype(np.int32)
    buckets = np.stack([_t5_bucket(np.clip(m, 0, None) * d) for _, d in DIL_GROUPS])
    valids = np.stack([valid] * N_DIL_GROUPS)
    return pl.pallas_call(
        _relbias_kernel,
        grid=(N_DIL_GROUPS,),
        in_specs=[pl.BlockSpec(memory_space=pltpu.SMEM),
                  pl.BlockSpec((1, L, 2 * L), lambda g: (g, 0, 0)),
                  pl.BlockSpec((1, L, 2 * L), lambda g: (g, 0, 0))],
        out_specs=pl.BlockSpec((2, DIL_HEADS_PER_GROUP, L, 2 * L), lambda g: (0, g, 0, 0)),
        out_shape=jax.ShapeDtypeStruct((2, DIL_HEADS, L, 2 * L), F32),
        name="rel_bias",
    )(table, jnp.asarray(buckets), jnp.asarray(valids))


def _dil_rows(start, d):
    return pl.ds(start, DIL_L) if d == 1 else pl.ds(start, DIL_L, stride=d)


def _dil_block(qkv_ref, bias_ref, m_scr, l_scr, acc_scr, g, d, r, n):
    L = DIL_L
    lane = lax.broadcasted_iota(jnp.int32, (1, LANES), 1)
    first = 1 - jnp.minimum(n, 1)
    cur = _dil_rows(r + d * (n * L), d)
    prev = _dil_rows(r + d * (jnp.maximum(n - 1, 0) * L), d)
    v_cur = _unpack_bf16_pair(qkv_ref[0, 3 * g + 2, cur, :])
    v_prev = _unpack_bf16_pair(qkv_ref[0, 3 * g + 2, prev, :])
    for pr in range(2):
        qt, k_cur = _unpack_bf16_pair(qkv_ref[0, 3 * g + pr, cur, :])
        _, k_prev = _unpack_bf16_pair(qkv_ref[0, 3 * g + pr, prev, :])
        kt = jnp.concatenate([k_prev, k_cur], axis=0)
        vt = jnp.concatenate([v_prev[pr], v_cur[pr]], axis=0)
        ms, ls, accs = [], [], []
        for hh in range(2):
            hsel = (lane >= HEAD_DIM) == bool(hh)
            qm = jnp.where(hsel, qt, jnp.zeros_like(qt))
            s = _dot_nt(qm, kt) + bias_ref[first, DIL_HEADS_PER_GROUP * g + 2 * pr + hh]
            m = jnp.max(s, axis=-1, keepdims=True)
            p = jnp.exp2(s - m)
            ms.append(m)
            ls.append(jnp.sum(p, axis=-1, keepdims=True))
            accs.append(_dot(p.astype(BF16), vt))
        low = lane < HEAD_DIM
        m_b = jnp.where(low, ms[0], ms[1])
        l_b = jnp.where(low, ls[0], ls[1])
        acc_b = jnp.where(low, accs[0], accs[1])
        if g == 0:
            m_scr[pr, cur, :] = m_b
            l_scr[pr, cur, :] = l_b
            acc_scr[pr, cur, :] = acc_b
        else:
            m_o = m_scr[pr, cur, :]
            m_n = jnp.maximum(m_o, m_b)
            a_o = jnp.exp2(m_o - m_n)
            a_b = jnp.exp2(m_b - m_n)
            m_scr[pr, cur, :] = m_n
            l_scr[pr, cur, :] = l_scr[pr, cur, :] * a_o + l_b * a_b
            acc_scr[pr, cur, :] = acc_scr[pr, cur, :] * a_o + acc_b * a_b


def _dil_kernel(qkv_ref, bias_ref, o_ref, m_scr, l_scr, acc_scr):
    S = o_ref.shape[1]
    for g, (window, d) in enumerate(DIL_GROUPS):
        nb = S // window

        def body(it, carry, g=g, d=d, nb=nb):
            _dil_block(qkv_ref, bias_ref, m_scr, l_scr, acc_scr, g, d, it // nb, it % nb)
            return carry
        lax.fori_loop(0, d * nb, body, 0, unroll=4)
    for pr in range(2):
        o_ref[0, :, pr * LANES:(pr + 1) * LANES] = (acc_scr[pr] / l_scr[pr]).astype(BF16)


def _dil(slabs, bias):
    B, _, S, _ = slabs.shape
    for window, d in DIL_GROUPS:
        assert window // d == DIL_L and S % window == 0
    stat = pltpu.VMEM((2, S, LANES), F32)
    return pl.pallas_call(
        _dil_kernel,
        grid=(B,),
        in_specs=[pl.BlockSpec((1, N_SLABS, S, LANES), lambda b: (b, 0, 0, 0)),
                  pl.BlockSpec(bias.shape, lambda b: (0, 0, 0, 0))],
        out_specs=pl.BlockSpec((1, S, DIL_OUT_WIDTH), lambda b: (b, 0, 0)),
        out_shape=jax.ShapeDtypeStruct((B, S, DIL_OUT_WIDTH), BF16),
        scratch_shapes=[stat, stat, stat],
        compiler_params=pltpu.CompilerParams(vmem_limit_bytes=VMEM_LIMIT),
        name="dil_attn",
    )(slabs, bias)


def _outproj_kernel(x_ref, ya_ref, yd_ref, ga_ref, gb_ref,
                    g1_ref, sc_ref, sh_ref, ng_ref, wa_ref, wb_ref, wo_ref, wr_ref, br_ref,
                    x1_ref, h2_ref, lg_ref):
    a = _dot(ya_ref[...], wa_ref[...])
    bm = _dot(yd_ref[...], wb_ref[...])
    merged = ga_ref[...].astype(F32) * a + gb_ref[...].astype(F32) * bm
    out = _dot(merged.astype(BF16), wo_ref[...])
    x1 = x_ref[...] + g1_ref[0] * out
    x1_ref[...] = x1
    ms = jnp.mean(x1 * x1, axis=-1, keepdims=True)
    h = x1 * lax.rsqrt(ms + EPS) * ng_ref[...]
    h = h * (1.0 + sc_ref[0]) + sh_ref[0]
    hb = h.astype(BF16)
    h2_ref[...] = hb
    lg_ref[...] = _dot(hb, wr_ref[...]) + br_ref[...]


def _outproj(x2d, ya2d, yd2d, p2d, g1, sc2, sh2, norm_g, wa, wb, wo, wr, br, S):
    T = x2d.shape[0]
    tm = TM_PROJ
    per_b = S // tm
    row = lambda w: pl.BlockSpec((tm, w), lambda i: (i, 0))
    full = lambda a: pl.BlockSpec(a.shape, lambda i: (0,) * a.ndim)
    mod = pl.BlockSpec((1, 1, D_MODEL), lambda i: (i // per_b, 0, 0))
    return pl.pallas_call(
        _outproj_kernel,
        grid=(T // tm,),
        in_specs=[row(D_MODEL), row(FOX_WIDTH), row(DIL_OUT_WIDTH)]
                 + [pl.BlockSpec((tm, D_MODEL), lambda i: (i, U_GATE_A * UNIT // D_MODEL)),
                    pl.BlockSpec((tm, D_MODEL), lambda i: (i, U_GATE_B * UNIT // D_MODEL)),
                    mod, mod, mod, full(norm_g), full(wa), full(wb), full(wo), full(wr), full(br)],
        out_specs=[row(D_MODEL), row(D_MODEL), row(LANES)],
        out_shape=[jax.ShapeDtypeStruct((T, D_MODEL), F32),
                   jax.ShapeDtypeStruct((T, D_MODEL), BF16),
                   jax.ShapeDtypeStruct((T, LANES), F32)],
        compiler_params=pltpu.CompilerParams(vmem_limit_bytes=VMEM_LIMIT),
        name="out_proj",
    )(x2d, ya2d, yd2d, p2d, p2d, g1, sc2, sh2, norm_g, wa, wb, wo, wr, br)


def _dispatch_kernel(lg_ref, h_ref, tri_ref, xs_ref, cnt_ref, pos_ref):
    tt = lg_ref.shape[0]
    lt = lg_ref[...].T
    row = lambda i: lt[i:i + 1, :]
    neg = -jnp.inf
    g = [row(i) for i in range(N_GROUPS)]
    gmax = functools.reduce(jnp.maximum, g)
    gidx = jnp.full(gmax.shape, N_GROUPS - 1, jnp.int32)
    for i in reversed(range(N_GROUPS - 1)):
        gidx = jnp.where(g[i] == gmax, i, gidx)
    gsum = sum(jnp.exp(gi - gmax) for gi in g)
    el = []
    for j in range(EXPERTS_PER_GROUP):
        v = row(N_GROUPS + EXPERTS_PER_GROUP * (N_GROUPS - 1) + j)
        for gg in reversed(range(N_GROUPS - 1)):
            v = jnp.where(gidx == gg, row(N_GROUPS + EXPERTS_PER_GROUP * gg + j), v)
        el.append(v)

    def top(vals):
        best = functools.reduce(jnp.maximum, vals)
        idx = jnp.full(best.shape, EXPERTS_PER_GROUP - 1, jnp.int32)
        for j in reversed(range(EXPERTS_PER_GROUP - 1)):
            idx = jnp.where(vals[j] == best, j, idx)
        return best, idx

    v1, i1 = top(el)
    v2, i2 = top([jnp.where(i1 == j, neg, el[j]) for j in range(EXPERTS_PER_GROUP)])
    t = jnp.exp(v2 - v1)
    den = (1.0 + t) * gsum
    wts = [1.0 / den, t / den]
    eid = [gidx * EXPERTS_PER_GROUP + i1, gidx * EXPERTS_PER_GROUP + i2]

    esub = lax.broadcasted_iota(jnp.int32, (N_EXPERTS, tt), 0)
    ohf = jnp.concatenate([jnp.where(esub == eid[k], 1.0, 0.0) for k in range(2)], axis=1)
    res = _dot(ohf.astype(BF16), tri_ref[...])
    prefix, cnt = res[:, :2 * tt], res[:, 2 * tt:]
    cnt = (((cnt.astype(jnp.int32) + (SUBLANES - 1)) // SUBLANES) * SUBLANES).astype(F32)
    esub_c = lax.broadcasted_iota(jnp.int32, cnt.shape, 0)
    start = jnp.zeros_like(cnt)
    for e in range(N_EXPERTS - 1):
        start = start + jnp.where(esub_c > e, cnt[e:e + 1, :], 0.0)
    start_w = jnp.concatenate([start] * (2 * tt // LANES), axis=1)
    pos = jnp.sum(ohf * (start_w + prefix), axis=0, keepdims=True)
    pos_k = [pos[:, :tt], pos[:, tt:]]

    n_rows = xs_ref.shape[0]
    psub = lax.broadcasted_iota(jnp.int32, (n_rows, tt), 0).astype(F32)
    pm = [jnp.where(psub == pos_k[k], 1.0, 0.0).astype(BF16) for k in range(2)]
    xs = _dot(pm[0] + pm[1], h_ref[...])
    wsub = lax.broadcasted_iota(jnp.int32, (LANES, tt), 0)
    ws = jnp.zeros((n_rows, LANES), F32)
    for k in range(2):
        parts = _split3(wts[k])
        wrows = jnp.zeros((LANES, tt), F32)
        for j in range(3):
            wrows = jnp.where(wsub == j, parts[j].astype(F32), wrows)
        ws = ws + _dot_nt(pm[k], wrows.astype(BF16))
    half = D_MODEL // 2
    xs_ref[:, :half] = _pack_bf16_pair(xs[:, :half], xs[:, half:])
    xs_ref[:, half:] = pltpu.bitcast(ws, jnp.uint32)
    cnt_ref[0] = cnt.astype(jnp.int32)
    posr = jnp.where(wsub == 0, pos_k[0], jnp.where(wsub == 1, pos_k[1], 0.0))
    pos_ref[...] = posr.T


def _dispatch(logits, h2):
    T = logits.shape[0]
    tt = MOE_TILE
    n_tiles = T // tt
    tri = np.concatenate([np.triu(np.ones((2 * tt, 2 * tt)), 1), np.ones((2 * tt, LANES))], axis=1)
    return pl.pallas_call(
        _dispatch_kernel,
        grid=(n_tiles,),
        in_specs=[pl.BlockSpec((tt, LANES), lambda i: (i, 0)),
                  pl.BlockSpec((tt, D_MODEL), lambda i: (i, 0)),
                  pl.BlockSpec(tri.shape, lambda i: (0, 0))],
        out_specs=[pl.BlockSpec((TILE_ROWS, XS_WIDTH), lambda i: (i, 0)),
                   pl.BlockSpec((1, N_EXPERTS, LANES), lambda i: (i, 0, 0)),
                   pl.BlockSpec((tt, LANES), lambda i: (i, 0))],
        out_shape=[jax.ShapeDtypeStruct((n_tiles * TILE_ROWS, XS_WIDTH), jnp.uint32),
                   jax.ShapeDtypeStruct((n_tiles, N_EXPERTS, LANES), jnp.int32),
                   jax.ShapeDtypeStruct((T, LANES), F32)],
        compiler_params=pltpu.CompilerParams(vmem_limit_bytes=VMEM_LIMIT),
        name="moe_dispatch",
    )(logits, h2, jnp.asarray(tri, BF16))


def _plan_kernel(cnt_ref, be_ref, nv_ref, nxt_ref, grp_ref, used_ref, cs_ref):
    n_tiles = cnt_ref.shape[0]
    n_blk = be_ref.shape[0]
    rows = MOE_ROWS
    row_shift = rows.bit_length() - 1
    grp_shift = SUBLANES.bit_length() - 1
    assert rows == 1 << row_shift and SUBLANES == 1 << grp_shift

    def tile_starts(t, c):
        def per_e(e, acc):
            cs_ref[t * N_EXPERTS + e] = acc
            return acc + cnt_ref[t, e]
        used_ref[t] = lax.fori_loop(0, N_EXPERTS, per_e, 0, unroll=8)
        return c
    lax.fori_loop(0, n_tiles, tile_starts, 0)

    def clear(b, c):
        nv_ref[b] = 0
        return c
    lax.fori_loop(0, n_blk, clear, 0)

    def clear_groups(g, c):
        grp_ref[g] = 0
        return c
    lax.fori_loop(0, n_blk * MOE_GROUPS, clear_groups, 0, unroll=8)

    def per_expert(e, b):
        g0 = b * MOE_GROUPS

        def per_tile(t, tot):
            c = cnt_ref[t, e]
            src = t * TILE_ROWS + cs_ref[t * N_EXPERTS + e]
            first = g0 + lax.shift_right_logical(tot, grp_shift)

            def per_group(k, cc):
                grp_ref[first + k] = src + k * SUBLANES
                return cc
            lax.fori_loop(0, lax.shift_right_logical(c, grp_shift), per_group, 0)
            return tot + c
        tot = lax.fori_loop(0, n_tiles, per_tile, 0)

        def per_block(j, c):
            be_ref[b + j] = e
            nv_ref[b + j] = jnp.minimum(rows, tot - j * rows)
            return c
        nb = lax.shift_right_logical(tot + rows - 1, row_shift)
        lax.fori_loop(0, nb, per_block, 0)
        return b + nb
    n_used = lax.fori_loop(0, N_EXPERTS, per_expert, 0)

    def unused(b, c):
        be_ref[b] = be_ref[n_used - 1]
        nxt_ref[b] = -1
        return c
    lax.fori_loop(n_used, n_blk, unused, 0)

    def next_run(k, nf):
        b = n_used - 1 - k
        nf = jnp.where(be_ref[b] != be_ref[jnp.minimum(b + 1, n_used - 1)], b + 1, nf)
        nxt_ref[b] = nf
        return nf
    lax.fori_loop(0, n_used, next_run, -1)


def _plan(cnt, n_blk):
    n_tiles = cnt.shape[0]
    smem = pl.BlockSpec(memory_space=pltpu.SMEM)
    i32 = lambda n: jax.ShapeDtypeStruct((n,), jnp.int32)
    return pl.pallas_call(
        _plan_kernel,
        in_specs=[smem],
        out_specs=[smem] * 5,
        out_shape=[i32(n_blk), i32(n_blk), i32(n_blk), i32(n_blk * MOE_GROUPS), i32(n_tiles)],
        scratch_shapes=[pltpu.SMEM((n_tiles * N_EXPERTS,), jnp.int32)],
        name="moe_plan",
    )(cnt)


def _pow2_pieces(n, fn):
    for b in reversed(range(SUBLANES.bit_length() - 1, MOE_ROWS.bit_length())):
        size = 1 << b

        @pl.when((n & size) != 0)
        def _():
            fn((n >> (b + 1)) << (b + 1), size)


def _moe_kernel(be_ref, nv_ref, nxt_ref, grp_ref, used_ref,
                w1_hbm, w3_hbm, w2_hbm, xs_hbm, ys_hbm,
                xbuf, ybuf, wb1, wb3, wb2, wst1, wst3, wst2, wslot, gsem, ssem, wsem):
    i = pl.program_id(0)
    last = pl.num_programs(0) - 1
    slot = i % 2
    nv = nv_ref[i]
    half = D_MODEL // 2
    grp_shift = SUBLANES.bit_length() - 1

    def group_row(blk, g):
        return pl.multiple_of(grp_ref[blk * MOE_GROUPS + g], SUBLANES)

    def gather(blk, s):
        top = jnp.maximum(lax.shift_right_logical(nv_ref[blk], grp_shift) - 1, 0)
        for g in range(MOE_GROUPS):
            src = group_row(blk, jnp.minimum(g, top))
            pltpu.make_async_copy(xs_hbm.at[pl.ds(src, SUBLANES)],
                                  xbuf.at[s, pl.ds(g * SUBLANES, SUBLANES)], gsem.at[s]).start()

    def wait_gather(s):
        pltpu.make_async_copy(xs_hbm.at[pl.ds(0, MOE_ROWS)], xbuf.at[s], gsem.at[s]).wait()

    def scatter(blk, s):
        def body(g, c):
            r = pl.multiple_of(g * SUBLANES, SUBLANES)
            pltpu.make_async_copy(ybuf.at[s, pl.ds(r, SUBLANES)],
                                  ys_hbm.at[pl.ds(group_row(blk, g), SUBLANES)], ssem.at[s]).start()
            return c
        lax.fori_loop(0, lax.shift_right_logical(nv_ref[blk], grp_shift), body, 0)

    def wait_scatter(s, count):
        _pow2_pieces(count, lambda a, size: pltpu.make_async_copy(
            ybuf.at[s, pl.ds(0, size)], ys_hbm.at[pl.ds(0, size)], ssem.at[s]).wait())

    @pl.when(i == 0)
    def _():
        @pl.when(nv > 0)
        def _():
            gather(0, 0)

        ybuf[1] = jnp.zeros(ybuf.shape[1:], ybuf.dtype)
        n_tiles = used_ref.shape[0]

        def fill(t, c):
            row0 = t * TILE_ROWS + used_ref[t]
            _pow2_pieces(TILE_ROWS - used_ref[t], lambda a, size: pltpu.make_async_copy(
                ybuf.at[1, pl.ds(0, size)], ys_hbm.at[pl.ds(pl.multiple_of(row0 + a, SUBLANES), size)],
                ssem.at[1]).start())
            return c
        lax.fori_loop(0, n_tiles, fill, 0)

        def drain(t, c):
            wait_scatter(1, TILE_ROWS - used_ref[t])
            return c
        lax.fori_loop(0, n_tiles, drain, 0)

    @pl.when(i >= 2)
    def _():
        wait_scatter(slot, nv_ref[jnp.maximum(i - 2, 0)])

    @pl.when((nv == 0) & (i > 0) & (nv_ref[jnp.maximum(i - 1, 0)] > 0))
    def _():
        wait_gather(slot)

    @pl.when(nv > 0)
    def _():
        e = be_ref[i]
        e_prev = be_ref[jnp.maximum(i - 1, 0)]

        def weight_copies(ex, ws):
            return [pltpu.make_async_copy(src.at[ex], dst.at[ws], wsem.at[ws])
                    for src, dst in ((w1_hbm, wst1), (w3_hbm, wst3), (w2_hbm, wst2))]

        @pl.when(i == 0)
        def _():
            wslot[0] = 0
            for cp in weight_copies(e, 0):
                cp.start()

        @pl.when((i == 0) | (e != e_prev))
        def _():
            ws = wslot[0]
            for cp in weight_copies(e, ws):
                cp.wait()
            wb1[...] = wst1[ws].astype(BF16)
            wb3[...] = wst3[ws].astype(BF16)
            wb2[...] = wst2[ws].astype(BF16)
            nb = nxt_ref[i]

            @pl.when(nb >= 0)
            def _():
                for cp in weight_copies(be_ref[jnp.maximum(nb, 0)], 1 - ws):
                    cp.start()
            wslot[0] = 1 - ws

        wait_gather(slot)
        gather(jnp.minimum(i + 1, last), 1 - slot)
        u = xbuf[slot]
        xa, xb = _unpack_bf16_pair(u[:, :half])
        wv = pltpu.bitcast(u[:, half:], F32)
        roww = wv[:, 0:1] + wv[:, 1:2] + wv[:, 2:3]
        a = _dot(xa, wb1[:half, :]) + _dot(xb, wb1[half:, :])
        b = _dot(xa, wb3[:half, :]) + _dot(xb, wb3[half:, :])
        hmid = (a * jax.nn.sigmoid(a) * b).astype(BF16)
        y = _dot(hmid, wb2[...]) * roww
        ybuf[slot] = _pack_bf16_pair(y[:, :half], y[:, half:])
        scatter(i, slot)

    @pl.when(i == last)
    def _():
        @pl.when(nv > 0)
        def _():
            wait_gather(1 - slot)

        @pl.when(last >= 1)
        def _():
            wait_scatter(1 - slot, nv_ref[jnp.maximum(last - 1, 0)])
        wait_scatter(slot, nv)


def _moe(xs, plan, w1, w3, w2):
    n_blk = plan[0].shape[0]
    rows = MOE_ROWS
    half = D_MODEL // 2
    hbm = pl.BlockSpec(memory_space=pl.ANY)
    grid_spec = pltpu.PrefetchScalarGridSpec(
        num_scalar_prefetch=5,
        grid=(n_blk,),
        in_specs=[hbm] * 4,
        out_specs=hbm,
        scratch_shapes=[pltpu.VMEM((2, rows, XS_WIDTH), jnp.uint32),
                        pltpu.VMEM((2, rows, half), jnp.uint32),
                        pltpu.VMEM((D_MODEL, EXPERT_HIDDEN), BF16),
                        pltpu.VMEM((D_MODEL, EXPERT_HIDDEN), BF16),
                        pltpu.VMEM((EXPERT_HIDDEN, D_MODEL), BF16),
                        pltpu.VMEM((2, D_MODEL, EXPERT_HIDDEN), F32),
                        pltpu.VMEM((2, D_MODEL, EXPERT_HIDDEN), F32),
                        pltpu.VMEM((2, EXPERT_HIDDEN, D_MODEL), F32),
                        pltpu.SMEM((1,), jnp.int32),
                        pltpu.SemaphoreType.DMA((2,)),
                        pltpu.SemaphoreType.DMA((2,)),
                        pltpu.SemaphoreType.DMA((2,))])
    return pl.pallas_call(
        _moe_kernel,
        grid_spec=grid_spec,
        out_shape=jax.ShapeDtypeStruct((xs.shape[0], half), jnp.uint32),
        compiler_params=pltpu.CompilerParams(dimension_semantics=("arbitrary",),
                                             vmem_limit_bytes=VMEM_LIMIT),
        name="moe_ffn",
    )(*plan, w1, w3, w2, xs)


def _combine_kernel(x1_ref, ys_ref, pos_ref, g2_ref, o_ref):
    tt = x1_ref.shape[0]
    half = D_MODEL // 2
    pos = pos_ref[...]
    pcol = lax.broadcasted_iota(jnp.int32, (tt, ys_ref.shape[0]), 1).astype(F32)
    sel = (jnp.where(pcol == pos[:, 0:1], 1.0, 0.0) + jnp.where(pcol == pos[:, 1:2], 1.0, 0.0))
    sel = sel.astype(BF16)
    lo, hi = _unpack_bf16_pair(ys_ref[...])
    g2 = g2_ref[0]
    x1 = x1_ref[...]
    o_ref[:, :half] = x1[:, :half] + g2[:, :half] * _dot(sel, lo)
    o_ref[:, half:] = x1[:, half:] + g2[:, half:] * _dot(sel, hi)


def _combine(x1, ys, pos, g2, S):
    T = x1.shape[0]
    tt = MOE_TILE
    per_b = S // tt
    return pl.pallas_call(
        _combine_kernel,
        grid=(T // tt,),
        in_specs=[pl.BlockSpec((tt, D_MODEL), lambda i: (i, 0)),
                  pl.BlockSpec((TILE_ROWS, D_MODEL // 2), lambda i: (i, 0)),
                  pl.BlockSpec((tt, LANES), lambda i: (i, 0)),
                  pl.BlockSpec((1, 1, D_MODEL), lambda i: (i // per_b, 0, 0))],
        out_specs=pl.BlockSpec((tt, D_MODEL), lambda i: (i, 0)),
        out_shape=jax.ShapeDtypeStruct((T, D_MODEL), F32),
        compiler_params=pltpu.CompilerParams(vmem_limit_bytes=VMEM_LIMIT),
        name="moe_combine",
    )(x1, ys, pos, g2)


def _prep_w_in(w_in):
    dq = w_in[:, OFF_DIL_Q:OFF_DIL_K]
    dk = w_in[:, OFF_DIL_K:OFF_DIL_V]
    dv = w_in[:, OFF_DIL_V:OFF_GATE_A]
    dil = []
    for g in range(N_DIL_GROUPS):
        cs = slice(g * DIL_OUT_WIDTH, (g + 1) * DIL_OUT_WIDTH)
        dil += [dq[:, cs], dk[:, cs], dv[:, cs]]
    pad = jnp.zeros((D_MODEL, UNIT - FOX_HEADS), w_in.dtype)
    cols = [w_in[:, OFF_GATE_A:OFF_GATE_B], w_in[:, OFF_GATE_B:N_IN],
            w_in[:, OFF_FOX_Q:OFF_FOX_K], w_in[:, OFF_FOX_K:OFF_FOX_V], w_in[:, OFF_FOX_V:OFF_FOX_F],
            *dil, w_in[:, OFF_FOX_F:OFF_DIL_Q], pad]
    return jnp.concatenate(cols, axis=1).astype(BF16)


def _prep_gain(q_gain, k_gain):
    qs = HEAD_DIM ** -0.5 * LOG2E
    ones = jnp.ones((UNIT,), F32)
    fq = q_gain[:FOX_HEADS].reshape(-1) * qs
    fk = k_gain[:FOX_HEADS].reshape(-1)
    dq = q_gain[FOX_HEADS:].reshape(-1) * qs
    dk = k_gain[FOX_HEADS:].reshape(-1)
    dil = []
    for g in range(N_DIL_GROUPS):
        cs = slice(g * DIL_OUT_WIDTH, (g + 1) * DIL_OUT_WIDTH)
        dil += [dq[cs], dk[cs], ones]
    parts = [ones] * 8 + [fq, fk, ones, ones] + dil + [ones]
    return jnp.concatenate(parts).reshape(1, N_UNITS * UNIT)


def _layer(x, mod, rel_bias_table, norm1_g, w_in, b_forget, q_gain, k_gain, w_branch_a, w_branch_b,
           w_out, norm2_g, w_rg, b_rg, w_re, b_re, w1, w3, w2):
    B, S, D = x.shape
    T = B * S
    sh1, sc1, g1, sh2, sc2, g2 = [m.reshape(B, 1, D) for m in jnp.split(mod, 6, axis=-1)]
    x2d = x.reshape(T, D)

    p2d, fgt, slabs = _inproj(x2d, norm1_g.reshape(1, D), sc1, sh1, _prep_w_in(w_in),
                              _prep_gain(q_gain, k_gain), S)
    p3 = p2d.reshape(B, S, P_WIDTH)
    ck = _fcum(fgt.reshape(B, S, LANES), b_forget)
    ya = _fox(p3, ck)
    yd = _dil(slabs, _relbias(rel_bias_table))

    n_router = N_GROUPS + N_EXPERTS
    wr = jnp.concatenate([w_rg, w_re, jnp.zeros((D, LANES - n_router), F32)], axis=1).astype(BF16)
    br = jnp.concatenate([b_rg, b_re, jnp.zeros((LANES - n_router,), F32)]).reshape(1, LANES)
    x1, h2, logits = _outproj(x2d, ya.reshape(T, FOX_WIDTH), yd.reshape(T, DIL_OUT_WIDTH), p2d,
                              g1, sc2, sh2, norm2_g.reshape(1, D),
                              w_branch_a.astype(BF16), w_branch_b.astype(BF16), w_out.astype(BF16),
                              wr, br, S)
    xs, cnt, pos = _dispatch(logits, h2)
    cnt2 = cnt[:, :, 0]
    n_blk = cnt.shape[0] * TILE_ROWS // MOE_ROWS + N_EXPERTS
    plan = _plan(cnt2, n_blk)
    ys = _moe(xs, plan, w1, w3, w2)
    out = _combine(x1, ys, pos, g2, S)
    return out.reshape(B, S, D)


def kernel(x, c, rel_bias_table, w_ada, b_ada, norm1_g, w_in, b_forget, q_gain, k_gain, w_branch_a, w_branch_b, w_out, norm2_g, w_router_group, b_router_group, w_router_expert, b_router_expert, w1, w3, w2):
    depth = w_ada.shape[0]
    for l in range(depth):
        mod = _ada(c, w_ada[l], b_ada[l])
        x = _layer(x, mod, rel_bias_table, norm1_g[l], w_in[l], b_forget[l], q_gain[l], k_gain[l],
                   w_branch_a[l], w_branch_b[l], w_out[l], norm2_g[l], w_router_group[l],
                   b_router_group[l], w_router_expert[l], b_router_expert[l], w1[l], w3[l], w2[l])
    return x
```

```python
import functools
import math

import numpy as np
import jax
import jax.numpy as jnp
from jax import lax
from jax.experimental import pallas as pl
from jax.experimental.pallas import tpu as pltpu

F32 = jnp.float32
BF16 = jnp.bfloat16

D_MODEL = 1024
HEAD_DIM = 64
FOX_HEADS = 8
DIL_GROUPS = ((128, 1), (512, 4), (2048, 16))
DIL_HEADS_PER_GROUP = 4
N_DIL_GROUPS = len(DIL_GROUPS)
DIL_HEADS = N_DIL_GROUPS * DIL_HEADS_PER_GROUP
FOX_WIDTH = FOX_HEADS * HEAD_DIM
DIL_WIDTH = DIL_HEADS * HEAD_DIM
DIL_OUT_WIDTH = DIL_HEADS_PER_GROUP * HEAD_DIM
NUM_BUCKETS = 32
REL_MAX_DISTANCE = 2048
N_GROUPS = 4
EXPERTS_PER_GROUP = 8
N_EXPERTS = N_GROUPS * EXPERTS_PER_GROUP
EXPERT_HIDDEN = D_MODEL // 2
EPS = 1e-6
LOG2E = math.log2(math.e)

OFF_FOX_Q = 0
OFF_FOX_K = OFF_FOX_Q + FOX_WIDTH
OFF_FOX_V = OFF_FOX_K + FOX_WIDTH
OFF_FOX_F = OFF_FOX_V + FOX_WIDTH
OFF_DIL_Q = OFF_FOX_F + FOX_HEADS
OFF_DIL_K = OFF_DIL_Q + DIL_WIDTH
OFF_DIL_V = OFF_DIL_K + DIL_WIDTH
OFF_GATE_A = OFF_DIL_V + DIL_WIDTH
OFF_GATE_B = OFF_GATE_A + D_MODEL
N_IN = OFF_GATE_B + D_MODEL

LANES = 128
UNIT = 256
DIL_L = 128

U_GATE_A, U_GATE_B, U_FOX_Q, U_FOX_K, U_FOX_V, U_DIL, U_FORGET = 0, 4, 8, 10, 12, 14, 23
N_UNITS = 24
P_WIDTH = U_DIL * UNIT
N_SLABS = 3 * N_DIL_GROUPS
_KIND = (["gate"] * 8 + ["normq"] * 2 + ["normk"] * 2 + ["plain"] * 2
         + ["normq", "normk", "plain"] * 3 + ["forget"])

TM_INPROJ = 1024
TM_PROJ = 512
TQ_FOX = 512
MOE_ROWS = 256
MOE_TILE = 512
XS_WIDTH = D_MODEL // 2 + LANES
SUBLANES = 8
TILE_ROWS = 2 * MOE_TILE + N_EXPERTS * SUBLANES
MOE_GROUPS = MOE_ROWS // SUBLANES
GATHER_BUFS = 3
VMEM_LIMIT = 56 * 1024 * 1024


def _dot(a, b):
    return jnp.dot(a, b, preferred_element_type=F32)


def _dot_nt(a, b):
    return lax.dot_general(a, b, (((1,), (1,)), ((), ())), preferred_element_type=F32)


def _split3(x):
    hi = x.astype(BF16)
    r1 = x - hi.astype(F32)
    mid = r1.astype(BF16)
    lo = (r1 - mid.astype(F32)).astype(BF16)
    return hi, mid, lo


def _ada_kernel(c_ref, w_ref, b_ref, o_ref):
    c = c_ref[...]
    s = c * jax.nn.sigmoid(c)
    s_hi = s.astype(BF16)
    s_lo = (s - s_hi.astype(F32)).astype(BF16)
    w = w_ref[...]
    w_hi = w.astype(BF16)
    w_lo = (w - w_hi.astype(F32)).astype(BF16)
    acc = _dot(s_hi, w_hi) + _dot(s_hi, w_lo) + _dot(s_lo, w_hi)
    o_ref[...] = acc + b_ref[...]


def _ada(c, w_ada, b_ada):
    B = c.shape[0]
    n_out = w_ada.shape[1]
    tn = 512
    return pl.pallas_call(
        _ada_kernel,
        grid=(n_out // tn,),
        in_specs=[pl.BlockSpec((B, D_MODEL), lambda j: (0, 0)),
                  pl.BlockSpec((D_MODEL, tn), lambda j: (0, j)),
                  pl.BlockSpec((1, tn), lambda j: (0, j))],
        out_specs=pl.BlockSpec((B, tn), lambda j: (0, j)),
        out_shape=jax.ShapeDtypeStruct((B, n_out), F32),
        name="ada_mod",
    )(c, w_ada, b_ada.reshape(1, n_out))


def _pack_bf16_pair(lo, hi):
    lo_bits = pltpu.bitcast(lo.astype(BF16).astype(F32), jnp.uint32) >> 16
    hi_bits = pltpu.bitcast(hi.astype(BF16).astype(F32), jnp.uint32) & jnp.uint32(0xFFFF0000)
    return lo_bits | hi_bits


def _unpack_bf16_pair(u):
    lo = pltpu.bitcast(u << 16, F32).astype(BF16)
    hi = pltpu.bitcast(u & jnp.uint32(0xFFFF0000), F32).astype(BF16)
    return lo, hi


def _inproj_kernel(x_ref, g_ref, sc_ref, sh_ref, w_ref, gain_ref, bd_ref, p_ref, f_ref, s_ref):
    x = x_ref[...]
    ms = jnp.mean(x * x, axis=-1, keepdims=True)
    h = x * lax.rsqrt(ms + EPS) * g_ref[...]
    h = h * (1.0 + sc_ref[0]) + sh_ref[0]
    hb = h.astype(BF16)

    def unit(u):
        cols = slice(u * UNIT, (u + 1) * UNIT)
        acc = _dot(hb, w_ref[:, cols])
        kind = _KIND[u]
        if kind == "gate":
            return jax.nn.sigmoid(acc)
        if kind in ("normq", "normk"):
            ss = _dot((acc * acc).astype(BF16), bd_ref[...])
            return acc * lax.rsqrt(ss * (1.0 / HEAD_DIM) + EPS) * gain_ref[:, cols]
        return acc

    for u in range(U_DIL):
        p_ref[:, u * UNIT:(u + 1) * UNIT] = unit(u).astype(BF16)
    for g in range(N_DIL_GROUPS):
        q, k, v = (unit(U_DIL + 3 * g + j) for j in range(3))
        s_ref[0, 3 * g] = _pack_bf16_pair(q[:, :LANES], k[:, :LANES])
        s_ref[0, 3 * g + 1] = _pack_bf16_pair(q[:, LANES:], k[:, LANES:])
        s_ref[0, 3 * g + 2] = _pack_bf16_pair(v[:, :LANES], v[:, LANES:])
    f_ref[...] = unit(U_FORGET)[:, :LANES]


def _inproj(x2d, norm_g, sc, sh, w_re, gain_row, S):
    T = x2d.shape[0]
    tm = TM_INPROJ
    per_b = S // tm
    bd = np.kron(np.eye(UNIT // HEAD_DIM), np.ones((HEAD_DIM, HEAD_DIM))).astype(np.float32)
    once = dict(pipeline_mode=pl.Buffered(1))
    return pl.pallas_call(
        _inproj_kernel,
        grid=(T // tm,),
        in_specs=[pl.BlockSpec((tm, D_MODEL), lambda i: (i, 0)),
                  pl.BlockSpec((1, D_MODEL), lambda i: (0, 0)),
                  pl.BlockSpec((1, 1, D_MODEL), lambda i: (i // per_b, 0, 0)),
                  pl.BlockSpec((1, 1, D_MODEL), lambda i: (i // per_b, 0, 0)),
                  pl.BlockSpec((D_MODEL, N_UNITS * UNIT), lambda i: (0, 0), **once),
                  pl.BlockSpec((1, N_UNITS * UNIT), lambda i: (0, 0), **once),
                  pl.BlockSpec((UNIT, UNIT), lambda i: (0, 0), **once)],
        out_specs=[pl.BlockSpec((tm, P_WIDTH), lambda i: (i, 0)),
                   pl.BlockSpec((tm, LANES), lambda i: (i, 0)),
                   pl.BlockSpec((1, N_SLABS, tm, LANES), lambda i: (i // per_b, 0, i % per_b, 0))],
        out_shape=[jax.ShapeDtypeStruct((T, P_WIDTH), BF16),
                   jax.ShapeDtypeStruct((T, LANES), F32),
                   jax.ShapeDtypeStruct((T // S, N_SLABS, S, LANES), jnp.uint32)],
        compiler_params=pltpu.CompilerParams(vmem_limit_bytes=VMEM_LIMIT),
        name="in_proj",
    )(x2d, norm_g, sc, sh, w_re, gain_row, jnp.asarray(bd, BF16))


def _fcum_kernel(f_ref, b_ref, tri_ref, o_ref):
    S = f_ref.shape[1]
    xf = f_ref[0] + b_ref[...]
    ls = (jnp.minimum(xf, 0.0) - jnp.log(1.0 + jnp.exp(-jnp.abs(xf)))) * LOG2E
    lst = ls.T
    carry = jnp.zeros((LANES, UNIT), F32)
    for blk in range(S // UNIT):
        seg = lst[:, blk * UNIT:(blk + 1) * UNIT]
        hi, mid, lo = _split3(seg)
        tri = tri_ref[...]
        res = _dot(hi, tri) + _dot(mid, tri) + _dot(lo, tri)
        o_ref[0, :, blk * UNIT:(blk + 1) * UNIT] = (res[:, :UNIT] + carry)[:FOX_HEADS]
        carry = carry + res[:, UNIT:]


def _fcum(fgt, b_forget):
    B, S, _ = fgt.shape
    brow = jnp.zeros((1, LANES), F32).at[0, :FOX_HEADS].set(b_forget)
    tri = np.concatenate([np.triu(np.ones((UNIT, UNIT))), np.ones((UNIT, UNIT))], axis=1)
    return pl.pallas_call(
        _fcum_kernel,
        grid=(B,),
        in_specs=[pl.BlockSpec((1, S, LANES), lambda b: (b, 0, 0)),
                  pl.BlockSpec((1, LANES), lambda b: (0, 0)),
                  pl.BlockSpec((UNIT, 2 * UNIT), lambda b: (0, 0))],
        out_specs=pl.BlockSpec((1, FOX_HEADS, S), lambda b: (b, 0, 0)),
        out_shape=jax.ShapeDtypeStruct((B, FOX_HEADS, S), F32),
        name="forget_cumsum",
    )(fgt, brow, jnp.asarray(tri, BF16))


def _fox_kernel(q_ref, k_ref, v_ref, ck_ref, o_ref):
    S = q_ref.shape[1]
    pair = pl.program_id(1)
    tq = TQ_FOX
    lane = lax.broadcasted_iota(jnp.int32, (1, LANES), 1)
    row = lax.broadcasted_iota(jnp.int32, (tq, tq), 0)
    col = lax.broadcasted_iota(jnp.int32, (tq, tq), 1)
    causal = col <= row
    cks = [ck_ref[0, pl.ds(2 * pair + hh, 1), :] for hh in range(2)]
    for t in range(S // tq):
        r0, r1 = t * tq, (t + 1) * tq
        qt = q_ref[0, r0:r1, :]
        outs = []
        for hh in range(2):
            hsel = (lane >= HEAD_DIM) == bool(hh)
            qm = jnp.where(hsel, qt, jnp.zeros_like(qt))
            ck = cks[hh]
            s_d = _dot_nt(qm, k_ref[0, r0:r1, :]) - ck[:, r0:r1]
            s_d = jnp.where(causal, s_d, -jnp.inf)
            m = jnp.max(s_d, axis=-1, keepdims=True)
            if t > 0:
                s_o = _dot_nt(qm, k_ref[0, :r0, :]) - ck[:, :r0]
                m = jnp.maximum(m, jnp.max(s_o, axis=-1, keepdims=True))
            p_d = jnp.exp2(s_d - m)
            l = jnp.sum(p_d, axis=-1, keepdims=True)
            acc = _dot(p_d.astype(BF16), v_ref[0, r0:r1, :])
            if t > 0:
                p_o = jnp.exp2(s_o - m)
                l = l + jnp.sum(p_o, axis=-1, keepdims=True)
                acc = acc + _dot(p_o.astype(BF16), v_ref[0, :r0, :])
            outs.append(acc / l)
        o_ref[0, r0:r1, :] = jnp.where(lane < HEAD_DIM, outs[0], outs[1]).astype(BF16)


def _fox(p3, ck):
    B, S, _ = p3.shape
    nq, nk, nv = (U_FOX_Q * UNIT // LANES, U_FOX_K * UNIT // LANES, U_FOX_V * UNIT // LANES)
    return pl.pallas_call(
        _fox_kernel,
        grid=(B, FOX_HEADS // 2),
        in_specs=[pl.BlockSpec((1, S, LANES), lambda b, p: (b, 0, nq + p)),
                  pl.BlockSpec((1, S, LANES), lambda b, p: (b, 0, nk + p)),
                  pl.BlockSpec((1, S, LANES), lambda b, p: (b, 0, nv + p)),
                  pl.BlockSpec((1, FOX_HEADS, S), lambda b, p: (b, 0, 0))],
        out_specs=pl.BlockSpec((1, S, LANES), lambda b, p: (b, 0, p)),
        out_shape=jax.ShapeDtypeStruct((B, S, FOX_WIDTH), BF16),
        compiler_params=pltpu.CompilerParams(vmem_limit_bytes=VMEM_LIMIT),
        name="fox_attn",
    )(p3, p3, p3, ck)


def _t5_bucket(dist):
    max_exact = NUM_BUCKETS // 2
    d = np.maximum(dist, 1).astype(np.float32)
    large = max_exact + (np.log(d / max_exact) / np.log(REL_MAX_DISTANCE / max_exact)
                         * (NUM_BUCKETS - max_exact)).astype(np.int32)
    large = np.minimum(large, NUM_BUCKETS - 1)
    return np.where(dist < max_exact, dist, large).astype(np.int32)


def _relbias_kernel(tab_ref, bucket_ref, valid_ref, o_ref):
    g = pl.program_id(0)
    bk = bucket_ref[0]
    vd = valid_ref[0]
    for hs in range(DIL_HEADS_PER_GROUP):
        acc = jnp.zeros(bk.shape, F32)
        for b in range(NUM_BUCKETS):
            acc = jnp.where(bk == b, tab_ref[b, g * DIL_HEADS_PER_GROUP + hs], acc)
        bias = jnp.where(vd != 0, acc * LOG2E, -jnp.inf)
        o_ref[0, hs] = bias
        col = lax.broadcasted_iota(jnp.int32, bias.shape, 1)
        o_ref[1, hs] = jnp.where(col >= DIL_L, bias, -jnp.inf)


def _relbias(table):
    L = DIL_L
    i = np.arange(L)[:, None]
    j = np.arange(2 * L)[None, :]
    m = L + i - j
    valid = ((m >= 0) & (m <= L)).astype(np.int32)
    buckets = np.stack([_t5_bucket(np.clip(m, 0, None) * d) for _, d in DIL_GROUPS])
    valids = np.stack([valid] * N_DIL_GROUPS)
    return pl.pallas_call(
        _relbias_kernel,
        grid=(N_DIL_GROUPS,),
        in_specs=[pl.BlockSpec(memory_space=pltpu.SMEM),
                  pl.BlockSpec((1, L, 2 * L), lambda g: (g, 0, 0)),
                  pl.BlockSpec((1, L, 2 * L), lambda g: (g, 0, 0))],
        out_specs=pl.BlockSpec((2, DIL_HEADS_PER_GROUP, L, 2 * L), lambda g: (0, g, 0, 0)),
        out_shape=jax.ShapeDtypeStruct((2, DIL_HEADS, L, 2 * L), F32),
        name="rel_bias",
    )(table, jnp.asarray(buckets), jnp.asarray(valids))


def _dil_rows(start, d):
    return pl.ds(start, DIL_L) if d == 1 else pl.ds(start, DIL_L, stride=d)


def _dil_block(qkv_ref, bias_ref, m_scr, l_scr, acc_scr, g, d, r, n):
    L = DIL_L
    lane = lax.broadcasted_iota(jnp.int32, (1, LANES), 1)
    first = 1 - jnp.minimum(n, 1)
    cur = _dil_rows(r + d * (n * L), d)
    prev = _dil_rows(r + d * (jnp.maximum(n - 1, 0) * L), d)
    v_cur = _unpack_bf16_pair(qkv_ref[0, 3 * g + 2, cur, :])
    v_prev = _unpack_bf16_pair(qkv_ref[0, 3 * g + 2, prev, :])
    for pr in range(2):
        qt, k_cur = _unpack_bf16_pair(qkv_ref[0, 3 * g + pr, cur, :])
        _, k_prev = _unpack_bf16_pair(qkv_ref[0, 3 * g + pr, prev, :])
        kt = jnp.concatenate([k_prev, k_cur], axis=0)
        vt = jnp.concatenate([v_prev[pr], v_cur[pr]], axis=0)
        ms, ls, accs = [], [], []
        for hh in range(2):
            hsel = (lane >= HEAD_DIM) == bool(hh)
            qm = jnp.where(hsel, qt, jnp.zeros_like(qt))
            s = _dot_nt(qm, kt) + bias_ref[first, DIL_HEADS_PER_GROUP * g + 2 * pr + hh]
            m = jnp.max(s, axis=-1, keepdims=True)
            p = jnp.exp2(s - m)
            ms.append(m)
            ls.append(jnp.sum(p, axis=-1, keepdims=True))
            accs.append(_dot(p.astype(BF16), vt))
        low = lane < HEAD_DIM
        m_b = jnp.where(low, ms[0], ms[1])
        l_b = jnp.where(low, ls[0], ls[1])
        acc_b = jnp.where(low, accs[0], accs[1])
        if g == 0:
            m_scr[pr, cur, :] = m_b
            l_scr[pr, cur, :] = l_b
            acc_scr[pr, cur, :] = acc_b
        else:
            m_o = m_scr[pr, cur, :]
            m_n = jnp.maximum(m_o, m_b)
            a_o = jnp.exp2(m_o - m_n)
            a_b = jnp.exp2(m_b - m_n)
            m_scr[pr, cur, :] = m_n
            l_scr[pr, cur, :] = l_scr[pr, cur, :] * a_o + l_b * a_b
            acc_scr[pr, cur, :] = acc_scr[pr, cur, :] * a_o + acc_b * a_b


def _dil_kernel(qkv_ref, bias_ref, o_ref, m_scr, l_scr, acc_scr):
    S = o_ref.shape[1]
    for g, (window, d) in enumerate(DIL_GROUPS):
        nb = S // window

        def body(it, carry, g=g, d=d, nb=nb):
            _dil_block(qkv_ref, bias_ref, m_scr, l_scr, acc_scr, g, d, it // nb, it % nb)
            return carry
        lax.fori_loop(0, d * nb, body, 0, unroll=4)
    for pr in range(2):
        o_ref[0, :, pr * LANES:(pr + 1) * LANES] = (acc_scr[pr] / l_scr[pr]).astype(BF16)


def _dil(slabs, bias):
    B, _, S, _ = slabs.shape
    for window, d in DIL_GROUPS:
        assert window // d == DIL_L and S % window == 0
    stat = pltpu.VMEM((2, S, LANES), F32)
    return pl.pallas_call(
        _dil_kernel,
        grid=(B,),
        in_specs=[pl.BlockSpec((1, N_SLABS, S, LANES), lambda b: (b, 0, 0, 0)),
                  pl.BlockSpec(bias.shape, lambda b: (0, 0, 0, 0))],
        out_specs=pl.BlockSpec((1, S, DIL_OUT_WIDTH), lambda b: (b, 0, 0)),
        out_shape=jax.ShapeDtypeStruct((B, S, DIL_OUT_WIDTH), BF16),
        scratch_shapes=[stat, stat, stat],
        compiler_params=pltpu.CompilerParams(vmem_limit_bytes=VMEM_LIMIT),
        name="dil_attn",
    )(slabs, bias)


def _outproj_kernel(x_ref, ya_ref, yd_ref, ga_ref, gb_ref,
                    g1_ref, sc_ref, sh_ref, ng_ref, wa_ref, wb_ref, wo_ref, wr_ref, br_ref,
                    x1_ref, h2_ref, lg_ref):
    a = _dot(ya_ref[...], wa_ref[...])
    bm = _dot(yd_ref[...], wb_ref[...])
    merged = ga_ref[...].astype(F32) * a + gb_ref[...].astype(F32) * bm
    out = _dot(merged.astype(BF16), wo_ref[...])
    x1 = x_ref[...] + g1_ref[0] * out
    x1_ref[...] = x1
    ms = jnp.mean(x1 * x1, axis=-1, keepdims=True)
    h = x1 * lax.rsqrt(ms + EPS) * ng_ref[...]
    h = h * (1.0 + sc_ref[0]) + sh_ref[0]
    hb = h.astype(BF16)
    h2_ref[...] = hb
    lg_ref[...] = _dot(hb, wr_ref[...]) + br_ref[...]


def _outproj(x2d, ya2d, yd2d, p2d, g1, sc2, sh2, norm_g, wa, wb, wo, wr, br, S):
    T = x2d.shape[0]
    tm = TM_PROJ
    per_b = S // tm
    row = lambda w: pl.BlockSpec((tm, w), lambda i: (i, 0))
    full = lambda a: pl.BlockSpec(a.shape, lambda i: (0,) * a.ndim)
    mod = pl.BlockSpec((1, 1, D_MODEL), lambda i: (i // per_b, 0, 0))
    return pl.pallas_call(
        _outproj_kernel,
        grid=(T // tm,),
        in_specs=[row(D_MODEL), row(FOX_WIDTH), row(DIL_OUT_WIDTH)]
                 + [pl.BlockSpec((tm, D_MODEL), lambda i: (i, U_GATE_A * UNIT // D_MODEL)),
                    pl.BlockSpec((tm, D_MODEL), lambda i: (i, U_GATE_B * UNIT // D_MODEL)),
                    mod, mod, mod, full(norm_g), full(wa), full(wb), full(wo), full(wr), full(br)],
        out_specs=[row(D_MODEL), row(D_MODEL), row(LANES)],
        out_shape=[jax.ShapeDtypeStruct((T, D_MODEL), F32),
                   jax.ShapeDtypeStruct((T, D_MODEL), BF16),
                   jax.ShapeDtypeStruct((T, LANES), F32)],
        compiler_params=pltpu.CompilerParams(vmem_limit_bytes=VMEM_LIMIT),
        name="out_proj",
    )(x2d, ya2d, yd2d, p2d, p2d, g1, sc2, sh2, norm_g, wa, wb, wo, wr, br)


def _dispatch_kernel(lg_ref, h_ref, tri_ref, xs_ref, cnt_ref, pos_ref):
    tt = lg_ref.shape[0]
    lt = lg_ref[...].T
    row = lambda i: lt[i:i + 1, :]
    neg = -jnp.inf
    g = [row(i) for i in range(N_GROUPS)]
    gmax = functools.reduce(jnp.maximum, g)
    gidx = jnp.full(gmax.shape, N_GROUPS - 1, jnp.int32)
    for i in reversed(range(N_GROUPS - 1)):
        gidx = jnp.where(g[i] == gmax, i, gidx)
    gsum = sum(jnp.exp(gi - gmax) for gi in g)
    el = []
    for j in range(EXPERTS_PER_GROUP):
        v = row(N_GROUPS + EXPERTS_PER_GROUP * (N_GROUPS - 1) + j)
        for gg in reversed(range(N_GROUPS - 1)):
            v = jnp.where(gidx == gg, row(N_GROUPS + EXPERTS_PER_GROUP * gg + j), v)
        el.append(v)

    def top(vals):
        best = functools.reduce(jnp.maximum, vals)
        idx = jnp.full(best.shape, EXPERTS_PER_GROUP - 1, jnp.int32)
        for j in reversed(range(EXPERTS_PER_GROUP - 1)):
            idx = jnp.where(vals[j] == best, j, idx)
        return best, idx

    v1, i1 = top(el)
    v2, i2 = top([jnp.where(i1 == j, neg, el[j]) for j in range(EXPERTS_PER_GROUP)])
    t = jnp.exp(v2 - v1)
    den = (1.0 + t) * gsum
    wts = [1.0 / den, t / den]
    eid = [gidx * EXPERTS_PER_GROUP + i1, gidx * EXPERTS_PER_GROUP + i2]

    esub = lax.broadcasted_iota(jnp.int32, (N_EXPERTS, tt), 0)
    ohf = jnp.concatenate([jnp.where(esub == eid[k], 1.0, 0.0) for k in range(2)], axis=1)
    res = _dot(ohf.astype(BF16), tri_ref[...])
    prefix, cnt = res[:, :2 * tt], res[:, 2 * tt:]
    cnt = (((cnt.astype(jnp.int32) + (SUBLANES - 1)) // SUBLANES) * SUBLANES).astype(F32)
    esub_c = lax.broadcasted_iota(jnp.int32, cnt.shape, 0)
    start = jnp.zeros_like(cnt)
    for e in range(N_EXPERTS - 1):
        start = start + jnp.where(esub_c > e, cnt[e:e + 1, :], 0.0)
    start_w = jnp.concatenate([start] * (2 * tt // LANES), axis=1)
    pos = jnp.sum(ohf * (start_w + prefix), axis=0, keepdims=True)
    pos_k = [pos[:, :tt], pos[:, tt:]]

    n_rows = xs_ref.shape[0]
    psub = lax.broadcasted_iota(jnp.int32, (n_rows, tt), 0).astype(F32)
    pm = [jnp.where(psub == pos_k[k], 1.0, 0.0).astype(BF16) for k in range(2)]
    xs = _dot(pm[0] + pm[1], h_ref[...])
    wsub = lax.broadcasted_iota(jnp.int32, (LANES, tt), 0)
    ws = jnp.zeros((n_rows, LANES), F32)
    for k in range(2):
        parts = _split3(wts[k])
        wrows = jnp.zeros((LANES, tt), F32)
        for j in range(3):
            wrows = jnp.where(wsub == j, parts[j].astype(F32), wrows)
        ws = ws + _dot_nt(pm[k], wrows.astype(BF16))
    half = D_MODEL // 2
    xs_ref[:, :half] = _pack_bf16_pair(xs[:, :half], xs[:, half:])
    xs_ref[:, half:] = pltpu.bitcast(ws, jnp.uint32)
    cnt_ref[0] = cnt.astype(jnp.int32)
    posr = jnp.where(wsub == 0, pos_k[0], jnp.where(wsub == 1, pos_k[1], 0.0))
    pos_ref[...] = posr.T


def _dispatch(logits, h2):
    T = logits.shape[0]
    tt = MOE_TILE
    n_tiles = T // tt
    tri = np.concatenate([np.triu(np.ones((2 * tt, 2 * tt)), 1), np.ones((2 * tt, LANES))], axis=1)
    return pl.pallas_call(
        _dispatch_kernel,
        grid=(n_tiles,),
        in_specs=[pl.BlockSpec((tt, LANES), lambda i: (i, 0)),
                  pl.BlockSpec((tt, D_MODEL), lambda i: (i, 0)),
                  pl.BlockSpec(tri.shape, lambda i: (0, 0))],
        out_specs=[pl.BlockSpec((TILE_ROWS, XS_WIDTH), lambda i: (i, 0)),
                   pl.BlockSpec((1, N_EXPERTS, LANES), lambda i: (i, 0, 0)),
                   pl.BlockSpec((tt, LANES), lambda i: (i, 0))],
        out_shape=[jax.ShapeDtypeStruct((n_tiles * TILE_ROWS, XS_WIDTH), jnp.uint32),
                   jax.ShapeDtypeStruct((n_tiles, N_EXPERTS, LANES), jnp.int32),
                   jax.ShapeDtypeStruct((T, LANES), F32)],
        compiler_params=pltpu.CompilerParams(vmem_limit_bytes=VMEM_LIMIT),
        name="moe_dispatch",
    )(logits, h2, jnp.asarray(tri, BF16))


def _plan_kernel(cnt_ref, be_ref, nv_ref, nxt_ref, grp_ref, used_ref, cs_ref):
    n_tiles = cnt_ref.shape[0]
    n_blk = be_ref.shape[0]
    rows = MOE_ROWS
    row_shift = rows.bit_length() - 1
    grp_shift = SUBLANES.bit_length() - 1
    assert rows == 1 << row_shift and SUBLANES == 1 << grp_shift

    def tile_starts(t, c):
        def per_e(e, acc):
            cs_ref[t * N_EXPERTS + e] = acc
            return acc + cnt_ref[t, e]
        used_ref[t] = lax.fori_loop(0, N_EXPERTS, per_e, 0, unroll=8)
        return c
    lax.fori_loop(0, n_tiles, tile_starts, 0)

    def clear(b, c):
        nv_ref[b] = 0
        return c
    lax.fori_loop(0, n_blk, clear, 0)

    def clear_groups(g, c):
        grp_ref[g] = 0
        return c
    lax.fori_loop(0, n_blk * MOE_GROUPS, clear_groups, 0, unroll=8)

    def per_expert(e, b):
        g0 = b * MOE_GROUPS

        def per_tile(t, tot):
            c = cnt_ref[t, e]
            src = t * TILE_ROWS + cs_ref[t * N_EXPERTS + e]
            first = g0 + lax.shift_right_logical(tot, grp_shift)

            def per_group(k, cc):
                grp_ref[first + k] = src + k * SUBLANES
                return cc
            lax.fori_loop(0, lax.shift_right_logical(c, grp_shift), per_group, 0)
            return tot + c
        tot = lax.fori_loop(0, n_tiles, per_tile, 0)

        def per_block(j, c):
            be_ref[b + j] = e
            nv_ref[b + j] = jnp.minimum(rows, tot - j * rows)
            return c
        nb = lax.shift_right_logical(tot + rows - 1, row_shift)
        lax.fori_loop(0, nb, per_block, 0)
        return b + nb
    n_used = lax.fori_loop(0, N_EXPERTS, per_expert, 0)

    def unused(b, c):
        be_ref[b] = be_ref[n_used - 1]
        nxt_ref[b] = -1
        return c
    lax.fori_loop(n_used, n_blk, unused, 0)

    def next_run(k, nf):
        b = n_used - 1 - k
        nf = jnp.where(be_ref[b] != be_ref[jnp.minimum(b + 1, n_used - 1)], b + 1, nf)
        nxt_ref[b] = nf
        return nf
    lax.fori_loop(0, n_used, next_run, -1)


def _plan(cnt, n_blk):
    n_tiles = cnt.shape[0]
    smem = pl.BlockSpec(memory_space=pltpu.SMEM)
    i32 = lambda n: jax.ShapeDtypeStruct((n,), jnp.int32)
    return pl.pallas_call(
        _plan_kernel,
        in_specs=[smem],
        out_specs=[smem] * 5,
        out_shape=[i32(n_blk), i32(n_blk), i32(n_blk), i32(n_blk * MOE_GROUPS), i32(n_tiles)],
        scratch_shapes=[pltpu.SMEM((n_tiles * N_EXPERTS,), jnp.int32)],
        name="moe_plan",
    )(cnt)


def _pow2_pieces(n, fn):
    for b in reversed(range(SUBLANES.bit_length() - 1, MOE_ROWS.bit_length())):
        size = 1 << b

        @pl.when((n & size) != 0)
        def _():
            fn((n >> (b + 1)) << (b + 1), size)


def _moe_kernel(be_ref, nv_ref, nxt_ref, grp_ref, used_ref,
                w1_hbm, w3_hbm, w2_hbm, xs_hbm, ys_hbm,
                xbuf, ybuf, wb1, wb3, wb2, wst1, wst3, wst2, wslot, gsem, ssem, wsem):
    i = pl.program_id(0)
    last = pl.num_programs(0) - 1
    slot = i % 2
    nv = nv_ref[i]
    half = D_MODEL // 2
    grp_shift = SUBLANES.bit_length() - 1

    def group_row(blk, g):
        return pl.multiple_of(grp_ref[blk * MOE_GROUPS + g], SUBLANES)

    def gather(blk, s):
        top = jnp.maximum(lax.shift_right_logical(nv_ref[blk], grp_shift) - 1, 0)
        for g in range(MOE_GROUPS):
            src = group_row(blk, jnp.minimum(g, top))
            pltpu.make_async_copy(xs_hbm.at[pl.ds(src, SUBLANES)],
                                  xbuf.at[s, pl.ds(g * SUBLANES, SUBLANES)], gsem.at[s]).start()

    def wait_gather(s):
        pltpu.make_async_copy(xs_hbm.at[pl.ds(0, MOE_ROWS)], xbuf.at[s], gsem.at[s]).wait()

    def scatter(blk, s):
        def body(g, c):
            r = pl.multiple_of(g * SUBLANES, SUBLANES)
            pltpu.make_async_copy(ybuf.at[s, pl.ds(r, SUBLANES)],
                                  ys_hbm.at[pl.ds(group_row(blk, g), SUBLANES)], ssem.at[s]).start()
            return c
        lax.fori_loop(0, lax.shift_right_logical(nv_ref[blk], grp_shift), body, 0)

    def wait_scatter(s, count):
        _pow2_pieces(count, lambda a, size: pltpu.make_async_copy(
            ybuf.at[s, pl.ds(0, size)], ys_hbm.at[pl.ds(0, size)], ssem.at[s]).wait())

    @pl.when(i == 0)
    def _():
        @pl.when(nv > 0)
        def _():
            gather(0, 0)
            gather(jnp.minimum(1, last), 1)

        ybuf[1] = jnp.zeros(ybuf.shape[1:], ybuf.dtype)
        n_tiles = used_ref.shape[0]

        def fill(t, c):
            row0 = t * TILE_ROWS + used_ref[t]
            _pow2_pieces(TILE_ROWS - used_ref[t], lambda a, size: pltpu.make_async_copy(
                ybuf.at[1, pl.ds(0, size)], ys_hbm.at[pl.ds(pl.multiple_of(row0 + a, SUBLANES), size)],
                ssem.at[1]).start())
            return c
        lax.fori_loop(0, n_tiles, fill, 0)

        def drain(t, c):
            wait_scatter(1, TILE_ROWS - used_ref[t])
            return c
        lax.fori_loop(0, n_tiles, drain, 0)

    @pl.when(i >= 2)
    def _():
        wait_scatter(slot, nv_ref[jnp.maximum(i - 2, 0)])

    xslot = i % GATHER_BUFS
    issuer_used = jnp.where(i >= 2, nv_ref[jnp.maximum(i - 2, 0)], nv_ref[0]) > 0

    @pl.when((nv == 0) & (i > 0) & issuer_used)
    def _():
        wait_gather(xslot)

    @pl.when(nv > 0)
    def _():
        e = be_ref[i]
        e_prev = be_ref[jnp.maximum(i - 1, 0)]

        def weight_copies(ex, ws):
            return [pltpu.make_async_copy(src.at[ex], dst.at[ws], wsem.at[ws])
                    for src, dst in ((w1_hbm, wst1), (w3_hbm, wst3), (w2_hbm, wst2))]

        @pl.when(i == 0)
        def _():
            wslot[0] = 0
            for cp in weight_copies(e, 0):
                cp.start()

        @pl.when((i == 0) | (e != e_prev))
        def _():
            ws = wslot[0]
            for cp in weight_copies(e, ws):
                cp.wait()
            wb1[...] = wst1[ws].astype(BF16)
            wb3[...] = wst3[ws].astype(BF16)
            wb2[...] = wst2[ws].astype(BF16)
            nb = nxt_ref[i]

            @pl.when(nb >= 0)
            def _():
                for cp in weight_copies(be_ref[jnp.maximum(nb, 0)], 1 - ws):
                    cp.start()
            wslot[0] = 1 - ws

        wait_gather(xslot)
        gather(jnp.minimum(i + 2, last), (i + 2) % GATHER_BUFS)
        u = xbuf[xslot]
        xa, xb = _unpack_bf16_pair(u[:, :half])
        wv = pltpu.bitcast(u[:, half:], F32)
        roww = wv[:, 0:1] + wv[:, 1:2] + wv[:, 2:3]
        a = _dot(xa, wb1[:half, :]) + _dot(xb, wb1[half:, :])
        b = _dot(xa, wb3[:half, :]) + _dot(xb, wb3[half:, :])
        hmid = (a * jax.nn.sigmoid(a) * b).astype(BF16)
        y = _dot(hmid, wb2[...]) * roww
        ybuf[slot] = _pack_bf16_pair(y[:, :half], y[:, half:])
        scatter(i, slot)

    @pl.when(i == last)
    def _():
        @pl.when((last >= 1) & (nv_ref[jnp.maximum(last - 1, 0)] > 0))
        def _():
            wait_gather((last + 1) % GATHER_BUFS)

        @pl.when(nv > 0)
        def _():
            wait_gather((last + 2) % GATHER_BUFS)

        @pl.when(last >= 1)
        def _():
            wait_scatter(1 - slot, nv_ref[jnp.maximum(last - 1, 0)])
        wait_scatter(slot, nv)


def _moe(xs, plan, w1, w3, w2):
    n_blk = plan[0].shape[0]
    rows = MOE_ROWS
    half = D_MODEL // 2
    hbm = pl.BlockSpec(memory_space=pl.ANY)
    grid_spec = pltpu.PrefetchScalarGridSpec(
        num_scalar_prefetch=5,
        grid=(n_blk,),
        in_specs=[hbm] * 4,
        out_specs=hbm,
        scratch_shapes=[pltpu.VMEM((GATHER_BUFS, rows, XS_WIDTH), jnp.uint32),
                        pltpu.VMEM((2, rows, half), jnp.uint32),
                        pltpu.VMEM((D_MODEL, EXPERT_HIDDEN), BF16),
                        pltpu.VMEM((D_MODEL, EXPERT_HIDDEN), BF16),
                        pltpu.VMEM((EXPERT_HIDDEN, D_MODEL), BF16),
                        pltpu.VMEM((2, D_MODEL, EXPERT_HIDDEN), F32),
                        pltpu.VMEM((2, D_MODEL, EXPERT_HIDDEN), F32),
                        pltpu.VMEM((2, EXPERT_HIDDEN, D_MODEL), F32),
                        pltpu.SMEM((1,), jnp.int32),
                        pltpu.SemaphoreType.DMA((GATHER_BUFS,)),
                        pltpu.SemaphoreType.DMA((2,)),
                        pltpu.SemaphoreType.DMA((2,))])
    return pl.pallas_call(
        _moe_kernel,
        grid_spec=grid_spec,
        out_shape=jax.ShapeDtypeStruct((xs.shape[0], half), jnp.uint32),
        compiler_params=pltpu.CompilerParams(dimension_semantics=("arbitrary",),
                                             vmem_limit_bytes=VMEM_LIMIT),
        name="moe_ffn",
    )(*plan, w1, w3, w2, xs)


def _combine_kernel(x1_ref, ys_ref, pos_ref, g2_ref, o_ref):
    tt = x1_ref.shape[0]
    half = D_MODEL // 2
    pos = pos_ref[...]
    pcol = lax.broadcasted_iota(jnp.int32, (tt, ys_ref.shape[0]), 1).astype(F32)
    sel = (jnp.where(pcol == pos[:, 0:1], 1.0, 0.0) + jnp.where(pcol == pos[:, 1:2], 1.0, 0.0))
    sel = sel.astype(BF16)
    lo, hi = _unpack_bf16_pair(ys_ref[...])
    g2 = g2_ref[0]
    x1 = x1_ref[...]
    o_ref[:, :half] = x1[:, :half] + g2[:, :half] * _dot(sel, lo)
    o_ref[:, half:] = x1[:, half:] + g2[:, half:] * _dot(sel, hi)


def _combine(x1, ys, pos, g2, S):
    T = x1.shape[0]
    tt = MOE_TILE
    per_b = S // tt
    return pl.pallas_call(
        _combine_kernel,
        grid=(T // tt,),
        in_specs=[pl.BlockSpec((tt, D_MODEL), lambda i: (i, 0)),
                  pl.BlockSpec((TILE_ROWS, D_MODEL // 2), lambda i: (i, 0)),
                  pl.BlockSpec((tt, LANES), lambda i: (i, 0)),
                  pl.BlockSpec((1, 1, D_MODEL), lambda i: (i // per_b, 0, 0))],
        out_specs=pl.BlockSpec((tt, D_MODEL), lambda i: (i, 0)),
        out_shape=jax.ShapeDtypeStruct((T, D_MODEL), F32),
        compiler_params=pltpu.CompilerParams(vmem_limit_bytes=VMEM_LIMIT),
        name="moe_combine",
    )(x1, ys, pos, g2)


def _prep_w_in(w_in):
    dq = w_in[:, OFF_DIL_Q:OFF_DIL_K]
    dk = w_in[:, OFF_DIL_K:OFF_DIL_V]
    dv = w_in[:, OFF_DIL_V:OFF_GATE_A]
    dil = []
    for g in range(N_DIL_GROUPS):
        cs = slice(g * DIL_OUT_WIDTH, (g + 1) * DIL_OUT_WIDTH)
        dil += [dq[:, cs], dk[:, cs], dv[:, cs]]
    pad = jnp.zeros((D_MODEL, UNIT - FOX_HEADS), w_in.dtype)
    cols = [w_in[:, OFF_GATE_A:OFF_GATE_B], w_in[:, OFF_GATE_B:N_IN],
            w_in[:, OFF_FOX_Q:OFF_FOX_K], w_in[:, OFF_FOX_K:OFF_FOX_V], w_in[:, OFF_FOX_V:OFF_FOX_F],
            *dil, w_in[:, OFF_FOX_F:OFF_DIL_Q], pad]
    return jnp.concatenate(cols, axis=1).astype(BF16)


def _prep_gain(q_gain, k_gain):
    qs = HEAD_DIM ** -0.5 * LOG2E
    ones = jnp.ones((UNIT,), F32)
    fq = q_gain[:FOX_HEADS].reshape(-1) * qs
    fk = k_gain[:FOX_HEADS].reshape(-1)
    dq = q_gain[FOX_HEADS:].reshape(-1) * qs
    dk = k_gain[FOX_HEADS:].reshape(-1)
    dil = []
    for g in range(N_DIL_GROUPS):
        cs = slice(g * DIL_OUT_WIDTH, (g + 1) * DIL_OUT_WIDTH)
        dil += [dq[cs], dk[cs], ones]
    parts = [ones] * 8 + [fq, fk, ones, ones] + dil + [ones]
    return jnp.concatenate(parts).reshape(1, N_UNITS * UNIT)


def _layer(x, mod, rel_bias_table, norm1_g, w_in, b_forget, q_gain, k_gain, w_branch_a, w_branch_b,
           w_out, norm2_g, w_rg, b_rg, w_re, b_re, w1, w3, w2):
    B, S, D = x.shape
    T = B * S
    sh1, sc1, g1, sh2, sc2, g2 = [m.reshape(B, 1, D) for m in jnp.split(mod, 6, axis=-1)]
    x2d = x.reshape(T, D)

    p2d, fgt, slabs = _inproj(x2d, norm1_g.reshape(1, D), sc1, sh1, _prep_w_in(w_in),
                              _prep_gain(q_gain, k_gain), S)
    p3 = p2d.reshape(B, S, P_WIDTH)
    ck = _fcum(fgt.reshape(B, S, LANES), b_forget)
    ya = _fox(p3, ck)
    yd = _dil(slabs, _relbias(rel_bias_table))

    n_router = N_GROUPS + N_EXPERTS
    wr = jnp.concatenate([w_rg, w_re, jnp.zeros((D, LANES - n_router), F32)], axis=1).astype(BF16)
    br = jnp.concatenate([b_rg, b_re, jnp.zeros((LANES - n_router,), F32)]).reshape(1, LANES)
    x1, h2, logits = _outproj(x2d, ya.reshape(T, FOX_WIDTH), yd.reshape(T, DIL_OUT_WIDTH), p2d,
                              g1, sc2, sh2, norm2_g.reshape(1, D),
                              w_branch_a.astype(BF16), w_branch_b.astype(BF16), w_out.astype(BF16),
                              wr, br, S)
    xs, cnt, pos = _dispatch(logits, h2)
    cnt2 = cnt[:, :, 0]
    n_blk = cnt.shape[0] * TILE_ROWS // MOE_ROWS + N_EXPERTS
    plan = _plan(cnt2, n_blk)
    ys = _moe(xs, plan, w1, w3, w2)
    out = _combine(x1, ys, pos, g2, S)
    return out.reshape(B, S, D)


def kernel(x, c, rel_bias_table, w_ada, b_ada, norm1_g, w_in, b_forget, q_gain, k_gain, w_branch_a, w_branch_b, w_out, norm2_g, w_router_group, b_router_group, w_router_expert, b_router_expert, w1, w3, w2):
    depth = w_ada.shape[0]
    for l in range(depth):
        mod = _ada(c, w_ada[l], b_ada[l])
        x = _layer(x, mod, rel_bias_table, norm1_g[l], w_in[l], b_forget[l], q_gain[l], k_gain[l],
                   w_branch_a[l], w_branch_b[l], w_out[l], norm2_g[l], w_router_group[l],
                   b_router_group[l], w_router_expert[l], b_router_expert[l], w1[l], w3[l], w2[l])
    return x
```

```python
import functools
import math

import numpy as np
import jax
import jax.numpy as jnp
from jax import lax
from jax.experimental import pallas as pl
from jax.experimental.pallas import tpu as pltpu

F32 = jnp.float32
BF16 = jnp.bfloat16

D_MODEL = 1024
HEAD_DIM = 64
FOX_HEADS = 8
DIL_GROUPS = ((128, 1), (512, 4), (2048, 16))
DIL_HEADS_PER_GROUP = 4
N_DIL_GROUPS = len(DIL_GROUPS)
DIL_HEADS = N_DIL_GROUPS * DIL_HEADS_PER_GROUP
FOX_WIDTH = FOX_HEADS * HEAD_DIM
DIL_WIDTH = DIL_HEADS * HEAD_DIM
DIL_OUT_WIDTH = DIL_HEADS_PER_GROUP * HEAD_DIM
NUM_BUCKETS = 32
REL_MAX_DISTANCE = 2048
N_GROUPS = 4
EXPERTS_PER_GROUP = 8
N_EXPERTS = N_GROUPS * EXPERTS_PER_GROUP
EXPERT_HIDDEN = D_MODEL // 2
EPS = 1e-6
LOG2E = math.log2(math.e)

OFF_FOX_Q = 0
OFF_FOX_K = OFF_FOX_Q + FOX_WIDTH
OFF_FOX_V = OFF_FOX_K + FOX_WIDTH
OFF_FOX_F = OFF_FOX_V + FOX_WIDTH
OFF_DIL_Q = OFF_FOX_F + FOX_HEADS
OFF_DIL_K = OFF_DIL_Q + DIL_WIDTH
OFF_DIL_V = OFF_DIL_K + DIL_WIDTH
OFF_GATE_A = OFF_DIL_V + DIL_WIDTH
OFF_GATE_B = OFF_GATE_A + D_MODEL
N_IN = OFF_GATE_B + D_MODEL

LANES = 128
UNIT = 256
DIL_L = 128

U_GATE_A, U_GATE_B, U_FOX_Q, U_FOX_K, U_FOX_V, U_DIL, U_FORGET = 0, 4, 8, 10, 12, 14, 23
N_UNITS = 24
P_WIDTH = U_DIL * UNIT
N_SLABS = 3 * N_DIL_GROUPS
_KIND = (["gate"] * 8 + ["normq"] * 2 + ["normk"] * 2 + ["plain"] * 2
         + ["normq", "normk", "plain"] * 3 + ["forget"])

TM_INPROJ = 1024
TM_PROJ = 1024
TQ_FOX = 512
MOE_ROWS = 256
MOE_TILE = 512
XS_WIDTH = D_MODEL // 2 + LANES
SUBLANES = 8
TILE_ROWS = 2 * MOE_TILE + N_EXPERTS * SUBLANES
MOE_GROUPS = MOE_ROWS // SUBLANES
GATHER_BUFS = 3
VMEM_LIMIT = 56 * 1024 * 1024


def _dot(a, b):
    return jnp.dot(a, b, preferred_element_type=F32)


def _dot_nt(a, b):
    return lax.dot_general(a, b, (((1,), (1,)), ((), ())), preferred_element_type=F32)


def _split3(x):
    hi = x.astype(BF16)
    r1 = x - hi.astype(F32)
    mid = r1.astype(BF16)
    lo = (r1 - mid.astype(F32)).astype(BF16)
    return hi, mid, lo


def _ada_kernel(c_ref, w_ref, b_ref, o_ref):
    c = c_ref[...]
    s = c * jax.nn.sigmoid(c)
    s_hi = s.astype(BF16)
    s_lo = (s - s_hi.astype(F32)).astype(BF16)
    w = w_ref[...]
    w_hi = w.astype(BF16)
    w_lo = (w - w_hi.astype(F32)).astype(BF16)
    acc = _dot(s_hi, w_hi) + _dot(s_hi, w_lo) + _dot(s_lo, w_hi)
    o_ref[...] = acc + b_ref[...]


def _ada(c, w_ada, b_ada):
    B = c.shape[0]
    n_out = w_ada.shape[1]
    tn = 512
    return pl.pallas_call(
        _ada_kernel,
        grid=(n_out // tn,),
        in_specs=[pl.BlockSpec((B, D_MODEL), lambda j: (0, 0)),
                  pl.BlockSpec((D_MODEL, tn), lambda j: (0, j)),
                  pl.BlockSpec((1, tn), lambda j: (0, j))],
        out_specs=pl.BlockSpec((B, tn), lambda j: (0, j)),
        out_shape=jax.ShapeDtypeStruct((B, n_out), F32),
        name="ada_mod",
    )(c, w_ada, b_ada.reshape(1, n_out))


def _pack_bf16_pair(lo, hi):
    lo_bits = pltpu.bitcast(lo.astype(BF16).astype(F32), jnp.uint32) >> 16
    hi_bits = pltpu.bitcast(hi.astype(BF16).astype(F32), jnp.uint32) & jnp.uint32(0xFFFF0000)
    return lo_bits | hi_bits


def _unpack_bf16_pair(u):
    lo = pltpu.bitcast(u << 16, F32).astype(BF16)
    hi = pltpu.bitcast(u & jnp.uint32(0xFFFF0000), F32).astype(BF16)
    return lo, hi


def _inproj_kernel(x_ref, g_ref, sc_ref, sh_ref, w_ref, gain_ref, bd_ref, p_ref, f_ref, s_ref):
    x = x_ref[...]
    ms = jnp.mean(x * x, axis=-1, keepdims=True)
    h = x * lax.rsqrt(ms + EPS) * g_ref[...]
    h = h * (1.0 + sc_ref[0]) + sh_ref[0]
    hb = h.astype(BF16)

    def unit(u):
        cols = slice(u * UNIT, (u + 1) * UNIT)
        acc = _dot(hb, w_ref[:, cols])
        kind = _KIND[u]
        if kind == "gate":
            return jax.nn.sigmoid(acc)
        if kind in ("normq", "normk"):
            ss = _dot((acc * acc).astype(BF16), bd_ref[...])
            return acc * lax.rsqrt(ss * (1.0 / HEAD_DIM) + EPS) * gain_ref[:, cols]
        return acc

    for u in range(U_DIL):
        p_ref[:, u * UNIT:(u + 1) * UNIT] = unit(u).astype(BF16)
    for g in range(N_DIL_GROUPS):
        q, k, v = (unit(U_DIL + 3 * g + j) for j in range(3))
        s_ref[0, 3 * g] = _pack_bf16_pair(q[:, :LANES], k[:, :LANES])
        s_ref[0, 3 * g + 1] = _pack_bf16_pair(q[:, LANES:], k[:, LANES:])
        s_ref[0, 3 * g + 2] = _pack_bf16_pair(v[:, :LANES], v[:, LANES:])
    f_ref[...] = unit(U_FORGET)[:, :LANES]


def _inproj(x2d, norm_g, sc, sh, w_re, gain_row, S):
    T = x2d.shape[0]
    tm = TM_INPROJ
    per_b = S // tm
    bd = np.kron(np.eye(UNIT // HEAD_DIM), np.ones((HEAD_DIM, HEAD_DIM))).astype(np.float32)
    once = dict(pipeline_mode=pl.Buffered(1))
    return pl.pallas_call(
        _inproj_kernel,
        grid=(T // tm,),
        in_specs=[pl.BlockSpec((tm, D_MODEL), lambda i: (i, 0)),
                  pl.BlockSpec((1, D_MODEL), lambda i: (0, 0)),
                  pl.BlockSpec((1, 1, D_MODEL), lambda i: (i // per_b, 0, 0)),
                  pl.BlockSpec((1, 1, D_MODEL), lambda i: (i // per_b, 0, 0)),
                  pl.BlockSpec((D_MODEL, N_UNITS * UNIT), lambda i: (0, 0), **once),
                  pl.BlockSpec((1, N_UNITS * UNIT), lambda i: (0, 0), **once),
                  pl.BlockSpec((UNIT, UNIT), lambda i: (0, 0), **once)],
        out_specs=[pl.BlockSpec((tm, P_WIDTH), lambda i: (i, 0)),
                   pl.BlockSpec((tm, LANES), lambda i: (i, 0)),
                   pl.BlockSpec((1, N_SLABS, tm, LANES), lambda i: (i // per_b, 0, i % per_b, 0))],
        out_shape=[jax.ShapeDtypeStruct((T, P_WIDTH), BF16),
                   jax.ShapeDtypeStruct((T, LANES), F32),
                   jax.ShapeDtypeStruct((T // S, N_SLABS, S, LANES), jnp.uint32)],
        compiler_params=pltpu.CompilerParams(vmem_limit_bytes=VMEM_LIMIT),
        name="in_proj",
    )(x2d, norm_g, sc, sh, w_re, gain_row, jnp.asarray(bd, BF16))


def _fcum_kernel(f_ref, b_ref, tri_ref, o_ref):
    S = f_ref.shape[1]
    xf = f_ref[0] + b_ref[...]
    ls = (jnp.minimum(xf, 0.0) - jnp.log(1.0 + jnp.exp(-jnp.abs(xf)))) * LOG2E
    lst = ls.T
    carry = jnp.zeros((LANES, UNIT), F32)
    for blk in range(S // UNIT):
        seg = lst[:, blk * UNIT:(blk + 1) * UNIT]
        hi, mid, lo = _split3(seg)
        tri = tri_ref[...]
        res = _dot(hi, tri) + _dot(mid, tri) + _dot(lo, tri)
        o_ref[0, :, blk * UNIT:(blk + 1) * UNIT] = (res[:, :UNIT] + carry)[:FOX_HEADS]
        carry = carry + res[:, UNIT:]


def _fcum(fgt, b_forget):
    B, S, _ = fgt.shape
    brow = jnp.zeros((1, LANES), F32).at[0, :FOX_HEADS].set(b_forget)
    tri = np.concatenate([np.triu(np.ones((UNIT, UNIT))), np.ones((UNIT, UNIT))], axis=1)
    return pl.pallas_call(
        _fcum_kernel,
        grid=(B,),
        in_specs=[pl.BlockSpec((1, S, LANES), lambda b: (b, 0, 0)),
                  pl.BlockSpec((1, LANES), lambda b: (0, 0)),
                  pl.BlockSpec((UNIT, 2 * UNIT), lambda b: (0, 0))],
        out_specs=pl.BlockSpec((1, FOX_HEADS, S), lambda b: (b, 0, 0)),
        out_shape=jax.ShapeDtypeStruct((B, FOX_HEADS, S), F32),
        name="forget_cumsum",
    )(fgt, brow, jnp.asarray(tri, BF16))


def _fox_kernel(q_ref, k_ref, v_ref, ck_ref, o_ref):
    S = q_ref.shape[1]
    pair = pl.program_id(1)
    tq = TQ_FOX
    lane = lax.broadcasted_iota(jnp.int32, (1, LANES), 1)
    row = lax.broadcasted_iota(jnp.int32, (tq, tq), 0)
    col = lax.broadcasted_iota(jnp.int32, (tq, tq), 1)
    causal = col <= row
    cks = [ck_ref[0, pl.ds(2 * pair + hh, 1), :] for hh in range(2)]
    for t in range(S // tq):
        r0, r1 = t * tq, (t + 1) * tq
        qt = q_ref[0, r0:r1, :]
        outs = []
        for hh in range(2):
            hsel = (lane >= HEAD_DIM) == bool(hh)
            qm = jnp.where(hsel, qt, jnp.zeros_like(qt))
            ck = cks[hh]
            s = _dot_nt(qm, k_ref[0, :r1, :]) - ck[:, :r1]
            s_d = jnp.where(causal, s[:, r0:], -jnp.inf)
            s = jnp.concatenate([s[:, :r0], s_d], axis=1) if t > 0 else s_d
            m = jnp.max(s, axis=-1, keepdims=True)
            p = jnp.exp2(s - m)
            l = jnp.sum(p, axis=-1, keepdims=True)
            outs.append(_dot(p.astype(BF16), v_ref[0, :r1, :]) / l)
        o_ref[0, r0:r1, :] = jnp.where(lane < HEAD_DIM, outs[0], outs[1]).astype(BF16)


def _fox(p3, ck):
    B, S, _ = p3.shape
    nq, nk, nv = (U_FOX_Q * UNIT // LANES, U_FOX_K * UNIT // LANES, U_FOX_V * UNIT // LANES)
    return pl.pallas_call(
        _fox_kernel,
        grid=(B, FOX_HEADS // 2),
        in_specs=[pl.BlockSpec((1, S, LANES), lambda b, p: (b, 0, nq + p)),
                  pl.BlockSpec((1, S, LANES), lambda b, p: (b, 0, nk + p)),
                  pl.BlockSpec((1, S, LANES), lambda b, p: (b, 0, nv + p)),
                  pl.BlockSpec((1, FOX_HEADS, S), lambda b, p: (b, 0, 0))],
        out_specs=pl.BlockSpec((1, S, LANES), lambda b, p: (b, 0, p)),
        out_shape=jax.ShapeDtypeStruct((B, S, FOX_WIDTH), BF16),
        compiler_params=pltpu.CompilerParams(vmem_limit_bytes=VMEM_LIMIT),
        name="fox_attn",
    )(p3, p3, p3, ck)


def _t5_bucket(dist):
    max_exact = NUM_BUCKETS // 2
    d = np.maximum(dist, 1).astype(np.float32)
    large = max_exact + (np.log(d / max_exact) / np.log(REL_MAX_DISTANCE / max_exact)
                         * (NUM_BUCKETS - max_exact)).astype(np.int32)
    large = np.minimum(large, NUM_BUCKETS - 1)
    return np.where(dist < max_exact, dist, large).astype(np.int32)


def _relbias_kernel(tab_ref, bucket_ref, valid_ref, o_ref):
    g = pl.program_id(0)
    bk = bucket_ref[0]
    vd = valid_ref[0]
    for hs in range(DIL_HEADS_PER_GROUP):
        acc = jnp.zeros(bk.shape, F32)
        for b in range(NUM_BUCKETS):
            acc = jnp.where(bk == b, tab_ref[b, g * DIL_HEADS_PER_GROUP + hs], acc)
        bias = jnp.where(vd != 0, acc * LOG2E, -jnp.inf)
        o_ref[0, hs] = bias
        col = lax.broadcasted_iota(jnp.int32, bias.shape, 1)
        o_ref[1, hs] = jnp.where(col >= DIL_L, bias, -jnp.inf)


def _relbias(table):
    L = DIL_L
    i = np.arange(L)[:, None]
    j = np.arange(2 * L)[None, :]
    m = L + i - j
    valid = ((m >= 0) & (m <= L)).astype(np.int32)
    buckets = np.stack([_t5_bucket(np.clip(m, 0, None) * d) for _, d in DIL_GROUPS])
    valids = np.stack([valid] * N_DIL_GROUPS)
    return pl.pallas_call(
        _relbias_kernel,
        grid=(N_DIL_GROUPS,),
        in_specs=[pl.BlockSpec(memory_space=pltpu.SMEM),
                  pl.BlockSpec((1, L, 2 * L), lambda g: (g, 0, 0)),
                  pl.BlockSpec((1, L, 2 * L), lambda g: (g, 0, 0))],
        out_specs=pl.BlockSpec((2, DIL_HEADS_PER_GROUP, L, 2 * L), lambda g: (0, g, 0, 0)),
        out_shape=jax.ShapeDtypeStruct((2, DIL_HEADS, L, 2 * L), F32),
        name="rel_bias",
    )(table, jnp.asarray(buckets), jnp.asarray(valids))


def _dil_rows(start, d):
    return pl.ds(start, DIL_L) if d == 1 else pl.ds(start, DIL_L, stride=d)


def _dil_block(qkv_ref, bias_ref, m_scr, l_scr, acc_scr, g, d, r, n):
    L = DIL_L
    lane = lax.broadcasted_iota(jnp.int32, (1, LANES), 1)
    first = 1 - jnp.minimum(n, 1)
    cur = _dil_rows(r + d * (n * L), d)
    prev = _dil_rows(r + d * (jnp.maximum(n - 1, 0) * L), d)
    v_cur = _unpack_bf16_pair(qkv_ref[0, 3 * g + 2, cur, :])
    v_prev = _unpack_bf16_pair(qkv_ref[0, 3 * g + 2, prev, :])
    for pr in range(2):
        qt, k_cur = _unpack_bf16_pair(qkv_ref[0, 3 * g + pr, cur, :])
        _, k_prev = _unpack_bf16_pair(qkv_ref[0, 3 * g + pr, prev, :])
        kt = jnp.concatenate([k_prev, k_cur], axis=0)
        vt = jnp.concatenate([v_prev[pr], v_cur[pr]], axis=0)
        ms, ls, accs = [], [], []
        for hh in range(2):
            hsel = (lane >= HEAD_DIM) == bool(hh)
            qm = jnp.where(hsel, qt, jnp.zeros_like(qt))
            s = _dot_nt(qm, kt) + bias_ref[first, DIL_HEADS_PER_GROUP * g + 2 * pr + hh]
            m = jnp.max(s, axis=-1, keepdims=True)
            p = jnp.exp2(s - m)
            ms.append(m)
            ls.append(jnp.sum(p, axis=-1, keepdims=True))
            accs.append(_dot(p.astype(BF16), vt))
        low = lane < HEAD_DIM
        m_b = jnp.where(low, ms[0], ms[1])
        l_b = jnp.where(low, ls[0], ls[1])
        acc_b = jnp.where(low, accs[0], accs[1])
        if g == 0:
            m_scr[pr, cur, :] = m_b
            l_scr[pr, cur, :] = l_b
            acc_scr[pr, cur, :] = acc_b
        else:
            m_o = m_scr[pr, cur, :]
            m_n = jnp.maximum(m_o, m_b)
            a_o = jnp.exp2(m_o - m_n)
            a_b = jnp.exp2(m_b - m_n)
            m_scr[pr, cur, :] = m_n
            l_scr[pr, cur, :] = l_scr[pr, cur, :] * a_o + l_b * a_b
            acc_scr[pr, cur, :] = acc_scr[pr, cur, :] * a_o + acc_b * a_b


def _dil_kernel(qkv_ref, bias_ref, o_ref, m_scr, l_scr, acc_scr):
    S = o_ref.shape[1]
    for g, (window, d) in enumerate(DIL_GROUPS):
        nb = S // window

        def body(it, carry, g=g, d=d, nb=nb):
            _dil_block(qkv_ref, bias_ref, m_scr, l_scr, acc_scr, g, d, it // nb, it % nb)
            return carry
        lax.fori_loop(0, d * nb, body, 0, unroll=4)
    for pr in range(2):
        o_ref[0, :, pr * LANES:(pr + 1) * LANES] = (acc_scr[pr] / l_scr[pr]).astype(BF16)


def _dil(slabs, bias):
    B, _, S, _ = slabs.shape
    for window, d in DIL_GROUPS:
        assert window // d == DIL_L and S % window == 0
    stat = pltpu.VMEM((2, S, LANES), F32)
    return pl.pallas_call(
        _dil_kernel,
        grid=(B,),
        in_specs=[pl.BlockSpec((1, N_SLABS, S, LANES), lambda b: (b, 0, 0, 0)),
                  pl.BlockSpec(bias.shape, lambda b: (0, 0, 0, 0))],
        out_specs=pl.BlockSpec((1, S, DIL_OUT_WIDTH), lambda b: (b, 0, 0)),
        out_shape=jax.ShapeDtypeStruct((B, S, DIL_OUT_WIDTH), BF16),
        scratch_shapes=[stat, stat, stat],
        compiler_params=pltpu.CompilerParams(vmem_limit_bytes=VMEM_LIMIT),
        name="dil_attn",
    )(slabs, bias)


def _outproj_kernel(x_ref, ya_ref, yd_ref, ga_ref, gb_ref,
                    g1_ref, sc_ref, sh_ref, ng_ref, wa_ref, wb_ref, wo_ref, wr_ref, br_ref,
                    x1_ref, h2_ref, lg_ref):
    a = _dot(ya_ref[...], wa_ref[...])
    bm = _dot(yd_ref[...], wb_ref[...])
    merged = ga_ref[...].astype(F32) * a + gb_ref[...].astype(F32) * bm
    out = _dot(merged.astype(BF16), wo_ref[...])
    x1 = x_ref[...] + g1_ref[0] * out
    x1_ref[...] = x1
    ms = jnp.mean(x1 * x1, axis=-1, keepdims=True)
    h = x1 * lax.rsqrt(ms + EPS) * ng_ref[...]
    h = h * (1.0 + sc_ref[0]) + sh_ref[0]
    hb = h.astype(BF16)
    h2_ref[...] = hb
    lg_ref[...] = _dot(hb, wr_ref[...]) + br_ref[...]


def _outproj(x2d, ya2d, yd2d, p2d, g1, sc2, sh2, norm_g, wa, wb, wo, wr, br, S):
    T = x2d.shape[0]
    tm = TM_PROJ
    per_b = S // tm
    row = lambda w: pl.BlockSpec((tm, w), lambda i: (i, 0))
    full = lambda a: pl.BlockSpec(a.shape, lambda i: (0,) * a.ndim)
    mod = pl.BlockSpec((1, 1, D_MODEL), lambda i: (i // per_b, 0, 0))
    return pl.pallas_call(
        _outproj_kernel,
        grid=(T // tm,),
        in_specs=[row(D_MODEL), row(FOX_WIDTH), row(DIL_OUT_WIDTH)]
                 + [pl.BlockSpec((tm, D_MODEL), lambda i: (i, U_GATE_A * UNIT // D_MODEL)),
                    pl.BlockSpec((tm, D_MODEL), lambda i: (i, U_GATE_B * UNIT // D_MODEL)),
                    mod, mod, mod, full(norm_g), full(wa), full(wb), full(wo), full(wr), full(br)],
        out_specs=[row(D_MODEL), row(D_MODEL), row(LANES)],
        out_shape=[jax.ShapeDtypeStruct((T, D_MODEL), F32),
                   jax.ShapeDtypeStruct((T, D_MODEL), BF16),
                   jax.ShapeDtypeStruct((T, LANES), F32)],
        compiler_params=pltpu.CompilerParams(vmem_limit_bytes=VMEM_LIMIT),
        name="out_proj",
    )(x2d, ya2d, yd2d, p2d, p2d, g1, sc2, sh2, norm_g, wa, wb, wo, wr, br)


def _dispatch_kernel(lg_ref, h_ref, tri_ref, xs_ref, cnt_ref, pos_ref):
    tt = lg_ref.shape[0]
    lt = lg_ref[...].T
    row = lambda i: lt[i:i + 1, :]
    neg = -jnp.inf
    g = [row(i) for i in range(N_GROUPS)]
    gmax = functools.reduce(jnp.maximum, g)
    gidx = jnp.full(gmax.shape, N_GROUPS - 1, jnp.int32)
    for i in reversed(range(N_GROUPS - 1)):
        gidx = jnp.where(g[i] == gmax, i, gidx)
    gsum = sum(jnp.exp(gi - gmax) for gi in g)
    el = []
    for j in range(EXPERTS_PER_GROUP):
        v = row(N_GROUPS + EXPERTS_PER_GROUP * (N_GROUPS - 1) + j)
        for gg in reversed(range(N_GROUPS - 1)):
            v = jnp.where(gidx == gg, row(N_GROUPS + EXPERTS_PER_GROUP * gg + j), v)
        el.append(v)

    def top(vals):
        best = functools.reduce(jnp.maximum, vals)
        idx = jnp.full(best.shape, EXPERTS_PER_GROUP - 1, jnp.int32)
        for j in reversed(range(EXPERTS_PER_GROUP - 1)):
            idx = jnp.where(vals[j] == best, j, idx)
        return best, idx

    v1, i1 = top(el)
    v2, i2 = top([jnp.where(i1 == j, neg, el[j]) for j in range(EXPERTS_PER_GROUP)])
    t = jnp.exp(v2 - v1)
    den = (1.0 + t) * gsum
    wts = [1.0 / den, t / den]
    eid = [gidx * EXPERTS_PER_GROUP + i1, gidx * EXPERTS_PER_GROUP + i2]

    esub = lax.broadcasted_iota(jnp.int32, (N_EXPERTS, tt), 0)
    ohf = jnp.concatenate([jnp.where(esub == eid[k], 1.0, 0.0) for k in range(2)], axis=1)
    res = _dot(ohf.astype(BF16), tri_ref[...])
    prefix, cnt = res[:, :2 * tt], res[:, 2 * tt:]
    cnt = (((cnt.astype(jnp.int32) + (SUBLANES - 1)) // SUBLANES) * SUBLANES).astype(F32)
    esub_c = lax.broadcasted_iota(jnp.int32, cnt.shape, 0)
    start = jnp.zeros_like(cnt)
    for e in range(N_EXPERTS - 1):
        start = start + jnp.where(esub_c > e, cnt[e:e + 1, :], 0.0)
    start_w = jnp.concatenate([start] * (2 * tt // LANES), axis=1)
    pos = jnp.sum(ohf * (start_w + prefix), axis=0, keepdims=True)
    pos_k = [pos[:, :tt], pos[:, tt:]]

    n_rows = xs_ref.shape[0]
    psub = lax.broadcasted_iota(jnp.int32, (n_rows, tt), 0).astype(F32)
    pm = [jnp.where(psub == pos_k[k], 1.0, 0.0).astype(BF16) for k in range(2)]
    xs = _dot(pm[0] + pm[1], h_ref[...])
    wsub = lax.broadcasted_iota(jnp.int32, (LANES, tt), 0)
    ws = jnp.zeros((n_rows, LANES), F32)
    for k in range(2):
        parts = _split3(wts[k])
        wrows = jnp.zeros((LANES, tt), F32)
        for j in range(3):
            wrows = jnp.where(wsub == j, parts[j].astype(F32), wrows)
        ws = ws + _dot_nt(pm[k], wrows.astype(BF16))
    half = D_MODEL // 2
    xs_ref[:, :half] = _pack_bf16_pair(xs[:, :half], xs[:, half:])
    xs_ref[:, half:] = pltpu.bitcast(ws, jnp.uint32)
    cnt_ref[0] = cnt.astype(jnp.int32)
    posr = jnp.where(wsub == 0, pos_k[0], jnp.where(wsub == 1, pos_k[1], 0.0))
    pos_ref[...] = posr.T


def _dispatch(logits, h2):
    T = logits.shape[0]
    tt = MOE_TILE
    n_tiles = T // tt
    tri = np.concatenate([np.triu(np.ones((2 * tt, 2 * tt)), 1), np.ones((2 * tt, LANES))], axis=1)
    return pl.pallas_call(
        _dispatch_kernel,
        grid=(n_tiles,),
        in_specs=[pl.BlockSpec((tt, LANES), lambda i: (i, 0)),
                  pl.BlockSpec((tt, D_MODEL), lambda i: (i, 0)),
                  pl.BlockSpec(tri.shape, lambda i: (0, 0))],
        out_specs=[pl.BlockSpec((TILE_ROWS, XS_WIDTH), lambda i: (i, 0)),
                   pl.BlockSpec((1, N_EXPERTS, LANES), lambda i: (i, 0, 0)),
                   pl.BlockSpec((tt, LANES), lambda i: (i, 0))],
        out_shape=[jax.ShapeDtypeStruct((n_tiles * TILE_ROWS, XS_WIDTH), jnp.uint32),
                   jax.ShapeDtypeStruct((n_tiles, N_EXPERTS, LANES), jnp.int32),
                   jax.ShapeDtypeStruct((T, LANES), F32)],
        compiler_params=pltpu.CompilerParams(vmem_limit_bytes=VMEM_LIMIT),
        name="moe_dispatch",
    )(logits, h2, jnp.asarray(tri, BF16))


def _plan_kernel(cnt_ref, be_ref, nv_ref, grp_ref, used_ref, cs_ref):
    n_tiles = cnt_ref.shape[0]
    n_blk = be_ref.shape[0]
    rows = MOE_ROWS
    row_shift = rows.bit_length() - 1
    grp_shift = SUBLANES.bit_length() - 1
    assert rows == 1 << row_shift and SUBLANES == 1 << grp_shift

    def tile_starts(t, c):
        def per_e(e, acc):
            cs_ref[t * N_EXPERTS + e] = acc
            return acc + cnt_ref[t, e]
        used_ref[t] = lax.fori_loop(0, N_EXPERTS, per_e, 0, unroll=8)
        return c
    lax.fori_loop(0, n_tiles, tile_starts, 0)

    def clear(b, c):
        nv_ref[b] = 0
        return c
    lax.fori_loop(0, n_blk, clear, 0)

    def clear_groups(g, c):
        grp_ref[g] = 0
        return c
    lax.fori_loop(0, n_blk * MOE_GROUPS, clear_groups, 0, unroll=8)

    def per_expert(e, b):
        g0 = b * MOE_GROUPS

        def per_tile(t, tot):
            c = cnt_ref[t, e]
            src = t * TILE_ROWS + cs_ref[t * N_EXPERTS + e]
            first = g0 + lax.shift_right_logical(tot, grp_shift)

            def per_group(k, cc):
                grp_ref[first + k] = src + k * SUBLANES
                return cc
            lax.fori_loop(0, lax.shift_right_logical(c, grp_shift), per_group, 0)
            return tot + c
        tot = lax.fori_loop(0, n_tiles, per_tile, 0)

        def per_block(j, c):
            be_ref[b + j] = e
            nv_ref[b + j] = jnp.minimum(rows, tot - j * rows)
            return c
        nb = lax.shift_right_logical(tot + rows - 1, row_shift)
        lax.fori_loop(0, nb, per_block, 0)
        return b + nb
    n_used = lax.fori_loop(0, N_EXPERTS, per_expert, 0)

    def unused(b, c):
        be_ref[b] = be_ref[n_used - 1]
        return c
    lax.fori_loop(n_used, n_blk, unused, 0)


def _plan(cnt, n_blk):
    n_tiles = cnt.shape[0]
    smem = pl.BlockSpec(memory_space=pltpu.SMEM)
    i32 = lambda n: jax.ShapeDtypeStruct((n,), jnp.int32)
    return pl.pallas_call(
        _plan_kernel,
        in_specs=[smem],
        out_specs=[smem] * 4,
        out_shape=[i32(n_blk), i32(n_blk), i32(n_blk * MOE_GROUPS), i32(n_tiles)],
        scratch_shapes=[pltpu.SMEM((n_tiles * N_EXPERTS,), jnp.int32)],
        name="moe_plan",
    )(cnt)


def _pow2_pieces(n, fn):
    for b in reversed(range(SUBLANES.bit_length() - 1, MOE_ROWS.bit_length())):
        size = 1 << b

        @pl.when((n & size) != 0)
        def _():
            fn((n >> (b + 1)) << (b + 1), size)


def _moe_kernel(be_ref, nv_ref, grp_ref, used_ref,
                w1_ref, w3_ref, w2_ref, xs_hbm, ys_hbm, xbuf, ybuf, wb1, wb3, wb2, gsem, ssem):
    i = pl.program_id(0)
    last = pl.num_programs(0) - 1
    slot = i % 2
    nv = nv_ref[i]
    half = D_MODEL // 2
    grp_shift = SUBLANES.bit_length() - 1

    def group_row(blk, g):
        return pl.multiple_of(grp_ref[blk * MOE_GROUPS + g], SUBLANES)

    def gather(blk, s):
        top = jnp.maximum(lax.shift_right_logical(nv_ref[blk], grp_shift) - 1, 0)
        for g in range(MOE_GROUPS):
            src = group_row(blk, jnp.minimum(g, top))
            pltpu.make_async_copy(xs_hbm.at[pl.ds(src, SUBLANES)],
                                  xbuf.at[s, pl.ds(g * SUBLANES, SUBLANES)], gsem.at[s]).start()

    def wait_gather(s):
        pltpu.make_async_copy(xs_hbm.at[pl.ds(0, MOE_ROWS)], xbuf.at[s], gsem.at[s]).wait()

    def scatter(blk, s):
        def body(g, c):
            r = pl.multiple_of(g * SUBLANES, SUBLANES)
            pltpu.make_async_copy(ybuf.at[s, pl.ds(r, SUBLANES)],
                                  ys_hbm.at[pl.ds(group_row(blk, g), SUBLANES)], ssem.at[s]).start()
            return c
        lax.fori_loop(0, lax.shift_right_logical(nv_ref[blk], grp_shift), body, 0)

    def wait_scatter(s, count):
        _pow2_pieces(count, lambda a, size: pltpu.make_async_copy(
            ybuf.at[s, pl.ds(0, size)], ys_hbm.at[pl.ds(0, size)], ssem.at[s]).wait())

    @pl.when(i == 0)
    def _():
        @pl.when(nv > 0)
        def _():
            gather(0, 0)
            gather(jnp.minimum(1, last), 1)

        ybuf[1] = jnp.zeros(ybuf.shape[1:], ybuf.dtype)
        n_tiles = used_ref.shape[0]

        def fill(t, c):
            row0 = t * TILE_ROWS + used_ref[t]
            _pow2_pieces(TILE_ROWS - used_ref[t], lambda a, size: pltpu.make_async_copy(
                ybuf.at[1, pl.ds(0, size)], ys_hbm.at[pl.ds(pl.multiple_of(row0 + a, SUBLANES), size)],
                ssem.at[1]).start())
            return c
        lax.fori_loop(0, n_tiles, fill, 0)

        def drain(t, c):
            wait_scatter(1, TILE_ROWS - used_ref[t])
            return c
        lax.fori_loop(0, n_tiles, drain, 0)

    @pl.when(i >= 2)
    def _():
        wait_scatter(slot, nv_ref[jnp.maximum(i - 2, 0)])

    xslot = i % GATHER_BUFS
    issuer_used = jnp.where(i >= 2, nv_ref[jnp.maximum(i - 2, 0)], nv_ref[0]) > 0

    @pl.when((nv == 0) & (i > 0) & issuer_used)
    def _():
        wait_gather(xslot)

    @pl.when(nv > 0)
    def _():
        e = be_ref[i]
        e_prev = be_ref[jnp.maximum(i - 1, 0)]

        @pl.when((i == 0) | (e != e_prev))
        def _():
            wb1[...] = w1_ref[0].astype(BF16)
            wb3[...] = w3_ref[0].astype(BF16)
            wb2[...] = w2_ref[0].astype(BF16)

        wait_gather(xslot)
        gather(jnp.minimum(i + 2, last), (i + 2) % GATHER_BUFS)
        u = xbuf[xslot]
        xa, xb = _unpack_bf16_pair(u[:, :half])
        wv = pltpu.bitcast(u[:, half:], F32)
        roww = wv[:, 0:1] + wv[:, 1:2] + wv[:, 2:3]
        a = _dot(xa, wb1[:half, :]) + _dot(xb, wb1[half:, :])
        b = _dot(xa, wb3[:half, :]) + _dot(xb, wb3[half:, :])
        hmid = (a * jax.nn.sigmoid(a) * b).astype(BF16)
        y = _dot(hmid, wb2[...]) * roww
        ybuf[slot] = _pack_bf16_pair(y[:, :half], y[:, half:])
        scatter(i, slot)

    @pl.when(i == last)
    def _():
        @pl.when((last >= 1) & (nv_ref[jnp.maximum(last - 1, 0)] > 0))
        def _():
            wait_gather((last + 1) % GATHER_BUFS)

        @pl.when(nv > 0)
        def _():
            wait_gather((last + 2) % GATHER_BUFS)

        @pl.when(last >= 1)
        def _():
            wait_scatter(1 - slot, nv_ref[jnp.maximum(last - 1, 0)])
        wait_scatter(slot, nv)


def _moe(xs, plan, w1, w3, w2):
    n_blk = plan[0].shape[0]
    rows = MOE_ROWS
    half = D_MODEL // 2
    hbm = pl.BlockSpec(memory_space=pl.ANY)
    wspec = lambda shape: pl.BlockSpec((1,) + shape, lambda i, be, *_: (be[i], 0, 0))
    grid_spec = pltpu.PrefetchScalarGridSpec(
        num_scalar_prefetch=4,
        grid=(n_blk,),
        in_specs=[wspec((D_MODEL, EXPERT_HIDDEN)), wspec((D_MODEL, EXPERT_HIDDEN)),
                  wspec((EXPERT_HIDDEN, D_MODEL)), hbm],
        out_specs=hbm,
        scratch_shapes=[pltpu.VMEM((GATHER_BUFS, rows, XS_WIDTH), jnp.uint32),
                        pltpu.VMEM((2, rows, half), jnp.uint32),
                        pltpu.VMEM((D_MODEL, EXPERT_HIDDEN), BF16),
                        pltpu.VMEM((D_MODEL, EXPERT_HIDDEN), BF16),
                        pltpu.VMEM((EXPERT_HIDDEN, D_MODEL), BF16),
                        pltpu.SemaphoreType.DMA((GATHER_BUFS,)),
                        pltpu.SemaphoreType.DMA((2,))])
    return pl.pallas_call(
        _moe_kernel,
        grid_spec=grid_spec,
        out_shape=jax.ShapeDtypeStruct((xs.shape[0], half), jnp.uint32),
        compiler_params=pltpu.CompilerParams(dimension_semantics=("arbitrary",),
                                             vmem_limit_bytes=VMEM_LIMIT),
        name="moe_ffn",
    )(*plan, w1, w3, w2, xs)


def _combine_kernel(x1_ref, ys_ref, pos_ref, g2_ref, o_ref):
    tt = x1_ref.shape[0]
    half = D_MODEL // 2
    pos = pos_ref[...]
    pcol = lax.broadcasted_iota(jnp.int32, (tt, ys_ref.shape[0]), 1).astype(F32)
    sel = (jnp.where(pcol == pos[:, 0:1], 1.0, 0.0) + jnp.where(pcol == pos[:, 1:2], 1.0, 0.0))
    sel = sel.astype(BF16)
    lo, hi = _unpack_bf16_pair(ys_ref[...])
    g2 = g2_ref[0]
    x1 = x1_ref[...]
    o_ref[:, :half] = x1[:, :half] + g2[:, :half] * _dot(sel, lo)
    o_ref[:, half:] = x1[:, half:] + g2[:, half:] * _dot(sel, hi)


def _combine(x1, ys, pos, g2, S):
    T = x1.shape[0]
    tt = MOE_TILE
    per_b = S // tt
    return pl.pallas_call(
        _combine_kernel,
        grid=(T // tt,),
        in_specs=[pl.BlockSpec((tt, D_MODEL), lambda i: (i, 0)),
                  pl.BlockSpec((TILE_ROWS, D_MODEL // 2), lambda i: (i, 0)),
                  pl.BlockSpec((tt, LANES), lambda i: (i, 0)),
                  pl.BlockSpec((1, 1, D_MODEL), lambda i: (i // per_b, 0, 0))],
        out_specs=pl.BlockSpec((tt, D_MODEL), lambda i: (i, 0)),
        out_shape=jax.ShapeDtypeStruct((T, D_MODEL), F32),
        compiler_params=pltpu.CompilerParams(vmem_limit_bytes=VMEM_LIMIT),
        name="moe_combine",
    )(x1, ys, pos, g2)


def _prep_w_in(w_in):
    dq = w_in[:, OFF_DIL_Q:OFF_DIL_K]
    dk = w_in[:, OFF_DIL_K:OFF_DIL_V]
    dv = w_in[:, OFF_DIL_V:OFF_GATE_A]
    dil = []
    for g in range(N_DIL_GROUPS):
        cs = slice(g * DIL_OUT_WIDTH, (g + 1) * DIL_OUT_WIDTH)
        dil += [dq[:, cs], dk[:, cs], dv[:, cs]]
    pad = jnp.zeros((D_MODEL, UNIT - FOX_HEADS), w_in.dtype)
    cols = [w_in[:, OFF_GATE_A:OFF_GATE_B], w_in[:, OFF_GATE_B:N_IN],
            w_in[:, OFF_FOX_Q:OFF_FOX_K], w_in[:, OFF_FOX_K:OFF_FOX_V], w_in[:, OFF_FOX_V:OFF_FOX_F],
            *dil, w_in[:, OFF_FOX_F:OFF_DIL_Q], pad]
    return jnp.concatenate(cols, axis=1).astype(BF16)


def _prep_gain(q_gain, k_gain):
    qs = HEAD_DIM ** -0.5 * LOG2E
    ones = jnp.ones((UNIT,), F32)
    fq = q_gain[:FOX_HEADS].reshape(-1) * qs
    fk = k_gain[:FOX_HEADS].reshape(-1)
    dq = q_gain[FOX_HEADS:].reshape(-1) * qs
    dk = k_gain[FOX_HEADS:].reshape(-1)
    dil = []
    for g in range(N_DIL_GROUPS):
        cs = slice(g * DIL_OUT_WIDTH, (g + 1) * DIL_OUT_WIDTH)
        dil += [dq[cs], dk[cs], ones]
    parts = [ones] * 8 + [fq, fk, ones, ones] + dil + [ones]
    return jnp.concatenate(parts).reshape(1, N_UNITS * UNIT)


def _layer(x, mod, rel_bias_table, norm1_g, w_in, b_forget, q_gain, k_gain, w_branch_a, w_branch_b,
           w_out, norm2_g, w_rg, b_rg, w_re, b_re, w1, w3, w2):
    B, S, D = x.shape
    T = B * S
    sh1, sc1, g1, sh2, sc2, g2 = [m.reshape(B, 1, D) for m in jnp.split(mod, 6, axis=-1)]
    x2d = x.reshape(T, D)

    p2d, fgt, slabs = _inproj(x2d, norm1_g.reshape(1, D), sc1, sh1, _prep_w_in(w_in),
                              _prep_gain(q_gain, k_gain), S)
    p3 = p2d.reshape(B, S, P_WIDTH)
    ck = _fcum(fgt.reshape(B, S, LANES), b_forget)
    ya = _fox(p3, ck)
    yd = _dil(slabs, _relbias(rel_bias_table))

    n_router = N_GROUPS + N_EXPERTS
    wr = jnp.concatenate([w_rg, w_re, jnp.zeros((D, LANES - n_router), F32)], axis=1).astype(BF16)
    br = jnp.concatenate([b_rg, b_re, jnp.zeros((LANES - n_router,), F32)]).reshape(1, LANES)
    x1, h2, logits = _outproj(x2d, ya.reshape(T, FOX_WIDTH), yd.reshape(T, DIL_OUT_WIDTH), p2d,
                              g1, sc2, sh2, norm2_g.reshape(1, D),
                              w_branch_a.astype(BF16), w_branch_b.astype(BF16), w_out.astype(BF16),
                              wr, br, S)
    xs, cnt, pos = _dispatch(logits, h2)
    cnt2 = cnt[:, :, 0]
    n_blk = cnt.shape[0] * TILE_ROWS // MOE_ROWS + N_EXPERTS
    plan = _plan(cnt2, n_blk)
    ys = _moe(xs, plan, w1, w3, w2)
    out = _combine(x1, ys, pos, g2, S)
    return out.reshape(B, S, D)


def kernel(x, c, rel_bias_table, w_ada, b_ada, norm1_g, w_in, b_forget, q_gain, k_gain, w_branch_a, w_branch_b, w_out, norm2_g, w_router_group, b_router_group, w_router_expert, b_router_expert, w1, w3, w2):
    depth = w_ada.shape[0]
    for l in range(depth):
        mod = _ada(c, w_ada[l], b_ada[l])
        x = _layer(x, mod, rel_bias_table, norm1_g[l], w_in[l], b_forget[l], q_gain[l], k_gain[l],
                   w_branch_a[l], w_branch_b[l], w_out[l], norm2_g[l], w_router_group[l],
                   b_router_group[l], w_router_expert[l], b_router_expert[l], w1[l], w3[l], w2[l])
    return x
```

```python
import functools
import math

import numpy as np
import jax
import jax.numpy as jnp
from jax import lax
from jax.experimental import pallas as pl
from jax.experimental.pallas import tpu as pltpu

F32 = jnp.float32
BF16 = jnp.bfloat16

D_MODEL = 1024
HEAD_DIM = 64
FOX_HEADS = 8
DIL_GROUPS = ((128, 1), (512, 4), (2048, 16))
DIL_HEADS_PER_GROUP = 4
N_DIL_GROUPS = len(DIL_GROUPS)
DIL_HEADS = N_DIL_GROUPS * DIL_HEADS_PER_GROUP
FOX_WIDTH = FOX_HEADS * HEAD_DIM
DIL_WIDTH = DIL_HEADS * HEAD_DIM
DIL_OUT_WIDTH = DIL_HEADS_PER_GROUP * HEAD_DIM
NUM_BUCKETS = 32
REL_MAX_DISTANCE = 2048
N_GROUPS = 4
EXPERTS_PER_GROUP = 8
N_EXPERTS = N_GROUPS * EXPERTS_PER_GROUP
EXPERT_HIDDEN = D_MODEL // 2
EPS = 1e-6
LOG2E = math.log2(math.e)

OFF_FOX_Q = 0
OFF_FOX_K = OFF_FOX_Q + FOX_WIDTH
OFF_FOX_V = OFF_FOX_K + FOX_WIDTH
OFF_FOX_F = OFF_FOX_V + FOX_WIDTH
OFF_DIL_Q = OFF_FOX_F + FOX_HEADS
OFF_DIL_K = OFF_DIL_Q + DIL_WIDTH
OFF_DIL_V = OFF_DIL_K + DIL_WIDTH
OFF_GATE_A = OFF_DIL_V + DIL_WIDTH
OFF_GATE_B = OFF_GATE_A + D_MODEL
N_IN = OFF_GATE_B + D_MODEL

LANES = 128
UNIT = 256
DIL_L = 128

U_GATE_A, U_GATE_B, U_FOX_Q, U_FOX_K, U_FOX_V, U_DIL, U_FORGET = 0, 4, 8, 10, 12, 14, 23
N_UNITS = 24
P_WIDTH = U_DIL * UNIT
N_SLABS = 3 * N_DIL_GROUPS
_KIND = (["gate"] * 8 + ["normq"] * 2 + ["normk"] * 2 + ["plain"] * 2
         + ["normq", "normk", "plain"] * 3 + ["forget"])

TM_INPROJ = 1024
TM_PROJ = 1024
TQ_FOX = 512
MOE_ROWS = 256
MOE_TILE = 512
XS_WIDTH = D_MODEL // 2 + LANES
SUBLANES = 8
TILE_ROWS = 2 * MOE_TILE + N_EXPERTS * SUBLANES
MOE_GROUPS = MOE_ROWS // SUBLANES
GATHER_BUFS = 3
VMEM_LIMIT = 56 * 1024 * 1024


def _dot(a, b):
    return jnp.dot(a, b, preferred_element_type=F32)


def _dot_nt(a, b):
    return lax.dot_general(a, b, (((1,), (1,)), ((), ())), preferred_element_type=F32)


def _split3(x):
    hi = x.astype(BF16)
    r1 = x - hi.astype(F32)
    mid = r1.astype(BF16)
    lo = (r1 - mid.astype(F32)).astype(BF16)
    return hi, mid, lo


def _ada_kernel(c_ref, w_ref, b_ref, o_ref):
    c = c_ref[...]
    s = c * jax.nn.sigmoid(c)
    s_hi = s.astype(BF16)
    s_lo = (s - s_hi.astype(F32)).astype(BF16)
    w = w_ref[...]
    w_hi = w.astype(BF16)
    w_lo = (w - w_hi.astype(F32)).astype(BF16)
    acc = _dot(s_hi, w_hi) + _dot(s_hi, w_lo) + _dot(s_lo, w_hi)
    o_ref[...] = acc + b_ref[...]


def _ada(c, w_ada, b_ada):
    B = c.shape[0]
    n_out = w_ada.shape[1]
    tn = 512
    return pl.pallas_call(
        _ada_kernel,
        grid=(n_out // tn,),
        in_specs=[pl.BlockSpec((B, D_MODEL), lambda j: (0, 0)),
                  pl.BlockSpec((D_MODEL, tn), lambda j: (0, j)),
                  pl.BlockSpec((1, tn), lambda j: (0, j))],
        out_specs=pl.BlockSpec((B, tn), lambda j: (0, j)),
        out_shape=jax.ShapeDtypeStruct((B, n_out), F32),
        name="ada_mod",
    )(c, w_ada, b_ada.reshape(1, n_out))


def _pack_bf16_pair(lo, hi):
    lo_bits = pltpu.bitcast(lo.astype(BF16).astype(F32), jnp.uint32) >> 16
    hi_bits = pltpu.bitcast(hi.astype(BF16).astype(F32), jnp.uint32) & jnp.uint32(0xFFFF0000)
    return lo_bits | hi_bits


def _unpack_bf16_pair(u):
    lo = pltpu.bitcast(u << 16, F32).astype(BF16)
    hi = pltpu.bitcast(u & jnp.uint32(0xFFFF0000), F32).astype(BF16)
    return lo, hi


def _inproj_kernel(x_ref, g_ref, sc_ref, sh_ref, w_ref, gain_ref, bd_ref, p_ref, f_ref, s_ref):
    x = x_ref[...]
    ms = jnp.mean(x * x, axis=-1, keepdims=True)
    h = x * lax.rsqrt(ms + EPS) * g_ref[...]
    h = h * (1.0 + sc_ref[0]) + sh_ref[0]
    hb = h.astype(BF16)

    def unit(u):
        cols = slice(u * UNIT, (u + 1) * UNIT)
        acc = _dot(hb, w_ref[:, cols])
        kind = _KIND[u]
        if kind == "gate":
            return jax.nn.sigmoid(acc)
        if kind in ("normq", "normk"):
            ss = _dot((acc * acc).astype(BF16), bd_ref[...])
            return acc * lax.rsqrt(ss * (1.0 / HEAD_DIM) + EPS) * gain_ref[:, cols]
        return acc

    for u in range(U_DIL):
        p_ref[:, u * UNIT:(u + 1) * UNIT] = unit(u).astype(BF16)
    for g in range(N_DIL_GROUPS):
        q, k, v = (unit(U_DIL + 3 * g + j) for j in range(3))
        s_ref[0, 3 * g] = _pack_bf16_pair(q[:, :LANES], k[:, :LANES])
        s_ref[0, 3 * g + 1] = _pack_bf16_pair(q[:, LANES:], k[:, LANES:])
        s_ref[0, 3 * g + 2] = _pack_bf16_pair(v[:, :LANES], v[:, LANES:])
    f_ref[...] = unit(U_FORGET)[:, :LANES]


def _inproj(x2d, norm_g, sc, sh, w_re, gain_row, S):
    T = x2d.shape[0]
    tm = TM_INPROJ
    per_b = S // tm
    bd = np.kron(np.eye(UNIT // HEAD_DIM), np.ones((HEAD_DIM, HEAD_DIM))).astype(np.float32)
    once = dict(pipeline_mode=pl.Buffered(1))
    return pl.pallas_call(
        _inproj_kernel,
        grid=(T // tm,),
        in_specs=[pl.BlockSpec((tm, D_MODEL), lambda i: (i, 0)),
                  pl.BlockSpec((1, D_MODEL), lambda i: (0, 0)),
                  pl.BlockSpec((1, 1, D_MODEL), lambda i: (i // per_b, 0, 0)),
                  pl.BlockSpec((1, 1, D_MODEL), lambda i: (i // per_b, 0, 0)),
                  pl.BlockSpec((D_MODEL, N_UNITS * UNIT), lambda i: (0, 0), **once),
                  pl.BlockSpec((1, N_UNITS * UNIT), lambda i: (0, 0), **once),
                  pl.BlockSpec((UNIT, UNIT), lambda i: (0, 0), **once)],
        out_specs=[pl.BlockSpec((tm, P_WIDTH), lambda i: (i, 0)),
                   pl.BlockSpec((tm, LANES), lambda i: (i, 0)),
                   pl.BlockSpec((1, N_SLABS, tm, LANES), lambda i: (i // per_b, 0, i % per_b, 0))],
        out_shape=[jax.ShapeDtypeStruct((T, P_WIDTH), BF16),
                   jax.ShapeDtypeStruct((T, LANES), F32),
                   jax.ShapeDtypeStruct((T // S, N_SLABS, S, LANES), jnp.uint32)],
        compiler_params=pltpu.CompilerParams(vmem_limit_bytes=VMEM_LIMIT),
        name="in_proj",
    )(x2d, norm_g, sc, sh, w_re, gain_row, jnp.asarray(bd, BF16))


def _fcum_kernel(f_ref, b_ref, tri_ref, o_ref):
    S = f_ref.shape[1]
    xf = f_ref[0] + b_ref[...]
    ls = (jnp.minimum(xf, 0.0) - jnp.log(1.0 + jnp.exp(-jnp.abs(xf)))) * LOG2E
    lst = ls.T
    carry = jnp.zeros((LANES, UNIT), F32)
    for blk in range(S // UNIT):
        seg = lst[:, blk * UNIT:(blk + 1) * UNIT]
        hi, mid, lo = _split3(seg)
        tri = tri_ref[...]
        res = _dot(hi, tri) + _dot(mid, tri) + _dot(lo, tri)
        o_ref[0, :, blk * UNIT:(blk + 1) * UNIT] = (res[:, :UNIT] + carry)[:FOX_HEADS]
        carry = carry + res[:, UNIT:]


def _fcum(fgt, b_forget):
    B, S, _ = fgt.shape
    brow = jnp.zeros((1, LANES), F32).at[0, :FOX_HEADS].set(b_forget)
    tri = np.concatenate([np.triu(np.ones((UNIT, UNIT))), np.ones((UNIT, UNIT))], axis=1)
    return pl.pallas_call(
        _fcum_kernel,
        grid=(B,),
        in_specs=[pl.BlockSpec((1, S, LANES), lambda b: (b, 0, 0)),
                  pl.BlockSpec((1, LANES), lambda b: (0, 0)),
                  pl.BlockSpec((UNIT, 2 * UNIT), lambda b: (0, 0))],
        out_specs=pl.BlockSpec((1, FOX_HEADS, S), lambda b: (b, 0, 0)),
        out_shape=jax.ShapeDtypeStruct((B, FOX_HEADS, S), F32),
        name="forget_cumsum",
    )(fgt, brow, jnp.asarray(tri, BF16))


def _fox_kernel(q_ref, k_ref, v_ref, ck_ref, o_ref):
    S = q_ref.shape[1]
    pair = pl.program_id(1)
    tq = TQ_FOX
    lane = lax.broadcasted_iota(jnp.int32, (1, LANES), 1)
    row = lax.broadcasted_iota(jnp.int32, (tq, tq), 0)
    col = lax.broadcasted_iota(jnp.int32, (tq, tq), 1)
    causal = col <= row
    cks = [ck_ref[0, pl.ds(2 * pair + hh, 1), :] for hh in range(2)]
    for t in range(S // tq):
        r0, r1 = t * tq, (t + 1) * tq
        qt = q_ref[0, r0:r1, :]
        outs = []
        for hh in range(2):
            hsel = (lane >= HEAD_DIM) == bool(hh)
            qm = jnp.where(hsel, qt, jnp.zeros_like(qt))
            ck = cks[hh]
            s = _dot_nt(qm, k_ref[0, :r1, :]) - ck[:, :r1]
            s_d = jnp.where(causal, s[:, r0:], -jnp.inf)
            s = jnp.concatenate([s[:, :r0], s_d], axis=1) if t > 0 else s_d
            m = jnp.max(s, axis=-1, keepdims=True)
            p = jnp.exp2(s - m)
            l = jnp.sum(p, axis=-1, keepdims=True)
            outs.append(_dot(p.astype(BF16), v_ref[0, :r1, :]) / l)
        o_ref[0, r0:r1, :] = jnp.where(lane < HEAD_DIM, outs[0], outs[1]).astype(BF16)


def _fox(p3, ck):
    B, S, _ = p3.shape
    nq, nk, nv = (U_FOX_Q * UNIT // LANES, U_FOX_K * UNIT // LANES, U_FOX_V * UNIT // LANES)
    return pl.pallas_call(
        _fox_kernel,
        grid=(B, FOX_HEADS // 2),
        in_specs=[pl.BlockSpec((1, S, LANES), lambda b, p: (b, 0, nq + p)),
                  pl.BlockSpec((1, S, LANES), lambda b, p: (b, 0, nk + p)),
                  pl.BlockSpec((1, S, LANES), lambda b, p: (b, 0, nv + p)),
                  pl.BlockSpec((1, FOX_HEADS, S), lambda b, p: (b, 0, 0))],
        out_specs=pl.BlockSpec((1, S, LANES), lambda b, p: (b, 0, p)),
        out_shape=jax.ShapeDtypeStruct((B, S, FOX_WIDTH), BF16),
        compiler_params=pltpu.CompilerParams(vmem_limit_bytes=VMEM_LIMIT),
        name="fox_attn",
    )(p3, p3, p3, ck)


def _t5_bucket(dist):
    max_exact = NUM_BUCKETS // 2
    d = np.maximum(dist, 1).astype(np.float32)
    large = max_exact + (np.log(d / max_exact) / np.log(REL_MAX_DISTANCE / max_exact)
                         * (NUM_BUCKETS - max_exact)).astype(np.int32)
    large = np.minimum(large, NUM_BUCKETS - 1)
    return np.where(dist < max_exact, dist, large).astype(np.int32)


def _relbias_kernel(tab_ref, bucket_ref, valid_ref, o_ref):
    g = pl.program_id(0)
    bk = bucket_ref[0]
    vd = valid_ref[0]
    for hs in range(DIL_HEADS_PER_GROUP):
        acc = jnp.zeros(bk.shape, F32)
        for b in range(NUM_BUCKETS):
            acc = jnp.where(bk == b, tab_ref[b, g * DIL_HEADS_PER_GROUP + hs], acc)
        bias = jnp.where(vd != 0, acc * LOG2E, -jnp.inf)
        o_ref[0, hs] = bias
        col = lax.broadcasted_iota(jnp.int32, bias.shape, 1)
        o_ref[1, hs] = jnp.where(col >= DIL_L, bias, -jnp.inf)


def _relbias(table):
    L = DIL_L
    i = np.arange(L)[:, None]
    j = np.arange(2 * L)[None, :]
    m = L + i - j
    valid = ((m >= 0) & (m <= L)).astype(np.int32)
    buckets = np.stack([_t5_bucket(np.clip(m, 0, None) * d) for _, d in DIL_GROUPS])
    valids = np.stack([valid] * N_DIL_GROUPS)
    return pl.pallas_call(
        _relbias_kernel,
        grid=(N_DIL_GROUPS,),
        in_specs=[pl.BlockSpec(memory_space=pltpu.SMEM),
                  pl.BlockSpec((1, L, 2 * L), lambda g: (g, 0, 0)),
                  pl.BlockSpec((1, L, 2 * L), lambda g: (g, 0, 0))],
        out_specs=pl.BlockSpec((2, DIL_HEADS_PER_GROUP, L, 2 * L), lambda g: (0, g, 0, 0)),
        out_shape=jax.ShapeDtypeStruct((2, DIL_HEADS, L, 2 * L), F32),
        name="rel_bias",
    )(table, jnp.asarray(buckets), jnp.asarray(valids))


def _dil_rows(start, d):
    return pl.ds(start, DIL_L) if d == 1 else pl.ds(start, DIL_L, stride=d)


def _dil_block(qkv_ref, bias_ref, m_scr, l_scr, acc_scr, g, d, r, n):
    L = DIL_L
    lane = lax.broadcasted_iota(jnp.int32, (1, LANES), 1)
    first = 1 - jnp.minimum(n, 1)
    cur = _dil_rows(r + d * (n * L), d)
    prev = _dil_rows(r + d * (jnp.maximum(n - 1, 0) * L), d)
    v_cur = _unpack_bf16_pair(qkv_ref[0, 3 * g + 2, cur, :])
    v_prev = _unpack_bf16_pair(qkv_ref[0, 3 * g + 2, prev, :])
    for pr in range(2):
        qt, k_cur = _unpack_bf16_pair(qkv_ref[0, 3 * g + pr, cur, :])
        _, k_prev = _unpack_bf16_pair(qkv_ref[0, 3 * g + pr, prev, :])
        kt = jnp.concatenate([k_prev, k_cur], axis=0)
        vt = jnp.concatenate([v_prev[pr], v_cur[pr]], axis=0)
        ms, ls, accs = [], [], []
        for hh in range(2):
            hsel = (lane >= HEAD_DIM) == bool(hh)
            qm = jnp.where(hsel, qt, jnp.zeros_like(qt))
            s = _dot_nt(qm, kt) + bias_ref[first, DIL_HEADS_PER_GROUP * g + 2 * pr + hh]
            m = jnp.max(s, axis=-1, keepdims=True)
            p = jnp.exp2(s - m)
            ms.append(m)
            ls.append(jnp.sum(p, axis=-1, keepdims=True))
            accs.append(_dot(p.astype(BF16), vt))
        low = lane < HEAD_DIM
        m_b = jnp.where(low, ms[0], ms[1])
        l_b = jnp.where(low, ls[0], ls[1])
        acc_b = jnp.where(low, accs[0], accs[1])
        if g == 0:
            m_scr[pr, cur, :] = m_b
            l_scr[pr, cur, :] = l_b
            acc_scr[pr, cur, :] = acc_b
        else:
            m_o = m_scr[pr, cur, :]
            m_n = jnp.maximum(m_o, m_b)
            a_o = jnp.exp2(m_o - m_n)
            a_b = jnp.exp2(m_b - m_n)
            m_scr[pr, cur, :] = m_n
            l_scr[pr, cur, :] = l_scr[pr, cur, :] * a_o + l_b * a_b
            acc_scr[pr, cur, :] = acc_scr[pr, cur, :] * a_o + acc_b * a_b


def _dil_kernel(qkv_ref, bias_ref, o_ref, m_scr, l_scr, acc_scr):
    S = o_ref.shape[1]
    for g, (window, d) in enumerate(DIL_GROUPS):
        nb = S // window

        def body(it, carry, g=g, d=d, nb=nb):
            _dil_block(qkv_ref, bias_ref, m_scr, l_scr, acc_scr, g, d, it // nb, it % nb)
            return carry
        lax.fori_loop(0, d * nb, body, 0, unroll=4)
    for pr in range(2):
        o_ref[0, :, pr * LANES:(pr + 1) * LANES] = (acc_scr[pr] / l_scr[pr]).astype(BF16)


def _dil(slabs, bias):
    B, _, S, _ = slabs.shape
    for window, d in DIL_GROUPS:
        assert window // d == DIL_L and S % window == 0
    stat = pltpu.VMEM((2, S, LANES), F32)
    return pl.pallas_call(
        _dil_kernel,
        grid=(B,),
        in_specs=[pl.BlockSpec((1, N_SLABS, S, LANES), lambda b: (b, 0, 0, 0)),
                  pl.BlockSpec(bias.shape, lambda b: (0, 0, 0, 0))],
        out_specs=pl.BlockSpec((1, S, DIL_OUT_WIDTH), lambda b: (b, 0, 0)),
        out_shape=jax.ShapeDtypeStruct((B, S, DIL_OUT_WIDTH), BF16),
        scratch_shapes=[stat, stat, stat],
        compiler_params=pltpu.CompilerParams(vmem_limit_bytes=VMEM_LIMIT),
        name="dil_attn",
    )(slabs, bias)


def _outproj_kernel(x_ref, ya_ref, yd_ref, ga_ref, gb_ref,
                    g1_ref, sc_ref, sh_ref, ng_ref, wa_ref, wb_ref, wo_ref, wr_ref, br_ref,
                    x1_ref, h2_ref, lg_ref):
    a = _dot(ya_ref[...], wa_ref[...])
    bm = _dot(yd_ref[...], wb_ref[...])
    merged = ga_ref[...].astype(F32) * a + gb_ref[...].astype(F32) * bm
    out = _dot(merged.astype(BF16), wo_ref[...])
    x1 = x_ref[...] + g1_ref[0] * out
    x1_ref[...] = x1
    ms = jnp.mean(x1 * x1, axis=-1, keepdims=True)
    h = x1 * lax.rsqrt(ms + EPS) * ng_ref[...]
    h = h * (1.0 + sc_ref[0]) + sh_ref[0]
    hb = h.astype(BF16)
    h2_ref[...] = hb
    lg_ref[...] = _dot(hb, wr_ref[...]) + br_ref[...]


def _outproj(x2d, ya2d, yd2d, p2d, g1, sc2, sh2, norm_g, wa, wb, wo, wr, br, S):
    T = x2d.shape[0]
    tm = TM_PROJ
    per_b = S // tm
    row = lambda w: pl.BlockSpec((tm, w), lambda i: (i, 0))
    full = lambda a: pl.BlockSpec(a.shape, lambda i: (0,) * a.ndim)
    mod = pl.BlockSpec((1, 1, D_MODEL), lambda i: (i // per_b, 0, 0))
    return pl.pallas_call(
        _outproj_kernel,
        grid=(T // tm,),
        in_specs=[row(D_MODEL), row(FOX_WIDTH), row(DIL_OUT_WIDTH)]
                 + [pl.BlockSpec((tm, D_MODEL), lambda i: (i, U_GATE_A * UNIT // D_MODEL)),
                    pl.BlockSpec((tm, D_MODEL), lambda i: (i, U_GATE_B * UNIT // D_MODEL)),
                    mod, mod, mod, full(norm_g), full(wa), full(wb), full(wo), full(wr), full(br)],
        out_specs=[row(D_MODEL), row(D_MODEL), row(LANES)],
        out_shape=[jax.ShapeDtypeStruct((T, D_MODEL), F32),
                   jax.ShapeDtypeStruct((T, D_MODEL), BF16),
                   jax.ShapeDtypeStruct((T, LANES), F32)],
        compiler_params=pltpu.CompilerParams(vmem_limit_bytes=VMEM_LIMIT),
        name="out_proj",
    )(x2d, ya2d, yd2d, p2d, p2d, g1, sc2, sh2, norm_g, wa, wb, wo, wr, br)


def _dispatch_kernel(lg_ref, h_ref, tri_ref, xs_ref, cnt_ref, pos_ref):
    tt = lg_ref.shape[0]
    lt = lg_ref[...].T
    row = lambda i: lt[i:i + 1, :]
    neg = -jnp.inf
    g = [row(i) for i in range(N_GROUPS)]
    gmax = functools.reduce(jnp.maximum, g)
    gidx = jnp.full(gmax.shape, N_GROUPS - 1, jnp.int32)
    for i in reversed(range(N_GROUPS - 1)):
        gidx = jnp.where(g[i] == gmax, i, gidx)
    gsum = sum(jnp.exp(gi - gmax) for gi in g)
    el = []
    for j in range(EXPERTS_PER_GROUP):
        v = row(N_GROUPS + EXPERTS_PER_GROUP * (N_GROUPS - 1) + j)
        for gg in reversed(range(N_GROUPS - 1)):
            v = jnp.where(gidx == gg, row(N_GROUPS + EXPERTS_PER_GROUP * gg + j), v)
        el.append(v)

    def top(vals):
        best = functools.reduce(jnp.maximum, vals)
        idx = jnp.full(best.shape, EXPERTS_PER_GROUP - 1, jnp.int32)
        for j in reversed(range(EXPERTS_PER_GROUP - 1)):
            idx = jnp.where(vals[j] == best, j, idx)
        return best, idx

    v1, i1 = top(el)
    v2, i2 = top([jnp.where(i1 == j, neg, el[j]) for j in range(EXPERTS_PER_GROUP)])
    t = jnp.exp(v2 - v1)
    den = (1.0 + t) * gsum
    wts = [1.0 / den, t / den]
    eid = [gidx * EXPERTS_PER_GROUP + i1, gidx * EXPERTS_PER_GROUP + i2]

    esub = lax.broadcasted_iota(jnp.int32, (N_EXPERTS, tt), 0)
    ohf = jnp.concatenate([jnp.where(esub == eid[k], 1.0, 0.0) for k in range(2)], axis=1)
    res = _dot(ohf.astype(BF16), tri_ref[...])
    prefix, cnt = res[:, :2 * tt], res[:, 2 * tt:]
    cnt = (((cnt.astype(jnp.int32) + (SUBLANES - 1)) // SUBLANES) * SUBLANES).astype(F32)
    esub_c = lax.broadcasted_iota(jnp.int32, cnt.shape, 0)
    start = jnp.zeros_like(cnt)
    for e in range(N_EXPERTS - 1):
        start = start + jnp.where(esub_c > e, cnt[e:e + 1, :], 0.0)
    start_w = jnp.concatenate([start] * (2 * tt // LANES), axis=1)
    pos = jnp.sum(ohf * (start_w + prefix), axis=0, keepdims=True)
    pos_k = [pos[:, :tt], pos[:, tt:]]

    n_rows = xs_ref.shape[0]
    psub = lax.broadcasted_iota(jnp.int32, (n_rows, tt), 0).astype(F32)
    pm = [jnp.where(psub == pos_k[k], 1.0, 0.0).astype(BF16) for k in range(2)]
    xs = _dot(pm[0] + pm[1], h_ref[...])
    wsub = lax.broadcasted_iota(jnp.int32, (LANES, tt), 0)
    ws = jnp.zeros((n_rows, LANES), F32)
    for k in range(2):
        parts = _split3(wts[k])
        wrows = jnp.zeros((LANES, tt), F32)
        for j in range(3):
            wrows = jnp.where(wsub == j, parts[j].astype(F32), wrows)
        ws = ws + _dot_nt(pm[k], wrows.astype(BF16))
    half = D_MODEL // 2
    xs_ref[:, :half] = _pack_bf16_pair(xs[:, :half], xs[:, half:])
    xs_ref[:, half:] = pltpu.bitcast(ws, jnp.uint32)
    cnt_ref[0] = cnt.astype(jnp.int32)
    posr = jnp.where(wsub == 0, pos_k[0], jnp.where(wsub == 1, pos_k[1], 0.0))
    pos_ref[...] = posr.T


def _dispatch(logits, h2):
    T = logits.shape[0]
    tt = MOE_TILE
    n_tiles = T // tt
    tri = np.concatenate([np.triu(np.ones((2 * tt, 2 * tt)), 1), np.ones((2 * tt, LANES))], axis=1)
    return pl.pallas_call(
        _dispatch_kernel,
        grid=(n_tiles,),
        in_specs=[pl.BlockSpec((tt, LANES), lambda i: (i, 0)),
                  pl.BlockSpec((tt, D_MODEL), lambda i: (i, 0)),
                  pl.BlockSpec(tri.shape, lambda i: (0, 0))],
        out_specs=[pl.BlockSpec((TILE_ROWS, XS_WIDTH), lambda i: (i, 0)),
                   pl.BlockSpec((1, N_EXPERTS, LANES), lambda i: (i, 0, 0)),
                   pl.BlockSpec((tt, LANES), lambda i: (i, 0))],
        out_shape=[jax.ShapeDtypeStruct((n_tiles * TILE_ROWS, XS_WIDTH), jnp.uint32),
                   jax.ShapeDtypeStruct((n_tiles, N_EXPERTS, LANES), jnp.int32),
                   jax.ShapeDtypeStruct((T, LANES), F32)],
        compiler_params=pltpu.CompilerParams(vmem_limit_bytes=VMEM_LIMIT),
        name="moe_dispatch",
    )(logits, h2, jnp.asarray(tri, BF16))


def _plan_kernel(cnt_ref, be_ref, nv_ref, nxt_ref, grp_ref, used_ref, cs_ref):
    n_tiles = cnt_ref.shape[0]
    n_blk = be_ref.shape[0]
    rows = MOE_ROWS
    row_shift = rows.bit_length() - 1
    grp_shift = SUBLANES.bit_length() - 1
    assert rows == 1 << row_shift and SUBLANES == 1 << grp_shift

    def tile_starts(t, c):
        def per_e(e, acc):
            cs_ref[t * N_EXPERTS + e] = acc
            return acc + cnt_ref[t, e]
        used_ref[t] = lax.fori_loop(0, N_EXPERTS, per_e, 0, unroll=8)
        return c
    lax.fori_loop(0, n_tiles, tile_starts, 0)

    def clear(b, c):
        nv_ref[b] = 0
        return c
    lax.fori_loop(0, n_blk, clear, 0)

    def clear_groups(g, c):
        grp_ref[g] = 0
        return c
    lax.fori_loop(0, n_blk * MOE_GROUPS, clear_groups, 0, unroll=8)

    def per_expert(e, b):
        g0 = b * MOE_GROUPS

        def per_tile(t, tot):
            c = cnt_ref[t, e]
            src = t * TILE_ROWS + cs_ref[t * N_EXPERTS + e]
            first = g0 + lax.shift_right_logical(tot, grp_shift)

            def per_group(k, cc):
                grp_ref[first + k] = src + k * SUBLANES
                return cc
            lax.fori_loop(0, lax.shift_right_logical(c, grp_shift), per_group, 0)
            return tot + c
        tot = lax.fori_loop(0, n_tiles, per_tile, 0)

        def per_block(j, c):
            be_ref[b + j] = e
            nv_ref[b + j] = jnp.minimum(rows, tot - j * rows)
            return c
        nb = lax.shift_right_logical(tot + rows - 1, row_shift)
        lax.fori_loop(0, nb, per_block, 0)
        return b + nb
    n_used = lax.fori_loop(0, N_EXPERTS, per_expert, 0)

    def unused(b, c):
        be_ref[b] = be_ref[n_used - 1]
        nxt_ref[b] = -1
        return c
    lax.fori_loop(n_used, n_blk, unused, 0)

    def next_run(k, nf):
        b = n_used - 1 - k
        nf = jnp.where(be_ref[b] != be_ref[jnp.minimum(b + 1, n_used - 1)], b + 1, nf)
        nxt_ref[b] = nf
        return nf
    lax.fori_loop(0, n_used, next_run, -1)


def _plan(cnt, n_blk):
    n_tiles = cnt.shape[0]
    smem = pl.BlockSpec(memory_space=pltpu.SMEM)
    i32 = lambda n: jax.ShapeDtypeStruct((n,), jnp.int32)
    return pl.pallas_call(
        _plan_kernel,
        in_specs=[smem],
        out_specs=[smem] * 5,
        out_shape=[i32(n_blk), i32(n_blk), i32(n_blk), i32(n_blk * MOE_GROUPS), i32(n_tiles)],
        scratch_shapes=[pltpu.SMEM((n_tiles * N_EXPERTS,), jnp.int32)],
        name="moe_plan",
    )(cnt)


def _pow2_pieces(n, fn):
    for b in reversed(range(SUBLANES.bit_length() - 1, MOE_ROWS.bit_length())):
        size = 1 << b

        @pl.when((n & size) != 0)
        def _():
            fn((n >> (b + 1)) << (b + 1), size)


def _moe_kernel(be_ref, nv_ref, nxt_ref, grp_ref, used_ref,
                w1_hbm, w3_hbm, w2_hbm, xs_hbm, ys_hbm,
                xbuf, ybuf, wb1, wb3, wb2, wst1, wst3, wst2, wslot, gsem, ssem, wsem):
    i = pl.program_id(0)
    last = pl.num_programs(0) - 1
    slot = i % 2
    nv = nv_ref[i]
    half = D_MODEL // 2
    grp_shift = SUBLANES.bit_length() - 1

    def group_row(blk, g):
        return pl.multiple_of(grp_ref[blk * MOE_GROUPS + g], SUBLANES)

    def gather(blk, s):
        top = jnp.maximum(lax.shift_right_logical(nv_ref[blk], grp_shift) - 1, 0)
        for g in range(MOE_GROUPS):
            src = group_row(blk, jnp.minimum(g, top))
            pltpu.make_async_copy(xs_hbm.at[pl.ds(src, SUBLANES)],
                                  xbuf.at[s, pl.ds(g * SUBLANES, SUBLANES)], gsem.at[s]).start()

    def wait_gather(s):
        pltpu.make_async_copy(xs_hbm.at[pl.ds(0, MOE_ROWS)], xbuf.at[s], gsem.at[s]).wait()

    def scatter(blk, s):
        def body(g, c):
            r = pl.multiple_of(g * SUBLANES, SUBLANES)
            pltpu.make_async_copy(ybuf.at[s, pl.ds(r, SUBLANES)],
                                  ys_hbm.at[pl.ds(group_row(blk, g), SUBLANES)], ssem.at[s]).start()
            return c
        lax.fori_loop(0, lax.shift_right_logical(nv_ref[blk], grp_shift), body, 0)

    def wait_scatter(s, count):
        _pow2_pieces(count, lambda a, size: pltpu.make_async_copy(
            ybuf.at[s, pl.ds(0, size)], ys_hbm.at[pl.ds(0, size)], ssem.at[s]).wait())

    @pl.when(i == 0)
    def _():
        @pl.when(nv > 0)
        def _():
            gather(0, 0)
            gather(jnp.minimum(1, last), 1)

        ybuf[1] = jnp.zeros(ybuf.shape[1:], ybuf.dtype)
        n_tiles = used_ref.shape[0]

        def fill(t, c):
            row0 = t * TILE_ROWS + used_ref[t]
            _pow2_pieces(TILE_ROWS - used_ref[t], lambda a, size: pltpu.make_async_copy(
                ybuf.at[1, pl.ds(0, size)], ys_hbm.at[pl.ds(pl.multiple_of(row0 + a, SUBLANES), size)],
                ssem.at[1]).start())
            return c
        lax.fori_loop(0, n_tiles, fill, 0)

        def drain(t, c):
            wait_scatter(1, TILE_ROWS - used_ref[t])
            return c
        lax.fori_loop(0, n_tiles, drain, 0)

    @pl.when(i >= 2)
    def _():
        wait_scatter(slot, nv_ref[jnp.maximum(i - 2, 0)])

    xslot = i % GATHER_BUFS
    issuer_used = jnp.where(i >= 2, nv_ref[jnp.maximum(i - 2, 0)], nv_ref[0]) > 0

    @pl.when((nv == 0) & (i > 0) & issuer_used)
    def _():
        wait_gather(xslot)

    @pl.when(nv > 0)
    def _():
        e = be_ref[i]
        e_prev = be_ref[jnp.maximum(i - 1, 0)]

        def weight_copies(ex, ws):
            return [pltpu.make_async_copy(src.at[ex], dst.at[ws], wsem.at[ws])
                    for src, dst in ((w1_hbm, wst1), (w3_hbm, wst3), (w2_hbm, wst2))]

        @pl.when(i == 0)
        def _():
            wslot[0] = 0
            for cp in weight_copies(e, 0):
                cp.start()

        @pl.when((i == 0) | (e != e_prev))
        def _():
            ws = wslot[0]
            for cp in weight_copies(e, ws):
                cp.wait()
            wb1[...] = wst1[ws].astype(BF16)
            wb3[...] = wst3[ws].astype(BF16)
            wb2[...] = wst2[ws].astype(BF16)
            nb = nxt_ref[i]

            @pl.when(nb >= 0)
            def _():
                for cp in weight_copies(be_ref[jnp.maximum(nb, 0)], 1 - ws):
                    cp.start()
            wslot[0] = 1 - ws

        wait_gather(xslot)
        gather(jnp.minimum(i + 2, last), (i + 2) % GATHER_BUFS)
        u = xbuf[xslot]
        xa, xb = _unpack_bf16_pair(u[:, :half])
        wv = pltpu.bitcast(u[:, half:], F32)
        roww = wv[:, 0:1] + wv[:, 1:2] + wv[:, 2:3]
        a = _dot(xa, wb1[:half, :]) + _dot(xb, wb1[half:, :])
        b = _dot(xa, wb3[:half, :]) + _dot(xb, wb3[half:, :])
        hmid = (a * jax.nn.sigmoid(a) * b).astype(BF16)
        y = _dot(hmid, wb2[...]) * roww
        ybuf[slot] = _pack_bf16_pair(y[:, :half], y[:, half:])
        scatter(i, slot)

    @pl.when(i == last)
    def _():
        @pl.when((last >= 1) & (nv_ref[jnp.maximum(last - 1, 0)] > 0))
        def _():
            wait_gather((last + 1) % GATHER_BUFS)

        @pl.when(nv > 0)
        def _():
            wait_gather((last + 2) % GATHER_BUFS)

        @pl.when(last >= 1)
        def _():
            wait_scatter(1 - slot, nv_ref[jnp.maximum(last - 1, 0)])
        wait_scatter(slot, nv)


def _moe(xs, plan, w1, w3, w2):
    n_blk = plan[0].shape[0]
    rows = MOE_ROWS
    half = D_MODEL // 2
    hbm = pl.BlockSpec(memory_space=pl.ANY)
    grid_spec = pltpu.PrefetchScalarGridSpec(
        num_scalar_prefetch=5,
        grid=(n_blk,),
        in_specs=[hbm] * 4,
        out_specs=hbm,
        scratch_shapes=[pltpu.VMEM((GATHER_BUFS, rows, XS_WIDTH), jnp.uint32),
                        pltpu.VMEM((2, rows, half), jnp.uint32),
                        pltpu.VMEM((D_MODEL, EXPERT_HIDDEN), BF16),
                        pltpu.VMEM((D_MODEL, EXPERT_HIDDEN), BF16),
                        pltpu.VMEM((EXPERT_HIDDEN, D_MODEL), BF16),
                        pltpu.VMEM((2, D_MODEL, EXPERT_HIDDEN), F32),
                        pltpu.VMEM((2, D_MODEL, EXPERT_HIDDEN), F32),
                        pltpu.VMEM((2, EXPERT_HIDDEN, D_MODEL), F32),
                        pltpu.SMEM((1,), jnp.int32),
                        pltpu.SemaphoreType.DMA((GATHER_BUFS,)),
                        pltpu.SemaphoreType.DMA((2,)),
                        pltpu.SemaphoreType.DMA((2,))])
    return pl.pallas_call(
        _moe_kernel,
        grid_spec=grid_spec,
        out_shape=jax.ShapeDtypeStruct((xs.shape[0], half), jnp.uint32),
        compiler_params=pltpu.CompilerParams(dimension_semantics=("arbitrary",),
                                             vmem_limit_bytes=VMEM_LIMIT),
        name="moe_ffn",
    )(*plan, w1, w3, w2, xs)


def _combine_kernel(x1_ref, ys_ref, pos_ref, g2_ref, o_ref):
    tt = x1_ref.shape[0]
    half = D_MODEL // 2
    pos = pos_ref[...]
    pcol = lax.broadcasted_iota(jnp.int32, (tt, ys_ref.shape[0]), 1).astype(F32)
    sel = (jnp.where(pcol == pos[:, 0:1], 1.0, 0.0) + jnp.where(pcol == pos[:, 1:2], 1.0, 0.0))
    sel = sel.astype(BF16)
    lo, hi = _unpack_bf16_pair(ys_ref[...])
    g2 = g2_ref[0]
    x1 = x1_ref[...]
    o_ref[:, :half] = x1[:, :half] + g2[:, :half] * _dot(sel, lo)
    o_ref[:, half:] = x1[:, half:] + g2[:, half:] * _dot(sel, hi)


def _combine(x1, ys, pos, g2, S):
    T = x1.shape[0]
    tt = MOE_TILE
    per_b = S // tt
    return pl.pallas_call(
        _combine_kernel,
        grid=(T // tt,),
        in_specs=[pl.BlockSpec((tt, D_MODEL), lambda i: (i, 0)),
                  pl.BlockSpec((TILE_ROWS, D_MODEL // 2), lambda i: (i, 0)),
                  pl.BlockSpec((tt, LANES), lambda i: (i, 0)),
                  pl.BlockSpec((1, 1, D_MODEL), lambda i: (i // per_b, 0, 0))],
        out_specs=pl.BlockSpec((tt, D_MODEL), lambda i: (i, 0)),
        out_shape=jax.ShapeDtypeStruct((T, D_MODEL), F32),
        compiler_params=pltpu.CompilerParams(vmem_limit_bytes=VMEM_LIMIT),
        name="moe_combine",
    )(x1, ys, pos, g2)


def _prep_w_in(w_in):
    dq = w_in[:, OFF_DIL_Q:OFF_DIL_K]
    dk = w_in[:, OFF_DIL_K:OFF_DIL_V]
    dv = w_in[:, OFF_DIL_V:OFF_GATE_A]
    dil = []
    for g in range(N_DIL_GROUPS):
        cs = slice(g * DIL_OUT_WIDTH, (g + 1) * DIL_OUT_WIDTH)
        dil += [dq[:, cs], dk[:, cs], dv[:, cs]]
    pad = jnp.zeros((D_MODEL, UNIT - FOX_HEADS), w_in.dtype)
    cols = [w_in[:, OFF_GATE_A:OFF_GATE_B], w_in[:, OFF_GATE_B:N_IN],
            w_in[:, OFF_FOX_Q:OFF_FOX_K], w_in[:, OFF_FOX_K:OFF_FOX_V], w_in[:, OFF_FOX_V:OFF_FOX_F],
            *dil, w_in[:, OFF_FOX_F:OFF_DIL_Q], pad]
    return jnp.concatenate(cols, axis=1).astype(BF16)


def _prep_gain(q_gain, k_gain):
    qs = HEAD_DIM ** -0.5 * LOG2E
    ones = jnp.ones((UNIT,), F32)
    fq = q_gain[:FOX_HEADS].reshape(-1) * qs
    fk = k_gain[:FOX_HEADS].reshape(-1)
    dq = q_gain[FOX_HEADS:].reshape(-1) * qs
    dk = k_gain[FOX_HEADS:].reshape(-1)
    dil = []
    for g in range(N_DIL_GROUPS):
        cs = slice(g * DIL_OUT_WIDTH, (g + 1) * DIL_OUT_WIDTH)
        dil += [dq[cs], dk[cs], ones]
    parts = [ones] * 8 + [fq, fk, ones, ones] + dil + [ones]
    return jnp.concatenate(parts).reshape(1, N_UNITS * UNIT)


def _layer(x, mod, rel_bias_table, norm1_g, w_in, b_forget, q_gain, k_gain, w_branch_a, w_branch_b,
           w_out, norm2_g, w_rg, b_rg, w_re, b_re, w1, w3, w2):
    B, S, D = x.shape
    T = B * S
    sh1, sc1, g1, sh2, sc2, g2 = [m.reshape(B, 1, D) for m in jnp.split(mod, 6, axis=-1)]
    x2d = x.reshape(T, D)

    p2d, fgt, slabs = _inproj(x2d, norm1_g.reshape(1, D), sc1, sh1, _prep_w_in(w_in),
                              _prep_gain(q_gain, k_gain), S)
    p3 = p2d.reshape(B, S, P_WIDTH)
    ck = _fcum(fgt.reshape(B, S, LANES), b_forget)
    ya = _fox(p3, ck)
    yd = _dil(slabs, _relbias(rel_bias_table))

    n_router = N_GROUPS + N_EXPERTS
    wr = jnp.concatenate([w_rg, w_re, jnp.zeros((D, LANES - n_router), F32)], axis=1).astype(BF16)
    br = jnp.concatenate([b_rg, b_re, jnp.zeros((LANES - n_router,), F32)]).reshape(1, LANES)
    x1, h2, logits = _outproj(x2d, ya.reshape(T, FOX_WIDTH), yd.reshape(T, DIL_OUT_WIDTH), p2d,
                              g1, sc2, sh2, norm2_g.reshape(1, D),
                              w_branch_a.astype(BF16), w_branch_b.astype(BF16), w_out.astype(BF16),
                              wr, br, S)
    xs, cnt, pos = _dispatch(logits, h2)
    cnt2 = cnt[:, :, 0]
    n_blk = cnt.shape[0] * TILE_ROWS // MOE_ROWS + N_EXPERTS
    plan = _plan(cnt2, n_blk)
    ys = _moe(xs, plan, w1, w3, w2)
    out = _combine(x1, ys, pos, g2, S)
    return out.reshape(B, S, D)


def kernel(x, c, rel_bias_table, w_ada, b_ada, norm1_g, w_in, b_forget, q_gain, k_gain, w_branch_a, w_branch_b, w_out, norm2_g, w_router_group, b_router_group, w_router_expert, b_router_expert, w1, w3, w2):
    depth = w_ada.shape[0]
    for l in range(depth):
        mod = _ada(c, w_ada[l], b_ada[l])
        x = _layer(x, mod, rel_bias_table, norm1_g[l], w_in[l], b_forget[l], q_gain[l], k_gain[l],
                   w_branch_a[l], w_branch_b[l], w_out[l], norm2_g[l], w_router_group[l],
                   b_router_group[l], w_router_expert[l], b_router_expert[l], w1[l], w3[l], w2[l])
    return x
```

```python
import functools
import math

import numpy as np
import jax
import jax.numpy as jnp
from jax import lax
from jax.experimental import pallas as pl
from jax.experimental.pallas import tpu as pltpu

F32 = jnp.float32
BF16 = jnp.bfloat16

D_MODEL = 1024
HEAD_DIM = 64
FOX_HEADS = 8
DIL_GROUPS = ((128, 1), (512, 4), (2048, 16))
DIL_HEADS_PER_GROUP = 4
N_DIL_GROUPS = len(DIL_GROUPS)
DIL_HEADS = N_DIL_GROUPS * DIL_HEADS_PER_GROUP
FOX_WIDTH = FOX_HEADS * HEAD_DIM
DIL_WIDTH = DIL_HEADS * HEAD_DIM
DIL_OUT_WIDTH = DIL_HEADS_PER_GROUP * HEAD_DIM
NUM_BUCKETS = 32
REL_MAX_DISTANCE = 2048
N_GROUPS = 4
EXPERTS_PER_GROUP = 8
N_EXPERTS = N_GROUPS * EXPERTS_PER_GROUP
EXPERT_HIDDEN = D_MODEL // 2
EPS = 1e-6
LOG2E = math.log2(math.e)

OFF_FOX_Q = 0
OFF_FOX_K = OFF_FOX_Q + FOX_WIDTH
OFF_FOX_V = OFF_FOX_K + FOX_WIDTH
OFF_FOX_F = OFF_FOX_V + FOX_WIDTH
OFF_DIL_Q = OFF_FOX_F + FOX_HEADS
OFF_DIL_K = OFF_DIL_Q + DIL_WIDTH
OFF_DIL_V = OFF_DIL_K + DIL_WIDTH
OFF_GATE_A = OFF_DIL_V + DIL_WIDTH
OFF_GATE_B = OFF_GATE_A + D_MODEL
N_IN = OFF_GATE_B + D_MODEL

LANES = 128
UNIT = 256
DIL_L = 128

U_GATE_A, U_GATE_B, U_FOX_Q, U_FOX_K, U_FOX_V, U_DIL, U_FORGET = 0, 4, 8, 10, 12, 14, 23
N_UNITS = 24
P_WIDTH = U_DIL * UNIT
N_SLABS = 3 * N_DIL_GROUPS
_KIND = (["gate"] * 8 + ["norm"] * 4 + ["plain"] * 2 + ["norm", "norm", "plain"] * 3 + ["forget"])

TM_INPROJ = 1024
TM_PROJ = 1024
TQ_FOX = 512
MOE_ROWS = 256
MOE_TILE = 512
XS_WIDTH = D_MODEL // 2 + LANES
SUBLANES = 8
TILE_ROWS = 2 * MOE_TILE + N_EXPERTS * SUBLANES
MOE_GROUPS = MOE_ROWS // SUBLANES
GATHER_BUFS = 3
VMEM_LIMIT = 56 * 1024 * 1024


def _dot(a, b):
    return jnp.dot(a, b, preferred_element_type=F32)


def _dot_nt(a, b):
    return lax.dot_general(a, b, (((1,), (1,)), ((), ())), preferred_element_type=F32)


def _split3(x):
    hi = x.astype(BF16)
    r1 = x - hi.astype(F32)
    mid = r1.astype(BF16)
    lo = (r1 - mid.astype(F32)).astype(BF16)
    return hi, mid, lo


def _ada_kernel(c_ref, w_ref, b_ref, o_ref):
    c = c_ref[...]
    s = c * jax.nn.sigmoid(c)
    s_hi = s.astype(BF16)
    s_lo = (s - s_hi.astype(F32)).astype(BF16)
    w = w_ref[...]
    w_hi = w.astype(BF16)
    w_lo = (w - w_hi.astype(F32)).astype(BF16)
    acc = _dot(s_hi, w_hi) + _dot(s_hi, w_lo) + _dot(s_lo, w_hi)
    o_ref[...] = acc + b_ref[...]


def _ada(c, w_ada, b_ada):
    B = c.shape[0]
    n_out = w_ada.shape[1]
    tn = 512
    return pl.pallas_call(
        _ada_kernel,
        grid=(n_out // tn,),
        in_specs=[pl.BlockSpec((B, D_MODEL), lambda j: (0, 0)),
                  pl.BlockSpec((D_MODEL, tn), lambda j: (0, j)),
                  pl.BlockSpec((1, tn), lambda j: (0, j))],
        out_specs=pl.BlockSpec((B, tn), lambda j: (0, j)),
        out_shape=jax.ShapeDtypeStruct((B, n_out), F32),
        name="ada_mod",
    )(c, w_ada, b_ada.reshape(1, n_out))


def _pack_bf16_pair(lo, hi):
    lo_bits = pltpu.bitcast(lo.astype(BF16).astype(F32), jnp.uint32) >> 16
    hi_bits = pltpu.bitcast(hi.astype(BF16).astype(F32), jnp.uint32) & jnp.uint32(0xFFFF0000)
    return lo_bits | hi_bits


def _unpack_bf16_pair(u):
    lo = pltpu.bitcast(u << 16, F32).astype(BF16)
    hi = pltpu.bitcast(u & jnp.uint32(0xFFFF0000), F32).astype(BF16)
    return lo, hi


def _inproj_kernel(x_ref, g_ref, sc_ref, sh_ref, w_ref, gain_ref, bd_ref, p_ref, f_ref, s_ref):
    x = x_ref[...]
    ms = jnp.mean(x * x, axis=-1, keepdims=True)
    h = x * lax.rsqrt(ms + EPS) * g_ref[...]
    h = h * (1.0 + sc_ref[0]) + sh_ref[0]
    hb = h.astype(BF16)

    def unit(u):
        cols = slice(u * UNIT, (u + 1) * UNIT)
        acc = _dot(hb, w_ref[:, cols])
        kind = _KIND[u]
        if kind == "gate":
            return jax.nn.sigmoid(acc)
        if kind == "norm":
            ss = _dot((acc * acc).astype(BF16), bd_ref[...])
            return acc * lax.rsqrt(ss * (1.0 / HEAD_DIM) + EPS) * gain_ref[:, cols]
        return acc

    def emit(u):
        o = unit(u)
        if u < U_DIL:
            p_ref[:, u * UNIT:(u + 1) * UNIT] = o.astype(BF16)
        elif u < U_FORGET:
            s_ref[0, u - U_DIL] = _pack_bf16_pair(o[:, :LANES], o[:, LANES:])
        else:
            f_ref[...] = o[:, :LANES]

    normed = [u for u in range(N_UNITS) if _KIND[u] == "norm"]
    others = [u for u in range(N_UNITS) if _KIND[u] != "norm"]
    while normed or others:
        for group in (normed, others):
            if group:
                emit(group.pop(0))


def _inproj(x2d, norm_g, sc, sh, w_re, gain_row, S):
    T = x2d.shape[0]
    tm = TM_INPROJ
    per_b = S // tm
    bd = np.kron(np.eye(UNIT // HEAD_DIM), np.ones((HEAD_DIM, HEAD_DIM))).astype(np.float32)
    once = dict(pipeline_mode=pl.Buffered(1))
    return pl.pallas_call(
        _inproj_kernel,
        grid=(T // tm,),
        in_specs=[pl.BlockSpec((tm, D_MODEL), lambda i: (i, 0)),
                  pl.BlockSpec((1, D_MODEL), lambda i: (0, 0)),
                  pl.BlockSpec((1, 1, D_MODEL), lambda i: (i // per_b, 0, 0)),
                  pl.BlockSpec((1, 1, D_MODEL), lambda i: (i // per_b, 0, 0)),
                  pl.BlockSpec((D_MODEL, N_UNITS * UNIT), lambda i: (0, 0), **once),
                  pl.BlockSpec((1, N_UNITS * UNIT), lambda i: (0, 0), **once),
                  pl.BlockSpec((UNIT, UNIT), lambda i: (0, 0), **once)],
        out_specs=[pl.BlockSpec((tm, P_WIDTH), lambda i: (i, 0)),
                   pl.BlockSpec((tm, LANES), lambda i: (i, 0)),
                   pl.BlockSpec((1, N_SLABS, tm, LANES), lambda i: (i // per_b, 0, i % per_b, 0))],
        out_shape=[jax.ShapeDtypeStruct((T, P_WIDTH), BF16),
                   jax.ShapeDtypeStruct((T, LANES), F32),
                   jax.ShapeDtypeStruct((T // S, N_SLABS, S, LANES), jnp.uint32)],
        compiler_params=pltpu.CompilerParams(vmem_limit_bytes=VMEM_LIMIT),
        name="in_proj",
    )(x2d, norm_g, sc, sh, w_re, gain_row, jnp.asarray(bd, BF16))


def _fcum_kernel(f_ref, b_ref, tri_ref, o_ref):
    S = f_ref.shape[1]
    xf = f_ref[0] + b_ref[...]
    ls = (jnp.minimum(xf, 0.0) - jnp.log(1.0 + jnp.exp(-jnp.abs(xf)))) * LOG2E
    lst = ls.T
    carry = jnp.zeros((LANES, UNIT), F32)
    for blk in range(S // UNIT):
        seg = lst[:, blk * UNIT:(blk + 1) * UNIT]
        hi, mid, lo = _split3(seg)
        tri = tri_ref[...]
        res = _dot(hi, tri) + _dot(mid, tri) + _dot(lo, tri)
        o_ref[0, :, blk * UNIT:(blk + 1) * UNIT] = (res[:, :UNIT] + carry)[:FOX_HEADS]
        carry = carry + res[:, UNIT:]


def _fcum(fgt, b_forget):
    B, S, _ = fgt.shape
    brow = jnp.zeros((1, LANES), F32).at[0, :FOX_HEADS].set(b_forget)
    tri = np.concatenate([np.triu(np.ones((UNIT, UNIT))), np.ones((UNIT, UNIT))], axis=1)
    return pl.pallas_call(
        _fcum_kernel,
        grid=(B,),
        in_specs=[pl.BlockSpec((1, S, LANES), lambda b: (b, 0, 0)),
                  pl.BlockSpec((1, LANES), lambda b: (0, 0)),
                  pl.BlockSpec((UNIT, 2 * UNIT), lambda b: (0, 0))],
        out_specs=pl.BlockSpec((1, FOX_HEADS, S), lambda b: (b, 0, 0)),
        out_shape=jax.ShapeDtypeStruct((B, FOX_HEADS, S), F32),
        name="forget_cumsum",
    )(fgt, brow, jnp.asarray(tri, BF16))


def _fox_kernel(q_ref, k_ref, v_ref, ck_ref, o_ref):
    S = q_ref.shape[1]
    pair = pl.program_id(1)
    tq = TQ_FOX
    lane = lax.broadcasted_iota(jnp.int32, (1, LANES), 1)
    row = lax.broadcasted_iota(jnp.int32, (tq, tq), 0)
    col = lax.broadcasted_iota(jnp.int32, (tq, tq), 1)
    causal = col <= row
    cks = [ck_ref[0, pl.ds(2 * pair + hh, 1), :] for hh in range(2)]
    for t in range(S // tq):
        r0, r1 = t * tq, (t + 1) * tq
        qt = q_ref[0, r0:r1, :]
        outs = []
        for hh in range(2):
            hsel = (lane >= HEAD_DIM) == bool(hh)
            qm = jnp.where(hsel, qt, jnp.zeros_like(qt))
            ck = cks[hh]
            s = _dot_nt(qm, k_ref[0, :r1, :]) - ck[:, :r1]
            s_d = jnp.where(causal, s[:, r0:], -jnp.inf)
            s = jnp.concatenate([s[:, :r0], s_d], axis=1) if t > 0 else s_d
            m = jnp.max(s, axis=-1, keepdims=True)
            p = jnp.exp2(s - m)
            l = jnp.sum(p, axis=-1, keepdims=True)
            outs.append(_dot(p.astype(BF16), v_ref[0, :r1, :]) / l)
        o_ref[0, r0:r1, :] = jnp.where(lane < HEAD_DIM, outs[0], outs[1]).astype(BF16)


def _fox(p3, ck):
    B, S, _ = p3.shape
    nq, nk, nv = (U_FOX_Q * UNIT // LANES, U_FOX_K * UNIT // LANES, U_FOX_V * UNIT // LANES)
    return pl.pallas_call(
        _fox_kernel,
        grid=(B, FOX_HEADS // 2),
        in_specs=[pl.BlockSpec((1, S, LANES), lambda b, p: (b, 0, nq + p)),
                  pl.BlockSpec((1, S, LANES), lambda b, p: (b, 0, nk + p)),
                  pl.BlockSpec((1, S, LANES), lambda b, p: (b, 0, nv + p)),
                  pl.BlockSpec((1, FOX_HEADS, S), lambda b, p: (b, 0, 0))],
        out_specs=pl.BlockSpec((1, S, LANES), lambda b, p: (b, 0, p)),
        out_shape=jax.ShapeDtypeStruct((B, S, FOX_WIDTH), BF16),
        compiler_params=pltpu.CompilerParams(vmem_limit_bytes=VMEM_LIMIT),
        name="fox_attn",
    )(p3, p3, p3, ck)


def _t5_bucket(dist):
    max_exact = NUM_BUCKETS // 2
    d = np.maximum(dist, 1).astype(np.float32)
    large = max_exact + (np.log(d / max_exact) / np.log(REL_MAX_DISTANCE / max_exact)
                         * (NUM_BUCKETS - max_exact)).astype(np.int32)
    large = np.minimum(large, NUM_BUCKETS - 1)
    return np.where(dist < max_exact, dist, large).astype(np.int32)


def _relbias_kernel(tab_ref, bucket_ref, valid_ref, o_ref):
    g = pl.program_id(0)
    bk = bucket_ref[0]
    vd = valid_ref[0]
    for hs in range(DIL_HEADS_PER_GROUP):
        acc = jnp.zeros(bk.shape, F32)
        for b in range(NUM_BUCKETS):
            acc = jnp.where(bk == b, tab_ref[b, g * DIL_HEADS_PER_GROUP + hs], acc)
        bias = jnp.where(vd != 0, acc * LOG2E, -jnp.inf)
        o_ref[0, hs] = bias
        col = lax.broadcasted_iota(jnp.int32, bias.shape, 1)
        o_ref[1, hs] = jnp.where(col >= DIL_L, bias, -jnp.inf)


def _relbias(table):
    L = DIL_L
    i = np.arange(L)[:, None]
    j = np.arange(2 * L)[None, :]
    m = L + i - j
    valid = ((m >= 0) & (m <= L)).astype(np.int32)
    buckets = np.stack([_t5_bucket(np.clip(m, 0, None) * d) for _, d in DIL_GROUPS])
    valids = np.stack([valid] * N_DIL_GROUPS)
    return pl.pallas_call(
        _relbias_kernel,
        grid=(N_DIL_GROUPS,),
        in_specs=[pl.BlockSpec(memory_space=pltpu.SMEM),
                  pl.BlockSpec((1, L, 2 * L), lambda g: (g, 0, 0)),
                  pl.BlockSpec((1, L, 2 * L), lambda g: (g, 0, 0))],
        out_specs=pl.BlockSpec((2, DIL_HEADS_PER_GROUP, L, 2 * L), lambda g: (0, g, 0, 0)),
        out_shape=jax.ShapeDtypeStruct((2, DIL_HEADS, L, 2 * L), F32),
        name="rel_bias",
    )(table, jnp.asarray(buckets), jnp.asarray(valids))


def _dil_rows(start, d):
    return pl.ds(start, DIL_L) if d == 1 else pl.ds(start, DIL_L, stride=d)


def _dil_block(qkv_ref, bias_ref, m_scr, l_scr, acc_scr, g, d, r, n):
    L = DIL_L
    lane = lax.broadcasted_iota(jnp.int32, (1, LANES), 1)
    first = 1 - jnp.minimum(n, 1)
    cur = _dil_rows(r + d * (n * L), d)
    prev = _dil_rows(r + d * (jnp.maximum(n - 1, 0) * L), d)
    v_cur = _unpack_bf16_pair(qkv_ref[0, 3 * g + 2, cur, :])
    v_prev = _unpack_bf16_pair(qkv_ref[0, 3 * g + 2, prev, :])
    for pr in range(2):
        qt, k_cur = _unpack_bf16_pair(qkv_ref[0, 3 * g + pr, cur, :])
        _, k_prev = _unpack_bf16_pair(qkv_ref[0, 3 * g + pr, prev, :])
        kt = jnp.concatenate([k_prev, k_cur], axis=0)
        vt = jnp.concatenate([v_prev[pr], v_cur[pr]], axis=0)
        ms, ls, accs = [], [], []
        for hh in range(2):
            hsel = (lane >= HEAD_DIM) == bool(hh)
            qm = jnp.where(hsel, qt, jnp.zeros_like(qt))
            s = _dot_nt(qm, kt) + bias_ref[first, DIL_HEADS_PER_GROUP * g + 2 * pr + hh]
            m = jnp.max(s, axis=-1, keepdims=True)
            p = jnp.exp2(s - m)
            ms.append(m)
            ls.append(jnp.sum(p, axis=-1, keepdims=True))
            accs.append(_dot(p.astype(BF16), vt))
        low = lane < HEAD_DIM
        m_b = jnp.where(low, ms[0], ms[1])
        l_b = jnp.where(low, ls[0], ls[1])
        acc_b = jnp.where(low, accs[0], accs[1])
        if g == 0:
            m_scr[pr, cur, :] = m_b
            l_scr[pr, cur, :] = l_b
            acc_scr[pr, cur, :] = acc_b
        else:
            m_o = m_scr[pr, cur, :]
            m_n = jnp.maximum(m_o, m_b)
            a_o = jnp.exp2(m_o - m_n)
            a_b = jnp.exp2(m_b - m_n)
            m_scr[pr, cur, :] = m_n
            l_scr[pr, cur, :] = l_scr[pr, cur, :] * a_o + l_b * a_b
            acc_scr[pr, cur, :] = acc_scr[pr, cur, :] * a_o + acc_b * a_b


def _dil_kernel(qkv_ref, bias_ref, o_ref, m_scr, l_scr, acc_scr):
    S = o_ref.shape[1]
    for g, (window, d) in enumerate(DIL_GROUPS):
        nb = S // window

        def body(it, carry, g=g, d=d, nb=nb):
            _dil_block(qkv_ref, bias_ref, m_scr, l_scr, acc_scr, g, d, it // nb, it % nb)
            return carry
        lax.fori_loop(0, d * nb, body, 0, unroll=4)
    for pr in range(2):
        o_ref[0, :, pr * LANES:(pr + 1) * LANES] = (acc_scr[pr] / l_scr[pr]).astype(BF16)


def _dil(slabs, bias):
    B, _, S, _ = slabs.shape
    for window, d in DIL_GROUPS:
        assert window // d == DIL_L and S % window == 0
    stat = pltpu.VMEM((2, S, LANES), F32)
    return pl.pallas_call(
        _dil_kernel,
        grid=(B,),
        in_specs=[pl.BlockSpec((1, N_SLABS, S, LANES), lambda b: (b, 0, 0, 0)),
                  pl.BlockSpec(bias.shape, lambda b: (0, 0, 0, 0))],
        out_specs=pl.BlockSpec((1, S, DIL_OUT_WIDTH), lambda b: (b, 0, 0)),
        out_shape=jax.ShapeDtypeStruct((B, S, DIL_OUT_WIDTH), BF16),
        scratch_shapes=[stat, stat, stat],
        compiler_params=pltpu.CompilerParams(vmem_limit_bytes=VMEM_LIMIT),
        name="dil_attn",
    )(slabs, bias)


def _outproj_kernel(x_ref, ya_ref, yd_ref, ga_ref, gb_ref,
                    g1_ref, sc_ref, sh_ref, ng_ref, wa_ref, wb_ref, wo_ref, wr_ref, br_ref,
                    x1_ref, h2_ref, lg_ref):
    n_chunks = 2
    cm = x_ref.shape[0] // n_chunks
    for c in range(n_chunks):
        rows = slice(c * cm, (c + 1) * cm)
        a = _dot(ya_ref[rows, :], wa_ref[...])
        bm = _dot(yd_ref[rows, :], wb_ref[...])
        merged = ga_ref[rows, :].astype(F32) * a + gb_ref[rows, :].astype(F32) * bm
        out = _dot(merged.astype(BF16), wo_ref[...])
        x1 = x_ref[rows, :] + g1_ref[0] * out
        x1_ref[rows, :] = x1
        ms = jnp.mean(x1 * x1, axis=-1, keepdims=True)
        h = x1 * lax.rsqrt(ms + EPS) * ng_ref[...]
        h = h * (1.0 + sc_ref[0]) + sh_ref[0]
        hb = h.astype(BF16)
        h2_ref[rows, :] = hb
        lg_ref[rows, :] = _dot(hb, wr_ref[...]) + br_ref[...]


def _outproj(x2d, ya2d, yd2d, p2d, g1, sc2, sh2, norm_g, wa, wb, wo, wr, br, S):
    T = x2d.shape[0]
    tm = TM_PROJ
    per_b = S // tm
    row = lambda w: pl.BlockSpec((tm, w), lambda i: (i, 0))
    full = lambda a: pl.BlockSpec(a.shape, lambda i: (0,) * a.ndim)
    mod = pl.BlockSpec((1, 1, D_MODEL), lambda i: (i // per_b, 0, 0))
    return pl.pallas_call(
        _outproj_kernel,
        grid=(T // tm,),
        in_specs=[row(D_MODEL), row(FOX_WIDTH), row(DIL_OUT_WIDTH)]
                 + [pl.BlockSpec((tm, D_MODEL), lambda i: (i, U_GATE_A * UNIT // D_MODEL)),
                    pl.BlockSpec((tm, D_MODEL), lambda i: (i, U_GATE_B * UNIT // D_MODEL)),
                    mod, mod, mod, full(norm_g), full(wa), full(wb), full(wo), full(wr), full(br)],
        out_specs=[row(D_MODEL), row(D_MODEL), row(LANES)],
        out_shape=[jax.ShapeDtypeStruct((T, D_MODEL), F32),
                   jax.ShapeDtypeStruct((T, D_MODEL), BF16),
                   jax.ShapeDtypeStruct((T, LANES), F32)],
        compiler_params=pltpu.CompilerParams(vmem_limit_bytes=VMEM_LIMIT),
        name="out_proj",
    )(x2d, ya2d, yd2d, p2d, p2d, g1, sc2, sh2, norm_g, wa, wb, wo, wr, br)


def _dispatch_kernel(lg_ref, h_ref, tri_ref, xs_ref, cnt_ref, pos_ref):
    tt = lg_ref.shape[0]
    lt = lg_ref[...].T
    row = lambda i: lt[i:i + 1, :]
    neg = -jnp.inf
    g = [row(i) for i in range(N_GROUPS)]
    gmax = functools.reduce(jnp.maximum, g)
    gidx = jnp.full(gmax.shape, N_GROUPS - 1, jnp.int32)
    for i in reversed(range(N_GROUPS - 1)):
        gidx = jnp.where(g[i] == gmax, i, gidx)
    gsum = sum(jnp.exp(gi - gmax) for gi in g)
    el = []
    for j in range(EXPERTS_PER_GROUP):
        v = row(N_GROUPS + EXPERTS_PER_GROUP * (N_GROUPS - 1) + j)
        for gg in reversed(range(N_GROUPS - 1)):
            v = jnp.where(gidx == gg, row(N_GROUPS + EXPERTS_PER_GROUP * gg + j), v)
        el.append(v)

    def top(vals):
        best = functools.reduce(jnp.maximum, vals)
        idx = jnp.full(best.shape, EXPERTS_PER_GROUP - 1, jnp.int32)
        for j in reversed(range(EXPERTS_PER_GROUP - 1)):
            idx = jnp.where(vals[j] == best, j, idx)
        return best, idx

    v1, i1 = top(el)
    v2, i2 = top([jnp.where(i1 == j, neg, el[j]) for j in range(EXPERTS_PER_GROUP)])
    t = jnp.exp(v2 - v1)
    den = (1.0 + t) * gsum
    wts = [1.0 / den, t / den]
    eid = [gidx * EXPERTS_PER_GROUP + i1, gidx * EXPERTS_PER_GROUP + i2]

    esub = lax.broadcasted_iota(jnp.int32, (N_EXPERTS, tt), 0)
    ohf = jnp.concatenate([jnp.where(esub == eid[k], 1.0, 0.0) for k in range(2)], axis=1)
    res = _dot(ohf.astype(BF16), tri_ref[...])
    prefix, cnt = res[:, :2 * tt], res[:, 2 * tt:]
    cnt = (((cnt.astype(jnp.int32) + (SUBLANES - 1)) // SUBLANES) * SUBLANES).astype(F32)
    esub_c = lax.broadcasted_iota(jnp.int32, cnt.shape, 0)
    start = jnp.zeros_like(cnt)
    for e in range(N_EXPERTS - 1):
        start = start + jnp.where(esub_c > e, cnt[e:e + 1, :], 0.0)
    start_w = jnp.concatenate([start] * (2 * tt // LANES), axis=1)
    pos = jnp.sum(ohf * (start_w + prefix), axis=0, keepdims=True)
    pos_k = [pos[:, :tt], pos[:, tt:]]

    n_rows = xs_ref.shape[0]
    psub = lax.broadcasted_iota(jnp.int32, (n_rows, tt), 0).astype(F32)
    pm = [jnp.where(psub == pos_k[k], 1.0, 0.0).astype(BF16) for k in range(2)]
    xs = _dot(pm[0] + pm[1], h_ref[...])
    wsub = lax.broadcasted_iota(jnp.int32, (LANES, tt), 0)
    ws = jnp.zeros((n_rows, LANES), F32)
    for k in range(2):
        parts = _split3(wts[k])
        wrows = jnp.zeros((LANES, tt), F32)
        for j in range(3):
            wrows = jnp.where(wsub == j, parts[j].astype(F32), wrows)
        ws = ws + _dot_nt(pm[k], wrows.astype(BF16))
    half = D_MODEL // 2
    xs_ref[:, :half] = _pack_bf16_pair(xs[:, :half], xs[:, half:])
    xs_ref[:, half:] = pltpu.bitcast(ws, jnp.uint32)
    cnt_ref[0] = cnt.astype(jnp.int32)
    posr = jnp.where(wsub == 0, pos_k[0], jnp.where(wsub == 1, pos_k[1], 0.0))
    pos_ref[...] = posr.T


def _dispatch(logits, h2):
    T = logits.shape[0]
    tt = MOE_TILE
    n_tiles = T // tt
    tri = np.concatenate([np.triu(np.ones((2 * tt, 2 * tt)), 1), np.ones((2 * tt, LANES))], axis=1)
    return pl.pallas_call(
        _dispatch_kernel,
        grid=(n_tiles,),
        in_specs=[pl.BlockSpec((tt, LANES), lambda i: (i, 0)),
                  pl.BlockSpec((tt, D_MODEL), lambda i: (i, 0)),
                  pl.BlockSpec(tri.shape, lambda i: (0, 0))],
        out_specs=[pl.BlockSpec((TILE_ROWS, XS_WIDTH), lambda i: (i, 0)),
                   pl.BlockSpec((1, N_EXPERTS, LANES), lambda i: (i, 0, 0)),
                   pl.BlockSpec((tt, LANES), lambda i: (i, 0))],
        out_shape=[jax.ShapeDtypeStruct((n_tiles * TILE_ROWS, XS_WIDTH), jnp.uint32),
                   jax.ShapeDtypeStruct((n_tiles, N_EXPERTS, LANES), jnp.int32),
                   jax.ShapeDtypeStruct((T, LANES), F32)],
        compiler_params=pltpu.CompilerParams(vmem_limit_bytes=VMEM_LIMIT),
        name="moe_dispatch",
    )(logits, h2, jnp.asarray(tri, BF16))


def _plan_kernel(cnt_ref, be_ref, nv_ref, nxt_ref, grp_ref, used_ref, cs_ref):
    n_tiles = cnt_ref.shape[0]
    n_blk = be_ref.shape[0]
    rows = MOE_ROWS
    row_shift = rows.bit_length() - 1
    grp_shift = SUBLANES.bit_length() - 1
    assert rows == 1 << row_shift and SUBLANES == 1 << grp_shift

    def tile_starts(t, c):
        def per_e(e, acc):
            cs_ref[t * N_EXPERTS + e] = acc
            return acc + cnt_ref[t, e]
        used_ref[t] = lax.fori_loop(0, N_EXPERTS, per_e, 0, unroll=8)
        return c
    lax.fori_loop(0, n_tiles, tile_starts, 0)

    def clear(b, c):
        nv_ref[b] = 0
        return c
    lax.fori_loop(0, n_blk, clear, 0)

    def clear_groups(g, c):
        grp_ref[g] = 0
        return c
    lax.fori_loop(0, n_blk * MOE_GROUPS, clear_groups, 0, unroll=8)

    def per_expert(e, b):
        g0 = b * MOE_GROUPS

        def per_tile(t, tot):
            c = cnt_ref[t, e]
            src = t * TILE_ROWS + cs_ref[t * N_EXPERTS + e]
            first = g0 + lax.shift_right_logical(tot, grp_shift)

            def per_group(k, cc):
                grp_ref[first + k] = src + k * SUBLANES
                return cc
            lax.fori_loop(0, lax.shift_right_logical(c, grp_shift), per_group, 0)
            return tot + c
        tot = lax.fori_loop(0, n_tiles, per_tile, 0)

        def per_block(j, c):
            be_ref[b + j] = e
            nv_ref[b + j] = jnp.minimum(rows, tot - j * rows)
            return c
        nb = lax.shift_right_logical(tot + rows - 1, row_shift)
        lax.fori_loop(0, nb, per_block, 0)
        return b + nb
    n_used = lax.fori_loop(0, N_EXPERTS, per_expert, 0)

    def unused(b, c):
        be_ref[b] = be_ref[n_used - 1]
        nxt_ref[b] = -1
        return c
    lax.fori_loop(n_used, n_blk, unused, 0)

    def next_run(k, nf):
        b = n_used - 1 - k
        nf = jnp.where(be_ref[b] != be_ref[jnp.minimum(b + 1, n_used - 1)], b + 1, nf)
        nxt_ref[b] = nf
        return nf
    lax.fori_loop(0, n_used, next_run, -1)


def _plan(cnt, n_blk):
    n_tiles = cnt.shape[0]
    smem = pl.BlockSpec(memory_space=pltpu.SMEM)
    i32 = lambda n: jax.ShapeDtypeStruct((n,), jnp.int32)
    return pl.pallas_call(
        _plan_kernel,
        in_specs=[smem],
        out_specs=[smem] * 5,
        out_shape=[i32(n_blk), i32(n_blk), i32(n_blk), i32(n_blk * MOE_GROUPS), i32(n_tiles)],
        scratch_shapes=[pltpu.SMEM((n_tiles * N_EXPERTS,), jnp.int32)],
        name="moe_plan",
    )(cnt)


def _pow2_pieces(n, fn):
    for b in reversed(range(SUBLANES.bit_length() - 1, MOE_ROWS.bit_length())):
        size = 1 << b

        @pl.when((n & size) != 0)
        def _():
            fn((n >> (b + 1)) << (b + 1), size)


def _moe_kernel(be_ref, nv_ref, nxt_ref, grp_ref, used_ref,
                w1_hbm, w3_hbm, w2_hbm, xs_hbm, ys_hbm,
                xbuf, ybuf, wb1, wb3, wb2, wst1, wst3, wst2, wslot, gsem, ssem, wsem):
    i = pl.program_id(0)
    last = pl.num_programs(0) - 1
    slot = i % 2
    nv = nv_ref[i]
    half = D_MODEL // 2
    grp_shift = SUBLANES.bit_length() - 1

    def group_row(blk, g):
        return pl.multiple_of(grp_ref[blk * MOE_GROUPS + g], SUBLANES)

    def gather(blk, s):
        top = jnp.maximum(lax.shift_right_logical(nv_ref[blk], grp_shift) - 1, 0)
        for g in range(MOE_GROUPS):
            src = group_row(blk, jnp.minimum(g, top))
            pltpu.make_async_copy(xs_hbm.at[pl.ds(src, SUBLANES)],
                                  xbuf.at[s, pl.ds(g * SUBLANES, SUBLANES)], gsem.at[s]).start()

    def wait_gather(s):
        pltpu.make_async_copy(xs_hbm.at[pl.ds(0, MOE_ROWS)], xbuf.at[s], gsem.at[s]).wait()

    def scatter(blk, s):
        def body(g, c):
            r = pl.multiple_of(g * SUBLANES, SUBLANES)
            pltpu.make_async_copy(ybuf.at[s, pl.ds(r, SUBLANES)],
                                  ys_hbm.at[pl.ds(group_row(blk, g), SUBLANES)], ssem.at[s]).start()
            return c
        lax.fori_loop(0, lax.shift_right_logical(nv_ref[blk], grp_shift), body, 0)

    def wait_scatter(s, count):
        _pow2_pieces(count, lambda a, size: pltpu.make_async_copy(
            ybuf.at[s, pl.ds(0, size)], ys_hbm.at[pl.ds(0, size)], ssem.at[s]).wait())

    @pl.when(i == 0)
    def _():
        @pl.when(nv > 0)
        def _():
            gather(0, 0)
            gather(jnp.minimum(1, last), 1)

        ybuf[1] = jnp.zeros(ybuf.shape[1:], ybuf.dtype)
        n_tiles = used_ref.shape[0]

        def fill(t, c):
            row0 = t * TILE_ROWS + used_ref[t]
            _pow2_pieces(TILE_ROWS - used_ref[t], lambda a, size: pltpu.make_async_copy(
                ybuf.at[1, pl.ds(0, size)], ys_hbm.at[pl.ds(pl.multiple_of(row0 + a, SUBLANES), size)],
                ssem.at[1]).start())
            return c
        lax.fori_loop(0, n_tiles, fill, 0)

        def drain(t, c):
            wait_scatter(1, TILE_ROWS - used_ref[t])
            return c
        lax.fori_loop(0, n_tiles, drain, 0)

    @pl.when(i >= 2)
    def _():
        wait_scatter(slot, nv_ref[jnp.maximum(i - 2, 0)])

    xslot = i % GATHER_BUFS
    issuer_used = jnp.where(i >= 2, nv_ref[jnp.maximum(i - 2, 0)], nv_ref[0]) > 0

    @pl.when((nv == 0) & (i > 0) & issuer_used)
    def _():
        wait_gather(xslot)

    @pl.when(nv > 0)
    def _():
        e = be_ref[i]
        e_prev = be_ref[jnp.maximum(i - 1, 0)]

        def weight_copies(ex, ws):
            return [pltpu.make_async_copy(src.at[ex], dst.at[ws], wsem.at[ws])
                    for src, dst in ((w1_hbm, wst1), (w3_hbm, wst3), (w2_hbm, wst2))]

        @pl.when(i == 0)
        def _():
            wslot[0] = 0
            for cp in weight_copies(e, 0):
                cp.start()

        @pl.when((i == 0) | (e != e_prev))
        def _():
            ws = wslot[0]
            for cp in weight_copies(e, ws):
                cp.wait()
            wb1[...] = wst1[ws].astype(BF16)
            wb3[...] = wst3[ws].astype(BF16)
            wb2[...] = wst2[ws].astype(BF16)
            nb = nxt_ref[i]

            @pl.when(nb >= 0)
            def _():
                for cp in weight_copies(be_ref[jnp.maximum(nb, 0)], 1 - ws):
                    cp.start()
            wslot[0] = 1 - ws

        wait_gather(xslot)
        gather(jnp.minimum(i + 2, last), (i + 2) % GATHER_BUFS)
        u = xbuf[xslot]
        xa, xb = _unpack_bf16_pair(u[:, :half])
        wv = pltpu.bitcast(u[:, half:], F32)
        roww = wv[:, 0:1] + wv[:, 1:2] + wv[:, 2:3]
        a = _dot(xa, wb1[:half, :]) + _dot(xb, wb1[half:, :])
        b = _dot(xa, wb3[:half, :]) + _dot(xb, wb3[half:, :])
        hmid = (a * jax.nn.sigmoid(a) * b).astype(BF16)
        y = _dot(hmid, wb2[...]) * roww
        ybuf[slot] = _pack_bf16_pair(y[:, :half], y[:, half:])
        scatter(i, slot)

    @pl.when(i == last)
    def _():
        @pl.when((last >= 1) & (nv_ref[jnp.maximum(last - 1, 0)] > 0))
        def _():
            wait_gather((last + 1) % GATHER_BUFS)

        @pl.when(nv > 0)
        def _():
            wait_gather((last + 2) % GATHER_BUFS)

        @pl.when(last >= 1)
        def _():
            wait_scatter(1 - slot, nv_ref[jnp.maximum(last - 1, 0)])
        wait_scatter(slot, nv)


def _moe(xs, plan, w1, w3, w2):
    n_blk = plan[0].shape[0]
    rows = MOE_ROWS
    half = D_MODEL // 2
    hbm = pl.BlockSpec(memory_space=pl.ANY)
    grid_spec = pltpu.PrefetchScalarGridSpec(
        num_scalar_prefetch=5,
        grid=(n_blk,),
        in_specs=[hbm] * 4,
        out_specs=hbm,
        scratch_shapes=[pltpu.VMEM((GATHER_BUFS, rows, XS_WIDTH), jnp.uint32),
                        pltpu.VMEM((2, rows, half), jnp.uint32),
                        pltpu.VMEM((D_MODEL, EXPERT_HIDDEN), BF16),
                        pltpu.VMEM((D_MODEL, EXPERT_HIDDEN), BF16),
                        pltpu.VMEM((EXPERT_HIDDEN, D_MODEL), BF16),
                        pltpu.VMEM((2, D_MODEL, EXPERT_HIDDEN), F32),
                        pltpu.VMEM((2, D_MODEL, EXPERT_HIDDEN), F32),
                        pltpu.VMEM((2, EXPERT_HIDDEN, D_MODEL), F32),
                        pltpu.SMEM((1,), jnp.int32),
                        pltpu.SemaphoreType.DMA((GATHER_BUFS,)),
                        pltpu.SemaphoreType.DMA((2,)),
                        pltpu.SemaphoreType.DMA((2,))])
    return pl.pallas_call(
        _moe_kernel,
        grid_spec=grid_spec,
        out_shape=jax.ShapeDtypeStruct((xs.shape[0], half), jnp.uint32),
        compiler_params=pltpu.CompilerParams(dimension_semantics=("arbitrary",),
                                             vmem_limit_bytes=VMEM_LIMIT),
        name="moe_ffn",
    )(*plan, w1, w3, w2, xs)


def _combine_kernel(x1_ref, ys_ref, pos_ref, g2_ref, o_ref):
    tt = x1_ref.shape[0]
    half = D_MODEL // 2
    pos = pos_ref[...]
    pcol = lax.broadcasted_iota(jnp.int32, (tt, ys_ref.shape[0]), 1).astype(F32)
    sel = (jnp.where(pcol == pos[:, 0:1], 1.0, 0.0) + jnp.where(pcol == pos[:, 1:2], 1.0, 0.0))
    sel = sel.astype(BF16)
    lo, hi = _unpack_bf16_pair(ys_ref[...])
    g2 = g2_ref[0]
    x1 = x1_ref[...]
    o_ref[:, :half] = x1[:, :half] + g2[:, :half] * _dot(sel, lo)
    o_ref[:, half:] = x1[:, half:] + g2[:, half:] * _dot(sel, hi)


def _combine(x1, ys, pos, g2, S):
    T = x1.shape[0]
    tt = MOE_TILE
    per_b = S // tt
    return pl.pallas_call(
        _combine_kernel,
        grid=(T // tt,),
        in_specs=[pl.BlockSpec((tt, D_MODEL), lambda i: (i, 0)),
                  pl.BlockSpec((TILE_ROWS, D_MODEL // 2), lambda i: (i, 0)),
                  pl.BlockSpec((tt, LANES), lambda i: (i, 0)),
                  pl.BlockSpec((1, 1, D_MODEL), lambda i: (i // per_b, 0, 0))],
        out_specs=pl.BlockSpec((tt, D_MODEL), lambda i: (i, 0)),
        out_shape=jax.ShapeDtypeStruct((T, D_MODEL), F32),
        compiler_params=pltpu.CompilerParams(vmem_limit_bytes=VMEM_LIMIT),
        name="moe_combine",
    )(x1, ys, pos, g2)


def _prep_w_in(w_in):
    dq = w_in[:, OFF_DIL_Q:OFF_DIL_K]
    dk = w_in[:, OFF_DIL_K:OFF_DIL_V]
    dv = w_in[:, OFF_DIL_V:OFF_GATE_A]
    dil = []
    for g in range(N_DIL_GROUPS):
        p0 = slice(g * DIL_OUT_WIDTH, g * DIL_OUT_WIDTH + LANES)
        p1 = slice(g * DIL_OUT_WIDTH + LANES, (g + 1) * DIL_OUT_WIDTH)
        dil += [dq[:, p0], dk[:, p0], dq[:, p1], dk[:, p1], dv[:, p0], dv[:, p1]]
    pad = jnp.zeros((D_MODEL, UNIT - FOX_HEADS), w_in.dtype)
    cols = [w_in[:, OFF_GATE_A:OFF_GATE_B], w_in[:, OFF_GATE_B:N_IN],
            w_in[:, OFF_FOX_Q:OFF_FOX_K], w_in[:, OFF_FOX_K:OFF_FOX_V], w_in[:, OFF_FOX_V:OFF_FOX_F],
            *dil, w_in[:, OFF_FOX_F:OFF_DIL_Q], pad]
    return jnp.concatenate(cols, axis=1).astype(BF16)


def _prep_gain(q_gain, k_gain):
    qs = HEAD_DIM ** -0.5 * LOG2E
    ones = jnp.ones((UNIT,), F32)
    fq = q_gain[:FOX_HEADS].reshape(-1) * qs
    fk = k_gain[:FOX_HEADS].reshape(-1)
    dq = q_gain[FOX_HEADS:].reshape(-1) * qs
    dk = k_gain[FOX_HEADS:].reshape(-1)
    dil = []
    for g in range(N_DIL_GROUPS):
        p0 = slice(g * DIL_OUT_WIDTH, g * DIL_OUT_WIDTH + LANES)
        p1 = slice(g * DIL_OUT_WIDTH + LANES, (g + 1) * DIL_OUT_WIDTH)
        dil += [dq[p0], dk[p0], dq[p1], dk[p1], ones]
    parts = [ones] * 8 + [fq, fk, ones, ones] + dil + [ones]
    return jnp.concatenate(parts).reshape(1, N_UNITS * UNIT)


def _layer(x, mod, rel_bias_table, norm1_g, w_in, b_forget, q_gain, k_gain, w_branch_a, w_branch_b,
           w_out, norm2_g, w_rg, b_rg, w_re, b_re, w1, w3, w2):
    B, S, D = x.shape
    T = B * S
    sh1, sc1, g1, sh2, sc2, g2 = [m.reshape(B, 1, D) for m in jnp.split(mod, 6, axis=-1)]
    x2d = x.reshape(T, D)

    p2d, fgt, slabs = _inproj(x2d, norm1_g.reshape(1, D), sc1, sh1, _prep_w_in(w_in),
                              _prep_gain(q_gain, k_gain), S)
    p3 = p2d.reshape(B, S, P_WIDTH)
    ck = _fcum(fgt.reshape(B, S, LANES), b_forget)
    ya = _fox(p3, ck)
    yd = _dil(slabs, _relbias(rel_bias_table))

    n_router = N_GROUPS + N_EXPERTS
    wr = jnp.concatenate([w_rg, w_re, jnp.zeros((D, LANES - n_router), F32)], axis=1).astype(BF16)
    br = jnp.concatenate([b_rg, b_re, jnp.zeros((LANES - n_router,), F32)]).reshape(1, LANES)
    x1, h2, logits = _outproj(x2d, ya.reshape(T, FOX_WIDTH), yd.reshape(T, DIL_OUT_WIDTH), p2d,
                              g1, sc2, sh2, norm2_g.reshape(1, D),
                              w_branch_a.astype(BF16), w_branch_b.astype(BF16), w_out.astype(BF16),
                              wr, br, S)
    xs, cnt, pos = _dispatch(logits, h2)
    cnt2 = cnt[:, :, 0]
    n_blk = cnt.shape[0] * TILE_ROWS // MOE_ROWS + N_EXPERTS
    plan = _plan(cnt2, n_blk)
    ys = _moe(xs, plan, w1, w3, w2)
    out = _combine(x1, ys, pos, g2, S)
    return out.reshape(B, S, D)


def kernel(x, c, rel_bias_table, w_ada, b_ada, norm1_g, w_in, b_forget, q_gain, k_gain, w_branch_a, w_branch_b, w_out, norm2_g, w_router_group, b_router_group, w_router_expert, b_router_expert, w1, w3, w2):
    depth = w_ada.shape[0]
    for l in range(depth):
        mod = _ada(c, w_ada[l], b_ada[l])
        x = _layer(x, mod, rel_bias_table, norm1_g[l], w_in[l], b_forget[l], q_gain[l], k_gain[l],
                   w_branch_a[l], w_branch_b[l], w_out[l], norm2_g[l], w_router_group[l],
                   b_router_group[l], w_router_expert[l], b_router_expert[l], w1[l], w3[l], w2[l])
    return x
```

```python
import functools
import math

import numpy as np
import jax
import jax.numpy as jnp
from jax import lax
from jax.experimental import pallas as pl
from jax.experimental.pallas import tpu as pltpu

F32 = jnp.float32
BF16 = jnp.bfloat16

D_MODEL = 1024
HEAD_DIM = 64
FOX_HEADS = 8
DIL_GROUPS = ((128, 1), (512, 4), (2048, 16))
DIL_HEADS_PER_GROUP = 4
N_DIL_GROUPS = len(DIL_GROUPS)
DIL_HEADS = N_DIL_GROUPS * DIL_HEADS_PER_GROUP
FOX_WIDTH = FOX_HEADS * HEAD_DIM
DIL_WIDTH = DIL_HEADS * HEAD_DIM
DIL_OUT_WIDTH = DIL_HEADS_PER_GROUP * HEAD_DIM
NUM_BUCKETS = 32
REL_MAX_DISTANCE = 2048
N_GROUPS = 4
EXPERTS_PER_GROUP = 8
N_EXPERTS = N_GROUPS * EXPERTS_PER_GROUP
EXPERT_HIDDEN = D_MODEL // 2
EPS = 1e-6
LOG2E = math.log2(math.e)

OFF_FOX_Q = 0
OFF_FOX_K = OFF_FOX_Q + FOX_WIDTH
OFF_FOX_V = OFF_FOX_K + FOX_WIDTH
OFF_FOX_F = OFF_FOX_V + FOX_WIDTH
OFF_DIL_Q = OFF_FOX_F + FOX_HEADS
OFF_DIL_K = OFF_DIL_Q + DIL_WIDTH
OFF_DIL_V = OFF_DIL_K + DIL_WIDTH
OFF_GATE_A = OFF_DIL_V + DIL_WIDTH
OFF_GATE_B = OFF_GATE_A + D_MODEL
N_IN = OFF_GATE_B + D_MODEL

LANES = 128
UNIT = 256
DIL_L = 128

U_GATE_A, U_GATE_B, U_FOX_Q, U_FOX_K, U_FOX_V, U_DIL, U_FORGET = 0, 4, 8, 10, 12, 14, 23
N_UNITS = 24
P_WIDTH = U_DIL * UNIT
N_SLABS = 3 * N_DIL_GROUPS
_KIND = (["gate"] * 8 + ["norm"] * 4 + ["plain"] * 2 + ["norm", "norm", "plain"] * 3 + ["forget"])

TM_INPROJ = 1024
TM_PROJ = 1024
TQ_FOX = 512
MOE_ROWS = 256
MOE_TILE = 512
XS_WIDTH = D_MODEL // 2 + LANES
SUBLANES = 8
TILE_ROWS = 2 * MOE_TILE + N_EXPERTS * SUBLANES
MOE_GROUPS = MOE_ROWS // SUBLANES
GATHER_BUFS = 3
VMEM_LIMIT = 56 * 1024 * 1024


def _dot(a, b):
    return jnp.dot(a, b, preferred_element_type=F32)


def _dot_nt(a, b):
    return lax.dot_general(a, b, (((1,), (1,)), ((), ())), preferred_element_type=F32)


def _split3(x):
    hi = x.astype(BF16)
    r1 = x - hi.astype(F32)
    mid = r1.astype(BF16)
    lo = (r1 - mid.astype(F32)).astype(BF16)
    return hi, mid, lo


def _ada_kernel(c_ref, w_ref, b_ref, o_ref):
    c = c_ref[...]
    s = c * jax.nn.sigmoid(c)
    s_hi = s.astype(BF16)
    s_lo = (s - s_hi.astype(F32)).astype(BF16)
    w = w_ref[...]
    w_hi = w.astype(BF16)
    w_lo = (w - w_hi.astype(F32)).astype(BF16)
    acc = _dot(s_hi, w_hi) + _dot(s_hi, w_lo) + _dot(s_lo, w_hi)
    o_ref[...] = acc + b_ref[...]


def _ada(c, w_ada, b_ada):
    B = c.shape[0]
    n_out = w_ada.shape[1]
    tn = 512
    return pl.pallas_call(
        _ada_kernel,
        grid=(n_out // tn,),
        in_specs=[pl.BlockSpec((B, D_MODEL), lambda j: (0, 0)),
                  pl.BlockSpec((D_MODEL, tn), lambda j: (0, j)),
                  pl.BlockSpec((1, tn), lambda j: (0, j))],
        out_specs=pl.BlockSpec((B, tn), lambda j: (0, j)),
        out_shape=jax.ShapeDtypeStruct((B, n_out), F32),
        name="ada_mod",
    )(c, w_ada, b_ada.reshape(1, n_out))


def _pack_bf16_pair(lo, hi):
    lo_bits = pltpu.bitcast(lo.astype(BF16).astype(F32), jnp.uint32) >> 16
    hi_bits = pltpu.bitcast(hi.astype(BF16).astype(F32), jnp.uint32) & jnp.uint32(0xFFFF0000)
    return lo_bits | hi_bits


def _unpack_bf16_pair(u):
    lo = pltpu.bitcast(u << 16, F32).astype(BF16)
    hi = pltpu.bitcast(u & jnp.uint32(0xFFFF0000), F32).astype(BF16)
    return lo, hi


def _inproj_kernel(x_ref, g_ref, sc_ref, sh_ref, w_ref, gain_ref, bd_ref, p_ref, f_ref, s_ref):
    x = x_ref[...]
    ms = jnp.mean(x * x, axis=-1, keepdims=True)
    h = x * lax.rsqrt(ms + EPS) * g_ref[...]
    h = h * (1.0 + sc_ref[0]) + sh_ref[0]
    hb = h.astype(BF16)

    def unit(u):
        cols = slice(u * UNIT, (u + 1) * UNIT)
        acc = _dot(hb, w_ref[:, cols])
        kind = _KIND[u]
        if kind == "gate":
            return jax.nn.sigmoid(acc)
        if kind == "norm":
            ss = _dot((acc * acc).astype(BF16), bd_ref[...])
            return acc * lax.rsqrt(ss * (1.0 / HEAD_DIM) + EPS) * gain_ref[:, cols]
        return acc

    def emit(u):
        o = unit(u)
        if u < U_DIL:
            p_ref[:, u * UNIT:(u + 1) * UNIT] = o.astype(BF16)
        elif u < U_FORGET:
            s_ref[0, u - U_DIL] = _pack_bf16_pair(o[:, :LANES], o[:, LANES:])
        else:
            f_ref[...] = o[:, :LANES]

    normed = [u for u in range(N_UNITS) if _KIND[u] == "norm"]
    others = [u for u in range(N_UNITS) if _KIND[u] != "norm"]
    while normed or others:
        for group in (normed, others):
            if group:
                emit(group.pop(0))


def _inproj(x2d, norm_g, sc, sh, w_re, gain_row, S):
    T = x2d.shape[0]
    tm = TM_INPROJ
    per_b = S // tm
    bd = np.kron(np.eye(UNIT // HEAD_DIM), np.ones((HEAD_DIM, HEAD_DIM))).astype(np.float32)
    once = dict(pipeline_mode=pl.Buffered(1))
    return pl.pallas_call(
        _inproj_kernel,
        grid=(T // tm,),
        in_specs=[pl.BlockSpec((tm, D_MODEL), lambda i: (i, 0)),
                  pl.BlockSpec((1, D_MODEL), lambda i: (0, 0)),
                  pl.BlockSpec((1, 1, D_MODEL), lambda i: (i // per_b, 0, 0)),
                  pl.BlockSpec((1, 1, D_MODEL), lambda i: (i // per_b, 0, 0)),
                  pl.BlockSpec((D_MODEL, N_UNITS * UNIT), lambda i: (0, 0), **once),
                  pl.BlockSpec((1, N_UNITS * UNIT), lambda i: (0, 0), **once),
                  pl.BlockSpec((UNIT, UNIT), lambda i: (0, 0), **once)],
        out_specs=[pl.BlockSpec((tm, P_WIDTH), lambda i: (i, 0)),
                   pl.BlockSpec((tm, LANES), lambda i: (i, 0)),
                   pl.BlockSpec((1, N_SLABS, tm, LANES), lambda i: (i // per_b, 0, i % per_b, 0))],
        out_shape=[jax.ShapeDtypeStruct((T, P_WIDTH), BF16),
                   jax.ShapeDtypeStruct((T, LANES), F32),
                   jax.ShapeDtypeStruct((T // S, N_SLABS, S, LANES), jnp.uint32)],
        compiler_params=pltpu.CompilerParams(vmem_limit_bytes=VMEM_LIMIT),
        name="in_proj",
    )(x2d, norm_g, sc, sh, w_re, gain_row, jnp.asarray(bd, BF16))


def _fcum_kernel(f_ref, b_ref, tri_ref, o_ref):
    S = f_ref.shape[1]
    xf = f_ref[0] + b_ref[...]
    ls = (jnp.minimum(xf, 0.0) - jnp.log(1.0 + jnp.exp(-jnp.abs(xf)))) * LOG2E
    lst = ls.T
    carry = jnp.zeros((LANES, UNIT), F32)
    for blk in range(S // UNIT):
        seg = lst[:, blk * UNIT:(blk + 1) * UNIT]
        hi, mid, lo = _split3(seg)
        tri = tri_ref[...]
        res = _dot(hi, tri) + _dot(mid, tri) + _dot(lo, tri)
        o_ref[0, :, blk * UNIT:(blk + 1) * UNIT] = (res[:, :UNIT] + carry)[:FOX_HEADS]
        carry = carry + res[:, UNIT:]


def _fcum(fgt, b_forget):
    B, S, _ = fgt.shape
    brow = jnp.zeros((1, LANES), F32).at[0, :FOX_HEADS].set(b_forget)
    tri = np.concatenate([np.triu(np.ones((UNIT, UNIT))), np.ones((UNIT, UNIT))], axis=1)
    return pl.pallas_call(
        _fcum_kernel,
        grid=(B,),
        in_specs=[pl.BlockSpec((1, S, LANES), lambda b: (b, 0, 0)),
                  pl.BlockSpec((1, LANES), lambda b: (0, 0)),
                  pl.BlockSpec((UNIT, 2 * UNIT), lambda b: (0, 0))],
        out_specs=pl.BlockSpec((1, FOX_HEADS, S), lambda b: (b, 0, 0)),
        out_shape=jax.ShapeDtypeStruct((B, FOX_HEADS, S), F32),
        name="forget_cumsum",
    )(fgt, brow, jnp.asarray(tri, BF16))


def _fox_kernel(q_ref, k_ref, v_ref, ck_ref, o_ref):
    S = q_ref.shape[1]
    pair = pl.program_id(1)
    tq = TQ_FOX
    lane = lax.broadcasted_iota(jnp.int32, (1, LANES), 1)
    row = lax.broadcasted_iota(jnp.int32, (tq, tq), 0)
    col = lax.broadcasted_iota(jnp.int32, (tq, tq), 1)
    causal = col <= row
    cks = [ck_ref[0, pl.ds(2 * pair + hh, 1), :] for hh in range(2)]
    for t in reversed(range(S // tq)):
        r0, r1 = t * tq, (t + 1) * tq
        qt = q_ref[0, r0:r1, :]
        outs = []
        for hh in range(2):
            hsel = (lane >= HEAD_DIM) == bool(hh)
            qm = jnp.where(hsel, qt, jnp.zeros_like(qt))
            ck = cks[hh]
            s = _dot_nt(qm, k_ref[0, :r1, :]) - ck[:, :r1]
            s_d = jnp.where(causal, s[:, r0:], -jnp.inf)
            s = jnp.concatenate([s[:, :r0], s_d], axis=1) if t > 0 else s_d
            m = jnp.max(s, axis=-1, keepdims=True)
            p = jnp.exp2(s - m)
            l = jnp.sum(p, axis=-1, keepdims=True)
            outs.append(_dot(p.astype(BF16), v_ref[0, :r1, :]) / l)
        o_ref[0, r0:r1, :] = jnp.where(lane < HEAD_DIM, outs[0], outs[1]).astype(BF16)


def _fox(p3, ck):
    B, S, _ = p3.shape
    nq, nk, nv = (U_FOX_Q * UNIT // LANES, U_FOX_K * UNIT // LANES, U_FOX_V * UNIT // LANES)
    return pl.pallas_call(
        _fox_kernel,
        grid=(B, FOX_HEADS // 2),
        in_specs=[pl.BlockSpec((1, S, LANES), lambda b, p: (b, 0, nq + p)),
                  pl.BlockSpec((1, S, LANES), lambda b, p: (b, 0, nk + p)),
                  pl.BlockSpec((1, S, LANES), lambda b, p: (b, 0, nv + p)),
                  pl.BlockSpec((1, FOX_HEADS, S), lambda b, p: (b, 0, 0))],
        out_specs=pl.BlockSpec((1, S, LANES), lambda b, p: (b, 0, p)),
        out_shape=jax.ShapeDtypeStruct((B, S, FOX_WIDTH), BF16),
        compiler_params=pltpu.CompilerParams(vmem_limit_bytes=VMEM_LIMIT),
        name="fox_attn",
    )(p3, p3, p3, ck)


def _t5_bucket(dist):
    max_exact = NUM_BUCKETS // 2
    d = np.maximum(dist, 1).astype(np.float32)
    large = max_exact + (np.log(d / max_exact) / np.log(REL_MAX_DISTANCE / max_exact)
                         * (NUM_BUCKETS - max_exact)).astype(np.int32)
    large = np.minimum(large, NUM_BUCKETS - 1)
    return np.where(dist < max_exact, dist, large).astype(np.int32)


def _relbias_kernel(tab_ref, bucket_ref, valid_ref, o_ref):
    g = pl.program_id(0)
    bk = bucket_ref[0]
    vd = valid_ref[0]
    for hs in range(DIL_HEADS_PER_GROUP):
        acc = jnp.zeros(bk.shape, F32)
        for b in range(NUM_BUCKETS):
            acc = jnp.where(bk == b, tab_ref[b, g * DIL_HEADS_PER_GROUP + hs], acc)
        bias = jnp.where(vd != 0, acc * LOG2E, -jnp.inf)
        o_ref[0, hs] = bias
        col = lax.broadcasted_iota(jnp.int32, bias.shape, 1)
        o_ref[1, hs] = jnp.where(col >= DIL_L, bias, -jnp.inf)


def _relbias(table):
    L = DIL_L
    i = np.arange(L)[:, None]
    j = np.arange(2 * L)[None, :]
    m = L + i - j
    valid = ((m >= 0) & (m <= L)).astype(np.int32)
    buckets = np.stack([_t5_bucket(np.clip(m, 0, None) * d) for _, d in DIL_GROUPS])
    valids = np.stack([valid] * N_DIL_GROUPS)
    return pl.pallas_call(
        _relbias_kernel,
        grid=(N_DIL_GROUPS,),
        in_specs=[pl.BlockSpec(memory_space=pltpu.SMEM),
                  pl.BlockSpec((1, L, 2 * L), lambda g: (g, 0, 0)),
                  pl.BlockSpec((1, L, 2 * L), lambda g: (g, 0, 0))],
        out_specs=pl.BlockSpec((2, DIL_HEADS_PER_GROUP, L, 2 * L), lambda g: (0, g, 0, 0)),
        out_shape=jax.ShapeDtypeStruct((2, DIL_HEADS, L, 2 * L), F32),
        name="rel_bias",
    )(table, jnp.asarray(buckets), jnp.asarray(valids))


def _dil_rows(start, d):
    return pl.ds(start, DIL_L) if d == 1 else pl.ds(start, DIL_L, stride=d)


def _dil_block(qkv_ref, bias_ref, m_scr, l_scr, acc_scr, g, d, r, n):
    L = DIL_L
    lane = lax.broadcasted_iota(jnp.int32, (1, LANES), 1)
    first = 1 - jnp.minimum(n, 1)
    cur = _dil_rows(r + d * (n * L), d)
    prev = _dil_rows(r + d * (jnp.maximum(n - 1, 0) * L), d)
    v_cur = _unpack_bf16_pair(qkv_ref[0, 3 * g + 2, cur, :])
    v_prev = _unpack_bf16_pair(qkv_ref[0, 3 * g + 2, prev, :])
    for pr in range(2):
        qt, k_cur = _unpack_bf16_pair(qkv_ref[0, 3 * g + pr, cur, :])
        _, k_prev = _unpack_bf16_pair(qkv_ref[0, 3 * g + pr, prev, :])
        kt = jnp.concatenate([k_prev, k_cur], axis=0)
        vt = jnp.concatenate([v_prev[pr], v_cur[pr]], axis=0)
        ms, ls, accs = [], [], []
        for hh in range(2):
            hsel = (lane >= HEAD_DIM) == bool(hh)
            qm = jnp.where(hsel, qt, jnp.zeros_like(qt))
            s = _dot_nt(qm, kt) + bias_ref[first, DIL_HEADS_PER_GROUP * g + 2 * pr + hh]
            m = jnp.max(s, axis=-1, keepdims=True)
            p = jnp.exp2(s - m)
            ms.append(m)
            ls.append(jnp.sum(p, axis=-1, keepdims=True))
            accs.append(_dot(p.astype(BF16), vt))
        low = lane < HEAD_DIM
        m_b = jnp.where(low, ms[0], ms[1])
        l_b = jnp.where(low, ls[0], ls[1])
        acc_b = jnp.where(low, accs[0], accs[1])
        if g == 0:
            m_scr[pr, cur, :] = m_b
            l_scr[pr, cur, :] = l_b
            acc_scr[pr, cur, :] = acc_b
        else:
            m_o = m_scr[pr, cur, :]
            m_n = jnp.maximum(m_o, m_b)
            a_o = jnp.exp2(m_o - m_n)
            a_b = jnp.exp2(m_b - m_n)
            m_scr[pr, cur, :] = m_n
            l_scr[pr, cur, :] = l_scr[pr, cur, :] * a_o + l_b * a_b
            acc_scr[pr, cur, :] = acc_scr[pr, cur, :] * a_o + acc_b * a_b


def _dil_kernel(qkv_ref, bias_ref, o_ref, m_scr, l_scr, acc_scr):
    S = o_ref.shape[1]
    for g, (window, d) in enumerate(DIL_GROUPS):
        nb = S // window

        def body(it, carry, g=g, d=d, nb=nb):
            _dil_block(qkv_ref, bias_ref, m_scr, l_scr, acc_scr, g, d, it // nb, it % nb)
            return carry
        lax.fori_loop(0, d * nb, body, 0, unroll=4)
    for pr in range(2):
        o_ref[0, :, pr * LANES:(pr + 1) * LANES] = (acc_scr[pr] / l_scr[pr]).astype(BF16)


def _dil(slabs, bias):
    B, _, S, _ = slabs.shape
    for window, d in DIL_GROUPS:
        assert window // d == DIL_L and S % window == 0
    stat = pltpu.VMEM((2, S, LANES), F32)
    return pl.pallas_call(
        _dil_kernel,
        grid=(B,),
        in_specs=[pl.BlockSpec((1, N_SLABS, S, LANES), lambda b: (b, 0, 0, 0)),
                  pl.BlockSpec(bias.shape, lambda b: (0, 0, 0, 0))],
        out_specs=pl.BlockSpec((1, S, DIL_OUT_WIDTH), lambda b: (b, 0, 0)),
        out_shape=jax.ShapeDtypeStruct((B, S, DIL_OUT_WIDTH), BF16),
        scratch_shapes=[stat, stat, stat],
        compiler_params=pltpu.CompilerParams(vmem_limit_bytes=VMEM_LIMIT),
        name="dil_attn",
    )(slabs, bias)


def _outproj_kernel(x_ref, ya_ref, yd_ref, ga_ref, gb_ref,
                    g1_ref, sc_ref, sh_ref, ng_ref, wa_ref, wb_ref, wo_ref, wr_ref, br_ref,
                    x1_ref, h2_ref, lg_ref):
    n_chunks = 2
    cm = x_ref.shape[0] // n_chunks
    for c in range(n_chunks):
        rows = slice(c * cm, (c + 1) * cm)
        a = _dot(ya_ref[rows, :], wa_ref[...])
        bm = _dot(yd_ref[rows, :], wb_ref[...])
        merged = ga_ref[rows, :].astype(F32) * a + gb_ref[rows, :].astype(F32) * bm
        out = _dot(merged.astype(BF16), wo_ref[...])
        x1 = x_ref[rows, :] + g1_ref[0] * out
        x1_ref[rows, :] = x1
        ms = jnp.mean(x1 * x1, axis=-1, keepdims=True)
        h = x1 * lax.rsqrt(ms + EPS) * ng_ref[...]
        h = h * (1.0 + sc_ref[0]) + sh_ref[0]
        hb = h.astype(BF16)
        h2_ref[rows, :] = hb
        lg_ref[rows, :] = _dot(hb, wr_ref[...]) + br_ref[...]


def _outproj(x2d, ya2d, yd2d, p2d, g1, sc2, sh2, norm_g, wa, wb, wo, wr, br, S):
    T = x2d.shape[0]
    tm = TM_PROJ
    per_b = S // tm
    row = lambda w: pl.BlockSpec((tm, w), lambda i: (i, 0))
    full = lambda a: pl.BlockSpec(a.shape, lambda i: (0,) * a.ndim)
    mod = pl.BlockSpec((1, 1, D_MODEL), lambda i: (i // per_b, 0, 0))
    return pl.pallas_call(
        _outproj_kernel,
        grid=(T // tm,),
        in_specs=[row(D_MODEL), row(FOX_WIDTH), row(DIL_OUT_WIDTH)]
                 + [pl.BlockSpec((tm, D_MODEL), lambda i: (i, U_GATE_A * UNIT // D_MODEL)),
                    pl.BlockSpec((tm, D_MODEL), lambda i: (i, U_GATE_B * UNIT // D_MODEL)),
                    mod, mod, mod, full(norm_g), full(wa), full(wb), full(wo), full(wr), full(br)],
        out_specs=[row(D_MODEL), row(D_MODEL), row(LANES)],
        out_shape=[jax.ShapeDtypeStruct((T, D_MODEL), F32),
                   jax.ShapeDtypeStruct((T, D_MODEL), BF16),
                   jax.ShapeDtypeStruct((T, LANES), F32)],
        compiler_params=pltpu.CompilerParams(vmem_limit_bytes=VMEM_LIMIT),
        name="out_proj",
    )(x2d, ya2d, yd2d, p2d, p2d, g1, sc2, sh2, norm_g, wa, wb, wo, wr, br)


def _dispatch_kernel(lg_ref, h_ref, tri_ref, xs_ref, cnt_ref, pos_ref):
    tt = lg_ref.shape[0]
    lt = lg_ref[...].T
    row = lambda i: lt[i:i + 1, :]
    neg = -jnp.inf
    g = [row(i) for i in range(N_GROUPS)]
    gmax = functools.reduce(jnp.maximum, g)
    gidx = jnp.full(gmax.shape, N_GROUPS - 1, jnp.int32)
    for i in reversed(range(N_GROUPS - 1)):
        gidx = jnp.where(g[i] == gmax, i, gidx)
    gsum = sum(jnp.exp(gi - gmax) for gi in g)
    el = []
    for j in range(EXPERTS_PER_GROUP):
        v = row(N_GROUPS + EXPERTS_PER_GROUP * (N_GROUPS - 1) + j)
        for gg in reversed(range(N_GROUPS - 1)):
            v = jnp.where(gidx == gg, row(N_GROUPS + EXPERTS_PER_GROUP * gg + j), v)
        el.append(v)

    def top(vals):
        best = functools.reduce(jnp.maximum, vals)
        idx = jnp.full(best.shape, EXPERTS_PER_GROUP - 1, jnp.int32)
        for j in reversed(range(EXPERTS_PER_GROUP - 1)):
            idx = jnp.where(vals[j] == best, j, idx)
        return best, idx

    v1, i1 = top(el)
    v2, i2 = top([jnp.where(i1 == j, neg, el[j]) for j in range(EXPERTS_PER_GROUP)])
    t = jnp.exp(v2 - v1)
    den = (1.0 + t) * gsum
    wts = [1.0 / den, t / den]
    eid = [gidx * EXPERTS_PER_GROUP + i1, gidx * EXPERTS_PER_GROUP + i2]

    esub = lax.broadcasted_iota(jnp.int32, (N_EXPERTS, tt), 0)
    ohf = jnp.concatenate([jnp.where(esub == eid[k], 1.0, 0.0) for k in range(2)], axis=1)
    res = _dot(ohf.astype(BF16), tri_ref[...])
    prefix, cnt = res[:, :2 * tt], res[:, 2 * tt:]
    cnt = (((cnt.astype(jnp.int32) + (SUBLANES - 1)) // SUBLANES) * SUBLANES).astype(F32)
    esub_c = lax.broadcasted_iota(jnp.int32, cnt.shape, 0)
    start = jnp.zeros_like(cnt)
    for e in range(N_EXPERTS - 1):
        start = start + jnp.where(esub_c > e, cnt[e:e + 1, :], 0.0)
    start_w = jnp.concatenate([start] * (2 * tt // LANES), axis=1)
    pos = jnp.sum(ohf * (start_w + prefix), axis=0, keepdims=True)
    pos_k = [pos[:, :tt], pos[:, tt:]]

    n_rows = xs_ref.shape[0]
    psub = lax.broadcasted_iota(jnp.int32, (n_rows, tt), 0).astype(F32)
    pm = [jnp.where(psub == pos_k[k], 1.0, 0.0).astype(BF16) for k in range(2)]
    xs = _dot(pm[0] + pm[1], h_ref[...])
    wsub = lax.broadcasted_iota(jnp.int32, (LANES, tt), 0)
    ws = jnp.zeros((n_rows, LANES), F32)
    for k in range(2):
        parts = _split3(wts[k])
        wrows = jnp.zeros((LANES, tt), F32)
        for j in range(3):
            wrows = jnp.where(wsub == j, parts[j].astype(F32), wrows)
        ws = ws + _dot_nt(pm[k], wrows.astype(BF16))
    half = D_MODEL // 2
    xs_ref[:, :half] = _pack_bf16_pair(xs[:, :half], xs[:, half:])
    xs_ref[:, half:] = pltpu.bitcast(ws, jnp.uint32)
    cnt_ref[0] = cnt.astype(jnp.int32)
    posr = jnp.where(wsub == 0, pos_k[0], jnp.where(wsub == 1, pos_k[1], 0.0))
    pos_ref[...] = posr.T


def _dispatch(logits, h2):
    T = logits.shape[0]
    tt = MOE_TILE
    n_tiles = T // tt
    tri = np.concatenate([np.triu(np.ones((2 * tt, 2 * tt)), 1), np.ones((2 * tt, LANES))], axis=1)
    return pl.pallas_call(
        _dispatch_kernel,
        grid=(n_tiles,),
        in_specs=[pl.BlockSpec((tt, LANES), lambda i: (i, 0)),
                  pl.BlockSpec((tt, D_MODEL), lambda i: (i, 0)),
                  pl.BlockSpec(tri.shape, lambda i: (0, 0))],
        out_specs=[pl.BlockSpec((TILE_ROWS, XS_WIDTH), lambda i: (i, 0)),
                   pl.BlockSpec((1, N_EXPERTS, LANES), lambda i: (i, 0, 0)),
                   pl.BlockSpec((tt, LANES), lambda i: (i, 0))],
        out_shape=[jax.ShapeDtypeStruct((n_tiles * TILE_ROWS, XS_WIDTH), jnp.uint32),
                   jax.ShapeDtypeStruct((n_tiles, N_EXPERTS, LANES), jnp.int32),
                   jax.ShapeDtypeStruct((T, LANES), F32)],
        compiler_params=pltpu.CompilerParams(vmem_limit_bytes=VMEM_LIMIT),
        name="moe_dispatch",
    )(logits, h2, jnp.asarray(tri, BF16))


def _plan_kernel(cnt_ref, be_ref, nv_ref, nxt_ref, grp_ref, used_ref, cs_ref):
    n_tiles = cnt_ref.shape[0]
    n_blk = be_ref.shape[0]
    rows = MOE_ROWS
    row_shift = rows.bit_length() - 1
    grp_shift = SUBLANES.bit_length() - 1
    assert rows == 1 << row_shift and SUBLANES == 1 << grp_shift

    def tile_starts(t, c):
        def per_e(e, acc):
            cs_ref[t * N_EXPERTS + e] = acc
            return acc + cnt_ref[t, e]
        used_ref[t] = lax.fori_loop(0, N_EXPERTS, per_e, 0, unroll=8)
        return c
    lax.fori_loop(0, n_tiles, tile_starts, 0)

    def clear(b, c):
        nv_ref[b] = 0
        return c
    lax.fori_loop(0, n_blk, clear, 0)

    def clear_groups(g, c):
        grp_ref[g] = 0
        return c
    lax.fori_loop(0, n_blk * MOE_GROUPS, clear_groups, 0, unroll=8)

    def per_expert(e, b):
        g0 = b * MOE_GROUPS

        def per_tile(t, tot):
            c = cnt_ref[t, e]
            src = t * TILE_ROWS + cs_ref[t * N_EXPERTS + e]
            first = g0 + lax.shift_right_logical(tot, grp_shift)

            def per_group(k, cc):
                grp_ref[first + k] = src + k * SUBLANES
                return cc
            lax.fori_loop(0, lax.shift_right_logical(c, grp_shift), per_group, 0)
            return tot + c
        tot = lax.fori_loop(0, n_tiles, per_tile, 0)

        def per_block(j, c):
            be_ref[b + j] = e
            nv_ref[b + j] = jnp.minimum(rows, tot - j * rows)
            return c
        nb = lax.shift_right_logical(tot + rows - 1, row_shift)
        lax.fori_loop(0, nb, per_block, 0)
        return b + nb
    n_used = lax.fori_loop(0, N_EXPERTS, per_expert, 0)

    def unused(b, c):
        be_ref[b] = be_ref[n_used - 1]
        nxt_ref[b] = -1
        return c
    lax.fori_loop(n_used, n_blk, unused, 0)

    def next_run(k, nf):
        b = n_used - 1 - k
        nf = jnp.where(be_ref[b] != be_ref[jnp.minimum(b + 1, n_used - 1)], b + 1, nf)
        nxt_ref[b] = nf
        return nf
    lax.fori_loop(0, n_used, next_run, -1)


def _plan(cnt, n_blk):
    n_tiles = cnt.shape[0]
    smem = pl.BlockSpec(memory_space=pltpu.SMEM)
    i32 = lambda n: jax.ShapeDtypeStruct((n,), jnp.int32)
    return pl.pallas_call(
        _plan_kernel,
        in_specs=[smem],
        out_specs=[smem] * 5,
        out_shape=[i32(n_blk), i32(n_blk), i32(n_blk), i32(n_blk * MOE_GROUPS), i32(n_tiles)],
        scratch_shapes=[pltpu.SMEM((n_tiles * N_EXPERTS,), jnp.int32)],
        name="moe_plan",
    )(cnt)


def _pow2_pieces(n, fn):
    for b in reversed(range(SUBLANES.bit_length() - 1, MOE_ROWS.bit_length())):
        size = 1 << b

        @pl.when((n & size) != 0)
        def _():
            fn((n >> (b + 1)) << (b + 1), size)


def _moe_kernel(be_ref, nv_ref, nxt_ref, grp_ref, used_ref,
                w1_hbm, w3_hbm, w2_hbm, xs_hbm, ys_hbm,
                xbuf, ybuf, wb1, wb3, wb2, wst1, wst3, wst2, wslot, gsem, ssem, wsem):
    i = pl.program_id(0)
    last = pl.num_programs(0) - 1
    slot = i % 2
    nv = nv_ref[i]
    half = D_MODEL // 2
    grp_shift = SUBLANES.bit_length() - 1

    def group_row(blk, g):
        return pl.multiple_of(grp_ref[blk * MOE_GROUPS + g], SUBLANES)

    def gather(blk, s):
        top = jnp.maximum(lax.shift_right_logical(nv_ref[blk], grp_shift) - 1, 0)
        for g in range(MOE_GROUPS):
            src = group_row(blk, jnp.minimum(g, top))
            pltpu.make_async_copy(xs_hbm.at[pl.ds(src, SUBLANES)],
                                  xbuf.at[s, pl.ds(g * SUBLANES, SUBLANES)], gsem.at[s]).start()

    def wait_gather(s):
        pltpu.make_async_copy(xs_hbm.at[pl.ds(0, MOE_ROWS)], xbuf.at[s], gsem.at[s]).wait()

    def scatter(blk, s):
        def body(g, c):
            r = pl.multiple_of(g * SUBLANES, SUBLANES)
            pltpu.make_async_copy(ybuf.at[s, pl.ds(r, SUBLANES)],
                                  ys_hbm.at[pl.ds(group_row(blk, g), SUBLANES)], ssem.at[s]).start()
            return c
        lax.fori_loop(0, lax.shift_right_logical(nv_ref[blk], grp_shift), body, 0)

    def wait_scatter(s, count):
        _pow2_pieces(count, lambda a, size: pltpu.make_async_copy(
            ybuf.at[s, pl.ds(0, size)], ys_hbm.at[pl.ds(0, size)], ssem.at[s]).wait())

    @pl.when(i == 0)
    def _():
        @pl.when(nv > 0)
        def _():
            gather(0, 0)
            gather(jnp.minimum(1, last), 1)

        ybuf[1] = jnp.zeros(ybuf.shape[1:], ybuf.dtype)
        n_tiles = used_ref.shape[0]

        def fill(t, c):
            row0 = t * TILE_ROWS + used_ref[t]
            _pow2_pieces(TILE_ROWS - used_ref[t], lambda a, size: pltpu.make_async_copy(
                ybuf.at[1, pl.ds(0, size)], ys_hbm.at[pl.ds(pl.multiple_of(row0 + a, SUBLANES), size)],
                ssem.at[1]).start())
            return c
        lax.fori_loop(0, n_tiles, fill, 0)

        def drain(t, c):
            wait_scatter(1, TILE_ROWS - used_ref[t])
            return c
        lax.fori_loop(0, n_tiles, drain, 0)

    @pl.when(i >= 2)
    def _():
        wait_scatter(slot, nv_ref[jnp.maximum(i - 2, 0)])

    xslot = i % GATHER_BUFS
    issuer_used = jnp.where(i >= 2, nv_ref[jnp.maximum(i - 2, 0)], nv_ref[0]) > 0

    @pl.when((nv == 0) & (i > 0) & issuer_used)
    def _():
        wait_gather(xslot)

    @pl.when(nv > 0)
    def _():
        e = be_ref[i]
        e_prev = be_ref[jnp.maximum(i - 1, 0)]

        def weight_copies(ex, ws):
            return [pltpu.make_async_copy(src.at[ex], dst.at[ws], wsem.at[ws])
                    for src, dst in ((w1_hbm, wst1), (w3_hbm, wst3), (w2_hbm, wst2))]

        @pl.when(i == 0)
        def _():
            wslot[0] = 0
            for cp in weight_copies(e, 0):
                cp.start()

        @pl.when((i == 0) | (e != e_prev))
        def _():
            ws = wslot[0]
            for cp in weight_copies(e, ws):
                cp.wait()
            wb1[...] = wst1[ws].astype(BF16)
            wb3[...] = wst3[ws].astype(BF16)
            wb2[...] = wst2[ws].astype(BF16)
            nb = nxt_ref[i]

            @pl.when(nb >= 0)
            def _():
                for cp in weight_copies(be_ref[jnp.maximum(nb, 0)], 1 - ws):
                    cp.start()
            wslot[0] = 1 - ws

        wait_gather(xslot)
        gather(jnp.minimum(i + 2, last), (i + 2) % GATHER_BUFS)
        u = xbuf[xslot]
        xa, xb = _unpack_bf16_pair(u[:, :half])
        wv = pltpu.bitcast(u[:, half:], F32)
        roww = wv[:, 0:1] + wv[:, 1:2] + wv[:, 2:3]
        a = _dot(xa, wb1[:half, :]) + _dot(xb, wb1[half:, :])
        b = _dot(xa, wb3[:half, :]) + _dot(xb, wb3[half:, :])
        hmid = (a * jax.nn.sigmoid(a) * b).astype(BF16)
        y = _dot(hmid, wb2[...]) * roww
        ybuf[slot] = _pack_bf16_pair(y[:, :half], y[:, half:])
        scatter(i, slot)

    @pl.when(i == last)
    def _():
        @pl.when((last >= 1) & (nv_ref[jnp.maximum(last - 1, 0)] > 0))
        def _():
            wait_gather((last + 1) % GATHER_BUFS)

        @pl.when(nv > 0)
        def _():
            wait_gather((last + 2) % GATHER_BUFS)

        @pl.when(last >= 1)
        def _():
            wait_scatter(1 - slot, nv_ref[jnp.maximum(last - 1, 0)])
        wait_scatter(slot, nv)


def _moe(xs, plan, w1, w3, w2):
    n_blk = plan[0].shape[0]
    rows = MOE_ROWS
    half = D_MODEL // 2
    hbm = pl.BlockSpec(memory_space=pl.ANY)
    grid_spec = pltpu.PrefetchScalarGridSpec(
        num_scalar_prefetch=5,
        grid=(n_blk,),
        in_specs=[hbm] * 4,
        out_specs=hbm,
        scratch_shapes=[pltpu.VMEM((GATHER_BUFS, rows, XS_WIDTH), jnp.uint32),
                        pltpu.VMEM((2, rows, half), jnp.uint32),
                        pltpu.VMEM((D_MODEL, EXPERT_HIDDEN), BF16),
                        pltpu.VMEM((D_MODEL, EXPERT_HIDDEN), BF16),
                        pltpu.VMEM((EXPERT_HIDDEN, D_MODEL), BF16),
                        pltpu.VMEM((2, D_MODEL, EXPERT_HIDDEN), F32),
                        pltpu.VMEM((2, D_MODEL, EXPERT_HIDDEN), F32),
                        pltpu.VMEM((2, EXPERT_HIDDEN, D_MODEL), F32),
                        pltpu.SMEM((1,), jnp.int32),
                        pltpu.SemaphoreType.DMA((GATHER_BUFS,)),
                        pltpu.SemaphoreType.DMA((2,)),
                        pltpu.SemaphoreType.DMA((2,))])
    return pl.pallas_call(
        _moe_kernel,
        grid_spec=grid_spec,
        out_shape=jax.ShapeDtypeStruct((xs.shape[0], half), jnp.uint32),
        compiler_params=pltpu.CompilerParams(dimension_semantics=("arbitrary",),
                                             vmem_limit_bytes=VMEM_LIMIT),
        name="moe_ffn",
    )(*plan, w1, w3, w2, xs)


def _combine_kernel(x1_ref, ys_ref, pos_ref, g2_ref, o_ref):
    tt = x1_ref.shape[0]
    half = D_MODEL // 2
    pos = pos_ref[...]
    pcol = lax.broadcasted_iota(jnp.int32, (tt, ys_ref.shape[0]), 1).astype(F32)
    sel = (jnp.where(pcol == pos[:, 0:1], 1.0, 0.0) + jnp.where(pcol == pos[:, 1:2], 1.0, 0.0))
    sel = sel.astype(BF16)
    lo, hi = _unpack_bf16_pair(ys_ref[...])
    g2 = g2_ref[0]
    x1 = x1_ref[...]
    o_ref[:, :half] = x1[:, :half] + g2[:, :half] * _dot(sel, lo)
    o_ref[:, half:] = x1[:, half:] + g2[:, half:] * _dot(sel, hi)


def _combine(x1, ys, pos, g2, S):
    T = x1.shape[0]
    tt = MOE_TILE
    per_b = S // tt
    return pl.pallas_call(
        _combine_kernel,
        grid=(T // tt,),
        in_specs=[pl.BlockSpec((tt, D_MODEL), lambda i: (i, 0)),
                  pl.BlockSpec((TILE_ROWS, D_MODEL // 2), lambda i: (i, 0)),
                  pl.BlockSpec((tt, LANES), lambda i: (i, 0)),
                  pl.BlockSpec((1, 1, D_MODEL), lambda i: (i // per_b, 0, 0))],
        out_specs=pl.BlockSpec((tt, D_MODEL), lambda i: (i, 0)),
        out_shape=jax.ShapeDtypeStruct((T, D_MODEL), F32),
        compiler_params=pltpu.CompilerParams(vmem_limit_bytes=VMEM_LIMIT),
        name="moe_combine",
    )(x1, ys, pos, g2)


def _prep_w_in(w_in):
    dq = w_in[:, OFF_DIL_Q:OFF_DIL_K]
    dk = w_in[:, OFF_DIL_K:OFF_DIL_V]
    dv = w_in[:, OFF_DIL_V:OFF_GATE_A]
    dil = []
    for g in range(N_DIL_GROUPS):
        p0 = slice(g * DIL_OUT_WIDTH, g * DIL_OUT_WIDTH + LANES)
        p1 = slice(g * DIL_OUT_WIDTH + LANES, (g + 1) * DIL_OUT_WIDTH)
        dil += [dq[:, p0], dk[:, p0], dq[:, p1], dk[:, p1], dv[:, p0], dv[:, p1]]
    pad = jnp.zeros((D_MODEL, UNIT - FOX_HEADS), w_in.dtype)
    cols = [w_in[:, OFF_GATE_A:OFF_GATE_B], w_in[:, OFF_GATE_B:N_IN],
            w_in[:, OFF_FOX_Q:OFF_FOX_K], w_in[:, OFF_FOX_K:OFF_FOX_V], w_in[:, OFF_FOX_V:OFF_FOX_F],
            *dil, w_in[:, OFF_FOX_F:OFF_DIL_Q], pad]
    return jnp.concatenate(cols, axis=1).astype(BF16)


def _prep_gain(q_gain, k_gain):
    qs = HEAD_DIM ** -0.5 * LOG2E
    ones = jnp.ones((UNIT,), F32)
    fq = q_gain[:FOX_HEADS].reshape(-1) * qs
    fk = k_gain[:FOX_HEADS].reshape(-1)
    dq = q_gain[FOX_HEADS:].reshape(-1) * qs
    dk = k_gain[FOX_HEADS:].reshape(-1)
    dil = []
    for g in range(N_DIL_GROUPS):
        p0 = slice(g * DIL_OUT_WIDTH, g * DIL_OUT_WIDTH + LANES)
        p1 = slice(g * DIL_OUT_WIDTH + LANES, (g + 1) * DIL_OUT_WIDTH)
        dil += [dq[p0], dk[p0], dq[p1], dk[p1], ones]
    parts = [ones] * 8 + [fq, fk, ones, ones] + dil + [ones]
    return jnp.concatenate(parts).reshape(1, N_UNITS * UNIT)


def _layer(x, mod, rel_bias_table, norm1_g, w_in, b_forget, q_gain, k_gain, w_branch_a, w_branch_b,
           w_out, norm2_g, w_rg, b_rg, w_re, b_re, w1, w3, w2):
    B, S, D = x.shape
    T = B * S
    sh1, sc1, g1, sh2, sc2, g2 = [m.reshape(B, 1, D) for m in jnp.split(mod, 6, axis=-1)]
    x2d = x.reshape(T, D)

    p2d, fgt, slabs = _inproj(x2d, norm1_g.reshape(1, D), sc1, sh1, _prep_w_in(w_in),
                              _prep_gain(q_gain, k_gain), S)
    p3 = p2d.reshape(B, S, P_WIDTH)
    ck = _fcum(fgt.reshape(B, S, LANES), b_forget)
    ya = _fox(p3, ck)
    yd = _dil(slabs, _relbias(rel_bias_table))

    n_router = N_GROUPS + N_EXPERTS
    wr = jnp.concatenate([w_rg, w_re, jnp.zeros((D, LANES - n_router), F32)], axis=1).astype(BF16)
    br = jnp.concatenate([b_rg, b_re, jnp.zeros((LANES - n_router,), F32)]).reshape(1, LANES)
    x1, h2, logits = _outproj(x2d, ya.reshape(T, FOX_WIDTH), yd.reshape(T, DIL_OUT_WIDTH), p2d,
                              g1, sc2, sh2, norm2_g.reshape(1, D),
                              w_branch_a.astype(BF16), w_branch_b.astype(BF16), w_out.astype(BF16),
                              wr, br, S)
    xs, cnt, pos = _dispatch(logits, h2)
    cnt2 = cnt[:, :, 0]
    n_blk = cnt.shape[0] * TILE_ROWS // MOE_ROWS + N_EXPERTS
    plan = _plan(cnt2, n_blk)
    ys = _moe(xs, plan, w1, w3, w2)
    out = _combine(x1, ys, pos, g2, S)
    return out.reshape(B, S, D)


def kernel(x, c, rel_bias_table, w_ada, b_ada, norm1_g, w_in, b_forget, q_gain, k_gain, w_branch_a, w_branch_b, w_out, norm2_g, w_router_group, b_router_group, w_router_expert, b_router_expert, w1, w3, w2):
    depth = w_ada.shape[0]
    for l in range(depth):
        mod = _ada(c, w_ada[l], b_ada[l])
        x = _layer(x, mod, rel_bias_table, norm1_g[l], w_in[l], b_forget[l], q_gain[l], k_gain[l],
                   w_branch_a[l], w_branch_b[l], w_out[l], norm2_g[l], w_router_group[l],
                   b_router_group[l], w_router_expert[l], b_router_expert[l], w1[l], w3[l], w2[l])
    return x
```

```python
import functools
import math

import numpy as np
import jax
import jax.numpy as jnp
from jax import lax
from jax.experimental import pallas as pl
from jax.experimental.pallas import tpu as pltpu

F32 = jnp.float32
BF16 = jnp.bfloat16

D_MODEL = 1024
HEAD_DIM = 64
FOX_HEADS = 8
DIL_GROUPS = ((128, 1), (512, 4), (2048, 16))
DIL_HEADS_PER_GROUP = 4
N_DIL_GROUPS = len(DIL_GROUPS)
DIL_HEADS = N_DIL_GROUPS * DIL_HEADS_PER_GROUP
FOX_WIDTH = FOX_HEADS * HEAD_DIM
DIL_WIDTH = DIL_HEADS * HEAD_DIM
DIL_OUT_WIDTH = DIL_HEADS_PER_GROUP * HEAD_DIM
NUM_BUCKETS = 32
REL_MAX_DISTANCE = 2048
N_GROUPS = 4
EXPERTS_PER_GROUP = 8
N_EXPERTS = N_GROUPS * EXPERTS_PER_GROUP
EXPERT_HIDDEN = D_MODEL // 2
EPS = 1e-6
LOG2E = math.log2(math.e)

OFF_FOX_Q = 0
OFF_FOX_K = OFF_FOX_Q + FOX_WIDTH
OFF_FOX_V = OFF_FOX_K + FOX_WIDTH
OFF_FOX_F = OFF_FOX_V + FOX_WIDTH
OFF_DIL_Q = OFF_FOX_F + FOX_HEADS
OFF_DIL_K = OFF_DIL_Q + DIL_WIDTH
OFF_DIL_V = OFF_DIL_K + DIL_WIDTH
OFF_GATE_A = OFF_DIL_V + DIL_WIDTH
OFF_GATE_B = OFF_GATE_A + D_MODEL
N_IN = OFF_GATE_B + D_MODEL

LANES = 128
UNIT = 256
DIL_L = 128

U_GATE_A, U_GATE_B, U_FOX_Q, U_FOX_K, U_FOX_V, U_DIL, U_FORGET = 0, 4, 8, 10, 12, 14, 23
N_UNITS = 24
P_WIDTH = U_DIL * UNIT
N_SLABS = 3 * N_DIL_GROUPS
_KIND = (["gate"] * 8 + ["norm"] * 4 + ["plain"] * 2 + ["norm", "norm", "plain"] * 3 + ["forget"])

TM_INPROJ = 1024
TM_PROJ = 1024
TQ_FOX = 512
MOE_ROWS = 256
MOE_TILE = 512
XS_WIDTH = D_MODEL // 2 + LANES
SUBLANES = 8
TILE_ROWS = 2 * MOE_TILE + N_EXPERTS * SUBLANES
MOE_GROUPS = MOE_ROWS // SUBLANES
GATHER_BUFS = 3
VMEM_LIMIT = 56 * 1024 * 1024


def _dot(a, b):
    return jnp.dot(a, b, preferred_element_type=F32)


def _dot_nt(a, b):
    return lax.dot_general(a, b, (((1,), (1,)), ((), ())), preferred_element_type=F32)


def _split3(x):
    hi = x.astype(BF16)
    r1 = x - hi.astype(F32)
    mid = r1.astype(BF16)
    lo = (r1 - mid.astype(F32)).astype(BF16)
    return hi, mid, lo


def _ada_kernel(c_ref, w_ref, b_ref, o_ref):
    c = c_ref[...]
    s = c * jax.nn.sigmoid(c)
    s_hi = s.astype(BF16)
    s_lo = (s - s_hi.astype(F32)).astype(BF16)
    w = w_ref[...]
    w_hi = w.astype(BF16)
    w_lo = (w - w_hi.astype(F32)).astype(BF16)
    acc = _dot(s_hi, w_hi) + _dot(s_hi, w_lo) + _dot(s_lo, w_hi)
    o_ref[...] = acc + b_ref[...]


def _ada(c, w_ada, b_ada):
    B = c.shape[0]
    n_out = w_ada.shape[1]
    tn = 512
    return pl.pallas_call(
        _ada_kernel,
        grid=(n_out // tn,),
        in_specs=[pl.BlockSpec((B, D_MODEL), lambda j: (0, 0)),
                  pl.BlockSpec((D_MODEL, tn), lambda j: (0, j)),
                  pl.BlockSpec((1, tn), lambda j: (0, j))],
        out_specs=pl.BlockSpec((B, tn), lambda j: (0, j)),
        out_shape=jax.ShapeDtypeStruct((B, n_out), F32),
        name="ada_mod",
    )(c, w_ada, b_ada.reshape(1, n_out))


def _pack_bf16_pair(lo, hi):
    lo_bits = pltpu.bitcast(lo.astype(BF16).astype(F32), jnp.uint32) >> 16
    hi_bits = pltpu.bitcast(hi.astype(BF16).astype(F32), jnp.uint32) & jnp.uint32(0xFFFF0000)
    return lo_bits | hi_bits


def _unpack_bf16_pair(u):
    lo = pltpu.bitcast(u << 16, F32).astype(BF16)
    hi = pltpu.bitcast(u & jnp.uint32(0xFFFF0000), F32).astype(BF16)
    return lo, hi


def _inproj_kernel(x_ref, g_ref, sc_ref, sh_ref, w_ref, gain_ref, bd_ref, p_ref, f_ref, s_ref):
    x = x_ref[...]
    ms = jnp.mean(x * x, axis=-1, keepdims=True)
    h = x * lax.rsqrt(ms + EPS) * g_ref[...]
    h = h * (1.0 + sc_ref[0]) + sh_ref[0]
    hb = h.astype(BF16)

    def unit(u):
        cols = slice(u * UNIT, (u + 1) * UNIT)
        acc = _dot(hb, w_ref[:, cols])
        kind = _KIND[u]
        if kind == "gate":
            return jax.nn.sigmoid(acc)
        if kind == "norm":
            ss = _dot((acc * acc).astype(BF16), bd_ref[...])
            return acc * lax.rsqrt(ss * (1.0 / HEAD_DIM) + EPS) * gain_ref[:, cols]
        return acc

    def emit(u):
        o = unit(u)
        if u < U_DIL:
            p_ref[:, u * UNIT:(u + 1) * UNIT] = o.astype(BF16)
        elif u < U_FORGET:
            s_ref[0, u - U_DIL] = _pack_bf16_pair(o[:, :LANES], o[:, LANES:])
        else:
            f_ref[...] = o[:, :LANES]

    normed = [u for u in range(N_UNITS) if _KIND[u] == "norm"]
    others = [u for u in range(N_UNITS) if _KIND[u] != "norm"]
    while normed or others:
        for group in (normed, others):
            if group:
                emit(group.pop(0))


def _inproj(x2d, norm_g, sc, sh, w_re, gain_row, S):
    T = x2d.shape[0]
    tm = TM_INPROJ
    per_b = S // tm
    bd = np.kron(np.eye(UNIT // HEAD_DIM), np.ones((HEAD_DIM, HEAD_DIM))).astype(np.float32)
    once = dict(pipeline_mode=pl.Buffered(1))
    return pl.pallas_call(
        _inproj_kernel,
        grid=(T // tm,),
        in_specs=[pl.BlockSpec((tm, D_MODEL), lambda i: (i, 0)),
                  pl.BlockSpec((1, D_MODEL), lambda i: (0, 0)),
                  pl.BlockSpec((1, 1, D_MODEL), lambda i: (i // per_b, 0, 0)),
                  pl.BlockSpec((1, 1, D_MODEL), lambda i: (i // per_b, 0, 0)),
                  pl.BlockSpec((D_MODEL, N_UNITS * UNIT), lambda i: (0, 0), **once),
                  pl.BlockSpec((1, N_UNITS * UNIT), lambda i: (0, 0), **once),
                  pl.BlockSpec((UNIT, UNIT), lambda i: (0, 0), **once)],
        out_specs=[pl.BlockSpec((tm, P_WIDTH), lambda i: (i, 0)),
                   pl.BlockSpec((tm, LANES), lambda i: (i, 0)),
                   pl.BlockSpec((1, N_SLABS, tm, LANES), lambda i: (i // per_b, 0, i % per_b, 0))],
        out_shape=[jax.ShapeDtypeStruct((T, P_WIDTH), BF16),
                   jax.ShapeDtypeStruct((T, LANES), F32),
                   jax.ShapeDtypeStruct((T // S, N_SLABS, S, LANES), jnp.uint32)],
        compiler_params=pltpu.CompilerParams(vmem_limit_bytes=VMEM_LIMIT),
        name="in_proj",
    )(x2d, norm_g, sc, sh, w_re, gain_row, jnp.asarray(bd, BF16))


def _fcum_kernel(f_ref, b_ref, tri_ref, o_ref):
    S = f_ref.shape[1]
    xf = f_ref[0] + b_ref[...]
    ls = (jnp.minimum(xf, 0.0) - jnp.log(1.0 + jnp.exp(-jnp.abs(xf)))) * LOG2E
    lst = ls.T
    carry = jnp.zeros((LANES, UNIT), F32)
    for blk in range(S // UNIT):
        seg = lst[:, blk * UNIT:(blk + 1) * UNIT]
        hi, mid, lo = _split3(seg)
        tri = tri_ref[...]
        res = _dot(hi, tri) + _dot(mid, tri) + _dot(lo, tri)
        o_ref[0, :, blk * UNIT:(blk + 1) * UNIT] = (res[:, :UNIT] + carry)[:FOX_HEADS]
        carry = carry + res[:, UNIT:]


def _fcum(fgt, b_forget):
    B, S, _ = fgt.shape
    brow = jnp.zeros((1, LANES), F32).at[0, :FOX_HEADS].set(b_forget)
    tri = np.concatenate([np.triu(np.ones((UNIT, UNIT))), np.ones((UNIT, UNIT))], axis=1)
    return pl.pallas_call(
        _fcum_kernel,
        grid=(B,),
        in_specs=[pl.BlockSpec((1, S, LANES), lambda b: (b, 0, 0)),
                  pl.BlockSpec((1, LANES), lambda b: (0, 0)),
                  pl.BlockSpec((UNIT, 2 * UNIT), lambda b: (0, 0))],
        out_specs=pl.BlockSpec((1, FOX_HEADS, S), lambda b: (b, 0, 0)),
        out_shape=jax.ShapeDtypeStruct((B, FOX_HEADS, S), F32),
        name="forget_cumsum",
    )(fgt, brow, jnp.asarray(tri, BF16))


def _fox_kernel(q_ref, k_ref, v_ref, ck_ref, o_ref):
    S = q_ref.shape[1]
    pair = pl.program_id(1)
    tq = TQ_FOX
    lane = lax.broadcasted_iota(jnp.int32, (1, LANES), 1)
    row = lax.broadcasted_iota(jnp.int32, (tq, tq), 0)
    col = lax.broadcasted_iota(jnp.int32, (tq, tq), 1)
    causal = col <= row
    cks = [ck_ref[0, pl.ds(2 * pair + hh, 1), :] for hh in range(2)]
    for t in reversed(range(S // tq)):
        r0, r1 = t * tq, (t + 1) * tq
        qt = q_ref[0, r0:r1, :]
        outs = []
        for hh in range(2):
            hsel = (lane >= HEAD_DIM) == bool(hh)
            qm = jnp.where(hsel, qt, jnp.zeros_like(qt))
            ck = cks[hh]
            s = _dot_nt(qm, k_ref[0, :r1, :]) - ck[:, :r1]
            s_d = jnp.where(causal, s[:, r0:], -jnp.inf)
            s = jnp.concatenate([s[:, :r0], s_d], axis=1) if t > 0 else s_d
            m = jnp.max(s, axis=-1, keepdims=True)
            p = jnp.exp2(s - m)
            l = jnp.sum(p, axis=-1, keepdims=True)
            outs.append(_dot(p.astype(BF16), v_ref[0, :r1, :]) / l)
        o_ref[0, r0:r1, :] = jnp.where(lane < HEAD_DIM, outs[0], outs[1]).astype(BF16)


def _fox(p3, ck):
    B, S, _ = p3.shape
    nq, nk, nv = (U_FOX_Q * UNIT // LANES, U_FOX_K * UNIT // LANES, U_FOX_V * UNIT // LANES)
    return pl.pallas_call(
        _fox_kernel,
        grid=(B, FOX_HEADS // 2),
        in_specs=[pl.BlockSpec((1, S, LANES), lambda b, p: (b, 0, nq + p)),
                  pl.BlockSpec((1, S, LANES), lambda b, p: (b, 0, nk + p)),
                  pl.BlockSpec((1, S, LANES), lambda b, p: (b, 0, nv + p)),
                  pl.BlockSpec((1, FOX_HEADS, S), lambda b, p: (b, 0, 0))],
        out_specs=pl.BlockSpec((1, S, LANES), lambda b, p: (b, 0, p)),
        out_shape=jax.ShapeDtypeStruct((B, S, FOX_WIDTH), BF16),
        compiler_params=pltpu.CompilerParams(vmem_limit_bytes=VMEM_LIMIT),
        name="fox_attn",
    )(p3, p3, p3, ck)


def _t5_bucket(dist):
    max_exact = NUM_BUCKETS // 2
    d = np.maximum(dist, 1).astype(np.float32)
    large = max_exact + (np.log(d / max_exact) / np.log(REL_MAX_DISTANCE / max_exact)
                         * (NUM_BUCKETS - max_exact)).astype(np.int32)
    large = np.minimum(large, NUM_BUCKETS - 1)
    return np.where(dist < max_exact, dist, large).astype(np.int32)


def _relbias_kernel(tab_ref, bucket_ref, valid_ref, o_ref):
    g = pl.program_id(0)
    bk = bucket_ref[0]
    vd = valid_ref[0]
    for hs in range(DIL_HEADS_PER_GROUP):
        acc = jnp.zeros(bk.shape, F32)
        for b in range(NUM_BUCKETS):
            acc = jnp.where(bk == b, tab_ref[b, g * DIL_HEADS_PER_GROUP + hs], acc)
        bias = jnp.where(vd != 0, acc * LOG2E, -jnp.inf)
        o_ref[0, hs] = bias
        col = lax.broadcasted_iota(jnp.int32, bias.shape, 1)
        o_ref[1, hs] = jnp.where(col >= DIL_L, bias, -jnp.inf)


def _relbias(table):
    L = DIL_L
    i = np.arange(L)[:, None]
    j = np.arange(2 * L)[None, :]
    m = L + i - j
    valid = ((m >= 0) & (m <= L)).astype(np.int32)
    buckets = np.stack([_t5_bucket(np.clip(m, 0, None) * d) for _, d in DIL_GROUPS])
    valids = np.stack([valid] * N_DIL_GROUPS)
    return pl.pallas_call(
        _relbias_kernel,
        grid=(N_DIL_GROUPS,),
        in_specs=[pl.BlockSpec(memory_space=pltpu.SMEM),
                  pl.BlockSpec((1, L, 2 * L), lambda g: (g, 0, 0)),
                  pl.BlockSpec((1, L, 2 * L), lambda g: (g, 0, 0))],
        out_specs=pl.BlockSpec((2, DIL_HEADS_PER_GROUP, L, 2 * L), lambda g: (0, g, 0, 0)),
        out_shape=jax.ShapeDtypeStruct((2, DIL_HEADS, L, 2 * L), F32),
        name="rel_bias",
    )(table, jnp.asarray(buckets), jnp.asarray(valids))


def _dil_rows(start, d):
    return pl.ds(start, DIL_L) if d == 1 else pl.ds(start, DIL_L, stride=d)


def _dil_block_rows(d, nb, it):
    r, n = it // nb, it % nb
    cur = _dil_rows(r + d * (n * DIL_L), d)
    prev = _dil_rows(r + d * (jnp.maximum(n - 1, 0) * DIL_L), d)
    return cur, prev, 1 - jnp.minimum(n, 1)


def _dil_scores(qkv_ref, bias_ref, s_scr, slot, g, d, nb, it):
    lane = lax.broadcasted_iota(jnp.int32, (1, LANES), 1)
    cur, prev, first = _dil_block_rows(d, nb, it)
    for pr in range(2):
        qt, k_cur = _unpack_bf16_pair(qkv_ref[0, 3 * g + pr, cur, :])
        _, k_prev = _unpack_bf16_pair(qkv_ref[0, 3 * g + pr, prev, :])
        kt = jnp.concatenate([k_prev, k_cur], axis=0)
        for hh in range(2):
            hsel = (lane >= HEAD_DIM) == bool(hh)
            qm = jnp.where(hsel, qt, jnp.zeros_like(qt))
            head = 2 * pr + hh
            s_scr[slot, head] = _dot_nt(qm, kt) + bias_ref[first, DIL_HEADS_PER_GROUP * g + head]


def _dil_merge(qkv_ref, s_scr, slot, m_scr, l_scr, acc_scr, g, d, nb, init, it):
    lane = lax.broadcasted_iota(jnp.int32, (1, LANES), 1)
    cur, prev, _ = _dil_block_rows(d, nb, it)
    v_cur = _unpack_bf16_pair(qkv_ref[0, 3 * g + 2, cur, :])
    v_prev = _unpack_bf16_pair(qkv_ref[0, 3 * g + 2, prev, :])
    for pr in range(2):
        vt = jnp.concatenate([v_prev[pr], v_cur[pr]], axis=0)
        ms, ls, accs = [], [], []
        for hh in range(2):
            s = s_scr[slot, 2 * pr + hh]
            m = jnp.max(s, axis=-1, keepdims=True)
            p = jnp.exp2(s - m)
            ms.append(m)
            ls.append(jnp.sum(p, axis=-1, keepdims=True))
            accs.append(_dot(p.astype(BF16), vt))
        low = lane < HEAD_DIM
        m_b = jnp.where(low, ms[0], ms[1])
        l_b = jnp.where(low, ls[0], ls[1])
        acc_b = jnp.where(low, accs[0], accs[1])
        if init:
            m_scr[pr, cur, :] = m_b
            l_scr[pr, cur, :] = l_b
            acc_scr[pr, cur, :] = acc_b
        else:
            m_o = m_scr[pr, cur, :]
            m_n = jnp.maximum(m_o, m_b)
            a_o = jnp.exp2(m_o - m_n)
            a_b = jnp.exp2(m_b - m_n)
            m_scr[pr, cur, :] = m_n
            l_scr[pr, cur, :] = l_scr[pr, cur, :] * a_o + l_b * a_b
            acc_scr[pr, cur, :] = acc_scr[pr, cur, :] * a_o + acc_b * a_b


def _dil_kernel(qkv_ref, bias_ref, o_ref, m_scr, l_scr, acc_scr, s_scr):
    S = o_ref.shape[1]
    order = sorted(range(N_DIL_GROUPS), key=lambda g: -DIL_GROUPS[g][1])
    for g in order:
        window, d = DIL_GROUPS[g]
        nb = S // window
        total = d * nb
        assert total % 2 == 0
        scores = functools.partial(_dil_scores, qkv_ref, bias_ref, s_scr, g=g, d=d, nb=nb)
        merge = functools.partial(_dil_merge, qkv_ref, s_scr, m_scr=m_scr, l_scr=l_scr,
                                  acc_scr=acc_scr, g=g, d=d, nb=nb, init=g == order[0])

        scores(slot=0, it=0)

        def body(j, carry, scores=scores, merge=merge, total=total):
            scores(slot=1, it=2 * j + 1)
            merge(slot=0, it=2 * j)
            scores(slot=0, it=jnp.minimum(2 * j + 2, total - 1))
            merge(slot=1, it=2 * j + 1)
            return carry
        lax.fori_loop(0, total // 2, body, 0, unroll=2)
    for pr in range(2):
        o_ref[0, :, pr * LANES:(pr + 1) * LANES] = (acc_scr[pr] / l_scr[pr]).astype(BF16)


def _dil(slabs, bias):
    B, _, S, _ = slabs.shape
    for window, d in DIL_GROUPS:
        assert window // d == DIL_L and S % window == 0
    stat = pltpu.VMEM((2, S, LANES), F32)
    return pl.pallas_call(
        _dil_kernel,
        grid=(B,),
        in_specs=[pl.BlockSpec((1, N_SLABS, S, LANES), lambda b: (b, 0, 0, 0)),
                  pl.BlockSpec(bias.shape, lambda b: (0, 0, 0, 0))],
        out_specs=pl.BlockSpec((1, S, DIL_OUT_WIDTH), lambda b: (b, 0, 0)),
        out_shape=jax.ShapeDtypeStruct((B, S, DIL_OUT_WIDTH), BF16),
        scratch_shapes=[stat, stat, stat,
                        pltpu.VMEM((2, DIL_HEADS_PER_GROUP, DIL_L, 2 * DIL_L), F32)],
        compiler_params=pltpu.CompilerParams(vmem_limit_bytes=VMEM_LIMIT),
        name="dil_attn",
    )(slabs, bias)


def _outproj_kernel(x_ref, ya_ref, yd_ref, ga_ref, gb_ref,
                    g1_ref, sc_ref, sh_ref, ng_ref, wa_ref, wb_ref, wo_ref, wr_ref, br_ref,
                    x1_ref, h2_ref, lg_ref):
    n_chunks = 2
    cm = x_ref.shape[0] // n_chunks
    for c in range(n_chunks):
        rows = slice(c * cm, (c + 1) * cm)
        a = _dot(ya_ref[rows, :], wa_ref[...])
        bm = _dot(yd_ref[rows, :], wb_ref[...])
        merged = ga_ref[rows, :].astype(F32) * a + gb_ref[rows, :].astype(F32) * bm
        out = _dot(merged.astype(BF16), wo_ref[...])
        x1 = x_ref[rows, :] + g1_ref[0] * out
        x1_ref[rows, :] = x1
        ms = jnp.mean(x1 * x1, axis=-1, keepdims=True)
        h = x1 * lax.rsqrt(ms + EPS) * ng_ref[...]
        h = h * (1.0 + sc_ref[0]) + sh_ref[0]
        hb = h.astype(BF16)
        h2_ref[rows, :] = hb
        lg_ref[rows, :] = _dot(hb, wr_ref[...]) + br_ref[...]


def _outproj(x2d, ya2d, yd2d, p2d, g1, sc2, sh2, norm_g, wa, wb, wo, wr, br, S):
    T = x2d.shape[0]
    tm = TM_PROJ
    per_b = S // tm
    row = lambda w: pl.BlockSpec((tm, w), lambda i: (i, 0))
    full = lambda a: pl.BlockSpec(a.shape, lambda i: (0,) * a.ndim)
    mod = pl.BlockSpec((1, 1, D_MODEL), lambda i: (i // per_b, 0, 0))
    return pl.pallas_call(
        _outproj_kernel,
        grid=(T // tm,),
        in_specs=[row(D_MODEL), row(FOX_WIDTH), row(DIL_OUT_WIDTH)]
                 + [pl.BlockSpec((tm, D_MODEL), lambda i: (i, U_GATE_A * UNIT // D_MODEL)),
                    pl.BlockSpec((tm, D_MODEL), lambda i: (i, U_GATE_B * UNIT // D_MODEL)),
                    mod, mod, mod, full(norm_g), full(wa), full(wb), full(wo), full(wr), full(br)],
        out_specs=[row(D_MODEL), row(D_MODEL), row(LANES)],
        out_shape=[jax.ShapeDtypeStruct((T, D_MODEL), F32),
                   jax.ShapeDtypeStruct((T, D_MODEL), BF16),
                   jax.ShapeDtypeStruct((T, LANES), F32)],
        compiler_params=pltpu.CompilerParams(vmem_limit_bytes=VMEM_LIMIT),
        name="out_proj",
    )(x2d, ya2d, yd2d, p2d, p2d, g1, sc2, sh2, norm_g, wa, wb, wo, wr, br)


def _dispatch_kernel(lg_ref, h_ref, tri_ref, xs_ref, cnt_ref, pos_ref):
    tt = lg_ref.shape[0]
    lt = lg_ref[...].T
    row = lambda i: lt[i:i + 1, :]
    neg = -jnp.inf
    g = [row(i) for i in range(N_GROUPS)]
    gmax = functools.reduce(jnp.maximum, g)
    gidx = jnp.full(gmax.shape, N_GROUPS - 1, jnp.int32)
    for i in reversed(range(N_GROUPS - 1)):
        gidx = jnp.where(g[i] == gmax, i, gidx)
    gsum = sum(jnp.exp(gi - gmax) for gi in g)
    el = []
    for j in range(EXPERTS_PER_GROUP):
        v = row(N_GROUPS + EXPERTS_PER_GROUP * (N_GROUPS - 1) + j)
        for gg in reversed(range(N_GROUPS - 1)):
            v = jnp.where(gidx == gg, row(N_GROUPS + EXPERTS_PER_GROUP * gg + j), v)
        el.append(v)

    def top(vals):
        best = functools.reduce(jnp.maximum, vals)
        idx = jnp.full(best.shape, EXPERTS_PER_GROUP - 1, jnp.int32)
        for j in reversed(range(EXPERTS_PER_GROUP - 1)):
            idx = jnp.where(vals[j] == best, j, idx)
        return best, idx

    v1, i1 = top(el)
    v2, i2 = top([jnp.where(i1 == j, neg, el[j]) for j in range(EXPERTS_PER_GROUP)])
    t = jnp.exp(v2 - v1)
    den = (1.0 + t) * gsum
    wts = [1.0 / den, t / den]
    eid = [gidx * EXPERTS_PER_GROUP + i1, gidx * EXPERTS_PER_GROUP + i2]

    esub = lax.broadcasted_iota(jnp.int32, (N_EXPERTS, tt), 0)
    ohf = jnp.concatenate([jnp.where(esub == eid[k], 1.0, 0.0) for k in range(2)], axis=1)
    res = _dot(ohf.astype(BF16), tri_ref[...])
    prefix, cnt = res[:, :2 * tt], res[:, 2 * tt:]
    cnt = (((cnt.astype(jnp.int32) + (SUBLANES - 1)) // SUBLANES) * SUBLANES).astype(F32)
    esub_c = lax.broadcasted_iota(jnp.int32, cnt.shape, 0)
    start = jnp.zeros_like(cnt)
    for e in range(N_EXPERTS - 1):
        start = start + jnp.where(esub_c > e, cnt[e:e + 1, :], 0.0)
    start_w = jnp.concatenate([start] * (2 * tt // LANES), axis=1)
    pos = jnp.sum(ohf * (start_w + prefix), axis=0, keepdims=True)
    pos_k = [pos[:, :tt], pos[:, tt:]]

    n_rows = xs_ref.shape[0]
    psub = lax.broadcasted_iota(jnp.int32, (n_rows, tt), 0).astype(F32)
    pm = [jnp.where(psub == pos_k[k], 1.0, 0.0).astype(BF16) for k in range(2)]
    xs = _dot(pm[0] + pm[1], h_ref[...])
    wsub = lax.broadcasted_iota(jnp.int32, (LANES, tt), 0)
    ws = jnp.zeros((n_rows, LANES), F32)
    for k in range(2):
        parts = _split3(wts[k])
        wrows = jnp.zeros((LANES, tt), F32)
        for j in range(3):
            wrows = jnp.where(wsub == j, parts[j].astype(F32), wrows)
        ws = ws + _dot_nt(pm[k], wrows.astype(BF16))
    half = D_MODEL // 2
    xs_ref[:, :half] = _pack_bf16_pair(xs[:, :half], xs[:, half:])
    xs_ref[:, half:] = pltpu.bitcast(ws, jnp.uint32)
    cnt_ref[0] = cnt.astype(jnp.int32)
    posr = jnp.where(wsub == 0, pos_k[0], jnp.where(wsub == 1, pos_k[1], 0.0))
    pos_ref[...] = posr.T


def _dispatch(logits, h2):
    T = logits.shape[0]
    tt = MOE_TILE
    n_tiles = T // tt
    tri = np.concatenate([np.triu(np.ones((2 * tt, 2 * tt)), 1), np.ones((2 * tt, LANES))], axis=1)
    return pl.pallas_call(
        _dispatch_kernel,
        grid=(n_tiles,),
        in_specs=[pl.BlockSpec((tt, LANES), lambda i: (i, 0)),
                  pl.BlockSpec((tt, D_MODEL), lambda i: (i, 0)),
                  pl.BlockSpec(tri.shape, lambda i: (0, 0))],
        out_specs=[pl.BlockSpec((TILE_ROWS, XS_WIDTH), lambda i: (i, 0)),
                   pl.BlockSpec((1, N_EXPERTS, LANES), lambda i: (i, 0, 0)),
                   pl.BlockSpec((tt, LANES), lambda i: (i, 0))],
        out_shape=[jax.ShapeDtypeStruct((n_tiles * TILE_ROWS, XS_WIDTH), jnp.uint32),
                   jax.ShapeDtypeStruct((n_tiles, N_EXPERTS, LANES), jnp.int32),
                   jax.ShapeDtypeStruct((T, LANES), F32)],
        compiler_params=pltpu.CompilerParams(vmem_limit_bytes=VMEM_LIMIT),
        name="moe_dispatch",
    )(logits, h2, jnp.asarray(tri, BF16))


def _plan_kernel(cnt_ref, be_ref, nv_ref, nxt_ref, grp_ref, used_ref, cs_ref):
    n_tiles = cnt_ref.shape[0]
    n_blk = be_ref.shape[0]
    rows = MOE_ROWS
    row_shift = rows.bit_length() - 1
    grp_shift = SUBLANES.bit_length() - 1
    assert rows == 1 << row_shift and SUBLANES == 1 << grp_shift

    def tile_starts(t, c):
        def per_e(e, acc):
            cs_ref[t * N_EXPERTS + e] = acc
            return acc + cnt_ref[t, e]
        used_ref[t] = lax.fori_loop(0, N_EXPERTS, per_e, 0, unroll=8)
        return c
    lax.fori_loop(0, n_tiles, tile_starts, 0)

    def clear(b, c):
        nv_ref[b] = 0
        return c
    lax.fori_loop(0, n_blk, clear, 0)

    def clear_groups(g, c):
        grp_ref[g] = 0
        return c
    lax.fori_loop(0, n_blk * MOE_GROUPS, clear_groups, 0, unroll=8)

    def per_expert(e, b):
        g0 = b * MOE_GROUPS

        def per_tile(t, tot):
            c = cnt_ref[t, e]
            src = t * TILE_ROWS + cs_ref[t * N_EXPERTS + e]
            first = g0 + lax.shift_right_logical(tot, grp_shift)

            def per_group(k, cc):
                grp_ref[first + k] = src + k * SUBLANES
                return cc
            lax.fori_loop(0, lax.shift_right_logical(c, grp_shift), per_group, 0)
            return tot + c
        tot = lax.fori_loop(0, n_tiles, per_tile, 0)

        def per_block(j, c):
            be_ref[b + j] = e
            nv_ref[b + j] = jnp.minimum(rows, tot - j * rows)
            return c
        nb = lax.shift_right_logical(tot + rows - 1, row_shift)
        lax.fori_loop(0, nb, per_block, 0)
        return b + nb
    n_used = lax.fori_loop(0, N_EXPERTS, per_expert, 0)

    def unused(b, c):
        be_ref[b] = be_ref[n_used - 1]
        nxt_ref[b] = -1
        return c
    lax.fori_loop(n_used, n_blk, unused, 0)

    def next_run(k, nf):
        b = n_used - 1 - k
        nf = jnp.where(be_ref[b] != be_ref[jnp.minimum(b + 1, n_used - 1)], b + 1, nf)
        nxt_ref[b] = nf
        return nf
    lax.fori_loop(0, n_used, next_run, -1)


def _plan(cnt, n_blk):
    n_tiles = cnt.shape[0]
    smem = pl.BlockSpec(memory_space=pltpu.SMEM)
    i32 = lambda n: jax.ShapeDtypeStruct((n,), jnp.int32)
    return pl.pallas_call(
        _plan_kernel,
        in_specs=[smem],
        out_specs=[smem] * 5,
        out_shape=[i32(n_blk), i32(n_blk), i32(n_blk), i32(n_blk * MOE_GROUPS), i32(n_tiles)],
        scratch_shapes=[pltpu.SMEM((n_tiles * N_EXPERTS,), jnp.int32)],
        name="moe_plan",
    )(cnt)


def _pow2_pieces(n, fn):
    for b in reversed(range(SUBLANES.bit_length() - 1, MOE_ROWS.bit_length())):
        size = 1 << b

        @pl.when((n & size) != 0)
        def _():
            fn((n >> (b + 1)) << (b + 1), size)


def _moe_kernel(be_ref, nv_ref, nxt_ref, grp_ref, used_ref,
                w1_hbm, w3_hbm, w2_hbm, xs_hbm, ys_hbm,
                xbuf, ybuf, wb1, wb3, wb2, wst1, wst3, wst2, wslot, gsem, ssem, wsem):
    i = pl.program_id(0)
    last = pl.num_programs(0) - 1
    slot = i % 2
    nv = nv_ref[i]
    half = D_MODEL // 2
    grp_shift = SUBLANES.bit_length() - 1

    def group_row(blk, g):
        return pl.multiple_of(grp_ref[blk * MOE_GROUPS + g], SUBLANES)

    def gather(blk, s):
        top = jnp.maximum(lax.shift_right_logical(nv_ref[blk], grp_shift) - 1, 0)
        for g in range(MOE_GROUPS):
            src = group_row(blk, jnp.minimum(g, top))
            pltpu.make_async_copy(xs_hbm.at[pl.ds(src, SUBLANES)],
                                  xbuf.at[s, pl.ds(g * SUBLANES, SUBLANES)], gsem.at[s]).start()

    def wait_gather(s):
        pltpu.make_async_copy(xs_hbm.at[pl.ds(0, MOE_ROWS)], xbuf.at[s], gsem.at[s]).wait()

    def scatter(blk, s):
        def body(g, c):
            r = pl.multiple_of(g * SUBLANES, SUBLANES)
            pltpu.make_async_copy(ybuf.at[s, pl.ds(r, SUBLANES)],
                                  ys_hbm.at[pl.ds(group_row(blk, g), SUBLANES)], ssem.at[s]).start()
            return c
        lax.fori_loop(0, lax.shift_right_logical(nv_ref[blk], grp_shift), body, 0)

    def wait_scatter(s, count):
        _pow2_pieces(count, lambda a, size: pltpu.make_async_copy(
            ybuf.at[s, pl.ds(0, size)], ys_hbm.at[pl.ds(0, size)], ssem.at[s]).wait())

    @pl.when(i == 0)
    def _():
        @pl.when(nv > 0)
        def _():
            gather(0, 0)
            gather(jnp.minimum(1, last), 1)

        ybuf[1] = jnp.zeros(ybuf.shape[1:], ybuf.dtype)
        n_tiles = used_ref.shape[0]

        def fill(t, c):
            row0 = t * TILE_ROWS + used_ref[t]
            _pow2_pieces(TILE_ROWS - used_ref[t], lambda a, size: pltpu.make_async_copy(
                ybuf.at[1, pl.ds(0, size)], ys_hbm.at[pl.ds(pl.multiple_of(row0 + a, SUBLANES), size)],
                ssem.at[1]).start())
            return c
        lax.fori_loop(0, n_tiles, fill, 0)

        def drain(t, c):
            wait_scatter(1, TILE_ROWS - used_ref[t])
            return c
        lax.fori_loop(0, n_tiles, drain, 0)

    @pl.when(i >= 2)
    def _():
        wait_scatter(slot, nv_ref[jnp.maximum(i - 2, 0)])

    xslot = i % GATHER_BUFS
    issuer_used = jnp.where(i >= 2, nv_ref[jnp.maximum(i - 2, 0)], nv_ref[0]) > 0

    @pl.when((nv == 0) & (i > 0) & issuer_used)
    def _():
        wait_gather(xslot)

    @pl.when(nv > 0)
    def _():
        e = be_ref[i]
        e_prev = be_ref[jnp.maximum(i - 1, 0)]

        def weight_copies(ex, ws):
            return [pltpu.make_async_copy(src.at[ex], dst.at[ws], wsem.at[ws])
                    for src, dst in ((w1_hbm, wst1), (w3_hbm, wst3), (w2_hbm, wst2))]

        @pl.when(i == 0)
        def _():
            wslot[0] = 0
            for cp in weight_copies(e, 0):
                cp.start()

        @pl.when((i == 0) | (e != e_prev))
        def _():
            ws = wslot[0]
            for cp in weight_copies(e, ws):
                cp.wait()
            wb1[...] = wst1[ws].astype(BF16)
            wb3[...] = wst3[ws].astype(BF16)
            wb2[...] = wst2[ws].astype(BF16)
            nb = nxt_ref[i]

            @pl.when(nb >= 0)
            def _():
                for cp in weight_copies(be_ref[jnp.maximum(nb, 0)], 1 - ws):
                    cp.start()
            wslot[0] = 1 - ws

        wait_gather(xslot)
        gather(jnp.minimum(i + 2, last), (i + 2) % GATHER_BUFS)
        u = xbuf[xslot]
        xa, xb = _unpack_bf16_pair(u[:, :half])
        wv = pltpu.bitcast(u[:, half:], F32)
        roww = wv[:, 0:1] + wv[:, 1:2] + wv[:, 2:3]
        a = _dot(xa, wb1[:half, :]) + _dot(xb, wb1[half:, :])
        b = _dot(xa, wb3[:half, :]) + _dot(xb, wb3[half:, :])
        hmid = (a * jax.nn.sigmoid(a) * b).astype(BF16)
        y = _dot(hmid, wb2[...]) * roww
        ybuf[slot] = _pack_bf16_pair(y[:, :half], y[:, half:])
        scatter(i, slot)

    @pl.when(i == last)
    def _():
        @pl.when((last >= 1) & (nv_ref[jnp.maximum(last - 1, 0)] > 0))
        def _():
            wait_gather((last + 1) % GATHER_BUFS)

        @pl.when(nv > 0)
        def _():
            wait_gather((last + 2) % GATHER_BUFS)

        @pl.when(last >= 1)
        def _():
            wait_scatter(1 - slot, nv_ref[jnp.maximum(last - 1, 0)])
        wait_scatter(slot, nv)


def _moe(xs, plan, w1, w3, w2):
    n_blk = plan[0].shape[0]
    rows = MOE_ROWS
    half = D_MODEL // 2
    hbm = pl.BlockSpec(memory_space=pl.ANY)
    grid_spec = pltpu.PrefetchScalarGridSpec(
        num_scalar_prefetch=5,
        grid=(n_blk,),
        in_specs=[hbm] * 4,
        out_specs=hbm,
        scratch_shapes=[pltpu.VMEM((GATHER_BUFS, rows, XS_WIDTH), jnp.uint32),
                        pltpu.VMEM((2, rows, half), jnp.uint32),
                        pltpu.VMEM((D_MODEL, EXPERT_HIDDEN), BF16),
                        pltpu.VMEM((D_MODEL, EXPERT_HIDDEN), BF16),
                        pltpu.VMEM((EXPERT_HIDDEN, D_MODEL), BF16),
                        pltpu.VMEM((2, D_MODEL, EXPERT_HIDDEN), F32),
                        pltpu.VMEM((2, D_MODEL, EXPERT_HIDDEN), F32),
                        pltpu.VMEM((2, EXPERT_HIDDEN, D_MODEL), F32),
                        pltpu.SMEM((1,), jnp.int32),
                        pltpu.SemaphoreType.DMA((GATHER_BUFS,)),
                        pltpu.SemaphoreType.DMA((2,)),
                        pltpu.SemaphoreType.DMA((2,))])
    return pl.pallas_call(
        _moe_kernel,
        grid_spec=grid_spec,
        out_shape=jax.ShapeDtypeStruct((xs.shape[0], half), jnp.uint32),
        compiler_params=pltpu.CompilerParams(dimension_semantics=("arbitrary",),
                                             vmem_limit_bytes=VMEM_LIMIT),
        name="moe_ffn",
    )(*plan, w1, w3, w2, xs)


def _combine_kernel(x1_ref, ys_ref, pos_ref, g2_ref, o_ref):
    tt = x1_ref.shape[0]
    half = D_MODEL // 2
    pos = pos_ref[...]
    pcol = lax.broadcasted_iota(jnp.int32, (tt, ys_ref.shape[0]), 1).astype(F32)
    sel = (jnp.where(pcol == pos[:, 0:1], 1.0, 0.0) + jnp.where(pcol == pos[:, 1:2], 1.0, 0.0))
    sel = sel.astype(BF16)
    lo, hi = _unpack_bf16_pair(ys_ref[...])
    g2 = g2_ref[0]
    x1 = x1_ref[...]
    o_ref[:, :half] = x1[:, :half] + g2[:, :half] * _dot(sel, lo)
    o_ref[:, half:] = x1[:, half:] + g2[:, half:] * _dot(sel, hi)


def _combine(x1, ys, pos, g2, S):
    T = x1.shape[0]
    tt = MOE_TILE
    per_b = S // tt
    return pl.pallas_call(
        _combine_kernel,
        grid=(T // tt,),
        in_specs=[pl.BlockSpec((tt, D_MODEL), lambda i: (i, 0)),
                  pl.BlockSpec((TILE_ROWS, D_MODEL // 2), lambda i: (i, 0)),
                  pl.BlockSpec((tt, LANES), lambda i: (i, 0)),
                  pl.BlockSpec((1, 1, D_MODEL), lambda i: (i // per_b, 0, 0))],
        out_specs=pl.BlockSpec((tt, D_MODEL), lambda i: (i, 0)),
        out_shape=jax.ShapeDtypeStruct((T, D_MODEL), F32),
        compiler_params=pltpu.CompilerParams(vmem_limit_bytes=VMEM_LIMIT),
        name="moe_combine",
    )(x1, ys, pos, g2)


def _prep_w_in(w_in):
    dq = w_in[:, OFF_DIL_Q:OFF_DIL_K]
    dk = w_in[:, OFF_DIL_K:OFF_DIL_V]
    dv = w_in[:, OFF_DIL_V:OFF_GATE_A]
    dil = []
    for g in range(N_DIL_GROUPS):
        p0 = slice(g * DIL_OUT_WIDTH, g * DIL_OUT_WIDTH + LANES)
        p1 = slice(g * DIL_OUT_WIDTH + LANES, (g + 1) * DIL_OUT_WIDTH)
        dil += [dq[:, p0], dk[:, p0], dq[:, p1], dk[:, p1], dv[:, p0], dv[:, p1]]
    pad = jnp.zeros((D_MODEL, UNIT - FOX_HEADS), w_in.dtype)
    cols = [w_in[:, OFF_GATE_A:OFF_GATE_B], w_in[:, OFF_GATE_B:N_IN],
            w_in[:, OFF_FOX_Q:OFF_FOX_K], w_in[:, OFF_FOX_K:OFF_FOX_V], w_in[:, OFF_FOX_V:OFF_FOX_F],
            *dil, w_in[:, OFF_FOX_F:OFF_DIL_Q], pad]
    return jnp.concatenate(cols, axis=1).astype(BF16)


def _prep_gain(q_gain, k_gain):
    qs = HEAD_DIM ** -0.5 * LOG2E
    ones = jnp.ones((UNIT,), F32)
    fq = q_gain[:FOX_HEADS].reshape(-1) * qs
    fk = k_gain[:FOX_HEADS].reshape(-1)
    dq = q_gain[FOX_HEADS:].reshape(-1) * qs
    dk = k_gain[FOX_HEADS:].reshape(-1)
    dil = []
    for g in range(N_DIL_GROUPS):
        p0 = slice(g * DIL_OUT_WIDTH, g * DIL_OUT_WIDTH + LANES)
        p1 = slice(g * DIL_OUT_WIDTH + LANES, (g + 1) * DIL_OUT_WIDTH)
        dil += [dq[p0], dk[p0], dq[p1], dk[p1], ones]
    parts = [ones] * 8 + [fq, fk, ones, ones] + dil + [ones]
    return jnp.concatenate(parts).reshape(1, N_UNITS * UNIT)


def _layer(x, mod, rel_bias_table, norm1_g, w_in, b_forget, q_gain, k_gain, w_branch_a, w_branch_b,
           w_out, norm2_g, w_rg, b_rg, w_re, b_re, w1, w3, w2):
    B, S, D = x.shape
    T = B * S
    sh1, sc1, g1, sh2, sc2, g2 = [m.reshape(B, 1, D) for m in jnp.split(mod, 6, axis=-1)]
    x2d = x.reshape(T, D)

    p2d, fgt, slabs = _inproj(x2d, norm1_g.reshape(1, D), sc1, sh1, _prep_w_in(w_in),
                              _prep_gain(q_gain, k_gain), S)
    p3 = p2d.reshape(B, S, P_WIDTH)
    ck = _fcum(fgt.reshape(B, S, LANES), b_forget)
    ya = _fox(p3, ck)
    yd = _dil(slabs, _relbias(rel_bias_table))

    n_router = N_GROUPS + N_EXPERTS
    wr = jnp.concatenate([w_rg, w_re, jnp.zeros((D, LANES - n_router), F32)], axis=1).astype(BF16)
    br = jnp.concatenate([b_rg, b_re, jnp.zeros((LANES - n_router,), F32)]).reshape(1, LANES)
    x1, h2, logits = _outproj(x2d, ya.reshape(T, FOX_WIDTH), yd.reshape(T, DIL_OUT_WIDTH), p2d,
                              g1, sc2, sh2, norm2_g.reshape(1, D),
                              w_branch_a.astype(BF16), w_branch_b.astype(BF16), w_out.astype(BF16),
                              wr, br, S)
    xs, cnt, pos = _dispatch(logits, h2)
    cnt2 = cnt[:, :, 0]
    n_blk = cnt.shape[0] * TILE_ROWS // MOE_ROWS + N_EXPERTS
    plan = _plan(cnt2, n_blk)
    ys = _moe(xs, plan, w1, w3, w2)
    out = _combine(x1, ys, pos, g2, S)
    return out.reshape(B, S, D)


def kernel(x, c, rel_bias_table, w_ada, b_ada, norm1_g, w_in, b_forget, q_gain, k_gain, w_branch_a, w_branch_b, w_out, norm2_g, w_router_group, b_router_group, w_router_expert, b_router_expert, w1, w3, w2):
    depth = w_ada.shape[0]
    for l in range(depth):
        mod = _ada(c, w_ada[l], b_ada[l])
        x = _layer(x, mod, rel_bias_table, norm1_g[l], w_in[l], b_forget[l], q_gain[l], k_gain[l],
                   w_branch_a[l], w_branch_b[l], w_out[l], norm2_g[l], w_router_group[l],
                   b_router_group[l], w_router_expert[l], b_router_expert[l], w1[l], w3[l], w2[l])
    return x
```

```python
import functools
import math

import numpy as np
import jax
import jax.numpy as jnp
from jax import lax
from jax.experimental import pallas as pl
from jax.experimental.pallas import tpu as pltpu

F32 = jnp.float32
BF16 = jnp.bfloat16

D_MODEL = 1024
HEAD_DIM = 64
FOX_HEADS = 8
DIL_GROUPS = ((128, 1), (512, 4), (2048, 16))
DIL_HEADS_PER_GROUP = 4
N_DIL_GROUPS = len(DIL_GROUPS)
DIL_HEADS = N_DIL_GROUPS * DIL_HEADS_PER_GROUP
FOX_WIDTH = FOX_HEADS * HEAD_DIM
DIL_WIDTH = DIL_HEADS * HEAD_DIM
DIL_OUT_WIDTH = DIL_HEADS_PER_GROUP * HEAD_DIM
NUM_BUCKETS = 32
REL_MAX_DISTANCE = 2048
N_GROUPS = 4
EXPERTS_PER_GROUP = 8
N_EXPERTS = N_GROUPS * EXPERTS_PER_GROUP
EXPERT_HIDDEN = D_MODEL // 2
EPS = 1e-6
LOG2E = math.log2(math.e)

OFF_FOX_Q = 0
OFF_FOX_K = OFF_FOX_Q + FOX_WIDTH
OFF_FOX_V = OFF_FOX_K + FOX_WIDTH
OFF_FOX_F = OFF_FOX_V + FOX_WIDTH
OFF_DIL_Q = OFF_FOX_F + FOX_HEADS
OFF_DIL_K = OFF_DIL_Q + DIL_WIDTH
OFF_DIL_V = OFF_DIL_K + DIL_WIDTH
OFF_GATE_A = OFF_DIL_V + DIL_WIDTH
OFF_GATE_B = OFF_GATE_A + D_MODEL
N_IN = OFF_GATE_B + D_MODEL

LANES = 128
UNIT = 256
DIL_L = 128

U_GATE_A, U_GATE_B, U_FOX_Q, U_FOX_K, U_FOX_V, U_DIL, U_FORGET = 0, 4, 8, 10, 12, 14, 23
N_UNITS = 24
P_WIDTH = U_DIL * UNIT
N_SLABS = 3 * N_DIL_GROUPS
_KIND = (["gate"] * 8 + ["norm"] * 4 + ["plain"] * 2 + ["norm", "norm", "plain"] * 3 + ["forget"])

TM_INPROJ = 1024
TM_PROJ = 1024
TQ_FOX = 512
MOE_ROWS = 256
MOE_TILE = 512
XS_WIDTH = D_MODEL // 2 + LANES
SUBLANES = 8
TILE_ROWS = 2 * MOE_TILE + N_EXPERTS * SUBLANES
MOE_GROUPS = MOE_ROWS // SUBLANES
GATHER_BUFS = 3
VMEM_LIMIT = 56 * 1024 * 1024


def _dot(a, b):
    return jnp.dot(a, b, preferred_element_type=F32)


def _dot_nt(a, b):
    return lax.dot_general(a, b, (((1,), (1,)), ((), ())), preferred_element_type=F32)


def _split3(x):
    hi = x.astype(BF16)
    r1 = x - hi.astype(F32)
    mid = r1.astype(BF16)
    lo = (r1 - mid.astype(F32)).astype(BF16)
    return hi, mid, lo


def _ada_kernel(c_ref, w_ref, b_ref, o_ref):
    c = c_ref[...]
    s = c * jax.nn.sigmoid(c)
    s_hi = s.astype(BF16)
    s_lo = (s - s_hi.astype(F32)).astype(BF16)
    w = w_ref[...]
    w_hi = w.astype(BF16)
    w_lo = (w - w_hi.astype(F32)).astype(BF16)
    acc = _dot(s_hi, w_hi) + _dot(s_hi, w_lo) + _dot(s_lo, w_hi)
    o_ref[...] = acc + b_ref[...]


def _ada(c, w_ada, b_ada):
    B = c.shape[0]
    n_out = w_ada.shape[1]
    tn = 512
    return pl.pallas_call(
        _ada_kernel,
        grid=(n_out // tn,),
        in_specs=[pl.BlockSpec((B, D_MODEL), lambda j: (0, 0)),
                  pl.BlockSpec((D_MODEL, tn), lambda j: (0, j)),
                  pl.BlockSpec((1, tn), lambda j: (0, j))],
        out_specs=pl.BlockSpec((B, tn), lambda j: (0, j)),
        out_shape=jax.ShapeDtypeStruct((B, n_out), F32),
        name="ada_mod",
    )(c, w_ada, b_ada.reshape(1, n_out))


def _pack_bf16_pair(lo, hi):
    lo_bits = pltpu.bitcast(lo.astype(BF16).astype(F32), jnp.uint32) >> 16
    hi_bits = pltpu.bitcast(hi.astype(BF16).astype(F32), jnp.uint32) & jnp.uint32(0xFFFF0000)
    return lo_bits | hi_bits


def _unpack_bf16_pair(u):
    lo = pltpu.bitcast(u << 16, F32).astype(BF16)
    hi = pltpu.bitcast(u & jnp.uint32(0xFFFF0000), F32).astype(BF16)
    return lo, hi


def _inproj_kernel(x_ref, g_ref, sc_ref, sh_ref, w_ref, gain_ref, bd_ref, p_ref, f_ref, s_ref):
    x = x_ref[...]
    ms = jnp.mean(x * x, axis=-1, keepdims=True)
    h = x * lax.rsqrt(ms + EPS) * g_ref[...]
    h = h * (1.0 + sc_ref[0]) + sh_ref[0]
    hb = h.astype(BF16)

    def unit(u):
        cols = slice(u * UNIT, (u + 1) * UNIT)
        acc = _dot(hb, w_ref[:, cols])
        kind = _KIND[u]
        if kind == "gate":
            return jax.nn.sigmoid(acc)
        if kind == "norm":
            ss = _dot((acc * acc).astype(BF16), bd_ref[...])
            return acc * lax.rsqrt(ss * (1.0 / HEAD_DIM) + EPS) * gain_ref[:, cols]
        return acc

    def emit(u):
        o = unit(u)
        if u < U_DIL:
            p_ref[:, u * UNIT:(u + 1) * UNIT] = o.astype(BF16)
        elif u < U_FORGET:
            s_ref[0, u - U_DIL] = _pack_bf16_pair(o[:, :LANES], o[:, LANES:])
        else:
            f_ref[...] = o[:, :LANES]

    normed = [u for u in range(N_UNITS) if _KIND[u] == "norm"]
    others = [u for u in range(N_UNITS) if _KIND[u] != "norm"]
    while normed or others:
        for group in (normed, others):
            if group:
                emit(group.pop(0))


def _inproj(x2d, norm_g, sc, sh, w_re, gain_row, S):
    T = x2d.shape[0]
    tm = TM_INPROJ
    per_b = S // tm
    bd = np.kron(np.eye(UNIT // HEAD_DIM), np.ones((HEAD_DIM, HEAD_DIM))).astype(np.float32)
    once = dict(pipeline_mode=pl.Buffered(1))
    return pl.pallas_call(
        _inproj_kernel,
        grid=(T // tm,),
        in_specs=[pl.BlockSpec((tm, D_MODEL), lambda i: (i, 0)),
                  pl.BlockSpec((1, D_MODEL), lambda i: (0, 0)),
                  pl.BlockSpec((1, 1, D_MODEL), lambda i: (i // per_b, 0, 0)),
                  pl.BlockSpec((1, 1, D_MODEL), lambda i: (i // per_b, 0, 0)),
                  pl.BlockSpec((D_MODEL, N_UNITS * UNIT), lambda i: (0, 0), **once),
                  pl.BlockSpec((1, N_UNITS * UNIT), lambda i: (0, 0), **once),
                  pl.BlockSpec((UNIT, UNIT), lambda i: (0, 0), **once)],
        out_specs=[pl.BlockSpec((tm, P_WIDTH), lambda i: (i, 0)),
                   pl.BlockSpec((tm, LANES), lambda i: (i, 0)),
                   pl.BlockSpec((1, N_SLABS, tm, LANES), lambda i: (i // per_b, 0, i % per_b, 0))],
        out_shape=[jax.ShapeDtypeStruct((T, P_WIDTH), BF16),
                   jax.ShapeDtypeStruct((T, LANES), F32),
                   jax.ShapeDtypeStruct((T // S, N_SLABS, S, LANES), jnp.uint32)],
        compiler_params=pltpu.CompilerParams(vmem_limit_bytes=VMEM_LIMIT),
        name="in_proj",
    )(x2d, norm_g, sc, sh, w_re, gain_row, jnp.asarray(bd, BF16))


def _fcum_kernel(f_ref, b_ref, tri_ref, o_ref):
    S = f_ref.shape[1]
    xf = f_ref[0] + b_ref[...]
    ls = (jnp.minimum(xf, 0.0) - jnp.log(1.0 + jnp.exp(-jnp.abs(xf)))) * LOG2E
    lst = ls.T
    carry = jnp.zeros((LANES, UNIT), F32)
    for blk in range(S // UNIT):
        seg = lst[:, blk * UNIT:(blk + 1) * UNIT]
        hi, mid, lo = _split3(seg)
        tri = tri_ref[...]
        res = _dot(hi, tri) + _dot(mid, tri) + _dot(lo, tri)
        o_ref[0, :, blk * UNIT:(blk + 1) * UNIT] = (res[:, :UNIT] + carry)[:FOX_HEADS]
        carry = carry + res[:, UNIT:]


def _fcum(fgt, b_forget):
    B, S, _ = fgt.shape
    brow = jnp.zeros((1, LANES), F32).at[0, :FOX_HEADS].set(b_forget)
    tri = np.concatenate([np.triu(np.ones((UNIT, UNIT))), np.ones((UNIT, UNIT))], axis=1)
    return pl.pallas_call(
        _fcum_kernel,
        grid=(B,),
        in_specs=[pl.BlockSpec((1, S, LANES), lambda b: (b, 0, 0)),
                  pl.BlockSpec((1, LANES), lambda b: (0, 0)),
                  pl.BlockSpec((UNIT, 2 * UNIT), lambda b: (0, 0))],
        out_specs=pl.BlockSpec((1, FOX_HEADS, S), lambda b: (b, 0, 0)),
        out_shape=jax.ShapeDtypeStruct((B, FOX_HEADS, S), F32),
        name="forget_cumsum",
    )(fgt, brow, jnp.asarray(tri, BF16))


def _fox_kernel(q_ref, k_ref, v_ref, ck_ref, o_ref):
    S = q_ref.shape[1]
    pair = pl.program_id(1)
    tq = TQ_FOX
    lane = lax.broadcasted_iota(jnp.int32, (1, LANES), 1)
    row = lax.broadcasted_iota(jnp.int32, (tq, tq), 0)
    col = lax.broadcasted_iota(jnp.int32, (tq, tq), 1)
    causal = col <= row
    cks = [ck_ref[0, pl.ds(2 * pair + hh, 1), :] for hh in range(2)]
    for t in reversed(range(S // tq)):
        r0, r1 = t * tq, (t + 1) * tq
        qt = q_ref[0, r0:r1, :]
        outs = []
        for hh in range(2):
            hsel = (lane >= HEAD_DIM) == bool(hh)
            qm = jnp.where(hsel, qt, jnp.zeros_like(qt))
            ck = cks[hh]
            s = _dot_nt(qm, k_ref[0, :r1, :]) - ck[:, :r1]
            s_d = jnp.where(causal, s[:, r0:], -jnp.inf)
            s = jnp.concatenate([s[:, :r0], s_d], axis=1) if t > 0 else s_d
            m = jnp.max(s, axis=-1, keepdims=True)
            p = jnp.exp2(s - m)
            l = jnp.sum(p, axis=-1, keepdims=True)
            outs.append(_dot(p.astype(BF16), v_ref[0, :r1, :]) / l)
        o_ref[0, r0:r1, :] = jnp.where(lane < HEAD_DIM, outs[0], outs[1]).astype(BF16)


def _fox(p3, ck):
    B, S, _ = p3.shape
    nq, nk, nv = (U_FOX_Q * UNIT // LANES, U_FOX_K * UNIT // LANES, U_FOX_V * UNIT // LANES)
    return pl.pallas_call(
        _fox_kernel,
        grid=(B, FOX_HEADS // 2),
        in_specs=[pl.BlockSpec((1, S, LANES), lambda b, p: (b, 0, nq + p)),
                  pl.BlockSpec((1, S, LANES), lambda b, p: (b, 0, nk + p)),
                  pl.BlockSpec((1, S, LANES), lambda b, p: (b, 0, nv + p)),
                  pl.BlockSpec((1, FOX_HEADS, S), lambda b, p: (b, 0, 0))],
        out_specs=pl.BlockSpec((1, S, LANES), lambda b, p: (b, 0, p)),
        out_shape=jax.ShapeDtypeStruct((B, S, FOX_WIDTH), BF16),
        compiler_params=pltpu.CompilerParams(vmem_limit_bytes=VMEM_LIMIT),
        name="fox_attn",
    )(p3, p3, p3, ck)


def _t5_bucket(dist):
    max_exact = NUM_BUCKETS // 2
    d = np.maximum(dist, 1).astype(np.float32)
    large = max_exact + (np.log(d / max_exact) / np.log(REL_MAX_DISTANCE / max_exact)
                         * (NUM_BUCKETS - max_exact)).astype(np.int32)
    large = np.minimum(large, NUM_BUCKETS - 1)
    return np.where(dist < max_exact, dist, large).astype(np.int32)


def _relbias_kernel(tab_ref, bucket_ref, valid_ref, o_ref):
    g = pl.program_id(0)
    bk = bucket_ref[0]
    vd = valid_ref[0]
    for hs in range(DIL_HEADS_PER_GROUP):
        acc = jnp.zeros(bk.shape, F32)
        for b in range(NUM_BUCKETS):
            acc = jnp.where(bk == b, tab_ref[b, g * DIL_HEADS_PER_GROUP + hs], acc)
        bias = jnp.where(vd != 0, acc * LOG2E, -jnp.inf)
        o_ref[0, hs] = bias
        col = lax.broadcasted_iota(jnp.int32, bias.shape, 1)
        o_ref[1, hs] = jnp.where(col >= DIL_L, bias, -jnp.inf)


def _relbias(table):
    L = DIL_L
    i = np.arange(L)[:, None]
    j = np.arange(2 * L)[None, :]
    m = L + i - j
    valid = ((m >= 0) & (m <= L)).astype(np.int32)
    buckets = np.stack([_t5_bucket(np.clip(m, 0, None) * d) for _, d in DIL_GROUPS])
    valids = np.stack([valid] * N_DIL_GROUPS)
    return pl.pallas_call(
        _relbias_kernel,
        grid=(N_DIL_GROUPS,),
        in_specs=[pl.BlockSpec(memory_space=pltpu.SMEM),
                  pl.BlockSpec((1, L, 2 * L), lambda g: (g, 0, 0)),
                  pl.BlockSpec((1, L, 2 * L), lambda g: (g, 0, 0))],
        out_specs=pl.BlockSpec((2, DIL_HEADS_PER_GROUP, L, 2 * L), lambda g: (0, g, 0, 0)),
        out_shape=jax.ShapeDtypeStruct((2, DIL_HEADS, L, 2 * L), F32),
        name="rel_bias",
    )(table, jnp.asarray(buckets), jnp.asarray(valids))


def _dil_rows(start, d):
    return pl.ds(start, DIL_L) if d == 1 else pl.ds(start, DIL_L, stride=d)


def _dil_block_rows(d, nb, it):
    r, n = it // nb, it % nb
    cur = _dil_rows(r + d * (n * DIL_L), d)
    prev = _dil_rows(r + d * (jnp.maximum(n - 1, 0) * DIL_L), d)
    return cur, prev, 1 - jnp.minimum(n, 1)


def _dil_scores(qkv_ref, bias_ref, s_scr, slot, g, d, nb, it):
    lane = lax.broadcasted_iota(jnp.int32, (1, LANES), 1)
    cur, prev, first = _dil_block_rows(d, nb, it)
    for pr in range(2):
        qt, kt = _unpack_bf16_pair(qkv_ref[0, 3 * g + pr, cur, :])
        if nb > 1:
            _, k_prev = _unpack_bf16_pair(qkv_ref[0, 3 * g + pr, prev, :])
            kt = jnp.concatenate([k_prev, kt], axis=0)
        for hh in range(2):
            hsel = (lane >= HEAD_DIM) == bool(hh)
            qm = jnp.where(hsel, qt, jnp.zeros_like(qt))
            head = 2 * pr + hh
            if nb > 1:
                s_scr[slot, head] = _dot_nt(qm, kt) + bias_ref[first, DIL_HEADS_PER_GROUP * g + head]
            else:
                bias = bias_ref[0, DIL_HEADS_PER_GROUP * g + head, :, DIL_L:]
                s_scr[slot, head, :, :DIL_L] = _dot_nt(qm, kt) + bias


def _dil_merge(qkv_ref, s_scr, slot, m_scr, l_scr, acc_scr, g, d, nb, init, it):
    lane = lax.broadcasted_iota(jnp.int32, (1, LANES), 1)
    cur, prev, _ = _dil_block_rows(d, nb, it)
    v_cur = _unpack_bf16_pair(qkv_ref[0, 3 * g + 2, cur, :])
    if nb > 1:
        v_prev = _unpack_bf16_pair(qkv_ref[0, 3 * g + 2, prev, :])
    for pr in range(2):
        vt = jnp.concatenate([v_prev[pr], v_cur[pr]], axis=0) if nb > 1 else v_cur[pr]
        ms, ls, accs = [], [], []
        for hh in range(2):
            s = s_scr[slot, 2 * pr + hh] if nb > 1 else s_scr[slot, 2 * pr + hh, :, :DIL_L]
            m = jnp.max(s, axis=-1, keepdims=True)
            p = jnp.exp2(s - m)
            ms.append(m)
            ls.append(jnp.sum(p, axis=-1, keepdims=True))
            accs.append(_dot(p.astype(BF16), vt))
        low = lane < HEAD_DIM
        m_b = jnp.where(low, ms[0], ms[1])
        l_b = jnp.where(low, ls[0], ls[1])
        acc_b = jnp.where(low, accs[0], accs[1])
        if init:
            m_scr[pr, cur, :] = m_b
            l_scr[pr, cur, :] = l_b
            acc_scr[pr, cur, :] = acc_b
        else:
            m_o = m_scr[pr, cur, :]
            m_n = jnp.maximum(m_o, m_b)
            a_o = jnp.exp2(m_o - m_n)
            a_b = jnp.exp2(m_b - m_n)
            m_scr[pr, cur, :] = m_n
            l_scr[pr, cur, :] = l_scr[pr, cur, :] * a_o + l_b * a_b
            acc_scr[pr, cur, :] = acc_scr[pr, cur, :] * a_o + acc_b * a_b


def _dil_kernel(qkv_ref, bias_ref, o_ref, m_scr, l_scr, acc_scr, s_scr):
    S = o_ref.shape[1]
    order = sorted(range(N_DIL_GROUPS), key=lambda g: -DIL_GROUPS[g][1])
    for g in order:
        window, d = DIL_GROUPS[g]
        nb = S // window
        total = d * nb
        assert total % 2 == 0
        scores = functools.partial(_dil_scores, qkv_ref, bias_ref, s_scr, g=g, d=d, nb=nb)
        merge = functools.partial(_dil_merge, qkv_ref, s_scr, m_scr=m_scr, l_scr=l_scr,
                                  acc_scr=acc_scr, g=g, d=d, nb=nb, init=g == order[0])

        scores(slot=0, it=0)

        def body(j, carry, scores=scores, merge=merge, total=total):
            scores(slot=1, it=2 * j + 1)
            merge(slot=0, it=2 * j)
            scores(slot=0, it=jnp.minimum(2 * j + 2, total - 1))
            merge(slot=1, it=2 * j + 1)
            return carry
        lax.fori_loop(0, total // 2, body, 0, unroll=2)
    for pr in range(2):
        o_ref[0, :, pr * LANES:(pr + 1) * LANES] = (acc_scr[pr] / l_scr[pr]).astype(BF16)


def _dil(slabs, bias):
    B, _, S, _ = slabs.shape
    for window, d in DIL_GROUPS:
        assert window // d == DIL_L and S % window == 0
    stat = pltpu.VMEM((2, S, LANES), F32)
    return pl.pallas_call(
        _dil_kernel,
        grid=(B,),
        in_specs=[pl.BlockSpec((1, N_SLABS, S, LANES), lambda b: (b, 0, 0, 0)),
                  pl.BlockSpec(bias.shape, lambda b: (0, 0, 0, 0))],
        out_specs=pl.BlockSpec((1, S, DIL_OUT_WIDTH), lambda b: (b, 0, 0)),
        out_shape=jax.ShapeDtypeStruct((B, S, DIL_OUT_WIDTH), BF16),
        scratch_shapes=[stat, stat, stat,
                        pltpu.VMEM((2, DIL_HEADS_PER_GROUP, DIL_L, 2 * DIL_L), F32)],
        compiler_params=pltpu.CompilerParams(vmem_limit_bytes=VMEM_LIMIT),
        name="dil_attn",
    )(slabs, bias)


def _outproj_kernel(x_ref, ya_ref, yd_ref, ga_ref, gb_ref,
                    g1_ref, sc_ref, sh_ref, ng_ref, wa_ref, wb_ref, wo_ref, wr_ref, br_ref,
                    x1_ref, h2_ref, lg_ref):
    n_chunks = 2
    cm = x_ref.shape[0] // n_chunks
    for c in range(n_chunks):
        rows = slice(c * cm, (c + 1) * cm)
        a = _dot(ya_ref[rows, :], wa_ref[...])
        bm = _dot(yd_ref[rows, :], wb_ref[...])
        merged = ga_ref[rows, :].astype(F32) * a + gb_ref[rows, :].astype(F32) * bm
        out = _dot(merged.astype(BF16), wo_ref[...])
        x1 = x_ref[rows, :] + g1_ref[0] * out
        x1_ref[rows, :] = x1
        ms = jnp.mean(x1 * x1, axis=-1, keepdims=True)
        h = x1 * lax.rsqrt(ms + EPS) * ng_ref[...]
        h = h * (1.0 + sc_ref[0]) + sh_ref[0]
        hb = h.astype(BF16)
        h2_ref[rows, :] = hb
        lg_ref[rows, :] = _dot(hb, wr_ref[...]) + br_ref[...]


def _outproj(x2d, ya2d, yd2d, p2d, g1, sc2, sh2, norm_g, wa, wb, wo, wr, br, S):
    T = x2d.shape[0]
    tm = TM_PROJ
    per_b = S // tm
    row = lambda w: pl.BlockSpec((tm, w), lambda i: (i, 0))
    full = lambda a: pl.BlockSpec(a.shape, lambda i: (0,) * a.ndim)
    mod = pl.BlockSpec((1, 1, D_MODEL), lambda i: (i // per_b, 0, 0))
    return pl.pallas_call(
        _outproj_kernel,
        grid=(T // tm,),
        in_specs=[row(D_MODEL), row(FOX_WIDTH), row(DIL_OUT_WIDTH)]
                 + [pl.BlockSpec((tm, D_MODEL), lambda i: (i, U_GATE_A * UNIT // D_MODEL)),
                    pl.BlockSpec((tm, D_MODEL), lambda i: (i, U_GATE_B * UNIT // D_MODEL)),
                    mod, mod, mod, full(norm_g), full(wa), full(wb), full(wo), full(wr), full(br)],
        out_specs=[row(D_MODEL), row(D_MODEL), row(LANES)],
        out_shape=[jax.ShapeDtypeStruct((T, D_MODEL), F32),
                   jax.ShapeDtypeStruct((T, D_MODEL), BF16),
                   jax.ShapeDtypeStruct((T, LANES), F32)],
        compiler_params=pltpu.CompilerParams(vmem_limit_bytes=VMEM_LIMIT),
        name="out_proj",
    )(x2d, ya2d, yd2d, p2d, p2d, g1, sc2, sh2, norm_g, wa, wb, wo, wr, br)


def _dispatch_kernel(lg_ref, h_ref, tri_ref, xs_ref, cnt_ref, pos_ref):
    tt = lg_ref.shape[0]
    lt = lg_ref[...].T
    row = lambda i: lt[i:i + 1, :]
    neg = -jnp.inf
    g = [row(i) for i in range(N_GROUPS)]
    gmax = functools.reduce(jnp.maximum, g)
    gidx = jnp.full(gmax.shape, N_GROUPS - 1, jnp.int32)
    for i in reversed(range(N_GROUPS - 1)):
        gidx = jnp.where(g[i] == gmax, i, gidx)
    gsum = sum(jnp.exp(gi - gmax) for gi in g)
    el = []
    for j in range(EXPERTS_PER_GROUP):
        v = row(N_GROUPS + EXPERTS_PER_GROUP * (N_GROUPS - 1) + j)
        for gg in reversed(range(N_GROUPS - 1)):
            v = jnp.where(gidx == gg, row(N_GROUPS + EXPERTS_PER_GROUP * gg + j), v)
        el.append(v)

    def top(vals):
        best = functools.reduce(jnp.maximum, vals)
        idx = jnp.full(best.shape, EXPERTS_PER_GROUP - 1, jnp.int32)
        for j in reversed(range(EXPERTS_PER_GROUP - 1)):
            idx = jnp.where(vals[j] == best, j, idx)
        return best, idx

    v1, i1 = top(el)
    v2, i2 = top([jnp.where(i1 == j, neg, el[j]) for j in range(EXPERTS_PER_GROUP)])
    t = jnp.exp(v2 - v1)
    den = (1.0 + t) * gsum
    wts = [1.0 / den, t / den]
    eid = [gidx * EXPERTS_PER_GROUP + i1, gidx * EXPERTS_PER_GROUP + i2]

    esub = lax.broadcasted_iota(jnp.int32, (N_EXPERTS, tt), 0)
    ohf = jnp.concatenate([jnp.where(esub == eid[k], 1.0, 0.0) for k in range(2)], axis=1)
    res = _dot(ohf.astype(BF16), tri_ref[...])
    prefix, cnt = res[:, :2 * tt], res[:, 2 * tt:]
    cnt = (((cnt.astype(jnp.int32) + (SUBLANES - 1)) // SUBLANES) * SUBLANES).astype(F32)
    esub_c = lax.broadcasted_iota(jnp.int32, cnt.shape, 0)
    start = jnp.zeros_like(cnt)
    for e in range(N_EXPERTS - 1):
        start = start + jnp.where(esub_c > e, cnt[e:e + 1, :], 0.0)
    start_w = jnp.concatenate([start] * (2 * tt // LANES), axis=1)
    pos = jnp.sum(ohf * (start_w + prefix), axis=0, keepdims=True)
    pos_k = [pos[:, :tt], pos[:, tt:]]

    n_rows = xs_ref.shape[0]
    psub = lax.broadcasted_iota(jnp.int32, (n_rows, tt), 0).astype(F32)
    pm = [jnp.where(psub == pos_k[k], 1.0, 0.0).astype(BF16) for k in range(2)]
    xs = _dot(pm[0] + pm[1], h_ref[...])
    wsub = lax.broadcasted_iota(jnp.int32, (LANES, tt), 0)
    ws = jnp.zeros((n_rows, LANES), F32)
    for k in range(2):
        parts = _split3(wts[k])
        wrows = jnp.zeros((LANES, tt), F32)
        for j in range(3):
            wrows = jnp.where(wsub == j, parts[j].astype(F32), wrows)
        ws = ws + _dot_nt(pm[k], wrows.astype(BF16))
    half = D_MODEL // 2
    xs_ref[:, :half] = _pack_bf16_pair(xs[:, :half], xs[:, half:])
    xs_ref[:, half:] = pltpu.bitcast(ws, jnp.uint32)
    cnt_ref[0] = cnt.astype(jnp.int32)
    posr = jnp.where(wsub == 0, pos_k[0], jnp.where(wsub == 1, pos_k[1], 0.0))
    pos_ref[...] = posr.T


def _dispatch(logits, h2):
    T = logits.shape[0]
    tt = MOE_TILE
    n_tiles = T // tt
    tri = np.concatenate([np.triu(np.ones((2 * tt, 2 * tt)), 1), np.ones((2 * tt, LANES))], axis=1)
    return pl.pallas_call(
        _dispatch_kernel,
        grid=(n_tiles,),
        in_specs=[pl.BlockSpec((tt, LANES), lambda i: (i, 0)),
                  pl.BlockSpec((tt, D_MODEL), lambda i: (i, 0)),
                  pl.BlockSpec(tri.shape, lambda i: (0, 0))],
        out_specs=[pl.BlockSpec((TILE_ROWS, XS_WIDTH), lambda i: (i, 0)),
                   pl.BlockSpec((1, N_EXPERTS, LANES), lambda i: (i, 0, 0)),
                   pl.BlockSpec((tt, LANES), lambda i: (i, 0))],
        out_shape=[jax.ShapeDtypeStruct((n_tiles * TILE_ROWS, XS_WIDTH), jnp.uint32),
                   jax.ShapeDtypeStruct((n_tiles, N_EXPERTS, LANES), jnp.int32),
                   jax.ShapeDtypeStruct((T, LANES), F32)],
        compiler_params=pltpu.CompilerParams(vmem_limit_bytes=VMEM_LIMIT),
        name="moe_dispatch",
    )(logits, h2, jnp.asarray(tri, BF16))


def _plan_kernel(cnt_ref, be_ref, nv_ref, nxt_ref, grp_ref, used_ref, cs_ref):
    n_tiles = cnt_ref.shape[0]
    n_blk = be_ref.shape[0]
    rows = MOE_ROWS
    row_shift = rows.bit_length() - 1
    grp_shift = SUBLANES.bit_length() - 1
    assert rows == 1 << row_shift and SUBLANES == 1 << grp_shift

    def tile_starts(t, c):
        def per_e(e, acc):
            cs_ref[t * N_EXPERTS + e] = acc
            return acc + cnt_ref[t, e]
        used_ref[t] = lax.fori_loop(0, N_EXPERTS, per_e, 0, unroll=8)
        return c
    lax.fori_loop(0, n_tiles, tile_starts, 0)

    def clear(b, c):
        nv_ref[b] = 0
        return c
    lax.fori_loop(0, n_blk, clear, 0)

    def clear_groups(g, c):
        grp_ref[g] = 0
        return c
    lax.fori_loop(0, n_blk * MOE_GROUPS, clear_groups, 0, unroll=8)

    def per_expert(e, b):
        g0 = b * MOE_GROUPS

        def per_tile(t, tot):
            c = cnt_ref[t, e]
            src = t * TILE_ROWS + cs_ref[t * N_EXPERTS + e]
            first = g0 + lax.shift_right_logical(tot, grp_shift)

            def per_group(k, cc):
                grp_ref[first + k] = src + k * SUBLANES
                return cc
            lax.fori_loop(0, lax.shift_right_logical(c, grp_shift), per_group, 0)
            return tot + c
        tot = lax.fori_loop(0, n_tiles, per_tile, 0)

        def per_block(j, c):
            be_ref[b + j] = e
            nv_ref[b + j] = jnp.minimum(rows, tot - j * rows)
            return c
        nb = lax.shift_right_logical(tot + rows - 1, row_shift)
        lax.fori_loop(0, nb, per_block, 0)
        return b + nb
    n_used = lax.fori_loop(0, N_EXPERTS, per_expert, 0)

    def unused(b, c):
        be_ref[b] = be_ref[n_used - 1]
        nxt_ref[b] = -1
        return c
    lax.fori_loop(n_used, n_blk, unused, 0)

    def next_run(k, nf):
        b = n_used - 1 - k
        nf = jnp.where(be_ref[b] != be_ref[jnp.minimum(b + 1, n_used - 1)], b + 1, nf)
        nxt_ref[b] = nf
        return nf
    lax.fori_loop(0, n_used, next_run, -1)


def _plan(cnt, n_blk):
    n_tiles = cnt.shape[0]
    smem = pl.BlockSpec(memory_space=pltpu.SMEM)
    i32 = lambda n: jax.ShapeDtypeStruct((n,), jnp.int32)
    return pl.pallas_call(
        _plan_kernel,
        in_specs=[smem],
        out_specs=[smem] * 5,
        out_shape=[i32(n_blk), i32(n_blk), i32(n_blk), i32(n_blk * MOE_GROUPS), i32(n_tiles)],
        scratch_shapes=[pltpu.SMEM((n_tiles * N_EXPERTS,), jnp.int32)],
        name="moe_plan",
    )(cnt)


def _pow2_pieces(n, fn):
    for b in reversed(range(SUBLANES.bit_length() - 1, MOE_ROWS.bit_length())):
        size = 1 << b

        @pl.when((n & size) != 0)
        def _():
            fn((n >> (b + 1)) << (b + 1), size)


def _moe_kernel(be_ref, nv_ref, nxt_ref, grp_ref, used_ref,
                w1_hbm, w3_hbm, w2_hbm, xs_hbm, ys_hbm,
                xbuf, ybuf, wb1, wb3, wb2, wst1, wst3, wst2, wslot, gsem, ssem, wsem):
    i = pl.program_id(0)
    last = pl.num_programs(0) - 1
    slot = i % 2
    nv = nv_ref[i]
    half = D_MODEL // 2
    grp_shift = SUBLANES.bit_length() - 1

    def group_row(blk, g):
        return pl.multiple_of(grp_ref[blk * MOE_GROUPS + g], SUBLANES)

    def gather(blk, s):
        top = jnp.maximum(lax.shift_right_logical(nv_ref[blk], grp_shift) - 1, 0)
        for g in range(MOE_GROUPS):
            src = group_row(blk, jnp.minimum(g, top))
            pltpu.make_async_copy(xs_hbm.at[pl.ds(src, SUBLANES)],
                                  xbuf.at[s, pl.ds(g * SUBLANES, SUBLANES)], gsem.at[s]).start()

    def wait_gather(s):
        pltpu.make_async_copy(xs_hbm.at[pl.ds(0, MOE_ROWS)], xbuf.at[s], gsem.at[s]).wait()

    def scatter(blk, s):
        def start(g):
            r = g * SUBLANES if isinstance(g, int) else pl.multiple_of(g * SUBLANES, SUBLANES)
            pltpu.make_async_copy(ybuf.at[s, pl.ds(r, SUBLANES)],
                                  ys_hbm.at[pl.ds(group_row(blk, g), SUBLANES)], ssem.at[s]).start()

        @pl.when(nv_ref[blk] == MOE_ROWS)
        def _():
            for g in range(MOE_GROUPS):
                start(g)

        @pl.when(nv_ref[blk] < MOE_ROWS)
        def _():
            def body(g, c):
                start(g)
                return c
            lax.fori_loop(0, lax.shift_right_logical(nv_ref[blk], grp_shift), body, 0)

    def wait_scatter(s, count):
        _pow2_pieces(count, lambda a, size: pltpu.make_async_copy(
            ybuf.at[s, pl.ds(0, size)], ys_hbm.at[pl.ds(0, size)], ssem.at[s]).wait())

    @pl.when(i == 0)
    def _():
        @pl.when(nv > 0)
        def _():
            gather(0, 0)
            gather(jnp.minimum(1, last), 1)

        ybuf[1] = jnp.zeros(ybuf.shape[1:], ybuf.dtype)
        n_tiles = used_ref.shape[0]

        def fill(t, c):
            row0 = t * TILE_ROWS + used_ref[t]
            _pow2_pieces(TILE_ROWS - used_ref[t], lambda a, size: pltpu.make_async_copy(
                ybuf.at[1, pl.ds(0, size)], ys_hbm.at[pl.ds(pl.multiple_of(row0 + a, SUBLANES), size)],
                ssem.at[1]).start())
            return c
        lax.fori_loop(0, n_tiles, fill, 0)

        def drain(t, c):
            wait_scatter(1, TILE_ROWS - used_ref[t])
            return c
        lax.fori_loop(0, n_tiles, drain, 0)

    @pl.when(i >= 2)
    def _():
        wait_scatter(slot, nv_ref[jnp.maximum(i - 2, 0)])

    xslot = i % GATHER_BUFS
    issuer_used = jnp.where(i >= 2, nv_ref[jnp.maximum(i - 2, 0)], nv_ref[0]) > 0

    @pl.when((nv == 0) & (i > 0) & issuer_used)
    def _():
        wait_gather(xslot)

    @pl.when(nv > 0)
    def _():
        e = be_ref[i]
        e_prev = be_ref[jnp.maximum(i - 1, 0)]

        def weight_copies(ex, ws):
            return [pltpu.make_async_copy(src.at[ex], dst.at[ws], wsem.at[ws])
                    for src, dst in ((w1_hbm, wst1), (w3_hbm, wst3), (w2_hbm, wst2))]

        @pl.when(i == 0)
        def _():
            wslot[0] = 0
            for cp in weight_copies(e, 0):
                cp.start()

        @pl.when((i == 0) | (e != e_prev))
        def _():
            ws = wslot[0]
            for cp in weight_copies(e, ws):
                cp.wait()
            wb1[...] = wst1[ws].astype(BF16)
            wb3[...] = wst3[ws].astype(BF16)
            wb2[...] = wst2[ws].astype(BF16)
            nb = nxt_ref[i]

            @pl.when(nb >= 0)
            def _():
                for cp in weight_copies(be_ref[jnp.maximum(nb, 0)], 1 - ws):
                    cp.start()
            wslot[0] = 1 - ws

        wait_gather(xslot)
        gather(jnp.minimum(i + 2, last), (i + 2) % GATHER_BUFS)
        u = xbuf[xslot]
        xa, xb = _unpack_bf16_pair(u[:, :half])
        wv = pltpu.bitcast(u[:, half:], F32)
        roww = wv[:, 0:1] + wv[:, 1:2] + wv[:, 2:3]
        a = _dot(xa, wb1[:half, :]) + _dot(xb, wb1[half:, :])
        b = _dot(xa, wb3[:half, :]) + _dot(xb, wb3[half:, :])
        hmid = (a * jax.nn.sigmoid(a) * b).astype(BF16)
        y = _dot(hmid, wb2[...]) * roww
        ybuf[slot] = _pack_bf16_pair(y[:, :half], y[:, half:])
        scatter(i, slot)

    @pl.when(i == last)
    def _():
        @pl.when((last >= 1) & (nv_ref[jnp.maximum(last - 1, 0)] > 0))
        def _():
            wait_gather((last + 1) % GATHER_BUFS)

        @pl.when(nv > 0)
        def _():
            wait_gather((last + 2) % GATHER_BUFS)

        @pl.when(last >= 1)
        def _():
            wait_scatter(1 - slot, nv_ref[jnp.maximum(last - 1, 0)])
        wait_scatter(slot, nv)


def _moe(xs, plan, w1, w3, w2):
    n_blk = plan[0].shape[0]
    rows = MOE_ROWS
    half = D_MODEL // 2
    hbm = pl.BlockSpec(memory_space=pl.ANY)
    grid_spec = pltpu.PrefetchScalarGridSpec(
        num_scalar_prefetch=5,
        grid=(n_blk,),
        in_specs=[hbm] * 4,
        out_specs=hbm,
        scratch_shapes=[pltpu.VMEM((GATHER_BUFS, rows, XS_WIDTH), jnp.uint32),
                        pltpu.VMEM((2, rows, half), jnp.uint32),
                        pltpu.VMEM((D_MODEL, EXPERT_HIDDEN), BF16),
                        pltpu.VMEM((D_MODEL, EXPERT_HIDDEN), BF16),
                        pltpu.VMEM((EXPERT_HIDDEN, D_MODEL), BF16),
                        pltpu.VMEM((2, D_MODEL, EXPERT_HIDDEN), F32),
                        pltpu.VMEM((2, D_MODEL, EXPERT_HIDDEN), F32),
                        pltpu.VMEM((2, EXPERT_HIDDEN, D_MODEL), F32),
                        pltpu.SMEM((1,), jnp.int32),
                        pltpu.SemaphoreType.DMA((GATHER_BUFS,)),
                        pltpu.SemaphoreType.DMA((2,)),
                        pltpu.SemaphoreType.DMA((2,))])
    return pl.pallas_call(
        _moe_kernel,
        grid_spec=grid_spec,
        out_shape=jax.ShapeDtypeStruct((xs.shape[0], half), jnp.uint32),
        compiler_params=pltpu.CompilerParams(dimension_semantics=("arbitrary",),
                                             vmem_limit_bytes=VMEM_LIMIT),
        name="moe_ffn",
    )(*plan, w1, w3, w2, xs)


def _combine_kernel(x1_ref, ys_ref, pos_ref, g2_ref, o_ref):
    tt = x1_ref.shape[0]
    half = D_MODEL // 2
    pos = pos_ref[...]
    pcol = lax.broadcasted_iota(jnp.int32, (tt, ys_ref.shape[0]), 1).astype(F32)
    sel = (jnp.where(pcol == pos[:, 0:1], 1.0, 0.0) + jnp.where(pcol == pos[:, 1:2], 1.0, 0.0))
    sel = sel.astype(BF16)
    lo, hi = _unpack_bf16_pair(ys_ref[...])
    g2 = g2_ref[0]
    x1 = x1_ref[...]
    o_ref[:, :half] = x1[:, :half] + g2[:, :half] * _dot(sel, lo)
    o_ref[:, half:] = x1[:, half:] + g2[:, half:] * _dot(sel, hi)


def _combine(x1, ys, pos, g2, S):
    T = x1.shape[0]
    tt = MOE_TILE
    per_b = S // tt
    return pl.pallas_call(
        _combine_kernel,
        grid=(T // tt,),
        in_specs=[pl.BlockSpec((tt, D_MODEL), lambda i: (i, 0)),
                  pl.BlockSpec((TILE_ROWS, D_MODEL // 2), lambda i: (i, 0)),
                  pl.BlockSpec((tt, LANES), lambda i: (i, 0)),
                  pl.BlockSpec((1, 1, D_MODEL), lambda i: (i // per_b, 0, 0))],
        out_specs=pl.BlockSpec((tt, D_MODEL), lambda i: (i, 0)),
        out_shape=jax.ShapeDtypeStruct((T, D_MODEL), F32),
        compiler_params=pltpu.CompilerParams(vmem_limit_bytes=VMEM_LIMIT),
        name="moe_combine",
    )(x1, ys, pos, g2)


def _prep_w_in(w_in):
    dq = w_in[:, OFF_DIL_Q:OFF_DIL_K]
    dk = w_in[:, OFF_DIL_K:OFF_DIL_V]
    dv = w_in[:, OFF_DIL_V:OFF_GATE_A]
    dil = []
    for g in range(N_DIL_GROUPS):
        p0 = slice(g * DIL_OUT_WIDTH, g * DIL_OUT_WIDTH + LANES)
        p1 = slice(g * DIL_OUT_WIDTH + LANES, (g + 1) * DIL_OUT_WIDTH)
        dil += [dq[:, p0], dk[:, p0], dq[:, p1], dk[:, p1], dv[:, p0], dv[:, p1]]
    pad = jnp.zeros((D_MODEL, UNIT - FOX_HEADS), w_in.dtype)
    cols = [w_in[:, OFF_GATE_A:OFF_GATE_B], w_in[:, OFF_GATE_B:N_IN],
            w_in[:, OFF_FOX_Q:OFF_FOX_K], w_in[:, OFF_FOX_K:OFF_FOX_V], w_in[:, OFF_FOX_V:OFF_FOX_F],
            *dil, w_in[:, OFF_FOX_F:OFF_DIL_Q], pad]
    return jnp.concatenate(cols, axis=1).astype(BF16)


def _prep_gain(q_gain, k_gain):
    qs = HEAD_DIM ** -0.5 * LOG2E
    ones = jnp.ones((UNIT,), F32)
    fq = q_gain[:FOX_HEADS].reshape(-1) * qs
    fk = k_gain[:FOX_HEADS].reshape(-1)
    dq = q_gain[FOX_HEADS:].reshape(-1) * qs
    dk = k_gain[FOX_HEADS:].reshape(-1)
    dil = []
    for g in range(N_DIL_GROUPS):
        p0 = slice(g * DIL_OUT_WIDTH, g * DIL_OUT_WIDTH + LANES)
        p1 = slice(g * DIL_OUT_WIDTH + LANES, (g + 1) * DIL_OUT_WIDTH)
        dil += [dq[p0], dk[p0], dq[p1], dk[p1], ones]
    parts = [ones] * 8 + [fq, fk, ones, ones] + dil + [ones]
    return jnp.concatenate(parts).reshape(1, N_UNITS * UNIT)


def _layer(x, mod, rel_bias_table, norm1_g, w_in, b_forget, q_gain, k_gain, w_branch_a, w_branch_b,
           w_out, norm2_g, w_rg, b_rg, w_re, b_re, w1, w3, w2):
    B, S, D = x.shape
    T = B * S
    sh1, sc1, g1, sh2, sc2, g2 = [m.reshape(B, 1, D) for m in jnp.split(mod, 6, axis=-1)]
    x2d = x.reshape(T, D)

    p2d, fgt, slabs = _inproj(x2d, norm1_g.reshape(1, D), sc1, sh1, _prep_w_in(w_in),
                              _prep_gain(q_gain, k_gain), S)
    p3 = p2d.reshape(B, S, P_WIDTH)
    ck = _fcum(fgt.reshape(B, S, LANES), b_forget)
    ya = _fox(p3, ck)
    yd = _dil(slabs, _relbias(rel_bias_table))

    n_router = N_GROUPS + N_EXPERTS
    wr = jnp.concatenate([w_rg, w_re, jnp.zeros((D, LANES - n_router), F32)], axis=1).astype(BF16)
    br = jnp.concatenate([b_rg, b_re, jnp.zeros((LANES - n_router,), F32)]).reshape(1, LANES)
    x1, h2, logits = _outproj(x2d, ya.reshape(T, FOX_WIDTH), yd.reshape(T, DIL_OUT_WIDTH), p2d,
                              g1, sc2, sh2, norm2_g.reshape(1, D),
                              w_branch_a.astype(BF16), w_branch_b.astype(BF16), w_out.astype(BF16),
                              wr, br, S)
    xs, cnt, pos = _dispatch(logits, h2)
    cnt2 = cnt[:, :, 0]
    n_blk = cnt.shape[0] * TILE_ROWS // MOE_ROWS + N_EXPERTS
    plan = _plan(cnt2, n_blk)
    ys = _moe(xs, plan, w1, w3, w2)
    out = _combine(x1, ys, pos, g2, S)
    return out.reshape(B, S, D)


def kernel(x, c, rel_bias_table, w_ada, b_ada, norm1_g, w_in, b_forget, q_gain, k_gain, w_branch_a, w_branch_b, w_out, norm2_g, w_router_group, b_router_group, w_router_expert, b_router_expert, w1, w3, w2):
    depth = w_ada.shape[0]
    for l in range(depth):
        mod = _ada(c, w_ada[l], b_ada[l])
        x = _layer(x, mod, rel_bias_table, norm1_g[l], w_in[l], b_forget[l], q_gain[l], k_gain[l],
                   w_branch_a[l], w_branch_b[l], w_out[l], norm2_g[l], w_router_group[l],
                   b_router_group[l], w_router_expert[l], b_router_expert[l], w1[l], w3[l], w2[l])
    return x
```

```python
import functools
import math

import numpy as np
import jax
import jax.numpy as jnp
from jax import lax
from jax.experimental import pallas as pl
from jax.experimental.pallas import tpu as pltpu

F32 = jnp.float32
BF16 = jnp.bfloat16

D_MODEL = 1024
HEAD_DIM = 64
FOX_HEADS = 8
DIL_GROUPS = ((128, 1), (512, 4), (2048, 16))
DIL_HEADS_PER_GROUP = 4
N_DIL_GROUPS = len(DIL_GROUPS)
DIL_HEADS = N_DIL_GROUPS * DIL_HEADS_PER_GROUP
FOX_WIDTH = FOX_HEADS * HEAD_DIM
DIL_WIDTH = DIL_HEADS * HEAD_DIM
DIL_OUT_WIDTH = DIL_HEADS_PER_GROUP * HEAD_DIM
NUM_BUCKETS = 32
REL_MAX_DISTANCE = 2048
N_GROUPS = 4
EXPERTS_PER_GROUP = 8
N_EXPERTS = N_GROUPS * EXPERTS_PER_GROUP
EXPERT_HIDDEN = D_MODEL // 2
EPS = 1e-6
LOG2E = math.log2(math.e)

OFF_FOX_Q = 0
OFF_FOX_K = OFF_FOX_Q + FOX_WIDTH
OFF_FOX_V = OFF_FOX_K + FOX_WIDTH
OFF_FOX_F = OFF_FOX_V + FOX_WIDTH
OFF_DIL_Q = OFF_FOX_F + FOX_HEADS
OFF_DIL_K = OFF_DIL_Q + DIL_WIDTH
OFF_DIL_V = OFF_DIL_K + DIL_WIDTH
OFF_GATE_A = OFF_DIL_V + DIL_WIDTH
OFF_GATE_B = OFF_GATE_A + D_MODEL
N_IN = OFF_GATE_B + D_MODEL

LANES = 128
UNIT = 256
DIL_L = 128

U_GATE_A, U_GATE_B, U_FOX_Q, U_FOX_K, U_FOX_V, U_DIL, U_FORGET = 0, 4, 8, 10, 12, 14, 23
N_UNITS = 24
P_WIDTH = U_DIL * UNIT
N_SLABS = 3 * N_DIL_GROUPS
_KIND = (["gate"] * 8 + ["norm"] * 4 + ["plain"] * 2 + ["norm", "norm", "plain"] * 3 + ["forget"])

TM_INPROJ = 1024
TM_PROJ = 1024
TQ_FOX = 512
MOE_ROWS = 256
MOE_TILE = 512
XS_WIDTH = D_MODEL // 2 + LANES
SUBLANES = 8
TILE_ROWS = 2 * MOE_TILE + N_EXPERTS * SUBLANES
MOE_GROUPS = MOE_ROWS // SUBLANES
GATHER_BUFS = 3
VMEM_LIMIT = 56 * 1024 * 1024


def _dot(a, b):
    return jnp.dot(a, b, preferred_element_type=F32)


def _dot_nt(a, b):
    return lax.dot_general(a, b, (((1,), (1,)), ((), ())), preferred_element_type=F32)


def _split3(x):
    hi = x.astype(BF16)
    r1 = x - hi.astype(F32)
    mid = r1.astype(BF16)
    lo = (r1 - mid.astype(F32)).astype(BF16)
    return hi, mid, lo


def _ada_kernel(c_ref, w_ref, b_ref, o_ref):
    c = c_ref[...]
    s = c * jax.nn.sigmoid(c)
    s_hi = s.astype(BF16)
    s_lo = (s - s_hi.astype(F32)).astype(BF16)
    w = w_ref[...]
    w_hi = w.astype(BF16)
    w_lo = (w - w_hi.astype(F32)).astype(BF16)
    acc = _dot(s_hi, w_hi) + _dot(s_hi, w_lo) + _dot(s_lo, w_hi)
    o_ref[...] = acc + b_ref[...]


def _ada(c, w_ada, b_ada):
    B = c.shape[0]
    n_out = w_ada.shape[1]
    tn = 512
    return pl.pallas_call(
        _ada_kernel,
        grid=(n_out // tn,),
        in_specs=[pl.BlockSpec((B, D_MODEL), lambda j: (0, 0)),
                  pl.BlockSpec((D_MODEL, tn), lambda j: (0, j)),
                  pl.BlockSpec((1, tn), lambda j: (0, j))],
        out_specs=pl.BlockSpec((B, tn), lambda j: (0, j)),
        out_shape=jax.ShapeDtypeStruct((B, n_out), F32),
        name="ada_mod",
    )(c, w_ada, b_ada.reshape(1, n_out))


def _pack_bf16_pair(lo, hi):
    lo_bits = pltpu.bitcast(lo.astype(BF16).astype(F32), jnp.uint32) >> 16
    hi_bits = pltpu.bitcast(hi.astype(BF16).astype(F32), jnp.uint32) & jnp.uint32(0xFFFF0000)
    return lo_bits | hi_bits


def _unpack_bf16_pair(u):
    lo = pltpu.bitcast(u << 16, F32).astype(BF16)
    hi = pltpu.bitcast(u & jnp.uint32(0xFFFF0000), F32).astype(BF16)
    return lo, hi


def _inproj_kernel(x_ref, g_ref, sc_ref, sh_ref, w_ref, gain_ref, bd_ref, p_ref, f_ref, s_ref):
    x = x_ref[...]
    ms = jnp.mean(x * x, axis=-1, keepdims=True)
    h = x * lax.rsqrt(ms + EPS) * g_ref[...]
    h = h * (1.0 + sc_ref[0]) + sh_ref[0]
    hb = h.astype(BF16)

    def unit(u):
        cols = slice(u * UNIT, (u + 1) * UNIT)
        acc = _dot(hb, w_ref[:, cols])
        kind = _KIND[u]
        if kind == "gate":
            return jax.nn.sigmoid(acc)
        if kind == "norm":
            ss = _dot((acc * acc).astype(BF16), bd_ref[...])
            return acc * lax.rsqrt(ss * (1.0 / HEAD_DIM) + EPS) * gain_ref[:, cols]
        return acc

    def emit(u):
        o = unit(u)
        if u < U_DIL:
            p_ref[:, u * UNIT:(u + 1) * UNIT] = o.astype(BF16)
        elif u < U_FORGET:
            s_ref[0, u - U_DIL] = _pack_bf16_pair(o[:, :LANES], o[:, LANES:])
        else:
            f_ref[...] = o[:, :LANES]

    normed = [u for u in range(N_UNITS) if _KIND[u] == "norm"]
    others = [u for u in range(N_UNITS) if _KIND[u] != "norm"]
    while normed or others:
        for group in (normed, others):
            if group:
                emit(group.pop(0))


def _inproj(x2d, norm_g, sc, sh, w_re, gain_row, S):
    T = x2d.shape[0]
    tm = TM_INPROJ
    per_b = S // tm
    bd = np.kron(np.eye(UNIT // HEAD_DIM), np.ones((HEAD_DIM, HEAD_DIM))).astype(np.float32)
    once = dict(pipeline_mode=pl.Buffered(1))
    return pl.pallas_call(
        _inproj_kernel,
        grid=(T // tm,),
        in_specs=[pl.BlockSpec((tm, D_MODEL), lambda i: (i, 0)),
                  pl.BlockSpec((1, D_MODEL), lambda i: (0, 0)),
                  pl.BlockSpec((1, 1, D_MODEL), lambda i: (i // per_b, 0, 0)),
                  pl.BlockSpec((1, 1, D_MODEL), lambda i: (i // per_b, 0, 0)),
                  pl.BlockSpec((D_MODEL, N_UNITS * UNIT), lambda i: (0, 0), **once),
                  pl.BlockSpec((1, N_UNITS * UNIT), lambda i: (0, 0), **once),
                  pl.BlockSpec((UNIT, UNIT), lambda i: (0, 0), **once)],
        out_specs=[pl.BlockSpec((tm, P_WIDTH), lambda i: (i, 0)),
                   pl.BlockSpec((tm, LANES), lambda i: (i, 0)),
                   pl.BlockSpec((1, N_SLABS, tm, LANES), lambda i: (i // per_b, 0, i % per_b, 0))],
        out_shape=[jax.ShapeDtypeStruct((T, P_WIDTH), BF16),
                   jax.ShapeDtypeStruct((T, LANES), F32),
                   jax.ShapeDtypeStruct((T // S, N_SLABS, S, LANES), jnp.uint32)],
        compiler_params=pltpu.CompilerParams(vmem_limit_bytes=VMEM_LIMIT),
        name="in_proj",
    )(x2d, norm_g, sc, sh, w_re, gain_row, jnp.asarray(bd, BF16))


def _fcum_kernel(f_ref, b_ref, tri_ref, o_ref):
    S = f_ref.shape[1]
    xf = f_ref[0] + b_ref[...]
    ls = (jnp.minimum(xf, 0.0) - jnp.log(1.0 + jnp.exp(-jnp.abs(xf)))) * LOG2E
    lst = ls.T
    carry = jnp.zeros((LANES, UNIT), F32)
    for blk in range(S // UNIT):
        seg = lst[:, blk * UNIT:(blk + 1) * UNIT]
        hi, mid, lo = _split3(seg)
        tri = tri_ref[...]
        res = _dot(hi, tri) + _dot(mid, tri) + _dot(lo, tri)
        o_ref[0, :, blk * UNIT:(blk + 1) * UNIT] = (res[:, :UNIT] + carry)[:FOX_HEADS]
        carry = carry + res[:, UNIT:]


def _fcum(fgt, b_forget):
    B, S, _ = fgt.shape
    brow = jnp.zeros((1, LANES), F32).at[0, :FOX_HEADS].set(b_forget)
    tri = np.concatenate([np.triu(np.ones((UNIT, UNIT))), np.ones((UNIT, UNIT))], axis=1)
    return pl.pallas_call(
        _fcum_kernel,
        grid=(B,),
        in_specs=[pl.BlockSpec((1, S, LANES), lambda b: (b, 0, 0)),
                  pl.BlockSpec((1, LANES), lambda b: (0, 0)),
                  pl.BlockSpec((UNIT, 2 * UNIT), lambda b: (0, 0))],
        out_specs=pl.BlockSpec((1, FOX_HEADS, S), lambda b: (b, 0, 0)),
        out_shape=jax.ShapeDtypeStruct((B, FOX_HEADS, S), F32),
        name="forget_cumsum",
    )(fgt, brow, jnp.asarray(tri, BF16))


def _fox_kernel(q_ref, k_ref, v_ref, ck_ref, o_ref):
    S = q_ref.shape[1]
    pair = pl.program_id(1)
    tq = TQ_FOX
    lane = lax.broadcasted_iota(jnp.int32, (1, LANES), 1)
    row = lax.broadcasted_iota(jnp.int32, (tq, tq), 0)
    col = lax.broadcasted_iota(jnp.int32, (tq, tq), 1)
    causal = col <= row
    cks = [ck_ref[0, pl.ds(2 * pair + hh, 1), :] for hh in range(2)]
    for t in reversed(range(S // tq)):
        r0, r1 = t * tq, (t + 1) * tq
        qt = q_ref[0, r0:r1, :]
        outs = []
        for hh in range(2):
            hsel = (lane >= HEAD_DIM) == bool(hh)
            qm = jnp.where(hsel, qt, jnp.zeros_like(qt))
            ck = cks[hh]
            s = _dot_nt(qm, k_ref[0, :r1, :]) - ck[:, :r1]
            s_d = jnp.where(causal, s[:, r0:], -jnp.inf)
            s = jnp.concatenate([s[:, :r0], s_d], axis=1) if t > 0 else s_d
            m = jnp.max(s, axis=-1, keepdims=True)
            p = jnp.exp2(s - m)
            l = jnp.sum(p, axis=-1, keepdims=True)
            outs.append(_dot(p.astype(BF16), v_ref[0, :r1, :]) / l)
        o_ref[0, r0:r1, :] = jnp.where(lane < HEAD_DIM, outs[0], outs[1]).astype(BF16)


def _fox(p3, ck):
    B, S, _ = p3.shape
    nq, nk, nv = (U_FOX_Q * UNIT // LANES, U_FOX_K * UNIT // LANES, U_FOX_V * UNIT // LANES)
    return pl.pallas_call(
        _fox_kernel,
        grid=(B, FOX_HEADS // 2),
        in_specs=[pl.BlockSpec((1, S, LANES), lambda b, p: (b, 0, nq + p)),
                  pl.BlockSpec((1, S, LANES), lambda b, p: (b, 0, nk + p)),
                  pl.BlockSpec((1, S, LANES), lambda b, p: (b, 0, nv + p)),
                  pl.BlockSpec((1, FOX_HEADS, S), lambda b, p: (b, 0, 0))],
        out_specs=pl.BlockSpec((1, S, LANES), lambda b, p: (b, 0, p)),
        out_shape=jax.ShapeDtypeStruct((B, S, FOX_WIDTH), BF16),
        compiler_params=pltpu.CompilerParams(vmem_limit_bytes=VMEM_LIMIT),
        name="fox_attn",
    )(p3, p3, p3, ck)


def _t5_bucket(dist):
    max_exact = NUM_BUCKETS // 2
    d = np.maximum(dist, 1).astype(np.float32)
    large = max_exact + (np.log(d / max_exact) / np.log(REL_MAX_DISTANCE / max_exact)
                         * (NUM_BUCKETS - max_exact)).astype(np.int32)
    large = np.minimum(large, NUM_BUCKETS - 1)
    return np.where(dist < max_exact, dist, large).astype(np.int32)


def _relbias_kernel(tab_ref, bucket_ref, valid_ref, o_ref):
    g = pl.program_id(0)
    bk = bucket_ref[0]
    vd = valid_ref[0]
    for hs in range(DIL_HEADS_PER_GROUP):
        acc = jnp.zeros(bk.shape, F32)
        for b in range(NUM_BUCKETS):
            acc = jnp.where(bk == b, tab_ref[b, g * DIL_HEADS_PER_GROUP + hs], acc)
        bias = jnp.where(vd != 0, acc * LOG2E, -jnp.inf)
        o_ref[0, hs] = bias
        col = lax.broadcasted_iota(jnp.int32, bias.shape, 1)
        o_ref[1, hs] = jnp.where(col >= DIL_L, bias, -jnp.inf)


def _relbias(table):
    L = DIL_L
    i = np.arange(L)[:, None]
    j = np.arange(2 * L)[None, :]
    m = L + i - j
    valid = ((m >= 0) & (m <= L)).astype(np.int32)
    buckets = np.stack([_t5_bucket(np.clip(m, 0, None) * d) for _, d in DIL_GROUPS])
    valids = np.stack([valid] * N_DIL_GROUPS)
    return pl.pallas_call(
        _relbias_kernel,
        grid=(N_DIL_GROUPS,),
        in_specs=[pl.BlockSpec(memory_space=pltpu.SMEM),
                  pl.BlockSpec((1, L, 2 * L), lambda g: (g, 0, 0)),
                  pl.BlockSpec((1, L, 2 * L), lambda g: (g, 0, 0))],
        out_specs=pl.BlockSpec((2, DIL_HEADS_PER_GROUP, L, 2 * L), lambda g: (0, g, 0, 0)),
        out_shape=jax.ShapeDtypeStruct((2, DIL_HEADS, L, 2 * L), F32),
        name="rel_bias",
    )(table, jnp.asarray(buckets), jnp.asarray(valids))


def _dil_rows(start, d):
    return pl.ds(start, DIL_L) if d == 1 else pl.ds(start, DIL_L, stride=d)


def _dil_block_rows(d, nb, it):
    r, n = it // nb, it % nb
    cur = _dil_rows(r + d * (n * DIL_L), d)
    prev = _dil_rows(r + d * (jnp.maximum(n - 1, 0) * DIL_L), d)
    return cur, prev, 1 - jnp.minimum(n, 1)


def _dil_scores(qkv_ref, bias_ref, s_scr, slot, g, d, nb, it):
    lane = lax.broadcasted_iota(jnp.int32, (1, LANES), 1)
    cur, prev, first = _dil_block_rows(d, nb, it)
    for pr in range(2):
        qt, kt = _unpack_bf16_pair(qkv_ref[0, 3 * g + pr, cur, :])
        if nb > 1:
            _, k_prev = _unpack_bf16_pair(qkv_ref[0, 3 * g + pr, prev, :])
            kt = jnp.concatenate([k_prev, kt], axis=0)
        for hh in range(2):
            hsel = (lane >= HEAD_DIM) == bool(hh)
            qm = jnp.where(hsel, qt, jnp.zeros_like(qt))
            head = 2 * pr + hh
            if nb > 1:
                s_scr[slot, head] = _dot_nt(qm, kt) + bias_ref[first, DIL_HEADS_PER_GROUP * g + head]
            else:
                bias = bias_ref[0, DIL_HEADS_PER_GROUP * g + head, :, DIL_L:]
                s_scr[slot, head, :, :DIL_L] = _dot_nt(qm, kt) + bias


def _dil_merge(qkv_ref, s_scr, slot, m_scr, l_scr, acc_scr, g, d, nb, init, it):
    lane = lax.broadcasted_iota(jnp.int32, (1, LANES), 1)
    cur, prev, _ = _dil_block_rows(d, nb, it)
    v_cur = _unpack_bf16_pair(qkv_ref[0, 3 * g + 2, cur, :])
    if nb > 1:
        v_prev = _unpack_bf16_pair(qkv_ref[0, 3 * g + 2, prev, :])
    for pr in range(2):
        vt = jnp.concatenate([v_prev[pr], v_cur[pr]], axis=0) if nb > 1 else v_cur[pr]
        ms, ls, accs = [], [], []
        for hh in range(2):
            s = s_scr[slot, 2 * pr + hh] if nb > 1 else s_scr[slot, 2 * pr + hh, :, :DIL_L]
            m = jnp.max(s, axis=-1, keepdims=True)
            p = jnp.exp2(s - m)
            ms.append(m)
            ls.append(jnp.sum(p, axis=-1, keepdims=True))
            accs.append(_dot(p.astype(BF16), vt))
        low = lane < HEAD_DIM
        m_b = jnp.where(low, ms[0], ms[1])
        l_b = jnp.where(low, ls[0], ls[1])
        acc_b = jnp.where(low, accs[0], accs[1])
        if init:
            m_scr[pr, cur, :] = m_b
            l_scr[pr, cur, :] = l_b
            acc_scr[pr, cur, :] = acc_b
        else:
            m_o = m_scr[pr, cur, :]
            m_n = jnp.maximum(m_o, m_b)
            a_o = jnp.exp2(m_o - m_n)
            a_b = jnp.exp2(m_b - m_n)
            m_scr[pr, cur, :] = m_n
            l_scr[pr, cur, :] = l_scr[pr, cur, :] * a_o + l_b * a_b
            acc_scr[pr, cur, :] = acc_scr[pr, cur, :] * a_o + acc_b * a_b


def _dil_kernel(qkv_ref, bias_ref, o_ref, m_scr, l_scr, acc_scr, s_scr):
    S = o_ref.shape[1]
    order = sorted(range(N_DIL_GROUPS), key=lambda g: -DIL_GROUPS[g][1])
    for g in order:
        window, d = DIL_GROUPS[g]
        nb = S // window
        total = d * nb
        assert total % 2 == 0
        scores = functools.partial(_dil_scores, qkv_ref, bias_ref, s_scr, g=g, d=d, nb=nb)
        merge = functools.partial(_dil_merge, qkv_ref, s_scr, m_scr=m_scr, l_scr=l_scr,
                                  acc_scr=acc_scr, g=g, d=d, nb=nb, init=g == order[0])

        scores(slot=0, it=0)

        def body(j, carry, scores=scores, merge=merge, total=total):
            scores(slot=1, it=2 * j + 1)
            merge(slot=0, it=2 * j)
            scores(slot=0, it=jnp.minimum(2 * j + 2, total - 1))
            merge(slot=1, it=2 * j + 1)
            return carry
        lax.fori_loop(0, total // 2, body, 0, unroll=2)
    for pr in range(2):
        o_ref[0, :, pr * LANES:(pr + 1) * LANES] = (acc_scr[pr] / l_scr[pr]).astype(BF16)


def _dil(slabs, bias):
    B, _, S, _ = slabs.shape
    for window, d in DIL_GROUPS:
        assert window // d == DIL_L and S % window == 0
    stat = pltpu.VMEM((2, S, LANES), F32)
    return pl.pallas_call(
        _dil_kernel,
        grid=(B,),
        in_specs=[pl.BlockSpec((1, N_SLABS, S, LANES), lambda b: (b, 0, 0, 0)),
                  pl.BlockSpec(bias.shape, lambda b: (0, 0, 0, 0))],
        out_specs=pl.BlockSpec((1, S, DIL_OUT_WIDTH), lambda b: (b, 0, 0)),
        out_shape=jax.ShapeDtypeStruct((B, S, DIL_OUT_WIDTH), BF16),
        scratch_shapes=[stat, stat, stat,
                        pltpu.VMEM((2, DIL_HEADS_PER_GROUP, DIL_L, 2 * DIL_L), F32)],
        compiler_params=pltpu.CompilerParams(vmem_limit_bytes=VMEM_LIMIT),
        name="dil_attn",
    )(slabs, bias)


def _outproj_kernel(x_ref, ya_ref, yd_ref, ga_ref, gb_ref,
                    g1_ref, sc_ref, sh_ref, ng_ref, wa_ref, wb_ref, wo_ref, wr_ref, br_ref,
                    x1_ref, h2_ref, lg_ref):
    n_chunks = 2
    cm = x_ref.shape[0] // n_chunks
    for c in range(n_chunks):
        rows = slice(c * cm, (c + 1) * cm)
        a = _dot(ya_ref[rows, :], wa_ref[...])
        bm = _dot(yd_ref[rows, :], wb_ref[...])
        merged = ga_ref[rows, :].astype(F32) * a + gb_ref[rows, :].astype(F32) * bm
        out = _dot(merged.astype(BF16), wo_ref[...])
        x1 = x_ref[rows, :] + g1_ref[0] * out
        x1_ref[rows, :] = x1
        ms = jnp.mean(x1 * x1, axis=-1, keepdims=True)
        h = x1 * lax.rsqrt(ms + EPS) * ng_ref[...]
        h = h * (1.0 + sc_ref[0]) + sh_ref[0]
        hb = h.astype(BF16)
        h2_ref[rows, :] = hb
        lg_ref[rows, :] = _dot(hb, wr_ref[...]) + br_ref[...]


def _outproj(x2d, ya2d, yd2d, p2d, g1, sc2, sh2, norm_g, wa, wb, wo, wr, br, S):
    T = x2d.shape[0]
    tm = TM_PROJ
    per_b = S // tm
    row = lambda w: pl.BlockSpec((tm, w), lambda i: (i, 0))
    full = lambda a: pl.BlockSpec(a.shape, lambda i: (0,) * a.ndim)
    mod = pl.BlockSpec((1, 1, D_MODEL), lambda i: (i // per_b, 0, 0))
    return pl.pallas_call(
        _outproj_kernel,
        grid=(T // tm,),
        in_specs=[row(D_MODEL), row(FOX_WIDTH), row(DIL_OUT_WIDTH)]
                 + [pl.BlockSpec((tm, D_MODEL), lambda i: (i, U_GATE_A * UNIT // D_MODEL)),
                    pl.BlockSpec((tm, D_MODEL), lambda i: (i, U_GATE_B * UNIT // D_MODEL)),
                    mod, mod, mod, full(norm_g), full(wa), full(wb), full(wo), full(wr), full(br)],
        out_specs=[row(D_MODEL), row(D_MODEL), row(LANES)],
        out_shape=[jax.ShapeDtypeStruct((T, D_MODEL), F32),
                   jax.ShapeDtypeStruct((T, D_MODEL), BF16),
                   jax.ShapeDtypeStruct((T, LANES), F32)],
        compiler_params=pltpu.CompilerParams(vmem_limit_bytes=VMEM_LIMIT),
        name="out_proj",
    )(x2d, ya2d, yd2d, p2d, p2d, g1, sc2, sh2, norm_g, wa, wb, wo, wr, br)


def _dispatch_kernel(lg_ref, h_ref, tri_ref, xs_ref, cnt_ref, pos_ref):
    tt = lg_ref.shape[0]
    lt = lg_ref[...].T
    row = lambda i: lt[i:i + 1, :]
    neg = -jnp.inf
    g = [row(i) for i in range(N_GROUPS)]
    gmax = functools.reduce(jnp.maximum, g)
    gidx = jnp.full(gmax.shape, N_GROUPS - 1, jnp.int32)
    for i in reversed(range(N_GROUPS - 1)):
        gidx = jnp.where(g[i] == gmax, i, gidx)
    gsum = sum(jnp.exp(gi - gmax) for gi in g)
    el = []
    for j in range(EXPERTS_PER_GROUP):
        v = row(N_GROUPS + EXPERTS_PER_GROUP * (N_GROUPS - 1) + j)
        for gg in reversed(range(N_GROUPS - 1)):
            v = jnp.where(gidx == gg, row(N_GROUPS + EXPERTS_PER_GROUP * gg + j), v)
        el.append(v)

    def top(vals):
        best = functools.reduce(jnp.maximum, vals)
        idx = jnp.full(best.shape, EXPERTS_PER_GROUP - 1, jnp.int32)
        for j in reversed(range(EXPERTS_PER_GROUP - 1)):
            idx = jnp.where(vals[j] == best, j, idx)
        return best, idx

    v1, i1 = top(el)
    v2, i2 = top([jnp.where(i1 == j, neg, el[j]) for j in range(EXPERTS_PER_GROUP)])
    t = jnp.exp(v2 - v1)
    den = (1.0 + t) * gsum
    wts = [1.0 / den, t / den]
    eid = [gidx * EXPERTS_PER_GROUP + i1, gidx * EXPERTS_PER_GROUP + i2]

    esub = lax.broadcasted_iota(jnp.int32, (N_EXPERTS, tt), 0)
    ohf = jnp.concatenate([jnp.where(esub == eid[k], 1.0, 0.0) for k in range(2)], axis=1)
    n_pb = 2 * tt // UNIT
    oh_blocks = jnp.concatenate([ohf[:, b * UNIT:(b + 1) * UNIT] for b in range(n_pb)], axis=0)
    res = _dot(oh_blocks.astype(BF16), tri_ref[...])
    cnt = jnp.zeros((N_EXPERTS, LANES), F32)
    pre = []
    for b in range(n_pb):
        r = res[b * N_EXPERTS:(b + 1) * N_EXPERTS]
        pre.append(r[:, :UNIT] + jnp.concatenate([cnt] * (UNIT // LANES), axis=1))
        cnt = cnt + r[:, UNIT:]
    prefix = jnp.concatenate(pre, axis=1)
    cnt = (((cnt.astype(jnp.int32) + (SUBLANES - 1)) // SUBLANES) * SUBLANES).astype(F32)
    esub_c = lax.broadcasted_iota(jnp.int32, cnt.shape, 0)
    start = jnp.zeros_like(cnt)
    for e in range(N_EXPERTS - 1):
        start = start + jnp.where(esub_c > e, cnt[e:e + 1, :], 0.0)
    start_w = jnp.concatenate([start] * (2 * tt // LANES), axis=1)
    pos = jnp.sum(ohf * (start_w + prefix), axis=0, keepdims=True)
    pos_k = [pos[:, :tt], pos[:, tt:]]

    n_rows = xs_ref.shape[0]
    psub = lax.broadcasted_iota(jnp.int32, (n_rows, tt), 0).astype(F32)
    pm = [jnp.where(psub == pos_k[k], 1.0, 0.0).astype(BF16) for k in range(2)]
    xs = _dot(pm[0] + pm[1], h_ref[...])
    wsub = lax.broadcasted_iota(jnp.int32, (LANES, tt), 0)
    ws = jnp.zeros((n_rows, LANES), F32)
    for k in range(2):
        parts = _split3(wts[k])
        wrows = jnp.zeros((LANES, tt), F32)
        for j in range(3):
            wrows = jnp.where(wsub == j, parts[j].astype(F32), wrows)
        ws = ws + _dot_nt(pm[k], wrows.astype(BF16))
    half = D_MODEL // 2
    xs_ref[:, :half] = _pack_bf16_pair(xs[:, :half], xs[:, half:])
    xs_ref[:, half:] = pltpu.bitcast(ws, jnp.uint32)
    cnt_ref[0] = cnt.astype(jnp.int32)
    posr = jnp.where(wsub == 0, pos_k[0], jnp.where(wsub == 1, pos_k[1], 0.0))
    pos_ref[...] = posr.T


def _dispatch(logits, h2):
    T = logits.shape[0]
    tt = MOE_TILE
    n_tiles = T // tt
    tri = np.concatenate([np.triu(np.ones((UNIT, UNIT)), 1), np.ones((UNIT, LANES))], axis=1)
    return pl.pallas_call(
        _dispatch_kernel,
        grid=(n_tiles,),
        in_specs=[pl.BlockSpec((tt, LANES), lambda i: (i, 0)),
                  pl.BlockSpec((tt, D_MODEL), lambda i: (i, 0)),
                  pl.BlockSpec(tri.shape, lambda i: (0, 0))],
        out_specs=[pl.BlockSpec((TILE_ROWS, XS_WIDTH), lambda i: (i, 0)),
                   pl.BlockSpec((1, N_EXPERTS, LANES), lambda i: (i, 0, 0)),
                   pl.BlockSpec((tt, LANES), lambda i: (i, 0))],
        out_shape=[jax.ShapeDtypeStruct((n_tiles * TILE_ROWS, XS_WIDTH), jnp.uint32),
                   jax.ShapeDtypeStruct((n_tiles, N_EXPERTS, LANES), jnp.int32),
                   jax.ShapeDtypeStruct((T, LANES), F32)],
        compiler_params=pltpu.CompilerParams(vmem_limit_bytes=VMEM_LIMIT),
        name="moe_dispatch",
    )(logits, h2, jnp.asarray(tri, BF16))


def _plan_kernel(cnt_ref, be_ref, nv_ref, nxt_ref, grp_ref, used_ref, cs_ref):
    n_tiles = cnt_ref.shape[0]
    n_blk = be_ref.shape[0]
    rows = MOE_ROWS
    row_shift = rows.bit_length() - 1
    grp_shift = SUBLANES.bit_length() - 1
    assert rows == 1 << row_shift and SUBLANES == 1 << grp_shift

    def tile_starts(t, c):
        def per_e(e, acc):
            cs_ref[t * N_EXPERTS + e] = acc
            return acc + cnt_ref[t, e]
        used_ref[t] = lax.fori_loop(0, N_EXPERTS, per_e, 0, unroll=8)
        return c
    lax.fori_loop(0, n_tiles, tile_starts, 0)

    def clear(b, c):
        nv_ref[b] = 0
        return c
    lax.fori_loop(0, n_blk, clear, 0)

    def clear_groups(g, c):
        grp_ref[g] = 0
        return c
    lax.fori_loop(0, n_blk * MOE_GROUPS, clear_groups, 0, unroll=8)

    def per_expert(e, b):
        g0 = b * MOE_GROUPS

        def per_tile(t, tot):
            c = cnt_ref[t, e]
            src = t * TILE_ROWS + cs_ref[t * N_EXPERTS + e]
            first = g0 + lax.shift_right_logical(tot, grp_shift)

            def per_group(k, cc):
                grp_ref[first + k] = src + k * SUBLANES
                return cc
            lax.fori_loop(0, lax.shift_right_logical(c, grp_shift), per_group, 0)
            return tot + c
        tot = lax.fori_loop(0, n_tiles, per_tile, 0)

        def per_block(j, c):
            be_ref[b + j] = e
            nv_ref[b + j] = jnp.minimum(rows, tot - j * rows)
            return c
        nb = lax.shift_right_logical(tot + rows - 1, row_shift)
        lax.fori_loop(0, nb, per_block, 0)
        return b + nb
    n_used = lax.fori_loop(0, N_EXPERTS, per_expert, 0)

    def unused(b, c):
        be_ref[b] = be_ref[n_used - 1]
        nxt_ref[b] = -1
        return c
    lax.fori_loop(n_used, n_blk, unused, 0)

    def next_run(k, nf):
        b = n_used - 1 - k
        nf = jnp.where(be_ref[b] != be_ref[jnp.minimum(b + 1, n_used - 1)], b + 1, nf)
        nxt_ref[b] = nf
        return nf
    lax.fori_loop(0, n_used, next_run, -1)


def _plan(cnt, n_blk):
    n_tiles = cnt.shape[0]
    smem = pl.BlockSpec(memory_space=pltpu.SMEM)
    i32 = lambda n: jax.ShapeDtypeStruct((n,), jnp.int32)
    return pl.pallas_call(
        _plan_kernel,
        in_specs=[smem],
        out_specs=[smem] * 5,
        out_shape=[i32(n_blk), i32(n_blk), i32(n_blk), i32(n_blk * MOE_GROUPS), i32(n_tiles)],
        scratch_shapes=[pltpu.SMEM((n_tiles * N_EXPERTS,), jnp.int32)],
        name="moe_plan",
    )(cnt)


def _pow2_pieces(n, fn):
    for b in reversed(range(SUBLANES.bit_length() - 1, MOE_ROWS.bit_length())):
        size = 1 << b

        @pl.when((n & size) != 0)
        def _():
            fn((n >> (b + 1)) << (b + 1), size)


def _moe_kernel(be_ref, nv_ref, nxt_ref, grp_ref, used_ref,
                w1_hbm, w3_hbm, w2_hbm, xs_hbm, ys_hbm,
                xbuf, ybuf, wb1, wb3, wb2, wst1, wst3, wst2, wslot, gsem, ssem, wsem):
    i = pl.program_id(0)
    last = pl.num_programs(0) - 1
    slot = i % 2
    nv = nv_ref[i]
    half = D_MODEL // 2
    grp_shift = SUBLANES.bit_length() - 1

    def group_row(blk, g):
        return pl.multiple_of(grp_ref[blk * MOE_GROUPS + g], SUBLANES)

    def gather(blk, s):
        top = jnp.maximum(lax.shift_right_logical(nv_ref[blk], grp_shift) - 1, 0)
        for g in range(MOE_GROUPS):
            src = group_row(blk, jnp.minimum(g, top))
            pltpu.make_async_copy(xs_hbm.at[pl.ds(src, SUBLANES)],
                                  xbuf.at[s, pl.ds(g * SUBLANES, SUBLANES)], gsem.at[s]).start()

    def wait_gather(s):
        pltpu.make_async_copy(xs_hbm.at[pl.ds(0, MOE_ROWS)], xbuf.at[s], gsem.at[s]).wait()

    def scatter(blk, s):
        def start(g):
            r = g * SUBLANES if isinstance(g, int) else pl.multiple_of(g * SUBLANES, SUBLANES)
            pltpu.make_async_copy(ybuf.at[s, pl.ds(r, SUBLANES)],
                                  ys_hbm.at[pl.ds(group_row(blk, g), SUBLANES)], ssem.at[s]).start()

        @pl.when(nv_ref[blk] == MOE_ROWS)
        def _():
            for g in range(MOE_GROUPS):
                start(g)

        @pl.when(nv_ref[blk] < MOE_ROWS)
        def _():
            def body(g, c):
                start(g)
                return c
            lax.fori_loop(0, lax.shift_right_logical(nv_ref[blk], grp_shift), body, 0)

    def wait_scatter(s, count):
        @pl.when(count == MOE_ROWS)
        def _():
            pltpu.make_async_copy(ybuf.at[s], ys_hbm.at[pl.ds(0, MOE_ROWS)], ssem.at[s]).wait()

        @pl.when(count < MOE_ROWS)
        def _():
            _pow2_pieces(count, lambda a, size: pltpu.make_async_copy(
                ybuf.at[s, pl.ds(0, size)], ys_hbm.at[pl.ds(0, size)], ssem.at[s]).wait())

    @pl.when(i == 0)
    def _():
        @pl.when(nv > 0)
        def _():
            gather(0, 0)
            gather(jnp.minimum(1, last), 1)

        ybuf[1] = jnp.zeros(ybuf.shape[1:], ybuf.dtype)
        n_tiles = used_ref.shape[0]

        def fill(t, c):
            row0 = t * TILE_ROWS + used_ref[t]
            _pow2_pieces(TILE_ROWS - used_ref[t], lambda a, size: pltpu.make_async_copy(
                ybuf.at[1, pl.ds(0, size)], ys_hbm.at[pl.ds(pl.multiple_of(row0 + a, SUBLANES), size)],
                ssem.at[1]).start())
            return c
        lax.fori_loop(0, n_tiles, fill, 0)

        def drain(t, c):
            wait_scatter(1, TILE_ROWS - used_ref[t])
            return c
        lax.fori_loop(0, n_tiles, drain, 0)

    @pl.when(i >= 2)
    def _():
        wait_scatter(slot, nv_ref[jnp.maximum(i - 2, 0)])

    xslot = i % GATHER_BUFS
    issuer_used = jnp.where(i >= 2, nv_ref[jnp.maximum(i - 2, 0)], nv_ref[0]) > 0

    @pl.when((nv == 0) & (i > 0) & issuer_used)
    def _():
        wait_gather(xslot)

    @pl.when(nv > 0)
    def _():
        e = be_ref[i]
        e_prev = be_ref[jnp.maximum(i - 1, 0)]

        def weight_copies(ex, ws):
            return [pltpu.make_async_copy(src.at[ex], dst.at[ws], wsem.at[ws])
                    for src, dst in ((w1_hbm, wst1), (w3_hbm, wst3), (w2_hbm, wst2))]

        @pl.when(i == 0)
        def _():
            wslot[0] = 0
            for cp in weight_copies(e, 0):
                cp.start()

        @pl.when((i == 0) | (e != e_prev))
        def _():
            ws = wslot[0]
            for cp in weight_copies(e, ws):
                cp.wait()
            wb1[...] = wst1[ws].astype(BF16)
            wb3[...] = wst3[ws].astype(BF16)
            wb2[...] = wst2[ws].astype(BF16)
            nb = nxt_ref[i]

            @pl.when(nb >= 0)
            def _():
                for cp in weight_copies(be_ref[jnp.maximum(nb, 0)], 1 - ws):
                    cp.start()
            wslot[0] = 1 - ws

        wait_gather(xslot)
        gather(jnp.minimum(i + 2, last), (i + 2) % GATHER_BUFS)
        u = xbuf[xslot]
        xa, xb = _unpack_bf16_pair(u[:, :half])
        wv = pltpu.bitcast(u[:, half:], F32)
        roww = wv[:, 0:1] + wv[:, 1:2] + wv[:, 2:3]
        a = _dot(xa, wb1[:half, :]) + _dot(xb, wb1[half:, :])
        b = _dot(xa, wb3[:half, :]) + _dot(xb, wb3[half:, :])
        hmid = (a * jax.nn.sigmoid(a) * b).astype(BF16)
        y = _dot(hmid, wb2[...]) * roww
        ybuf[slot] = _pack_bf16_pair(y[:, :half], y[:, half:])
        scatter(i, slot)

    @pl.when(i == last)
    def _():
        @pl.when((last >= 1) & (nv_ref[jnp.maximum(last - 1, 0)] > 0))
        def _():
            wait_gather((last + 1) % GATHER_BUFS)

        @pl.when(nv > 0)
        def _():
            wait_gather((last + 2) % GATHER_BUFS)

        @pl.when(last >= 1)
        def _():
            wait_scatter(1 - slot, nv_ref[jnp.maximum(last - 1, 0)])
        wait_scatter(slot, nv)


def _moe(xs, plan, w1, w3, w2):
    n_blk = plan[0].shape[0]
    rows = MOE_ROWS
    half = D_MODEL // 2
    hbm = pl.BlockSpec(memory_space=pl.ANY)
    grid_spec = pltpu.PrefetchScalarGridSpec(
        num_scalar_prefetch=5,
        grid=(n_blk,),
        in_specs=[hbm] * 4,
        out_specs=hbm,
        scratch_shapes=[pltpu.VMEM((GATHER_BUFS, rows, XS_WIDTH), jnp.uint32),
                        pltpu.VMEM((2, rows, half), jnp.uint32),
                        pltpu.VMEM((D_MODEL, EXPERT_HIDDEN), BF16),
                        pltpu.VMEM((D_MODEL, EXPERT_HIDDEN), BF16),
                        pltpu.VMEM((EXPERT_HIDDEN, D_MODEL), BF16),
                        pltpu.VMEM((2, D_MODEL, EXPERT_HIDDEN), F32),
                        pltpu.VMEM((2, D_MODEL, EXPERT_HIDDEN), F32),
                        pltpu.VMEM((2, EXPERT_HIDDEN, D_MODEL), F32),
                        pltpu.SMEM((1,), jnp.int32),
                        pltpu.SemaphoreType.DMA((GATHER_BUFS,)),
                        pltpu.SemaphoreType.DMA((2,)),
                        pltpu.SemaphoreType.DMA((2,))])
    return pl.pallas_call(
        _moe_kernel,
        grid_spec=grid_spec,
        out_shape=jax.ShapeDtypeStruct((xs.shape[0], half), jnp.uint32),
        compiler_params=pltpu.CompilerParams(dimension_semantics=("arbitrary",),
                                             vmem_limit_bytes=VMEM_LIMIT),
        name="moe_ffn",
    )(*plan, w1, w3, w2, xs)


def _combine_kernel(x1_ref, ys_ref, pos_ref, g2_ref, o_ref):
    tt = x1_ref.shape[0]
    half = D_MODEL // 2
    pos = pos_ref[...]
    pcol = lax.broadcasted_iota(jnp.int32, (tt, ys_ref.shape[0]), 1).astype(F32)
    sel = (jnp.where(pcol == pos[:, 0:1], 1.0, 0.0) + jnp.where(pcol == pos[:, 1:2], 1.0, 0.0))
    sel = sel.astype(BF16)
    lo, hi = _unpack_bf16_pair(ys_ref[...])
    g2 = g2_ref[0]
    x1 = x1_ref[...]
    o_ref[:, :half] = x1[:, :half] + g2[:, :half] * _dot(sel, lo)
    o_ref[:, half:] = x1[:, half:] + g2[:, half:] * _dot(sel, hi)


def _combine(x1, ys, pos, g2, S):
    T = x1.shape[0]
    tt = MOE_TILE
    per_b = S // tt
    return pl.pallas_call(
        _combine_kernel,
        grid=(T // tt,),
        in_specs=[pl.BlockSpec((tt, D_MODEL), lambda i: (i, 0)),
                  pl.BlockSpec((TILE_ROWS, D_MODEL // 2), lambda i: (i, 0)),
                  pl.BlockSpec((tt, LANES), lambda i: (i, 0)),
                  pl.BlockSpec((1, 1, D_MODEL), lambda i: (i // per_b, 0, 0))],
        out_specs=pl.BlockSpec((tt, D_MODEL), lambda i: (i, 0)),
        out_shape=jax.ShapeDtypeStruct((T, D_MODEL), F32),
        compiler_params=pltpu.CompilerParams(vmem_limit_bytes=VMEM_LIMIT),
        name="moe_combine",
    )(x1, ys, pos, g2)


def _prep_w_in(w_in):
    dq = w_in[:, OFF_DIL_Q:OFF_DIL_K]
    dk = w_in[:, OFF_DIL_K:OFF_DIL_V]
    dv = w_in[:, OFF_DIL_V:OFF_GATE_A]
    dil = []
    for g in range(N_DIL_GROUPS):
        p0 = slice(g * DIL_OUT_WIDTH, g * DIL_OUT_WIDTH + LANES)
        p1 = slice(g * DIL_OUT_WIDTH + LANES, (g + 1) * DIL_OUT_WIDTH)
        dil += [dq[:, p0], dk[:, p0], dq[:, p1], dk[:, p1], dv[:, p0], dv[:, p1]]
    pad = jnp.zeros((D_MODEL, UNIT - FOX_HEADS), w_in.dtype)
    cols = [w_in[:, OFF_GATE_A:OFF_GATE_B], w_in[:, OFF_GATE_B:N_IN],
            w_in[:, OFF_FOX_Q:OFF_FOX_K], w_in[:, OFF_FOX_K:OFF_FOX_V], w_in[:, OFF_FOX_V:OFF_FOX_F],
            *dil, w_in[:, OFF_FOX_F:OFF_DIL_Q], pad]
    return jnp.concatenate(cols, axis=1).astype(BF16)


def _prep_gain(q_gain, k_gain):
    qs = HEAD_DIM ** -0.5 * LOG2E
    ones = jnp.ones((UNIT,), F32)
    fq = q_gain[:FOX_HEADS].reshape(-1) * qs
    fk = k_gain[:FOX_HEADS].reshape(-1)
    dq = q_gain[FOX_HEADS:].reshape(-1) * qs
    dk = k_gain[FOX_HEADS:].reshape(-1)
    dil = []
    for g in range(N_DIL_GROUPS):
        p0 = slice(g * DIL_OUT_WIDTH, g * DIL_OUT_WIDTH + LANES)
        p1 = slice(g * DIL_OUT_WIDTH + LANES, (g + 1) * DIL_OUT_WIDTH)
        dil += [dq[p0], dk[p0], dq[p1], dk[p1], ones]
    parts = [ones] * 8 + [fq, fk, ones, ones] + dil + [ones]
    return jnp.concatenate(parts).reshape(1, N_UNITS * UNIT)


def _layer(x, mod, rel_bias_table, norm1_g, w_in, b_forget, q_gain, k_gain, w_branch_a, w_branch_b,
           w_out, norm2_g, w_rg, b_rg, w_re, b_re, w1, w3, w2):
    B, S, D = x.shape
    T = B * S
    sh1, sc1, g1, sh2, sc2, g2 = [m.reshape(B, 1, D) for m in jnp.split(mod, 6, axis=-1)]
    x2d = x.reshape(T, D)

    p2d, fgt, slabs = _inproj(x2d, norm1_g.reshape(1, D), sc1, sh1, _prep_w_in(w_in),
                              _prep_gain(q_gain, k_gain), S)
    p3 = p2d.reshape(B, S, P_WIDTH)
    ck = _fcum(fgt.reshape(B, S, LANES), b_forget)
    ya = _fox(p3, ck)
    yd = _dil(slabs, _relbias(rel_bias_table))

    n_router = N_GROUPS + N_EXPERTS
    wr = jnp.concatenate([w_rg, w_re, jnp.zeros((D, LANES - n_router), F32)], axis=1).astype(BF16)
    br = jnp.concatenate([b_rg, b_re, jnp.zeros((LANES - n_router,), F32)]).reshape(1, LANES)
    x1, h2, logits = _outproj(x2d, ya.reshape(T, FOX_WIDTH), yd.reshape(T, DIL_OUT_WIDTH), p2d,
                              g1, sc2, sh2, norm2_g.reshape(1, D),
                              w_branch_a.astype(BF16), w_branch_b.astype(BF16), w_out.astype(BF16),
                              wr, br, S)
    xs, cnt, pos = _dispatch(logits, h2)
    cnt2 = cnt[:, :, 0]
    n_blk = cnt.shape[0] * TILE_ROWS // MOE_ROWS + N_EXPERTS
    plan = _plan(cnt2, n_blk)
    ys = _moe(xs, plan, w1, w3, w2)
    out = _combine(x1, ys, pos, g2, S)
    return out.reshape(B, S, D)


def kernel(x, c, rel_bias_table, w_ada, b_ada, norm1_g, w_in, b_forget, q_gain, k_gain, w_branch_a, w_branch_b, w_out, norm2_g, w_router_group, b_router_group, w_router_expert, b_router_expert, w1, w3, w2):
    depth = w_ada.shape[0]
    for l in range(depth):
        mod = _ada(c, w_ada[l], b_ada[l])
        x = _layer(x, mod, rel_bias_table, norm1_g[l], w_in[l], b_forget[l], q_gain[l], k_gain[l],
                   w_branch_a[l], w_branch_b[l], w_out[l], norm2_g[l], w_router_group[l],
                   b_router_group[l], w_router_expert[l], b_router_expert[l], w1[l], w3[l], w2[l])
    return x
```

```python
import functools
import math

import numpy as np
import jax
import jax.numpy as jnp
from jax import lax
from jax.experimental import pallas as pl
from jax.experimental.pallas import tpu as pltpu

F32 = jnp.float32
BF16 = jnp.bfloat16

D_MODEL = 1024
HEAD_DIM = 64
FOX_HEADS = 8
DIL_GROUPS = ((128, 1), (512, 4), (2048, 16))
DIL_HEADS_PER_GROUP = 4
N_DIL_GROUPS = len(DIL_GROUPS)
DIL_HEADS = N_DIL_GROUPS * DIL_HEADS_PER_GROUP
FOX_WIDTH = FOX_HEADS * HEAD_DIM
DIL_WIDTH = DIL_HEADS * HEAD_DIM
DIL_OUT_WIDTH = DIL_HEADS_PER_GROUP * HEAD_DIM
NUM_BUCKETS = 32
REL_MAX_DISTANCE = 2048
N_GROUPS = 4
EXPERTS_PER_GROUP = 8
N_EXPERTS = N_GROUPS * EXPERTS_PER_GROUP
EXPERT_HIDDEN = D_MODEL // 2
EPS = 1e-6
LOG2E = math.log2(math.e)

OFF_FOX_Q = 0
OFF_FOX_K = OFF_FOX_Q + FOX_WIDTH
OFF_FOX_V = OFF_FOX_K + FOX_WIDTH
OFF_FOX_F = OFF_FOX_V + FOX_WIDTH
OFF_DIL_Q = OFF_FOX_F + FOX_HEADS
OFF_DIL_K = OFF_DIL_Q + DIL_WIDTH
OFF_DIL_V = OFF_DIL_K + DIL_WIDTH
OFF_GATE_A = OFF_DIL_V + DIL_WIDTH
OFF_GATE_B = OFF_GATE_A + D_MODEL
N_IN = OFF_GATE_B + D_MODEL

LANES = 128
UNIT = 256
DIL_L = 128

U_GATE_A, U_GATE_B, U_FOX_Q, U_FOX_K, U_FOX_V, U_DIL, U_FORGET = 0, 4, 8, 10, 12, 14, 23
N_UNITS = 24
P_WIDTH = U_DIL * UNIT
N_SLABS = 3 * N_DIL_GROUPS
_KIND = (["gate"] * 8 + ["norm"] * 4 + ["plain"] * 2 + ["norm", "norm", "plain"] * 3 + ["forget"])

TM_INPROJ = 1024
TM_PROJ = 1024
TQ_FOX = 512
MOE_ROWS = 256
MOE_TILE = 512
XS_WIDTH = D_MODEL // 2 + LANES
SUBLANES = 8
TILE_ROWS = 2 * MOE_TILE + N_EXPERTS * SUBLANES
MOE_GROUPS = MOE_ROWS // SUBLANES
GATHER_BUFS = 3
VMEM_LIMIT = 56 * 1024 * 1024


def _dot(a, b):
    return jnp.dot(a, b, preferred_element_type=F32)


def _dot_nt(a, b):
    return lax.dot_general(a, b, (((1,), (1,)), ((), ())), preferred_element_type=F32)


def _split3(x):
    hi = x.astype(BF16)
    r1 = x - hi.astype(F32)
    mid = r1.astype(BF16)
    lo = (r1 - mid.astype(F32)).astype(BF16)
    return hi, mid, lo


def _ada_kernel(c_ref, w_ref, b_ref, o_ref):
    c = c_ref[...]
    s = c * jax.nn.sigmoid(c)
    s_hi = s.astype(BF16)
    s_lo = (s - s_hi.astype(F32)).astype(BF16)
    w = w_ref[...]
    w_hi = w.astype(BF16)
    w_lo = (w - w_hi.astype(F32)).astype(BF16)
    acc = _dot(s_hi, w_hi) + _dot(s_hi, w_lo) + _dot(s_lo, w_hi)
    o_ref[...] = acc + b_ref[...]


def _ada(c, w_ada, b_ada):
    B = c.shape[0]
    n_out = w_ada.shape[1]
    tn = 512
    return pl.pallas_call(
        _ada_kernel,
        grid=(n_out // tn,),
        in_specs=[pl.BlockSpec((B, D_MODEL), lambda j: (0, 0)),
                  pl.BlockSpec((D_MODEL, tn), lambda j: (0, j)),
                  pl.BlockSpec((1, tn), lambda j: (0, j))],
        out_specs=pl.BlockSpec((B, tn), lambda j: (0, j)),
        out_shape=jax.ShapeDtypeStruct((B, n_out), F32),
        name="ada_mod",
    )(c, w_ada, b_ada.reshape(1, n_out))


def _pack_bf16_pair(lo, hi):
    lo_bits = pltpu.bitcast(lo.astype(BF16).astype(F32), jnp.uint32) >> 16
    hi_bits = pltpu.bitcast(hi.astype(BF16).astype(F32), jnp.uint32) & jnp.uint32(0xFFFF0000)
    return lo_bits | hi_bits


def _unpack_bf16_pair(u):
    lo = pltpu.bitcast(u << 16, F32).astype(BF16)
    hi = pltpu.bitcast(u & jnp.uint32(0xFFFF0000), F32).astype(BF16)
    return lo, hi


def _inproj_kernel(x_ref, g_ref, sc_ref, sh_ref, w_ref, gain_ref, bd_ref, p_ref, f_ref, s_ref):
    x = x_ref[...]
    ms = jnp.mean(x * x, axis=-1, keepdims=True)
    h = x * lax.rsqrt(ms + EPS) * g_ref[...]
    h = h * (1.0 + sc_ref[0]) + sh_ref[0]
    hb = h.astype(BF16)

    def unit(u):
        cols = slice(u * UNIT, (u + 1) * UNIT)
        acc = _dot(hb, w_ref[:, cols])
        kind = _KIND[u]
        if kind == "gate":
            return jax.nn.sigmoid(acc)
        if kind == "norm":
            ss = _dot((acc * acc).astype(BF16), bd_ref[...])
            return acc * lax.rsqrt(ss * (1.0 / HEAD_DIM) + EPS) * gain_ref[:, cols]
        return acc

    def emit(u):
        o = unit(u)
        if u < U_DIL:
            p_ref[:, u * UNIT:(u + 1) * UNIT] = o.astype(BF16)
        elif u < U_FORGET:
            s_ref[0, u - U_DIL] = _pack_bf16_pair(o[:, :LANES], o[:, LANES:])
        else:
            f_ref[...] = o[:, :LANES]

    normed = [u for u in range(N_UNITS) if _KIND[u] == "norm"]
    others = [u for u in range(N_UNITS) if _KIND[u] != "norm"]
    while normed or others:
        for group in (normed, others):
            if group:
                emit(group.pop(0))


def _inproj(x2d, norm_g, sc, sh, w_re, gain_row, S):
    T = x2d.shape[0]
    tm = TM_INPROJ
    per_b = S // tm
    bd = np.kron(np.eye(UNIT // HEAD_DIM), np.ones((HEAD_DIM, HEAD_DIM))).astype(np.float32)
    once = dict(pipeline_mode=pl.Buffered(1))
    return pl.pallas_call(
        _inproj_kernel,
        grid=(T // tm,),
        in_specs=[pl.BlockSpec((tm, D_MODEL), lambda i: (i, 0)),
                  pl.BlockSpec((1, D_MODEL), lambda i: (0, 0)),
                  pl.BlockSpec((1, 1, D_MODEL), lambda i: (i // per_b, 0, 0)),
                  pl.BlockSpec((1, 1, D_MODEL), lambda i: (i // per_b, 0, 0)),
                  pl.BlockSpec((D_MODEL, N_UNITS * UNIT), lambda i: (0, 0), **once),
                  pl.BlockSpec((1, N_UNITS * UNIT), lambda i: (0, 0), **once),
                  pl.BlockSpec((UNIT, UNIT), lambda i: (0, 0), **once)],
        out_specs=[pl.BlockSpec((tm, P_WIDTH), lambda i: (i, 0)),
                   pl.BlockSpec((tm, LANES), lambda i: (i, 0)),
                   pl.BlockSpec((1, N_SLABS, tm, LANES), lambda i: (i // per_b, 0, i % per_b, 0))],
        out_shape=[jax.ShapeDtypeStruct((T, P_WIDTH), BF16),
                   jax.ShapeDtypeStruct((T, LANES), F32),
                   jax.ShapeDtypeStruct((T // S, N_SLABS, S, LANES), jnp.uint32)],
        compiler_params=pltpu.CompilerParams(vmem_limit_bytes=VMEM_LIMIT),
        name="in_proj",
    )(x2d, norm_g, sc, sh, w_re, gain_row, jnp.asarray(bd, BF16))


def _fcum_kernel(f_ref, b_ref, tri_ref, o_ref):
    S = f_ref.shape[1]
    xf = f_ref[0] + b_ref[...]
    ls = (jnp.minimum(xf, 0.0) - jnp.log(1.0 + jnp.exp(-jnp.abs(xf)))) * LOG2E
    lst = ls.T
    carry = jnp.zeros((LANES, UNIT), F32)
    for blk in range(S // UNIT):
        seg = lst[:, blk * UNIT:(blk + 1) * UNIT]
        hi, mid, lo = _split3(seg)
        tri = tri_ref[...]
        res = _dot(hi, tri) + _dot(mid, tri) + _dot(lo, tri)
        o_ref[0, :, blk * UNIT:(blk + 1) * UNIT] = (res[:, :UNIT] + carry)[:FOX_HEADS]
        carry = carry + res[:, UNIT:]


def _fcum(fgt, b_forget):
    B, S, _ = fgt.shape
    brow = jnp.zeros((1, LANES), F32).at[0, :FOX_HEADS].set(b_forget)
    tri = np.concatenate([np.triu(np.ones((UNIT, UNIT))), np.ones((UNIT, UNIT))], axis=1)
    return pl.pallas_call(
        _fcum_kernel,
        grid=(B,),
        in_specs=[pl.BlockSpec((1, S, LANES), lambda b: (b, 0, 0)),
                  pl.BlockSpec((1, LANES), lambda b: (0, 0)),
                  pl.BlockSpec((UNIT, 2 * UNIT), lambda b: (0, 0))],
        out_specs=pl.BlockSpec((1, FOX_HEADS, S), lambda b: (b, 0, 0)),
        out_shape=jax.ShapeDtypeStruct((B, FOX_HEADS, S), F32),
        name="forget_cumsum",
    )(fgt, brow, jnp.asarray(tri, BF16))


def _fox_kernel(q_ref, k_ref, v_ref, ck_ref, o_ref):
    S = q_ref.shape[1]
    pair = pl.program_id(1)
    tq = TQ_FOX
    lane = lax.broadcasted_iota(jnp.int32, (1, LANES), 1)
    row = lax.broadcasted_iota(jnp.int32, (tq, tq), 0)
    col = lax.broadcasted_iota(jnp.int32, (tq, tq), 1)
    causal = col <= row
    cks = [ck_ref[0, pl.ds(2 * pair + hh, 1), :] for hh in range(2)]
    for t in reversed(range(S // tq)):
        r0, r1 = t * tq, (t + 1) * tq
        qt = q_ref[0, r0:r1, :]
        outs = []
        for hh in range(2):
            hsel = (lane >= HEAD_DIM) == bool(hh)
            qm = jnp.where(hsel, qt, jnp.zeros_like(qt))
            ck = cks[hh]
            s = _dot_nt(qm, k_ref[0, :r1, :]) - ck[:, :r1]
            s_d = jnp.where(causal, s[:, r0:], -jnp.inf)
            s = jnp.concatenate([s[:, :r0], s_d], axis=1) if t > 0 else s_d
            m = jnp.max(s, axis=-1, keepdims=True)
            p = jnp.exp2(s - m)
            l = jnp.sum(p, axis=-1, keepdims=True)
            outs.append(_dot(p.astype(BF16), v_ref[0, :r1, :]) / l)
        o_ref[0, r0:r1, :] = jnp.where(lane < HEAD_DIM, outs[0], outs[1]).astype(BF16)


def _fox(p3, ck):
    B, S, _ = p3.shape
    nq, nk, nv = (U_FOX_Q * UNIT // LANES, U_FOX_K * UNIT // LANES, U_FOX_V * UNIT // LANES)
    return pl.pallas_call(
        _fox_kernel,
        grid=(B, FOX_HEADS // 2),
        in_specs=[pl.BlockSpec((1, S, LANES), lambda b, p: (b, 0, nq + p)),
                  pl.BlockSpec((1, S, LANES), lambda b, p: (b, 0, nk + p)),
                  pl.BlockSpec((1, S, LANES), lambda b, p: (b, 0, nv + p)),
                  pl.BlockSpec((1, FOX_HEADS, S), lambda b, p: (b, 0, 0))],
        out_specs=pl.BlockSpec((1, S, LANES), lambda b, p: (b, 0, p)),
        out_shape=jax.ShapeDtypeStruct((B, S, FOX_WIDTH), BF16),
        compiler_params=pltpu.CompilerParams(vmem_limit_bytes=VMEM_LIMIT),
        name="fox_attn",
    )(p3, p3, p3, ck)


def _t5_bucket(dist):
    max_exact = NUM_BUCKETS // 2
    d = np.maximum(dist, 1).astype(np.float32)
    large = max_exact + (np.log(d / max_exact) / np.log(REL_MAX_DISTANCE / max_exact)
                         * (NUM_BUCKETS - max_exact)).astype(np.int32)
    large = np.minimum(large, NUM_BUCKETS - 1)
    return np.where(dist < max_exact, dist, large).astype(np.int32)


def _relbias_kernel(tab_ref, bucket_ref, valid_ref, o_ref):
    g = pl.program_id(0)
    bk = bucket_ref[0]
    vd = valid_ref[0]
    for hs in range(DIL_HEADS_PER_GROUP):
        acc = jnp.zeros(bk.shape, F32)
        for b in range(NUM_BUCKETS):
            acc = jnp.where(bk == b, tab_ref[b, g * DIL_HEADS_PER_GROUP + hs], acc)
        bias = jnp.where(vd != 0, acc * LOG2E, -jnp.inf)
        o_ref[0, hs] = bias
        col = lax.broadcasted_iota(jnp.int32, bias.shape, 1)
        o_ref[1, hs] = jnp.where(col >= DIL_L, bias, -jnp.inf)


def _relbias(table):
    L = DIL_L
    i = np.arange(L)[:, None]
    j = np.arange(2 * L)[None, :]
    m = L + i - j
    valid = ((m >= 0) & (m <= L)).astype(np.int32)
    buckets = np.stack([_t5_bucket(np.clip(m, 0, None) * d) for _, d in DIL_GROUPS])
    valids = np.stack([valid] * N_DIL_GROUPS)
    return pl.pallas_call(
        _relbias_kernel,
        grid=(N_DIL_GROUPS,),
        in_specs=[pl.BlockSpec(memory_space=pltpu.SMEM),
                  pl.BlockSpec((1, L, 2 * L), lambda g: (g, 0, 0)),
                  pl.BlockSpec((1, L, 2 * L), lambda g: (g, 0, 0))],
        out_specs=pl.BlockSpec((2, DIL_HEADS_PER_GROUP, L, 2 * L), lambda g: (0, g, 0, 0)),
        out_shape=jax.ShapeDtypeStruct((2, DIL_HEADS, L, 2 * L), F32),
        name="rel_bias",
    )(table, jnp.asarray(buckets), jnp.asarray(valids))


def _dil_rows(start, d):
    return pl.ds(start, DIL_L) if d == 1 else pl.ds(start, DIL_L, stride=d)


def _dil_block_rows(d, nb, it):
    r, n = it // nb, it % nb
    cur = _dil_rows(r + d * (n * DIL_L), d)
    prev = _dil_rows(r + d * (jnp.maximum(n - 1, 0) * DIL_L), d)
    return cur, prev, 1 - jnp.minimum(n, 1)


def _dil_scores(qkv_ref, bias_ref, s_scr, slot, g, d, nb, it):
    lane = lax.broadcasted_iota(jnp.int32, (1, LANES), 1)
    cur, prev, first = _dil_block_rows(d, nb, it)
    for pr in range(2):
        qt, kt = _unpack_bf16_pair(qkv_ref[0, 3 * g + pr, cur, :])
        if nb > 1:
            _, k_prev = _unpack_bf16_pair(qkv_ref[0, 3 * g + pr, prev, :])
            kt = jnp.concatenate([k_prev, kt], axis=0)
        for hh in range(2):
            hsel = (lane >= HEAD_DIM) == bool(hh)
            qm = jnp.where(hsel, qt, jnp.zeros_like(qt))
            head = 2 * pr + hh
            if nb > 1:
                s_scr[slot, head] = _dot_nt(qm, kt) + bias_ref[first, DIL_HEADS_PER_GROUP * g + head]
            else:
                bias = bias_ref[0, DIL_HEADS_PER_GROUP * g + head, :, DIL_L:]
                s_scr[slot, head, :, :DIL_L] = _dot_nt(qm, kt) + bias


def _dil_merge(qkv_ref, s_scr, slot, m_scr, l_scr, acc_scr, g, d, nb, init, it):
    lane = lax.broadcasted_iota(jnp.int32, (1, LANES), 1)
    cur, prev, _ = _dil_block_rows(d, nb, it)
    v_cur = _unpack_bf16_pair(qkv_ref[0, 3 * g + 2, cur, :])
    if nb > 1:
        v_prev = _unpack_bf16_pair(qkv_ref[0, 3 * g + 2, prev, :])
    for pr in range(2):
        vt = jnp.concatenate([v_prev[pr], v_cur[pr]], axis=0) if nb > 1 else v_cur[pr]
        ms, ls, accs = [], [], []
        for hh in range(2):
            s = s_scr[slot, 2 * pr + hh] if nb > 1 else s_scr[slot, 2 * pr + hh, :, :DIL_L]
            m = jnp.max(s, axis=-1, keepdims=True)
            p = jnp.exp2(s - m)
            ms.append(m)
            ls.append(jnp.sum(p, axis=-1, keepdims=True))
            accs.append(_dot(p.astype(BF16), vt))
        low = lane < HEAD_DIM
        m_b = jnp.where(low, ms[0], ms[1])
        l_b = jnp.where(low, ls[0], ls[1])
        acc_b = jnp.where(low, accs[0], accs[1])
        if init:
            m_scr[pr, cur, :] = m_b
            l_scr[pr, cur, :] = l_b
            acc_scr[pr, cur, :] = acc_b
        else:
            m_o = m_scr[pr, cur, :]
            m_n = jnp.maximum(m_o, m_b)
            a_o = jnp.exp2(m_o - m_n)
            a_b = jnp.exp2(m_b - m_n)
            m_scr[pr, cur, :] = m_n
            l_scr[pr, cur, :] = l_scr[pr, cur, :] * a_o + l_b * a_b
            acc_scr[pr, cur, :] = acc_scr[pr, cur, :] * a_o + acc_b * a_b


def _dil_kernel(qkv_ref, bias_ref, o_ref, m_scr, l_scr, acc_scr, s_scr):
    S = o_ref.shape[1]
    order = sorted(range(N_DIL_GROUPS), key=lambda g: -DIL_GROUPS[g][1])
    for g in order:
        window, d = DIL_GROUPS[g]
        nb = S // window
        total = d * nb
        assert total % 2 == 0
        scores = functools.partial(_dil_scores, qkv_ref, bias_ref, s_scr, g=g, d=d, nb=nb)
        merge = functools.partial(_dil_merge, qkv_ref, s_scr, m_scr=m_scr, l_scr=l_scr,
                                  acc_scr=acc_scr, g=g, d=d, nb=nb, init=g == order[0])

        scores(slot=0, it=0)

        def body(j, carry, scores=scores, merge=merge, total=total):
            scores(slot=1, it=2 * j + 1)
            merge(slot=0, it=2 * j)
            scores(slot=0, it=jnp.minimum(2 * j + 2, total - 1))
            merge(slot=1, it=2 * j + 1)
            return carry
        lax.fori_loop(0, total // 2, body, 0, unroll=2)
    for pr in range(2):
        o_ref[0, :, pr * LANES:(pr + 1) * LANES] = (acc_scr[pr] / l_scr[pr]).astype(BF16)


def _dil(slabs, bias):
    B, _, S, _ = slabs.shape
    for window, d in DIL_GROUPS:
        assert window // d == DIL_L and S % window == 0
    stat = pltpu.VMEM((2, S, LANES), F32)
    return pl.pallas_call(
        _dil_kernel,
        grid=(B,),
        in_specs=[pl.BlockSpec((1, N_SLABS, S, LANES), lambda b: (b, 0, 0, 0)),
                  pl.BlockSpec(bias.shape, lambda b: (0, 0, 0, 0))],
        out_specs=pl.BlockSpec((1, S, DIL_OUT_WIDTH), lambda b: (b, 0, 0)),
        out_shape=jax.ShapeDtypeStruct((B, S, DIL_OUT_WIDTH), BF16),
        scratch_shapes=[stat, stat, stat,
                        pltpu.VMEM((2, DIL_HEADS_PER_GROUP, DIL_L, 2 * DIL_L), F32)],
        compiler_params=pltpu.CompilerParams(vmem_limit_bytes=VMEM_LIMIT),
        name="dil_attn",
    )(slabs, bias)


def _outproj_kernel(x_ref, ya_ref, yd_ref, ga_ref, gb_ref,
                    g1_ref, sc_ref, sh_ref, ng_ref, wa_ref, wb_ref, wo_ref, wr_ref, br_ref,
                    x1_ref, h2_ref, lg_ref):
    n_chunks = 2
    cm = x_ref.shape[0] // n_chunks
    for c in range(n_chunks):
        rows = slice(c * cm, (c + 1) * cm)
        a = _dot(ya_ref[rows, :], wa_ref[...])
        bm = _dot(yd_ref[rows, :], wb_ref[...])
        merged = ga_ref[rows, :].astype(F32) * a + gb_ref[rows, :].astype(F32) * bm
        out = _dot(merged.astype(BF16), wo_ref[...])
        x1 = x_ref[rows, :] + g1_ref[0] * out
        x1_ref[rows, :] = x1
        ms = jnp.mean(x1 * x1, axis=-1, keepdims=True)
        h = x1 * lax.rsqrt(ms + EPS) * ng_ref[...]
        h = h * (1.0 + sc_ref[0]) + sh_ref[0]
        hb = h.astype(BF16)
        h2_ref[rows, :] = hb
        lg_ref[rows, :] = _dot(hb, wr_ref[...]) + br_ref[...]


def _outproj(x2d, ya2d, yd2d, p2d, g1, sc2, sh2, norm_g, wa, wb, wo, wr, br, S):
    T = x2d.shape[0]
    tm = TM_PROJ
    per_b = S // tm
    row = lambda w: pl.BlockSpec((tm, w), lambda i: (i, 0))
    full = lambda a: pl.BlockSpec(a.shape, lambda i: (0,) * a.ndim)
    mod = pl.BlockSpec((1, 1, D_MODEL), lambda i: (i // per_b, 0, 0))
    return pl.pallas_call(
        _outproj_kernel,
        grid=(T // tm,),
        in_specs=[row(D_MODEL), row(FOX_WIDTH), row(DIL_OUT_WIDTH)]
                 + [pl.BlockSpec((tm, D_MODEL), lambda i: (i, U_GATE_A * UNIT // D_MODEL)),
                    pl.BlockSpec((tm, D_MODEL), lambda i: (i, U_GATE_B * UNIT // D_MODEL)),
                    mod, mod, mod, full(norm_g), full(wa), full(wb), full(wo), full(wr), full(br)],
        out_specs=[row(D_MODEL), row(D_MODEL), row(LANES)],
        out_shape=[jax.ShapeDtypeStruct((T, D_MODEL), F32),
                   jax.ShapeDtypeStruct((T, D_MODEL), BF16),
                   jax.ShapeDtypeStruct((T, LANES), F32)],
        compiler_params=pltpu.CompilerParams(vmem_limit_bytes=VMEM_LIMIT),
        name="out_proj",
    )(x2d, ya2d, yd2d, p2d, p2d, g1, sc2, sh2, norm_g, wa, wb, wo, wr, br)


def _dispatch_kernel(lg_ref, h_ref, tri_ref, xs_ref, cnt_ref, pos_ref):
    tt = lg_ref.shape[0]
    lt = lg_ref[...].T
    row = lambda i: lt[i:i + 1, :]
    neg = -jnp.inf
    g = [row(i) for i in range(N_GROUPS)]
    gmax = functools.reduce(jnp.maximum, g)
    gidx = jnp.full(gmax.shape, N_GROUPS - 1, jnp.int32)
    for i in reversed(range(N_GROUPS - 1)):
        gidx = jnp.where(g[i] == gmax, i, gidx)
    gsum = sum(jnp.exp(gi - gmax) for gi in g)
    el = []
    for j in range(EXPERTS_PER_GROUP):
        v = row(N_GROUPS + EXPERTS_PER_GROUP * (N_GROUPS - 1) + j)
        for gg in reversed(range(N_GROUPS - 1)):
            v = jnp.where(gidx == gg, row(N_GROUPS + EXPERTS_PER_GROUP * gg + j), v)
        el.append(v)

    def top(vals):
        best = functools.reduce(jnp.maximum, vals)
        idx = jnp.full(best.shape, EXPERTS_PER_GROUP - 1, jnp.int32)
        for j in reversed(range(EXPERTS_PER_GROUP - 1)):
            idx = jnp.where(vals[j] == best, j, idx)
        return best, idx

    v1, i1 = top(el)
    v2, i2 = top([jnp.where(i1 == j, neg, el[j]) for j in range(EXPERTS_PER_GROUP)])
    t = jnp.exp(v2 - v1)
    den = (1.0 + t) * gsum
    wts = [1.0 / den, t / den]
    eid = [gidx * EXPERTS_PER_GROUP + i1, gidx * EXPERTS_PER_GROUP + i2]

    esub = lax.broadcasted_iota(jnp.int32, (N_EXPERTS, tt), 0)
    ohf = jnp.concatenate([jnp.where(esub == eid[k], 1.0, 0.0) for k in range(2)], axis=1)
    n_pb = 2 * tt // UNIT
    oh_blocks = jnp.concatenate([ohf[:, b * UNIT:(b + 1) * UNIT] for b in range(n_pb)], axis=0)
    res = _dot(oh_blocks.astype(BF16), tri_ref[...])
    cnt = jnp.zeros((N_EXPERTS, LANES), F32)
    pre = []
    for b in range(n_pb):
        r = res[b * N_EXPERTS:(b + 1) * N_EXPERTS]
        pre.append(r[:, :UNIT] + jnp.concatenate([cnt] * (UNIT // LANES), axis=1))
        cnt = cnt + r[:, UNIT:]
    prefix = jnp.concatenate(pre, axis=1)
    cnt = (((cnt.astype(jnp.int32) + (SUBLANES - 1)) // SUBLANES) * SUBLANES).astype(F32)
    esub_c = lax.broadcasted_iota(jnp.int32, cnt.shape, 0)
    start = jnp.zeros_like(cnt)
    for e in range(N_EXPERTS - 1):
        start = start + jnp.where(esub_c > e, cnt[e:e + 1, :], 0.0)
    start_w = jnp.concatenate([start] * (2 * tt // LANES), axis=1)
    pos = jnp.sum(ohf * (start_w + prefix), axis=0, keepdims=True)
    pos_k = [pos[:, :tt], pos[:, tt:]]

    n_rows = xs_ref.shape[0]
    psub = lax.broadcasted_iota(jnp.int32, (n_rows, tt), 0).astype(F32)
    pm = [jnp.where(psub == pos_k[k], 1.0, 0.0).astype(BF16) for k in range(2)]
    xs = _dot(pm[0] + pm[1], h_ref[...])
    wsub = lax.broadcasted_iota(jnp.int32, (LANES, tt), 0)
    ws = jnp.zeros((n_rows, LANES), F32)
    for k in range(2):
        parts = _split3(wts[k])
        wrows = jnp.zeros((LANES, tt), F32)
        for j in range(3):
            wrows = jnp.where(wsub == j, parts[j].astype(F32), wrows)
        ws = ws + _dot_nt(pm[k], wrows.astype(BF16))
    half = D_MODEL // 2
    xs_ref[:, :half] = _pack_bf16_pair(xs[:, :half], xs[:, half:])
    xs_ref[:, half:] = pltpu.bitcast(ws, jnp.uint32)
    cnt_ref[0] = cnt.astype(jnp.int32)
    posr = jnp.where(wsub == 0, pos_k[0], jnp.where(wsub == 1, pos_k[1], 0.0))
    pos_ref[...] = posr.T


def _dispatch(logits, h2):
    T = logits.shape[0]
    tt = MOE_TILE
    n_tiles = T // tt
    tri = np.concatenate([np.triu(np.ones((UNIT, UNIT)), 1), np.ones((UNIT, LANES))], axis=1)
    return pl.pallas_call(
        _dispatch_kernel,
        grid=(n_tiles,),
        in_specs=[pl.BlockSpec((tt, LANES), lambda i: (i, 0)),
                  pl.BlockSpec((tt, D_MODEL), lambda i: (i, 0)),
                  pl.BlockSpec(tri.shape, lambda i: (0, 0))],
        out_specs=[pl.BlockSpec((TILE_ROWS, XS_WIDTH), lambda i: (i, 0)),
                   pl.BlockSpec((1, N_EXPERTS, LANES), lambda i: (i, 0, 0)),
                   pl.BlockSpec((tt, LANES), lambda i: (i, 0))],
        out_shape=[jax.ShapeDtypeStruct((n_tiles * TILE_ROWS, XS_WIDTH), jnp.uint32),
                   jax.ShapeDtypeStruct((n_tiles, N_EXPERTS, LANES), jnp.int32),
                   jax.ShapeDtypeStruct((T, LANES), F32)],
        compiler_params=pltpu.CompilerParams(vmem_limit_bytes=VMEM_LIMIT),
        name="moe_dispatch",
    )(logits, h2, jnp.asarray(tri, BF16))


def _plan_kernel(cnt_ref, be_ref, nv_ref, nxt_ref, grp_ref, used_ref, cs_ref):
    n_tiles = cnt_ref.shape[0]
    n_blk = be_ref.shape[0]
    rows = MOE_ROWS
    row_shift = rows.bit_length() - 1
    grp_shift = SUBLANES.bit_length() - 1
    assert rows == 1 << row_shift and SUBLANES == 1 << grp_shift

    def tile_starts(t, c):
        def per_e(e, acc):
            cs_ref[t * N_EXPERTS + e] = acc
            return acc + cnt_ref[t, e]
        used_ref[t] = lax.fori_loop(0, N_EXPERTS, per_e, 0, unroll=8)
        return c
    lax.fori_loop(0, n_tiles, tile_starts, 0)

    def clear(b, c):
        nv_ref[b] = 0
        return c
    lax.fori_loop(0, n_blk, clear, 0)

    def clear_groups(g, c):
        grp_ref[g] = 0
        return c
    lax.fori_loop(0, n_blk * MOE_GROUPS, clear_groups, 0, unroll=8)

    def per_expert(e, b):
        g0 = b * MOE_GROUPS

        def per_tile(t, tot):
            c = cnt_ref[t, e]
            src = t * TILE_ROWS + cs_ref[t * N_EXPERTS + e]
            first = g0 + lax.shift_right_logical(tot, grp_shift)

            def per_group(k, cc):
                grp_ref[first + k] = src + k * SUBLANES
                return cc
            lax.fori_loop(0, lax.shift_right_logical(c, grp_shift), per_group, 0)
            return tot + c
        tot = lax.fori_loop(0, n_tiles, per_tile, 0)

        def per_block(j, c):
            be_ref[b + j] = e
            nv_ref[b + j] = jnp.minimum(rows, tot - j * rows)
            return c
        nb = lax.shift_right_logical(tot + rows - 1, row_shift)
        lax.fori_loop(0, nb, per_block, 0)
        return b + nb
    n_used = lax.fori_loop(0, N_EXPERTS, per_expert, 0)

    def unused(b, c):
        be_ref[b] = be_ref[n_used - 1]
        nxt_ref[b] = -1
        return c
    lax.fori_loop(n_used, n_blk, unused, 0)

    def next_run(k, nf):
        b = n_used - 1 - k
        nf = jnp.where(be_ref[b] != be_ref[jnp.minimum(b + 1, n_used - 1)], b + 1, nf)
        nxt_ref[b] = nf
        return nf
    lax.fori_loop(0, n_used, next_run, -1)


def _plan(cnt, n_blk):
    n_tiles = cnt.shape[0]
    smem = pl.BlockSpec(memory_space=pltpu.SMEM)
    i32 = lambda n: jax.ShapeDtypeStruct((n,), jnp.int32)
    return pl.pallas_call(
        _plan_kernel,
        in_specs=[smem],
        out_specs=[smem] * 5,
        out_shape=[i32(n_blk), i32(n_blk), i32(n_blk), i32(n_blk * MOE_GROUPS), i32(n_tiles)],
        scratch_shapes=[pltpu.SMEM((n_tiles * N_EXPERTS,), jnp.int32)],
        name="moe_plan",
    )(cnt)


def _pow2_pieces(n, fn):
    for b in reversed(range(SUBLANES.bit_length() - 1, MOE_ROWS.bit_length())):
        size = 1 << b

        @pl.when((n & size) != 0)
        def _():
            fn((n >> (b + 1)) << (b + 1), size)


def _moe_kernel(be_ref, nv_ref, nxt_ref, grp_ref, used_ref,
                w1_hbm, w3_hbm, w2_hbm, xs_hbm, ys_hbm,
                xbuf, ybuf, wb1, wb3, wb2, wst1, wst3, wst2, wslot, gsem, ssem, wsem):
    i = pl.program_id(0)
    last = pl.num_programs(0) - 1
    slot = i % 2
    nv = nv_ref[i]
    half = D_MODEL // 2
    grp_shift = SUBLANES.bit_length() - 1

    def group_row(blk, g):
        return pl.multiple_of(grp_ref[blk * MOE_GROUPS + g], SUBLANES)

    def gather(blk, s):
        top = jnp.maximum(lax.shift_right_logical(nv_ref[blk], grp_shift) - 1, 0)
        for g in range(MOE_GROUPS):
            src = group_row(blk, jnp.minimum(g, top))
            pltpu.make_async_copy(xs_hbm.at[pl.ds(src, SUBLANES)],
                                  xbuf.at[s, pl.ds(g * SUBLANES, SUBLANES)], gsem.at[s]).start()

    def wait_gather(s):
        pltpu.make_async_copy(xs_hbm.at[pl.ds(0, MOE_ROWS)], xbuf.at[s], gsem.at[s]).wait()

    def scatter(blk, s):
        def start(g):
            r = g * SUBLANES if isinstance(g, int) else pl.multiple_of(g * SUBLANES, SUBLANES)
            pltpu.make_async_copy(ybuf.at[s, pl.ds(r, SUBLANES)],
                                  ys_hbm.at[pl.ds(group_row(blk, g), SUBLANES)], ssem.at[s]).start()

        @pl.when(nv_ref[blk] == MOE_ROWS)
        def _():
            for g in range(MOE_GROUPS):
                start(g)

        @pl.when(nv_ref[blk] < MOE_ROWS)
        def _():
            def body(g, c):
                start(g)
                return c
            lax.fori_loop(0, lax.shift_right_logical(nv_ref[blk], grp_shift), body, 0)

    def wait_scatter(s, count):
        @pl.when(count == MOE_ROWS)
        def _():
            pltpu.make_async_copy(ybuf.at[s], ys_hbm.at[pl.ds(0, MOE_ROWS)], ssem.at[s]).wait()

        @pl.when(count < MOE_ROWS)
        def _():
            _pow2_pieces(count, lambda a, size: pltpu.make_async_copy(
                ybuf.at[s, pl.ds(0, size)], ys_hbm.at[pl.ds(0, size)], ssem.at[s]).wait())

    @pl.when(i == 0)
    def _():
        @pl.when(nv > 0)
        def _():
            gather(0, 0)
            gather(jnp.minimum(1, last), 1)

        ybuf[1] = jnp.zeros(ybuf.shape[1:], ybuf.dtype)
        n_tiles = used_ref.shape[0]

        def fill(t, c):
            row0 = t * TILE_ROWS + used_ref[t]
            _pow2_pieces(TILE_ROWS - used_ref[t], lambda a, size: pltpu.make_async_copy(
                ybuf.at[1, pl.ds(0, size)], ys_hbm.at[pl.ds(pl.multiple_of(row0 + a, SUBLANES), size)],
                ssem.at[1]).start())
            return c
        lax.fori_loop(0, n_tiles, fill, 0)

        def drain(t, c):
            wait_scatter(1, TILE_ROWS - used_ref[t])
            return c
        lax.fori_loop(0, n_tiles, drain, 0)

    @pl.when(i >= 2)
    def _():
        wait_scatter(slot, nv_ref[jnp.maximum(i - 2, 0)])

    xslot = i % GATHER_BUFS
    issuer_used = jnp.where(i >= 2, nv_ref[jnp.maximum(i - 2, 0)], nv_ref[0]) > 0

    @pl.when((nv == 0) & (i > 0) & issuer_used)
    def _():
        wait_gather(xslot)

    @pl.when(nv > 0)
    def _():
        e = be_ref[i]
        e_prev = be_ref[jnp.maximum(i - 1, 0)]

        def weight_copies(ex, ws):
            return [pltpu.make_async_copy(src.at[ex], dst.at[ws], wsem.at[ws])
                    for src, dst in ((w1_hbm, wst1), (w3_hbm, wst3), (w2_hbm, wst2))]

        @pl.when(i == 0)
        def _():
            wslot[0] = 0
            for cp in weight_copies(e, 0):
                cp.start()

        @pl.when((i == 0) | (e != e_prev))
        def _():
            ws = wslot[0]
            for cp in weight_copies(e, ws):
                cp.wait()
            wb1[...] = wst1[ws].astype(BF16)
            wb3[...] = wst3[ws].astype(BF16)
            wb2[...] = wst2[ws].astype(BF16)
            nb = nxt_ref[i]

            @pl.when(nb >= 0)
            def _():
                for cp in weight_copies(be_ref[jnp.maximum(nb, 0)], 1 - ws):
                    cp.start()
            wslot[0] = 1 - ws

        wait_gather(xslot)
        u = xbuf[xslot]
        xa, xb = _unpack_bf16_pair(u[:, :half])
        wv = pltpu.bitcast(u[:, half:], F32)
        roww = wv[:, 0:1] + wv[:, 1:2] + wv[:, 2:3]
        a = _dot(xa, wb1[:half, :]) + _dot(xb, wb1[half:, :])
        b = _dot(xa, wb3[:half, :]) + _dot(xb, wb3[half:, :])
        hmid = (a * jax.nn.sigmoid(a) * b).astype(BF16)
        y = _dot(hmid, wb2[...]) * roww
        gather(jnp.minimum(i + 2, last), (i + 2) % GATHER_BUFS)
        ybuf[slot] = _pack_bf16_pair(y[:, :half], y[:, half:])
        scatter(i, slot)

    @pl.when(i == last)
    def _():
        @pl.when((last >= 1) & (nv_ref[jnp.maximum(last - 1, 0)] > 0))
        def _():
            wait_gather((last + 1) % GATHER_BUFS)

        @pl.when(nv > 0)
        def _():
            wait_gather((last + 2) % GATHER_BUFS)

        @pl.when(last >= 1)
        def _():
            wait_scatter(1 - slot, nv_ref[jnp.maximum(last - 1, 0)])
        wait_scatter(slot, nv)


def _moe(xs, plan, w1, w3, w2):
    n_blk = plan[0].shape[0]
    rows = MOE_ROWS
    half = D_MODEL // 2
    hbm = pl.BlockSpec(memory_space=pl.ANY)
    grid_spec = pltpu.PrefetchScalarGridSpec(
        num_scalar_prefetch=5,
        grid=(n_blk,),
        in_specs=[hbm] * 4,
        out_specs=hbm,
        scratch_shapes=[pltpu.VMEM((GATHER_BUFS, rows, XS_WIDTH), jnp.uint32),
                        pltpu.VMEM((2, rows, half), jnp.uint32),
                        pltpu.VMEM((D_MODEL, EXPERT_HIDDEN), BF16),
                        pltpu.VMEM((D_MODEL, EXPERT_HIDDEN), BF16),
                        pltpu.VMEM((EXPERT_HIDDEN, D_MODEL), BF16),
                        pltpu.VMEM((2, D_MODEL, EXPERT_HIDDEN), F32),
                        pltpu.VMEM((2, D_MODEL, EXPERT_HIDDEN), F32),
                        pltpu.VMEM((2, EXPERT_HIDDEN, D_MODEL), F32),
                        pltpu.SMEM((1,), jnp.int32),
                        pltpu.SemaphoreType.DMA((GATHER_BUFS,)),
                        pltpu.SemaphoreType.DMA((2,)),
                        pltpu.SemaphoreType.DMA((2,))])
    return pl.pallas_call(
        _moe_kernel,
        grid_spec=grid_spec,
        out_shape=jax.ShapeDtypeStruct((xs.shape[0], half), jnp.uint32),
        compiler_params=pltpu.CompilerParams(dimension_semantics=("arbitrary",),
                                             vmem_limit_bytes=VMEM_LIMIT),
        name="moe_ffn",
    )(*plan, w1, w3, w2, xs)


def _combine_kernel(x1_ref, ys_ref, pos_ref, g2_ref, o_ref):
    tt = x1_ref.shape[0]
    half = D_MODEL // 2
    pos = pos_ref[...]
    pcol = lax.broadcasted_iota(jnp.int32, (tt, ys_ref.shape[0]), 1).astype(F32)
    sel = (jnp.where(pcol == pos[:, 0:1], 1.0, 0.0) + jnp.where(pcol == pos[:, 1:2], 1.0, 0.0))
    sel = sel.astype(BF16)
    lo, hi = _unpack_bf16_pair(ys_ref[...])
    g2 = g2_ref[0]
    x1 = x1_ref[...]
    o_ref[:, :half] = x1[:, :half] + g2[:, :half] * _dot(sel, lo)
    o_ref[:, half:] = x1[:, half:] + g2[:, half:] * _dot(sel, hi)


def _combine(x1, ys, pos, g2, S):
    T = x1.shape[0]
    tt = MOE_TILE
    per_b = S // tt
    return pl.pallas_call(
        _combine_kernel,
        grid=(T // tt,),
        in_specs=[pl.BlockSpec((tt, D_MODEL), lambda i: (i, 0)),
                  pl.BlockSpec((TILE_ROWS, D_MODEL // 2), lambda i: (i, 0)),
                  pl.BlockSpec((tt, LANES), lambda i: (i, 0)),
                  pl.BlockSpec((1, 1, D_MODEL), lambda i: (i // per_b, 0, 0))],
        out_specs=pl.BlockSpec((tt, D_MODEL), lambda i: (i, 0)),
        out_shape=jax.ShapeDtypeStruct((T, D_MODEL), F32),
        compiler_params=pltpu.CompilerParams(vmem_limit_bytes=VMEM_LIMIT),
        name="moe_combine",
    )(x1, ys, pos, g2)


def _prep_w_in(w_in):
    dq = w_in[:, OFF_DIL_Q:OFF_DIL_K]
    dk = w_in[:, OFF_DIL_K:OFF_DIL_V]
    dv = w_in[:, OFF_DIL_V:OFF_GATE_A]
    dil = []
    for g in range(N_DIL_GROUPS):
        p0 = slice(g * DIL_OUT_WIDTH, g * DIL_OUT_WIDTH + LANES)
        p1 = slice(g * DIL_OUT_WIDTH + LANES, (g + 1) * DIL_OUT_WIDTH)
        dil += [dq[:, p0], dk[:, p0], dq[:, p1], dk[:, p1], dv[:, p0], dv[:, p1]]
    pad = jnp.zeros((D_MODEL, UNIT - FOX_HEADS), w_in.dtype)
    cols = [w_in[:, OFF_GATE_A:OFF_GATE_B], w_in[:, OFF_GATE_B:N_IN],
            w_in[:, OFF_FOX_Q:OFF_FOX_K], w_in[:, OFF_FOX_K:OFF_FOX_V], w_in[:, OFF_FOX_V:OFF_FOX_F],
            *dil, w_in[:, OFF_FOX_F:OFF_DIL_Q], pad]
    return jnp.concatenate(cols, axis=1).astype(BF16)


def _prep_gain(q_gain, k_gain):
    qs = HEAD_DIM ** -0.5 * LOG2E
    ones = jnp.ones((UNIT,), F32)
    fq = q_gain[:FOX_HEADS].reshape(-1) * qs
    fk = k_gain[:FOX_HEADS].reshape(-1)
    dq = q_gain[FOX_HEADS:].reshape(-1) * qs
    dk = k_gain[FOX_HEADS:].reshape(-1)
    dil = []
    for g in range(N_DIL_GROUPS):
        p0 = slice(g * DIL_OUT_WIDTH, g * DIL_OUT_WIDTH + LANES)
        p1 = slice(g * DIL_OUT_WIDTH + LANES, (g + 1) * DIL_OUT_WIDTH)
        dil += [dq[p0], dk[p0], dq[p1], dk[p1], ones]
    parts = [ones] * 8 + [fq, fk, ones, ones] + dil + [ones]
    return jnp.concatenate(parts).reshape(1, N_UNITS * UNIT)


def _layer(x, mod, rel_bias_table, norm1_g, w_in, b_forget, q_gain, k_gain, w_branch_a, w_branch_b,
           w_out, norm2_g, w_rg, b_rg, w_re, b_re, w1, w3, w2):
    B, S, D = x.shape
    T = B * S
    sh1, sc1, g1, sh2, sc2, g2 = [m.reshape(B, 1, D) for m in jnp.split(mod, 6, axis=-1)]
    x2d = x.reshape(T, D)

    p2d, fgt, slabs = _inproj(x2d, norm1_g.reshape(1, D), sc1, sh1, _prep_w_in(w_in),
                              _prep_gain(q_gain, k_gain), S)
    p3 = p2d.reshape(B, S, P_WIDTH)
    ck = _fcum(fgt.reshape(B, S, LANES), b_forget)
    ya = _fox(p3, ck)
    yd = _dil(slabs, _relbias(rel_bias_table))

    n_router = N_GROUPS + N_EXPERTS
    wr = jnp.concatenate([w_rg, w_re, jnp.zeros((D, LANES - n_router), F32)], axis=1).astype(BF16)
    br = jnp.concatenate([b_rg, b_re, jnp.zeros((LANES - n_router,), F32)]).reshape(1, LANES)
    x1, h2, logits = _outproj(x2d, ya.reshape(T, FOX_WIDTH), yd.reshape(T, DIL_OUT_WIDTH), p2d,
                              g1, sc2, sh2, norm2_g.reshape(1, D),
                              w_branch_a.astype(BF16), w_branch_b.astype(BF16), w_out.astype(BF16),
                              wr, br, S)
    xs, cnt, pos = _dispatch(logits, h2)
    cnt2 = cnt[:, :, 0]
    n_blk = cnt.shape[0] * TILE_ROWS // MOE_ROWS + N_EXPERTS
    plan = _plan(cnt2, n_blk)
    ys = _moe(xs, plan, w1, w3, w2)
    out = _combine(x1, ys, pos, g2, S)
    return out.reshape(B, S, D)


def kernel(x, c, rel_bias_table, w_ada, b_ada, norm1_g, w_in, b_forget, q_gain, k_gain, w_branch_a, w_branch_b, w_out, norm2_g, w_router_group, b_router_group, w_router_expert, b_router_expert, w1, w3, w2):
    depth = w_ada.shape[0]
    for l in range(depth):
        mod = _ada(c, w_ada[l], b_ada[l])
        x = _layer(x, mod, rel_bias_table, norm1_g[l], w_in[l], b_forget[l], q_gain[l], k_gain[l],
                   w_branch_a[l], w_branch_b[l], w_out[l], norm2_g[l], w_router_group[l],
                   b_router_group[l], w_router_expert[l], b_router_expert[l], w1[l], w3[l], w2[l])
    return x
```

```python
import functools
import math

import numpy as np
import jax
import jax.numpy as jnp
from jax import lax
from jax.experimental import pallas as pl
from jax.experimental.pallas import tpu as pltpu

F32 = jnp.float32
BF16 = jnp.bfloat16

D_MODEL = 1024
HEAD_DIM = 64
FOX_HEADS = 8
DIL_GROUPS = ((128, 1), (512, 4), (2048, 16))
DIL_HEADS_PER_GROUP = 4
N_DIL_GROUPS = len(DIL_GROUPS)
DIL_HEADS = N_DIL_GROUPS * DIL_HEADS_PER_GROUP
FOX_WIDTH = FOX_HEADS * HEAD_DIM
DIL_WIDTH = DIL_HEADS * HEAD_DIM
DIL_OUT_WIDTH = DIL_HEADS_PER_GROUP * HEAD_DIM
NUM_BUCKETS = 32
REL_MAX_DISTANCE = 2048
N_GROUPS = 4
EXPERTS_PER_GROUP = 8
N_EXPERTS = N_GROUPS * EXPERTS_PER_GROUP
EXPERT_HIDDEN = D_MODEL // 2
EPS = 1e-6
LOG2E = math.log2(math.e)

OFF_FOX_Q = 0
OFF_FOX_K = OFF_FOX_Q + FOX_WIDTH
OFF_FOX_V = OFF_FOX_K + FOX_WIDTH
OFF_FOX_F = OFF_FOX_V + FOX_WIDTH
OFF_DIL_Q = OFF_FOX_F + FOX_HEADS
OFF_DIL_K = OFF_DIL_Q + DIL_WIDTH
OFF_DIL_V = OFF_DIL_K + DIL_WIDTH
OFF_GATE_A = OFF_DIL_V + DIL_WIDTH
OFF_GATE_B = OFF_GATE_A + D_MODEL
N_IN = OFF_GATE_B + D_MODEL

LANES = 128
UNIT = 256
DIL_L = 128

U_GATE_A, U_GATE_B, U_FOX_Q, U_FOX_K, U_FOX_V, U_DIL, U_FORGET = 0, 4, 8, 10, 12, 14, 23
N_UNITS = 24
P_WIDTH = U_DIL * UNIT
N_SLABS = 3 * N_DIL_GROUPS
_KIND = (["gate"] * 8 + ["norm"] * 4 + ["plain"] * 2 + ["norm", "norm", "plain"] * 3 + ["forget"])

TM_INPROJ = 1024
TM_PROJ = 1024
TQ_FOX = 512
MOE_ROWS = 256
MOE_TILE = 512
XS_WIDTH = D_MODEL // 2 + LANES
SUBLANES = 8
TILE_ROWS = 2 * MOE_TILE + N_EXPERTS * SUBLANES
MOE_GROUPS = MOE_ROWS // SUBLANES
GATHER_BUFS = 3
VMEM_LIMIT = 56 * 1024 * 1024


def _dot(a, b):
    return jnp.dot(a, b, preferred_element_type=F32)


def _dot_nt(a, b):
    return lax.dot_general(a, b, (((1,), (1,)), ((), ())), preferred_element_type=F32)


def _split3(x):
    hi = x.astype(BF16)
    r1 = x - hi.astype(F32)
    mid = r1.astype(BF16)
    lo = (r1 - mid.astype(F32)).astype(BF16)
    return hi, mid, lo


def _ada_kernel(c_ref, w_ref, b_ref, o_ref):
    c = c_ref[...]
    s = c * jax.nn.sigmoid(c)
    s_hi = s.astype(BF16)
    s_lo = (s - s_hi.astype(F32)).astype(BF16)
    w = w_ref[...]
    w_hi = w.astype(BF16)
    w_lo = (w - w_hi.astype(F32)).astype(BF16)
    acc = _dot(s_hi, w_hi) + _dot(s_hi, w_lo) + _dot(s_lo, w_hi)
    o_ref[...] = acc + b_ref[...]


def _ada(c, w_ada, b_ada):
    B = c.shape[0]
    n_out = w_ada.shape[1]
    tn = 512
    return pl.pallas_call(
        _ada_kernel,
        grid=(n_out // tn,),
        in_specs=[pl.BlockSpec((B, D_MODEL), lambda j: (0, 0)),
                  pl.BlockSpec((D_MODEL, tn), lambda j: (0, j)),
                  pl.BlockSpec((1, tn), lambda j: (0, j))],
        out_specs=pl.BlockSpec((B, tn), lambda j: (0, j)),
        out_shape=jax.ShapeDtypeStruct((B, n_out), F32),
        name="ada_mod",
    )(c, w_ada, b_ada.reshape(1, n_out))


def _pack_bf16_pair(lo, hi):
    lo_bits = pltpu.bitcast(lo.astype(BF16).astype(F32), jnp.uint32) >> 16
    hi_bits = pltpu.bitcast(hi.astype(BF16).astype(F32), jnp.uint32) & jnp.uint32(0xFFFF0000)
    return lo_bits | hi_bits


def _unpack_bf16_pair(u):
    lo = pltpu.bitcast(u << 16, F32).astype(BF16)
    hi = pltpu.bitcast(u & jnp.uint32(0xFFFF0000), F32).astype(BF16)
    return lo, hi


def _inproj_kernel(x_ref, g_ref, sc_ref, sh_ref, wg_ref, wf_ref, wd_ref, wz_ref, gain_ref, bd_ref,
                   p_ref, f_ref, s_ref):
    w_groups = ((U_GATE_A, wg_ref), (U_FOX_Q, wf_ref), (U_DIL, wd_ref), (U_FORGET, wz_ref))
    x = x_ref[...]
    ms = jnp.mean(x * x, axis=-1, keepdims=True)
    h = x * lax.rsqrt(ms + EPS) * g_ref[...]
    h = h * (1.0 + sc_ref[0]) + sh_ref[0]
    hb = h.astype(BF16)

    def unit(u):
        cols = slice(u * UNIT, (u + 1) * UNIT)
        u0, w_ref = [grp for grp in w_groups if grp[0] <= u][-1]
        acc = _dot(hb, w_ref[:, (u - u0) * UNIT:(u - u0 + 1) * UNIT])
        kind = _KIND[u]
        if kind == "gate":
            return jax.nn.sigmoid(acc)
        if kind == "norm":
            ss = _dot((acc * acc).astype(BF16), bd_ref[...])
            return acc * lax.rsqrt(ss * (1.0 / HEAD_DIM) + EPS) * gain_ref[:, cols]
        return acc

    def emit(u):
        o = unit(u)
        if u < U_DIL:
            p_ref[:, u * UNIT:(u + 1) * UNIT] = o.astype(BF16)
        elif u < U_FORGET:
            s_ref[0, u - U_DIL] = _pack_bf16_pair(o[:, :LANES], o[:, LANES:])
        else:
            f_ref[...] = o[:, :LANES]

    normed = [u for u in range(N_UNITS) if _KIND[u] == "norm"]
    others = [u for u in range(N_UNITS) if _KIND[u] != "norm"]
    while normed or others:
        for group in (normed, others):
            if group:
                emit(group.pop(0))


def _inproj(x2d, norm_g, sc, sh, w_re, gain_row, S):
    T = x2d.shape[0]
    tm = TM_INPROJ
    per_b = S // tm
    bd = np.kron(np.eye(UNIT // HEAD_DIM), np.ones((HEAD_DIM, HEAD_DIM))).astype(np.float32)
    once = dict(pipeline_mode=pl.Buffered(1))
    return pl.pallas_call(
        _inproj_kernel,
        grid=(T // tm,),
        in_specs=[pl.BlockSpec((tm, D_MODEL), lambda i: (i, 0)),
                  pl.BlockSpec((1, D_MODEL), lambda i: (0, 0)),
                  pl.BlockSpec((1, 1, D_MODEL), lambda i: (i // per_b, 0, 0)),
                  pl.BlockSpec((1, 1, D_MODEL), lambda i: (i // per_b, 0, 0)),
                  *[pl.BlockSpec(w.shape, lambda i: (0, 0), **once) for w in w_re],
                  pl.BlockSpec((1, N_UNITS * UNIT), lambda i: (0, 0), **once),
                  pl.BlockSpec((UNIT, UNIT), lambda i: (0, 0), **once)],
        out_specs=[pl.BlockSpec((tm, P_WIDTH), lambda i: (i, 0)),
                   pl.BlockSpec((tm, LANES), lambda i: (i, 0)),
                   pl.BlockSpec((1, N_SLABS, tm, LANES), lambda i: (i // per_b, 0, i % per_b, 0))],
        out_shape=[jax.ShapeDtypeStruct((T, P_WIDTH), BF16),
                   jax.ShapeDtypeStruct((T, LANES), F32),
                   jax.ShapeDtypeStruct((T // S, N_SLABS, S, LANES), jnp.uint32)],
        compiler_params=pltpu.CompilerParams(vmem_limit_bytes=VMEM_LIMIT),
        name="in_proj",
    )(x2d, norm_g, sc, sh, *w_re, gain_row, jnp.asarray(bd, BF16))


def _fcum_kernel(f_ref, b_ref, tri_ref, o_ref):
    S = f_ref.shape[1]
    xf = f_ref[0] + b_ref[...]
    ls = (jnp.minimum(xf, 0.0) - jnp.log(1.0 + jnp.exp(-jnp.abs(xf)))) * LOG2E
    lst = ls.T
    carry = jnp.zeros((LANES, UNIT), F32)
    for blk in range(S // UNIT):
        seg = lst[:, blk * UNIT:(blk + 1) * UNIT]
        hi, mid, lo = _split3(seg)
        tri = tri_ref[...]
        res = _dot(hi, tri) + _dot(mid, tri) + _dot(lo, tri)
        o_ref[0, :, blk * UNIT:(blk + 1) * UNIT] = (res[:, :UNIT] + carry)[:FOX_HEADS]
        carry = carry + res[:, UNIT:]


def _fcum(fgt, b_forget):
    B, S, _ = fgt.shape
    brow = jnp.zeros((1, LANES), F32).at[0, :FOX_HEADS].set(b_forget)
    tri = np.concatenate([np.triu(np.ones((UNIT, UNIT))), np.ones((UNIT, UNIT))], axis=1)
    return pl.pallas_call(
        _fcum_kernel,
        grid=(B,),
        in_specs=[pl.BlockSpec((1, S, LANES), lambda b: (b, 0, 0)),
                  pl.BlockSpec((1, LANES), lambda b: (0, 0)),
                  pl.BlockSpec((UNIT, 2 * UNIT), lambda b: (0, 0))],
        out_specs=pl.BlockSpec((1, FOX_HEADS, S), lambda b: (b, 0, 0)),
        out_shape=jax.ShapeDtypeStruct((B, FOX_HEADS, S), F32),
        name="forget_cumsum",
    )(fgt, brow, jnp.asarray(tri, BF16))


def _fox_kernel(q_ref, k_ref, v_ref, ck_ref, o_ref):
    S = q_ref.shape[1]
    pair = pl.program_id(1)
    tq = TQ_FOX
    lane = lax.broadcasted_iota(jnp.int32, (1, LANES), 1)
    row = lax.broadcasted_iota(jnp.int32, (tq, tq), 0)
    col = lax.broadcasted_iota(jnp.int32, (tq, tq), 1)
    causal = col <= row
    cks = [ck_ref[0, pl.ds(2 * pair + hh, 1), :] for hh in range(2)]
    for t in reversed(range(S // tq)):
        r0, r1 = t * tq, (t + 1) * tq
        qt = q_ref[0, r0:r1, :]
        outs = []
        for hh in range(2):
            hsel = (lane >= HEAD_DIM) == bool(hh)
            qm = jnp.where(hsel, qt, jnp.zeros_like(qt))
            ck = cks[hh]
            s = _dot_nt(qm, k_ref[0, :r1, :]) - ck[:, :r1]
            s_d = jnp.where(causal, s[:, r0:], -jnp.inf)
            s = jnp.concatenate([s[:, :r0], s_d], axis=1) if t > 0 else s_d
            m = jnp.max(s, axis=-1, keepdims=True)
            p = jnp.exp2(s - m)
            l = jnp.sum(p, axis=-1, keepdims=True)
            outs.append(_dot(p.astype(BF16), v_ref[0, :r1, :]) / l)
        o_ref[0, r0:r1, :] = jnp.where(lane < HEAD_DIM, outs[0], outs[1]).astype(BF16)


def _fox(p3, ck):
    B, S, _ = p3.shape
    nq, nk, nv = (U_FOX_Q * UNIT // LANES, U_FOX_K * UNIT // LANES, U_FOX_V * UNIT // LANES)
    return pl.pallas_call(
        _fox_kernel,
        grid=(B, FOX_HEADS // 2),
        in_specs=[pl.BlockSpec((1, S, LANES), lambda b, p: (b, 0, nq + p)),
                  pl.BlockSpec((1, S, LANES), lambda b, p: (b, 0, nk + p)),
                  pl.BlockSpec((1, S, LANES), lambda b, p: (b, 0, nv + p)),
                  pl.BlockSpec((1, FOX_HEADS, S), lambda b, p: (b, 0, 0))],
        out_specs=pl.BlockSpec((1, S, LANES), lambda b, p: (b, 0, p)),
        out_shape=jax.ShapeDtypeStruct((B, S, FOX_WIDTH), BF16),
        compiler_params=pltpu.CompilerParams(vmem_limit_bytes=VMEM_LIMIT),
        name="fox_attn",
    )(p3, p3, p3, ck)


def _t5_bucket(dist):
    max_exact = NUM_BUCKETS // 2
    d = np.maximum(dist, 1).astype(np.float32)
    large = max_exact + (np.log(d / max_exact) / np.log(REL_MAX_DISTANCE / max_exact)
                         * (NUM_BUCKETS - max_exact)).astype(np.int32)
    large = np.minimum(large, NUM_BUCKETS - 1)
    return np.where(dist < max_exact, dist, large).astype(np.int32)


def _relbias_kernel(tab_ref, bucket_ref, valid_ref, o_ref):
    g = pl.program_id(0)
    bk = bucket_ref[0]
    vd = valid_ref[0]
    for hs in range(DIL_HEADS_PER_GROUP):
        acc = jnp.zeros(bk.shape, F32)
        for b in range(NUM_BUCKETS):
            acc = jnp.where(bk == b, tab_ref[b, g * DIL_HEADS_PER_GROUP + hs], acc)
        bias = jnp.where(vd != 0, acc * LOG2E, -jnp.inf)
        o_ref[0, hs] = bias
        col = lax.broadcasted_iota(jnp.int32, bias.shape, 1)
        o_ref[1, hs] = jnp.where(col >= DIL_L, bias, -jnp.inf)


def _relbias(table):
    L = DIL_L
    i = np.arange(L)[:, None]
    j = np.arange(2 * L)[None, :]
    m = L + i - j
    valid = ((m >= 0) & (m <= L)).astype(np.int32)
    buckets = np.stack([_t5_bucket(np.clip(m, 0, None) * d) for _, d in DIL_GROUPS])
    valids = np.stack([valid] * N_DIL_GROUPS)
    return pl.pallas_call(
        _relbias_kernel,
        grid=(N_DIL_GROUPS,),
        in_specs=[pl.BlockSpec(memory_space=pltpu.SMEM),
                  pl.BlockSpec((1, L, 2 * L), lambda g: (g, 0, 0)),
                  pl.BlockSpec((1, L, 2 * L), lambda g: (g, 0, 0))],
        out_specs=pl.BlockSpec((2, DIL_HEADS_PER_GROUP, L, 2 * L), lambda g: (0, g, 0, 0)),
        out_shape=jax.ShapeDtypeStruct((2, DIL_HEADS, L, 2 * L), F32),
        name="rel_bias",
    )(table, jnp.asarray(buckets), jnp.asarray(valids))


def _dil_rows(start, d):
    return pl.ds(start, DIL_L) if d == 1 else pl.ds(start, DIL_L, stride=d)


def _dil_block_rows(d, nb, it):
    r, n = it // nb, it % nb
    cur = _dil_rows(r + d * (n * DIL_L), d)
    prev = _dil_rows(r + d * (jnp.maximum(n - 1, 0) * DIL_L), d)
    return cur, prev, 1 - jnp.minimum(n, 1)


def _dil_scores(qkv_ref, bias_ref, s_scr, slot, g, d, nb, it):
    lane = lax.broadcasted_iota(jnp.int32, (1, LANES), 1)
    cur, prev, first = _dil_block_rows(d, nb, it)
    for pr in range(2):
        qt, kt = _unpack_bf16_pair(qkv_ref[0, 3 * g + pr, cur, :])
        if nb > 1:
            _, k_prev = _unpack_bf16_pair(qkv_ref[0, 3 * g + pr, prev, :])
            kt = jnp.concatenate([k_prev, kt], axis=0)
        for hh in range(2):
            hsel = (lane >= HEAD_DIM) == bool(hh)
            qm = jnp.where(hsel, qt, jnp.zeros_like(qt))
            head = 2 * pr + hh
            if nb > 1:
                s_scr[slot, head] = _dot_nt(qm, kt) + bias_ref[first, DIL_HEADS_PER_GROUP * g + head]
            else:
                bias = bias_ref[0, DIL_HEADS_PER_GROUP * g + head, :, DIL_L:]
                s_scr[slot, head, :, :DIL_L] = _dot_nt(qm, kt) + bias


def _dil_merge(qkv_ref, s_scr, slot, m_scr, l_scr, acc_scr, g, d, nb, init, it):
    lane = lax.broadcasted_iota(jnp.int32, (1, LANES), 1)
    cur, prev, _ = _dil_block_rows(d, nb, it)
    v_cur = _unpack_bf16_pair(qkv_ref[0, 3 * g + 2, cur, :])
    if nb > 1:
        v_prev = _unpack_bf16_pair(qkv_ref[0, 3 * g + 2, prev, :])
    for pr in range(2):
        vt = jnp.concatenate([v_prev[pr], v_cur[pr]], axis=0) if nb > 1 else v_cur[pr]
        ms, ls, accs = [], [], []
        for hh in range(2):
            s = s_scr[slot, 2 * pr + hh] if nb > 1 else s_scr[slot, 2 * pr + hh, :, :DIL_L]
            m = jnp.max(s, axis=-1, keepdims=True)
            p = jnp.exp2(s - m)
            ms.append(m)
            ls.append(jnp.sum(p, axis=-1, keepdims=True))
            accs.append(_dot(p.astype(BF16), vt))
        low = lane < HEAD_DIM
        m_b = jnp.where(low, ms[0], ms[1])
        l_b = jnp.where(low, ls[0], ls[1])
        acc_b = jnp.where(low, accs[0], accs[1])
        if init:
            m_scr[pr, cur, :] = m_b
            l_scr[pr, cur, :] = l_b
            acc_scr[pr, cur, :] = acc_b
        else:
            m_o = m_scr[pr, cur, :]
            m_n = jnp.maximum(m_o, m_b)
            a_o = jnp.exp2(m_o - m_n)
            a_b = jnp.exp2(m_b - m_n)
            m_scr[pr, cur, :] = m_n
            l_scr[pr, cur, :] = l_scr[pr, cur, :] * a_o + l_b * a_b
            acc_scr[pr, cur, :] = acc_scr[pr, cur, :] * a_o + acc_b * a_b


def _dil_kernel(qkv_ref, bias_ref, o_ref, m_scr, l_scr, acc_scr, s_scr):
    S = o_ref.shape[1]
    order = sorted(range(N_DIL_GROUPS), key=lambda g: -DIL_GROUPS[g][1])
    for g in order:
        window, d = DIL_GROUPS[g]
        nb = S // window
        total = d * nb
        assert total % 2 == 0
        scores = functools.partial(_dil_scores, qkv_ref, bias_ref, s_scr, g=g, d=d, nb=nb)
        merge = functools.partial(_dil_merge, qkv_ref, s_scr, m_scr=m_scr, l_scr=l_scr,
                                  acc_scr=acc_scr, g=g, d=d, nb=nb, init=g == order[0])

        scores(slot=0, it=0)

        def body(j, carry, scores=scores, merge=merge, total=total):
            scores(slot=1, it=2 * j + 1)
            merge(slot=0, it=2 * j)
            scores(slot=0, it=jnp.minimum(2 * j + 2, total - 1))
            merge(slot=1, it=2 * j + 1)
            return carry
        lax.fori_loop(0, total // 2, body, 0, unroll=2)
    for pr in range(2):
        o_ref[0, :, pr * LANES:(pr + 1) * LANES] = (acc_scr[pr] / l_scr[pr]).astype(BF16)


def _dil(slabs, bias):
    B, _, S, _ = slabs.shape
    for window, d in DIL_GROUPS:
        assert window // d == DIL_L and S % window == 0
    stat = pltpu.VMEM((2, S, LANES), F32)
    return pl.pallas_call(
        _dil_kernel,
        grid=(B,),
        in_specs=[pl.BlockSpec((1, N_SLABS, S, LANES), lambda b: (b, 0, 0, 0)),
                  pl.BlockSpec(bias.shape, lambda b: (0, 0, 0, 0))],
        out_specs=pl.BlockSpec((1, S, DIL_OUT_WIDTH), lambda b: (b, 0, 0)),
        out_shape=jax.ShapeDtypeStruct((B, S, DIL_OUT_WIDTH), BF16),
        scratch_shapes=[stat, stat, stat,
                        pltpu.VMEM((2, DIL_HEADS_PER_GROUP, DIL_L, 2 * DIL_L), F32)],
        compiler_params=pltpu.CompilerParams(vmem_limit_bytes=VMEM_LIMIT),
        name="dil_attn",
    )(slabs, bias)


def _outproj_kernel(x_ref, ya_ref, yd_ref, ga_ref, gb_ref,
                    g1_ref, sc_ref, sh_ref, ng_ref, wa_ref, wb_ref, wo_ref, wr_ref, br_ref,
                    x1_ref, h2_ref, lg_ref):
    n_chunks = 2
    cm = x_ref.shape[0] // n_chunks
    for c in range(n_chunks):
        rows = slice(c * cm, (c + 1) * cm)
        a = _dot(ya_ref[rows, :], wa_ref[...])
        bm = _dot(yd_ref[rows, :], wb_ref[...])
        merged = ga_ref[rows, :].astype(F32) * a + gb_ref[rows, :].astype(F32) * bm
        out = _dot(merged.astype(BF16), wo_ref[...])
        x1 = x_ref[rows, :] + g1_ref[0] * out
        x1_ref[rows, :] = x1
        ms = jnp.mean(x1 * x1, axis=-1, keepdims=True)
        h = x1 * lax.rsqrt(ms + EPS) * ng_ref[...]
        h = h * (1.0 + sc_ref[0]) + sh_ref[0]
        hb = h.astype(BF16)
        h2_ref[rows, :] = hb
        lg_ref[rows, :] = _dot(hb, wr_ref[...]) + br_ref[...]


def _outproj(x2d, ya2d, yd2d, p2d, g1, sc2, sh2, norm_g, wa, wb, wo, wr, br, S):
    T = x2d.shape[0]
    tm = TM_PROJ
    per_b = S // tm
    row = lambda w: pl.BlockSpec((tm, w), lambda i: (i, 0))
    full = lambda a: pl.BlockSpec(a.shape, lambda i: (0,) * a.ndim)
    mod = pl.BlockSpec((1, 1, D_MODEL), lambda i: (i // per_b, 0, 0))
    return pl.pallas_call(
        _outproj_kernel,
        grid=(T // tm,),
        in_specs=[row(D_MODEL), row(FOX_WIDTH), row(DIL_OUT_WIDTH)]
                 + [pl.BlockSpec((tm, D_MODEL), lambda i: (i, U_GATE_A * UNIT // D_MODEL)),
                    pl.BlockSpec((tm, D_MODEL), lambda i: (i, U_GATE_B * UNIT // D_MODEL)),
                    mod, mod, mod, full(norm_g), full(wa), full(wb), full(wo), full(wr), full(br)],
        out_specs=[row(D_MODEL), row(D_MODEL), row(LANES)],
        out_shape=[jax.ShapeDtypeStruct((T, D_MODEL), F32),
                   jax.ShapeDtypeStruct((T, D_MODEL), BF16),
                   jax.ShapeDtypeStruct((T, LANES), F32)],
        compiler_params=pltpu.CompilerParams(vmem_limit_bytes=VMEM_LIMIT),
        name="out_proj",
    )(x2d, ya2d, yd2d, p2d, p2d, g1, sc2, sh2, norm_g, wa, wb, wo, wr, br)


def _dispatch_kernel(lg_ref, h_ref, tri_ref, xs_ref, cnt_ref, pos_ref):
    tt = lg_ref.shape[0]
    lt = lg_ref[...].T
    row = lambda i: lt[i:i + 1, :]
    neg = -jnp.inf
    g = [row(i) for i in range(N_GROUPS)]
    gmax = functools.reduce(jnp.maximum, g)
    gidx = jnp.full(gmax.shape, N_GROUPS - 1, jnp.int32)
    for i in reversed(range(N_GROUPS - 1)):
        gidx = jnp.where(g[i] == gmax, i, gidx)
    gsum = sum(jnp.exp(gi - gmax) for gi in g)
    el = []
    for j in range(EXPERTS_PER_GROUP):
        v = row(N_GROUPS + EXPERTS_PER_GROUP * (N_GROUPS - 1) + j)
        for gg in reversed(range(N_GROUPS - 1)):
            v = jnp.where(gidx == gg, row(N_GROUPS + EXPERTS_PER_GROUP * gg + j), v)
        el.append(v)

    def top(vals):
        best = functools.reduce(jnp.maximum, vals)
        idx = jnp.full(best.shape, EXPERTS_PER_GROUP - 1, jnp.int32)
        for j in reversed(range(EXPERTS_PER_GROUP - 1)):
            idx = jnp.where(vals[j] == best, j, idx)
        return best, idx

    v1, i1 = top(el)
    v2, i2 = top([jnp.where(i1 == j, neg, el[j]) for j in range(EXPERTS_PER_GROUP)])
    t = jnp.exp(v2 - v1)
    den = (1.0 + t) * gsum
    wts = [1.0 / den, t / den]
    eid = [gidx * EXPERTS_PER_GROUP + i1, gidx * EXPERTS_PER_GROUP + i2]

    esub = lax.broadcasted_iota(jnp.int32, (N_EXPERTS, tt), 0)
    ohf = jnp.concatenate([jnp.where(esub == eid[k], 1.0, 0.0) for k in range(2)], axis=1)
    n_pb = 2 * tt // UNIT
    oh_blocks = jnp.concatenate([ohf[:, b * UNIT:(b + 1) * UNIT] for b in range(n_pb)], axis=0)
    res = _dot(oh_blocks.astype(BF16), tri_ref[...])
    cnt = jnp.zeros((N_EXPERTS, LANES), F32)
    pre = []
    for b in range(n_pb):
        r = res[b * N_EXPERTS:(b + 1) * N_EXPERTS]
        pre.append(r[:, :UNIT] + jnp.concatenate([cnt] * (UNIT // LANES), axis=1))
        cnt = cnt + r[:, UNIT:]
    prefix = jnp.concatenate(pre, axis=1)
    cnt = (((cnt.astype(jnp.int32) + (SUBLANES - 1)) // SUBLANES) * SUBLANES).astype(F32)
    esub_c = lax.broadcasted_iota(jnp.int32, cnt.shape, 0)
    start = jnp.zeros_like(cnt)
    for e in range(N_EXPERTS - 1):
        start = start + jnp.where(esub_c > e, cnt[e:e + 1, :], 0.0)
    start_w = jnp.concatenate([start] * (2 * tt // LANES), axis=1)
    pos = jnp.sum(ohf * (start_w + prefix), axis=0, keepdims=True)
    pos_k = [pos[:, :tt], pos[:, tt:]]

    n_rows = xs_ref.shape[0]
    psub = lax.broadcasted_iota(jnp.int32, (n_rows, tt), 0).astype(F32)
    pm = [jnp.where(psub == pos_k[k], 1.0, 0.0).astype(BF16) for k in range(2)]
    xs = _dot(pm[0] + pm[1], h_ref[...])
    wsub = lax.broadcasted_iota(jnp.int32, (LANES, tt), 0)
    ws = jnp.zeros((n_rows, LANES), F32)
    for k in range(2):
        parts = _split3(wts[k])
        wrows = jnp.zeros((LANES, tt), F32)
        for j in range(3):
            wrows = jnp.where(wsub == j, parts[j].astype(F32), wrows)
        ws = ws + _dot_nt(pm[k], wrows.astype(BF16))
    half = D_MODEL // 2
    xs_ref[:, :half] = _pack_bf16_pair(xs[:, :half], xs[:, half:])
    xs_ref[:, half:] = pltpu.bitcast(ws, jnp.uint32)
    cnt_ref[0] = cnt.astype(jnp.int32)
    posr = jnp.where(wsub == 0, pos_k[0], jnp.where(wsub == 1, pos_k[1], 0.0))
    pos_ref[...] = posr.T


def _dispatch(logits, h2):
    T = logits.shape[0]
    tt = MOE_TILE
    n_tiles = T // tt
    tri = np.concatenate([np.triu(np.ones((UNIT, UNIT)), 1), np.ones((UNIT, LANES))], axis=1)
    return pl.pallas_call(
        _dispatch_kernel,
        grid=(n_tiles,),
        in_specs=[pl.BlockSpec((tt, LANES), lambda i: (i, 0)),
                  pl.BlockSpec((tt, D_MODEL), lambda i: (i, 0)),
                  pl.BlockSpec(tri.shape, lambda i: (0, 0))],
        out_specs=[pl.BlockSpec((TILE_ROWS, XS_WIDTH), lambda i: (i, 0)),
                   pl.BlockSpec((1, N_EXPERTS, LANES), lambda i: (i, 0, 0)),
                   pl.BlockSpec((tt, LANES), lambda i: (i, 0))],
        out_shape=[jax.ShapeDtypeStruct((n_tiles * TILE_ROWS, XS_WIDTH), jnp.uint32),
                   jax.ShapeDtypeStruct((n_tiles, N_EXPERTS, LANES), jnp.int32),
                   jax.ShapeDtypeStruct((T, LANES), F32)],
        compiler_params=pltpu.CompilerParams(vmem_limit_bytes=VMEM_LIMIT),
        name="moe_dispatch",
    )(logits, h2, jnp.asarray(tri, BF16))


def _plan_kernel(cnt_ref, be_ref, nv_ref, nxt_ref, grp_ref, used_ref, cs_ref):
    n_tiles = cnt_ref.shape[0]
    n_blk = be_ref.shape[0]
    rows = MOE_ROWS
    row_shift = rows.bit_length() - 1
    grp_shift = SUBLANES.bit_length() - 1
    assert rows == 1 << row_shift and SUBLANES == 1 << grp_shift

    def tile_starts(t, c):
        def per_e(e, acc):
            cs_ref[t * N_EXPERTS + e] = acc
            return acc + cnt_ref[t, e]
        used_ref[t] = lax.fori_loop(0, N_EXPERTS, per_e, 0, unroll=8)
        return c
    lax.fori_loop(0, n_tiles, tile_starts, 0)

    def clear(b, c):
        nv_ref[b] = 0
        return c
    lax.fori_loop(0, n_blk, clear, 0)

    def clear_groups(g, c):
        grp_ref[g] = 0
        return c
    lax.fori_loop(0, n_blk * MOE_GROUPS, clear_groups, 0, unroll=8)

    def per_expert(e, b):
        g0 = b * MOE_GROUPS

        def per_tile(t, tot):
            c = cnt_ref[t, e]
            src = t * TILE_ROWS + cs_ref[t * N_EXPERTS + e]
            first = g0 + lax.shift_right_logical(tot, grp_shift)

            def per_group(k, cc):
                grp_ref[first + k] = src + k * SUBLANES
                return cc
            lax.fori_loop(0, lax.shift_right_logical(c, grp_shift), per_group, 0)
            return tot + c
        tot = lax.fori_loop(0, n_tiles, per_tile, 0)

        def per_block(j, c):
            be_ref[b + j] = e
            nv_ref[b + j] = jnp.minimum(rows, tot - j * rows)
            return c
        nb = lax.shift_right_logical(tot + rows - 1, row_shift)
        lax.fori_loop(0, nb, per_block, 0)
        return b + nb
    n_used = lax.fori_loop(0, N_EXPERTS, per_expert, 0)

    def unused(b, c):
        be_ref[b] = be_ref[n_used - 1]
        nxt_ref[b] = -1
        return c
    lax.fori_loop(n_used, n_blk, unused, 0)

    def next_run(k, nf):
        b = n_used - 1 - k
        nf = jnp.where(be_ref[b] != be_ref[jnp.minimum(b + 1, n_used - 1)], b + 1, nf)
        nxt_ref[b] = nf
        return nf
    lax.fori_loop(0, n_used, next_run, -1)


def _plan(cnt, n_blk):
    n_tiles = cnt.shape[0]
    smem = pl.BlockSpec(memory_space=pltpu.SMEM)
    i32 = lambda n: jax.ShapeDtypeStruct((n,), jnp.int32)
    return pl.pallas_call(
        _plan_kernel,
        in_specs=[smem],
        out_specs=[smem] * 5,
        out_shape=[i32(n_blk), i32(n_blk), i32(n_blk), i32(n_blk * MOE_GROUPS), i32(n_tiles)],
        scratch_shapes=[pltpu.SMEM((n_tiles * N_EXPERTS,), jnp.int32)],
        name="moe_plan",
    )(cnt)


def _pow2_pieces(n, fn):
    for b in reversed(range(SUBLANES.bit_length() - 1, MOE_ROWS.bit_length())):
        size = 1 << b

        @pl.when((n & size) != 0)
        def _():
            fn((n >> (b + 1)) << (b + 1), size)


def _moe_kernel(be_ref, nv_ref, nxt_ref, grp_ref, used_ref,
                w1_hbm, w3_hbm, w2_hbm, xs_hbm, ys_hbm,
                xbuf, ybuf, wb1, wb3, wb2, wst1, wst3, wst2, wslot, gsem, ssem, wsem):
    i = pl.program_id(0)
    last = pl.num_programs(0) - 1
    slot = i % 2
    nv = nv_ref[i]
    half = D_MODEL // 2
    grp_shift = SUBLANES.bit_length() - 1

    def group_row(blk, g):
        return pl.multiple_of(grp_ref[blk * MOE_GROUPS + g], SUBLANES)

    def gather(blk, s):
        top = jnp.maximum(lax.shift_right_logical(nv_ref[blk], grp_shift) - 1, 0)
        for g in range(MOE_GROUPS):
            src = group_row(blk, jnp.minimum(g, top))
            pltpu.make_async_copy(xs_hbm.at[pl.ds(src, SUBLANES)],
                                  xbuf.at[s, pl.ds(g * SUBLANES, SUBLANES)], gsem.at[s]).start()

    def wait_gather(s):
        pltpu.make_async_copy(xs_hbm.at[pl.ds(0, MOE_ROWS)], xbuf.at[s], gsem.at[s]).wait()

    def scatter(blk, s):
        def start(g):
            r = g * SUBLANES if isinstance(g, int) else pl.multiple_of(g * SUBLANES, SUBLANES)
            pltpu.make_async_copy(ybuf.at[s, pl.ds(r, SUBLANES)],
                                  ys_hbm.at[pl.ds(group_row(blk, g), SUBLANES)], ssem.at[s]).start()

        @pl.when(nv_ref[blk] == MOE_ROWS)
        def _():
            for g in range(MOE_GROUPS):
                start(g)

        @pl.when(nv_ref[blk] < MOE_ROWS)
        def _():
            def body(g, c):
                start(g)
                return c
            lax.fori_loop(0, lax.shift_right_logical(nv_ref[blk], grp_shift), body, 0)

    def wait_scatter(s, count):
        @pl.when(count == MOE_ROWS)
        def _():
            pltpu.make_async_copy(ybuf.at[s], ys_hbm.at[pl.ds(0, MOE_ROWS)], ssem.at[s]).wait()

        @pl.when(count < MOE_ROWS)
        def _():
            _pow2_pieces(count, lambda a, size: pltpu.make_async_copy(
                ybuf.at[s, pl.ds(0, size)], ys_hbm.at[pl.ds(0, size)], ssem.at[s]).wait())

    @pl.when(i == 0)
    def _():
        @pl.when(nv > 0)
        def _():
            gather(0, 0)
            gather(jnp.minimum(1, last), 1)

        ybuf[1] = jnp.zeros(ybuf.shape[1:], ybuf.dtype)
        n_tiles = used_ref.shape[0]

        def fill(t, c):
            row0 = t * TILE_ROWS + used_ref[t]
            _pow2_pieces(TILE_ROWS - used_ref[t], lambda a, size: pltpu.make_async_copy(
                ybuf.at[1, pl.ds(0, size)], ys_hbm.at[pl.ds(pl.multiple_of(row0 + a, SUBLANES), size)],
                ssem.at[1]).start())
            return c
        lax.fori_loop(0, n_tiles, fill, 0)

        def drain(t, c):
            wait_scatter(1, TILE_ROWS - used_ref[t])
            return c
        lax.fori_loop(0, n_tiles, drain, 0)

    @pl.when(i >= 2)
    def _():
        wait_scatter(slot, nv_ref[jnp.maximum(i - 2, 0)])

    xslot = i % GATHER_BUFS
    issuer_used = jnp.where(i >= 2, nv_ref[jnp.maximum(i - 2, 0)], nv_ref[0]) > 0

    @pl.when((nv == 0) & (i > 0) & issuer_used)
    def _():
        wait_gather(xslot)

    @pl.when(nv > 0)
    def _():
        e = be_ref[i]
        e_prev = be_ref[jnp.maximum(i - 1, 0)]

        def weight_copies(ex, ws):
            return [pltpu.make_async_copy(src.at[ex], dst.at[ws], wsem.at[ws])
                    for src, dst in ((w1_hbm, wst1), (w3_hbm, wst3), (w2_hbm, wst2))]

        @pl.when(i == 0)
        def _():
            wslot[0] = 0
            for cp in weight_copies(e, 0):
                cp.start()

        @pl.when((i == 0) | (e != e_prev))
        def _():
            ws = wslot[0]
            for cp in weight_copies(e, ws):
                cp.wait()
            wb1[...] = wst1[ws].astype(BF16)
            wb3[...] = wst3[ws].astype(BF16)
            wb2[...] = wst2[ws].astype(BF16)
            nb = nxt_ref[i]

            @pl.when(nb >= 0)
            def _():
                for cp in weight_copies(be_ref[jnp.maximum(nb, 0)], 1 - ws):
                    cp.start()
            wslot[0] = 1 - ws

        wait_gather(xslot)
        gather(jnp.minimum(i + 2, last), (i + 2) % GATHER_BUFS)
        u = xbuf[xslot]
        xa, xb = _unpack_bf16_pair(u[:, :half])
        wv = pltpu.bitcast(u[:, half:], F32)
        roww = wv[:, 0:1] + wv[:, 1:2] + wv[:, 2:3]
        a = _dot(xa, wb1[:half, :]) + _dot(xb, wb1[half:, :])
        b = _dot(xa, wb3[:half, :]) + _dot(xb, wb3[half:, :])
        hmid = (a * jax.nn.sigmoid(a) * b).astype(BF16)
        y = _dot(hmid, wb2[...]) * roww
        ybuf[slot] = _pack_bf16_pair(y[:, :half], y[:, half:])
        scatter(i, slot)

    @pl.when(i == last)
    def _():
        @pl.when((last >= 1) & (nv_ref[jnp.maximum(last - 1, 0)] > 0))
        def _():
            wait_gather((last + 1) % GATHER_BUFS)

        @pl.when(nv > 0)
        def _():
            wait_gather((last + 2) % GATHER_BUFS)

        @pl.when(last >= 1)
        def _():
            wait_scatter(1 - slot, nv_ref[jnp.maximum(last - 1, 0)])
        wait_scatter(slot, nv)


def _moe(xs, plan, w1, w3, w2):
    n_blk = plan[0].shape[0]
    rows = MOE_ROWS
    half = D_MODEL // 2
    hbm = pl.BlockSpec(memory_space=pl.ANY)
    grid_spec = pltpu.PrefetchScalarGridSpec(
        num_scalar_prefetch=5,
        grid=(n_blk,),
        in_specs=[hbm] * 4,
        out_specs=hbm,
        scratch_shapes=[pltpu.VMEM((GATHER_BUFS, rows, XS_WIDTH), jnp.uint32),
                        pltpu.VMEM((2, rows, half), jnp.uint32),
                        pltpu.VMEM((D_MODEL, EXPERT_HIDDEN), BF16),
                        pltpu.VMEM((D_MODEL, EXPERT_HIDDEN), BF16),
                        pltpu.VMEM((EXPERT_HIDDEN, D_MODEL), BF16),
                        pltpu.VMEM((2, D_MODEL, EXPERT_HIDDEN), F32),
                        pltpu.VMEM((2, D_MODEL, EXPERT_HIDDEN), F32),
                        pltpu.VMEM((2, EXPERT_HIDDEN, D_MODEL), F32),
                        pltpu.SMEM((1,), jnp.int32),
                        pltpu.SemaphoreType.DMA((GATHER_BUFS,)),
                        pltpu.SemaphoreType.DMA((2,)),
                        pltpu.SemaphoreType.DMA((2,))])
    return pl.pallas_call(
        _moe_kernel,
        grid_spec=grid_spec,
        out_shape=jax.ShapeDtypeStruct((xs.shape[0], half), jnp.uint32),
        compiler_params=pltpu.CompilerParams(dimension_semantics=("arbitrary",),
                                             vmem_limit_bytes=VMEM_LIMIT),
        name="moe_ffn",
    )(*plan, w1, w3, w2, xs)


def _combine_kernel(x1_ref, ys_ref, pos_ref, g2_ref, o_ref):
    tt = x1_ref.shape[0]
    half = D_MODEL // 2
    pos = pos_ref[...]
    pcol = lax.broadcasted_iota(jnp.int32, (tt, ys_ref.shape[0]), 1).astype(F32)
    sel = (jnp.where(pcol == pos[:, 0:1], 1.0, 0.0) + jnp.where(pcol == pos[:, 1:2], 1.0, 0.0))
    sel = sel.astype(BF16)
    lo, hi = _unpack_bf16_pair(ys_ref[...])
    g2 = g2_ref[0]
    x1 = x1_ref[...]
    o_ref[:, :half] = x1[:, :half] + g2[:, :half] * _dot(sel, lo)
    o_ref[:, half:] = x1[:, half:] + g2[:, half:] * _dot(sel, hi)


def _combine(x1, ys, pos, g2, S):
    T = x1.shape[0]
    tt = MOE_TILE
    per_b = S // tt
    return pl.pallas_call(
        _combine_kernel,
        grid=(T // tt,),
        in_specs=[pl.BlockSpec((tt, D_MODEL), lambda i: (i, 0)),
                  pl.BlockSpec((TILE_ROWS, D_MODEL // 2), lambda i: (i, 0)),
                  pl.BlockSpec((tt, LANES), lambda i: (i, 0)),
                  pl.BlockSpec((1, 1, D_MODEL), lambda i: (i // per_b, 0, 0))],
        out_specs=pl.BlockSpec((tt, D_MODEL), lambda i: (i, 0)),
        out_shape=jax.ShapeDtypeStruct((T, D_MODEL), F32),
        compiler_params=pltpu.CompilerParams(vmem_limit_bytes=VMEM_LIMIT),
        name="moe_combine",
    )(x1, ys, pos, g2)


def _prep_w_in(w_in):
    dq = w_in[:, OFF_DIL_Q:OFF_DIL_K]
    dk = w_in[:, OFF_DIL_K:OFF_DIL_V]
    dv = w_in[:, OFF_DIL_V:OFF_GATE_A]
    dil = []
    for g in range(N_DIL_GROUPS):
        p0 = slice(g * DIL_OUT_WIDTH, g * DIL_OUT_WIDTH + LANES)
        p1 = slice(g * DIL_OUT_WIDTH + LANES, (g + 1) * DIL_OUT_WIDTH)
        dil += [dq[:, p0], dk[:, p0], dq[:, p1], dk[:, p1], dv[:, p0], dv[:, p1]]
    pad = jnp.zeros((D_MODEL, UNIT - FOX_HEADS), w_in.dtype)
    groups = [w_in[:, OFF_GATE_A:N_IN],
              w_in[:, OFF_FOX_Q:OFF_FOX_F],
              jnp.concatenate(dil, axis=1),
              jnp.concatenate([w_in[:, OFF_FOX_F:OFF_DIL_Q], pad], axis=1)]
    return [w.astype(BF16) for w in groups]


def _prep_gain(q_gain, k_gain):
    qs = HEAD_DIM ** -0.5 * LOG2E
    ones = jnp.ones((UNIT,), F32)
    fq = q_gain[:FOX_HEADS].reshape(-1) * qs
    fk = k_gain[:FOX_HEADS].reshape(-1)
    dq = q_gain[FOX_HEADS:].reshape(-1) * qs
    dk = k_gain[FOX_HEADS:].reshape(-1)
    dil = []
    for g in range(N_DIL_GROUPS):
        p0 = slice(g * DIL_OUT_WIDTH, g * DIL_OUT_WIDTH + LANES)
        p1 = slice(g * DIL_OUT_WIDTH + LANES, (g + 1) * DIL_OUT_WIDTH)
        dil += [dq[p0], dk[p0], dq[p1], dk[p1], ones]
    parts = [ones] * 8 + [fq, fk, ones, ones] + dil + [ones]
    return jnp.concatenate(parts).reshape(1, N_UNITS * UNIT)


def _layer(x, mod, rel_bias_table, norm1_g, w_in, b_forget, q_gain, k_gain, w_branch_a, w_branch_b,
           w_out, norm2_g, w_rg, b_rg, w_re, b_re, w1, w3, w2):
    B, S, D = x.shape
    T = B * S
    sh1, sc1, g1, sh2, sc2, g2 = [m.reshape(B, 1, D) for m in jnp.split(mod, 6, axis=-1)]
    x2d = x.reshape(T, D)

    p2d, fgt, slabs = _inproj(x2d, norm1_g.reshape(1, D), sc1, sh1, _prep_w_in(w_in),
                              _prep_gain(q_gain, k_gain), S)
    p3 = p2d.reshape(B, S, P_WIDTH)
    ck = _fcum(fgt.reshape(B, S, LANES), b_forget)
    ya = _fox(p3, ck)
    yd = _dil(slabs, _relbias(rel_bias_table))

    n_router = N_GROUPS + N_EXPERTS
    wr = jnp.concatenate([w_rg, w_re, jnp.zeros((D, LANES - n_router), F32)], axis=1).astype(BF16)
    br = jnp.concatenate([b_rg, b_re, jnp.zeros((LANES - n_router,), F32)]).reshape(1, LANES)
    x1, h2, logits = _outproj(x2d, ya.reshape(T, FOX_WIDTH), yd.reshape(T, DIL_OUT_WIDTH), p2d,
                              g1, sc2, sh2, norm2_g.reshape(1, D),
                              w_branch_a.astype(BF16), w_branch_b.astype(BF16), w_out.astype(BF16),
                              wr, br, S)
    xs, cnt, pos = _dispatch(logits, h2)
    cnt2 = cnt[:, :, 0]
    n_blk = cnt.shape[0] * TILE_ROWS // MOE_ROWS + N_EXPERTS
    plan = _plan(cnt2, n_blk)
    ys = _moe(xs, plan, w1, w3, w2)
    out = _combine(x1, ys, pos, g2, S)
    return out.reshape(B, S, D)


def kernel(x, c, rel_bias_table, w_ada, b_ada, norm1_g, w_in, b_forget, q_gain, k_gain, w_branch_a, w_branch_b, w_out, norm2_g, w_router_group, b_router_group, w_router_expert, b_router_expert, w1, w3, w2):
    depth = w_ada.shape[0]
    for l in range(depth):
        mod = _ada(c, w_ada[l], b_ada[l])
        x = _layer(x, mod, rel_bias_table, norm1_g[l], w_in[l], b_forget[l], q_gain[l], k_gain[l],
                   w_branch_a[l], w_branch_b[l], w_out[l], norm2_g[l], w_router_group[l],
                   b_router_group[l], w_router_expert[l], b_router_expert[l], w1[l], w3[l], w2[l])
    return x
```

```python
import functools
import math

import numpy as np
import jax
import jax.numpy as jnp
from jax import lax
from jax.experimental import pallas as pl
from jax.experimental.pallas import tpu as pltpu

F32 = jnp.float32
BF16 = jnp.bfloat16

D_MODEL = 1024
HEAD_DIM = 64
FOX_HEADS = 8
DIL_GROUPS = ((128, 1), (512, 4), (2048, 16))
DIL_HEADS_PER_GROUP = 4
N_DIL_GROUPS = len(DIL_GROUPS)
DIL_HEADS = N_DIL_GROUPS * DIL_HEADS_PER_GROUP
FOX_WIDTH = FOX_HEADS * HEAD_DIM
DIL_WIDTH = DIL_HEADS * HEAD_DIM
DIL_OUT_WIDTH = DIL_HEADS_PER_GROUP * HEAD_DIM
NUM_BUCKETS = 32
REL_MAX_DISTANCE = 2048
N_GROUPS = 4
EXPERTS_PER_GROUP = 8
N_EXPERTS = N_GROUPS * EXPERTS_PER_GROUP
EXPERT_HIDDEN = D_MODEL // 2
EPS = 1e-6
LOG2E = math.log2(math.e)

OFF_FOX_Q = 0
OFF_FOX_K = OFF_FOX_Q + FOX_WIDTH
OFF_FOX_V = OFF_FOX_K + FOX_WIDTH
OFF_FOX_F = OFF_FOX_V + FOX_WIDTH
OFF_DIL_Q = OFF_FOX_F + FOX_HEADS
OFF_DIL_K = OFF_DIL_Q + DIL_WIDTH
OFF_DIL_V = OFF_DIL_K + DIL_WIDTH
OFF_GATE_A = OFF_DIL_V + DIL_WIDTH
OFF_GATE_B = OFF_GATE_A + D_MODEL
N_IN = OFF_GATE_B + D_MODEL

LANES = 128
UNIT = 256
DIL_L = 128

U_GATE_A, U_GATE_B, U_FOX_Q, U_FOX_K, U_FOX_V, U_DIL, U_FORGET = 0, 4, 8, 10, 12, 14, 23
N_UNITS = 24
P_WIDTH = U_DIL * UNIT
N_SLABS = 3 * N_DIL_GROUPS
_KIND = (["gate"] * 8 + ["norm"] * 4 + ["plain"] * 2 + ["norm", "norm", "plain"] * 3 + ["forget"])

TM_INPROJ = 1024
TM_PROJ = 1024
TQ_FOX = 512
MOE_ROWS = 256
MOE_TILE = 512
XS_WIDTH = D_MODEL // 2 + LANES
SUBLANES = 8
TILE_ROWS = 2 * MOE_TILE + N_EXPERTS * SUBLANES
MOE_GROUPS = MOE_ROWS // SUBLANES
GATHER_BUFS = 3
VMEM_LIMIT = 56 * 1024 * 1024


def _dot(a, b):
    return jnp.dot(a, b, preferred_element_type=F32)


def _dot_nt(a, b):
    return lax.dot_general(a, b, (((1,), (1,)), ((), ())), preferred_element_type=F32)


def _split3(x):
    hi = x.astype(BF16)
    r1 = x - hi.astype(F32)
    mid = r1.astype(BF16)
    lo = (r1 - mid.astype(F32)).astype(BF16)
    return hi, mid, lo


def _ada_kernel(c_ref, w_ref, b_ref, o_ref):
    c = c_ref[...]
    s = c * jax.nn.sigmoid(c)
    s_hi = s.astype(BF16)
    s_lo = (s - s_hi.astype(F32)).astype(BF16)
    w = w_ref[...]
    w_hi = w.astype(BF16)
    w_lo = (w - w_hi.astype(F32)).astype(BF16)
    acc = _dot(s_hi, w_hi) + _dot(s_hi, w_lo) + _dot(s_lo, w_hi)
    o_ref[...] = acc + b_ref[...]


def _ada(c, w_ada, b_ada):
    B = c.shape[0]
    n_out = w_ada.shape[1]
    tn = 512
    return pl.pallas_call(
        _ada_kernel,
        grid=(n_out // tn,),
        in_specs=[pl.BlockSpec((B, D_MODEL), lambda j: (0, 0)),
                  pl.BlockSpec((D_MODEL, tn), lambda j: (0, j)),
                  pl.BlockSpec((1, tn), lambda j: (0, j))],
        out_specs=pl.BlockSpec((B, tn), lambda j: (0, j)),
        out_shape=jax.ShapeDtypeStruct((B, n_out), F32),
        name="ada_mod",
    )(c, w_ada, b_ada.reshape(1, n_out))


def _pack_bf16_pair(lo, hi):
    lo_bits = pltpu.bitcast(lo.astype(BF16).astype(F32), jnp.uint32) >> 16
    hi_bits = pltpu.bitcast(hi.astype(BF16).astype(F32), jnp.uint32) & jnp.uint32(0xFFFF0000)
    return lo_bits | hi_bits


def _unpack_bf16_pair(u):
    lo = pltpu.bitcast(u << 16, F32).astype(BF16)
    hi = pltpu.bitcast(u & jnp.uint32(0xFFFF0000), F32).astype(BF16)
    return lo, hi


def _inproj_kernel(x_ref, g_ref, sc_ref, sh_ref, wg_ref, wf_ref, wq_ref, wk_ref, wv_ref, wz_ref,
                   gain_ref, bd_ref, p_ref, f_ref, s_ref):
    x = x_ref[...]
    ms = jnp.mean(x * x, axis=-1, keepdims=True)
    h = x * lax.rsqrt(ms + EPS) * g_ref[...]
    h = h * (1.0 + sc_ref[0]) + sh_ref[0]
    hb = h.astype(BF16)

    def weights(u):
        if u < U_FOX_Q:
            return wg_ref, (u - U_GATE_A) * UNIT
        if u < U_DIL:
            return wf_ref, (u - U_FOX_Q) * UNIT
        if u < U_FORGET:
            g, j = divmod(u - U_DIL, 3)
            return (wq_ref, wk_ref, wv_ref)[j], g * UNIT
        return wz_ref, 0

    def unit(u):
        cols = slice(u * UNIT, (u + 1) * UNIT)
        w_ref, c0 = weights(u)
        acc = _dot(hb, w_ref[:, c0:c0 + UNIT])
        kind = _KIND[u]
        if kind == "gate":
            return jax.nn.sigmoid(acc)
        if kind == "norm":
            ss = _dot((acc * acc).astype(BF16), bd_ref[...])
            return acc * lax.rsqrt(ss * (1.0 / HEAD_DIM) + EPS) * gain_ref[:, cols]
        return acc

    dil_q = {}

    def emit(u):
        o = unit(u)
        if u < U_DIL:
            p_ref[:, u * UNIT:(u + 1) * UNIT] = o.astype(BF16)
        elif u < U_FORGET:
            g, j = divmod(u - U_DIL, 3)
            if j == 0:
                dil_q[g] = o
            elif j == 1:
                q = dil_q.pop(g)
                s_ref[0, 3 * g] = _pack_bf16_pair(q[:, :LANES], o[:, :LANES])
                s_ref[0, 3 * g + 1] = _pack_bf16_pair(q[:, LANES:], o[:, LANES:])
            else:
                s_ref[0, 3 * g + 2] = _pack_bf16_pair(o[:, :LANES], o[:, LANES:])
        else:
            f_ref[...] = o[:, :LANES]

    normed = [u for u in range(N_UNITS) if _KIND[u] == "norm"]
    others = [u for u in range(N_UNITS) if _KIND[u] != "norm"]
    while normed or others:
        for group in (normed, others):
            if group:
                emit(group.pop(0))


def _inproj(x2d, norm_g, sc, sh, w_re, gain_row, S):
    T = x2d.shape[0]
    tm = TM_INPROJ
    per_b = S // tm
    bd = np.kron(np.eye(UNIT // HEAD_DIM), np.ones((HEAD_DIM, HEAD_DIM))).astype(np.float32)
    once = dict(pipeline_mode=pl.Buffered(1))
    return pl.pallas_call(
        _inproj_kernel,
        grid=(T // tm,),
        in_specs=[pl.BlockSpec((tm, D_MODEL), lambda i: (i, 0)),
                  pl.BlockSpec((1, D_MODEL), lambda i: (0, 0)),
                  pl.BlockSpec((1, 1, D_MODEL), lambda i: (i // per_b, 0, 0)),
                  pl.BlockSpec((1, 1, D_MODEL), lambda i: (i // per_b, 0, 0)),
                  *[pl.BlockSpec(w.shape, lambda i: (0, 0), **once) for w in w_re],
                  pl.BlockSpec((1, N_UNITS * UNIT), lambda i: (0, 0), **once),
                  pl.BlockSpec((UNIT, UNIT), lambda i: (0, 0), **once)],
        out_specs=[pl.BlockSpec((tm, P_WIDTH), lambda i: (i, 0)),
                   pl.BlockSpec((tm, LANES), lambda i: (i, 0)),
                   pl.BlockSpec((1, N_SLABS, tm, LANES), lambda i: (i // per_b, 0, i % per_b, 0))],
        out_shape=[jax.ShapeDtypeStruct((T, P_WIDTH), BF16),
                   jax.ShapeDtypeStruct((T, LANES), F32),
                   jax.ShapeDtypeStruct((T // S, N_SLABS, S, LANES), jnp.uint32)],
        compiler_params=pltpu.CompilerParams(vmem_limit_bytes=VMEM_LIMIT),
        name="in_proj",
    )(x2d, norm_g, sc, sh, *w_re, gain_row, jnp.asarray(bd, BF16))


def _fcum_kernel(f_ref, b_ref, tri_ref, o_ref):
    S = f_ref.shape[1]
    xf = f_ref[0] + b_ref[...]
    ls = (jnp.minimum(xf, 0.0) - jnp.log(1.0 + jnp.exp(-jnp.abs(xf)))) * LOG2E
    lst = ls.T
    carry = jnp.zeros((LANES, UNIT), F32)
    for blk in range(S // UNIT):
        seg = lst[:, blk * UNIT:(blk + 1) * UNIT]
        hi, mid, lo = _split3(seg)
        tri = tri_ref[...]
        res = _dot(hi, tri) + _dot(mid, tri) + _dot(lo, tri)
        o_ref[0, :, blk * UNIT:(blk + 1) * UNIT] = (res[:, :UNIT] + carry)[:FOX_HEADS]
        carry = carry + res[:, UNIT:]


def _fcum(fgt, b_forget):
    B, S, _ = fgt.shape
    brow = jnp.zeros((1, LANES), F32).at[0, :FOX_HEADS].set(b_forget)
    tri = np.concatenate([np.triu(np.ones((UNIT, UNIT))), np.ones((UNIT, UNIT))], axis=1)
    return pl.pallas_call(
        _fcum_kernel,
        grid=(B,),
        in_specs=[pl.BlockSpec((1, S, LANES), lambda b: (b, 0, 0)),
                  pl.BlockSpec((1, LANES), lambda b: (0, 0)),
                  pl.BlockSpec((UNIT, 2 * UNIT), lambda b: (0, 0))],
        out_specs=pl.BlockSpec((1, FOX_HEADS, S), lambda b: (b, 0, 0)),
        out_shape=jax.ShapeDtypeStruct((B, FOX_HEADS, S), F32),
        name="forget_cumsum",
    )(fgt, brow, jnp.asarray(tri, BF16))


def _fox_kernel(q_ref, k_ref, v_ref, ck_ref, o_ref):
    S = q_ref.shape[1]
    pair = pl.program_id(1)
    tq = TQ_FOX
    lane = lax.broadcasted_iota(jnp.int32, (1, LANES), 1)
    row = lax.broadcasted_iota(jnp.int32, (tq, tq), 0)
    col = lax.broadcasted_iota(jnp.int32, (tq, tq), 1)
    causal = col <= row
    cks = [ck_ref[0, pl.ds(2 * pair + hh, 1), :] for hh in range(2)]
    for t in reversed(range(S // tq)):
        r0, r1 = t * tq, (t + 1) * tq
        qt = q_ref[0, r0:r1, :]
        outs = []
        for hh in range(2):
            hsel = (lane >= HEAD_DIM) == bool(hh)
            qm = jnp.where(hsel, qt, jnp.zeros_like(qt))
            ck = cks[hh]
            s = _dot_nt(qm, k_ref[0, :r1, :]) - ck[:, :r1]
            s_d = jnp.where(causal, s[:, r0:], -jnp.inf)
            s = jnp.concatenate([s[:, :r0], s_d], axis=1) if t > 0 else s_d
            m = jnp.max(s, axis=-1, keepdims=True)
            p = jnp.exp2(s - m)
            l = jnp.sum(p, axis=-1, keepdims=True)
            outs.append(_dot(p.astype(BF16), v_ref[0, :r1, :]) / l)
        o_ref[0, r0:r1, :] = jnp.where(lane < HEAD_DIM, outs[0], outs[1]).astype(BF16)


def _fox(p3, ck):
    B, S, _ = p3.shape
    nq, nk, nv = (U_FOX_Q * UNIT // LANES, U_FOX_K * UNIT // LANES, U_FOX_V * UNIT // LANES)
    return pl.pallas_call(
        _fox_kernel,
        grid=(B, FOX_HEADS // 2),
        in_specs=[pl.BlockSpec((1, S, LANES), lambda b, p: (b, 0, nq + p)),
                  pl.BlockSpec((1, S, LANES), lambda b, p: (b, 0, nk + p)),
                  pl.BlockSpec((1, S, LANES), lambda b, p: (b, 0, nv + p)),
                  pl.BlockSpec((1, FOX_HEADS, S), lambda b, p: (b, 0, 0))],
        out_specs=pl.BlockSpec((1, S, LANES), lambda b, p: (b, 0, p)),
        out_shape=jax.ShapeDtypeStruct((B, S, FOX_WIDTH), BF16),
        compiler_params=pltpu.CompilerParams(vmem_limit_bytes=VMEM_LIMIT),
        name="fox_attn",
    )(p3, p3, p3, ck)


def _t5_bucket(dist):
    max_exact = NUM_BUCKETS // 2
    d = np.maximum(dist, 1).astype(np.float32)
    large = max_exact + (np.log(d / max_exact) / np.log(REL_MAX_DISTANCE / max_exact)
                         * (NUM_BUCKETS - max_exact)).astype(np.int32)
    large = np.minimum(large, NUM_BUCKETS - 1)
    return np.where(dist < max_exact, dist, large).astype(np.int32)


def _relbias_kernel(tab_ref, bucket_ref, valid_ref, o_ref):
    g = pl.program_id(0)
    bk = bucket_ref[0]
    vd = valid_ref[0]
    for hs in range(DIL_HEADS_PER_GROUP):
        acc = jnp.zeros(bk.shape, F32)
        for b in range(NUM_BUCKETS):
            acc = jnp.where(bk == b, tab_ref[b, g * DIL_HEADS_PER_GROUP + hs], acc)
        bias = jnp.where(vd != 0, acc * LOG2E, -jnp.inf)
        o_ref[0, hs] = bias
        col = lax.broadcasted_iota(jnp.int32, bias.shape, 1)
        o_ref[1, hs] = jnp.where(col >= DIL_L, bias, -jnp.inf)


def _relbias(table):
    L = DIL_L
    i = np.arange(L)[:, None]
    j = np.arange(2 * L)[None, :]
    m = L + i - j
    valid = ((m >= 0) & (m <= L)).astype(np.int32)
    buckets = np.stack([_t5_bucket(np.clip(m, 0, None) * d) for _, d in DIL_GROUPS])
    valids = np.stack([valid] * N_DIL_GROUPS)
    return pl.pallas_call(
        _relbias_kernel,
        grid=(N_DIL_GROUPS,),
        in_specs=[pl.BlockSpec(memory_space=pltpu.SMEM),
                  pl.BlockSpec((1, L, 2 * L), lambda g: (g, 0, 0)),
                  pl.BlockSpec((1, L, 2 * L), lambda g: (g, 0, 0))],
        out_specs=pl.BlockSpec((2, DIL_HEADS_PER_GROUP, L, 2 * L), lambda g: (0, g, 0, 0)),
        out_shape=jax.ShapeDtypeStruct((2, DIL_HEADS, L, 2 * L), F32),
        name="rel_bias",
    )(table, jnp.asarray(buckets), jnp.asarray(valids))


def _dil_rows(start, d):
    return pl.ds(start, DIL_L) if d == 1 else pl.ds(start, DIL_L, stride=d)


def _dil_block_rows(d, nb, it):
    r, n = it // nb, it % nb
    cur = _dil_rows(r + d * (n * DIL_L), d)
    prev = _dil_rows(r + d * (jnp.maximum(n - 1, 0) * DIL_L), d)
    return cur, prev, 1 - jnp.minimum(n, 1)


def _dil_scores(qkv_ref, bias_ref, s_scr, slot, g, d, nb, it):
    lane = lax.broadcasted_iota(jnp.int32, (1, LANES), 1)
    cur, prev, first = _dil_block_rows(d, nb, it)
    for pr in range(2):
        qt, kt = _unpack_bf16_pair(qkv_ref[0, 3 * g + pr, cur, :])
        if nb > 1:
            _, k_prev = _unpack_bf16_pair(qkv_ref[0, 3 * g + pr, prev, :])
            kt = jnp.concatenate([k_prev, kt], axis=0)
        for hh in range(2):
            hsel = (lane >= HEAD_DIM) == bool(hh)
            qm = jnp.where(hsel, qt, jnp.zeros_like(qt))
            head = 2 * pr + hh
            if nb > 1:
                s_scr[slot, head] = _dot_nt(qm, kt) + bias_ref[first, DIL_HEADS_PER_GROUP * g + head]
            else:
                bias = bias_ref[0, DIL_HEADS_PER_GROUP * g + head, :, DIL_L:]
                s_scr[slot, head, :, :DIL_L] = _dot_nt(qm, kt) + bias


def _dil_merge(qkv_ref, s_scr, slot, m_scr, l_scr, acc_scr, g, d, nb, init, it):
    lane = lax.broadcasted_iota(jnp.int32, (1, LANES), 1)
    cur, prev, _ = _dil_block_rows(d, nb, it)
    v_cur = _unpack_bf16_pair(qkv_ref[0, 3 * g + 2, cur, :])
    if nb > 1:
        v_prev = _unpack_bf16_pair(qkv_ref[0, 3 * g + 2, prev, :])
    for pr in range(2):
        vt = jnp.concatenate([v_prev[pr], v_cur[pr]], axis=0) if nb > 1 else v_cur[pr]
        ms, ls, accs = [], [], []
        for hh in range(2):
            s = s_scr[slot, 2 * pr + hh] if nb > 1 else s_scr[slot, 2 * pr + hh, :, :DIL_L]
            m = jnp.max(s, axis=-1, keepdims=True)
            p = jnp.exp2(s - m)
            ms.append(m)
            ls.append(jnp.sum(p, axis=-1, keepdims=True))
            accs.append(_dot(p.astype(BF16), vt))
        low = lane < HEAD_DIM
        m_b = jnp.where(low, ms[0], ms[1])
        l_b = jnp.where(low, ls[0], ls[1])
        acc_b = jnp.where(low, accs[0], accs[1])
        if init:
            m_scr[pr, cur, :] = m_b
            l_scr[pr, cur, :] = l_b
            acc_scr[pr, cur, :] = acc_b
        else:
            m_o = m_scr[pr, cur, :]
            m_n = jnp.maximum(m_o, m_b)
            a_o = jnp.exp2(m_o - m_n)
            a_b = jnp.exp2(m_b - m_n)
            m_scr[pr, cur, :] = m_n
            l_scr[pr, cur, :] = l_scr[pr, cur, :] * a_o + l_b * a_b
            acc_scr[pr, cur, :] = acc_scr[pr, cur, :] * a_o + acc_b * a_b


def _dil_kernel(qkv_ref, bias_ref, o_ref, m_scr, l_scr, acc_scr, s_scr):
    S = o_ref.shape[1]
    order = sorted(range(N_DIL_GROUPS), key=lambda g: -DIL_GROUPS[g][1])
    for g in order:
        window, d = DIL_GROUPS[g]
        nb = S // window
        total = d * nb
        assert total % 2 == 0
        scores = functools.partial(_dil_scores, qkv_ref, bias_ref, s_scr, g=g, d=d, nb=nb)
        merge = functools.partial(_dil_merge, qkv_ref, s_scr, m_scr=m_scr, l_scr=l_scr,
                                  acc_scr=acc_scr, g=g, d=d, nb=nb, init=g == order[0])

        scores(slot=0, it=0)

        def body(j, carry, scores=scores, merge=merge, total=total):
            scores(slot=1, it=2 * j + 1)
            merge(slot=0, it=2 * j)
            scores(slot=0, it=jnp.minimum(2 * j + 2, total - 1))
            merge(slot=1, it=2 * j + 1)
            return carry
        lax.fori_loop(0, total // 2, body, 0, unroll=2)
    for pr in range(2):
        o_ref[0, :, pr * LANES:(pr + 1) * LANES] = (acc_scr[pr] / l_scr[pr]).astype(BF16)


def _dil(slabs, bias):
    B, _, S, _ = slabs.shape
    for window, d in DIL_GROUPS:
        assert window // d == DIL_L and S % window == 0
    stat = pltpu.VMEM((2, S, LANES), F32)
    return pl.pallas_call(
        _dil_kernel,
        grid=(B,),
        in_specs=[pl.BlockSpec((1, N_SLABS, S, LANES), lambda b: (b, 0, 0, 0)),
                  pl.BlockSpec(bias.shape, lambda b: (0, 0, 0, 0))],
        out_specs=pl.BlockSpec((1, S, DIL_OUT_WIDTH), lambda b: (b, 0, 0)),
        out_shape=jax.ShapeDtypeStruct((B, S, DIL_OUT_WIDTH), BF16),
        scratch_shapes=[stat, stat, stat,
                        pltpu.VMEM((2, DIL_HEADS_PER_GROUP, DIL_L, 2 * DIL_L), F32)],
        compiler_params=pltpu.CompilerParams(vmem_limit_bytes=VMEM_LIMIT),
        name="dil_attn",
    )(slabs, bias)


def _outproj_kernel(x_ref, ya_ref, yd_ref, ga_ref, gb_ref,
                    g1_ref, sc_ref, sh_ref, ng_ref, wa_ref, wb_ref, wo_ref, wr_ref, br_ref,
                    x1_ref, h2_ref, lg_ref):
    n_chunks = 2
    cm = x_ref.shape[0] // n_chunks
    for c in range(n_chunks):
        rows = slice(c * cm, (c + 1) * cm)
        a = _dot(ya_ref[rows, :], wa_ref[...])
        bm = _dot(yd_ref[rows, :], wb_ref[...])
        merged = ga_ref[rows, :].astype(F32) * a + gb_ref[rows, :].astype(F32) * bm
        out = _dot(merged.astype(BF16), wo_ref[...])
        x1 = x_ref[rows, :] + g1_ref[0] * out
        x1_ref[rows, :] = x1
        ms = jnp.mean(x1 * x1, axis=-1, keepdims=True)
        h = x1 * lax.rsqrt(ms + EPS) * ng_ref[...]
        h = h * (1.0 + sc_ref[0]) + sh_ref[0]
        hb = h.astype(BF16)
        h2_ref[rows, :] = hb
        lg_ref[rows, :] = _dot(hb, wr_ref[...]) + br_ref[...]


def _outproj(x2d, ya2d, yd2d, p2d, g1, sc2, sh2, norm_g, wa, wb, wo, wr, br, S):
    T = x2d.shape[0]
    tm = TM_PROJ
    per_b = S // tm
    row = lambda w: pl.BlockSpec((tm, w), lambda i: (i, 0))
    full = lambda a: pl.BlockSpec(a.shape, lambda i: (0,) * a.ndim)
    mod = pl.BlockSpec((1, 1, D_MODEL), lambda i: (i // per_b, 0, 0))
    return pl.pallas_call(
        _outproj_kernel,
        grid=(T // tm,),
        in_specs=[row(D_MODEL), row(FOX_WIDTH), row(DIL_OUT_WIDTH)]
                 + [pl.BlockSpec((tm, D_MODEL), lambda i: (i, U_GATE_A * UNIT // D_MODEL)),
                    pl.BlockSpec((tm, D_MODEL), lambda i: (i, U_GATE_B * UNIT // D_MODEL)),
                    mod, mod, mod, full(norm_g), full(wa), full(wb), full(wo), full(wr), full(br)],
        out_specs=[row(D_MODEL), row(D_MODEL), row(LANES)],
        out_shape=[jax.ShapeDtypeStruct((T, D_MODEL), F32),
                   jax.ShapeDtypeStruct((T, D_MODEL), BF16),
                   jax.ShapeDtypeStruct((T, LANES), F32)],
        compiler_params=pltpu.CompilerParams(vmem_limit_bytes=VMEM_LIMIT),
        name="out_proj",
    )(x2d, ya2d, yd2d, p2d, p2d, g1, sc2, sh2, norm_g, wa, wb, wo, wr, br)


def _dispatch_kernel(lg_ref, h_ref, tri_ref, xs_ref, cnt_ref, pos_ref):
    tt = lg_ref.shape[0]
    lt = lg_ref[...].T
    row = lambda i: lt[i:i + 1, :]
    neg = -jnp.inf
    g = [row(i) for i in range(N_GROUPS)]
    gmax = functools.reduce(jnp.maximum, g)
    gidx = jnp.full(gmax.shape, N_GROUPS - 1, jnp.int32)
    for i in reversed(range(N_GROUPS - 1)):
        gidx = jnp.where(g[i] == gmax, i, gidx)
    gsum = sum(jnp.exp(gi - gmax) for gi in g)
    el = []
    for j in range(EXPERTS_PER_GROUP):
        v = row(N_GROUPS + EXPERTS_PER_GROUP * (N_GROUPS - 1) + j)
        for gg in reversed(range(N_GROUPS - 1)):
            v = jnp.where(gidx == gg, row(N_GROUPS + EXPERTS_PER_GROUP * gg + j), v)
        el.append(v)

    def top(vals):
        best = functools.reduce(jnp.maximum, vals)
        idx = jnp.full(best.shape, EXPERTS_PER_GROUP - 1, jnp.int32)
        for j in reversed(range(EXPERTS_PER_GROUP - 1)):
            idx = jnp.where(vals[j] == best, j, idx)
        return best, idx

    v1, i1 = top(el)
    v2, i2 = top([jnp.where(i1 == j, neg, el[j]) for j in range(EXPERTS_PER_GROUP)])
    t = jnp.exp(v2 - v1)
    den = (1.0 + t) * gsum
    wts = [1.0 / den, t / den]
    eid = [gidx * EXPERTS_PER_GROUP + i1, gidx * EXPERTS_PER_GROUP + i2]

    esub = lax.broadcasted_iota(jnp.int32, (N_EXPERTS, tt), 0)
    ohf = jnp.concatenate([jnp.where(esub == eid[k], 1.0, 0.0) for k in range(2)], axis=1)
    n_pb = 2 * tt // UNIT
    oh_blocks = jnp.concatenate([ohf[:, b * UNIT:(b + 1) * UNIT] for b in range(n_pb)], axis=0)
    res = _dot(oh_blocks.astype(BF16), tri_ref[...])
    cnt = jnp.zeros((N_EXPERTS, LANES), F32)
    pre = []
    for b in range(n_pb):
        r = res[b * N_EXPERTS:(b + 1) * N_EXPERTS]
        pre.append(r[:, :UNIT] + jnp.concatenate([cnt] * (UNIT // LANES), axis=1))
        cnt = cnt + r[:, UNIT:]
    prefix = jnp.concatenate(pre, axis=1)
    cnt = (((cnt.astype(jnp.int32) + (SUBLANES - 1)) // SUBLANES) * SUBLANES).astype(F32)
    esub_c = lax.broadcasted_iota(jnp.int32, cnt.shape, 0)
    start = jnp.zeros_like(cnt)
    for e in range(N_EXPERTS - 1):
        start = start + jnp.where(esub_c > e, cnt[e:e + 1, :], 0.0)
    start_w = jnp.concatenate([start] * (2 * tt // LANES), axis=1)
    pos = jnp.sum(ohf * (start_w + prefix), axis=0, keepdims=True)
    pos_k = [pos[:, :tt], pos[:, tt:]]

    n_rows = xs_ref.shape[0]
    psub = lax.broadcasted_iota(jnp.int32, (n_rows, tt), 0).astype(F32)
    pm = [jnp.where(psub == pos_k[k], 1.0, 0.0).astype(BF16) for k in range(2)]
    xs = _dot(pm[0] + pm[1], h_ref[...])
    wsub = lax.broadcasted_iota(jnp.int32, (LANES, tt), 0)
    ws = jnp.zeros((n_rows, LANES), F32)
    for k in range(2):
        parts = _split3(wts[k])
        wrows = jnp.zeros((LANES, tt), F32)
        for j in range(3):
            wrows = jnp.where(wsub == j, parts[j].astype(F32), wrows)
        ws = ws + _dot_nt(pm[k], wrows.astype(BF16))
    half = D_MODEL // 2
    xs_ref[:, :half] = _pack_bf16_pair(xs[:, :half], xs[:, half:])
    xs_ref[:, half:] = pltpu.bitcast(ws, jnp.uint32)
    cnt_ref[0] = cnt.astype(jnp.int32)
    posr = jnp.where(wsub == 0, pos_k[0], jnp.where(wsub == 1, pos_k[1], 0.0))
    pos_ref[...] = posr.T


def _dispatch(logits, h2):
    T = logits.shape[0]
    tt = MOE_TILE
    n_tiles = T // tt
    tri = np.concatenate([np.triu(np.ones((UNIT, UNIT)), 1), np.ones((UNIT, LANES))], axis=1)
    return pl.pallas_call(
        _dispatch_kernel,
        grid=(n_tiles,),
        in_specs=[pl.BlockSpec((tt, LANES), lambda i: (i, 0)),
                  pl.BlockSpec((tt, D_MODEL), lambda i: (i, 0)),
                  pl.BlockSpec(tri.shape, lambda i: (0, 0))],
        out_specs=[pl.BlockSpec((TILE_ROWS, XS_WIDTH), lambda i: (i, 0)),
                   pl.BlockSpec((1, N_EXPERTS, LANES), lambda i: (i, 0, 0)),
                   pl.BlockSpec((tt, LANES), lambda i: (i, 0))],
        out_shape=[jax.ShapeDtypeStruct((n_tiles * TILE_ROWS, XS_WIDTH), jnp.uint32),
                   jax.ShapeDtypeStruct((n_tiles, N_EXPERTS, LANES), jnp.int32),
                   jax.ShapeDtypeStruct((T, LANES), F32)],
        compiler_params=pltpu.CompilerParams(vmem_limit_bytes=VMEM_LIMIT),
        name="moe_dispatch",
    )(logits, h2, jnp.asarray(tri, BF16))


def _plan_kernel(cnt_ref, be_ref, nv_ref, nxt_ref, grp_ref, used_ref, cs_ref):
    n_tiles = cnt_ref.shape[0]
    n_blk = be_ref.shape[0]
    rows = MOE_ROWS
    row_shift = rows.bit_length() - 1
    grp_shift = SUBLANES.bit_length() - 1
    assert rows == 1 << row_shift and SUBLANES == 1 << grp_shift

    def tile_starts(t, c):
        def per_e(e, acc):
            cs_ref[t * N_EXPERTS + e] = acc
            return acc + cnt_ref[t, e]
        used_ref[t] = lax.fori_loop(0, N_EXPERTS, per_e, 0, unroll=8)
        return c
    lax.fori_loop(0, n_tiles, tile_starts, 0)

    def clear(b, c):
        nv_ref[b] = 0
        return c
    lax.fori_loop(0, n_blk, clear, 0)

    def clear_groups(g, c):
        grp_ref[g] = 0
        return c
    lax.fori_loop(0, n_blk * MOE_GROUPS, clear_groups, 0, unroll=8)

    def per_expert(e, b):
        g0 = b * MOE_GROUPS

        def per_tile(t, tot):
            c = cnt_ref[t, e]
            src = t * TILE_ROWS + cs_ref[t * N_EXPERTS + e]
            first = g0 + lax.shift_right_logical(tot, grp_shift)

            def per_group(k, cc):
                grp_ref[first + k] = src + k * SUBLANES
                return cc
            lax.fori_loop(0, lax.shift_right_logical(c, grp_shift), per_group, 0)
            return tot + c
        tot = lax.fori_loop(0, n_tiles, per_tile, 0)

        def per_block(j, c):
            be_ref[b + j] = e
            nv_ref[b + j] = jnp.minimum(rows, tot - j * rows)
            return c
        nb = lax.shift_right_logical(tot + rows - 1, row_shift)
        lax.fori_loop(0, nb, per_block, 0)
        return b + nb
    n_used = lax.fori_loop(0, N_EXPERTS, per_expert, 0)

    def unused(b, c):
        be_ref[b] = be_ref[n_used - 1]
        nxt_ref[b] = -1
        return c
    lax.fori_loop(n_used, n_blk, unused, 0)

    def next_run(k, nf):
        b = n_used - 1 - k
        nf = jnp.where(be_ref[b] != be_ref[jnp.minimum(b + 1, n_used - 1)], b + 1, nf)
        nxt_ref[b] = nf
        return nf
    lax.fori_loop(0, n_used, next_run, -1)


def _plan(cnt, n_blk):
    n_tiles = cnt.shape[0]
    smem = pl.BlockSpec(memory_space=pltpu.SMEM)
    i32 = lambda n: jax.ShapeDtypeStruct((n,), jnp.int32)
    return pl.pallas_call(
        _plan_kernel,
        in_specs=[smem],
        out_specs=[smem] * 5,
        out_shape=[i32(n_blk), i32(n_blk), i32(n_blk), i32(n_blk * MOE_GROUPS), i32(n_tiles)],
        scratch_shapes=[pltpu.SMEM((n_tiles * N_EXPERTS,), jnp.int32)],
        name="moe_plan",
    )(cnt)


def _pow2_pieces(n, fn):
    for b in reversed(range(SUBLANES.bit_length() - 1, MOE_ROWS.bit_length())):
        size = 1 << b

        @pl.when((n & size) != 0)
        def _():
            fn((n >> (b + 1)) << (b + 1), size)


def _moe_kernel(be_ref, nv_ref, nxt_ref, grp_ref, used_ref,
                w1_hbm, w3_hbm, w2_hbm, xs_hbm, ys_hbm,
                xbuf, ybuf, wb1, wb3, wb2, wst1, wst3, wst2, wslot, gsem, ssem, wsem):
    i = pl.program_id(0)
    last = pl.num_programs(0) - 1
    slot = i % 2
    nv = nv_ref[i]
    half = D_MODEL // 2
    grp_shift = SUBLANES.bit_length() - 1

    def group_row(blk, g):
        return pl.multiple_of(grp_ref[blk * MOE_GROUPS + g], SUBLANES)

    def gather(blk, s):
        top = jnp.maximum(lax.shift_right_logical(nv_ref[blk], grp_shift) - 1, 0)
        for g in range(MOE_GROUPS):
            src = group_row(blk, jnp.minimum(g, top))
            pltpu.make_async_copy(xs_hbm.at[pl.ds(src, SUBLANES)],
                                  xbuf.at[s, pl.ds(g * SUBLANES, SUBLANES)], gsem.at[s]).start()

    def wait_gather(s):
        pltpu.make_async_copy(xs_hbm.at[pl.ds(0, MOE_ROWS)], xbuf.at[s], gsem.at[s]).wait()

    def scatter(blk, s):
        def start(g):
            r = g * SUBLANES if isinstance(g, int) else pl.multiple_of(g * SUBLANES, SUBLANES)
            pltpu.make_async_copy(ybuf.at[s, pl.ds(r, SUBLANES)],
                                  ys_hbm.at[pl.ds(group_row(blk, g), SUBLANES)], ssem.at[s]).start()

        @pl.when(nv_ref[blk] == MOE_ROWS)
        def _():
            for g in range(MOE_GROUPS):
                start(g)

        @pl.when(nv_ref[blk] < MOE_ROWS)
        def _():
            def body(g, c):
                start(g)
                return c
            lax.fori_loop(0, lax.shift_right_logical(nv_ref[blk], grp_shift), body, 0)

    def wait_scatter(s, count):
        @pl.when(count == MOE_ROWS)
        def _():
            pltpu.make_async_copy(ybuf.at[s], ys_hbm.at[pl.ds(0, MOE_ROWS)], ssem.at[s]).wait()

        @pl.when(count < MOE_ROWS)
        def _():
            _pow2_pieces(count, lambda a, size: pltpu.make_async_copy(
                ybuf.at[s, pl.ds(0, size)], ys_hbm.at[pl.ds(0, size)], ssem.at[s]).wait())

    @pl.when(i == 0)
    def _():
        @pl.when(nv > 0)
        def _():
            gather(0, 0)
            gather(jnp.minimum(1, last), 1)

        ybuf[1] = jnp.zeros(ybuf.shape[1:], ybuf.dtype)
        n_tiles = used_ref.shape[0]

        def fill(t, c):
            row0 = t * TILE_ROWS + used_ref[t]
            _pow2_pieces(TILE_ROWS - used_ref[t], lambda a, size: pltpu.make_async_copy(
                ybuf.at[1, pl.ds(0, size)], ys_hbm.at[pl.ds(pl.multiple_of(row0 + a, SUBLANES), size)],
                ssem.at[1]).start())
            return c
        lax.fori_loop(0, n_tiles, fill, 0)

        def drain(t, c):
            wait_scatter(1, TILE_ROWS - used_ref[t])
            return c
        lax.fori_loop(0, n_tiles, drain, 0)

    @pl.when(i >= 2)
    def _():
        wait_scatter(slot, nv_ref[jnp.maximum(i - 2, 0)])

    xslot = i % GATHER_BUFS
    issuer_used = jnp.where(i >= 2, nv_ref[jnp.maximum(i - 2, 0)], nv_ref[0]) > 0

    @pl.when((nv == 0) & (i > 0) & issuer_used)
    def _():
        wait_gather(xslot)

    @pl.when(nv > 0)
    def _():
        e = be_ref[i]
        e_prev = be_ref[jnp.maximum(i - 1, 0)]

        def weight_copies(ex, ws):
            return [pltpu.make_async_copy(src.at[ex], dst.at[ws], wsem.at[ws])
                    for src, dst in ((w1_hbm, wst1), (w3_hbm, wst3), (w2_hbm, wst2))]

        @pl.when(i == 0)
        def _():
            wslot[0] = 0
            for cp in weight_copies(e, 0):
                cp.start()

        @pl.when((i == 0) | (e != e_prev))
        def _():
            ws = wslot[0]
            for cp in weight_copies(e, ws):
                cp.wait()
            wb1[...] = wst1[ws].astype(BF16)
            wb3[...] = wst3[ws].astype(BF16)
            wb2[...] = wst2[ws].astype(BF16)
            nb = nxt_ref[i]

            @pl.when(nb >= 0)
            def _():
                for cp in weight_copies(be_ref[jnp.maximum(nb, 0)], 1 - ws):
                    cp.start()
            wslot[0] = 1 - ws

        wait_gather(xslot)
        gather(jnp.minimum(i + 2, last), (i + 2) % GATHER_BUFS)
        u = xbuf[xslot]
        xa, xb = _unpack_bf16_pair(u[:, :half])
        wv = pltpu.bitcast(u[:, half:], F32)
        roww = wv[:, 0:1] + wv[:, 1:2] + wv[:, 2:3]
        a = _dot(xa, wb1[:half, :]) + _dot(xb, wb1[half:, :])
        b = _dot(xa, wb3[:half, :]) + _dot(xb, wb3[half:, :])
        hmid = (a * jax.nn.sigmoid(a) * b).astype(BF16)
        y = _dot(hmid, wb2[...]) * roww
        ybuf[slot] = _pack_bf16_pair(y[:, :half], y[:, half:])
        scatter(i, slot)

    @pl.when(i == last)
    def _():
        @pl.when((last >= 1) & (nv_ref[jnp.maximum(last - 1, 0)] > 0))
        def _():
            wait_gather((last + 1) % GATHER_BUFS)

        @pl.when(nv > 0)
        def _():
            wait_gather((last + 2) % GATHER_BUFS)

        @pl.when(last >= 1)
        def _():
            wait_scatter(1 - slot, nv_ref[jnp.maximum(last - 1, 0)])
        wait_scatter(slot, nv)


def _moe(xs, plan, w1, w3, w2):
    n_blk = plan[0].shape[0]
    rows = MOE_ROWS
    half = D_MODEL // 2
    hbm = pl.BlockSpec(memory_space=pl.ANY)
    grid_spec = pltpu.PrefetchScalarGridSpec(
        num_scalar_prefetch=5,
        grid=(n_blk,),
        in_specs=[hbm] * 4,
        out_specs=hbm,
        scratch_shapes=[pltpu.VMEM((GATHER_BUFS, rows, XS_WIDTH), jnp.uint32),
                        pltpu.VMEM((2, rows, half), jnp.uint32),
                        pltpu.VMEM((D_MODEL, EXPERT_HIDDEN), BF16),
                        pltpu.VMEM((D_MODEL, EXPERT_HIDDEN), BF16),
                        pltpu.VMEM((EXPERT_HIDDEN, D_MODEL), BF16),
                        pltpu.VMEM((2, D_MODEL, EXPERT_HIDDEN), F32),
                        pltpu.VMEM((2, D_MODEL, EXPERT_HIDDEN), F32),
                        pltpu.VMEM((2, EXPERT_HIDDEN, D_MODEL), F32),
                        pltpu.SMEM((1,), jnp.int32),
                        pltpu.SemaphoreType.DMA((GATHER_BUFS,)),
                        pltpu.SemaphoreType.DMA((2,)),
                        pltpu.SemaphoreType.DMA((2,))])
    return pl.pallas_call(
        _moe_kernel,
        grid_spec=grid_spec,
        out_shape=jax.ShapeDtypeStruct((xs.shape[0], half), jnp.uint32),
        compiler_params=pltpu.CompilerParams(dimension_semantics=("arbitrary",),
                                             vmem_limit_bytes=VMEM_LIMIT),
        name="moe_ffn",
    )(*plan, w1, w3, w2, xs)


def _combine_kernel(x1_ref, ys_ref, pos_ref, g2_ref, o_ref):
    tt = x1_ref.shape[0]
    half = D_MODEL // 2
    pos = pos_ref[...]
    pcol = lax.broadcasted_iota(jnp.int32, (tt, ys_ref.shape[0]), 1).astype(F32)
    sel = (jnp.where(pcol == pos[:, 0:1], 1.0, 0.0) + jnp.where(pcol == pos[:, 1:2], 1.0, 0.0))
    sel = sel.astype(BF16)
    lo, hi = _unpack_bf16_pair(ys_ref[...])
    g2 = g2_ref[0]
    x1 = x1_ref[...]
    o_ref[:, :half] = x1[:, :half] + g2[:, :half] * _dot(sel, lo)
    o_ref[:, half:] = x1[:, half:] + g2[:, half:] * _dot(sel, hi)


def _combine(x1, ys, pos, g2, S):
    T = x1.shape[0]
    tt = MOE_TILE
    per_b = S // tt
    return pl.pallas_call(
        _combine_kernel,
        grid=(T // tt,),
        in_specs=[pl.BlockSpec((tt, D_MODEL), lambda i: (i, 0)),
                  pl.BlockSpec((TILE_ROWS, D_MODEL // 2), lambda i: (i, 0)),
                  pl.BlockSpec((tt, LANES), lambda i: (i, 0)),
                  pl.BlockSpec((1, 1, D_MODEL), lambda i: (i // per_b, 0, 0))],
        out_specs=pl.BlockSpec((tt, D_MODEL), lambda i: (i, 0)),
        out_shape=jax.ShapeDtypeStruct((T, D_MODEL), F32),
        compiler_params=pltpu.CompilerParams(vmem_limit_bytes=VMEM_LIMIT),
        name="moe_combine",
    )(x1, ys, pos, g2)


def _prep_w_in(w_in):
    pad = jnp.zeros((D_MODEL, UNIT - FOX_HEADS), w_in.dtype)
    groups = [w_in[:, OFF_GATE_A:N_IN],
              w_in[:, OFF_FOX_Q:OFF_FOX_F],
              w_in[:, OFF_DIL_Q:OFF_DIL_K],
              w_in[:, OFF_DIL_K:OFF_DIL_V],
              w_in[:, OFF_DIL_V:OFF_GATE_A],
              jnp.concatenate([w_in[:, OFF_FOX_F:OFF_DIL_Q], pad], axis=1)]
    return [w.astype(BF16) for w in groups]


def _prep_gain(q_gain, k_gain):
    qs = HEAD_DIM ** -0.5 * LOG2E
    ones = jnp.ones((UNIT,), F32)
    fq = q_gain[:FOX_HEADS].reshape(-1) * qs
    fk = k_gain[:FOX_HEADS].reshape(-1)
    dq = q_gain[FOX_HEADS:].reshape(-1) * qs
    dk = k_gain[FOX_HEADS:].reshape(-1)
    dil = []
    for g in range(N_DIL_GROUPS):
        cs = slice(g * DIL_OUT_WIDTH, (g + 1) * DIL_OUT_WIDTH)
        dil += [dq[cs], dk[cs], ones]
    parts = [ones] * 8 + [fq, fk, ones, ones] + dil + [ones]
    return jnp.concatenate(parts).reshape(1, N_UNITS * UNIT)


def _layer(x, mod, rel_bias_table, norm1_g, w_in, b_forget, q_gain, k_gain, w_branch_a, w_branch_b,
           w_out, norm2_g, w_rg, b_rg, w_re, b_re, w1, w3, w2):
    B, S, D = x.shape
    T = B * S
    sh1, sc1, g1, sh2, sc2, g2 = [m.reshape(B, 1, D) for m in jnp.split(mod, 6, axis=-1)]
    x2d = x.reshape(T, D)

    p2d, fgt, slabs = _inproj(x2d, norm1_g.reshape(1, D), sc1, sh1, _prep_w_in(w_in),
                              _prep_gain(q_gain, k_gain), S)
    p3 = p2d.reshape(B, S, P_WIDTH)
    ck = _fcum(fgt.reshape(B, S, LANES), b_forget)
    ya = _fox(p3, ck)
    yd = _dil(slabs, _relbias(rel_bias_table))

    n_router = N_GROUPS + N_EXPERTS
    wr = jnp.concatenate([w_rg, w_re, jnp.zeros((D, LANES - n_router), F32)], axis=1).astype(BF16)
    br = jnp.concatenate([b_rg, b_re, jnp.zeros((LANES - n_router,), F32)]).reshape(1, LANES)
    x1, h2, logits = _outproj(x2d, ya.reshape(T, FOX_WIDTH), yd.reshape(T, DIL_OUT_WIDTH), p2d,
                              g1, sc2, sh2, norm2_g.reshape(1, D),
                              w_branch_a.astype(BF16), w_branch_b.astype(BF16), w_out.astype(BF16),
                              wr, br, S)
    xs, cnt, pos = _dispatch(logits, h2)
    cnt2 = cnt[:, :, 0]
    n_blk = cnt.shape[0] * TILE_ROWS // MOE_ROWS + N_EXPERTS
    plan = _plan(cnt2, n_blk)
    ys = _moe(xs, plan, w1, w3, w2)
    out = _combine(x1, ys, pos, g2, S)
    return out.reshape(B, S, D)


def kernel(x, c, rel_bias_table, w_ada, b_ada, norm1_g, w_in, b_forget, q_gain, k_gain, w_branch_a, w_branch_b, w_out, norm2_g, w_router_group, b_router_group, w_router_expert, b_router_expert, w1, w3, w2):
    depth = w_ada.shape[0]
    for l in range(depth):
        mod = _ada(c, w_ada[l], b_ada[l])
        x = _layer(x, mod, rel_bias_table, norm1_g[l], w_in[l], b_forget[l], q_gain[l], k_gain[l],
                   w_branch_a[l], w_branch_b[l], w_out[l], norm2_g[l], w_router_group[l],
                   b_router_group[l], w_router_expert[l], b_router_expert[l], w1[l], w3[l], w2[l])
    return x
```

```python
import functools
import math

import numpy as np
import jax
import jax.numpy as jnp
from jax import lax
from jax.experimental import pallas as pl
from jax.experimental.pallas import tpu as pltpu

F32 = jnp.float32
BF16 = jnp.bfloat16

D_MODEL = 1024
HEAD_DIM = 64
FOX_HEADS = 8
DIL_GROUPS = ((128, 1), (512, 4), (2048, 16))
DIL_HEADS_PER_GROUP = 4
N_DIL_GROUPS = len(DIL_GROUPS)
DIL_HEADS = N_DIL_GROUPS * DIL_HEADS_PER_GROUP
FOX_WIDTH = FOX_HEADS * HEAD_DIM
DIL_WIDTH = DIL_HEADS * HEAD_DIM
DIL_OUT_WIDTH = DIL_HEADS_PER_GROUP * HEAD_DIM
NUM_BUCKETS = 32
REL_MAX_DISTANCE = 2048
N_GROUPS = 4
EXPERTS_PER_GROUP = 8
N_EXPERTS = N_GROUPS * EXPERTS_PER_GROUP
EXPERT_HIDDEN = D_MODEL // 2
EPS = 1e-6
LOG2E = math.log2(math.e)

OFF_FOX_Q = 0
OFF_FOX_K = OFF_FOX_Q + FOX_WIDTH
OFF_FOX_V = OFF_FOX_K + FOX_WIDTH
OFF_FOX_F = OFF_FOX_V + FOX_WIDTH
OFF_DIL_Q = OFF_FOX_F + FOX_HEADS
OFF_DIL_K = OFF_DIL_Q + DIL_WIDTH
OFF_DIL_V = OFF_DIL_K + DIL_WIDTH
OFF_GATE_A = OFF_DIL_V + DIL_WIDTH
OFF_GATE_B = OFF_GATE_A + D_MODEL
N_IN = OFF_GATE_B + D_MODEL

LANES = 128
UNIT = 256
DIL_L = 128

U_GATE_A, U_GATE_B, U_FOX_Q, U_FOX_K, U_FOX_V, U_DIL, U_FORGET = 0, 4, 8, 10, 12, 14, 23
N_UNITS = 24
P_WIDTH = U_DIL * UNIT
N_SLABS = 3 * N_DIL_GROUPS
_KIND = (["gate"] * 8 + ["norm"] * 4 + ["plain"] * 2 + ["norm", "norm", "plain"] * 3 + ["forget"])

TM_INPROJ = 1024
TM_PROJ = 1024
TQ_FOX = 512
MOE_ROWS = 256
MOE_TILE = 512
XS_WIDTH = D_MODEL // 2 + LANES
SUBLANES = 8
TILE_ROWS = 2 * MOE_TILE + N_EXPERTS * SUBLANES
MOE_GROUPS = MOE_ROWS // SUBLANES
GATHER_BUFS = 3
VMEM_LIMIT = 56 * 1024 * 1024


def _dot(a, b):
    return jnp.dot(a, b, preferred_element_type=F32)


def _dot_nt(a, b):
    return lax.dot_general(a, b, (((1,), (1,)), ((), ())), preferred_element_type=F32)


def _split3(x):
    hi = x.astype(BF16)
    r1 = x - hi.astype(F32)
    mid = r1.astype(BF16)
    lo = (r1 - mid.astype(F32)).astype(BF16)
    return hi, mid, lo


def _ada_kernel(c_ref, w_ref, b_ref, o_ref):
    c = c_ref[...]
    s = c * jax.nn.sigmoid(c)
    s_hi = s.astype(BF16)
    s_lo = (s - s_hi.astype(F32)).astype(BF16)
    w = w_ref[...]
    w_hi = w.astype(BF16)
    w_lo = (w - w_hi.astype(F32)).astype(BF16)
    acc = _dot(s_hi, w_hi) + _dot(s_hi, w_lo) + _dot(s_lo, w_hi)
    o_ref[...] = acc + b_ref[...]


def _ada(c, w_ada, b_ada):
    B = c.shape[0]
    n_out = w_ada.shape[1]
    tn = 512
    return pl.pallas_call(
        _ada_kernel,
        grid=(n_out // tn,),
        in_specs=[pl.BlockSpec((B, D_MODEL), lambda j: (0, 0)),
                  pl.BlockSpec((D_MODEL, tn), lambda j: (0, j)),
                  pl.BlockSpec((1, tn), lambda j: (0, j))],
        out_specs=pl.BlockSpec((B, tn), lambda j: (0, j)),
        out_shape=jax.ShapeDtypeStruct((B, n_out), F32),
        name="ada_mod",
    )(c, w_ada, b_ada.reshape(1, n_out))


def _pack_bf16_pair(lo, hi):
    lo_bits = pltpu.bitcast(lo.astype(BF16).astype(F32), jnp.uint32) >> 16
    hi_bits = pltpu.bitcast(hi.astype(BF16).astype(F32), jnp.uint32) & jnp.uint32(0xFFFF0000)
    return lo_bits | hi_bits


def _unpack_bf16_pair(u):
    lo = pltpu.bitcast(u << 16, F32).astype(BF16)
    hi = pltpu.bitcast(u & jnp.uint32(0xFFFF0000), F32).astype(BF16)
    return lo, hi


def _inproj_kernel(x_ref, g_ref, sc_ref, sh_ref, wg_ref, wf_ref, wq_ref, wk_ref, wv_ref, wz_ref,
                   gain_ref, bd_ref, p_ref, f_ref, s_ref):
    x = x_ref[...]
    ms = jnp.mean(x * x, axis=-1, keepdims=True)
    h = x * lax.rsqrt(ms + EPS) * g_ref[...]
    h = h * (1.0 + sc_ref[0]) + sh_ref[0]
    hb = h.astype(BF16)

    def weights(u):
        if u < U_FOX_Q:
            return wg_ref, (u - U_GATE_A) * UNIT
        if u < U_DIL:
            return wf_ref, (u - U_FOX_Q) * UNIT
        if u < U_FORGET:
            g, j = divmod(u - U_DIL, 3)
            return (wq_ref, wk_ref, wv_ref)[j], g * UNIT
        return wz_ref, 0

    def gain_col(u):
        if u < U_DIL:
            return (u - U_FOX_Q) * UNIT
        g, j = divmod(u - U_DIL, 3)
        return 2 * FOX_WIDTH + j * DIL_WIDTH + g * UNIT

    def unit(u):
        w_ref, c0 = weights(u)
        acc = _dot(hb, w_ref[:, c0:c0 + UNIT])
        kind = _KIND[u]
        if kind == "gate":
            return jax.nn.sigmoid(acc)
        if kind == "norm":
            ss = _dot((acc * acc).astype(BF16), bd_ref[...])
            gc = gain_col(u)
            return acc * lax.rsqrt(ss * (1.0 / HEAD_DIM) + EPS) * gain_ref[:, gc:gc + UNIT]
        return acc

    dil_q = {}

    def emit(u):
        o = unit(u)
        if u < U_DIL:
            p_ref[:, u * UNIT:(u + 1) * UNIT] = o.astype(BF16)
        elif u < U_FORGET:
            g, j = divmod(u - U_DIL, 3)
            if j == 0:
                dil_q[g] = o
            elif j == 1:
                q = dil_q.pop(g)
                s_ref[0, 3 * g] = _pack_bf16_pair(q[:, :LANES], o[:, :LANES])
                s_ref[0, 3 * g + 1] = _pack_bf16_pair(q[:, LANES:], o[:, LANES:])
            else:
                s_ref[0, 3 * g + 2] = _pack_bf16_pair(o[:, :LANES], o[:, LANES:])
        else:
            f_ref[...] = o[:, :LANES]

    normed = [u for u in range(N_UNITS) if _KIND[u] == "norm"]
    others = [u for u in range(N_UNITS) if _KIND[u] != "norm"]
    while normed or others:
        for group in (normed, others):
            if group:
                emit(group.pop(0))


def _inproj(x2d, norm_g, sc, sh, w_re, gain_row, S):
    T = x2d.shape[0]
    tm = TM_INPROJ
    per_b = S // tm
    bd = np.kron(np.eye(UNIT // HEAD_DIM), np.ones((HEAD_DIM, HEAD_DIM))).astype(np.float32)
    once = dict(pipeline_mode=pl.Buffered(1))
    return pl.pallas_call(
        _inproj_kernel,
        grid=(T // tm,),
        in_specs=[pl.BlockSpec((tm, D_MODEL), lambda i: (i, 0)),
                  pl.BlockSpec((1, D_MODEL), lambda i: (0, 0)),
                  pl.BlockSpec((1, 1, D_MODEL), lambda i: (i // per_b, 0, 0)),
                  pl.BlockSpec((1, 1, D_MODEL), lambda i: (i // per_b, 0, 0)),
                  *[pl.BlockSpec(w.shape, lambda i: (0, 0), **once) for w in w_re],
                  pl.BlockSpec(gain_row.shape, lambda i: (0, 0), **once),
                  pl.BlockSpec((UNIT, UNIT), lambda i: (0, 0), **once)],
        out_specs=[pl.BlockSpec((tm, P_WIDTH), lambda i: (i, 0)),
                   pl.BlockSpec((tm, LANES), lambda i: (i, 0)),
                   pl.BlockSpec((1, N_SLABS, tm, LANES), lambda i: (i // per_b, 0, i % per_b, 0))],
        out_shape=[jax.ShapeDtypeStruct((T, P_WIDTH), BF16),
                   jax.ShapeDtypeStruct((T, LANES), F32),
                   jax.ShapeDtypeStruct((T // S, N_SLABS, S, LANES), jnp.uint32)],
        compiler_params=pltpu.CompilerParams(vmem_limit_bytes=VMEM_LIMIT),
        name="in_proj",
    )(x2d, norm_g, sc, sh, *w_re, gain_row, jnp.asarray(bd, BF16))


def _fcum_kernel(f_ref, b_ref, tri_ref, o_ref):
    S = f_ref.shape[1]
    xf = f_ref[0] + b_ref[...]
    ls = (jnp.minimum(xf, 0.0) - jnp.log(1.0 + jnp.exp(-jnp.abs(xf)))) * LOG2E
    lst = ls.T
    carry = jnp.zeros((LANES, UNIT), F32)
    for blk in range(S // UNIT):
        seg = lst[:, blk * UNIT:(blk + 1) * UNIT]
        hi, mid, lo = _split3(seg)
        tri = tri_ref[...]
        res = _dot(hi, tri) + _dot(mid, tri) + _dot(lo, tri)
        o_ref[0, :, blk * UNIT:(blk + 1) * UNIT] = (res[:, :UNIT] + carry)[:FOX_HEADS]
        carry = carry + res[:, UNIT:]


def _fcum(fgt, b_forget):
    B, S, _ = fgt.shape
    brow = jnp.zeros((1, LANES), F32).at[0, :FOX_HEADS].set(b_forget)
    tri = np.concatenate([np.triu(np.ones((UNIT, UNIT))), np.ones((UNIT, UNIT))], axis=1)
    return pl.pallas_call(
        _fcum_kernel,
        grid=(B,),
        in_specs=[pl.BlockSpec((1, S, LANES), lambda b: (b, 0, 0)),
                  pl.BlockSpec((1, LANES), lambda b: (0, 0)),
                  pl.BlockSpec((UNIT, 2 * UNIT), lambda b: (0, 0))],
        out_specs=pl.BlockSpec((1, FOX_HEADS, S), lambda b: (b, 0, 0)),
        out_shape=jax.ShapeDtypeStruct((B, FOX_HEADS, S), F32),
        name="forget_cumsum",
    )(fgt, brow, jnp.asarray(tri, BF16))


def _fox_kernel(q_ref, k_ref, v_ref, ck_ref, o_ref):
    S = q_ref.shape[1]
    pair = pl.program_id(1)
    tq = TQ_FOX
    lane = lax.broadcasted_iota(jnp.int32, (1, LANES), 1)
    row = lax.broadcasted_iota(jnp.int32, (tq, tq), 0)
    col = lax.broadcasted_iota(jnp.int32, (tq, tq), 1)
    causal = col <= row
    cks = [ck_ref[0, pl.ds(2 * pair + hh, 1), :] for hh in range(2)]
    for t in reversed(range(S // tq)):
        r0, r1 = t * tq, (t + 1) * tq
        qt = q_ref[0, r0:r1, :]
        outs = []
        for hh in range(2):
            hsel = (lane >= HEAD_DIM) == bool(hh)
            qm = jnp.where(hsel, qt, jnp.zeros_like(qt))
            ck = cks[hh]
            s = _dot_nt(qm, k_ref[0, :r1, :]) - ck[:, :r1]
            s_d = jnp.where(causal, s[:, r0:], -jnp.inf)
            s = jnp.concatenate([s[:, :r0], s_d], axis=1) if t > 0 else s_d
            m = jnp.max(s, axis=-1, keepdims=True)
            p = jnp.exp2(s - m)
            l = jnp.sum(p, axis=-1, keepdims=True)
            outs.append(_dot(p.astype(BF16), v_ref[0, :r1, :]) / l)
        o_ref[0, r0:r1, :] = jnp.where(lane < HEAD_DIM, outs[0], outs[1]).astype(BF16)


def _fox(p3, ck):
    B, S, _ = p3.shape
    nq, nk, nv = (U_FOX_Q * UNIT // LANES, U_FOX_K * UNIT // LANES, U_FOX_V * UNIT // LANES)
    return pl.pallas_call(
        _fox_kernel,
        grid=(B, FOX_HEADS // 2),
        in_specs=[pl.BlockSpec((1, S, LANES), lambda b, p: (b, 0, nq + p)),
                  pl.BlockSpec((1, S, LANES), lambda b, p: (b, 0, nk + p)),
                  pl.BlockSpec((1, S, LANES), lambda b, p: (b, 0, nv + p)),
                  pl.BlockSpec((1, FOX_HEADS, S), lambda b, p: (b, 0, 0))],
        out_specs=pl.BlockSpec((1, S, LANES), lambda b, p: (b, 0, p)),
        out_shape=jax.ShapeDtypeStruct((B, S, FOX_WIDTH), BF16),
        compiler_params=pltpu.CompilerParams(vmem_limit_bytes=VMEM_LIMIT),
        name="fox_attn",
    )(p3, p3, p3, ck)


def _t5_bucket(dist):
    max_exact = NUM_BUCKETS // 2
    d = np.maximum(dist, 1).astype(np.float32)
    large = max_exact + (np.log(d / max_exact) / np.log(REL_MAX_DISTANCE / max_exact)
                         * (NUM_BUCKETS - max_exact)).astype(np.int32)
    large = np.minimum(large, NUM_BUCKETS - 1)
    return np.where(dist < max_exact, dist, large).astype(np.int32)


def _relbias_kernel(tab_ref, bucket_ref, valid_ref, o_ref):
    g = pl.program_id(0)
    bk = bucket_ref[0]
    vd = valid_ref[0]
    for hs in range(DIL_HEADS_PER_GROUP):
        acc = jnp.zeros(bk.shape, F32)
        for b in range(NUM_BUCKETS):
            acc = jnp.where(bk == b, tab_ref[b, g * DIL_HEADS_PER_GROUP + hs], acc)
        bias = jnp.where(vd != 0, acc * LOG2E, -jnp.inf)
        o_ref[0, hs] = bias
        col = lax.broadcasted_iota(jnp.int32, bias.shape, 1)
        o_ref[1, hs] = jnp.where(col >= DIL_L, bias, -jnp.inf)


def _relbias(table):
    L = DIL_L
    i = np.arange(L)[:, None]
    j = np.arange(2 * L)[None, :]
    m = L + i - j
    valid = ((m >= 0) & (m <= L)).astype(np.int32)
    buckets = np.stack([_t5_bucket(np.clip(m, 0, None) * d) for _, d in DIL_GROUPS])
    valids = np.stack([valid] * N_DIL_GROUPS)
    return pl.pallas_call(
        _relbias_kernel,
        grid=(N_DIL_GROUPS,),
        in_specs=[pl.BlockSpec(memory_space=pltpu.SMEM),
                  pl.BlockSpec((1, L, 2 * L), lambda g: (g, 0, 0)),
                  pl.BlockSpec((1, L, 2 * L), lambda g: (g, 0, 0))],
        out_specs=pl.BlockSpec((2, DIL_HEADS_PER_GROUP, L, 2 * L), lambda g: (0, g, 0, 0)),
        out_shape=jax.ShapeDtypeStruct((2, DIL_HEADS, L, 2 * L), F32),
        name="rel_bias",
    )(table, jnp.asarray(buckets), jnp.asarray(valids))


def _dil_rows(start, d):
    return pl.ds(start, DIL_L) if d == 1 else pl.ds(start, DIL_L, stride=d)


def _dil_block_rows(d, nb, it):
    r, n = it // nb, it % nb
    cur = _dil_rows(r + d * (n * DIL_L), d)
    prev = _dil_rows(r + d * (jnp.maximum(n - 1, 0) * DIL_L), d)
    return cur, prev, 1 - jnp.minimum(n, 1)


def _dil_scores(qkv_ref, bias_ref, s_scr, slot, g, d, nb, it):
    lane = lax.broadcasted_iota(jnp.int32, (1, LANES), 1)
    cur, prev, first = _dil_block_rows(d, nb, it)
    for pr in range(2):
        qt, kt = _unpack_bf16_pair(qkv_ref[0, 3 * g + pr, cur, :])
        if nb > 1:
            _, k_prev = _unpack_bf16_pair(qkv_ref[0, 3 * g + pr, prev, :])
            kt = jnp.concatenate([k_prev, kt], axis=0)
        for hh in range(2):
            hsel = (lane >= HEAD_DIM) == bool(hh)
            qm = jnp.where(hsel, qt, jnp.zeros_like(qt))
            head = 2 * pr + hh
            if nb > 1:
                s_scr[slot, head] = _dot_nt(qm, kt) + bias_ref[first, DIL_HEADS_PER_GROUP * g + head]
            else:
                bias = bias_ref[0, DIL_HEADS_PER_GROUP * g + head, :, DIL_L:]
                s_scr[slot, head, :, :DIL_L] = _dot_nt(qm, kt) + bias


def _dil_merge(qkv_ref, s_scr, slot, m_scr, l_scr, acc_scr, g, d, nb, init, it):
    lane = lax.broadcasted_iota(jnp.int32, (1, LANES), 1)
    cur, prev, _ = _dil_block_rows(d, nb, it)
    v_cur = _unpack_bf16_pair(qkv_ref[0, 3 * g + 2, cur, :])
    if nb > 1:
        v_prev = _unpack_bf16_pair(qkv_ref[0, 3 * g + 2, prev, :])
    for pr in range(2):
        vt = jnp.concatenate([v_prev[pr], v_cur[pr]], axis=0) if nb > 1 else v_cur[pr]
        ms, ls, accs = [], [], []
        for hh in range(2):
            s = s_scr[slot, 2 * pr + hh] if nb > 1 else s_scr[slot, 2 * pr + hh, :, :DIL_L]
            m = jnp.max(s, axis=-1, keepdims=True)
            p = jnp.exp2(s - m)
            ms.append(m)
            ls.append(jnp.sum(p, axis=-1, keepdims=True))
            accs.append(_dot(p.astype(BF16), vt))
        low = lane < HEAD_DIM
        m_b = jnp.where(low, ms[0], ms[1])
        l_b = jnp.where(low, ls[0], ls[1])
        acc_b = jnp.where(low, accs[0], accs[1])
        if init:
            m_scr[pr, cur, :] = m_b
            l_scr[pr, cur, :] = l_b
            acc_scr[pr, cur, :] = acc_b
        else:
            m_o = m_scr[pr, cur, :]
            m_n = jnp.maximum(m_o, m_b)
            a_o = jnp.exp2(m_o - m_n)
            a_b = jnp.exp2(m_b - m_n)
            m_scr[pr, cur, :] = m_n
            l_scr[pr, cur, :] = l_scr[pr, cur, :] * a_o + l_b * a_b
            acc_scr[pr, cur, :] = acc_scr[pr, cur, :] * a_o + acc_b * a_b


def _dil_kernel(qkv_ref, bias_ref, o_ref, m_scr, l_scr, acc_scr, s_scr):
    S = o_ref.shape[1]
    order = sorted(range(N_DIL_GROUPS), key=lambda g: -DIL_GROUPS[g][1])
    for g in order:
        window, d = DIL_GROUPS[g]
        nb = S // window
        total = d * nb
        assert total % 2 == 0
        scores = functools.partial(_dil_scores, qkv_ref, bias_ref, s_scr, g=g, d=d, nb=nb)
        merge = functools.partial(_dil_merge, qkv_ref, s_scr, m_scr=m_scr, l_scr=l_scr,
                                  acc_scr=acc_scr, g=g, d=d, nb=nb, init=g == order[0])

        scores(slot=0, it=0)

        def body(j, carry, scores=scores, merge=merge, total=total):
            scores(slot=1, it=2 * j + 1)
            merge(slot=0, it=2 * j)
            scores(slot=0, it=jnp.minimum(2 * j + 2, total - 1))
            merge(slot=1, it=2 * j + 1)
            return carry
        lax.fori_loop(0, total // 2, body, 0, unroll=2)
    for pr in range(2):
        o_ref[0, :, pr * LANES:(pr + 1) * LANES] = (acc_scr[pr] / l_scr[pr]).astype(BF16)


def _dil(slabs, bias):
    B, _, S, _ = slabs.shape
    for window, d in DIL_GROUPS:
        assert window // d == DIL_L and S % window == 0
    stat = pltpu.VMEM((2, S, LANES), F32)
    return pl.pallas_call(
        _dil_kernel,
        grid=(B,),
        in_specs=[pl.BlockSpec((1, N_SLABS, S, LANES), lambda b: (b, 0, 0, 0)),
                  pl.BlockSpec(bias.shape, lambda b: (0, 0, 0, 0))],
        out_specs=pl.BlockSpec((1, S, DIL_OUT_WIDTH), lambda b: (b, 0, 0)),
        out_shape=jax.ShapeDtypeStruct((B, S, DIL_OUT_WIDTH), BF16),
        scratch_shapes=[stat, stat, stat,
                        pltpu.VMEM((2, DIL_HEADS_PER_GROUP, DIL_L, 2 * DIL_L), F32)],
        compiler_params=pltpu.CompilerParams(vmem_limit_bytes=VMEM_LIMIT),
        name="dil_attn",
    )(slabs, bias)


def _outproj_kernel(x_ref, ya_ref, yd_ref, ga_ref, gb_ref,
                    g1_ref, sc_ref, sh_ref, ng_ref, wa_ref, wb_ref, wo_ref, wr_ref, br_ref,
                    x1_ref, h2_ref, lg_ref):
    n_chunks = 2
    cm = x_ref.shape[0] // n_chunks
    for c in range(n_chunks):
        rows = slice(c * cm, (c + 1) * cm)
        a = _dot(ya_ref[rows, :], wa_ref[...])
        bm = _dot(yd_ref[rows, :], wb_ref[...])
        merged = ga_ref[rows, :].astype(F32) * a + gb_ref[rows, :].astype(F32) * bm
        out = _dot(merged.astype(BF16), wo_ref[...])
        x1 = x_ref[rows, :] + g1_ref[0] * out
        x1_ref[rows, :] = x1
        ms = jnp.mean(x1 * x1, axis=-1, keepdims=True)
        h = x1 * lax.rsqrt(ms + EPS) * ng_ref[...]
        h = h * (1.0 + sc_ref[0]) + sh_ref[0]
        hb = h.astype(BF16)
        h2_ref[rows, :] = hb
        lg_ref[rows, :] = _dot(hb, wr_ref[...]) + br_ref[...]


def _outproj(x2d, ya2d, yd2d, p2d, g1, sc2, sh2, norm_g, wa, wb, wo, wr, br, S):
    T = x2d.shape[0]
    tm = TM_PROJ
    per_b = S // tm
    row = lambda w: pl.BlockSpec((tm, w), lambda i: (i, 0))
    full = lambda a: pl.BlockSpec(a.shape, lambda i: (0,) * a.ndim)
    mod = pl.BlockSpec((1, 1, D_MODEL), lambda i: (i // per_b, 0, 0))
    return pl.pallas_call(
        _outproj_kernel,
        grid=(T // tm,),
        in_specs=[row(D_MODEL), row(FOX_WIDTH), row(DIL_OUT_WIDTH)]
                 + [pl.BlockSpec((tm, D_MODEL), lambda i: (i, U_GATE_A * UNIT // D_MODEL)),
                    pl.BlockSpec((tm, D_MODEL), lambda i: (i, U_GATE_B * UNIT // D_MODEL)),
                    mod, mod, mod, full(norm_g), full(wa), full(wb), full(wo), full(wr), full(br)],
        out_specs=[row(D_MODEL), row(D_MODEL), row(LANES)],
        out_shape=[jax.ShapeDtypeStruct((T, D_MODEL), F32),
                   jax.ShapeDtypeStruct((T, D_MODEL), BF16),
                   jax.ShapeDtypeStruct((T, LANES), F32)],
        compiler_params=pltpu.CompilerParams(vmem_limit_bytes=VMEM_LIMIT),
        name="out_proj",
    )(x2d, ya2d, yd2d, p2d, p2d, g1, sc2, sh2, norm_g, wa, wb, wo, wr, br)


def _dispatch_kernel(lg_ref, h_ref, tri_ref, xs_ref, cnt_ref, pos_ref):
    tt = lg_ref.shape[0]
    lt = lg_ref[...].T
    row = lambda i: lt[i:i + 1, :]
    neg = -jnp.inf
    g = [row(i) for i in range(N_GROUPS)]
    gmax = functools.reduce(jnp.maximum, g)
    gidx = jnp.full(gmax.shape, N_GROUPS - 1, jnp.int32)
    for i in reversed(range(N_GROUPS - 1)):
        gidx = jnp.where(g[i] == gmax, i, gidx)
    gsum = sum(jnp.exp(gi - gmax) for gi in g)
    el = []
    for j in range(EXPERTS_PER_GROUP):
        v = row(N_GROUPS + EXPERTS_PER_GROUP * (N_GROUPS - 1) + j)
        for gg in reversed(range(N_GROUPS - 1)):
            v = jnp.where(gidx == gg, row(N_GROUPS + EXPERTS_PER_GROUP * gg + j), v)
        el.append(v)

    def top(vals):
        best = functools.reduce(jnp.maximum, vals)
        idx = jnp.full(best.shape, EXPERTS_PER_GROUP - 1, jnp.int32)
        for j in reversed(range(EXPERTS_PER_GROUP - 1)):
            idx = jnp.where(vals[j] == best, j, idx)
        return best, idx

    v1, i1 = top(el)
    v2, i2 = top([jnp.where(i1 == j, neg, el[j]) for j in range(EXPERTS_PER_GROUP)])
    t = jnp.exp(v2 - v1)
    den = (1.0 + t) * gsum
    wts = [1.0 / den, t / den]
    eid = [gidx * EXPERTS_PER_GROUP + i1, gidx * EXPERTS_PER_GROUP + i2]

    esub = lax.broadcasted_iota(jnp.int32, (N_EXPERTS, tt), 0)
    ohf = jnp.concatenate([jnp.where(esub == eid[k], 1.0, 0.0) for k in range(2)], axis=1)
    n_pb = 2 * tt // UNIT
    oh_blocks = jnp.concatenate([ohf[:, b * UNIT:(b + 1) * UNIT] for b in range(n_pb)], axis=0)
    res = _dot(oh_blocks.astype(BF16), tri_ref[...])
    cnt = jnp.zeros((N_EXPERTS, LANES), F32)
    pre = []
    for b in range(n_pb):
        r = res[b * N_EXPERTS:(b + 1) * N_EXPERTS]
        pre.append(r[:, :UNIT] + jnp.concatenate([cnt] * (UNIT // LANES), axis=1))
        cnt = cnt + r[:, UNIT:]
    prefix = jnp.concatenate(pre, axis=1)
    cnt = (((cnt.astype(jnp.int32) + (SUBLANES - 1)) // SUBLANES) * SUBLANES).astype(F32)
    esub_c = lax.broadcasted_iota(jnp.int32, cnt.shape, 0)
    start = jnp.zeros_like(cnt)
    for e in range(N_EXPERTS - 1):
        start = start + jnp.where(esub_c > e, cnt[e:e + 1, :], 0.0)
    start_w = jnp.concatenate([start] * (2 * tt // LANES), axis=1)
    pos = jnp.sum(ohf * (start_w + prefix), axis=0, keepdims=True)
    pos_k = [pos[:, :tt], pos[:, tt:]]

    n_rows = xs_ref.shape[0]
    psub = lax.broadcasted_iota(jnp.int32, (n_rows, tt), 0).astype(F32)
    pm = [jnp.where(psub == pos_k[k], 1.0, 0.0).astype(BF16) for k in range(2)]
    xs = _dot(pm[0] + pm[1], h_ref[...])
    wsub = lax.broadcasted_iota(jnp.int32, (LANES, tt), 0)
    ws = jnp.zeros((n_rows, LANES), F32)
    for k in range(2):
        parts = _split3(wts[k])
        wrows = jnp.zeros((LANES, tt), F32)
        for j in range(3):
            wrows = jnp.where(wsub == j, parts[j].astype(F32), wrows)
        ws = ws + _dot_nt(pm[k], wrows.astype(BF16))
    half = D_MODEL // 2
    xs_ref[:, :half] = _pack_bf16_pair(xs[:, :half], xs[:, half:])
    xs_ref[:, half:] = pltpu.bitcast(ws, jnp.uint32)
    cnt_ref[0] = cnt.astype(jnp.int32)
    posr = jnp.where(wsub == 0, pos_k[0], jnp.where(wsub == 1, pos_k[1], 0.0))
    pos_ref[...] = posr.T


def _dispatch(logits, h2):
    T = logits.shape[0]
    tt = MOE_TILE
    n_tiles = T // tt
    tri = np.concatenate([np.triu(np.ones((UNIT, UNIT)), 1), np.ones((UNIT, LANES))], axis=1)
    return pl.pallas_call(
        _dispatch_kernel,
        grid=(n_tiles,),
        in_specs=[pl.BlockSpec((tt, LANES), lambda i: (i, 0)),
                  pl.BlockSpec((tt, D_MODEL), lambda i: (i, 0)),
                  pl.BlockSpec(tri.shape, lambda i: (0, 0))],
        out_specs=[pl.BlockSpec((TILE_ROWS, XS_WIDTH), lambda i: (i, 0)),
                   pl.BlockSpec((1, N_EXPERTS, LANES), lambda i: (i, 0, 0)),
                   pl.BlockSpec((tt, LANES), lambda i: (i, 0))],
        out_shape=[jax.ShapeDtypeStruct((n_tiles * TILE_ROWS, XS_WIDTH), jnp.uint32),
                   jax.ShapeDtypeStruct((n_tiles, N_EXPERTS, LANES), jnp.int32),
                   jax.ShapeDtypeStruct((T, LANES), F32)],
        compiler_params=pltpu.CompilerParams(vmem_limit_bytes=VMEM_LIMIT),
        name="moe_dispatch",
    )(logits, h2, jnp.asarray(tri, BF16))


def _plan_kernel(cnt_ref, be_ref, nv_ref, nxt_ref, grp_ref, used_ref, cs_ref):
    n_tiles = cnt_ref.shape[0]
    n_blk = be_ref.shape[0]
    rows = MOE_ROWS
    row_shift = rows.bit_length() - 1
    grp_shift = SUBLANES.bit_length() - 1
    assert rows == 1 << row_shift and SUBLANES == 1 << grp_shift

    def tile_starts(t, c):
        def per_e(e, acc):
            cs_ref[t * N_EXPERTS + e] = acc
            return acc + cnt_ref[t, e]
        used_ref[t] = lax.fori_loop(0, N_EXPERTS, per_e, 0, unroll=8)
        return c
    lax.fori_loop(0, n_tiles, tile_starts, 0)

    def clear(b, c):
        nv_ref[b] = 0
        return c
    lax.fori_loop(0, n_blk, clear, 0)

    def clear_groups(g, c):
        grp_ref[g] = 0
        return c
    lax.fori_loop(0, n_blk * MOE_GROUPS, clear_groups, 0, unroll=8)

    def per_expert(e, b):
        g0 = b * MOE_GROUPS

        def per_tile(t, tot):
            c = cnt_ref[t, e]
            src = t * TILE_ROWS + cs_ref[t * N_EXPERTS + e]
            first = g0 + lax.shift_right_logical(tot, grp_shift)

            def per_group(k, cc):
                grp_ref[first + k] = src + k * SUBLANES
                return cc
            lax.fori_loop(0, lax.shift_right_logical(c, grp_shift), per_group, 0)
            return tot + c
        tot = lax.fori_loop(0, n_tiles, per_tile, 0)

        def per_block(j, c):
            be_ref[b + j] = e
            nv_ref[b + j] = jnp.minimum(rows, tot - j * rows)
            return c
        nb = lax.shift_right_logical(tot + rows - 1, row_shift)
        lax.fori_loop(0, nb, per_block, 0)
        return b + nb
    n_used = lax.fori_loop(0, N_EXPERTS, per_expert, 0)

    def unused(b, c):
        be_ref[b] = be_ref[n_used - 1]
        nxt_ref[b] = -1
        return c
    lax.fori_loop(n_used, n_blk, unused, 0)

    def next_run(k, nf):
        b = n_used - 1 - k
        nf = jnp.where(be_ref[b] != be_ref[jnp.minimum(b + 1, n_used - 1)], b + 1, nf)
        nxt_ref[b] = nf
        return nf
    lax.fori_loop(0, n_used, next_run, -1)


def _plan(cnt, n_blk):
    n_tiles = cnt.shape[0]
    smem = pl.BlockSpec(memory_space=pltpu.SMEM)
    i32 = lambda n: jax.ShapeDtypeStruct((n,), jnp.int32)
    return pl.pallas_call(
        _plan_kernel,
        in_specs=[smem],
        out_specs=[smem] * 5,
        out_shape=[i32(n_blk), i32(n_blk), i32(n_blk), i32(n_blk * MOE_GROUPS), i32(n_tiles)],
        scratch_shapes=[pltpu.SMEM((n_tiles * N_EXPERTS,), jnp.int32)],
        name="moe_plan",
    )(cnt)


def _pow2_pieces(n, fn):
    for b in reversed(range(SUBLANES.bit_length() - 1, MOE_ROWS.bit_length())):
        size = 1 << b

        @pl.when((n & size) != 0)
        def _():
            fn((n >> (b + 1)) << (b + 1), size)


def _moe_kernel(be_ref, nv_ref, nxt_ref, grp_ref, used_ref,
                w1_hbm, w3_hbm, w2_hbm, xs_hbm, ys_hbm,
                xbuf, ybuf, wb1, wb3, wb2, wst1, wst3, wst2, wslot, gsem, ssem, wsem):
    i = pl.program_id(0)
    last = pl.num_programs(0) - 1
    slot = i % 2
    nv = nv_ref[i]
    half = D_MODEL // 2
    grp_shift = SUBLANES.bit_length() - 1

    def group_row(blk, g):
        return pl.multiple_of(grp_ref[blk * MOE_GROUPS + g], SUBLANES)

    def gather(blk, s):
        top = jnp.maximum(lax.shift_right_logical(nv_ref[blk], grp_shift) - 1, 0)
        for g in range(MOE_GROUPS):
            src = group_row(blk, jnp.minimum(g, top))
            pltpu.make_async_copy(xs_hbm.at[pl.ds(src, SUBLANES)],
                                  xbuf.at[s, pl.ds(g * SUBLANES, SUBLANES)], gsem.at[s]).start()

    def wait_gather(s):
        pltpu.make_async_copy(xs_hbm.at[pl.ds(0, MOE_ROWS)], xbuf.at[s], gsem.at[s]).wait()

    def scatter(blk, s):
        def start(g):
            r = g * SUBLANES if isinstance(g, int) else pl.multiple_of(g * SUBLANES, SUBLANES)
            pltpu.make_async_copy(ybuf.at[s, pl.ds(r, SUBLANES)],
                                  ys_hbm.at[pl.ds(group_row(blk, g), SUBLANES)], ssem.at[s]).start()

        @pl.when(nv_ref[blk] == MOE_ROWS)
        def _():
            for g in range(MOE_GROUPS):
                start(g)

        @pl.when(nv_ref[blk] < MOE_ROWS)
        def _():
            def body(g, c):
                start(g)
                return c
            lax.fori_loop(0, lax.shift_right_logical(nv_ref[blk], grp_shift), body, 0)

    def wait_scatter(s, count):
        @pl.when(count == MOE_ROWS)
        def _():
            pltpu.make_async_copy(ybuf.at[s], ys_hbm.at[pl.ds(0, MOE_ROWS)], ssem.at[s]).wait()

        @pl.when(count < MOE_ROWS)
        def _():
            _pow2_pieces(count, lambda a, size: pltpu.make_async_copy(
                ybuf.at[s, pl.ds(0, size)], ys_hbm.at[pl.ds(0, size)], ssem.at[s]).wait())

    @pl.when(i == 0)
    def _():
        @pl.when(nv > 0)
        def _():
            gather(0, 0)
            gather(jnp.minimum(1, last), 1)

        ybuf[1] = jnp.zeros(ybuf.shape[1:], ybuf.dtype)
        n_tiles = used_ref.shape[0]

        def fill(t, c):
            row0 = t * TILE_ROWS + used_ref[t]
            _pow2_pieces(TILE_ROWS - used_ref[t], lambda a, size: pltpu.make_async_copy(
                ybuf.at[1, pl.ds(0, size)], ys_hbm.at[pl.ds(pl.multiple_of(row0 + a, SUBLANES), size)],
                ssem.at[1]).start())
            return c
        lax.fori_loop(0, n_tiles, fill, 0)

        def drain(t, c):
            wait_scatter(1, TILE_ROWS - used_ref[t])
            return c
        lax.fori_loop(0, n_tiles, drain, 0)

    @pl.when(i >= 2)
    def _():
        wait_scatter(slot, nv_ref[jnp.maximum(i - 2, 0)])

    xslot = i % GATHER_BUFS
    issuer_used = jnp.where(i >= 2, nv_ref[jnp.maximum(i - 2, 0)], nv_ref[0]) > 0

    @pl.when((nv == 0) & (i > 0) & issuer_used)
    def _():
        wait_gather(xslot)

    @pl.when(nv > 0)
    def _():
        e = be_ref[i]
        e_prev = be_ref[jnp.maximum(i - 1, 0)]

        def weight_copies(ex, ws):
            return [pltpu.make_async_copy(src.at[ex], dst.at[ws], wsem.at[ws])
                    for src, dst in ((w1_hbm, wst1), (w3_hbm, wst3), (w2_hbm, wst2))]

        @pl.when(i == 0)
        def _():
            wslot[0] = 0
            for cp in weight_copies(e, 0):
                cp.start()

        @pl.when((i == 0) | (e != e_prev))
        def _():
            ws = wslot[0]
            for cp in weight_copies(e, ws):
                cp.wait()
            wb1[...] = wst1[ws].astype(BF16)
            wb3[...] = wst3[ws].astype(BF16)
            wb2[...] = wst2[ws].astype(BF16)
            nb = nxt_ref[i]

            @pl.when(nb >= 0)
            def _():
                for cp in weight_copies(be_ref[jnp.maximum(nb, 0)], 1 - ws):
                    cp.start()
            wslot[0] = 1 - ws

        wait_gather(xslot)
        gather(jnp.minimum(i + 2, last), (i + 2) % GATHER_BUFS)
        u = xbuf[xslot]
        xa, xb = _unpack_bf16_pair(u[:, :half])
        wv = pltpu.bitcast(u[:, half:], F32)
        roww = wv[:, 0:1] + wv[:, 1:2] + wv[:, 2:3]
        a = _dot(xa, wb1[:half, :]) + _dot(xb, wb1[half:, :])
        b = _dot(xa, wb3[:half, :]) + _dot(xb, wb3[half:, :])
        hmid = (a * jax.nn.sigmoid(a) * b).astype(BF16)
        y = _dot(hmid, wb2[...]) * roww
        ybuf[slot] = _pack_bf16_pair(y[:, :half], y[:, half:])
        scatter(i, slot)

    @pl.when(i == last)
    def _():
        @pl.when((last >= 1) & (nv_ref[jnp.maximum(last - 1, 0)] > 0))
        def _():
            wait_gather((last + 1) % GATHER_BUFS)

        @pl.when(nv > 0)
        def _():
            wait_gather((last + 2) % GATHER_BUFS)

        @pl.when(last >= 1)
        def _():
            wait_scatter(1 - slot, nv_ref[jnp.maximum(last - 1, 0)])
        wait_scatter(slot, nv)


def _moe(xs, plan, w1, w3, w2):
    n_blk = plan[0].shape[0]
    rows = MOE_ROWS
    half = D_MODEL // 2
    hbm = pl.BlockSpec(memory_space=pl.ANY)
    grid_spec = pltpu.PrefetchScalarGridSpec(
        num_scalar_prefetch=5,
        grid=(n_blk,),
        in_specs=[hbm] * 4,
        out_specs=hbm,
        scratch_shapes=[pltpu.VMEM((GATHER_BUFS, rows, XS_WIDTH), jnp.uint32),
                        pltpu.VMEM((2, rows, half), jnp.uint32),
                        pltpu.VMEM((D_MODEL, EXPERT_HIDDEN), BF16),
                        pltpu.VMEM((D_MODEL, EXPERT_HIDDEN), BF16),
                        pltpu.VMEM((EXPERT_HIDDEN, D_MODEL), BF16),
                        pltpu.VMEM((2, D_MODEL, EXPERT_HIDDEN), F32),
                        pltpu.VMEM((2, D_MODEL, EXPERT_HIDDEN), F32),
                        pltpu.VMEM((2, EXPERT_HIDDEN, D_MODEL), F32),
                        pltpu.SMEM((1,), jnp.int32),
                        pltpu.SemaphoreType.DMA((GATHER_BUFS,)),
                        pltpu.SemaphoreType.DMA((2,)),
                        pltpu.SemaphoreType.DMA((2,))])
    return pl.pallas_call(
        _moe_kernel,
        grid_spec=grid_spec,
        out_shape=jax.ShapeDtypeStruct((xs.shape[0], half), jnp.uint32),
        compiler_params=pltpu.CompilerParams(dimension_semantics=("arbitrary",),
                                             vmem_limit_bytes=VMEM_LIMIT),
        name="moe_ffn",
    )(*plan, w1, w3, w2, xs)


def _combine_kernel(x1_ref, ys_ref, pos_ref, g2_ref, o_ref):
    tt = x1_ref.shape[0]
    half = D_MODEL // 2
    pos = pos_ref[...]
    pcol = lax.broadcasted_iota(jnp.int32, (tt, ys_ref.shape[0]), 1).astype(F32)
    sel = (jnp.where(pcol == pos[:, 0:1], 1.0, 0.0) + jnp.where(pcol == pos[:, 1:2], 1.0, 0.0))
    sel = sel.astype(BF16)
    lo, hi = _unpack_bf16_pair(ys_ref[...])
    g2 = g2_ref[0]
    x1 = x1_ref[...]
    o_ref[:, :half] = x1[:, :half] + g2[:, :half] * _dot(sel, lo)
    o_ref[:, half:] = x1[:, half:] + g2[:, half:] * _dot(sel, hi)


def _combine(x1, ys, pos, g2, S):
    T = x1.shape[0]
    tt = MOE_TILE
    per_b = S // tt
    return pl.pallas_call(
        _combine_kernel,
        grid=(T // tt,),
        in_specs=[pl.BlockSpec((tt, D_MODEL), lambda i: (i, 0)),
                  pl.BlockSpec((TILE_ROWS, D_MODEL // 2), lambda i: (i, 0)),
                  pl.BlockSpec((tt, LANES), lambda i: (i, 0)),
                  pl.BlockSpec((1, 1, D_MODEL), lambda i: (i // per_b, 0, 0))],
        out_specs=pl.BlockSpec((tt, D_MODEL), lambda i: (i, 0)),
        out_shape=jax.ShapeDtypeStruct((T, D_MODEL), F32),
        compiler_params=pltpu.CompilerParams(vmem_limit_bytes=VMEM_LIMIT),
        name="moe_combine",
    )(x1, ys, pos, g2)


def _prep_w_in(w_in):
    pad = jnp.zeros((D_MODEL, UNIT - FOX_HEADS), w_in.dtype)
    groups = [w_in[:, OFF_GATE_A:N_IN],
              w_in[:, OFF_FOX_Q:OFF_FOX_F],
              w_in[:, OFF_DIL_Q:OFF_DIL_K],
              w_in[:, OFF_DIL_K:OFF_DIL_V],
              w_in[:, OFF_DIL_V:OFF_GATE_A],
              jnp.concatenate([w_in[:, OFF_FOX_F:OFF_DIL_Q], pad], axis=1)]
    return [w.astype(BF16) for w in groups]


def _prep_gain(q_gain, k_gain):
    qs = HEAD_DIM ** -0.5 * LOG2E
    parts = [q_gain[:FOX_HEADS] * qs, k_gain[:FOX_HEADS], q_gain[FOX_HEADS:] * qs, k_gain[FOX_HEADS:]]
    return jnp.concatenate([p.reshape(-1) for p in parts]).reshape(1, -1)


def _layer(x, mod, rel_bias_table, norm1_g, w_in, b_forget, q_gain, k_gain, w_branch_a, w_branch_b,
           w_out, norm2_g, w_rg, b_rg, w_re, b_re, w1, w3, w2):
    B, S, D = x.shape
    T = B * S
    sh1, sc1, g1, sh2, sc2, g2 = [m.reshape(B, 1, D) for m in jnp.split(mod, 6, axis=-1)]
    x2d = x.reshape(T, D)

    p2d, fgt, slabs = _inproj(x2d, norm1_g.reshape(1, D), sc1, sh1, _prep_w_in(w_in),
                              _prep_gain(q_gain, k_gain), S)
    p3 = p2d.reshape(B, S, P_WIDTH)
    ck = _fcum(fgt.reshape(B, S, LANES), b_forget)
    ya = _fox(p3, ck)
    yd = _dil(slabs, _relbias(rel_bias_table))

    n_router = N_GROUPS + N_EXPERTS
    wr = jnp.concatenate([w_rg, w_re, jnp.zeros((D, LANES - n_router), F32)], axis=1).astype(BF16)
    br = jnp.concatenate([b_rg, b_re, jnp.zeros((LANES - n_router,), F32)]).reshape(1, LANES)
    x1, h2, logits = _outproj(x2d, ya.reshape(T, FOX_WIDTH), yd.reshape(T, DIL_OUT_WIDTH), p2d,
                              g1, sc2, sh2, norm2_g.reshape(1, D),
                              w_branch_a.astype(BF16), w_branch_b.astype(BF16), w_out.astype(BF16),
                              wr, br, S)
    xs, cnt, pos = _dispatch(logits, h2)
    cnt2 = cnt[:, :, 0]
    n_blk = cnt.shape[0] * TILE_ROWS // MOE_ROWS + N_EXPERTS
    plan = _plan(cnt2, n_blk)
    ys = _moe(xs, plan, w1, w3, w2)
    out = _combine(x1, ys, pos, g2, S)
    return out.reshape(B, S, D)


def kernel(x, c, rel_bias_table, w_ada, b_ada, norm1_g, w_in, b_forget, q_gain, k_gain, w_branch_a, w_branch_b, w_out, norm2_g, w_router_group, b_router_group, w_router_expert, b_router_expert, w1, w3, w2):
    depth = w_ada.shape[0]
    for l in range(depth):
        mod = _ada(c, w_ada[l], b_ada[l])
        x = _layer(x, mod, rel_bias_table, norm1_g[l], w_in[l], b_forget[l], q_gain[l], k_gain[l],
                   w_branch_a[l], w_branch_b[l], w_out[l], norm2_g[l], w_router_group[l],
                   b_router_group[l], w_router_expert[l], b_router_expert[l], w1[l], w3[l], w2[l])
    return x
```

```python
import functools
import math

import numpy as np
import jax
import jax.numpy as jnp
from jax import lax
from jax.experimental import pallas as pl
from jax.experimental.pallas import tpu as pltpu

F32 = jnp.float32
BF16 = jnp.bfloat16

D_MODEL = 1024
HEAD_DIM = 64
FOX_HEADS = 8
DIL_GROUPS = ((128, 1), (512, 4), (2048, 16))
DIL_HEADS_PER_GROUP = 4
N_DIL_GROUPS = len(DIL_GROUPS)
DIL_HEADS = N_DIL_GROUPS * DIL_HEADS_PER_GROUP
FOX_WIDTH = FOX_HEADS * HEAD_DIM
DIL_WIDTH = DIL_HEADS * HEAD_DIM
DIL_OUT_WIDTH = DIL_HEADS_PER_GROUP * HEAD_DIM
NUM_BUCKETS = 32
REL_MAX_DISTANCE = 2048
N_GROUPS = 4
EXPERTS_PER_GROUP = 8
N_EXPERTS = N_GROUPS * EXPERTS_PER_GROUP
EXPERT_HIDDEN = D_MODEL // 2
EPS = 1e-6
LOG2E = math.log2(math.e)

OFF_FOX_Q = 0
OFF_FOX_K = OFF_FOX_Q + FOX_WIDTH
OFF_FOX_V = OFF_FOX_K + FOX_WIDTH
OFF_FOX_F = OFF_FOX_V + FOX_WIDTH
OFF_DIL_Q = OFF_FOX_F + FOX_HEADS
OFF_DIL_K = OFF_DIL_Q + DIL_WIDTH
OFF_DIL_V = OFF_DIL_K + DIL_WIDTH
OFF_GATE_A = OFF_DIL_V + DIL_WIDTH
OFF_GATE_B = OFF_GATE_A + D_MODEL
N_IN = OFF_GATE_B + D_MODEL

LANES = 128
UNIT = 256
DIL_L = 128

U_GATE_A, U_GATE_B, U_FOX_Q, U_FOX_K, U_FOX_V, U_DIL, U_FORGET = 0, 4, 8, 10, 12, 14, 23
N_UNITS = 24
P_WIDTH = U_DIL * UNIT
N_SLABS = 3 * N_DIL_GROUPS
_KIND = (["gate"] * 8 + ["norm"] * 4 + ["plain"] * 2 + ["norm", "norm", "plain"] * 3 + ["forget"])

TM_INPROJ = 1024
TM_PROJ = 1024
TQ_FOX = 512
MOE_ROWS = 256
MOE_TILE = 512
XS_WIDTH = D_MODEL // 2 + LANES
SUBLANES = 8
TILE_ROWS = 2 * MOE_TILE + N_EXPERTS * SUBLANES
MOE_GROUPS = MOE_ROWS // SUBLANES
GATHER_BUFS = 3
VMEM_LIMIT = 56 * 1024 * 1024


def _dot(a, b):
    return jnp.dot(a, b, preferred_element_type=F32)


def _dot_nt(a, b):
    return lax.dot_general(a, b, (((1,), (1,)), ((), ())), preferred_element_type=F32)


def _split3(x):
    hi = x.astype(BF16)
    r1 = x - hi.astype(F32)
    mid = r1.astype(BF16)
    lo = (r1 - mid.astype(F32)).astype(BF16)
    return hi, mid, lo


def _ada_kernel(c_ref, w_ref, b_ref, o_ref):
    c = c_ref[...]
    s = c * jax.nn.sigmoid(c)
    s_hi = s.astype(BF16)
    s_lo = (s - s_hi.astype(F32)).astype(BF16)
    w = w_ref[...]
    w_hi = w.astype(BF16)
    w_lo = (w - w_hi.astype(F32)).astype(BF16)
    acc = _dot(s_hi, w_hi) + _dot(s_hi, w_lo) + _dot(s_lo, w_hi)
    o_ref[...] = acc + b_ref[...]


def _ada(c, w_ada, b_ada):
    B = c.shape[0]
    n_out = w_ada.shape[1]
    tn = 512
    return pl.pallas_call(
        _ada_kernel,
        grid=(n_out // tn,),
        in_specs=[pl.BlockSpec((B, D_MODEL), lambda j: (0, 0)),
                  pl.BlockSpec((D_MODEL, tn), lambda j: (0, j)),
                  pl.BlockSpec((1, tn), lambda j: (0, j))],
        out_specs=pl.BlockSpec((B, tn), lambda j: (0, j)),
        out_shape=jax.ShapeDtypeStruct((B, n_out), F32),
        name="ada_mod",
    )(c, w_ada, b_ada.reshape(1, n_out))


def _pack_bf16_pair(lo, hi):
    lo_bits = pltpu.bitcast(lo.astype(BF16).astype(F32), jnp.uint32) >> 16
    hi_bits = pltpu.bitcast(hi.astype(BF16).astype(F32), jnp.uint32) & jnp.uint32(0xFFFF0000)
    return lo_bits | hi_bits


def _unpack_bf16_pair(u):
    lo = pltpu.bitcast(u << 16, F32).astype(BF16)
    hi = pltpu.bitcast(u & jnp.uint32(0xFFFF0000), F32).astype(BF16)
    return lo, hi


def _inproj_kernel(x_ref, g_ref, sc_ref, sh_ref, wg_ref, wf_ref, wq_ref, wk_ref, wv_ref, wz_ref,
                   gain_ref, bd_ref, p_ref, f_ref, s_ref):
    x = x_ref[...]
    ms = jnp.mean(x * x, axis=-1, keepdims=True)
    h = x * lax.rsqrt(ms + EPS) * g_ref[...]
    h = h * (1.0 + sc_ref[0]) + sh_ref[0]
    hb = h.astype(BF16)

    def weights(u):
        if u < U_FOX_Q:
            return wg_ref, (u - U_GATE_A) * UNIT
        if u < U_DIL:
            return wf_ref, (u - U_FOX_Q) * UNIT
        if u < U_FORGET:
            g, j = divmod(u - U_DIL, 3)
            return (wq_ref, wk_ref, wv_ref)[j], g * UNIT
        return wz_ref, 0

    def gain_col(u):
        if u < U_DIL:
            return (u - U_FOX_Q) * UNIT
        g, j = divmod(u - U_DIL, 3)
        return 2 * FOX_WIDTH + j * DIL_WIDTH + g * UNIT

    def unit(u):
        w_ref, c0 = weights(u)
        acc = _dot(hb, w_ref[:, c0:c0 + UNIT])
        kind = _KIND[u]
        if kind == "gate":
            return jax.nn.sigmoid(acc)
        if kind == "norm":
            ss = _dot((acc * acc).astype(BF16), bd_ref[...])
            gc = gain_col(u)
            return acc * lax.rsqrt(ss * (1.0 / HEAD_DIM) + EPS) * gain_ref[:, gc:gc + UNIT]
        return acc

    dil_q = {}

    def emit(u):
        o = unit(u)
        if u < U_DIL:
            p_ref[:, u * UNIT:(u + 1) * UNIT] = o.astype(BF16)
        elif u < U_FORGET:
            g, j = divmod(u - U_DIL, 3)
            if j == 0:
                dil_q[g] = o
            elif j == 1:
                q = dil_q.pop(g)
                s_ref[0, 3 * g] = _pack_bf16_pair(q[:, :LANES], o[:, :LANES])
                s_ref[0, 3 * g + 1] = _pack_bf16_pair(q[:, LANES:], o[:, LANES:])
            else:
                s_ref[0, 3 * g + 2] = _pack_bf16_pair(o[:, :LANES], o[:, LANES:])
        else:
            f_ref[...] = o[:, :LANES]

    normed = [u for u in range(N_UNITS) if _KIND[u] == "norm"]
    others = [u for u in range(N_UNITS) if _KIND[u] != "norm"]
    while normed or others:
        for group in (normed, others):
            if group:
                emit(group.pop(0))


def _inproj(x2d, norm_g, sc, sh, w_re, gain_row, S):
    T = x2d.shape[0]
    tm = TM_INPROJ
    per_b = S // tm
    bd = np.kron(np.eye(UNIT // HEAD_DIM), np.ones((HEAD_DIM, HEAD_DIM))).astype(np.float32)
    once = dict(pipeline_mode=pl.Buffered(1))
    return pl.pallas_call(
        _inproj_kernel,
        grid=(T // tm,),
        in_specs=[pl.BlockSpec((tm, D_MODEL), lambda i: (i, 0)),
                  pl.BlockSpec((1, D_MODEL), lambda i: (0, 0)),
                  pl.BlockSpec((1, 1, D_MODEL), lambda i: (i // per_b, 0, 0)),
                  pl.BlockSpec((1, 1, D_MODEL), lambda i: (i // per_b, 0, 0)),
                  *[pl.BlockSpec(w.shape, lambda i: (0, 0), **once) for w in w_re],
                  pl.BlockSpec(gain_row.shape, lambda i: (0, 0), **once),
                  pl.BlockSpec((UNIT, UNIT), lambda i: (0, 0), **once)],
        out_specs=[pl.BlockSpec((tm, P_WIDTH), lambda i: (i, 0)),
                   pl.BlockSpec((tm, LANES), lambda i: (i, 0)),
                   pl.BlockSpec((1, N_SLABS, tm, LANES), lambda i: (i // per_b, 0, i % per_b, 0))],
        out_shape=[jax.ShapeDtypeStruct((T, P_WIDTH), BF16),
                   jax.ShapeDtypeStruct((T, LANES), F32),
                   jax.ShapeDtypeStruct((T // S, N_SLABS, S, LANES), jnp.uint32)],
        compiler_params=pltpu.CompilerParams(vmem_limit_bytes=VMEM_LIMIT),
        name="in_proj",
    )(x2d, norm_g, sc, sh, *w_re, gain_row, jnp.asarray(bd, BF16))


def _fcum_kernel(f_ref, b_ref, tri_ref, o_ref):
    S = f_ref.shape[1]
    xf = f_ref[0] + b_ref[...]
    ls = (jnp.minimum(xf, 0.0) - jnp.log(1.0 + jnp.exp(-jnp.abs(xf)))) * LOG2E
    lst = ls.T
    carry = jnp.zeros((LANES, UNIT), F32)
    for blk in range(S // UNIT):
        seg = lst[:, blk * UNIT:(blk + 1) * UNIT]
        hi, mid, lo = _split3(seg)
        tri = tri_ref[...]
        res = _dot(hi, tri) + _dot(mid, tri) + _dot(lo, tri)
        o_ref[0, :, blk * UNIT:(blk + 1) * UNIT] = (res[:, :UNIT] + carry)[:FOX_HEADS]
        carry = carry + res[:, UNIT:]


def _fcum(fgt, b_forget):
    B, S, _ = fgt.shape
    brow = jnp.zeros((1, LANES), F32).at[0, :FOX_HEADS].set(b_forget)
    tri = np.concatenate([np.triu(np.ones((UNIT, UNIT))), np.ones((UNIT, UNIT))], axis=1)
    return pl.pallas_call(
        _fcum_kernel,
        grid=(B,),
        in_specs=[pl.BlockSpec((1, S, LANES), lambda b: (b, 0, 0)),
                  pl.BlockSpec((1, LANES), lambda b: (0, 0)),
                  pl.BlockSpec((UNIT, 2 * UNIT), lambda b: (0, 0))],
        out_specs=pl.BlockSpec((1, FOX_HEADS, S), lambda b: (b, 0, 0)),
        out_shape=jax.ShapeDtypeStruct((B, FOX_HEADS, S), F32),
        name="forget_cumsum",
    )(fgt, brow, jnp.asarray(tri, BF16))


def _fox_kernel(q_ref, k_ref, v_ref, ck_ref, o_ref):
    S = q_ref.shape[1]
    pair = pl.program_id(1)
    tq = TQ_FOX
    lane = lax.broadcasted_iota(jnp.int32, (1, LANES), 1)
    row = lax.broadcasted_iota(jnp.int32, (tq, tq), 0)
    col = lax.broadcasted_iota(jnp.int32, (tq, tq), 1)
    causal = col <= row
    cks = [ck_ref[0, pl.ds(2 * pair + hh, 1), :] for hh in range(2)]
    for t in reversed(range(S // tq)):
        r0, r1 = t * tq, (t + 1) * tq
        qt = q_ref[0, r0:r1, :]
        zero = jnp.zeros_like(qt)
        q2 = jnp.concatenate([jnp.where(lane < HEAD_DIM, qt, zero),
                              jnp.where(lane >= HEAD_DIM, qt, zero)], axis=0)
        s2 = _dot_nt(q2, k_ref[0, :r1, :])
        halves = []
        for hh in range(2):
            s = s2[hh * tq:(hh + 1) * tq] - cks[hh][:, :r1]
            s_d = jnp.where(causal, s[:, r0:], -jnp.inf)
            halves.append(jnp.concatenate([s[:, :r0], s_d], axis=1) if t > 0 else s_d)
        s = jnp.concatenate(halves, axis=0)
        m = jnp.max(s, axis=-1, keepdims=True)
        p = jnp.exp2(s - m)
        l = jnp.sum(p, axis=-1, keepdims=True)
        o2 = _dot(p.astype(BF16), v_ref[0, :r1, :]) / l
        o_ref[0, r0:r1, :] = jnp.where(lane < HEAD_DIM, o2[:tq], o2[tq:]).astype(BF16)


def _fox(p3, ck):
    B, S, _ = p3.shape
    nq, nk, nv = (U_FOX_Q * UNIT // LANES, U_FOX_K * UNIT // LANES, U_FOX_V * UNIT // LANES)
    return pl.pallas_call(
        _fox_kernel,
        grid=(B, FOX_HEADS // 2),
        in_specs=[pl.BlockSpec((1, S, LANES), lambda b, p: (b, 0, nq + p)),
                  pl.BlockSpec((1, S, LANES), lambda b, p: (b, 0, nk + p)),
                  pl.BlockSpec((1, S, LANES), lambda b, p: (b, 0, nv + p)),
                  pl.BlockSpec((1, FOX_HEADS, S), lambda b, p: (b, 0, 0))],
        out_specs=pl.BlockSpec((1, S, LANES), lambda b, p: (b, 0, p)),
        out_shape=jax.ShapeDtypeStruct((B, S, FOX_WIDTH), BF16),
        compiler_params=pltpu.CompilerParams(vmem_limit_bytes=VMEM_LIMIT),
        name="fox_attn",
    )(p3, p3, p3, ck)


def _t5_bucket(dist):
    max_exact = NUM_BUCKETS // 2
    d = np.maximum(dist, 1).astype(np.float32)
    large = max_exact + (np.log(d / max_exact) / np.log(REL_MAX_DISTANCE / max_exact)
                         * (NUM_BUCKETS - max_exact)).astype(np.int32)
    large = np.minimum(large, NUM_BUCKETS - 1)
    return np.where(dist < max_exact, dist, large).astype(np.int32)


def _relbias_kernel(tab_ref, bucket_ref, valid_ref, o_ref):
    g = pl.program_id(0)
    bk = bucket_ref[0]
    vd = valid_ref[0]
    for hs in range(DIL_HEADS_PER_GROUP):
        acc = jnp.zeros(bk.shape, F32)
        for b in range(NUM_BUCKETS):
            acc = jnp.where(bk == b, tab_ref[b, g * DIL_HEADS_PER_GROUP + hs], acc)
        bias = jnp.where(vd != 0, acc * LOG2E, -jnp.inf)
        o_ref[0, hs] = bias
        col = lax.broadcasted_iota(jnp.int32, bias.shape, 1)
        o_ref[1, hs] = jnp.where(col >= DIL_L, bias, -jnp.inf)


def _relbias(table):
    L = DIL_L
    i = np.arange(L)[:, None]
    j = np.arange(2 * L)[None, :]
    m = L + i - j
    valid = ((m >= 0) & (m <= L)).astype(np.int32)
    buckets = np.stack([_t5_bucket(np.clip(m, 0, None) * d) for _, d in DIL_GROUPS])
    valids = np.stack([valid] * N_DIL_GROUPS)
    return pl.pallas_call(
        _relbias_kernel,
        grid=(N_DIL_GROUPS,),
        in_specs=[pl.BlockSpec(memory_space=pltpu.SMEM),
                  pl.BlockSpec((1, L, 2 * L), lambda g: (g, 0, 0)),
                  pl.BlockSpec((1, L, 2 * L), lambda g: (g, 0, 0))],
        out_specs=pl.BlockSpec((2, DIL_HEADS_PER_GROUP, L, 2 * L), lambda g: (0, g, 0, 0)),
        out_shape=jax.ShapeDtypeStruct((2, DIL_HEADS, L, 2 * L), F32),
        name="rel_bias",
    )(table, jnp.asarray(buckets), jnp.asarray(valids))


def _dil_rows(start, d):
    return pl.ds(start, DIL_L) if d == 1 else pl.ds(start, DIL_L, stride=d)


def _dil_block_rows(d, nb, it):
    r, n = it // nb, it % nb
    cur = _dil_rows(r + d * (n * DIL_L), d)
    prev = _dil_rows(r + d * (jnp.maximum(n - 1, 0) * DIL_L), d)
    return cur, prev, 1 - jnp.minimum(n, 1)


def _dil_scores(qkv_ref, bias_ref, s_scr, slot, g, d, nb, it):
    lane = lax.broadcasted_iota(jnp.int32, (1, LANES), 1)
    cur, prev, first = _dil_block_rows(d, nb, it)
    for pr in range(2):
        qt, kt = _unpack_bf16_pair(qkv_ref[0, 3 * g + pr, cur, :])
        if nb > 1:
            _, k_prev = _unpack_bf16_pair(qkv_ref[0, 3 * g + pr, prev, :])
            kt = jnp.concatenate([k_prev, kt], axis=0)
        for hh in range(2):
            hsel = (lane >= HEAD_DIM) == bool(hh)
            qm = jnp.where(hsel, qt, jnp.zeros_like(qt))
            head = 2 * pr + hh
            if nb > 1:
                s_scr[slot, head] = _dot_nt(qm, kt) + bias_ref[first, DIL_HEADS_PER_GROUP * g + head]
            else:
                bias = bias_ref[0, DIL_HEADS_PER_GROUP * g + head, :, DIL_L:]
                s_scr[slot, head, :, :DIL_L] = _dot_nt(qm, kt) + bias


def _dil_merge(qkv_ref, s_scr, slot, m_scr, l_scr, acc_scr, g, d, nb, init, it):
    lane = lax.broadcasted_iota(jnp.int32, (1, LANES), 1)
    cur, prev, _ = _dil_block_rows(d, nb, it)
    v_cur = _unpack_bf16_pair(qkv_ref[0, 3 * g + 2, cur, :])
    if nb > 1:
        v_prev = _unpack_bf16_pair(qkv_ref[0, 3 * g + 2, prev, :])
    for pr in range(2):
        vt = jnp.concatenate([v_prev[pr], v_cur[pr]], axis=0) if nb > 1 else v_cur[pr]
        ms, ls, accs = [], [], []
        for hh in range(2):
            s = s_scr[slot, 2 * pr + hh] if nb > 1 else s_scr[slot, 2 * pr + hh, :, :DIL_L]
            m = jnp.max(s, axis=-1, keepdims=True)
            p = jnp.exp2(s - m)
            ms.append(m)
            ls.append(jnp.sum(p, axis=-1, keepdims=True))
            accs.append(_dot(p.astype(BF16), vt))
        low = lane < HEAD_DIM
        m_b = jnp.where(low, ms[0], ms[1])
        l_b = jnp.where(low, ls[0], ls[1])
        acc_b = jnp.where(low, accs[0], accs[1])
        if init:
            m_scr[pr, cur, :] = m_b
            l_scr[pr, cur, :] = l_b
            acc_scr[pr, cur, :] = acc_b
        else:
            m_o = m_scr[pr, cur, :]
            m_n = jnp.maximum(m_o, m_b)
            a_o = jnp.exp2(m_o - m_n)
            a_b = jnp.exp2(m_b - m_n)
            m_scr[pr, cur, :] = m_n
            l_scr[pr, cur, :] = l_scr[pr, cur, :] * a_o + l_b * a_b
            acc_scr[pr, cur, :] = acc_scr[pr, cur, :] * a_o + acc_b * a_b


def _dil_kernel(qkv_ref, bias_ref, o_ref, m_scr, l_scr, acc_scr, s_scr):
    S = o_ref.shape[1]
    order = sorted(range(N_DIL_GROUPS), key=lambda g: -DIL_GROUPS[g][1])
    for g in order:
        window, d = DIL_GROUPS[g]
        nb = S // window
        total = d * nb
        assert total % 2 == 0
        scores = functools.partial(_dil_scores, qkv_ref, bias_ref, s_scr, g=g, d=d, nb=nb)
        merge = functools.partial(_dil_merge, qkv_ref, s_scr, m_scr=m_scr, l_scr=l_scr,
                                  acc_scr=acc_scr, g=g, d=d, nb=nb, init=g == order[0])

        scores(slot=0, it=0)

        def body(j, carry, scores=scores, merge=merge, total=total):
            scores(slot=1, it=2 * j + 1)
            merge(slot=0, it=2 * j)
            scores(slot=0, it=jnp.minimum(2 * j + 2, total - 1))
            merge(slot=1, it=2 * j + 1)
            return carry
        lax.fori_loop(0, total // 2, body, 0, unroll=2)
    for pr in range(2):
        o_ref[0, :, pr * LANES:(pr + 1) * LANES] = (acc_scr[pr] / l_scr[pr]).astype(BF16)


def _dil(slabs, bias):
    B, _, S, _ = slabs.shape
    for window, d in DIL_GROUPS:
        assert window // d == DIL_L and S % window == 0
    stat = pltpu.VMEM((2, S, LANES), F32)
    return pl.pallas_call(
        _dil_kernel,
        grid=(B,),
        in_specs=[pl.BlockSpec((1, N_SLABS, S, LANES), lambda b: (b, 0, 0, 0)),
                  pl.BlockSpec(bias.shape, lambda b: (0, 0, 0, 0))],
        out_specs=pl.BlockSpec((1, S, DIL_OUT_WIDTH), lambda b: (b, 0, 0)),
        out_shape=jax.ShapeDtypeStruct((B, S, DIL_OUT_WIDTH), BF16),
        scratch_shapes=[stat, stat, stat,
                        pltpu.VMEM((2, DIL_HEADS_PER_GROUP, DIL_L, 2 * DIL_L), F32)],
        compiler_params=pltpu.CompilerParams(vmem_limit_bytes=VMEM_LIMIT),
        name="dil_attn",
    )(slabs, bias)


def _outproj_kernel(x_ref, ya_ref, yd_ref, ga_ref, gb_ref,
                    g1_ref, sc_ref, sh_ref, ng_ref, wa_ref, wb_ref, wo_ref, wr_ref, br_ref,
                    x1_ref, h2_ref, lg_ref):
    n_chunks = 2
    cm = x_ref.shape[0] // n_chunks
    for c in range(n_chunks):
        rows = slice(c * cm, (c + 1) * cm)
        a = _dot(ya_ref[rows, :], wa_ref[...])
        bm = _dot(yd_ref[rows, :], wb_ref[...])
        merged = ga_ref[rows, :].astype(F32) * a + gb_ref[rows, :].astype(F32) * bm
        out = _dot(merged.astype(BF16), wo_ref[...])
        x1 = x_ref[rows, :] + g1_ref[0] * out
        x1_ref[rows, :] = x1
        ms = jnp.mean(x1 * x1, axis=-1, keepdims=True)
        h = x1 * lax.rsqrt(ms + EPS) * ng_ref[...]
        h = h * (1.0 + sc_ref[0]) + sh_ref[0]
        hb = h.astype(BF16)
        h2_ref[rows, :] = hb
        lg_ref[rows, :] = _dot(hb, wr_ref[...]) + br_ref[...]


def _outproj(x2d, ya2d, yd2d, p2d, g1, sc2, sh2, norm_g, wa, wb, wo, wr, br, S):
    T = x2d.shape[0]
    tm = TM_PROJ
    per_b = S // tm
    row = lambda w: pl.BlockSpec((tm, w), lambda i: (i, 0))
    full = lambda a: pl.BlockSpec(a.shape, lambda i: (0,) * a.ndim)
    mod = pl.BlockSpec((1, 1, D_MODEL), lambda i: (i // per_b, 0, 0))
    return pl.pallas_call(
        _outproj_kernel,
        grid=(T // tm,),
        in_specs=[row(D_MODEL), row(FOX_WIDTH), row(DIL_OUT_WIDTH)]
                 + [pl.BlockSpec((tm, D_MODEL), lambda i: (i, U_GATE_A * UNIT // D_MODEL)),
                    pl.BlockSpec((tm, D_MODEL), lambda i: (i, U_GATE_B * UNIT // D_MODEL)),
                    mod, mod, mod, full(norm_g), full(wa), full(wb), full(wo), full(wr), full(br)],
        out_specs=[row(D_MODEL), row(D_MODEL), row(LANES)],
        out_shape=[jax.ShapeDtypeStruct((T, D_MODEL), F32),
                   jax.ShapeDtypeStruct((T, D_MODEL), BF16),
                   jax.ShapeDtypeStruct((T, LANES), F32)],
        compiler_params=pltpu.CompilerParams(vmem_limit_bytes=VMEM_LIMIT),
        name="out_proj",
    )(x2d, ya2d, yd2d, p2d, p2d, g1, sc2, sh2, norm_g, wa, wb, wo, wr, br)


def _dispatch_kernel(lg_ref, h_ref, tri_ref, xs_ref, cnt_ref, pos_ref):
    tt = lg_ref.shape[0]
    lt = lg_ref[...].T
    row = lambda i: lt[i:i + 1, :]
    neg = -jnp.inf
    g = [row(i) for i in range(N_GROUPS)]
    gmax = functools.reduce(jnp.maximum, g)
    gidx = jnp.full(gmax.shape, N_GROUPS - 1, jnp.int32)
    for i in reversed(range(N_GROUPS - 1)):
        gidx = jnp.where(g[i] == gmax, i, gidx)
    gsum = sum(jnp.exp(gi - gmax) for gi in g)
    el = []
    for j in range(EXPERTS_PER_GROUP):
        v = row(N_GROUPS + EXPERTS_PER_GROUP * (N_GROUPS - 1) + j)
        for gg in reversed(range(N_GROUPS - 1)):
            v = jnp.where(gidx == gg, row(N_GROUPS + EXPERTS_PER_GROUP * gg + j), v)
        el.append(v)

    def top(vals):
        best = functools.reduce(jnp.maximum, vals)
        idx = jnp.full(best.shape, EXPERTS_PER_GROUP - 1, jnp.int32)
        for j in reversed(range(EXPERTS_PER_GROUP - 1)):
            idx = jnp.where(vals[j] == best, j, idx)
        return best, idx

    v1, i1 = top(el)
    v2, i2 = top([jnp.where(i1 == j, neg, el[j]) for j in range(EXPERTS_PER_GROUP)])
    t = jnp.exp(v2 - v1)
    den = (1.0 + t) * gsum
    wts = [1.0 / den, t / den]
    eid = [gidx * EXPERTS_PER_GROUP + i1, gidx * EXPERTS_PER_GROUP + i2]

    esub = lax.broadcasted_iota(jnp.int32, (N_EXPERTS, tt), 0)
    ohf = jnp.concatenate([jnp.where(esub == eid[k], 1.0, 0.0) for k in range(2)], axis=1)
    n_pb = 2 * tt // UNIT
    oh_blocks = jnp.concatenate([ohf[:, b * UNIT:(b + 1) * UNIT] for b in range(n_pb)], axis=0)
    res = _dot(oh_blocks.astype(BF16), tri_ref[...])
    cnt = jnp.zeros((N_EXPERTS, LANES), F32)
    pre = []
    for b in range(n_pb):
        r = res[b * N_EXPERTS:(b + 1) * N_EXPERTS]
        pre.append(r[:, :UNIT] + jnp.concatenate([cnt] * (UNIT // LANES), axis=1))
        cnt = cnt + r[:, UNIT:]
    prefix = jnp.concatenate(pre, axis=1)
    cnt = (((cnt.astype(jnp.int32) + (SUBLANES - 1)) // SUBLANES) * SUBLANES).astype(F32)
    esub_c = lax.broadcasted_iota(jnp.int32, cnt.shape, 0)
    start = jnp.zeros_like(cnt)
    for e in range(N_EXPERTS - 1):
        start = start + jnp.where(esub_c > e, cnt[e:e + 1, :], 0.0)
    start_w = jnp.concatenate([start] * (2 * tt // LANES), axis=1)
    pos = jnp.sum(ohf * (start_w + prefix), axis=0, keepdims=True)
    pos_k = [pos[:, :tt], pos[:, tt:]]

    n_rows = xs_ref.shape[0]
    psub = lax.broadcasted_iota(jnp.int32, (n_rows, tt), 0).astype(F32)
    pm = [jnp.where(psub == pos_k[k], 1.0, 0.0).astype(BF16) for k in range(2)]
    xs = _dot(pm[0] + pm[1], h_ref[...])
    wsub = lax.broadcasted_iota(jnp.int32, (LANES, tt), 0)
    ws = jnp.zeros((n_rows, LANES), F32)
    for k in range(2):
        parts = _split3(wts[k])
        wrows = jnp.zeros((LANES, tt), F32)
        for j in range(3):
            wrows = jnp.where(wsub == j, parts[j].astype(F32), wrows)
        ws = ws + _dot_nt(pm[k], wrows.astype(BF16))
    half = D_MODEL // 2
    xs_ref[:, :half] = _pack_bf16_pair(xs[:, :half], xs[:, half:])
    xs_ref[:, half:] = pltpu.bitcast(ws, jnp.uint32)
    cnt_ref[0] = cnt.astype(jnp.int32)
    posr = jnp.where(wsub == 0, pos_k[0], jnp.where(wsub == 1, pos_k[1], 0.0))
    pos_ref[...] = posr.T


def _dispatch(logits, h2):
    T = logits.shape[0]
    tt = MOE_TILE
    n_tiles = T // tt
    tri = np.concatenate([np.triu(np.ones((UNIT, UNIT)), 1), np.ones((UNIT, LANES))], axis=1)
    return pl.pallas_call(
        _dispatch_kernel,
        grid=(n_tiles,),
        in_specs=[pl.BlockSpec((tt, LANES), lambda i: (i, 0)),
                  pl.BlockSpec((tt, D_MODEL), lambda i: (i, 0)),
                  pl.BlockSpec(tri.shape, lambda i: (0, 0))],
        out_specs=[pl.BlockSpec((TILE_ROWS, XS_WIDTH), lambda i: (i, 0)),
                   pl.BlockSpec((1, N_EXPERTS, LANES), lambda i: (i, 0, 0)),
                   pl.BlockSpec((tt, LANES), lambda i: (i, 0))],
        out_shape=[jax.ShapeDtypeStruct((n_tiles * TILE_ROWS, XS_WIDTH), jnp.uint32),
                   jax.ShapeDtypeStruct((n_tiles, N_EXPERTS, LANES), jnp.int32),
                   jax.ShapeDtypeStruct((T, LANES), F32)],
        compiler_params=pltpu.CompilerParams(vmem_limit_bytes=VMEM_LIMIT),
        name="moe_dispatch",
    )(logits, h2, jnp.asarray(tri, BF16))


def _plan_kernel(cnt_ref, be_ref, nv_ref, nxt_ref, grp_ref, used_ref, cs_ref):
    n_tiles = cnt_ref.shape[0]
    n_blk = be_ref.shape[0]
    rows = MOE_ROWS
    row_shift = rows.bit_length() - 1
    grp_shift = SUBLANES.bit_length() - 1
    assert rows == 1 << row_shift and SUBLANES == 1 << grp_shift

    def tile_starts(t, c):
        def per_e(e, acc):
            cs_ref[t * N_EXPERTS + e] = acc
            return acc + cnt_ref[t, e]
        used_ref[t] = lax.fori_loop(0, N_EXPERTS, per_e, 0, unroll=8)
        return c
    lax.fori_loop(0, n_tiles, tile_starts, 0)

    def clear(b, c):
        nv_ref[b] = 0
        return c
    lax.fori_loop(0, n_blk, clear, 0)

    def clear_groups(g, c):
        grp_ref[g] = 0
        return c
    lax.fori_loop(0, n_blk * MOE_GROUPS, clear_groups, 0, unroll=8)

    def per_expert(e, b):
        g0 = b * MOE_GROUPS

        def per_tile(t, tot):
            c = cnt_ref[t, e]
            src = t * TILE_ROWS + cs_ref[t * N_EXPERTS + e]
            first = g0 + lax.shift_right_logical(tot, grp_shift)

            def per_group(k, cc):
                grp_ref[first + k] = src + k * SUBLANES
                return cc
            lax.fori_loop(0, lax.shift_right_logical(c, grp_shift), per_group, 0)
            return tot + c
        tot = lax.fori_loop(0, n_tiles, per_tile, 0)

        def per_block(j, c):
            be_ref[b + j] = e
            nv_ref[b + j] = jnp.minimum(rows, tot - j * rows)
            return c
        nb = lax.shift_right_logical(tot + rows - 1, row_shift)
        lax.fori_loop(0, nb, per_block, 0)
        return b + nb
    n_used = lax.fori_loop(0, N_EXPERTS, per_expert, 0)

    def unused(b, c):
        be_ref[b] = be_ref[n_used - 1]
        nxt_ref[b] = -1
        return c
    lax.fori_loop(n_used, n_blk, unused, 0)

    def next_run(k, nf):
        b = n_used - 1 - k
        nf = jnp.where(be_ref[b] != be_ref[jnp.minimum(b + 1, n_used - 1)], b + 1, nf)
        nxt_ref[b] = nf
        return nf
    lax.fori_loop(0, n_used, next_run, -1)


def _plan(cnt, n_blk):
    n_tiles = cnt.shape[0]
    smem = pl.BlockSpec(memory_space=pltpu.SMEM)
    i32 = lambda n: jax.ShapeDtypeStruct((n,), jnp.int32)
    return pl.pallas_call(
        _plan_kernel,
        in_specs=[smem],
        out_specs=[smem] * 5,
        out_shape=[i32(n_blk), i32(n_blk), i32(n_blk), i32(n_blk * MOE_GROUPS), i32(n_tiles)],
        scratch_shapes=[pltpu.SMEM((n_tiles * N_EXPERTS,), jnp.int32)],
        name="moe_plan",
    )(cnt)


def _pow2_pieces(n, fn):
    for b in reversed(range(SUBLANES.bit_length() - 1, MOE_ROWS.bit_length())):
        size = 1 << b

        @pl.when((n & size) != 0)
        def _():
            fn((n >> (b + 1)) << (b + 1), size)


def _moe_kernel(be_ref, nv_ref, nxt_ref, grp_ref, used_ref,
                w1_hbm, w3_hbm, w2_hbm, xs_hbm, ys_hbm,
                xbuf, ybuf, wb1, wb3, wb2, wst1, wst3, wst2, wslot, gsem, ssem, wsem):
    i = pl.program_id(0)
    last = pl.num_programs(0) - 1
    slot = i % 2
    nv = nv_ref[i]
    half = D_MODEL // 2
    grp_shift = SUBLANES.bit_length() - 1

    def group_row(blk, g):
        return pl.multiple_of(grp_ref[blk * MOE_GROUPS + g], SUBLANES)

    def gather(blk, s):
        top = jnp.maximum(lax.shift_right_logical(nv_ref[blk], grp_shift) - 1, 0)
        for g in range(MOE_GROUPS):
            src = group_row(blk, jnp.minimum(g, top))
            pltpu.make_async_copy(xs_hbm.at[pl.ds(src, SUBLANES)],
                                  xbuf.at[s, pl.ds(g * SUBLANES, SUBLANES)], gsem.at[s]).start()

    def wait_gather(s):
        pltpu.make_async_copy(xs_hbm.at[pl.ds(0, MOE_ROWS)], xbuf.at[s], gsem.at[s]).wait()

    def scatter(blk, s):
        def start(g):
            r = g * SUBLANES if isinstance(g, int) else pl.multiple_of(g * SUBLANES, SUBLANES)
            pltpu.make_async_copy(ybuf.at[s, pl.ds(r, SUBLANES)],
                                  ys_hbm.at[pl.ds(group_row(blk, g), SUBLANES)], ssem.at[s]).start()

        @pl.when(nv_ref[blk] == MOE_ROWS)
        def _():
            for g in range(MOE_GROUPS):
                start(g)

        @pl.when(nv_ref[blk] < MOE_ROWS)
        def _():
            def body(g, c):
                start(g)
                return c
            lax.fori_loop(0, lax.shift_right_logical(nv_ref[blk], grp_shift), body, 0)

    def wait_scatter(s, count):
        @pl.when(count == MOE_ROWS)
        def _():
            pltpu.make_async_copy(ybuf.at[s], ys_hbm.at[pl.ds(0, MOE_ROWS)], ssem.at[s]).wait()

        @pl.when(count < MOE_ROWS)
        def _():
            _pow2_pieces(count, lambda a, size: pltpu.make_async_copy(
                ybuf.at[s, pl.ds(0, size)], ys_hbm.at[pl.ds(0, size)], ssem.at[s]).wait())

    @pl.when(i == 0)
    def _():
        @pl.when(nv > 0)
        def _():
            gather(0, 0)
            gather(jnp.minimum(1, last), 1)

        ybuf[1] = jnp.zeros(ybuf.shape[1:], ybuf.dtype)
        n_tiles = used_ref.shape[0]

        def fill(t, c):
            row0 = t * TILE_ROWS + used_ref[t]
            _pow2_pieces(TILE_ROWS - used_ref[t], lambda a, size: pltpu.make_async_copy(
                ybuf.at[1, pl.ds(0, size)], ys_hbm.at[pl.ds(pl.multiple_of(row0 + a, SUBLANES), size)],
                ssem.at[1]).start())
            return c
        lax.fori_loop(0, n_tiles, fill, 0)

        def drain(t, c):
            wait_scatter(1, TILE_ROWS - used_ref[t])
            return c
        lax.fori_loop(0, n_tiles, drain, 0)

    @pl.when(i >= 2)
    def _():
        wait_scatter(slot, nv_ref[jnp.maximum(i - 2, 0)])

    xslot = i % GATHER_BUFS
    issuer_used = jnp.where(i >= 2, nv_ref[jnp.maximum(i - 2, 0)], nv_ref[0]) > 0

    @pl.when((nv == 0) & (i > 0) & issuer_used)
    def _():
        wait_gather(xslot)

    @pl.when(nv > 0)
    def _():
        e = be_ref[i]
        e_prev = be_ref[jnp.maximum(i - 1, 0)]

        def weight_copies(ex, ws):
            return [pltpu.make_async_copy(src.at[ex], dst.at[ws], wsem.at[ws])
                    for src, dst in ((w1_hbm, wst1), (w3_hbm, wst3), (w2_hbm, wst2))]

        @pl.when(i == 0)
        def _():
            wslot[0] = 0
            for cp in weight_copies(e, 0):
                cp.start()

        @pl.when((i == 0) | (e != e_prev))
        def _():
            ws = wslot[0]
            for cp in weight_copies(e, ws):
                cp.wait()
            wb1[...] = wst1[ws].astype(BF16)
            wb3[...] = wst3[ws].astype(BF16)
            wb2[...] = wst2[ws].astype(BF16)
            nb = nxt_ref[i]

            @pl.when(nb >= 0)
            def _():
                for cp in weight_copies(be_ref[jnp.maximum(nb, 0)], 1 - ws):
                    cp.start()
            wslot[0] = 1 - ws

        wait_gather(xslot)
        gather(jnp.minimum(i + 2, last), (i + 2) % GATHER_BUFS)
        u = xbuf[xslot]
        xa, xb = _unpack_bf16_pair(u[:, :half])
        wv = pltpu.bitcast(u[:, half:], F32)
        roww = wv[:, 0:1] + wv[:, 1:2] + wv[:, 2:3]
        a = _dot(xa, wb1[:half, :]) + _dot(xb, wb1[half:, :])
        b = _dot(xa, wb3[:half, :]) + _dot(xb, wb3[half:, :])
        hmid = (a * jax.nn.sigmoid(a) * b).astype(BF16)
        y = _dot(hmid, wb2[...]) * roww
        ybuf[slot] = _pack_bf16_pair(y[:, :half], y[:, half:])
        scatter(i, slot)

    @pl.when(i == last)
    def _():
        @pl.when((last >= 1) & (nv_ref[jnp.maximum(last - 1, 0)] > 0))
        def _():
            wait_gather((last + 1) % GATHER_BUFS)

        @pl.when(nv > 0)
        def _():
            wait_gather((last + 2) % GATHER_BUFS)

        @pl.when(last >= 1)
        def _():
            wait_scatter(1 - slot, nv_ref[jnp.maximum(last - 1, 0)])
        wait_scatter(slot, nv)


def _moe(xs, plan, w1, w3, w2):
    n_blk = plan[0].shape[0]
    rows = MOE_ROWS
    half = D_MODEL // 2
    hbm = pl.BlockSpec(memory_space=pl.ANY)
    grid_spec = pltpu.PrefetchScalarGridSpec(
        num_scalar_prefetch=5,
        grid=(n_blk,),
        in_specs=[hbm] * 4,
        out_specs=hbm,
        scratch_shapes=[pltpu.VMEM((GATHER_BUFS, rows, XS_WIDTH), jnp.uint32),
                        pltpu.VMEM((2, rows, half), jnp.uint32),
                        pltpu.VMEM((D_MODEL, EXPERT_HIDDEN), BF16),
                        pltpu.VMEM((D_MODEL, EXPERT_HIDDEN), BF16),
                        pltpu.VMEM((EXPERT_HIDDEN, D_MODEL), BF16),
                        pltpu.VMEM((2, D_MODEL, EXPERT_HIDDEN), F32),
                        pltpu.VMEM((2, D_MODEL, EXPERT_HIDDEN), F32),
                        pltpu.VMEM((2, EXPERT_HIDDEN, D_MODEL), F32),
                        pltpu.SMEM((1,), jnp.int32),
                        pltpu.SemaphoreType.DMA((GATHER_BUFS,)),
                        pltpu.SemaphoreType.DMA((2,)),
                        pltpu.SemaphoreType.DMA((2,))])
    return pl.pallas_call(
        _moe_kernel,
        grid_spec=grid_spec,
        out_shape=jax.ShapeDtypeStruct((xs.shape[0], half), jnp.uint32),
        compiler_params=pltpu.CompilerParams(dimension_semantics=("arbitrary",),
                                             vmem_limit_bytes=VMEM_LIMIT),
        name="moe_ffn",
    )(*plan, w1, w3, w2, xs)


def _combine_kernel(x1_ref, ys_ref, pos_ref, g2_ref, o_ref):
    tt = x1_ref.shape[0]
    half = D_MODEL // 2
    pos = pos_ref[...]
    pcol = lax.broadcasted_iota(jnp.int32, (tt, ys_ref.shape[0]), 1).astype(F32)
    sel = (jnp.where(pcol == pos[:, 0:1], 1.0, 0.0) + jnp.where(pcol == pos[:, 1:2], 1.0, 0.0))
    sel = sel.astype(BF16)
    lo, hi = _unpack_bf16_pair(ys_ref[...])
    g2 = g2_ref[0]
    x1 = x1_ref[...]
    o_ref[:, :half] = x1[:, :half] + g2[:, :half] * _dot(sel, lo)
    o_ref[:, half:] = x1[:, half:] + g2[:, half:] * _dot(sel, hi)


def _combine(x1, ys, pos, g2, S):
    T = x1.shape[0]
    tt = MOE_TILE
    per_b = S // tt
    return pl.pallas_call(
        _combine_kernel,
        grid=(T // tt,),
        in_specs=[pl.BlockSpec((tt, D_MODEL), lambda i: (i, 0)),
                  pl.BlockSpec((TILE_ROWS, D_MODEL // 2), lambda i: (i, 0)),
                  pl.BlockSpec((tt, LANES), lambda i: (i, 0)),
                  pl.BlockSpec((1, 1, D_MODEL), lambda i: (i // per_b, 0, 0))],
        out_specs=pl.BlockSpec((tt, D_MODEL), lambda i: (i, 0)),
        out_shape=jax.ShapeDtypeStruct((T, D_MODEL), F32),
        compiler_params=pltpu.CompilerParams(vmem_limit_bytes=VMEM_LIMIT),
        name="moe_combine",
    )(x1, ys, pos, g2)


def _prep_w_in(w_in):
    pad = jnp.zeros((D_MODEL, UNIT - FOX_HEADS), w_in.dtype)
    groups = [w_in[:, OFF_GATE_A:N_IN],
              w_in[:, OFF_FOX_Q:OFF_FOX_F],
              w_in[:, OFF_DIL_Q:OFF_DIL_K],
              w_in[:, OFF_DIL_K:OFF_DIL_V],
              w_in[:, OFF_DIL_V:OFF_GATE_A],
              jnp.concatenate([w_in[:, OFF_FOX_F:OFF_DIL_Q], pad], axis=1)]
    return [w.astype(BF16) for w in groups]


def _prep_gain(q_gain, k_gain):
    qs = HEAD_DIM ** -0.5 * LOG2E
    parts = [q_gain[:FOX_HEADS] * qs, k_gain[:FOX_HEADS], q_gain[FOX_HEADS:] * qs, k_gain[FOX_HEADS:]]
    return jnp.concatenate([p.reshape(-1) for p in parts]).reshape(1, -1)


def _layer(x, mod, rel_bias_table, norm1_g, w_in, b_forget, q_gain, k_gain, w_branch_a, w_branch_b,
           w_out, norm2_g, w_rg, b_rg, w_re, b_re, w1, w3, w2):
    B, S, D = x.shape
    T = B * S
    sh1, sc1, g1, sh2, sc2, g2 = [m.reshape(B, 1, D) for m in jnp.split(mod, 6, axis=-1)]
    x2d = x.reshape(T, D)

    p2d, fgt, slabs = _inproj(x2d, norm1_g.reshape(1, D), sc1, sh1, _prep_w_in(w_in),
                              _prep_gain(q_gain, k_gain), S)
    p3 = p2d.reshape(B, S, P_WIDTH)
    ck = _fcum(fgt.reshape(B, S, LANES), b_forget)
    ya = _fox(p3, ck)
    yd = _dil(slabs, _relbias(rel_bias_table))

    n_router = N_GROUPS + N_EXPERTS
    wr = jnp.concatenate([w_rg, w_re, jnp.zeros((D, LANES - n_router), F32)], axis=1).astype(BF16)
    br = jnp.concatenate([b_rg, b_re, jnp.zeros((LANES - n_router,), F32)]).reshape(1, LANES)
    x1, h2, logits = _outproj(x2d, ya.reshape(T, FOX_WIDTH), yd.reshape(T, DIL_OUT_WIDTH), p2d,
                              g1, sc2, sh2, norm2_g.reshape(1, D),
                              w_branch_a.astype(BF16), w_branch_b.astype(BF16), w_out.astype(BF16),
                              wr, br, S)
    xs, cnt, pos = _dispatch(logits, h2)
    cnt2 = cnt[:, :, 0]
    n_blk = cnt.shape[0] * TILE_ROWS // MOE_ROWS + N_EXPERTS
    plan = _plan(cnt2, n_blk)
    ys = _moe(xs, plan, w1, w3, w2)
    out = _combine(x1, ys, pos, g2, S)
    return out.reshape(B, S, D)


def kernel(x, c, rel_bias_table, w_ada, b_ada, norm1_g, w_in, b_forget, q_gain, k_gain, w_branch_a, w_branch_b, w_out, norm2_g, w_router_group, b_router_group, w_router_expert, b_router_expert, w1, w3, w2):
    depth = w_ada.shape[0]
    for l in range(depth):
        mod = _ada(c, w_ada[l], b_ada[l])
        x = _layer(x, mod, rel_bias_table, norm1_g[l], w_in[l], b_forget[l], q_gain[l], k_gain[l],
                   w_branch_a[l], w_branch_b[l], w_out[l], norm2_g[l], w_router_group[l],
                   b_router_group[l], w_router_expert[l], b_router_expert[l], w1[l], w3[l], w2[l])
    return x
```

```python
import functools
import math

import numpy as np
import jax
import jax.numpy as jnp
from jax import lax
from jax.experimental import pallas as pl
from jax.experimental.pallas import tpu as pltpu

F32 = jnp.float32
BF16 = jnp.bfloat16

D_MODEL = 1024
HEAD_DIM = 64
FOX_HEADS = 8
DIL_GROUPS = ((128, 1), (512, 4), (2048, 16))
DIL_HEADS_PER_GROUP = 4
N_DIL_GROUPS = len(DIL_GROUPS)
DIL_HEADS = N_DIL_GROUPS * DIL_HEADS_PER_GROUP
FOX_WIDTH = FOX_HEADS * HEAD_DIM
DIL_WIDTH = DIL_HEADS * HEAD_DIM
DIL_OUT_WIDTH = DIL_HEADS_PER_GROUP * HEAD_DIM
NUM_BUCKETS = 32
REL_MAX_DISTANCE = 2048
N_GROUPS = 4
EXPERTS_PER_GROUP = 8
N_EXPERTS = N_GROUPS * EXPERTS_PER_GROUP
EXPERT_HIDDEN = D_MODEL // 2
EPS = 1e-6
LOG2E = math.log2(math.e)

OFF_FOX_Q = 0
OFF_FOX_K = OFF_FOX_Q + FOX_WIDTH
OFF_FOX_V = OFF_FOX_K + FOX_WIDTH
OFF_FOX_F = OFF_FOX_V + FOX_WIDTH
OFF_DIL_Q = OFF_FOX_F + FOX_HEADS
OFF_DIL_K = OFF_DIL_Q + DIL_WIDTH
OFF_DIL_V = OFF_DIL_K + DIL_WIDTH
OFF_GATE_A = OFF_DIL_V + DIL_WIDTH
OFF_GATE_B = OFF_GATE_A + D_MODEL
N_IN = OFF_GATE_B + D_MODEL

LANES = 128
UNIT = 256
DIL_L = 128

U_GATE_A, U_GATE_B, U_FOX_Q, U_FOX_K, U_FOX_V, U_DIL, U_FORGET = 0, 4, 8, 10, 12, 14, 23
N_UNITS = 24
P_WIDTH = U_DIL * UNIT
N_SLABS = 3 * N_DIL_GROUPS
_KIND = (["gate"] * 8 + ["norm"] * 4 + ["plain"] * 2 + ["norm", "norm", "plain"] * 3 + ["forget"])

TM_INPROJ = 1024
TM_PROJ = 1024
TQ_FOX = 256
MOE_ROWS = 256
MOE_TILE = 512
XS_WIDTH = D_MODEL // 2 + LANES
SUBLANES = 8
TILE_ROWS = 2 * MOE_TILE + N_EXPERTS * SUBLANES
MOE_GROUPS = MOE_ROWS // SUBLANES
GATHER_BUFS = 3
VMEM_LIMIT = 56 * 1024 * 1024


def _dot(a, b):
    return jnp.dot(a, b, preferred_element_type=F32)


def _dot_nt(a, b):
    return lax.dot_general(a, b, (((1,), (1,)), ((), ())), preferred_element_type=F32)


def _split3(x):
    hi = x.astype(BF16)
    r1 = x - hi.astype(F32)
    mid = r1.astype(BF16)
    lo = (r1 - mid.astype(F32)).astype(BF16)
    return hi, mid, lo


def _ada_kernel(c_ref, w_ref, b_ref, o_ref):
    c = c_ref[...]
    s = c * jax.nn.sigmoid(c)
    s_hi = s.astype(BF16)
    s_lo = (s - s_hi.astype(F32)).astype(BF16)
    w = w_ref[...]
    w_hi = w.astype(BF16)
    w_lo = (w - w_hi.astype(F32)).astype(BF16)
    acc = _dot(s_hi, w_hi) + _dot(s_hi, w_lo) + _dot(s_lo, w_hi)
    o_ref[...] = acc + b_ref[...]


def _ada(c, w_ada, b_ada):
    B = c.shape[0]
    n_out = w_ada.shape[1]
    tn = 512
    return pl.pallas_call(
        _ada_kernel,
        grid=(n_out // tn,),
        in_specs=[pl.BlockSpec((B, D_MODEL), lambda j: (0, 0)),
                  pl.BlockSpec((D_MODEL, tn), lambda j: (0, j)),
                  pl.BlockSpec((1, tn), lambda j: (0, j))],
        out_specs=pl.BlockSpec((B, tn), lambda j: (0, j)),
        out_shape=jax.ShapeDtypeStruct((B, n_out), F32),
        name="ada_mod",
    )(c, w_ada, b_ada.reshape(1, n_out))


def _pack_bf16_pair(lo, hi):
    lo_bits = pltpu.bitcast(lo.astype(BF16).astype(F32), jnp.uint32) >> 16
    hi_bits = pltpu.bitcast(hi.astype(BF16).astype(F32), jnp.uint32) & jnp.uint32(0xFFFF0000)
    return lo_bits | hi_bits


def _unpack_bf16_pair(u):
    lo = pltpu.bitcast(u << 16, F32).astype(BF16)
    hi = pltpu.bitcast(u & jnp.uint32(0xFFFF0000), F32).astype(BF16)
    return lo, hi


def _inproj_kernel(x_ref, g_ref, sc_ref, sh_ref, wg_ref, wf_ref, wq_ref, wk_ref, wv_ref, wz_ref,
                   gain_ref, bd_ref, p_ref, f_ref, s_ref):
    x = x_ref[...]
    ms = jnp.mean(x * x, axis=-1, keepdims=True)
    h = x * lax.rsqrt(ms + EPS) * g_ref[...]
    h = h * (1.0 + sc_ref[0]) + sh_ref[0]
    hb = h.astype(BF16)

    def weights(u):
        if u < U_FOX_Q:
            return wg_ref, (u - U_GATE_A) * UNIT
        if u < U_DIL:
            return wf_ref, (u - U_FOX_Q) * UNIT
        if u < U_FORGET:
            g, j = divmod(u - U_DIL, 3)
            return (wq_ref, wk_ref, wv_ref)[j], g * UNIT
        return wz_ref, 0

    def gain_col(u):
        if u < U_DIL:
            return (u - U_FOX_Q) * UNIT
        g, j = divmod(u - U_DIL, 3)
        return 2 * FOX_WIDTH + j * DIL_WIDTH + g * UNIT

    def unit(u):
        w_ref, c0 = weights(u)
        acc = _dot(hb, w_ref[:, c0:c0 + UNIT])
        kind = _KIND[u]
        if kind == "gate":
            return jax.nn.sigmoid(acc)
        if kind == "norm":
            ss = _dot((acc * acc).astype(BF16), bd_ref[...])
            gc = gain_col(u)
            return acc * lax.rsqrt(ss * (1.0 / HEAD_DIM) + EPS) * gain_ref[:, gc:gc + UNIT]
        return acc

    dil_q = {}

    def emit(u):
        o = unit(u)
        if u < U_DIL:
            p_ref[:, u * UNIT:(u + 1) * UNIT] = o.astype(BF16)
        elif u < U_FORGET:
            g, j = divmod(u - U_DIL, 3)
            if j == 0:
                dil_q[g] = o
            elif j == 1:
                q = dil_q.pop(g)
                s_ref[0, 3 * g] = _pack_bf16_pair(q[:, :LANES], o[:, :LANES])
                s_ref[0, 3 * g + 1] = _pack_bf16_pair(q[:, LANES:], o[:, LANES:])
            else:
                s_ref[0, 3 * g + 2] = _pack_bf16_pair(o[:, :LANES], o[:, LANES:])
        else:
            f_ref[...] = o[:, :LANES]

    normed = [u for u in range(N_UNITS) if _KIND[u] == "norm"]
    others = [u for u in range(N_UNITS) if _KIND[u] != "norm"]
    while normed or others:
        for group in (normed, others):
            if group:
                emit(group.pop(0))


def _inproj(x2d, norm_g, sc, sh, w_re, gain_row, S):
    T = x2d.shape[0]
    tm = TM_INPROJ
    per_b = S // tm
    bd = np.kron(np.eye(UNIT // HEAD_DIM), np.ones((HEAD_DIM, HEAD_DIM))).astype(np.float32)
    once = dict(pipeline_mode=pl.Buffered(1))
    return pl.pallas_call(
        _inproj_kernel,
        grid=(T // tm,),
        in_specs=[pl.BlockSpec((tm, D_MODEL), lambda i: (i, 0)),
                  pl.BlockSpec((1, D_MODEL), lambda i: (0, 0)),
                  pl.BlockSpec((1, 1, D_MODEL), lambda i: (i // per_b, 0, 0)),
                  pl.BlockSpec((1, 1, D_MODEL), lambda i: (i // per_b, 0, 0)),
                  *[pl.BlockSpec(w.shape, lambda i: (0, 0), **once) for w in w_re],
                  pl.BlockSpec(gain_row.shape, lambda i: (0, 0), **once),
                  pl.BlockSpec((UNIT, UNIT), lambda i: (0, 0), **once)],
        out_specs=[pl.BlockSpec((tm, P_WIDTH), lambda i: (i, 0)),
                   pl.BlockSpec((tm, LANES), lambda i: (i, 0)),
                   pl.BlockSpec((1, N_SLABS, tm, LANES), lambda i: (i // per_b, 0, i % per_b, 0))],
        out_shape=[jax.ShapeDtypeStruct((T, P_WIDTH), BF16),
                   jax.ShapeDtypeStruct((T, LANES), F32),
                   jax.ShapeDtypeStruct((T // S, N_SLABS, S, LANES), jnp.uint32)],
        compiler_params=pltpu.CompilerParams(vmem_limit_bytes=VMEM_LIMIT),
        name="in_proj",
    )(x2d, norm_g, sc, sh, *w_re, gain_row, jnp.asarray(bd, BF16))


def _fcum_kernel(f_ref, b_ref, tri_ref, o_ref):
    S = f_ref.shape[1]
    xf = f_ref[0] + b_ref[...]
    ls = (jnp.minimum(xf, 0.0) - jnp.log(1.0 + jnp.exp(-jnp.abs(xf)))) * LOG2E
    lst = ls.T
    carry = jnp.zeros((LANES, UNIT), F32)
    for blk in range(S // UNIT):
        seg = lst[:, blk * UNIT:(blk + 1) * UNIT]
        hi, mid, lo = _split3(seg)
        tri = tri_ref[...]
        res = _dot(hi, tri) + _dot(mid, tri) + _dot(lo, tri)
        o_ref[0, :, blk * UNIT:(blk + 1) * UNIT] = (res[:, :UNIT] + carry)[:FOX_HEADS]
        carry = carry + res[:, UNIT:]


def _fcum(fgt, b_forget):
    B, S, _ = fgt.shape
    brow = jnp.zeros((1, LANES), F32).at[0, :FOX_HEADS].set(b_forget)
    tri = np.concatenate([np.triu(np.ones((UNIT, UNIT))), np.ones((UNIT, UNIT))], axis=1)
    return pl.pallas_call(
        _fcum_kernel,
        grid=(B,),
        in_specs=[pl.BlockSpec((1, S, LANES), lambda b: (b, 0, 0)),
                  pl.BlockSpec((1, LANES), lambda b: (0, 0)),
                  pl.BlockSpec((UNIT, 2 * UNIT), lambda b: (0, 0))],
        out_specs=pl.BlockSpec((1, FOX_HEADS, S), lambda b: (b, 0, 0)),
        out_shape=jax.ShapeDtypeStruct((B, FOX_HEADS, S), F32),
        name="forget_cumsum",
    )(fgt, brow, jnp.asarray(tri, BF16))


def _fox_kernel(q_ref, k_ref, v_ref, ck_ref, o_ref):
    S = q_ref.shape[1]
    pair = pl.program_id(1)
    tq = TQ_FOX
    lane = lax.broadcasted_iota(jnp.int32, (1, LANES), 1)
    row = lax.broadcasted_iota(jnp.int32, (tq, tq), 0)
    col = lax.broadcasted_iota(jnp.int32, (tq, tq), 1)
    causal = col <= row
    cks = [ck_ref[0, pl.ds(2 * pair + hh, 1), :] for hh in range(2)]
    for t in reversed(range(S // tq)):
        r0, r1 = t * tq, (t + 1) * tq
        qt = q_ref[0, r0:r1, :]
        zero = jnp.zeros_like(qt)
        q2 = jnp.concatenate([jnp.where(lane < HEAD_DIM, qt, zero),
                              jnp.where(lane >= HEAD_DIM, qt, zero)], axis=0)
        s2 = _dot_nt(q2, k_ref[0, :r1, :])
        halves = []
        for hh in range(2):
            s = s2[hh * tq:(hh + 1) * tq] - cks[hh][:, :r1]
            s_d = jnp.where(causal, s[:, r0:], -jnp.inf)
            halves.append(jnp.concatenate([s[:, :r0], s_d], axis=1) if t > 0 else s_d)
        s = jnp.concatenate(halves, axis=0)
        m = jnp.max(s, axis=-1, keepdims=True)
        p = jnp.exp2(s - m)
        l = jnp.sum(p, axis=-1, keepdims=True)
        o2 = _dot(p.astype(BF16), v_ref[0, :r1, :]) / l
        o_ref[0, r0:r1, :] = jnp.where(lane < HEAD_DIM, o2[:tq], o2[tq:]).astype(BF16)


def _fox(p3, ck):
    B, S, _ = p3.shape
    nq, nk, nv = (U_FOX_Q * UNIT // LANES, U_FOX_K * UNIT // LANES, U_FOX_V * UNIT // LANES)
    return pl.pallas_call(
        _fox_kernel,
        grid=(B, FOX_HEADS // 2),
        in_specs=[pl.BlockSpec((1, S, LANES), lambda b, p: (b, 0, nq + p)),
                  pl.BlockSpec((1, S, LANES), lambda b, p: (b, 0, nk + p)),
                  pl.BlockSpec((1, S, LANES), lambda b, p: (b, 0, nv + p)),
                  pl.BlockSpec((1, FOX_HEADS, S), lambda b, p: (b, 0, 0))],
        out_specs=pl.BlockSpec((1, S, LANES), lambda b, p: (b, 0, p)),
        out_shape=jax.ShapeDtypeStruct((B, S, FOX_WIDTH), BF16),
        compiler_params=pltpu.CompilerParams(vmem_limit_bytes=VMEM_LIMIT),
        name="fox_attn",
    )(p3, p3, p3, ck)


def _t5_bucket(dist):
    max_exact = NUM_BUCKETS // 2
    d = np.maximum(dist, 1).astype(np.float32)
    large = max_exact + (np.log(d / max_exact) / np.log(REL_MAX_DISTANCE / max_exact)
                         * (NUM_BUCKETS - max_exact)).astype(np.int32)
    large = np.minimum(large, NUM_BUCKETS - 1)
    return np.where(dist < max_exact, dist, large).astype(np.int32)


def _relbias_kernel(tab_ref, bucket_ref, valid_ref, o_ref):
    g = pl.program_id(0)
    bk = bucket_ref[0]
    vd = valid_ref[0]
    for hs in range(DIL_HEADS_PER_GROUP):
        acc = jnp.zeros(bk.shape, F32)
        for b in range(NUM_BUCKETS):
            acc = jnp.where(bk == b, tab_ref[b, g * DIL_HEADS_PER_GROUP + hs], acc)
        bias = jnp.where(vd != 0, acc * LOG2E, -jnp.inf)
        o_ref[0, hs] = bias
        col = lax.broadcasted_iota(jnp.int32, bias.shape, 1)
        o_ref[1, hs] = jnp.where(col >= DIL_L, bias, -jnp.inf)


def _relbias(table):
    L = DIL_L
    i = np.arange(L)[:, None]
    j = np.arange(2 * L)[None, :]
    m = L + i - j
    valid = ((m >= 0) & (m <= L)).astype(np.int32)
    buckets = np.stack([_t5_bucket(np.clip(m, 0, None) * d) for _, d in DIL_GROUPS])
    valids = np.stack([valid] * N_DIL_GROUPS)
    return pl.pallas_call(
        _relbias_kernel,
        grid=(N_DIL_GROUPS,),
        in_specs=[pl.BlockSpec(memory_space=pltpu.SMEM),
                  pl.BlockSpec((1, L, 2 * L), lambda g: (g, 0, 0)),
                  pl.BlockSpec((1, L, 2 * L), lambda g: (g, 0, 0))],
        out_specs=pl.BlockSpec((2, DIL_HEADS_PER_GROUP, L, 2 * L), lambda g: (0, g, 0, 0)),
        out_shape=jax.ShapeDtypeStruct((2, DIL_HEADS, L, 2 * L), F32),
        name="rel_bias",
    )(table, jnp.asarray(buckets), jnp.asarray(valids))


def _dil_rows(start, d):
    return pl.ds(start, DIL_L) if d == 1 else pl.ds(start, DIL_L, stride=d)


def _dil_block_rows(d, nb, it):
    r, n = it // nb, it % nb
    cur = _dil_rows(r + d * (n * DIL_L), d)
    prev = _dil_rows(r + d * (jnp.maximum(n - 1, 0) * DIL_L), d)
    return cur, prev, 1 - jnp.minimum(n, 1)


def _dil_scores(qkv_ref, bias_ref, s_scr, slot, g, d, nb, it):
    lane = lax.broadcasted_iota(jnp.int32, (1, LANES), 1)
    cur, prev, first = _dil_block_rows(d, nb, it)
    for pr in range(2):
        qt, kt = _unpack_bf16_pair(qkv_ref[0, 3 * g + pr, cur, :])
        if nb > 1:
            _, k_prev = _unpack_bf16_pair(qkv_ref[0, 3 * g + pr, prev, :])
            kt = jnp.concatenate([k_prev, kt], axis=0)
        for hh in range(2):
            hsel = (lane >= HEAD_DIM) == bool(hh)
            qm = jnp.where(hsel, qt, jnp.zeros_like(qt))
            head = 2 * pr + hh
            if nb > 1:
                s_scr[slot, head] = _dot_nt(qm, kt) + bias_ref[first, DIL_HEADS_PER_GROUP * g + head]
            else:
                bias = bias_ref[0, DIL_HEADS_PER_GROUP * g + head, :, DIL_L:]
                s_scr[slot, head, :, :DIL_L] = _dot_nt(qm, kt) + bias


def _dil_merge(qkv_ref, s_scr, slot, m_scr, l_scr, acc_scr, g, d, nb, init, it):
    lane = lax.broadcasted_iota(jnp.int32, (1, LANES), 1)
    cur, prev, _ = _dil_block_rows(d, nb, it)
    v_cur = _unpack_bf16_pair(qkv_ref[0, 3 * g + 2, cur, :])
    if nb > 1:
        v_prev = _unpack_bf16_pair(qkv_ref[0, 3 * g + 2, prev, :])
    for pr in range(2):
        vt = jnp.concatenate([v_prev[pr], v_cur[pr]], axis=0) if nb > 1 else v_cur[pr]
        ms, ls, accs = [], [], []
        for hh in range(2):
            s = s_scr[slot, 2 * pr + hh] if nb > 1 else s_scr[slot, 2 * pr + hh, :, :DIL_L]
            m = jnp.max(s, axis=-1, keepdims=True)
            p = jnp.exp2(s - m)
            ms.append(m)
            ls.append(jnp.sum(p, axis=-1, keepdims=True))
            accs.append(_dot(p.astype(BF16), vt))
        low = lane < HEAD_DIM
        m_b = jnp.where(low, ms[0], ms[1])
        l_b = jnp.where(low, ls[0], ls[1])
        acc_b = jnp.where(low, accs[0], accs[1])
        if init:
            m_scr[pr, cur, :] = m_b
            l_scr[pr, cur, :] = l_b
            acc_scr[pr, cur, :] = acc_b
        else:
            m_o = m_scr[pr, cur, :]
            m_n = jnp.maximum(m_o, m_b)
            a_o = jnp.exp2(m_o - m_n)
            a_b = jnp.exp2(m_b - m_n)
            m_scr[pr, cur, :] = m_n
            l_scr[pr, cur, :] = l_scr[pr, cur, :] * a_o + l_b * a_b
            acc_scr[pr, cur, :] = acc_scr[pr, cur, :] * a_o + acc_b * a_b


def _dil_kernel(qkv_ref, bias_ref, o_ref, m_scr, l_scr, acc_scr, s_scr):
    S = o_ref.shape[1]
    order = sorted(range(N_DIL_GROUPS), key=lambda g: -DIL_GROUPS[g][1])
    for g in order:
        window, d = DIL_GROUPS[g]
        nb = S // window
        total = d * nb
        assert total % 2 == 0
        scores = functools.partial(_dil_scores, qkv_ref, bias_ref, s_scr, g=g, d=d, nb=nb)
        merge = functools.partial(_dil_merge, qkv_ref, s_scr, m_scr=m_scr, l_scr=l_scr,
                                  acc_scr=acc_scr, g=g, d=d, nb=nb, init=g == order[0])

        scores(slot=0, it=0)

        def body(j, carry, scores=scores, merge=merge, total=total):
            scores(slot=1, it=2 * j + 1)
            merge(slot=0, it=2 * j)
            scores(slot=0, it=jnp.minimum(2 * j + 2, total - 1))
            merge(slot=1, it=2 * j + 1)
            return carry
        lax.fori_loop(0, total // 2, body, 0, unroll=2)
    for pr in range(2):
        o_ref[0, :, pr * LANES:(pr + 1) * LANES] = (acc_scr[pr] / l_scr[pr]).astype(BF16)


def _dil(slabs, bias):
    B, _, S, _ = slabs.shape
    for window, d in DIL_GROUPS:
        assert window // d == DIL_L and S % window == 0
    stat = pltpu.VMEM((2, S, LANES), F32)
    return pl.pallas_call(
        _dil_kernel,
        grid=(B,),
        in_specs=[pl.BlockSpec((1, N_SLABS, S, LANES), lambda b: (b, 0, 0, 0)),
                  pl.BlockSpec(bias.shape, lambda b: (0, 0, 0, 0))],
        out_specs=pl.BlockSpec((1, S, DIL_OUT_WIDTH), lambda b: (b, 0, 0)),
        out_shape=jax.ShapeDtypeStruct((B, S, DIL_OUT_WIDTH), BF16),
        scratch_shapes=[stat, stat, stat,
                        pltpu.VMEM((2, DIL_HEADS_PER_GROUP, DIL_L, 2 * DIL_L), F32)],
        compiler_params=pltpu.CompilerParams(vmem_limit_bytes=VMEM_LIMIT),
        name="dil_attn",
    )(slabs, bias)


def _outproj_kernel(x_ref, ya_ref, yd_ref, ga_ref, gb_ref,
                    g1_ref, sc_ref, sh_ref, ng_ref, wa_ref, wb_ref, wo_ref, wr_ref, br_ref,
                    x1_ref, h2_ref, lg_ref):
    n_chunks = 2
    cm = x_ref.shape[0] // n_chunks
    for c in range(n_chunks):
        rows = slice(c * cm, (c + 1) * cm)
        a = _dot(ya_ref[rows, :], wa_ref[...])
        bm = _dot(yd_ref[rows, :], wb_ref[...])
        merged = ga_ref[rows, :].astype(F32) * a + gb_ref[rows, :].astype(F32) * bm
        out = _dot(merged.astype(BF16), wo_ref[...])
        x1 = x_ref[rows, :] + g1_ref[0] * out
        x1_ref[rows, :] = x1
        ms = jnp.mean(x1 * x1, axis=-1, keepdims=True)
        h = x1 * lax.rsqrt(ms + EPS) * ng_ref[...]
        h = h * (1.0 + sc_ref[0]) + sh_ref[0]
        hb = h.astype(BF16)
        h2_ref[rows, :] = hb
        lg_ref[rows, :] = _dot(hb, wr_ref[...]) + br_ref[...]


def _outproj(x2d, ya2d, yd2d, p2d, g1, sc2, sh2, norm_g, wa, wb, wo, wr, br, S):
    T = x2d.shape[0]
    tm = TM_PROJ
    per_b = S // tm
    row = lambda w: pl.BlockSpec((tm, w), lambda i: (i, 0))
    full = lambda a: pl.BlockSpec(a.shape, lambda i: (0,) * a.ndim)
    mod = pl.BlockSpec((1, 1, D_MODEL), lambda i: (i // per_b, 0, 0))
    return pl.pallas_call(
        _outproj_kernel,
        grid=(T // tm,),
        in_specs=[row(D_MODEL), row(FOX_WIDTH), row(DIL_OUT_WIDTH)]
                 + [pl.BlockSpec((tm, D_MODEL), lambda i: (i, U_GATE_A * UNIT // D_MODEL)),
                    pl.BlockSpec((tm, D_MODEL), lambda i: (i, U_GATE_B * UNIT // D_MODEL)),
                    mod, mod, mod, full(norm_g), full(wa), full(wb), full(wo), full(wr), full(br)],
        out_specs=[row(D_MODEL), row(D_MODEL), row(LANES)],
        out_shape=[jax.ShapeDtypeStruct((T, D_MODEL), F32),
                   jax.ShapeDtypeStruct((T, D_MODEL), BF16),
                   jax.ShapeDtypeStruct((T, LANES), F32)],
        compiler_params=pltpu.CompilerParams(vmem_limit_bytes=VMEM_LIMIT),
        name="out_proj",
    )(x2d, ya2d, yd2d, p2d, p2d, g1, sc2, sh2, norm_g, wa, wb, wo, wr, br)


def _dispatch_kernel(lg_ref, h_ref, tri_ref, xs_ref, cnt_ref, pos_ref):
    tt = lg_ref.shape[0]
    lt = lg_ref[...].T
    row = lambda i: lt[i:i + 1, :]
    neg = -jnp.inf
    g = [row(i) for i in range(N_GROUPS)]
    gmax = functools.reduce(jnp.maximum, g)
    gidx = jnp.full(gmax.shape, N_GROUPS - 1, jnp.int32)
    for i in reversed(range(N_GROUPS - 1)):
        gidx = jnp.where(g[i] == gmax, i, gidx)
    gsum = sum(jnp.exp(gi - gmax) for gi in g)
    el = []
    for j in range(EXPERTS_PER_GROUP):
        v = row(N_GROUPS + EXPERTS_PER_GROUP * (N_GROUPS - 1) + j)
        for gg in reversed(range(N_GROUPS - 1)):
            v = jnp.where(gidx == gg, row(N_GROUPS + EXPERTS_PER_GROUP * gg + j), v)
        el.append(v)

    def top(vals):
        best = functools.reduce(jnp.maximum, vals)
        idx = jnp.full(best.shape, EXPERTS_PER_GROUP - 1, jnp.int32)
        for j in reversed(range(EXPERTS_PER_GROUP - 1)):
            idx = jnp.where(vals[j] == best, j, idx)
        return best, idx

    v1, i1 = top(el)
    v2, i2 = top([jnp.where(i1 == j, neg, el[j]) for j in range(EXPERTS_PER_GROUP)])
    t = jnp.exp(v2 - v1)
    den = (1.0 + t) * gsum
    wts = [1.0 / den, t / den]
    eid = [gidx * EXPERTS_PER_GROUP + i1, gidx * EXPERTS_PER_GROUP + i2]

    esub = lax.broadcasted_iota(jnp.int32, (N_EXPERTS, tt), 0)
    ohf = jnp.concatenate([jnp.where(esub == eid[k], 1.0, 0.0) for k in range(2)], axis=1)
    n_pb = 2 * tt // UNIT
    oh_blocks = jnp.concatenate([ohf[:, b * UNIT:(b + 1) * UNIT] for b in range(n_pb)], axis=0)
    res = _dot(oh_blocks.astype(BF16), tri_ref[...])
    cnt = jnp.zeros((N_EXPERTS, LANES), F32)
    pre = []
    for b in range(n_pb):
        r = res[b * N_EXPERTS:(b + 1) * N_EXPERTS]
        pre.append(r[:, :UNIT] + jnp.concatenate([cnt] * (UNIT // LANES), axis=1))
        cnt = cnt + r[:, UNIT:]
    prefix = jnp.concatenate(pre, axis=1)
    cnt = (((cnt.astype(jnp.int32) + (SUBLANES - 1)) // SUBLANES) * SUBLANES).astype(F32)
    esub_c = lax.broadcasted_iota(jnp.int32, cnt.shape, 0)
    start = jnp.zeros_like(cnt)
    for e in range(N_EXPERTS - 1):
        start = start + jnp.where(esub_c > e, cnt[e:e + 1, :], 0.0)
    start_w = jnp.concatenate([start] * (2 * tt // LANES), axis=1)
    pos = jnp.sum(ohf * (start_w + prefix), axis=0, keepdims=True)
    pos_k = [pos[:, :tt], pos[:, tt:]]

    n_rows = xs_ref.shape[0]
    psub = lax.broadcasted_iota(jnp.int32, (n_rows, tt), 0).astype(F32)
    pm = [jnp.where(psub == pos_k[k], 1.0, 0.0).astype(BF16) for k in range(2)]
    xs = _dot(pm[0] + pm[1], h_ref[...])
    wsub = lax.broadcasted_iota(jnp.int32, (LANES, tt), 0)
    ws = jnp.zeros((n_rows, LANES), F32)
    for k in range(2):
        parts = _split3(wts[k])
        wrows = jnp.zeros((LANES, tt), F32)
        for j in range(3):
            wrows = jnp.where(wsub == j, parts[j].astype(F32), wrows)
        ws = ws + _dot_nt(pm[k], wrows.astype(BF16))
    half = D_MODEL // 2
    xs_ref[:, :half] = _pack_bf16_pair(xs[:, :half], xs[:, half:])
    xs_ref[:, half:] = pltpu.bitcast(ws, jnp.uint32)
    cnt_ref[0] = cnt.astype(jnp.int32)
    posr = jnp.where(wsub == 0, pos_k[0], jnp.where(wsub == 1, pos_k[1], 0.0))
    pos_ref[...] = posr.T


def _dispatch(logits, h2):
    T = logits.shape[0]
    tt = MOE_TILE
    n_tiles = T // tt
    tri = np.concatenate([np.triu(np.ones((UNIT, UNIT)), 1), np.ones((UNIT, LANES))], axis=1)
    return pl.pallas_call(
        _dispatch_kernel,
        grid=(n_tiles,),
        in_specs=[pl.BlockSpec((tt, LANES), lambda i: (i, 0)),
                  pl.BlockSpec((tt, D_MODEL), lambda i: (i, 0)),
                  pl.BlockSpec(tri.shape, lambda i: (0, 0))],
        out_specs=[pl.BlockSpec((TILE_ROWS, XS_WIDTH), lambda i: (i, 0)),
                   pl.BlockSpec((1, N_EXPERTS, LANES), lambda i: (i, 0, 0)),
                   pl.BlockSpec((tt, LANES), lambda i: (i, 0))],
        out_shape=[jax.ShapeDtypeStruct((n_tiles * TILE_ROWS, XS_WIDTH), jnp.uint32),
                   jax.ShapeDtypeStruct((n_tiles, N_EXPERTS, LANES), jnp.int32),
                   jax.ShapeDtypeStruct((T, LANES), F32)],
        compiler_params=pltpu.CompilerParams(vmem_limit_bytes=VMEM_LIMIT),
        name="moe_dispatch",
    )(logits, h2, jnp.asarray(tri, BF16))


def _plan_kernel(cnt_ref, be_ref, nv_ref, nxt_ref, grp_ref, used_ref, cs_ref):
    n_tiles = cnt_ref.shape[0]
    n_blk = be_ref.shape[0]
    rows = MOE_ROWS
    row_shift = rows.bit_length() - 1
    grp_shift = SUBLANES.bit_length() - 1
    assert rows == 1 << row_shift and SUBLANES == 1 << grp_shift

    def tile_starts(t, c):
        def per_e(e, acc):
            cs_ref[t * N_EXPERTS + e] = acc
            return acc + cnt_ref[t, e]
        used_ref[t] = lax.fori_loop(0, N_EXPERTS, per_e, 0, unroll=8)
        return c
    lax.fori_loop(0, n_tiles, tile_starts, 0)

    def clear(b, c):
        nv_ref[b] = 0
        return c
    lax.fori_loop(0, n_blk, clear, 0)

    def clear_groups(g, c):
        grp_ref[g] = 0
        return c
    lax.fori_loop(0, n_blk * MOE_GROUPS, clear_groups, 0, unroll=8)

    def per_expert(e, b):
        g0 = b * MOE_GROUPS

        def per_tile(t, tot):
            c = cnt_ref[t, e]
            src = t * TILE_ROWS + cs_ref[t * N_EXPERTS + e]
            first = g0 + lax.shift_right_logical(tot, grp_shift)

            def per_group(k, cc):
                grp_ref[first + k] = src + k * SUBLANES
                return cc
            lax.fori_loop(0, lax.shift_right_logical(c, grp_shift), per_group, 0)
            return tot + c
        tot = lax.fori_loop(0, n_tiles, per_tile, 0)

        def per_block(j, c):
            be_ref[b + j] = e
            nv_ref[b + j] = jnp.minimum(rows, tot - j * rows)
            return c
        nb = lax.shift_right_logical(tot + rows - 1, row_shift)
        lax.fori_loop(0, nb, per_block, 0)
        return b + nb
    n_used = lax.fori_loop(0, N_EXPERTS, per_expert, 0)

    def unused(b, c):
        be_ref[b] = be_ref[n_used - 1]
        nxt_ref[b] = -1
        return c
    lax.fori_loop(n_used, n_blk, unused, 0)

    def next_run(k, nf):
        b = n_used - 1 - k
        nf = jnp.where(be_ref[b] != be_ref[jnp.minimum(b + 1, n_used - 1)], b + 1, nf)
        nxt_ref[b] = nf
        return nf
    lax.fori_loop(0, n_used, next_run, -1)


def _plan(cnt, n_blk):
    n_tiles = cnt.shape[0]
    smem = pl.BlockSpec(memory_space=pltpu.SMEM)
    i32 = lambda n: jax.ShapeDtypeStruct((n,), jnp.int32)
    return pl.pallas_call(
        _plan_kernel,
        in_specs=[smem],
        out_specs=[smem] * 5,
        out_shape=[i32(n_blk), i32(n_blk), i32(n_blk), i32(n_blk * MOE_GROUPS), i32(n_tiles)],
        scratch_shapes=[pltpu.SMEM((n_tiles * N_EXPERTS,), jnp.int32)],
        name="moe_plan",
    )(cnt)


def _pow2_pieces(n, fn):
    for b in reversed(range(SUBLANES.bit_length() - 1, MOE_ROWS.bit_length())):
        size = 1 << b

        @pl.when((n & size) != 0)
        def _():
            fn((n >> (b + 1)) << (b + 1), size)


def _moe_kernel(be_ref, nv_ref, nxt_ref, grp_ref, used_ref,
                w1_hbm, w3_hbm, w2_hbm, xs_hbm, ys_hbm,
                xbuf, ybuf, wb1, wb3, wb2, wst1, wst3, wst2, wslot, gsem, ssem, wsem):
    i = pl.program_id(0)
    last = pl.num_programs(0) - 1
    slot = i % 2
    nv = nv_ref[i]
    half = D_MODEL // 2
    grp_shift = SUBLANES.bit_length() - 1

    def group_row(blk, g):
        return pl.multiple_of(grp_ref[blk * MOE_GROUPS + g], SUBLANES)

    def gather(blk, s):
        top = jnp.maximum(lax.shift_right_logical(nv_ref[blk], grp_shift) - 1, 0)
        for g in range(MOE_GROUPS):
            src = group_row(blk, jnp.minimum(g, top))
            pltpu.make_async_copy(xs_hbm.at[pl.ds(src, SUBLANES)],
                                  xbuf.at[s, pl.ds(g * SUBLANES, SUBLANES)], gsem.at[s]).start()

    def wait_gather(s):
        pltpu.make_async_copy(xs_hbm.at[pl.ds(0, MOE_ROWS)], xbuf.at[s], gsem.at[s]).wait()

    def scatter(blk, s):
        def start(g):
            r = g * SUBLANES if isinstance(g, int) else pl.multiple_of(g * SUBLANES, SUBLANES)
            pltpu.make_async_copy(ybuf.at[s, pl.ds(r, SUBLANES)],
                                  ys_hbm.at[pl.ds(group_row(blk, g), SUBLANES)], ssem.at[s]).start()

        @pl.when(nv_ref[blk] == MOE_ROWS)
        def _():
            for g in range(MOE_GROUPS):
                start(g)

        @pl.when(nv_ref[blk] < MOE_ROWS)
        def _():
            def body(g, c):
                start(g)
                return c
            lax.fori_loop(0, lax.shift_right_logical(nv_ref[blk], grp_shift), body, 0)

    def wait_scatter(s, count):
        @pl.when(count == MOE_ROWS)
        def _():
            pltpu.make_async_copy(ybuf.at[s], ys_hbm.at[pl.ds(0, MOE_ROWS)], ssem.at[s]).wait()

        @pl.when(count < MOE_ROWS)
        def _():
            _pow2_pieces(count, lambda a, size: pltpu.make_async_copy(
                ybuf.at[s, pl.ds(0, size)], ys_hbm.at[pl.ds(0, size)], ssem.at[s]).wait())

    @pl.when(i == 0)
    def _():
        @pl.when(nv > 0)
        def _():
            gather(0, 0)
            gather(jnp.minimum(1, last), 1)

        ybuf[1] = jnp.zeros(ybuf.shape[1:], ybuf.dtype)
        n_tiles = used_ref.shape[0]

        def fill(t, c):
            row0 = t * TILE_ROWS + used_ref[t]
            _pow2_pieces(TILE_ROWS - used_ref[t], lambda a, size: pltpu.make_async_copy(
                ybuf.at[1, pl.ds(0, size)], ys_hbm.at[pl.ds(pl.multiple_of(row0 + a, SUBLANES), size)],
                ssem.at[1]).start())
            return c
        lax.fori_loop(0, n_tiles, fill, 0)

        def drain(t, c):
            wait_scatter(1, TILE_ROWS - used_ref[t])
            return c
        lax.fori_loop(0, n_tiles, drain, 0)

    @pl.when(i >= 2)
    def _():
        wait_scatter(slot, nv_ref[jnp.maximum(i - 2, 0)])

    xslot = i % GATHER_BUFS
    issuer_used = jnp.where(i >= 2, nv_ref[jnp.maximum(i - 2, 0)], nv_ref[0]) > 0

    @pl.when((nv == 0) & (i > 0) & issuer_used)
    def _():
        wait_gather(xslot)

    @pl.when(nv > 0)
    def _():
        e = be_ref[i]
        e_prev = be_ref[jnp.maximum(i - 1, 0)]

        def weight_copies(ex, ws):
            return [pltpu.make_async_copy(src.at[ex], dst.at[ws], wsem.at[ws])
                    for src, dst in ((w1_hbm, wst1), (w3_hbm, wst3), (w2_hbm, wst2))]

        @pl.when(i == 0)
        def _():
            wslot[0] = 0
            for cp in weight_copies(e, 0):
                cp.start()

        @pl.when((i == 0) | (e != e_prev))
        def _():
            ws = wslot[0]
            for cp in weight_copies(e, ws):
                cp.wait()
            wb1[...] = wst1[ws].astype(BF16)
            wb3[...] = wst3[ws].astype(BF16)
            wb2[...] = wst2[ws].astype(BF16)
            nb = nxt_ref[i]

            @pl.when(nb >= 0)
            def _():
                for cp in weight_copies(be_ref[jnp.maximum(nb, 0)], 1 - ws):
                    cp.start()
            wslot[0] = 1 - ws

        wait_gather(xslot)
        gather(jnp.minimum(i + 2, last), (i + 2) % GATHER_BUFS)
        u = xbuf[xslot]
        xa, xb = _unpack_bf16_pair(u[:, :half])
        wv = pltpu.bitcast(u[:, half:], F32)
        roww = wv[:, 0:1] + wv[:, 1:2] + wv[:, 2:3]
        a = _dot(xa, wb1[:half, :]) + _dot(xb, wb1[half:, :])
        b = _dot(xa, wb3[:half, :]) + _dot(xb, wb3[half:, :])
        hmid = (a * jax.nn.sigmoid(a) * b).astype(BF16)
        y = _dot(hmid, wb2[...]) * roww
        ybuf[slot] = _pack_bf16_pair(y[:, :half], y[:, half:])
        scatter(i, slot)

    @pl.when(i == last)
    def _():
        @pl.when((last >= 1) & (nv_ref[jnp.maximum(last - 1, 0)] > 0))
        def _():
            wait_gather((last + 1) % GATHER_BUFS)

        @pl.when(nv > 0)
        def _():
            wait_gather((last + 2) % GATHER_BUFS)

        @pl.when(last >= 1)
        def _():
            wait_scatter(1 - slot, nv_ref[jnp.maximum(last - 1, 0)])
        wait_scatter(slot, nv)


def _moe(xs, plan, w1, w3, w2):
    n_blk = plan[0].shape[0]
    rows = MOE_ROWS
    half = D_MODEL // 2
    hbm = pl.BlockSpec(memory_space=pl.ANY)
    grid_spec = pltpu.PrefetchScalarGridSpec(
        num_scalar_prefetch=5,
        grid=(n_blk,),
        in_specs=[hbm] * 4,
        out_specs=hbm,
        scratch_shapes=[pltpu.VMEM((GATHER_BUFS, rows, XS_WIDTH), jnp.uint32),
                        pltpu.VMEM((2, rows, half), jnp.uint32),
                        pltpu.VMEM((D_MODEL, EXPERT_HIDDEN), BF16),
                        pltpu.VMEM((D_MODEL, EXPERT_HIDDEN), BF16),
                        pltpu.VMEM((EXPERT_HIDDEN, D_MODEL), BF16),
                        pltpu.VMEM((2, D_MODEL, EXPERT_HIDDEN), F32),
                        pltpu.VMEM((2, D_MODEL, EXPERT_HIDDEN), F32),
                        pltpu.VMEM((2, EXPERT_HIDDEN, D_MODEL), F32),
                        pltpu.SMEM((1,), jnp.int32),
                        pltpu.SemaphoreType.DMA((GATHER_BUFS,)),
                        pltpu.SemaphoreType.DMA((2,)),
                        pltpu.SemaphoreType.DMA((2,))])
    return pl.pallas_call(
        _moe_kernel,
        grid_spec=grid_spec,
        out_shape=jax.ShapeDtypeStruct((xs.shape[0], half), jnp.uint32),
        compiler_params=pltpu.CompilerParams(dimension_semantics=("arbitrary",),
                                             vmem_limit_bytes=VMEM_LIMIT),
        name="moe_ffn",
    )(*plan, w1, w3, w2, xs)


def _combine_kernel(x1_ref, ys_ref, pos_ref, g2_ref, o_ref):
    tt = x1_ref.shape[0]
    half = D_MODEL // 2
    pos = pos_ref[...]
    pcol = lax.broadcasted_iota(jnp.int32, (tt, ys_ref.shape[0]), 1).astype(F32)
    sel = (jnp.where(pcol == pos[:, 0:1], 1.0, 0.0) + jnp.where(pcol == pos[:, 1:2], 1.0, 0.0))
    sel = sel.astype(BF16)
    lo, hi = _unpack_bf16_pair(ys_ref[...])
    g2 = g2_ref[0]
    x1 = x1_ref[...]
    o_ref[:, :half] = x1[:, :half] + g2[:, :half] * _dot(sel, lo)
    o_ref[:, half:] = x1[:, half:] + g2[:, half:] * _dot(sel, hi)


def _combine(x1, ys, pos, g2, S):
    T = x1.shape[0]
    tt = MOE_TILE
    per_b = S // tt
    return pl.pallas_call(
        _combine_kernel,
        grid=(T // tt,),
        in_specs=[pl.BlockSpec((tt, D_MODEL), lambda i: (i, 0)),
                  pl.BlockSpec((TILE_ROWS, D_MODEL // 2), lambda i: (i, 0)),
                  pl.BlockSpec((tt, LANES), lambda i: (i, 0)),
                  pl.BlockSpec((1, 1, D_MODEL), lambda i: (i // per_b, 0, 0))],
        out_specs=pl.BlockSpec((tt, D_MODEL), lambda i: (i, 0)),
        out_shape=jax.ShapeDtypeStruct((T, D_MODEL), F32),
        compiler_params=pltpu.CompilerParams(vmem_limit_bytes=VMEM_LIMIT),
        name="moe_combine",
    )(x1, ys, pos, g2)


def _prep_w_in(w_in):
    pad = jnp.zeros((D_MODEL, UNIT - FOX_HEADS), w_in.dtype)
    groups = [w_in[:, OFF_GATE_A:N_IN],
              w_in[:, OFF_FOX_Q:OFF_FOX_F],
              w_in[:, OFF_DIL_Q:OFF_DIL_K],
              w_in[:, OFF_DIL_K:OFF_DIL_V],
              w_in[:, OFF_DIL_V:OFF_GATE_A],
              jnp.concatenate([w_in[:, OFF_FOX_F:OFF_DIL_Q], pad], axis=1)]
    return [w.astype(BF16) for w in groups]


def _prep_gain(q_gain, k_gain):
    qs = HEAD_DIM ** -0.5 * LOG2E
    parts = [q_gain[:FOX_HEADS] * qs, k_gain[:FOX_HEADS], q_gain[FOX_HEADS:] * qs, k_gain[FOX_HEADS:]]
    return jnp.concatenate([p.reshape(-1) for p in parts]).reshape(1, -1)


def _layer(x, mod, rel_bias_table, norm1_g, w_in, b_forget, q_gain, k_gain, w_branch_a, w_branch_b,
           w_out, norm2_g, w_rg, b_rg, w_re, b_re, w1, w3, w2):
    B, S, D = x.shape
    T = B * S
    sh1, sc1, g1, sh2, sc2, g2 = [m.reshape(B, 1, D) for m in jnp.split(mod, 6, axis=-1)]
    x2d = x.reshape(T, D)

    p2d, fgt, slabs = _inproj(x2d, norm1_g.reshape(1, D), sc1, sh1, _prep_w_in(w_in),
                              _prep_gain(q_gain, k_gain), S)
    p3 = p2d.reshape(B, S, P_WIDTH)
    ck = _fcum(fgt.reshape(B, S, LANES), b_forget)
    ya = _fox(p3, ck)
    yd = _dil(slabs, _relbias(rel_bias_table))

    n_router = N_GROUPS + N_EXPERTS
    wr = jnp.concatenate([w_rg, w_re, jnp.zeros((D, LANES - n_router), F32)], axis=1).astype(BF16)
    br = jnp.concatenate([b_rg, b_re, jnp.zeros((LANES - n_router,), F32)]).reshape(1, LANES)
    x1, h2, logits = _outproj(x2d, ya.reshape(T, FOX_WIDTH), yd.reshape(T, DIL_OUT_WIDTH), p2d,
                              g1, sc2, sh2, norm2_g.reshape(1, D),
                              w_branch_a.astype(BF16), w_branch_b.astype(BF16), w_out.astype(BF16),
                              wr, br, S)
    xs, cnt, pos = _dispatch(logits, h2)
    cnt2 = cnt[:, :, 0]
    n_blk = cnt.shape[0] * TILE_ROWS // MOE_ROWS + N_EXPERTS
    plan = _plan(cnt2, n_blk)
    ys = _moe(xs, plan, w1, w3, w2)
    out = _combine(x1, ys, pos, g2, S)
    return out.reshape(B, S, D)


def kernel(x, c, rel_bias_table, w_ada, b_ada, norm1_g, w_in, b_forget, q_gain, k_gain, w_branch_a, w_branch_b, w_out, norm2_g, w_router_group, b_router_group, w_router_expert, b_router_expert, w1, w3, w2):
    depth = w_ada.shape[0]
    for l in range(depth):
        mod = _ada(c, w_ada[l], b_ada[l])
        x = _layer(x, mod, rel_bias_table, norm1_g[l], w_in[l], b_forget[l], q_gain[l], k_gain[l],
                   w_branch_a[l], w_branch_b[l], w_out[l], norm2_g[l], w_router_group[l],
                   b_router_group[l], w_router_expert[l], b_router_expert[l], w1[l], w3[l], w2[l])
    return x
```

```python
import functools
import math

import numpy as np
import jax
import jax.numpy as jnp
from jax import lax
from jax.experimental import pallas as pl
from jax.experimental.pallas import tpu as pltpu

F32 = jnp.float32
BF16 = jnp.bfloat16

D_MODEL = 1024
HEAD_DIM = 64
FOX_HEADS = 8
DIL_GROUPS = ((128, 1), (512, 4), (2048, 16))
DIL_HEADS_PER_GROUP = 4
N_DIL_GROUPS = len(DIL_GROUPS)
DIL_HEADS = N_DIL_GROUPS * DIL_HEADS_PER_GROUP
FOX_WIDTH = FOX_HEADS * HEAD_DIM
DIL_WIDTH = DIL_HEADS * HEAD_DIM
DIL_OUT_WIDTH = DIL_HEADS_PER_GROUP * HEAD_DIM
NUM_BUCKETS = 32
REL_MAX_DISTANCE = 2048
N_GROUPS = 4
EXPERTS_PER_GROUP = 8
N_EXPERTS = N_GROUPS * EXPERTS_PER_GROUP
EXPERT_HIDDEN = D_MODEL // 2
EPS = 1e-6
LOG2E = math.log2(math.e)

OFF_FOX_Q = 0
OFF_FOX_K = OFF_FOX_Q + FOX_WIDTH
OFF_FOX_V = OFF_FOX_K + FOX_WIDTH
OFF_FOX_F = OFF_FOX_V + FOX_WIDTH
OFF_DIL_Q = OFF_FOX_F + FOX_HEADS
OFF_DIL_K = OFF_DIL_Q + DIL_WIDTH
OFF_DIL_V = OFF_DIL_K + DIL_WIDTH
OFF_GATE_A = OFF_DIL_V + DIL_WIDTH
OFF_GATE_B = OFF_GATE_A + D_MODEL
N_IN = OFF_GATE_B + D_MODEL

LANES = 128
UNIT = 256
DIL_L = 128

U_GATE_A, U_GATE_B, U_FOX_Q, U_FOX_K, U_FOX_V, U_DIL, U_FORGET = 0, 4, 8, 10, 12, 14, 23
N_UNITS = 24
P_WIDTH = U_DIL * UNIT
N_SLABS = 3 * N_DIL_GROUPS
_KIND = (["gate"] * 8 + ["norm"] * 4 + ["plain"] * 2 + ["norm", "norm", "plain"] * 3 + ["forget"])

TM_INPROJ = 1024
TM_PROJ = 1024
TQ_FOX = 256
MOE_ROWS = 256
MOE_TILE = 512
XS_WIDTH = D_MODEL // 2 + LANES
SUBLANES = 8
TILE_ROWS = 2 * MOE_TILE + N_EXPERTS * SUBLANES
MOE_GROUPS = MOE_ROWS // SUBLANES
GATHER_BUFS = 3
VMEM_LIMIT = 56 * 1024 * 1024


def _dot(a, b):
    return jnp.dot(a, b, preferred_element_type=F32)


def _dot_nt(a, b):
    return lax.dot_general(a, b, (((1,), (1,)), ((), ())), preferred_element_type=F32)


def _split3(x):
    hi = x.astype(BF16)
    r1 = x - hi.astype(F32)
    mid = r1.astype(BF16)
    lo = (r1 - mid.astype(F32)).astype(BF16)
    return hi, mid, lo


def _ada_kernel(c_ref, w_ref, b_ref, o_ref):
    c = c_ref[...]
    s = c * jax.nn.sigmoid(c)
    s_hi = s.astype(BF16)
    s_lo = (s - s_hi.astype(F32)).astype(BF16)
    w = w_ref[...]
    w_hi = w.astype(BF16)
    w_lo = (w - w_hi.astype(F32)).astype(BF16)
    acc = _dot(s_hi, w_hi) + _dot(s_hi, w_lo) + _dot(s_lo, w_hi)
    o_ref[...] = acc + b_ref[...]


def _ada(c, w_ada, b_ada):
    B = c.shape[0]
    n_out = w_ada.shape[1]
    tn = 512
    return pl.pallas_call(
        _ada_kernel,
        grid=(n_out // tn,),
        in_specs=[pl.BlockSpec((B, D_MODEL), lambda j: (0, 0)),
                  pl.BlockSpec((D_MODEL, tn), lambda j: (0, j)),
                  pl.BlockSpec((1, tn), lambda j: (0, j))],
        out_specs=pl.BlockSpec((B, tn), lambda j: (0, j)),
        out_shape=jax.ShapeDtypeStruct((B, n_out), F32),
        name="ada_mod",
    )(c, w_ada, b_ada.reshape(1, n_out))


def _pack_bf16_pair(lo, hi):
    lo_bits = pltpu.bitcast(lo.astype(BF16).astype(F32), jnp.uint32) >> 16
    hi_bits = pltpu.bitcast(hi.astype(BF16).astype(F32), jnp.uint32) & jnp.uint32(0xFFFF0000)
    return lo_bits | hi_bits


def _unpack_bf16_pair(u):
    lo = pltpu.bitcast(u << 16, F32).astype(BF16)
    hi = pltpu.bitcast(u & jnp.uint32(0xFFFF0000), F32).astype(BF16)
    return lo, hi


def _inproj_kernel(x_ref, g_ref, sc_ref, sh_ref, wg_ref, wf_ref, wq_ref, wk_ref, wv_ref, wz_ref,
                   gain_ref, bd_ref, p_ref, f_ref, s_ref):
    x = x_ref[...]
    ms = jnp.mean(x * x, axis=-1, keepdims=True)
    h = x * lax.rsqrt(ms + EPS) * g_ref[...]
    h = h * (1.0 + sc_ref[0]) + sh_ref[0]
    hb = h.astype(BF16)

    def weights(u):
        if u < U_FOX_Q:
            return wg_ref, (u - U_GATE_A) * UNIT
        if u < U_DIL:
            return wf_ref, (u - U_FOX_Q) * UNIT
        if u < U_FORGET:
            g, j = divmod(u - U_DIL, 3)
            return (wq_ref, wk_ref, wv_ref)[j], g * UNIT
        return wz_ref, 0

    def gain_col(u):
        if u < U_DIL:
            return (u - U_FOX_Q) * UNIT
        g, j = divmod(u - U_DIL, 3)
        return 2 * FOX_WIDTH + j * DIL_WIDTH + g * UNIT

    def unit(u):
        w_ref, c0 = weights(u)
        acc = _dot(hb, w_ref[:, c0:c0 + UNIT])
        kind = _KIND[u]
        if kind == "gate":
            return jax.nn.sigmoid(acc)
        if kind == "norm":
            ss = _dot((acc * acc).astype(BF16), bd_ref[...])
            gc = gain_col(u)
            return acc * lax.rsqrt(ss * (1.0 / HEAD_DIM) + EPS) * gain_ref[:, gc:gc + UNIT]
        return acc

    dil_q = {}

    def emit(u):
        o = unit(u)
        if u < U_DIL:
            p_ref[:, u * UNIT:(u + 1) * UNIT] = o.astype(BF16)
        elif u < U_FORGET:
            g, j = divmod(u - U_DIL, 3)
            if j == 0:
                dil_q[g] = o
            elif j == 1:
                q = dil_q.pop(g)
                s_ref[0, 3 * g] = _pack_bf16_pair(q[:, :LANES], o[:, :LANES])
                s_ref[0, 3 * g + 1] = _pack_bf16_pair(q[:, LANES:], o[:, LANES:])
            else:
                s_ref[0, 3 * g + 2] = _pack_bf16_pair(o[:, :LANES], o[:, LANES:])
        else:
            f_ref[...] = o[:, :LANES]

    normed = [u for u in range(N_UNITS) if _KIND[u] == "norm"]
    others = [u for u in range(N_UNITS) if _KIND[u] != "norm"]
    while normed or others:
        for group in (others, normed):
            if group:
                emit(group.pop(0))


def _inproj(x2d, norm_g, sc, sh, w_re, gain_row, S):
    T = x2d.shape[0]
    tm = TM_INPROJ
    per_b = S // tm
    bd = np.kron(np.eye(UNIT // HEAD_DIM), np.ones((HEAD_DIM, HEAD_DIM))).astype(np.float32)
    once = dict(pipeline_mode=pl.Buffered(1))
    return pl.pallas_call(
        _inproj_kernel,
        grid=(T // tm,),
        in_specs=[pl.BlockSpec((tm, D_MODEL), lambda i: (i, 0)),
                  pl.BlockSpec((1, D_MODEL), lambda i: (0, 0)),
                  pl.BlockSpec((1, 1, D_MODEL), lambda i: (i // per_b, 0, 0)),
                  pl.BlockSpec((1, 1, D_MODEL), lambda i: (i // per_b, 0, 0)),
                  *[pl.BlockSpec(w.shape, lambda i: (0, 0), **once) for w in w_re],
                  pl.BlockSpec(gain_row.shape, lambda i: (0, 0), **once),
                  pl.BlockSpec((UNIT, UNIT), lambda i: (0, 0), **once)],
        out_specs=[pl.BlockSpec((tm, P_WIDTH), lambda i: (i, 0)),
                   pl.BlockSpec((tm, LANES), lambda i: (i, 0)),
                   pl.BlockSpec((1, N_SLABS, tm, LANES), lambda i: (i // per_b, 0, i % per_b, 0))],
        out_shape=[jax.ShapeDtypeStruct((T, P_WIDTH), BF16),
                   jax.ShapeDtypeStruct((T, LANES), F32),
                   jax.ShapeDtypeStruct((T // S, N_SLABS, S, LANES), jnp.uint32)],
        compiler_params=pltpu.CompilerParams(vmem_limit_bytes=VMEM_LIMIT),
        name="in_proj",
    )(x2d, norm_g, sc, sh, *w_re, gain_row, jnp.asarray(bd, BF16))


def _fcum_kernel(f_ref, b_ref, tri_ref, o_ref):
    S = f_ref.shape[1]
    xf = f_ref[0] + b_ref[...]
    ls = (jnp.minimum(xf, 0.0) - jnp.log(1.0 + jnp.exp(-jnp.abs(xf)))) * LOG2E
    lst = ls.T
    carry = jnp.zeros((LANES, UNIT), F32)
    for blk in range(S // UNIT):
        seg = lst[:, blk * UNIT:(blk + 1) * UNIT]
        hi, mid, lo = _split3(seg)
        tri = tri_ref[...]
        res = _dot(hi, tri) + _dot(mid, tri) + _dot(lo, tri)
        o_ref[0, :, blk * UNIT:(blk + 1) * UNIT] = (res[:, :UNIT] + carry)[:FOX_HEADS]
        carry = carry + res[:, UNIT:]


def _fcum(fgt, b_forget):
    B, S, _ = fgt.shape
    brow = jnp.zeros((1, LANES), F32).at[0, :FOX_HEADS].set(b_forget)
    tri = np.concatenate([np.triu(np.ones((UNIT, UNIT))), np.ones((UNIT, UNIT))], axis=1)
    return pl.pallas_call(
        _fcum_kernel,
        grid=(B,),
        in_specs=[pl.BlockSpec((1, S, LANES), lambda b: (b, 0, 0)),
                  pl.BlockSpec((1, LANES), lambda b: (0, 0)),
                  pl.BlockSpec((UNIT, 2 * UNIT), lambda b: (0, 0))],
        out_specs=pl.BlockSpec((1, FOX_HEADS, S), lambda b: (b, 0, 0)),
        out_shape=jax.ShapeDtypeStruct((B, FOX_HEADS, S), F32),
        name="forget_cumsum",
    )(fgt, brow, jnp.asarray(tri, BF16))


def _fox_kernel(q_ref, k_ref, v_ref, ck_ref, o_ref):
    S = q_ref.shape[1]
    pair = pl.program_id(1)
    tq = TQ_FOX
    lane = lax.broadcasted_iota(jnp.int32, (1, LANES), 1)
    row = lax.broadcasted_iota(jnp.int32, (tq, tq), 0)
    col = lax.broadcasted_iota(jnp.int32, (tq, tq), 1)
    causal = col <= row
    cks = [ck_ref[0, pl.ds(2 * pair + hh, 1), :] for hh in range(2)]
    for t in reversed(range(S // tq)):
        r0, r1 = t * tq, (t + 1) * tq
        qt = q_ref[0, r0:r1, :]
        zero = jnp.zeros_like(qt)
        q2 = jnp.concatenate([jnp.where(lane < HEAD_DIM, qt, zero),
                              jnp.where(lane >= HEAD_DIM, qt, zero)], axis=0)
        s2 = _dot_nt(q2, k_ref[0, :r1, :])
        halves = []
        for hh in range(2):
            s = s2[hh * tq:(hh + 1) * tq] - cks[hh][:, :r1]
            s_d = jnp.where(causal, s[:, r0:], -jnp.inf)
            halves.append(jnp.concatenate([s[:, :r0], s_d], axis=1) if t > 0 else s_d)
        s = jnp.concatenate(halves, axis=0)
        m = jnp.max(s, axis=-1, keepdims=True)
        p = jnp.exp2(s - m)
        l = jnp.sum(p, axis=-1, keepdims=True)
        o2 = _dot(p.astype(BF16), v_ref[0, :r1, :]) / l
        o_ref[0, r0:r1, :] = jnp.where(lane < HEAD_DIM, o2[:tq], o2[tq:]).astype(BF16)


def _fox(p3, ck):
    B, S, _ = p3.shape
    nq, nk, nv = (U_FOX_Q * UNIT // LANES, U_FOX_K * UNIT // LANES, U_FOX_V * UNIT // LANES)
    return pl.pallas_call(
        _fox_kernel,
        grid=(B, FOX_HEADS // 2),
        in_specs=[pl.BlockSpec((1, S, LANES), lambda b, p: (b, 0, nq + p)),
                  pl.BlockSpec((1, S, LANES), lambda b, p: (b, 0, nk + p)),
                  pl.BlockSpec((1, S, LANES), lambda b, p: (b, 0, nv + p)),
                  pl.BlockSpec((1, FOX_HEADS, S), lambda b, p: (b, 0, 0))],
        out_specs=pl.BlockSpec((1, S, LANES), lambda b, p: (b, 0, p)),
        out_shape=jax.ShapeDtypeStruct((B, S, FOX_WIDTH), BF16),
        compiler_params=pltpu.CompilerParams(vmem_limit_bytes=VMEM_LIMIT),
        name="fox_attn",
    )(p3, p3, p3, ck)


def _t5_bucket(dist):
    max_exact = NUM_BUCKETS // 2
    d = np.maximum(dist, 1).astype(np.float32)
    large = max_exact + (np.log(d / max_exact) / np.log(REL_MAX_DISTANCE / max_exact)
                         * (NUM_BUCKETS - max_exact)).astype(np.int32)
    large = np.minimum(large, NUM_BUCKETS - 1)
    return np.where(dist < max_exact, dist, large).astype(np.int32)


def _relbias_kernel(tab_ref, bucket_ref, valid_ref, o_ref):
    g = pl.program_id(0)
    bk = bucket_ref[0]
    vd = valid_ref[0]
    for hs in range(DIL_HEADS_PER_GROUP):
        acc = jnp.zeros(bk.shape, F32)
        for b in range(NUM_BUCKETS):
            acc = jnp.where(bk == b, tab_ref[b, g * DIL_HEADS_PER_GROUP + hs], acc)
        bias = jnp.where(vd != 0, acc * LOG2E, -jnp.inf)
        o_ref[0, hs] = bias
        col = lax.broadcasted_iota(jnp.int32, bias.shape, 1)
        o_ref[1, hs] = jnp.where(col >= DIL_L, bias, -jnp.inf)


def _relbias(table):
    L = DIL_L
    i = np.arange(L)[:, None]
    j = np.arange(2 * L)[None, :]
    m = L + i - j
    valid = ((m >= 0) & (m <= L)).astype(np.int32)
    buckets = np.stack([_t5_bucket(np.clip(m, 0, None) * d) for _, d in DIL_GROUPS])
    valids = np.stack([valid] * N_DIL_GROUPS)
    return pl.pallas_call(
        _relbias_kernel,
        grid=(N_DIL_GROUPS,),
        in_specs=[pl.BlockSpec(memory_space=pltpu.SMEM),
                  pl.BlockSpec((1, L, 2 * L), lambda g: (g, 0, 0)),
                  pl.BlockSpec((1, L, 2 * L), lambda g: (g, 0, 0))],
        out_specs=pl.BlockSpec((2, DIL_HEADS_PER_GROUP, L, 2 * L), lambda g: (0, g, 0, 0)),
        out_shape=jax.ShapeDtypeStruct((2, DIL_HEADS, L, 2 * L), F32),
        name="rel_bias",
    )(table, jnp.asarray(buckets), jnp.asarray(valids))


def _dil_rows(start, d):
    return pl.ds(start, DIL_L) if d == 1 else pl.ds(start, DIL_L, stride=d)


def _dil_block_rows(d, nb, it):
    r, n = it // nb, it % nb
    cur = _dil_rows(r + d * (n * DIL_L), d)
    prev = _dil_rows(r + d * (jnp.maximum(n - 1, 0) * DIL_L), d)
    return cur, prev, 1 - jnp.minimum(n, 1)


def _dil_scores(qkv_ref, bias_ref, s_scr, slot, g, d, nb, it):
    lane = lax.broadcasted_iota(jnp.int32, (1, LANES), 1)
    cur, prev, first = _dil_block_rows(d, nb, it)
    for pr in range(2):
        qt, kt = _unpack_bf16_pair(qkv_ref[0, 3 * g + pr, cur, :])
        if nb > 1:
            _, k_prev = _unpack_bf16_pair(qkv_ref[0, 3 * g + pr, prev, :])
            kt = jnp.concatenate([k_prev, kt], axis=0)
        for hh in range(2):
            hsel = (lane >= HEAD_DIM) == bool(hh)
            qm = jnp.where(hsel, qt, jnp.zeros_like(qt))
            head = 2 * pr + hh
            if nb > 1:
                s_scr[slot, head] = _dot_nt(qm, kt) + bias_ref[first, DIL_HEADS_PER_GROUP * g + head]
            else:
                bias = bias_ref[0, DIL_HEADS_PER_GROUP * g + head, :, DIL_L:]
                s_scr[slot, head, :, :DIL_L] = _dot_nt(qm, kt) + bias


def _dil_merge(qkv_ref, s_scr, slot, m_scr, l_scr, acc_scr, g, d, nb, init, it):
    lane = lax.broadcasted_iota(jnp.int32, (1, LANES), 1)
    cur, prev, _ = _dil_block_rows(d, nb, it)
    v_cur = _unpack_bf16_pair(qkv_ref[0, 3 * g + 2, cur, :])
    if nb > 1:
        v_prev = _unpack_bf16_pair(qkv_ref[0, 3 * g + 2, prev, :])
    for pr in range(2):
        vt = jnp.concatenate([v_prev[pr], v_cur[pr]], axis=0) if nb > 1 else v_cur[pr]
        ms, ls, accs = [], [], []
        for hh in range(2):
            s = s_scr[slot, 2 * pr + hh] if nb > 1 else s_scr[slot, 2 * pr + hh, :, :DIL_L]
            m = jnp.max(s, axis=-1, keepdims=True)
            p = jnp.exp2(s - m)
            ms.append(m)
            ls.append(jnp.sum(p, axis=-1, keepdims=True))
            accs.append(_dot(p.astype(BF16), vt))
        low = lane < HEAD_DIM
        m_b = jnp.where(low, ms[0], ms[1])
        l_b = jnp.where(low, ls[0], ls[1])
        acc_b = jnp.where(low, accs[0], accs[1])
        if init:
            m_scr[pr, cur, :] = m_b
            l_scr[pr, cur, :] = l_b
            acc_scr[pr, cur, :] = acc_b
        else:
            m_o = m_scr[pr, cur, :]
            m_n = jnp.maximum(m_o, m_b)
            a_o = jnp.exp2(m_o - m_n)
            a_b = jnp.exp2(m_b - m_n)
            m_scr[pr, cur, :] = m_n
            l_scr[pr, cur, :] = l_scr[pr, cur, :] * a_o + l_b * a_b
            acc_scr[pr, cur, :] = acc_scr[pr, cur, :] * a_o + acc_b * a_b


def _dil_kernel(qkv_ref, bias_ref, o_ref, m_scr, l_scr, acc_scr, s_scr):
    S = o_ref.shape[1]
    order = sorted(range(N_DIL_GROUPS), key=lambda g: -DIL_GROUPS[g][1])
    for g in order:
        window, d = DIL_GROUPS[g]
        nb = S // window
        total = d * nb
        assert total % 2 == 0
        scores = functools.partial(_dil_scores, qkv_ref, bias_ref, s_scr, g=g, d=d, nb=nb)
        merge = functools.partial(_dil_merge, qkv_ref, s_scr, m_scr=m_scr, l_scr=l_scr,
                                  acc_scr=acc_scr, g=g, d=d, nb=nb, init=g == order[0])

        scores(slot=0, it=0)

        def body(j, carry, scores=scores, merge=merge, total=total):
            scores(slot=1, it=2 * j + 1)
            merge(slot=0, it=2 * j)
            scores(slot=0, it=jnp.minimum(2 * j + 2, total - 1))
            merge(slot=1, it=2 * j + 1)
            return carry
        lax.fori_loop(0, total // 2, body, 0, unroll=2)
    for pr in range(2):
        o_ref[0, :, pr * LANES:(pr + 1) * LANES] = (acc_scr[pr] / l_scr[pr]).astype(BF16)


def _dil(slabs, bias):
    B, _, S, _ = slabs.shape
    for window, d in DIL_GROUPS:
        assert window // d == DIL_L and S % window == 0
    stat = pltpu.VMEM((2, S, LANES), F32)
    return pl.pallas_call(
        _dil_kernel,
        grid=(B,),
        in_specs=[pl.BlockSpec((1, N_SLABS, S, LANES), lambda b: (b, 0, 0, 0)),
                  pl.BlockSpec(bias.shape, lambda b: (0, 0, 0, 0))],
        out_specs=pl.BlockSpec((1, S, DIL_OUT_WIDTH), lambda b: (b, 0, 0)),
        out_shape=jax.ShapeDtypeStruct((B, S, DIL_OUT_WIDTH), BF16),
        scratch_shapes=[stat, stat, stat,
                        pltpu.VMEM((2, DIL_HEADS_PER_GROUP, DIL_L, 2 * DIL_L), F32)],
        compiler_params=pltpu.CompilerParams(vmem_limit_bytes=VMEM_LIMIT),
        name="dil_attn",
    )(slabs, bias)


def _outproj_kernel(x_ref, ya_ref, yd_ref, ga_ref, gb_ref,
                    g1_ref, sc_ref, sh_ref, ng_ref, wa_ref, wb_ref, wo_ref, wr_ref, br_ref,
                    x1_ref, h2_ref, lg_ref):
    n_chunks = 2
    cm = x_ref.shape[0] // n_chunks
    for c in range(n_chunks):
        rows = slice(c * cm, (c + 1) * cm)
        a = _dot(ya_ref[rows, :], wa_ref[...])
        bm = _dot(yd_ref[rows, :], wb_ref[...])
        merged = ga_ref[rows, :].astype(F32) * a + gb_ref[rows, :].astype(F32) * bm
        out = _dot(merged.astype(BF16), wo_ref[...])
        x1 = x_ref[rows, :] + g1_ref[0] * out
        x1_ref[rows, :] = x1
        ms = jnp.mean(x1 * x1, axis=-1, keepdims=True)
        h = x1 * lax.rsqrt(ms + EPS) * ng_ref[...]
        h = h * (1.0 + sc_ref[0]) + sh_ref[0]
        hb = h.astype(BF16)
        h2_ref[rows, :] = hb
        lg_ref[rows, :] = _dot(hb, wr_ref[...]) + br_ref[...]


def _outproj(x2d, ya2d, yd2d, p2d, g1, sc2, sh2, norm_g, wa, wb, wo, wr, br, S):
    T = x2d.shape[0]
    tm = TM_PROJ
    per_b = S // tm
    row = lambda w: pl.BlockSpec((tm, w), lambda i: (i, 0))
    full = lambda a: pl.BlockSpec(a.shape, lambda i: (0,) * a.ndim)
    mod = pl.BlockSpec((1, 1, D_MODEL), lambda i: (i // per_b, 0, 0))
    return pl.pallas_call(
        _outproj_kernel,
        grid=(T // tm,),
        in_specs=[row(D_MODEL), row(FOX_WIDTH), row(DIL_OUT_WIDTH)]
                 + [pl.BlockSpec((tm, D_MODEL), lambda i: (i, U_GATE_A * UNIT // D_MODEL)),
                    pl.BlockSpec((tm, D_MODEL), lambda i: (i, U_GATE_B * UNIT // D_MODEL)),
                    mod, mod, mod, full(norm_g), full(wa), full(wb), full(wo), full(wr), full(br)],
        out_specs=[row(D_MODEL), row(D_MODEL), row(LANES)],
        out_shape=[jax.ShapeDtypeStruct((T, D_MODEL), F32),
                   jax.ShapeDtypeStruct((T, D_MODEL), BF16),
                   jax.ShapeDtypeStruct((T, LANES), F32)],
        compiler_params=pltpu.CompilerParams(vmem_limit_bytes=VMEM_LIMIT),
        name="out_proj",
    )(x2d, ya2d, yd2d, p2d, p2d, g1, sc2, sh2, norm_g, wa, wb, wo, wr, br)


def _dispatch_kernel(lg_ref, h_ref, tri_ref, xs_ref, cnt_ref, pos_ref):
    tt = lg_ref.shape[0]
    lt = lg_ref[...].T
    row = lambda i: lt[i:i + 1, :]
    neg = -jnp.inf
    g = [row(i) for i in range(N_GROUPS)]
    gmax = functools.reduce(jnp.maximum, g)
    gidx = jnp.full(gmax.shape, N_GROUPS - 1, jnp.int32)
    for i in reversed(range(N_GROUPS - 1)):
        gidx = jnp.where(g[i] == gmax, i, gidx)
    gsum = sum(jnp.exp(gi - gmax) for gi in g)
    el = []
    for j in range(EXPERTS_PER_GROUP):
        v = row(N_GROUPS + EXPERTS_PER_GROUP * (N_GROUPS - 1) + j)
        for gg in reversed(range(N_GROUPS - 1)):
            v = jnp.where(gidx == gg, row(N_GROUPS + EXPERTS_PER_GROUP * gg + j), v)
        el.append(v)

    def top(vals):
        best = functools.reduce(jnp.maximum, vals)
        idx = jnp.full(best.shape, EXPERTS_PER_GROUP - 1, jnp.int32)
        for j in reversed(range(EXPERTS_PER_GROUP - 1)):
            idx = jnp.where(vals[j] == best, j, idx)
        return best, idx

    v1, i1 = top(el)
    v2, i2 = top([jnp.where(i1 == j, neg, el[j]) for j in range(EXPERTS_PER_GROUP)])
    t = jnp.exp(v2 - v1)
    den = (1.0 + t) * gsum
    wts = [1.0 / den, t / den]
    eid = [gidx * EXPERTS_PER_GROUP + i1, gidx * EXPERTS_PER_GROUP + i2]

    esub = lax.broadcasted_iota(jnp.int32, (N_EXPERTS, tt), 0)
    ohf = jnp.concatenate([jnp.where(esub == eid[k], 1.0, 0.0) for k in range(2)], axis=1)
    n_pb = 2 * tt // UNIT
    oh_blocks = jnp.concatenate([ohf[:, b * UNIT:(b + 1) * UNIT] for b in range(n_pb)], axis=0)
    res = _dot(oh_blocks.astype(BF16), tri_ref[...])
    cnt = jnp.zeros((N_EXPERTS, LANES), F32)
    pre = []
    for b in range(n_pb):
        r = res[b * N_EXPERTS:(b + 1) * N_EXPERTS]
        pre.append(r[:, :UNIT] + jnp.concatenate([cnt] * (UNIT // LANES), axis=1))
        cnt = cnt + r[:, UNIT:]
    prefix = jnp.concatenate(pre, axis=1)
    cnt = (((cnt.astype(jnp.int32) + (SUBLANES - 1)) // SUBLANES) * SUBLANES).astype(F32)
    esub_c = lax.broadcasted_iota(jnp.int32, cnt.shape, 0)
    start = jnp.zeros_like(cnt)
    for e in range(N_EXPERTS - 1):
        start = start + jnp.where(esub_c > e, cnt[e:e + 1, :], 0.0)
    start_w = jnp.concatenate([start] * (2 * tt // LANES), axis=1)
    pos = jnp.sum(ohf * (start_w + prefix), axis=0, keepdims=True)
    pos_k = [pos[:, :tt], pos[:, tt:]]

    n_rows = xs_ref.shape[0]
    psub = lax.broadcasted_iota(jnp.int32, (n_rows, tt), 0).astype(F32)
    pm = [jnp.where(psub == pos_k[k], 1.0, 0.0).astype(BF16) for k in range(2)]
    xs = _dot(pm[0] + pm[1], h_ref[...])
    wsub = lax.broadcasted_iota(jnp.int32, (LANES, tt), 0)
    ws = jnp.zeros((n_rows, LANES), F32)
    for k in range(2):
        parts = _split3(wts[k])
        wrows = jnp.zeros((LANES, tt), F32)
        for j in range(3):
            wrows = jnp.where(wsub == j, parts[j].astype(F32), wrows)
        ws = ws + _dot_nt(pm[k], wrows.astype(BF16))
    half = D_MODEL // 2
    xs_ref[:, :half] = _pack_bf16_pair(xs[:, :half], xs[:, half:])
    xs_ref[:, half:] = pltpu.bitcast(ws, jnp.uint32)
    cnt_ref[0] = cnt.astype(jnp.int32)
    posr = jnp.where(wsub == 0, pos_k[0], jnp.where(wsub == 1, pos_k[1], 0.0))
    pos_ref[...] = posr.T


def _dispatch(logits, h2):
    T = logits.shape[0]
    tt = MOE_TILE
    n_tiles = T // tt
    tri = np.concatenate([np.triu(np.ones((UNIT, UNIT)), 1), np.ones((UNIT, LANES))], axis=1)
    return pl.pallas_call(
        _dispatch_kernel,
        grid=(n_tiles,),
        in_specs=[pl.BlockSpec((tt, LANES), lambda i: (i, 0)),
                  pl.BlockSpec((tt, D_MODEL), lambda i: (i, 0)),
                  pl.BlockSpec(tri.shape, lambda i: (0, 0))],
        out_specs=[pl.BlockSpec((TILE_ROWS, XS_WIDTH), lambda i: (i, 0)),
                   pl.BlockSpec((1, N_EXPERTS, LANES), lambda i: (i, 0, 0)),
                   pl.BlockSpec((tt, LANES), lambda i: (i, 0))],
        out_shape=[jax.ShapeDtypeStruct((n_tiles * TILE_ROWS, XS_WIDTH), jnp.uint32),
                   jax.ShapeDtypeStruct((n_tiles, N_EXPERTS, LANES), jnp.int32),
                   jax.ShapeDtypeStruct((T, LANES), F32)],
        compiler_params=pltpu.CompilerParams(vmem_limit_bytes=VMEM_LIMIT),
        name="moe_dispatch",
    )(logits, h2, jnp.asarray(tri, BF16))


def _plan_kernel(cnt_ref, be_ref, nv_ref, nxt_ref, grp_ref, used_ref, cs_ref):
    n_tiles = cnt_ref.shape[0]
    n_blk = be_ref.shape[0]
    rows = MOE_ROWS
    row_shift = rows.bit_length() - 1
    grp_shift = SUBLANES.bit_length() - 1
    assert rows == 1 << row_shift and SUBLANES == 1 << grp_shift

    def tile_starts(t, c):
        def per_e(e, acc):
            cs_ref[t * N_EXPERTS + e] = acc
            return acc + cnt_ref[t, e]
        used_ref[t] = lax.fori_loop(0, N_EXPERTS, per_e, 0, unroll=8)
        return c
    lax.fori_loop(0, n_tiles, tile_starts, 0)

    def clear(b, c):
        nv_ref[b] = 0
        return c
    lax.fori_loop(0, n_blk, clear, 0)

    def clear_groups(g, c):
        grp_ref[g] = 0
        return c
    lax.fori_loop(0, n_blk * MOE_GROUPS, clear_groups, 0, unroll=8)

    def per_expert(e, b):
        g0 = b * MOE_GROUPS

        def per_tile(t, tot):
            c = cnt_ref[t, e]
            src = t * TILE_ROWS + cs_ref[t * N_EXPERTS + e]
            first = g0 + lax.shift_right_logical(tot, grp_shift)

            def per_group(k, cc):
                grp_ref[first + k] = src + k * SUBLANES
                return cc
            lax.fori_loop(0, lax.shift_right_logical(c, grp_shift), per_group, 0)
            return tot + c
        tot = lax.fori_loop(0, n_tiles, per_tile, 0)

        def per_block(j, c):
            be_ref[b + j] = e
            nv_ref[b + j] = jnp.minimum(rows, tot - j * rows)
            return c
        nb = lax.shift_right_logical(tot + rows - 1, row_shift)
        lax.fori_loop(0, nb, per_block, 0)
        return b + nb
    n_used = lax.fori_loop(0, N_EXPERTS, per_expert, 0)

    def unused(b, c):
        be_ref[b] = be_ref[n_used - 1]
        nxt_ref[b] = -1
        return c
    lax.fori_loop(n_used, n_blk, unused, 0)

    def next_run(k, nf):
        b = n_used - 1 - k
        nf = jnp.where(be_ref[b] != be_ref[jnp.minimum(b + 1, n_used - 1)], b + 1, nf)
        nxt_ref[b] = nf
        return nf
    lax.fori_loop(0, n_used, next_run, -1)


def _plan(cnt, n_blk):
    n_tiles = cnt.shape[0]
    smem = pl.BlockSpec(memory_space=pltpu.SMEM)
    i32 = lambda n: jax.ShapeDtypeStruct((n,), jnp.int32)
    return pl.pallas_call(
        _plan_kernel,
        in_specs=[smem],
        out_specs=[smem] * 5,
        out_shape=[i32(n_blk), i32(n_blk), i32(n_blk), i32(n_blk * MOE_GROUPS), i32(n_tiles)],
        scratch_shapes=[pltpu.SMEM((n_tiles * N_EXPERTS,), jnp.int32)],
        name="moe_plan",
    )(cnt)


def _pow2_pieces(n, fn):
    for b in reversed(range(SUBLANES.bit_length() - 1, MOE_ROWS.bit_length())):
        size = 1 << b

        @pl.when((n & size) != 0)
        def _():
            fn((n >> (b + 1)) << (b + 1), size)


def _moe_kernel(be_ref, nv_ref, nxt_ref, grp_ref, used_ref,
                w1_hbm, w3_hbm, w2_hbm, xs_hbm, ys_hbm,
                xbuf, ybuf, wb1, wb3, wb2, wst1, wst3, wst2, wslot, gsem, ssem, wsem):
    i = pl.program_id(0)
    last = pl.num_programs(0) - 1
    slot = i % 2
    nv = nv_ref[i]
    half = D_MODEL // 2
    grp_shift = SUBLANES.bit_length() - 1

    def group_row(blk, g):
        return pl.multiple_of(grp_ref[blk * MOE_GROUPS + g], SUBLANES)

    def gather(blk, s):
        top = jnp.maximum(lax.shift_right_logical(nv_ref[blk], grp_shift) - 1, 0)
        for g in range(MOE_GROUPS):
            src = group_row(blk, jnp.minimum(g, top))
            pltpu.make_async_copy(xs_hbm.at[pl.ds(src, SUBLANES)],
                                  xbuf.at[s, pl.ds(g * SUBLANES, SUBLANES)], gsem.at[s]).start()

    def wait_gather(s):
        pltpu.make_async_copy(xs_hbm.at[pl.ds(0, MOE_ROWS)], xbuf.at[s], gsem.at[s]).wait()

    def scatter(blk, s):
        def start(g):
            r = g * SUBLANES if isinstance(g, int) else pl.multiple_of(g * SUBLANES, SUBLANES)
            pltpu.make_async_copy(ybuf.at[s, pl.ds(r, SUBLANES)],
                                  ys_hbm.at[pl.ds(group_row(blk, g), SUBLANES)], ssem.at[s]).start()

        @pl.when(nv_ref[blk] == MOE_ROWS)
        def _():
            for g in range(MOE_GROUPS):
                start(g)

        @pl.when(nv_ref[blk] < MOE_ROWS)
        def _():
            def body(g, c):
                start(g)
                return c
            lax.fori_loop(0, lax.shift_right_logical(nv_ref[blk], grp_shift), body, 0)

    def wait_scatter(s, count):
        @pl.when(count == MOE_ROWS)
        def _():
            pltpu.make_async_copy(ybuf.at[s], ys_hbm.at[pl.ds(0, MOE_ROWS)], ssem.at[s]).wait()

        @pl.when(count < MOE_ROWS)
        def _():
            _pow2_pieces(count, lambda a, size: pltpu.make_async_copy(
                ybuf.at[s, pl.ds(0, size)], ys_hbm.at[pl.ds(0, size)], ssem.at[s]).wait())

    @pl.when(i == 0)
    def _():
        @pl.when(nv > 0)
        def _():
            gather(0, 0)
            gather(jnp.minimum(1, last), 1)

        ybuf[1] = jnp.zeros(ybuf.shape[1:], ybuf.dtype)
        n_tiles = used_ref.shape[0]

        def fill(t, c):
            row0 = t * TILE_ROWS + used_ref[t]
            _pow2_pieces(TILE_ROWS - used_ref[t], lambda a, size: pltpu.make_async_copy(
                ybuf.at[1, pl.ds(0, size)], ys_hbm.at[pl.ds(pl.multiple_of(row0 + a, SUBLANES), size)],
                ssem.at[1]).start())
            return c
        lax.fori_loop(0, n_tiles, fill, 0)

        def drain(t, c):
            wait_scatter(1, TILE_ROWS - used_ref[t])
            return c
        lax.fori_loop(0, n_tiles, drain, 0)

    @pl.when(i >= 2)
    def _():
        wait_scatter(slot, nv_ref[jnp.maximum(i - 2, 0)])

    xslot = i % GATHER_BUFS
    issuer_used = jnp.where(i >= 2, nv_ref[jnp.maximum(i - 2, 0)], nv_ref[0]) > 0

    @pl.when((nv == 0) & (i > 0) & issuer_used)
    def _():
        wait_gather(xslot)

    @pl.when(nv > 0)
    def _():
        e = be_ref[i]
        e_prev = be_ref[jnp.maximum(i - 1, 0)]

        def weight_copies(ex, ws):
            return [pltpu.make_async_copy(src.at[ex], dst.at[ws], wsem.at[ws])
                    for src, dst in ((w1_hbm, wst1), (w3_hbm, wst3), (w2_hbm, wst2))]

        @pl.when(i == 0)
        def _():
            wslot[0] = 0
            for cp in weight_copies(e, 0):
                cp.start()

        @pl.when((i == 0) | (e != e_prev))
        def _():
            ws = wslot[0]
            for cp in weight_copies(e, ws):
                cp.wait()
            wb1[...] = wst1[ws].astype(BF16)
            wb3[...] = wst3[ws].astype(BF16)
            wb2[...] = wst2[ws].astype(BF16)
            nb = nxt_ref[i]

            @pl.when(nb >= 0)
            def _():
                for cp in weight_copies(be_ref[jnp.maximum(nb, 0)], 1 - ws):
                    cp.start()
            wslot[0] = 1 - ws

        wait_gather(xslot)
        gather(jnp.minimum(i + 2, last), (i + 2) % GATHER_BUFS)
        u = xbuf[xslot]
        xa, xb = _unpack_bf16_pair(u[:, :half])
        wv = pltpu.bitcast(u[:, half:], F32)
        roww = wv[:, 0:1] + wv[:, 1:2] + wv[:, 2:3]
        a = _dot(xa, wb1[:half, :]) + _dot(xb, wb1[half:, :])
        b = _dot(xa, wb3[:half, :]) + _dot(xb, wb3[half:, :])
        hmid = (a * jax.nn.sigmoid(a) * b).astype(BF16)
        y = _dot(hmid, wb2[...]) * roww
        ybuf[slot] = _pack_bf16_pair(y[:, :half], y[:, half:])
        scatter(i, slot)

    @pl.when(i == last)
    def _():
        @pl.when((last >= 1) & (nv_ref[jnp.maximum(last - 1, 0)] > 0))
        def _():
            wait_gather((last + 1) % GATHER_BUFS)

        @pl.when(nv > 0)
        def _():
            wait_gather((last + 2) % GATHER_BUFS)

        @pl.when(last >= 1)
        def _():
            wait_scatter(1 - slot, nv_ref[jnp.maximum(last - 1, 0)])
        wait_scatter(slot, nv)


def _moe(xs, plan, w1, w3, w2):
    n_blk = plan[0].shape[0]
    rows = MOE_ROWS
    half = D_MODEL // 2
    hbm = pl.BlockSpec(memory_space=pl.ANY)
    grid_spec = pltpu.PrefetchScalarGridSpec(
        num_scalar_prefetch=5,
        grid=(n_blk,),
        in_specs=[hbm] * 4,
        out_specs=hbm,
        scratch_shapes=[pltpu.VMEM((GATHER_BUFS, rows, XS_WIDTH), jnp.uint32),
                        pltpu.VMEM((2, rows, half), jnp.uint32),
                        pltpu.VMEM((D_MODEL, EXPERT_HIDDEN), BF16),
                        pltpu.VMEM((D_MODEL, EXPERT_HIDDEN), BF16),
                        pltpu.VMEM((EXPERT_HIDDEN, D_MODEL), BF16),
                        pltpu.VMEM((2, D_MODEL, EXPERT_HIDDEN), F32),
                        pltpu.VMEM((2, D_MODEL, EXPERT_HIDDEN), F32),
                        pltpu.VMEM((2, EXPERT_HIDDEN, D_MODEL), F32),
                        pltpu.SMEM((1,), jnp.int32),
                        pltpu.SemaphoreType.DMA((GATHER_BUFS,)),
                        pltpu.SemaphoreType.DMA((2,)),
                        pltpu.SemaphoreType.DMA((2,))])
    return pl.pallas_call(
        _moe_kernel,
        grid_spec=grid_spec,
        out_shape=jax.ShapeDtypeStruct((xs.shape[0], half), jnp.uint32),
        compiler_params=pltpu.CompilerParams(dimension_semantics=("arbitrary",),
                                             vmem_limit_bytes=VMEM_LIMIT),
        name="moe_ffn",
    )(*plan, w1, w3, w2, xs)


def _combine_kernel(x1_ref, ys_ref, pos_ref, g2_ref, o_ref):
    tt = x1_ref.shape[0]
    half = D_MODEL // 2
    pos = pos_ref[...]
    pcol = lax.broadcasted_iota(jnp.int32, (tt, ys_ref.shape[0]), 1).astype(F32)
    sel = (jnp.where(pcol == pos[:, 0:1], 1.0, 0.0) + jnp.where(pcol == pos[:, 1:2], 1.0, 0.0))
    sel = sel.astype(BF16)
    lo, hi = _unpack_bf16_pair(ys_ref[...])
    g2 = g2_ref[0]
    x1 = x1_ref[...]
    o_ref[:, :half] = x1[:, :half] + g2[:, :half] * _dot(sel, lo)
    o_ref[:, half:] = x1[:, half:] + g2[:, half:] * _dot(sel, hi)


def _combine(x1, ys, pos, g2, S):
    T = x1.shape[0]
    tt = MOE_TILE
    per_b = S // tt
    return pl.pallas_call(
        _combine_kernel,
        grid=(T // tt,),
        in_specs=[pl.BlockSpec((tt, D_MODEL), lambda i: (i, 0)),
                  pl.BlockSpec((TILE_ROWS, D_MODEL // 2), lambda i: (i, 0)),
                  pl.BlockSpec((tt, LANES), lambda i: (i, 0)),
                  pl.BlockSpec((1, 1, D_MODEL), lambda i: (i // per_b, 0, 0))],
        out_specs=pl.BlockSpec((tt, D_MODEL), lambda i: (i, 0)),
        out_shape=jax.ShapeDtypeStruct((T, D_MODEL), F32),
        compiler_params=pltpu.CompilerParams(vmem_limit_bytes=VMEM_LIMIT),
        name="moe_combine",
    )(x1, ys, pos, g2)


def _prep_w_in(w_in):
    pad = jnp.zeros((D_MODEL, UNIT - FOX_HEADS), w_in.dtype)
    groups = [w_in[:, OFF_GATE_A:N_IN],
              w_in[:, OFF_FOX_Q:OFF_FOX_F],
              w_in[:, OFF_DIL_Q:OFF_DIL_K],
              w_in[:, OFF_DIL_K:OFF_DIL_V],
              w_in[:, OFF_DIL_V:OFF_GATE_A],
              jnp.concatenate([w_in[:, OFF_FOX_F:OFF_DIL_Q], pad], axis=1)]
    return [w.astype(BF16) for w in groups]


def _prep_gain(q_gain, k_gain):
    qs = HEAD_DIM ** -0.5 * LOG2E
    parts = [q_gain[:FOX_HEADS] * qs, k_gain[:FOX_HEADS], q_gain[FOX_HEADS:] * qs, k_gain[FOX_HEADS:]]
    return jnp.concatenate([p.reshape(-1) for p in parts]).reshape(1, -1)


def _layer(x, mod, rel_bias_table, norm1_g, w_in, b_forget, q_gain, k_gain, w_branch_a, w_branch_b,
           w_out, norm2_g, w_rg, b_rg, w_re, b_re, w1, w3, w2):
    B, S, D = x.shape
    T = B * S
    sh1, sc1, g1, sh2, sc2, g2 = [m.reshape(B, 1, D) for m in jnp.split(mod, 6, axis=-1)]
    x2d = x.reshape(T, D)

    p2d, fgt, slabs = _inproj(x2d, norm1_g.reshape(1, D), sc1, sh1, _prep_w_in(w_in),
                              _prep_gain(q_gain, k_gain), S)
    p3 = p2d.reshape(B, S, P_WIDTH)
    ck = _fcum(fgt.reshape(B, S, LANES), b_forget)
    ya = _fox(p3, ck)
    yd = _dil(slabs, _relbias(rel_bias_table))

    n_router = N_GROUPS + N_EXPERTS
    wr = jnp.concatenate([w_rg, w_re, jnp.zeros((D, LANES - n_router), F32)], axis=1).astype(BF16)
    br = jnp.concatenate([b_rg, b_re, jnp.zeros((LANES - n_router,), F32)]).reshape(1, LANES)
    x1, h2, logits = _outproj(x2d, ya.reshape(T, FOX_WIDTH), yd.reshape(T, DIL_OUT_WIDTH), p2d,
                              g1, sc2, sh2, norm2_g.reshape(1, D),
                              w_branch_a.astype(BF16), w_branch_b.astype(BF16), w_out.astype(BF16),
                              wr, br, S)
    xs, cnt, pos = _dispatch(logits, h2)
    cnt2 = cnt[:, :, 0]
    n_blk = cnt.shape[0] * TILE_ROWS // MOE_ROWS + N_EXPERTS
    plan = _plan(cnt2, n_blk)
    ys = _moe(xs, plan, w1, w3, w2)
    out = _combine(x1, ys, pos, g2, S)
    return out.reshape(B, S, D)


def kernel(x, c, rel_bias_table, w_ada, b_ada, norm1_g, w_in, b_forget, q_gain, k_gain, w_branch_a, w_branch_b, w_out, norm2_g, w_router_group, b_router_group, w_router_expert, b_router_expert, w1, w3, w2):
    depth = w_ada.shape[0]
    for l in range(depth):
        mod = _ada(c, w_ada[l], b_ada[l])
        x = _layer(x, mod, rel_bias_table, norm1_g[l], w_in[l], b_forget[l], q_gain[l], k_gain[l],
                   w_branch_a[l], w_branch_b[l], w_out[l], norm2_g[l], w_router_group[l],
                   b_router_group[l], w_router_expert[l], b_router_expert[l], w1[l], w3[l], w2[l])
    return x
```

```python
import functools
import math

import numpy as np
import jax
import jax.numpy as jnp
from jax import lax
from jax.experimental import pallas as pl
from jax.experimental.pallas import tpu as pltpu

F32 = jnp.float32
BF16 = jnp.bfloat16

D_MODEL = 1024
HEAD_DIM = 64
FOX_HEADS = 8
DIL_GROUPS = ((128, 1), (512, 4), (2048, 16))
DIL_HEADS_PER_GROUP = 4
N_DIL_GROUPS = len(DIL_GROUPS)
DIL_HEADS = N_DIL_GROUPS * DIL_HEADS_PER_GROUP
FOX_WIDTH = FOX_HEADS * HEAD_DIM
DIL_WIDTH = DIL_HEADS * HEAD_DIM
DIL_OUT_WIDTH = DIL_HEADS_PER_GROUP * HEAD_DIM
NUM_BUCKETS = 32
REL_MAX_DISTANCE = 2048
N_GROUPS = 4
EXPERTS_PER_GROUP = 8
N_EXPERTS = N_GROUPS * EXPERTS_PER_GROUP
EXPERT_HIDDEN = D_MODEL // 2
EPS = 1e-6
LOG2E = math.log2(math.e)

OFF_FOX_Q = 0
OFF_FOX_K = OFF_FOX_Q + FOX_WIDTH
OFF_FOX_V = OFF_FOX_K + FOX_WIDTH
OFF_FOX_F = OFF_FOX_V + FOX_WIDTH
OFF_DIL_Q = OFF_FOX_F + FOX_HEADS
OFF_DIL_K = OFF_DIL_Q + DIL_WIDTH
OFF_DIL_V = OFF_DIL_K + DIL_WIDTH
OFF_GATE_A = OFF_DIL_V + DIL_WIDTH
OFF_GATE_B = OFF_GATE_A + D_MODEL
N_IN = OFF_GATE_B + D_MODEL

LANES = 128
UNIT = 256
DIL_L = 128

U_GATE_A, U_GATE_B, U_FOX_Q, U_FOX_K, U_FOX_V, U_DIL, U_FORGET = 0, 4, 8, 10, 12, 14, 23
N_UNITS = 24
P_WIDTH = U_DIL * UNIT
N_SLABS = 3 * N_DIL_GROUPS
_KIND = (["gate"] * 8 + ["norm"] * 4 + ["plain"] * 2 + ["norm", "norm", "plain"] * 3 + ["forget"])

TM_INPROJ = 1024
TM_PROJ = 1024
TQ_FOX = 256
MOE_ROWS = 256
MOE_TILE = 512
XS_WIDTH = D_MODEL // 2 + LANES
SUBLANES = 8
TILE_ROWS = 2 * MOE_TILE + N_EXPERTS * SUBLANES
MOE_GROUPS = MOE_ROWS // SUBLANES
GATHER_BUFS = 3
VMEM_LIMIT = 56 * 1024 * 1024


def _dot(a, b):
    return jnp.dot(a, b, preferred_element_type=F32)


def _dot_nt(a, b):
    return lax.dot_general(a, b, (((1,), (1,)), ((), ())), preferred_element_type=F32)


def _split3(x):
    hi = x.astype(BF16)
    r1 = x - hi.astype(F32)
    mid = r1.astype(BF16)
    lo = (r1 - mid.astype(F32)).astype(BF16)
    return hi, mid, lo


def _ada_kernel(c_ref, w_ref, b_ref, o_ref):
    c = c_ref[...]
    s = c * jax.nn.sigmoid(c)
    s_hi = s.astype(BF16)
    s_lo = (s - s_hi.astype(F32)).astype(BF16)
    w = w_ref[...]
    w_hi = w.astype(BF16)
    w_lo = (w - w_hi.astype(F32)).astype(BF16)
    acc = _dot(s_hi, w_hi) + _dot(s_hi, w_lo) + _dot(s_lo, w_hi)
    o_ref[...] = acc + b_ref[...]


def _ada(c, w_ada, b_ada):
    B = c.shape[0]
    n_out = w_ada.shape[1]
    tn = 512
    return pl.pallas_call(
        _ada_kernel,
        grid=(n_out // tn,),
        in_specs=[pl.BlockSpec((B, D_MODEL), lambda j: (0, 0)),
                  pl.BlockSpec((D_MODEL, tn), lambda j: (0, j)),
                  pl.BlockSpec((1, tn), lambda j: (0, j))],
        out_specs=pl.BlockSpec((B, tn), lambda j: (0, j)),
        out_shape=jax.ShapeDtypeStruct((B, n_out), F32),
        name="ada_mod",
    )(c, w_ada, b_ada.reshape(1, n_out))


def _pack_bf16_pair(lo, hi):
    lo_bits = pltpu.bitcast(lo.astype(BF16).astype(F32), jnp.uint32) >> 16
    hi_bits = pltpu.bitcast(hi.astype(BF16).astype(F32), jnp.uint32) & jnp.uint32(0xFFFF0000)
    return lo_bits | hi_bits


def _unpack_bf16_pair(u):
    lo = pltpu.bitcast(u << 16, F32).astype(BF16)
    hi = pltpu.bitcast(u & jnp.uint32(0xFFFF0000), F32).astype(BF16)
    return lo, hi


def _inproj_kernel(x_ref, g_ref, sc_ref, sh_ref, wg_ref, wf_ref, wq_ref, wk_ref, wv_ref, wz_ref,
                   gain_ref, bd_ref, p_ref, f_ref, s_ref):
    x = x_ref[...]
    ms = jnp.mean(x * x, axis=-1, keepdims=True)
    h = x * lax.rsqrt(ms + EPS) * g_ref[...]
    h = h * (1.0 + sc_ref[0]) + sh_ref[0]
    hb = h.astype(BF16)

    def weights(u):
        if u < U_FOX_Q:
            return wg_ref, (u - U_GATE_A) * UNIT
        if u < U_DIL:
            return wf_ref, (u - U_FOX_Q) * UNIT
        if u < U_FORGET:
            g, j = divmod(u - U_DIL, 3)
            return (wq_ref, wk_ref, wv_ref)[j], g * UNIT
        return wz_ref, 0

    def gain_col(u):
        if u < U_DIL:
            return (u - U_FOX_Q) * UNIT
        g, j = divmod(u - U_DIL, 3)
        return 2 * FOX_WIDTH + j * DIL_WIDTH + g * UNIT

    def unit(u):
        w_ref, c0 = weights(u)
        acc = _dot(hb, w_ref[:, c0:c0 + UNIT])
        kind = _KIND[u]
        if kind == "gate":
            return jax.nn.sigmoid(acc)
        if kind == "norm":
            ss = _dot((acc * acc).astype(BF16), bd_ref[...])
            gc = gain_col(u)
            return acc * lax.rsqrt(ss * (1.0 / HEAD_DIM) + EPS) * gain_ref[:, gc:gc + UNIT]
        return acc

    dil_q = {}

    def emit(u):
        o = unit(u)
        if u < U_DIL:
            p_ref[:, u * UNIT:(u + 1) * UNIT] = o.astype(BF16)
        elif u < U_FORGET:
            g, j = divmod(u - U_DIL, 3)
            if j == 0:
                dil_q[g] = o
            elif j == 1:
                q = dil_q.pop(g)
                s_ref[0, 3 * g] = _pack_bf16_pair(q[:, :LANES], o[:, :LANES])
                s_ref[0, 3 * g + 1] = _pack_bf16_pair(q[:, LANES:], o[:, LANES:])
            else:
                s_ref[0, 3 * g + 2] = _pack_bf16_pair(o[:, :LANES], o[:, LANES:])
        else:
            f_ref[...] = o[:, :LANES]

    normed = [u for u in range(N_UNITS) if _KIND[u] == "norm"]
    others = [u for u in range(N_UNITS) if _KIND[u] != "norm"]
    while normed or others:
        for group in (others, normed):
            if group:
                emit(group.pop(0))


def _inproj(x2d, norm_g, sc, sh, w_re, gain_row, S):
    T = x2d.shape[0]
    tm = TM_INPROJ
    per_b = S // tm
    bd = np.kron(np.eye(UNIT // HEAD_DIM), np.ones((HEAD_DIM, HEAD_DIM))).astype(np.float32)
    once = dict(pipeline_mode=pl.Buffered(1))
    return pl.pallas_call(
        _inproj_kernel,
        grid=(T // tm,),
        in_specs=[pl.BlockSpec((tm, D_MODEL), lambda i: (i, 0)),
                  pl.BlockSpec((1, D_MODEL), lambda i: (0, 0)),
                  pl.BlockSpec((1, 1, D_MODEL), lambda i: (i // per_b, 0, 0)),
                  pl.BlockSpec((1, 1, D_MODEL), lambda i: (i // per_b, 0, 0)),
                  *[pl.BlockSpec(w.shape, lambda i: (0, 0), **once) for w in w_re],
                  pl.BlockSpec(gain_row.shape, lambda i: (0, 0), **once),
                  pl.BlockSpec((UNIT, UNIT), lambda i: (0, 0), **once)],
        out_specs=[pl.BlockSpec((tm, P_WIDTH), lambda i: (i, 0)),
                   pl.BlockSpec((tm, LANES), lambda i: (i, 0)),
                   pl.BlockSpec((1, N_SLABS, tm, LANES), lambda i: (i // per_b, 0, i % per_b, 0))],
        out_shape=[jax.ShapeDtypeStruct((T, P_WIDTH), BF16),
                   jax.ShapeDtypeStruct((T, LANES), F32),
                   jax.ShapeDtypeStruct((T // S, N_SLABS, S, LANES), jnp.uint32)],
        compiler_params=pltpu.CompilerParams(vmem_limit_bytes=VMEM_LIMIT),
        name="in_proj",
    )(x2d, norm_g, sc, sh, *w_re, gain_row, jnp.asarray(bd, BF16))


def _fcum_kernel(f_ref, b_ref, tri_ref, o_ref):
    S = f_ref.shape[1]
    xf = f_ref[0] + b_ref[...]
    ls = (jnp.minimum(xf, 0.0) - jnp.log(1.0 + jnp.exp(-jnp.abs(xf)))) * LOG2E
    lst = ls.T
    carry = jnp.zeros((LANES, UNIT), F32)
    for blk in range(S // UNIT):
        seg = lst[:, blk * UNIT:(blk + 1) * UNIT]
        hi, mid, lo = _split3(seg)
        tri = tri_ref[...]
        res = _dot(hi, tri) + _dot(mid, tri) + _dot(lo, tri)
        o_ref[0, :, blk * UNIT:(blk + 1) * UNIT] = (res[:, :UNIT] + carry)[:FOX_HEADS]
        carry = carry + res[:, UNIT:]


def _fcum(fgt, b_forget):
    B, S, _ = fgt.shape
    brow = jnp.zeros((1, LANES), F32).at[0, :FOX_HEADS].set(b_forget)
    tri = np.concatenate([np.triu(np.ones((UNIT, UNIT))), np.ones((UNIT, UNIT))], axis=1)
    return pl.pallas_call(
        _fcum_kernel,
        grid=(B,),
        in_specs=[pl.BlockSpec((1, S, LANES), lambda b: (b, 0, 0)),
                  pl.BlockSpec((1, LANES), lambda b: (0, 0)),
                  pl.BlockSpec((UNIT, 2 * UNIT), lambda b: (0, 0))],
        out_specs=pl.BlockSpec((1, FOX_HEADS, S), lambda b: (b, 0, 0)),
        out_shape=jax.ShapeDtypeStruct((B, FOX_HEADS, S), F32),
        name="forget_cumsum",
    )(fgt, brow, jnp.asarray(tri, BF16))


def _fox_kernel(q_ref, k_ref, v_ref, ck_ref, o_ref):
    S = q_ref.shape[1]
    pair = pl.program_id(1)
    tq = TQ_FOX
    lane = lax.broadcasted_iota(jnp.int32, (1, LANES), 1)
    row = lax.broadcasted_iota(jnp.int32, (tq, tq), 0)
    col = lax.broadcasted_iota(jnp.int32, (tq, tq), 1)
    causal = col <= row
    cks = [ck_ref[0, pl.ds(2 * pair + hh, 1), :] for hh in range(2)]
    for t in reversed(range(S // tq)):
        r0, r1 = t * tq, (t + 1) * tq
        qt = q_ref[0, r0:r1, :]
        zero = jnp.zeros_like(qt)
        q2 = jnp.concatenate([jnp.where(lane < HEAD_DIM, qt, zero),
                              jnp.where(lane >= HEAD_DIM, qt, zero)], axis=0)
        s2 = _dot_nt(q2, k_ref[0, :r1, :])
        halves = []
        for hh in range(2):
            s = s2[hh * tq:(hh + 1) * tq] - cks[hh][:, :r1]
            s_d = jnp.where(causal, s[:, r0:], -jnp.inf)
            halves.append(jnp.concatenate([s[:, :r0], s_d], axis=1) if t > 0 else s_d)
        s = jnp.concatenate(halves, axis=0)
        m = jnp.max(s, axis=-1, keepdims=True)
        p = jnp.exp2(s - m)
        l = jnp.sum(p, axis=-1, keepdims=True)
        o2 = _dot(p.astype(BF16), v_ref[0, :r1, :]) / l
        o_ref[0, r0:r1, :] = jnp.where(lane < HEAD_DIM, o2[:tq], o2[tq:]).astype(BF16)


def _fox(p3, ck):
    B, S, _ = p3.shape
    nq, nk, nv = (U_FOX_Q * UNIT // LANES, U_FOX_K * UNIT // LANES, U_FOX_V * UNIT // LANES)
    return pl.pallas_call(
        _fox_kernel,
        grid=(B, FOX_HEADS // 2),
        in_specs=[pl.BlockSpec((1, S, LANES), lambda b, p: (b, 0, nq + p)),
                  pl.BlockSpec((1, S, LANES), lambda b, p: (b, 0, nk + p)),
                  pl.BlockSpec((1, S, LANES), lambda b, p: (b, 0, nv + p)),
                  pl.BlockSpec((1, FOX_HEADS, S), lambda b, p: (b, 0, 0))],
        out_specs=pl.BlockSpec((1, S, LANES), lambda b, p: (b, 0, p)),
        out_shape=jax.ShapeDtypeStruct((B, S, FOX_WIDTH), BF16),
        compiler_params=pltpu.CompilerParams(vmem_limit_bytes=VMEM_LIMIT),
        name="fox_attn",
    )(p3, p3, p3, ck)


def _t5_bucket(dist):
    max_exact = NUM_BUCKETS // 2
    d = np.maximum(dist, 1).astype(np.float32)
    large = max_exact + (np.log(d / max_exact) / np.log(REL_MAX_DISTANCE / max_exact)
                         * (NUM_BUCKETS - max_exact)).astype(np.int32)
    large = np.minimum(large, NUM_BUCKETS - 1)
    return np.where(dist < max_exact, dist, large).astype(np.int32)


def _relbias_kernel(tab_ref, bucket_ref, valid_ref, o_ref):
    g = pl.program_id(0)
    bk = bucket_ref[0]
    vd = valid_ref[0]
    for hs in range(DIL_HEADS_PER_GROUP):
        acc = jnp.zeros(bk.shape, F32)
        for b in range(NUM_BUCKETS):
            acc = jnp.where(bk == b, tab_ref[b, g * DIL_HEADS_PER_GROUP + hs], acc)
        bias = jnp.where(vd != 0, acc * LOG2E, -jnp.inf)
        o_ref[0, hs] = bias
        col = lax.broadcasted_iota(jnp.int32, bias.shape, 1)
        o_ref[1, hs] = jnp.where(col >= DIL_L, bias, -jnp.inf)


def _relbias(table):
    L = DIL_L
    i = np.arange(L)[:, None]
    j = np.arange(2 * L)[None, :]
    m = L + i - j
    valid = ((m >= 0) & (m <= L)).astype(np.int32)
    buckets = np.stack([_t5_bucket(np.clip(m, 0, None) * d) for _, d in DIL_GROUPS])
    valids = np.stack([valid] * N_DIL_GROUPS)
    return pl.pallas_call(
        _relbias_kernel,
        grid=(N_DIL_GROUPS,),
        in_specs=[pl.BlockSpec(memory_space=pltpu.SMEM),
                  pl.BlockSpec((1, L, 2 * L), lambda g: (g, 0, 0)),
                  pl.BlockSpec((1, L, 2 * L), lambda g: (g, 0, 0))],
        out_specs=pl.BlockSpec((2, DIL_HEADS_PER_GROUP, L, 2 * L), lambda g: (0, g, 0, 0)),
        out_shape=jax.ShapeDtypeStruct((2, DIL_HEADS, L, 2 * L), F32),
        name="rel_bias",
    )(table, jnp.asarray(buckets), jnp.asarray(valids))


def _dil_rows(start, d):
    return pl.ds(start, DIL_L) if d == 1 else pl.ds(start, DIL_L, stride=d)


def _dil_block_rows(d, nb, it):
    r, n = it // nb, it % nb
    cur = _dil_rows(r + d * (n * DIL_L), d)
    prev = _dil_rows(r + d * (jnp.maximum(n - 1, 0) * DIL_L), d)
    return cur, prev, 1 - jnp.minimum(n, 1)


def _dil_scores(qkv_ref, bias_ref, s_scr, slot, g, d, nb, it):
    lane = lax.broadcasted_iota(jnp.int32, (1, LANES), 1)
    cur, prev, first = _dil_block_rows(d, nb, it)
    for pr in range(2):
        qt, kt = _unpack_bf16_pair(qkv_ref[0, 3 * g + pr, cur, :])
        if nb > 1:
            _, k_prev = _unpack_bf16_pair(qkv_ref[0, 3 * g + pr, prev, :])
            kt = jnp.concatenate([k_prev, kt], axis=0)
        for hh in range(2):
            hsel = (lane >= HEAD_DIM) == bool(hh)
            qm = jnp.where(hsel, qt, jnp.zeros_like(qt))
            head = 2 * pr + hh
            if nb > 1:
                s_scr[slot, head] = _dot_nt(qm, kt) + bias_ref[first, DIL_HEADS_PER_GROUP * g + head]
            else:
                bias = bias_ref[0, DIL_HEADS_PER_GROUP * g + head, :, DIL_L:]
                s_scr[slot, head, :, :DIL_L] = _dot_nt(qm, kt) + bias


def _dil_merge(qkv_ref, s_scr, slot, m_scr, l_scr, acc_scr, g, d, nb, init, it):
    lane = lax.broadcasted_iota(jnp.int32, (1, LANES), 1)
    cur, prev, _ = _dil_block_rows(d, nb, it)
    v_cur = _unpack_bf16_pair(qkv_ref[0, 3 * g + 2, cur, :])
    if nb > 1:
        v_prev = _unpack_bf16_pair(qkv_ref[0, 3 * g + 2, prev, :])
    for pr in range(2):
        vt = jnp.concatenate([v_prev[pr], v_cur[pr]], axis=0) if nb > 1 else v_cur[pr]
        ms, ls, accs = [], [], []
        for hh in range(2):
            s = s_scr[slot, 2 * pr + hh] if nb > 1 else s_scr[slot, 2 * pr + hh, :, :DIL_L]
            m = jnp.max(s, axis=-1, keepdims=True)
            p = jnp.exp2(s - m)
            ms.append(m)
            ls.append(jnp.sum(p, axis=-1, keepdims=True))
            accs.append(_dot(p.astype(BF16), vt))
        low = lane < HEAD_DIM
        m_b = jnp.where(low, ms[0], ms[1])
        l_b = jnp.where(low, ls[0], ls[1])
        acc_b = jnp.where(low, accs[0], accs[1])
        if init:
            m_scr[pr, cur, :] = m_b
            l_scr[pr, cur, :] = l_b
            acc_scr[pr, cur, :] = acc_b
        else:
            m_o = m_scr[pr, cur, :]
            m_n = jnp.maximum(m_o, m_b)
            a_o = jnp.exp2(m_o - m_n)
            a_b = jnp.exp2(m_b - m_n)
            m_scr[pr, cur, :] = m_n
            l_scr[pr, cur, :] = l_scr[pr, cur, :] * a_o + l_b * a_b
            acc_scr[pr, cur, :] = acc_scr[pr, cur, :] * a_o + acc_b * a_b


def _dil_kernel(qkv_ref, bias_ref, o_ref, m_scr, l_scr, acc_scr, s_scr):
    S = o_ref.shape[1]
    order = sorted(range(N_DIL_GROUPS), key=lambda g: -DIL_GROUPS[g][1])
    for g in order:
        window, d = DIL_GROUPS[g]
        nb = S // window
        total = d * nb
        assert total % 2 == 0
        scores = functools.partial(_dil_scores, qkv_ref, bias_ref, s_scr, g=g, d=d, nb=nb)
        merge = functools.partial(_dil_merge, qkv_ref, s_scr, m_scr=m_scr, l_scr=l_scr,
                                  acc_scr=acc_scr, g=g, d=d, nb=nb, init=g == order[0])

        scores(slot=0, it=0)

        def body(j, carry, scores=scores, merge=merge, total=total):
            scores(slot=1, it=2 * j + 1)
            merge(slot=0, it=2 * j)
            scores(slot=0, it=jnp.minimum(2 * j + 2, total - 1))
            merge(slot=1, it=2 * j + 1)
            return carry
        lax.fori_loop(0, total // 2, body, 0, unroll=2)
    for pr in range(2):
        o_ref[0, :, pr * LANES:(pr + 1) * LANES] = (acc_scr[pr] / l_scr[pr]).astype(BF16)


def _dil(slabs, bias):
    B, _, S, _ = slabs.shape
    for window, d in DIL_GROUPS:
        assert window // d == DIL_L and S % window == 0
    stat = pltpu.VMEM((2, S, LANES), F32)
    return pl.pallas_call(
        _dil_kernel,
        grid=(B,),
        in_specs=[pl.BlockSpec((1, N_SLABS, S, LANES), lambda b: (b, 0, 0, 0)),
                  pl.BlockSpec(bias.shape, lambda b: (0, 0, 0, 0))],
        out_specs=pl.BlockSpec((1, S, DIL_OUT_WIDTH), lambda b: (b, 0, 0)),
        out_shape=jax.ShapeDtypeStruct((B, S, DIL_OUT_WIDTH), BF16),
        scratch_shapes=[stat, stat, stat,
                        pltpu.VMEM((2, DIL_HEADS_PER_GROUP, DIL_L, 2 * DIL_L), F32)],
        compiler_params=pltpu.CompilerParams(vmem_limit_bytes=VMEM_LIMIT),
        name="dil_attn",
    )(slabs, bias)


def _outproj_kernel(x_ref, ya_ref, yd_ref, ga_ref, gb_ref,
                    g1_ref, sc_ref, sh_ref, ng_ref, wa_ref, wb_ref, wo_ref, wr_ref, br_ref,
                    x1_ref, h2_ref, lg_ref):
    n_chunks = 2
    cm = x_ref.shape[0] // n_chunks
    for c in range(n_chunks):
        rows = slice(c * cm, (c + 1) * cm)
        a = _dot(ya_ref[rows, :], wa_ref[...])
        bm = _dot(yd_ref[rows, :], wb_ref[...])
        merged = ga_ref[rows, :].astype(F32) * a + gb_ref[rows, :].astype(F32) * bm
        out = _dot(merged.astype(BF16), wo_ref[...])
        x1 = x_ref[rows, :] + g1_ref[0] * out
        x1_ref[rows, :] = x1
        ms = jnp.mean(x1 * x1, axis=-1, keepdims=True)
        h = x1 * lax.rsqrt(ms + EPS) * ng_ref[...]
        h = h * (1.0 + sc_ref[0]) + sh_ref[0]
        hb = h.astype(BF16)
        h2_ref[rows, :] = hb
        lg_ref[rows, :] = _dot(hb, wr_ref[...]) + br_ref[...]


def _outproj(x2d, ya2d, yd2d, p2d, g1, sc2, sh2, norm_g, wa, wb, wo, wr, br, S):
    T = x2d.shape[0]
    tm = TM_PROJ
    per_b = S // tm
    row = lambda w: pl.BlockSpec((tm, w), lambda i: (i, 0))
    full = lambda a: pl.BlockSpec(a.shape, lambda i: (0,) * a.ndim)
    mod = pl.BlockSpec((1, 1, D_MODEL), lambda i: (i // per_b, 0, 0))
    return pl.pallas_call(
        _outproj_kernel,
        grid=(T // tm,),
        in_specs=[row(D_MODEL), row(FOX_WIDTH), row(DIL_OUT_WIDTH)]
                 + [pl.BlockSpec((tm, D_MODEL), lambda i: (i, U_GATE_A * UNIT // D_MODEL)),
                    pl.BlockSpec((tm, D_MODEL), lambda i: (i, U_GATE_B * UNIT // D_MODEL)),
                    mod, mod, mod, full(norm_g), full(wa), full(wb), full(wo), full(wr), full(br)],
        out_specs=[row(D_MODEL), row(D_MODEL), row(LANES)],
        out_shape=[jax.ShapeDtypeStruct((T, D_MODEL), F32),
                   jax.ShapeDtypeStruct((T, D_MODEL), BF16),
                   jax.ShapeDtypeStruct((T, LANES), F32)],
        compiler_params=pltpu.CompilerParams(vmem_limit_bytes=VMEM_LIMIT),
        name="out_proj",
    )(x2d, ya2d, yd2d, p2d, p2d, g1, sc2, sh2, norm_g, wa, wb, wo, wr, br)


def _dispatch_kernel(lg_ref, h_ref, tri_ref, xs_ref, cnt_ref, pos_ref):
    tt = lg_ref.shape[0]
    lt = lg_ref[...].T
    row = lambda i: lt[i:i + 1, :]
    neg = -jnp.inf
    g = [row(i) for i in range(N_GROUPS)]
    gmax = functools.reduce(jnp.maximum, g)
    gidx = jnp.full(gmax.shape, N_GROUPS - 1, jnp.int32)
    for i in reversed(range(N_GROUPS - 1)):
        gidx = jnp.where(g[i] == gmax, i, gidx)
    gsum = sum(jnp.exp(gi - gmax) for gi in g)
    el = []
    for j in range(EXPERTS_PER_GROUP):
        v = row(N_GROUPS + EXPERTS_PER_GROUP * (N_GROUPS - 1) + j)
        for gg in reversed(range(N_GROUPS - 1)):
            v = jnp.where(gidx == gg, row(N_GROUPS + EXPERTS_PER_GROUP * gg + j), v)
        el.append(v)

    def top(vals):
        best = functools.reduce(jnp.maximum, vals)
        idx = jnp.full(best.shape, EXPERTS_PER_GROUP - 1, jnp.int32)
        for j in reversed(range(EXPERTS_PER_GROUP - 1)):
            idx = jnp.where(vals[j] == best, j, idx)
        return best, idx

    v1, i1 = top(el)
    v2, i2 = top([jnp.where(i1 == j, neg, el[j]) for j in range(EXPERTS_PER_GROUP)])
    t = jnp.exp(v2 - v1)
    den = (1.0 + t) * gsum
    wts = [1.0 / den, t / den]
    eid = [gidx * EXPERTS_PER_GROUP + i1, gidx * EXPERTS_PER_GROUP + i2]

    esub = lax.broadcasted_iota(jnp.int32, (N_EXPERTS, tt), 0)
    ohf = jnp.concatenate([jnp.where(esub == eid[k], 1.0, 0.0) for k in range(2)], axis=1)
    n_pb = 2 * tt // UNIT
    oh_blocks = jnp.concatenate([ohf[:, b * UNIT:(b + 1) * UNIT] for b in range(n_pb)], axis=0)
    res = _dot(oh_blocks.astype(BF16), tri_ref[...])
    cnt = jnp.zeros((N_EXPERTS, LANES), F32)
    pre = []
    for b in range(n_pb):
        r = res[b * N_EXPERTS:(b + 1) * N_EXPERTS]
        pre.append(r[:, :UNIT] + jnp.concatenate([cnt] * (UNIT // LANES), axis=1))
        cnt = cnt + r[:, UNIT:]
    prefix = jnp.concatenate(pre, axis=1)
    cnt = (((cnt.astype(jnp.int32) + (SUBLANES - 1)) // SUBLANES) * SUBLANES).astype(F32)
    esub_c = lax.broadcasted_iota(jnp.int32, cnt.shape, 0)
    start = jnp.zeros_like(cnt)
    for e in range(N_EXPERTS - 1):
        start = start + jnp.where(esub_c > e, cnt[e:e + 1, :], 0.0)
    start_w = jnp.concatenate([start] * (2 * tt // LANES), axis=1)
    pos = jnp.sum(ohf * (start_w + prefix), axis=0, keepdims=True)
    pos_k = [pos[:, :tt], pos[:, tt:]]

    n_rows = xs_ref.shape[0]
    psub = lax.broadcasted_iota(jnp.int32, (n_rows, tt), 0).astype(F32)
    pm = [jnp.where(psub == pos_k[k], 1.0, 0.0).astype(BF16) for k in range(2)]
    xs = _dot(pm[0] + pm[1], h_ref[...])
    wsub = lax.broadcasted_iota(jnp.int32, (LANES, tt), 0)
    ws = jnp.zeros((n_rows, LANES), F32)
    for k in range(2):
        parts = _split3(wts[k])
        wrows = jnp.zeros((LANES, tt), F32)
        for j in range(3):
            wrows = jnp.where(wsub == j, parts[j].astype(F32), wrows)
        ws = ws + _dot_nt(pm[k], wrows.astype(BF16))
    half = D_MODEL // 2
    xs_ref[:, :half] = _pack_bf16_pair(xs[:, :half], xs[:, half:])
    xs_ref[:, half:] = pltpu.bitcast(ws, jnp.uint32)
    cnt_ref[0] = cnt.astype(jnp.int32)
    posr = jnp.where(wsub == 0, pos_k[0], jnp.where(wsub == 1, pos_k[1], 0.0))
    pos_ref[...] = posr.T


def _dispatch(logits, h2):
    T = logits.shape[0]
    tt = MOE_TILE
    n_tiles = T // tt
    tri = np.concatenate([np.triu(np.ones((UNIT, UNIT)), 1), np.ones((UNIT, LANES))], axis=1)
    return pl.pallas_call(
        _dispatch_kernel,
        grid=(n_tiles,),
        in_specs=[pl.BlockSpec((tt, LANES), lambda i: (i, 0)),
                  pl.BlockSpec((tt, D_MODEL), lambda i: (i, 0)),
                  pl.BlockSpec(tri.shape, lambda i: (0, 0))],
        out_specs=[pl.BlockSpec((TILE_ROWS, XS_WIDTH), lambda i: (i, 0)),
                   pl.BlockSpec((1, N_EXPERTS, LANES), lambda i: (i, 0, 0)),
                   pl.BlockSpec((tt, LANES), lambda i: (i, 0))],
        out_shape=[jax.ShapeDtypeStruct((n_tiles * TILE_ROWS, XS_WIDTH), jnp.uint32),
                   jax.ShapeDtypeStruct((n_tiles, N_EXPERTS, LANES), jnp.int32),
                   jax.ShapeDtypeStruct((T, LANES), F32)],
        compiler_params=pltpu.CompilerParams(vmem_limit_bytes=VMEM_LIMIT),
        name="moe_dispatch",
    )(logits, h2, jnp.asarray(tri, BF16))


def _plan_kernel(cnt_ref, be_ref, nv_ref, nxt_ref, grp_ref, used_ref, cs_ref):
    n_tiles = cnt_ref.shape[0]
    n_blk = be_ref.shape[0]
    rows = MOE_ROWS
    row_shift = rows.bit_length() - 1
    grp_shift = SUBLANES.bit_length() - 1
    assert rows == 1 << row_shift and SUBLANES == 1 << grp_shift

    def tile_starts(t, c):
        def per_e(e, acc):
            cs_ref[t * N_EXPERTS + e] = acc
            return acc + cnt_ref[t, e]
        used_ref[t] = lax.fori_loop(0, N_EXPERTS, per_e, 0, unroll=8)
        return c
    lax.fori_loop(0, n_tiles, tile_starts, 0)

    def clear(b, c):
        nv_ref[b] = 0
        return c
    lax.fori_loop(0, n_blk, clear, 0)

    def clear_groups(g, c):
        grp_ref[g] = 0
        return c
    lax.fori_loop(0, n_blk * MOE_GROUPS, clear_groups, 0, unroll=8)

    def per_expert(e, b):
        g0 = b * MOE_GROUPS

        def per_tile(t, tot):
            c = cnt_ref[t, e]
            src = t * TILE_ROWS + cs_ref[t * N_EXPERTS + e]
            first = g0 + lax.shift_right_logical(tot, grp_shift)

            def per_group(k, cc):
                grp_ref[first + k] = src + k * SUBLANES
                return cc
            lax.fori_loop(0, lax.shift_right_logical(c, grp_shift), per_group, 0)
            return tot + c
        tot = lax.fori_loop(0, n_tiles, per_tile, 0)

        def per_block(j, c):
            be_ref[b + j] = e
            nv_ref[b + j] = jnp.minimum(rows, tot - j * rows)
            return c
        nb = lax.shift_right_logical(tot + rows - 1, row_shift)
        lax.fori_loop(0, nb, per_block, 0)
        return b + nb
    n_used = lax.fori_loop(0, N_EXPERTS, per_expert, 0)

    def unused(b, c):
        be_ref[b] = be_ref[n_used - 1]
        nxt_ref[b] = -1
        return c
    lax.fori_loop(n_used, n_blk, unused, 0)

    def next_run(k, nf):
        b = n_used - 1 - k
        nf = jnp.where(be_ref[b] != be_ref[jnp.minimum(b + 1, n_used - 1)], b + 1, nf)
        nxt_ref[b] = nf
        return nf
    lax.fori_loop(0, n_used, next_run, -1)


def _plan(cnt, n_blk):
    n_tiles = cnt.shape[0]
    smem = pl.BlockSpec(memory_space=pltpu.SMEM)
    i32 = lambda n: jax.ShapeDtypeStruct((n,), jnp.int32)
    return pl.pallas_call(
        _plan_kernel,
        in_specs=[smem],
        out_specs=[smem] * 5,
        out_shape=[i32(n_blk), i32(n_blk), i32(n_blk), i32(n_blk * MOE_GROUPS), i32(n_tiles)],
        scratch_shapes=[pltpu.SMEM((n_tiles * N_EXPERTS,), jnp.int32)],
        name="moe_plan",
    )(cnt)


def _pow2_pieces(n, fn):
    for b in reversed(range(SUBLANES.bit_length() - 1, MOE_ROWS.bit_length())):
        size = 1 << b

        @pl.when((n & size) != 0)
        def _():
            fn((n >> (b + 1)) << (b + 1), size)


def _moe_kernel(be_ref, nv_ref, nxt_ref, grp_ref, used_ref,
                w1_hbm, w3_hbm, w2_hbm, xs_hbm, ys_hbm,
                xbuf, ybuf, wb1, wb3, wb2, wst1, wst3, wst2, wslot, gsem, ssem, wsem):
    i = pl.program_id(0)
    last = pl.num_programs(0) - 1
    slot = i % 2
    nv = nv_ref[i]
    half = D_MODEL // 2
    grp_shift = SUBLANES.bit_length() - 1

    def group_row(blk, g):
        return pl.multiple_of(grp_ref[blk * MOE_GROUPS + g], SUBLANES)

    def gather(blk, s):
        top = jnp.maximum(lax.shift_right_logical(nv_ref[blk], grp_shift) - 1, 0)
        for g in range(MOE_GROUPS):
            src = group_row(blk, jnp.minimum(g, top))
            pltpu.make_async_copy(xs_hbm.at[pl.ds(src, SUBLANES)],
                                  xbuf.at[s, pl.ds(g * SUBLANES, SUBLANES)],
                                  gsem.at[s]).start(priority=g % 2)

    def wait_gather(s):
        pltpu.make_async_copy(xs_hbm.at[pl.ds(0, MOE_ROWS)], xbuf.at[s], gsem.at[s]).wait()

    def scatter(blk, s):
        def start(g):
            static = isinstance(g, int)
            r = g * SUBLANES if static else pl.multiple_of(g * SUBLANES, SUBLANES)
            pltpu.make_async_copy(ybuf.at[s, pl.ds(r, SUBLANES)],
                                  ys_hbm.at[pl.ds(group_row(blk, g), SUBLANES)],
                                  ssem.at[s]).start(priority=g % 2 if static else 0)

        @pl.when(nv_ref[blk] == MOE_ROWS)
        def _():
            for g in range(MOE_GROUPS):
                start(g)

        @pl.when(nv_ref[blk] < MOE_ROWS)
        def _():
            def body(g, c):
                start(g)
                return c
            lax.fori_loop(0, lax.shift_right_logical(nv_ref[blk], grp_shift), body, 0)

    def wait_scatter(s, count):
        @pl.when(count == MOE_ROWS)
        def _():
            pltpu.make_async_copy(ybuf.at[s], ys_hbm.at[pl.ds(0, MOE_ROWS)], ssem.at[s]).wait()

        @pl.when(count < MOE_ROWS)
        def _():
            _pow2_pieces(count, lambda a, size: pltpu.make_async_copy(
                ybuf.at[s, pl.ds(0, size)], ys_hbm.at[pl.ds(0, size)], ssem.at[s]).wait())

    @pl.when(i == 0)
    def _():
        @pl.when(nv > 0)
        def _():
            gather(0, 0)
            gather(jnp.minimum(1, last), 1)

        ybuf[1] = jnp.zeros(ybuf.shape[1:], ybuf.dtype)
        n_tiles = used_ref.shape[0]

        def fill(t, c):
            row0 = t * TILE_ROWS + used_ref[t]
            _pow2_pieces(TILE_ROWS - used_ref[t], lambda a, size: pltpu.make_async_copy(
                ybuf.at[1, pl.ds(0, size)], ys_hbm.at[pl.ds(pl.multiple_of(row0 + a, SUBLANES), size)],
                ssem.at[1]).start())
            return c
        lax.fori_loop(0, n_tiles, fill, 0)

        def drain(t, c):
            wait_scatter(1, TILE_ROWS - used_ref[t])
            return c
        lax.fori_loop(0, n_tiles, drain, 0)

    @pl.when(i >= 2)
    def _():
        wait_scatter(slot, nv_ref[jnp.maximum(i - 2, 0)])

    xslot = i % GATHER_BUFS
    issuer_used = jnp.where(i >= 2, nv_ref[jnp.maximum(i - 2, 0)], nv_ref[0]) > 0

    @pl.when((nv == 0) & (i > 0) & issuer_used)
    def _():
        wait_gather(xslot)

    @pl.when(nv > 0)
    def _():
        e = be_ref[i]
        e_prev = be_ref[jnp.maximum(i - 1, 0)]

        def weight_copies(ex, ws):
            return [pltpu.make_async_copy(src.at[ex], dst.at[ws], wsem.at[ws])
                    for src, dst in ((w1_hbm, wst1), (w3_hbm, wst3), (w2_hbm, wst2))]

        @pl.when(i == 0)
        def _():
            wslot[0] = 0
            for cp in weight_copies(e, 0):
                cp.start()

        @pl.when((i == 0) | (e != e_prev))
        def _():
            ws = wslot[0]
            for cp in weight_copies(e, ws):
                cp.wait()
            wb1[...] = wst1[ws].astype(BF16)
            wb3[...] = wst3[ws].astype(BF16)
            wb2[...] = wst2[ws].astype(BF16)
            nb = nxt_ref[i]

            @pl.when(nb >= 0)
            def _():
                for cp in weight_copies(be_ref[jnp.maximum(nb, 0)], 1 - ws):
                    cp.start()
            wslot[0] = 1 - ws

        wait_gather(xslot)
        gather(jnp.minimum(i + 2, last), (i + 2) % GATHER_BUFS)
        u = xbuf[xslot]
        xa, xb = _unpack_bf16_pair(u[:, :half])
        wv = pltpu.bitcast(u[:, half:], F32)
        roww = wv[:, 0:1] + wv[:, 1:2] + wv[:, 2:3]
        a = _dot(xa, wb1[:half, :]) + _dot(xb, wb1[half:, :])
        b = _dot(xa, wb3[:half, :]) + _dot(xb, wb3[half:, :])
        hmid = (a * jax.nn.sigmoid(a) * b).astype(BF16)
        y = _dot(hmid, wb2[...]) * roww
        ybuf[slot] = _pack_bf16_pair(y[:, :half], y[:, half:])
        scatter(i, slot)

    @pl.when(i == last)
    def _():
        @pl.when((last >= 1) & (nv_ref[jnp.maximum(last - 1, 0)] > 0))
        def _():
            wait_gather((last + 1) % GATHER_BUFS)

        @pl.when(nv > 0)
        def _():
            wait_gather((last + 2) % GATHER_BUFS)

        @pl.when(last >= 1)
        def _():
            wait_scatter(1 - slot, nv_ref[jnp.maximum(last - 1, 0)])
        wait_scatter(slot, nv)


def _moe(xs, plan, w1, w3, w2):
    n_blk = plan[0].shape[0]
    rows = MOE_ROWS
    half = D_MODEL // 2
    hbm = pl.BlockSpec(memory_space=pl.ANY)
    grid_spec = pltpu.PrefetchScalarGridSpec(
        num_scalar_prefetch=5,
        grid=(n_blk,),
        in_specs=[hbm] * 4,
        out_specs=hbm,
        scratch_shapes=[pltpu.VMEM((GATHER_BUFS, rows, XS_WIDTH), jnp.uint32),
                        pltpu.VMEM((2, rows, half), jnp.uint32),
                        pltpu.VMEM((D_MODEL, EXPERT_HIDDEN), BF16),
                        pltpu.VMEM((D_MODEL, EXPERT_HIDDEN), BF16),
                        pltpu.VMEM((EXPERT_HIDDEN, D_MODEL), BF16),
                        pltpu.VMEM((2, D_MODEL, EXPERT_HIDDEN), F32),
                        pltpu.VMEM((2, D_MODEL, EXPERT_HIDDEN), F32),
                        pltpu.VMEM((2, EXPERT_HIDDEN, D_MODEL), F32),
                        pltpu.SMEM((1,), jnp.int32),
                        pltpu.SemaphoreType.DMA((GATHER_BUFS,)),
                        pltpu.SemaphoreType.DMA((2,)),
                        pltpu.SemaphoreType.DMA((2,))])
    return pl.pallas_call(
        _moe_kernel,
        grid_spec=grid_spec,
        out_shape=jax.ShapeDtypeStruct((xs.shape[0], half), jnp.uint32),
        compiler_params=pltpu.CompilerParams(dimension_semantics=("arbitrary",),
                                             vmem_limit_bytes=VMEM_LIMIT),
        name="moe_ffn",
    )(*plan, w1, w3, w2, xs)


def _combine_kernel(x1_ref, ys_ref, pos_ref, g2_ref, o_ref):
    tt = x1_ref.shape[0]
    half = D_MODEL // 2
    pos = pos_ref[...]
    pcol = lax.broadcasted_iota(jnp.int32, (tt, ys_ref.shape[0]), 1).astype(F32)
    sel = (jnp.where(pcol == pos[:, 0:1], 1.0, 0.0) + jnp.where(pcol == pos[:, 1:2], 1.0, 0.0))
    sel = sel.astype(BF16)
    lo, hi = _unpack_bf16_pair(ys_ref[...])
    g2 = g2_ref[0]
    x1 = x1_ref[...]
    o_ref[:, :half] = x1[:, :half] + g2[:, :half] * _dot(sel, lo)
    o_ref[:, half:] = x1[:, half:] + g2[:, half:] * _dot(sel, hi)


def _combine(x1, ys, pos, g2, S):
    T = x1.shape[0]
    tt = MOE_TILE
    per_b = S // tt
    return pl.pallas_call(
        _combine_kernel,
        grid=(T // tt,),
        in_specs=[pl.BlockSpec((tt, D_MODEL), lambda i: (i, 0)),
                  pl.BlockSpec((TILE_ROWS, D_MODEL // 2), lambda i: (i, 0)),
                  pl.BlockSpec((tt, LANES), lambda i: (i, 0)),
                  pl.BlockSpec((1, 1, D_MODEL), lambda i: (i // per_b, 0, 0))],
        out_specs=pl.BlockSpec((tt, D_MODEL), lambda i: (i, 0)),
        out_shape=jax.ShapeDtypeStruct((T, D_MODEL), F32),
        compiler_params=pltpu.CompilerParams(vmem_limit_bytes=VMEM_LIMIT),
        name="moe_combine",
    )(x1, ys, pos, g2)


def _prep_w_in(w_in):
    pad = jnp.zeros((D_MODEL, UNIT - FOX_HEADS), w_in.dtype)
    groups = [w_in[:, OFF_GATE_A:N_IN],
              w_in[:, OFF_FOX_Q:OFF_FOX_F],
              w_in[:, OFF_DIL_Q:OFF_DIL_K],
              w_in[:, OFF_DIL_K:OFF_DIL_V],
              w_in[:, OFF_DIL_V:OFF_GATE_A],
              jnp.concatenate([w_in[:, OFF_FOX_F:OFF_DIL_Q], pad], axis=1)]
    return [w.astype(BF16) for w in groups]


def _prep_gain(q_gain, k_gain):
    qs = HEAD_DIM ** -0.5 * LOG2E
    parts = [q_gain[:FOX_HEADS] * qs, k_gain[:FOX_HEADS], q_gain[FOX_HEADS:] * qs, k_gain[FOX_HEADS:]]
    return jnp.concatenate([p.reshape(-1) for p in parts]).reshape(1, -1)


def _layer(x, mod, rel_bias_table, norm1_g, w_in, b_forget, q_gain, k_gain, w_branch_a, w_branch_b,
           w_out, norm2_g, w_rg, b_rg, w_re, b_re, w1, w3, w2):
    B, S, D = x.shape
    T = B * S
    sh1, sc1, g1, sh2, sc2, g2 = [m.reshape(B, 1, D) for m in jnp.split(mod, 6, axis=-1)]
    x2d = x.reshape(T, D)

    p2d, fgt, slabs = _inproj(x2d, norm1_g.reshape(1, D), sc1, sh1, _prep_w_in(w_in),
                              _prep_gain(q_gain, k_gain), S)
    p3 = p2d.reshape(B, S, P_WIDTH)
    ck = _fcum(fgt.reshape(B, S, LANES), b_forget)
    ya = _fox(p3, ck)
    yd = _dil(slabs, _relbias(rel_bias_table))

    n_router = N_GROUPS + N_EXPERTS
    wr = jnp.concatenate([w_rg, w_re, jnp.zeros((D, LANES - n_router), F32)], axis=1).astype(BF16)
    br = jnp.concatenate([b_rg, b_re, jnp.zeros((LANES - n_router,), F32)]).reshape(1, LANES)
    x1, h2, logits = _outproj(x2d, ya.reshape(T, FOX_WIDTH), yd.reshape(T, DIL_OUT_WIDTH), p2d,
                              g1, sc2, sh2, norm2_g.reshape(1, D),
                              w_branch_a.astype(BF16), w_branch_b.astype(BF16), w_out.astype(BF16),
                              wr, br, S)
    xs, cnt, pos = _dispatch(logits, h2)
    cnt2 = cnt[:, :, 0]
    n_blk = cnt.shape[0] * TILE_ROWS // MOE_ROWS + N_EXPERTS
    plan = _plan(cnt2, n_blk)
    ys = _moe(xs, plan, w1, w3, w2)
    out = _combine(x1, ys, pos, g2, S)
    return out.reshape(B, S, D)


def kernel(x, c, rel_bias_table, w_ada, b_ada, norm1_g, w_in, b_forget, q_gain, k_gain, w_branch_a, w_branch_b, w_out, norm2_g, w_router_group, b_router_group, w_router_expert, b_router_expert, w1, w3, w2):
    depth = w_ada.shape[0]
    for l in range(depth):
        mod = _ada(c, w_ada[l], b_ada[l])
        x = _layer(x, mod, rel_bias_table, norm1_g[l], w_in[l], b_forget[l], q_gain[l], k_gain[l],
                   w_branch_a[l], w_branch_b[l], w_out[l], norm2_g[l], w_router_group[l],
                   b_router_group[l], w_router_expert[l], b_router_expert[l], w1[l], w3[l], w2[l])
    return x
```
